```python
import math
import jax, jax.numpy as jnp
from jax import lax
import numpy as np

D_MODEL = 1024
BATCH = 16
SEQ = 4096
DEPTH = 1

HEAD_DIM = 64
MIX_WIDTH = D_MODEL
A_HEADS = (MIX_WIDTH // 2) // HEAD_DIM
A_KV_HEADS = 2
A_GROUP = A_HEADS // A_KV_HEADS
B_HEADS = (MIX_WIDTH // 2) // HEAD_DIM
WINDOW = 128
BLOCK = 128
D_FF = 4 * D_MODEL
EPS = 1e-6

A_Q_W = A_HEADS * HEAD_DIM
A_KV_W = A_KV_HEADS * HEAD_DIM
B_W = B_HEADS * HEAD_DIM
IN_SPLITS = tuple(np.cumsum([A_Q_W, A_KV_W, A_KV_W, B_W, B_W, B_W]).tolist())
IN_WIDTH = A_Q_W + 2 * A_KV_W + 3 * B_W + B_HEADS

kernel_name = "hybrid_swa_sinks_fox_sqrelu"


def rmsnorm(x, g):
    x32 = x.astype(jnp.float32)
    y = x32 * lax.rsqrt(jnp.mean(x32 * x32, axis=-1, keepdims=True) + EPS)
    return (y * g.astype(jnp.float32)).astype(x.dtype)


def alibi_slopes(n):
    return jnp.exp2(-(8.0 / n) * (jnp.arange(n, dtype=jnp.float32) + 1.0))


def swa_sinks_attention(q, k, v, sinks):
    b, s, _, d = q.shape
    nb = s // BLOCK
    scale = 1.0 / math.sqrt(d)
    qb = q.reshape(b, nb, BLOCK, A_KV_HEADS, A_GROUP, d)
    pad = ((0, 0), (BLOCK, 0), (0, 0), (0, 0))
    kp = jnp.pad(k, pad).reshape(b, nb + 1, BLOCK, A_KV_HEADS, d)
    vp = jnp.pad(v, pad).reshape(b, nb + 1, BLOCK, A_KV_HEADS, d)
    kb = jnp.concatenate([kp[:, :-1], kp[:, 1:]], axis=2)
    vb = jnp.concatenate([vp[:, :-1], vp[:, 1:]], axis=2)
    scores = jnp.einsum('bnqkgd,bnskd->bnkgqs', qb, kb).astype(jnp.float32) * scale
    qpos = BLOCK + jnp.arange(BLOCK)
    kpos = jnp.arange(2 * BLOCK)
    dist = qpos[:, None] - kpos[None, :]
    band = (dist >= 0) & (dist < WINDOW)
    first_pad = (jnp.arange(nb) == 0)[:, None, None] & (kpos < BLOCK)[None, None, :]
    valid = band[None] & ~first_pad
    slopes = alibi_slopes(A_HEADS).reshape(A_KV_HEADS, A_GROUP)
    alibi = -slopes[:, :, None, None] * dist.astype(jnp.float32)[None, None]
    scores = scores + alibi[None, None]
    scores = jnp.where(valid[None, :, None, None], scores, -jnp.inf)
    sink = sinks.astype(jnp.float32).reshape(1, 1, A_KV_HEADS, A_GROUP, 1, 1)
    m = jnp.maximum(jnp.max(scores, axis=-1, keepdims=True), sink)
    p = jnp.exp(scores - m)
    denom = jnp.sum(p, axis=-1, keepdims=True) + jnp.exp(sink - m)
    p = (p / denom).astype(v.dtype)
    out = jnp.einsum('bnkgqs,bnskd->bnqkgd', p, vb)
    return out.reshape(b, s, A_HEADS, d)


def forgetting_attention(q, k, v, log_f):
    b, s, h, d = q.shape
    nb = s // BLOCK
    scale = 1.0 / math.sqrt(d)
    c = jnp.cumsum(log_f, axis=1)
    c_keys = jnp.transpose(c, (0, 2, 1))
    qb = jnp.moveaxis(q.reshape(b, nb, BLOCK, h, d), 1, 0)
    cb = jnp.moveaxis(c.reshape(b, nb, BLOCK, h), 1, 0)
    kpos = jnp.arange(s)

    def block_step(args):
        qi, ci, i = args
        sc = jnp.einsum('bqhd,bshd->bhqs', qi, k).astype(jnp.float32) * scale
        bias = jnp.transpose(ci, (0, 2, 1))[..., None] - c_keys[:, :, None, :]
        qpos = i * BLOCK + jnp.arange(BLOCK)
        causal = kpos[None, :] <= qpos[:, None]
        sc = jnp.where(causal[None, None], sc + bias, -jnp.inf)
        p = jax.nn.softmax(sc, axis=-1).astype(v.dtype)
        return jnp.einsum('bhqs,bshd->bqhd', p, v)

    out = lax.map(block_step, (qb, cb, jnp.arange(nb)))
    return jnp.moveaxis(out, 0, 1).reshape(b, s, h, d)


def _fwd_setup_inputs(seed: int = 0) -> dict:
    key = jax.random.key(seed)
    ks = jax.random.split(key, 14)
    f32 = jnp.float32
    x = jax.random.normal(ks[0], (BATCH, SEQ, D_MODEL), f32)
    attn_norm_g = 1.0 + 0.02 * jax.random.normal(ks[1], (D_MODEL,), f32)
    w_in = jax.random.normal(ks[2], (D_MODEL, IN_WIDTH), f32) * D_MODEL ** -0.5
    b_forget = 2.0 + 0.5 * jax.random.normal(ks[3], (B_HEADS,), f32)
    q_norm_a = 1.0 + 0.02 * jax.random.normal(ks[4], (HEAD_DIM,), f32)
    k_norm_a = 1.0 + 0.02 * jax.random.normal(ks[5], (HEAD_DIM,), f32)
    sink_logits = 0.5 * jax.random.normal(ks[6], (A_HEADS,), f32)
    q_norm_b = 1.0 + 0.02 * jax.random.normal(ks[7], (HEAD_DIM,), f32)
    k_norm_b = 1.0 + 0.02 * jax.random.normal(ks[8], (HEAD_DIM,), f32)
    w_out = jax.random.normal(ks[9], (MIX_WIDTH, D_MODEL), f32) * MIX_WIDTH ** -0.5
    mlp_norm_g = 1.0 + 0.02 * jax.random.normal(ks[10], (D_MODEL,), f32)
    w_up = jax.random.normal(ks[11], (D_MODEL, D_FF), f32) * D_MODEL ** -0.5
    w_down = jax.random.normal(ks[12], (D_FF, D_MODEL), f32) * D_FF ** -0.5
    return {"x": x, "attn_norm_g": attn_norm_g, "w_in": w_in, "b_forget": b_forget,
            "q_norm_a": q_norm_a, "k_norm_a": k_norm_a, "sink_logits": sink_logits,
            "q_norm_b": q_norm_b, "k_norm_b": k_norm_b, "w_out": w_out,
            "mlp_norm_g": mlp_norm_g, "w_up": w_up, "w_down": w_down}


def _fwd_reference(x, attn_norm_g, w_in, b_forget, q_norm_a, k_norm_a, sink_logits,
              q_norm_b, k_norm_b, w_out, mlp_norm_g, w_up, w_down):
    b, s, _ = x.shape
    for _layer in range(DEPTH):
        xn = rmsnorm(x, attn_norm_g)
        proj = jnp.einsum('bsd,de->bse', xn, w_in)
        qa, ka, va, qb, kb, vb, f_logit = jnp.split(proj, IN_SPLITS, axis=-1)
        qa = rmsnorm(qa.reshape(b, s, A_HEADS, HEAD_DIM), q_norm_a)
        ka = rmsnorm(ka.reshape(b, s, A_KV_HEADS, HEAD_DIM), k_norm_a)
        va = va.reshape(b, s, A_KV_HEADS, HEAD_DIM)
        out_a = swa_sinks_attention(qa, ka, va, sink_logits)
        qb = rmsnorm(qb.reshape(b, s, B_HEADS, HEAD_DIM), q_norm_b)
        kb = rmsnorm(kb.reshape(b, s, B_HEADS, HEAD_DIM), k_norm_b)
        vb = vb.reshape(b, s, B_HEADS, HEAD_DIM)
        log_f = jax.nn.log_sigmoid(f_logit.astype(jnp.float32) + b_forget.astype(jnp.float32))
        out_b = forgetting_attention(qb, kb, vb, log_f)
        mixed = jnp.concatenate([out_a.reshape(b, s, A_Q_W), out_b.reshape(b, s, B_W)], axis=-1)
        x = x + jnp.einsum('bse,ed->bsd', mixed, w_out)
        hn = rmsnorm(x, mlp_norm_g)
        hid = jnp.square(jax.nn.relu(jnp.einsum('bsd,df->bsf', hn, w_up)))
        x = x + jnp.einsum('bsf,fd->bsd', hid, w_down)
    return x


import jax as _jax
import jax.numpy as _jnp

TWIN_FORMAT = 'train_step'
FWD_PARAMS = ['x', 'attn_norm_g', 'w_in', 'b_forget', 'q_norm_a', 'k_norm_a', 'sink_logits', 'q_norm_b', 'k_norm_b', 'w_out', 'mlp_norm_g', 'w_up', 'w_down']
TWIN_WEIGHTS = ['attn_norm_g', 'w_in', 'b_forget', 'q_norm_a', 'k_norm_a', 'sink_logits', 'q_norm_b', 'k_norm_b', 'w_out', 'mlp_norm_g', 'w_up', 'w_down']
TWIN_DIFF_INPUT = 'x'
TWIN_INPUTS = ['x', 'attn_norm_g', 'w_in', 'b_forget', 'q_norm_a', 'k_norm_a', 'sink_logits', 'q_norm_b', 'k_norm_b', 'w_out', 'mlp_norm_g', 'w_up', 'w_down', 'loss_target', 'm_attn_norm_g', 'm_w_in', 'm_b_forget', 'm_q_norm_a', 'm_k_norm_a', 'm_sink_logits', 'm_q_norm_b', 'm_k_norm_b', 'm_w_out', 'm_mlp_norm_g', 'm_w_up', 'm_w_down', 'v_attn_norm_g', 'v_w_in', 'v_b_forget', 'v_q_norm_a', 'v_k_norm_a', 'v_sink_logits', 'v_q_norm_b', 'v_k_norm_b', 'v_w_out', 'v_mlp_norm_g', 'v_w_up', 'v_w_down']
TWIN_OUTPUTS = ['loss', 'grad_x', 'grad_attn_norm_g', 'grad_w_in', 'grad_b_forget', 'grad_q_norm_a', 'grad_k_norm_a', 'grad_sink_logits', 'grad_q_norm_b', 'grad_k_norm_b', 'grad_w_out', 'grad_mlp_norm_g', 'grad_w_up', 'grad_w_down', 'delta_attn_norm_g', 'delta_w_in', 'delta_b_forget', 'delta_q_norm_a', 'delta_k_norm_a', 'delta_sink_logits', 'delta_q_norm_b', 'delta_k_norm_b', 'delta_w_out', 'delta_mlp_norm_g', 'delta_w_up', 'delta_w_down', 'new_m_attn_norm_g', 'new_m_w_in', 'new_m_b_forget', 'new_m_q_norm_a', 'new_m_k_norm_a', 'new_m_sink_logits', 'new_m_q_norm_b', 'new_m_k_norm_b', 'new_m_w_out', 'new_m_mlp_norm_g', 'new_m_w_up', 'new_m_w_down', 'new_v_attn_norm_g', 'new_v_w_in', 'new_v_b_forget', 'new_v_q_norm_a', 'new_v_k_norm_a', 'new_v_sink_logits', 'new_v_q_norm_b', 'new_v_k_norm_b', 'new_v_w_out', 'new_v_mlp_norm_g', 'new_v_w_up', 'new_v_w_down']
TWIN_LEAF_KINDS = {'loss': 'loss', 'grad_x': 'grad_x', 'grad_attn_norm_g': 'grad_w', 'grad_w_in': 'grad_w', 'grad_b_forget': 'grad_w', 'grad_q_norm_a': 'grad_w', 'grad_k_norm_a': 'grad_w', 'grad_sink_logits': 'grad_w', 'grad_q_norm_b': 'grad_w', 'grad_k_norm_b': 'grad_w', 'grad_w_out': 'grad_w', 'grad_mlp_norm_g': 'grad_w', 'grad_w_up': 'grad_w', 'grad_w_down': 'grad_w', 'delta_attn_norm_g': 'delta_w', 'delta_w_in': 'delta_w', 'delta_b_forget': 'delta_w', 'delta_q_norm_a': 'delta_w', 'delta_k_norm_a': 'delta_w', 'delta_sink_logits': 'delta_w', 'delta_q_norm_b': 'delta_w', 'delta_k_norm_b': 'delta_w', 'delta_w_out': 'delta_w', 'delta_mlp_norm_g': 'delta_w', 'delta_w_up': 'delta_w', 'delta_w_down': 'delta_w', 'new_m_attn_norm_g': 'new_m', 'new_m_w_in': 'new_m', 'new_m_b_forget': 'new_m', 'new_m_q_norm_a': 'new_m', 'new_m_k_norm_a': 'new_m', 'new_m_sink_logits': 'new_m', 'new_m_q_norm_b': 'new_m', 'new_m_k_norm_b': 'new_m', 'new_m_w_out': 'new_m', 'new_m_mlp_norm_g': 'new_m', 'new_m_w_up': 'new_m', 'new_m_w_down': 'new_m', 'new_v_attn_norm_g': 'new_v', 'new_v_w_in': 'new_v', 'new_v_b_forget': 'new_v', 'new_v_q_norm_a': 'new_v', 'new_v_k_norm_a': 'new_v', 'new_v_sink_logits': 'new_v', 'new_v_q_norm_b': 'new_v', 'new_v_k_norm_b': 'new_v', 'new_v_w_out': 'new_v', 'new_v_mlp_norm_g': 'new_v', 'new_v_w_up': 'new_v', 'new_v_w_down': 'new_v'}


def _forward(args):
    return _fwd_reference(*[args[k] for k in FWD_PARAMS])


def _output_shape():
    out = _jax.eval_shape(lambda: _forward(_fwd_setup_inputs(0)))
    return out.shape, out.dtype

N_MICROBATCH = 1
ADAM_LR = 0.001
ADAM_B1 = 0.9
ADAM_B2 = 0.999
ADAM_EPS = 1e-08
ADAM_WD = 0.01
ADAM_STEP = 10
PER_EXAMPLE_BATCH_AXIS = {'x': 0, 'loss_target': 0}
SHARED_INPUTS = []
_WEIGHT_DTYPES = {'attn_norm_g': _jnp.float32, 'w_in': _jnp.float32, 'b_forget': _jnp.float32, 'q_norm_a': _jnp.float32, 'k_norm_a': _jnp.float32, 'sink_logits': _jnp.float32, 'q_norm_b': _jnp.float32, 'k_norm_b': _jnp.float32, 'w_out': _jnp.float32, 'mlp_norm_g': _jnp.float32, 'w_up': _jnp.float32, 'w_down': _jnp.float32}
MOMENT_SCALE = {'attn_norm_g': 7.321469e+00, 'w_in': 4.751112e-01, 'b_forget': 2.141516e+02, 'q_norm_a': 1.923292e+01, 'k_norm_a': 1.911539e+01, 'sink_logits': 6.834999e+01, 'q_norm_b': 2.836298e+01, 'k_norm_b': 2.827914e+01, 'w_out': 5.769792e-01, 'mlp_norm_g': 1.921564e+02, 'w_up': 1.119556e+00, 'w_down': 1.570808e+01}


def _to_microbatches(a, axis):
    t = _jnp.moveaxis(a, axis, 0)
    t = t.reshape((N_MICROBATCH, t.shape[0] // N_MICROBATCH) + t.shape[1:])
    return _jnp.moveaxis(t, 1, axis + 1)


def setup_inputs(seed: int = 0) -> dict:
    inp = _fwd_setup_inputs(seed)
    key = _jax.random.fold_in(_jax.random.key(seed), 7919)
    shape, _ = _output_shape()
    out = dict(inp)
    out["loss_target"] = _jax.random.normal(_jax.random.fold_in(key, 0), shape, _jnp.float32)
    for i, name in enumerate(TWIN_WEIGHTS):
        w = inp[name].astype(_jnp.float32)
        if MOMENT_SCALE is None:
            s = _jnp.sqrt(_jnp.mean(_jnp.square(w)) + 1e-30)
        else:
            s = MOMENT_SCALE[name]
        km, kv = _jax.random.split(_jax.random.fold_in(key, i + 1))
        out[name] = w
        out["m_" + name] = s * _jax.random.normal(km, w.shape, _jnp.float32)
        out["v_" + name] = (s * s) * _jax.random.uniform(kv, w.shape, _jnp.float32, 0.5, 1.5)
    if N_MICROBATCH > 1:
        for name, axis in PER_EXAMPLE_BATCH_AXIS.items():
            out[name] = _to_microbatches(out[name], axis)
    return {'x': out['x'], 'attn_norm_g': out['attn_norm_g'], 'w_in': out['w_in'], 'b_forget': out['b_forget'], 'q_norm_a': out['q_norm_a'], 'k_norm_a': out['k_norm_a'], 'sink_logits': out['sink_logits'], 'q_norm_b': out['q_norm_b'], 'k_norm_b': out['k_norm_b'], 'w_out': out['w_out'], 'mlp_norm_g': out['mlp_norm_g'], 'w_up': out['w_up'], 'w_down': out['w_down'], 'loss_target': out['loss_target'], 'm_attn_norm_g': out['m_attn_norm_g'], 'm_w_in': out['m_w_in'], 'm_b_forget': out['m_b_forget'], 'm_q_norm_a': out['m_q_norm_a'], 'm_k_norm_a': out['m_k_norm_a'], 'm_sink_logits': out['m_sink_logits'], 'm_q_norm_b': out['m_q_norm_b'], 'm_k_norm_b': out['m_k_norm_b'], 'm_w_out': out['m_w_out'], 'm_mlp_norm_g': out['m_mlp_norm_g'], 'm_w_up': out['m_w_up'], 'm_w_down': out['m_w_down'], 'v_attn_norm_g': out['v_attn_norm_g'], 'v_w_in': out['v_w_in'], 'v_b_forget': out['v_b_forget'], 'v_q_norm_a': out['v_q_norm_a'], 'v_k_norm_a': out['v_k_norm_a'], 'v_sink_logits': out['v_sink_logits'], 'v_q_norm_b': out['v_q_norm_b'], 'v_k_norm_b': out['v_k_norm_b'], 'v_w_out': out['v_w_out'], 'v_mlp_norm_g': out['v_mlp_norm_g'], 'v_w_up': out['v_w_up'], 'v_w_down': out['v_w_down']}


def _loss(weights, diff, rest, loss_target):
    with _jax.named_scope("forward"):
        args = {**rest, TWIN_DIFF_INPUT: diff, **{k: w.astype(_WEIGHT_DTYPES[k]) for k, w in weights.items()}}
        y = _forward(args)
    with _jax.named_scope("loss_head"):
        err = _jnp.square(y.astype(_jnp.float32) - loss_target)
        return 0.5 * _jnp.sum(_jnp.mean(err, axis=-1)) if err.ndim else 0.5 * err


def _adamw(w, g, m, v):
    m = ADAM_B1 * m + (1.0 - ADAM_B1) * g
    v = ADAM_B2 * v + (1.0 - ADAM_B2) * _jnp.square(g)
    m_hat = m / (1.0 - ADAM_B1 ** ADAM_STEP)
    v_hat = v / (1.0 - ADAM_B2 ** ADAM_STEP)
    delta = -ADAM_LR * (m_hat / (_jnp.sqrt(v_hat) + ADAM_EPS) + ADAM_WD * w)
    return delta, m, v


def reference(x, attn_norm_g, w_in, b_forget, q_norm_a, k_norm_a, sink_logits, q_norm_b, k_norm_b, w_out, mlp_norm_g, w_up, w_down, loss_target, m_attn_norm_g, m_w_in, m_b_forget, m_q_norm_a, m_k_norm_a, m_sink_logits, m_q_norm_b, m_k_norm_b, m_w_out, m_mlp_norm_g, m_w_up, m_w_down, v_attn_norm_g, v_w_in, v_b_forget, v_q_norm_a, v_k_norm_a, v_sink_logits, v_q_norm_b, v_k_norm_b, v_w_out, v_mlp_norm_g, v_w_up, v_w_down):
    given = dict(x=x, attn_norm_g=attn_norm_g, w_in=w_in, b_forget=b_forget, q_norm_a=q_norm_a, k_norm_a=k_norm_a, sink_logits=sink_logits, q_norm_b=q_norm_b, k_norm_b=k_norm_b, w_out=w_out, mlp_norm_g=mlp_norm_g, w_up=w_up, w_down=w_down, loss_target=loss_target, m_attn_norm_g=m_attn_norm_g, m_w_in=m_w_in, m_b_forget=m_b_forget, m_q_norm_a=m_q_norm_a, m_k_norm_a=m_k_norm_a, m_sink_logits=m_sink_logits, m_q_norm_b=m_q_norm_b, m_k_norm_b=m_k_norm_b, m_w_out=m_w_out, m_mlp_norm_g=m_mlp_norm_g, m_w_up=m_w_up, m_w_down=m_w_down, v_attn_norm_g=v_attn_norm_g, v_w_in=v_w_in, v_b_forget=v_b_forget, v_q_norm_a=v_q_norm_a, v_k_norm_a=v_k_norm_a, v_sink_logits=v_sink_logits, v_q_norm_b=v_q_norm_b, v_k_norm_b=v_k_norm_b, v_w_out=v_w_out, v_mlp_norm_g=v_mlp_norm_g, v_w_up=v_w_up, v_w_down=v_w_down)
    weights = {n: given[n] for n in TWIN_WEIGHTS}
    shared = {n: given[n] for n in SHARED_INPUTS}
    per_example = {n: given[n] for n in ['x']}
    grad_fn = _jax.value_and_grad(_loss, argnums=(0, 1))

    def one_microbatch(ex, loss_target):
        ex = dict(ex)
        diff = ex.pop(TWIN_DIFF_INPUT)
        return grad_fn(weights, diff, {**shared, **ex}, loss_target)

    if N_MICROBATCH == 1:
        loss, (grad_w, grad_x) = one_microbatch(per_example, given["loss_target"])
    else:
        def body(carry, xs):
            loss_sum, grad_sum = carry
            l_k, (gw_k, gx_k) = one_microbatch(xs[0], xs[1])
            with _jax.named_scope("update"):
                return (loss_sum + l_k, _jax.tree.map(_jnp.add, grad_sum, gw_k)), gx_k

        init = (_jnp.zeros((), _jnp.float32), _jax.tree.map(_jnp.zeros_like, weights))
        (loss, grad_w), grad_x = _jax.lax.scan(body, init, (per_example, given["loss_target"]))
    with _jax.named_scope("update"):
        delta_w, new_m, new_v = {}, {}, {}
        for n in TWIN_WEIGHTS:
            delta_w[n], new_m[n], new_v[n] = _adamw(weights[n], grad_w[n], given["m_" + n], given["v_" + n])
    return (loss, grad_x, *[grad_w[n] for n in TWIN_WEIGHTS], *[delta_w[n] for n in TWIN_WEIGHTS],
            *[new_m[n] for n in TWIN_WEIGHTS], *[new_v[n] for n in TWIN_WEIGHTS])
```

```python
import functools
import math

import jax
import jax.numpy as jnp
from jax import lax
from jax.experimental import pallas as pl
from jax.experimental.pallas import tpu as pltpu

F32 = jnp.float32
BF16 = jnp.bfloat16

D_MODEL = 1024
HEAD_DIM = 64
N_DEV = 8
D_FF = 4096
A_QW = 512
A_KVW = 128
B_W = 512
MAIN_W = 2304
IN_W = 2312
WINDOW = 128
EPS = 1e-6
SCALE = 0.125
LANES = 128
NEG_INF = float("-inf")

ADAM_LR = 0.001
ADAM_B1 = 0.9
ADAM_B2 = 0.999
ADAM_EPS = 1e-08
ADAM_WD = 0.01
ADAM_STEP = 10

R_OUT, R_UP, R_DOWN, R_IN = 0, 128, 640, 1152
IN_SHARD = 289
R_SMALL = 1456
R_PACK = 1472
VMEM_LIMIT = 56 * 1024 * 1024


def _params(sem, vmem=VMEM_LIMIT):
    return pltpu.CompilerParams(dimension_semantics=sem, vmem_limit_bytes=vmem)


def _const_spec(shape):
    nd = len(shape)
    return pl.BlockSpec(shape, lambda *_: (0,) * nd, pipeline_mode=pl.Buffered(1))


def _lane(shape):
    return lax.broadcasted_iota(jnp.int32, shape, len(shape) - 1)


def _split_dot(v, mat):
    hi = v.astype(BF16)
    lo = (v - hi.astype(F32)).astype(BF16)
    return (jnp.dot(hi, mat, preferred_element_type=F32) + jnp.dot(lo, mat, preferred_element_type=F32))


def _head_ones(n):
    r = lax.shift_right_logical(lax.broadcasted_iota(jnp.int32, (n, n), 0), 6)
    c = lax.shift_right_logical(lax.broadcasted_iota(jnp.int32, (n, n), 1), 6)
    return (r == c).astype(BF16)


def _head_sum(v):
    w = v.shape[1]
    if w <= 256:
        return _split_dot(v, _head_ones(w))
    ones = _head_ones(256)
    return jnp.concatenate([_split_dot(v[:, s:s + 256], ones) for s in range(0, w, 256)], axis=1)


def _head_norm(seg, gain):
    rs = lax.rsqrt(_head_sum(seg * seg) * (1.0 / HEAD_DIM) + EPS)
    return seg * rs * gain


def _head_norm_bwd(seg, gain, d_out):
    rs = lax.rsqrt(_head_sum(seg * seg) * (1.0 / HEAD_DIM) + EPS)
    hat = seg * rs
    gd = d_out * gain
    d_seg = rs * (gd - hat * (_head_sum(gd * hat) * (1.0 / HEAD_DIM)))
    return d_seg, d_out * hat


def _expand_kv(v):
    r = pltpu.roll(v, 64, axis=1)
    lo = _lane(v.shape) < 64
    return jnp.concatenate([jnp.where(lo, v, r), jnp.where(lo, r, v)], axis=1)


def _fold_kv(e4):
    t0 = e4[:, 0:128] + e4[:, 128:256]
    t1 = e4[:, 256:384] + e4[:, 384:512]
    t0 = t0 + pltpu.roll(t0, 64, axis=1)
    t1 = t1 + pltpu.roll(t1, 64, axis=1)
    return jnp.where(_lane(t0.shape) < 64, t0, t1)


def _pick_lane(blk, idx):
    return jnp.sum(jnp.where(_lane(blk.shape) == idx, blk, 0.0), axis=1, keepdims=True)


def _nt(a, b):
    return lax.dot_general(a, b, (((1,), (1,)), ((), ())), preferred_element_type=F32)


def _tn(a, b):
    return lax.dot_general(a, b, (((0,), (0,)), ((), ())), preferred_element_type=F32)


def _norm_proj(x2, g1, w_main, w_f, gqa, gka, gqb, gkb, tm):
    t = x2.shape[0]

    def body(x_ref, g1_ref, wm_ref, wf_ref, gqa_ref, gka_ref, gqb_ref, gkb_ref,
             xn_ref, raw_ref, fl_ref, qa_ref, kae_ref, vae_ref, qb_ref, kb_ref, vb_ref):
        x = x_ref[...]
        r = lax.rsqrt(jnp.mean(x * x, axis=-1, keepdims=True) + EPS)
        xn = (x * r * g1_ref[...]).astype(BF16)
        xn_ref[...] = xn
        proj = jnp.dot(xn, wm_ref[...], preferred_element_type=F32)
        raw_ref[...] = proj
        fl_ref[...] = jnp.dot(xn, wf_ref[...], preferred_element_type=F32)
        qa_ref[...] = _head_norm(proj[:, 0:512], gqa_ref[...]).astype(BF16)
        kae_ref[...] = _expand_kv(_head_norm(proj[:, 512:640], gka_ref[...])).astype(BF16)
        vae_ref[...] = _expand_kv(proj[:, 640:768]).astype(BF16)
        qb_ref[...] = _head_norm(proj[:, 768:1280], gqb_ref[...]).astype(BF16)
        kb_ref[...] = _head_norm(proj[:, 1280:1792], gkb_ref[...]).astype(BF16)
        vb_ref[...] = proj[:, 1792:2304].astype(BF16)

    def tile(w):
        return pl.BlockSpec((tm, w), lambda i: (i, 0))

    return pl.pallas_call(
        body, name="norm_proj", grid=(t // tm,),
        in_specs=[tile(D_MODEL), _const_spec((1, D_MODEL)), _const_spec((D_MODEL, MAIN_W)), _const_spec((D_MODEL, LANES)),
                  _const_spec((1, 512)), _const_spec((1, 128)), _const_spec((1, 512)), _const_spec((1, 512))],
        out_specs=[tile(D_MODEL), tile(MAIN_W), tile(LANES), tile(512), tile(256), tile(256), tile(512), tile(512), tile(512)],
        out_shape=[jax.ShapeDtypeStruct((t, D_MODEL), BF16), jax.ShapeDtypeStruct((t, MAIN_W), F32),
                   jax.ShapeDtypeStruct((t, LANES), F32), jax.ShapeDtypeStruct((t, 512), BF16),
                   jax.ShapeDtypeStruct((t, 256), BF16), jax.ShapeDtypeStruct((t, 256), BF16),
                   jax.ShapeDtypeStruct((t, 512), BF16), jax.ShapeDtypeStruct((t, 512), BF16),
                   jax.ShapeDtypeStruct((t, 512), BF16)],
        compiler_params=_params(("arbitrary",)),
    )(x2, g1, w_main, w_f, gqa, gka, gqb, gkb)


def _tri(n, upper):
    r = lax.broadcasted_iota(jnp.int32, (n, n), 0)
    c = lax.broadcasted_iota(jnp.int32, (n, n), 1)
    return ((c >= r) if upper else (c <= r)).astype(F32)


def _gate_cumsum(fl, bf_row, nb, s, ts):
    t = fl.shape[0]
    nt = s // ts

    def body(fl_ref, b_ref, c_ref, carry):
        @pl.when(pl.program_id(1) == 0)
        def _():
            carry[...] = jnp.zeros_like(carry)

        z = fl_ref[...] + b_ref[...]
        e = jnp.exp(-jnp.abs(z))
        u = 1.0 + e
        log1p = jnp.where(u == 1.0, e, jnp.log(u) * (e / (u - 1.0)))
        lf = jnp.minimum(z, 0.0) - log1p
        c_ref[...] = jnp.dot(_tri(ts, False), lf, precision=lax.Precision.HIGHEST, preferred_element_type=F32) + carry[...]
        carry[...] = c_ref[pl.ds(ts - 1, 1), :]

    return pl.pallas_call(
        body, name="gate_cumsum", grid=(nb, nt),
        in_specs=[pl.BlockSpec((ts, LANES), lambda b, i: (b * nt + i, 0)), _const_spec((1, LANES))],
        out_specs=pl.BlockSpec((ts, LANES), lambda b, i: (b * nt + i, 0)),
        out_shape=jax.ShapeDtypeStruct((t, LANES), F32),
        scratch_shapes=[pltpu.VMEM((1, LANES), F32)],
        compiler_params=_params(("arbitrary", "arbitrary")),
    )(fl, bf_row)


def _gate_cumsum_bwd(dc_k, dc_q, fl, bf_row, nb, s, ts):
    t = fl.shape[0]
    nt = s // ts

    def body(dck_ref, dcq_ref, fl_ref, b_ref, df_ref, gb_ref, carry, dlf_ref):
        @pl.when(pl.program_id(1) == 0)
        def _():
            carry[...] = jnp.zeros_like(carry)

        @pl.when((pl.program_id(0) == 0) & (pl.program_id(1) == 0))
        def _():
            gb_ref[...] = jnp.zeros_like(gb_ref)

        dlf_ref[...] = jnp.dot(_tri(ts, True), dck_ref[...] + dcq_ref[...], precision=lax.Precision.HIGHEST,
                               preferred_element_type=F32) + carry[...]
        carry[...] = dlf_ref[pl.ds(0, 1), :]
        dlf = dlf_ref[...]
        z = fl_ref[...] + b_ref[...]
        df = dlf * (1.0 / (1.0 + jnp.exp(z)))
        df_ref[...] = df
        gb_ref[...] += jnp.sum(df, axis=0, keepdims=True)

    def rev(b, i):
        return (b * nt + (nt - 1 - i), 0)

    return pl.pallas_call(
        body, name="gate_cumsum_bwd", grid=(nb, nt),
        in_specs=[pl.BlockSpec((ts, LANES), rev), pl.BlockSpec((ts, LANES), rev), pl.BlockSpec((ts, LANES), rev),
                  _const_spec((1, LANES))],
        out_specs=[pl.BlockSpec((ts, LANES), rev), pl.BlockSpec((1, LANES), lambda b, i: (0, 0))],
        out_shape=[jax.ShapeDtypeStruct((t, LANES), F32), jax.ShapeDtypeStruct((1, LANES), F32)],
        scratch_shapes=[pltpu.VMEM((1, LANES), F32), pltpu.VMEM((ts, LANES), F32)],
        compiler_params=_params(("arbitrary", "arbitrary")),
    )(dc_k, dc_q, fl, bf_row)


def _slope(p, hh):
    out = jnp.float32(2.0 ** -(2 * 3 + hh + 1))
    for pp in (2, 1, 0):
        out = jnp.where(p == pp, jnp.float32(2.0 ** -(2 * pp + hh + 1)), out)
    return out


def _swa_fwd(qa, kae, vae, sink_row, nb, s, tq):
    t = qa.shape[0]
    nq = s // tq
    nsub = tq // WINDOW

    def body(q_ref, k_ref, v_ref, sink_ref, o_ref, lse_ref):
        p, i = pl.program_id(1), pl.program_id(2)
        lo = _lane((1, LANES)) < 64
        row = lax.broadcasted_iota(jnp.int32, (WINDOW, WINDOW), 0)
        col = lax.broadcasted_iota(jnp.int32, (WINDOW, WINDOW), 1)
        rel = (row - col).astype(F32)
        for u in range(nsub):
            g0 = i * nsub + u
            prev = pl.multiple_of(jnp.maximum(g0 - 1, 0) * WINDOW, WINDOW)
            cur = pl.multiple_of(g0 * WINDOW, WINDOW)
            kp, kc = k_ref[pl.ds(prev, WINDOW), :], k_ref[pl.ds(cur, WINDOW), :]
            vp, vc = v_ref[pl.ds(prev, WINDOW), :], v_ref[pl.ds(cur, WINDOW), :]
            qs = (q_ref[u * WINDOW:(u + 1) * WINDOW, :].astype(F32) * SCALE).astype(BF16)
            outs, lses = [], []
            for hh in range(2):
                mask_h = lo if hh == 0 else jnp.logical_not(lo)
                qh = jnp.where(mask_h, qs, jnp.zeros_like(qs))
                slope = _slope(p, hh)
                sink = _pick_lane(sink_ref[...], 2 * p + hh)
                sp = _nt(qh, kp) - slope * (rel + float(WINDOW))
                sc = _nt(qh, kc) - slope * rel
                sp = jnp.where((col > row) & (g0 > 0), sp, NEG_INF)
                sc = jnp.where(col <= row, sc, NEG_INF)
                m = jnp.maximum(jnp.maximum(jnp.max(sp, axis=1, keepdims=True), jnp.max(sc, axis=1, keepdims=True)), sink)
                pp, pc = jnp.exp(sp - m), jnp.exp(sc - m)
                den = jnp.sum(pp, axis=1, keepdims=True) + jnp.sum(pc, axis=1, keepdims=True) + jnp.exp(sink - m)
                pp, pc = (pp / den).astype(BF16), (pc / den).astype(BF16)
                outs.append(jnp.dot(pp, vp, preferred_element_type=F32) + jnp.dot(pc, vc, preferred_element_type=F32))
                lses.append(m + jnp.log(den))
            o_ref[u * WINDOW:(u + 1) * WINDOW, :] = jnp.where(lo, outs[0], outs[1]).astype(BF16)
            lse_ref[u * WINDOW:(u + 1) * WINDOW, :] = jnp.where(lo, lses[0], lses[1])

    return pl.pallas_call(
        body, name="swa_fwd", grid=(nb, 4, nq),
        in_specs=[pl.BlockSpec((tq, LANES), lambda b, p, i: (b * nq + i, p)),
                  pl.BlockSpec((s, LANES), lambda b, p, i: (b, lax.shift_right_logical(p, 1))),
                  pl.BlockSpec((s, LANES), lambda b, p, i: (b, lax.shift_right_logical(p, 1))),
                  pl.BlockSpec((1, LANES), lambda b, p, i: (0, 0))],
        out_specs=[pl.BlockSpec((tq, LANES), lambda b, p, i: (b * nq + i, p)),
                   pl.BlockSpec((None, tq, LANES), lambda b, p, i: (p, b * nq + i, 0))],
        out_shape=[jax.ShapeDtypeStruct((t, 512), BF16), jax.ShapeDtypeStruct((4, t, LANES), F32)],
        compiler_params=_params(("arbitrary", "arbitrary", "arbitrary")),
    )(qa, kae, vae, sink_row)


def _swa_bwd(qa, kae, vae, do_a, sink_row, lse_rows, delta_rows, nb, s, tq):
    t = qa.shape[0]
    nq = s // tq
    nsub = tq // WINDOW

    def body(q_ref, do_ref, k_ref, v_ref, sink_ref, lse_ref, dl_ref, dq_ref, dk_ref, dv_ref, ds_ref):
        p, i = pl.program_id(1), pl.program_id(2)

        @pl.when(i == 0)
        def _():
            dk_ref[...] = jnp.zeros_like(dk_ref)
            dv_ref[...] = jnp.zeros_like(dv_ref)
            ds_ref[...] = jnp.zeros_like(ds_ref)

        lo = _lane((1, LANES)) < 64
        row = lax.broadcasted_iota(jnp.int32, (WINDOW, WINDOW), 0)
        col = lax.broadcasted_iota(jnp.int32, (WINDOW, WINDOW), 1)
        rel = (col - row).astype(F32)
        for u in range(nsub):
            g0 = i * nsub + u
            prev = pl.multiple_of(jnp.maximum(g0 - 1, 0) * WINDOW, WINDOW)
            cur = pl.multiple_of(g0 * WINDOW, WINDOW)
            chunks = []
            for start in (prev, cur):
                kk = k_ref[pl.ds(start, WINDOW), :]
                chunks.append((start, (kk.astype(F32) * SCALE).astype(BF16), v_ref[pl.ds(start, WINDOW), :]))
            qsub = q_ref[u * WINDOW:(u + 1) * WINDOW, :]
            dosub = do_ref[u * WINDOW:(u + 1) * WINDOW, :]
            dq_h = []
            for hh in range(2):
                mask_h = lo if hh == 0 else jnp.logical_not(lo)
                qh = jnp.where(mask_h, qsub, jnp.zeros_like(qsub))
                doh = jnp.where(mask_h, dosub, jnp.zeros_like(dosub))
                slope = _slope(p, hh)
                sink = _pick_lane(sink_ref[...], 2 * p + hh)
                lse = lse_ref[pl.ds(hh, 1), pl.ds(cur, WINDOW)]
                dlt = dl_ref[pl.ds(hh, 1), pl.ds(cur, WINDOW)]
                psink = jnp.exp(sink - lse)
                row_h = lax.broadcasted_iota(jnp.int32, (8, LANES), 0)
                ds_ref[...] += jnp.where(row_h == hh, -jnp.sum(psink * dlt, axis=1, keepdims=True), 0.0)
                dq_acc = jnp.zeros((WINDOW, LANES), F32)
                for ci, (start, ks, vv) in enumerate(chunks):
                    if ci == 0:
                        valid = (row > col) & (g0 > 0)
                        dist = rel + float(WINDOW)
                    else:
                        valid = row <= col
                        dist = rel
                    st = _nt(ks, qh) - slope * dist - lse
                    pt = jnp.where(valid, jnp.exp(jnp.where(valid, st, 0.0)), 0.0)
                    dpt = _nt(vv, doh)
                    dst = pt * (dpt - dlt)
                    ptb, dstb = pt.astype(BF16), dst.astype(BF16)
                    dv_ref[pl.ds(start, WINDOW), :] += jnp.dot(ptb, doh, preferred_element_type=F32)
                    dk_ref[pl.ds(start, WINDOW), :] += jnp.dot(dstb, qh, preferred_element_type=F32) * SCALE
                    dq_acc = dq_acc + _tn(dstb, ks)
                dq_h.append(dq_acc)
            dq_ref[u * WINDOW:(u + 1) * WINDOW, :] = jnp.where(lo, dq_h[0], dq_h[1])

    rows = pl.BlockSpec((None, None, 2, s), lambda b, p, i: (b, p, 0, 0))
    return pl.pallas_call(
        body, name="swa_bwd", grid=(nb, 4, nq),
        in_specs=[pl.BlockSpec((tq, LANES), lambda b, p, i: (b * nq + i, p)),
                  pl.BlockSpec((tq, LANES), lambda b, p, i: (b * nq + i, p)),
                  pl.BlockSpec((s, LANES), lambda b, p, i: (b, lax.shift_right_logical(p, 1))),
                  pl.BlockSpec((s, LANES), lambda b, p, i: (b, lax.shift_right_logical(p, 1))),
                  pl.BlockSpec((1, LANES), lambda b, p, i: (0, 0)), rows, rows],
        out_specs=[pl.BlockSpec((tq, LANES), lambda b, p, i: (b * nq + i, p)),
                   pl.BlockSpec((s, LANES), lambda b, p, i: (b, p)),
                   pl.BlockSpec((s, LANES), lambda b, p, i: (b, p)),
                   pl.BlockSpec((None, None, 8, LANES), lambda b, p, i: (b, p, 0, 0))],
        out_shape=[jax.ShapeDtypeStruct((t, 512), F32), jax.ShapeDtypeStruct((t, 512), F32),
                   jax.ShapeDtypeStruct((t, 512), F32), jax.ShapeDtypeStruct((nb, 4, 8, LANES), F32)],
        compiler_params=_params(("arbitrary", "arbitrary", "arbitrary")),
    )(qa, do_a, kae, vae, sink_row, lse_rows, delta_rows)


def _fox_fwd(qb, kb, vb, c_col, c_rows, nb, s, bt):
    t = qb.shape[0]
    nq = s // bt

    def body(q_ref, k_ref, v_ref, cc_ref, cr_ref, o_ref, lse_ref):
        j, i = pl.program_id(1), pl.program_id(2)
        lo = _lane((1, LANES)) < 64
        row = lax.broadcasted_iota(jnp.int32, (bt, bt), 0)
        col = lax.broadcasted_iota(jnp.int32, (bt, bt), 1)
        qs = (q_ref[...].astype(F32) * SCALE).astype(BF16)
        cc = cc_ref[...]
        outs, lses = [], []
        for hh in range(2):
            mask_h = lo if hh == 0 else jnp.logical_not(lo)
            qh = jnp.where(mask_h, qs, jnp.zeros_like(qs))
            cq = _pick_lane(cc, 2 * j + hh)

            def blk(kb_i, carry, diag):
                m, l, acc = carry
                start = pl.multiple_of(kb_i * bt, bt)
                sc = _nt(qh, k_ref[pl.ds(start, bt), :]) + (cq - cr_ref[pl.ds(hh, 1), pl.ds(start, bt)])
                if diag:
                    sc = jnp.where(row >= col, sc, NEG_INF)
                m_new = jnp.maximum(m, jnp.max(sc, axis=1, keepdims=True))
                alpha = jnp.exp(m - m_new)
                pr = jnp.exp(sc - m_new)
                l = alpha * l + jnp.sum(pr, axis=1, keepdims=True)
                acc = alpha * acc + jnp.dot(pr.astype(BF16), v_ref[pl.ds(start, bt), :], preferred_element_type=F32)
                return m_new, l, acc

            init = (jnp.full((bt, 1), NEG_INF, F32), jnp.zeros((bt, 1), F32), jnp.zeros((bt, LANES), F32))
            carry = lax.fori_loop(0, i, lambda kb_i, c: blk(kb_i, c, False), init)
            m, l, acc = blk(i, carry, True)
            outs.append(acc / l)
            lses.append(m + jnp.log(l))
        o_ref[...] = jnp.where(lo, outs[0], outs[1]).astype(BF16)
        lse_ref[...] = jnp.where(lo, lses[0], lses[1])

    rows = pl.BlockSpec((None, None, 2, s), lambda b, j, i: (b, j, 0, 0))
    return pl.pallas_call(
        body, name="fox_fwd", grid=(nb, 4, nq),
        in_specs=[pl.BlockSpec((bt, LANES), lambda b, j, i: (b * nq + i, j)),
                  pl.BlockSpec((s, LANES), lambda b, j, i: (b, j)),
                  pl.BlockSpec((s, LANES), lambda b, j, i: (b, j)),
                  pl.BlockSpec((bt, LANES), lambda b, j, i: (b * nq + i, 0)), rows],
        out_specs=[pl.BlockSpec((bt, LANES), lambda b, j, i: (b * nq + i, j)),
                   pl.BlockSpec((None, bt, LANES), lambda b, j, i: (j, b * nq + i, 0))],
        out_shape=[jax.ShapeDtypeStruct((t, 512), BF16), jax.ShapeDtypeStruct((4, t, LANES), F32)],
        compiler_params=_params(("arbitrary", "arbitrary", "arbitrary")),
    )(qb, kb, vb, c_col, c_rows)


def _fox_bwd(qb, kb, vb, do_b, c_col, c_rows, lse_rows, delta_rows, nb, s, bt):
    t = qb.shape[0]
    nk = s // bt

    def body(q_ref, do_ref, k_ref, v_ref, cc_ref, cr_ref, lse_ref, dl_ref, dq_ref, dk_ref, dv_ref, dc_ref, dcq_ref):
        j, kb_i = pl.program_id(1), pl.program_id(2)

        @pl.when(kb_i == 0)
        def _():
            dq_ref[...] = jnp.zeros_like(dq_ref)
            dcq_ref[...] = jnp.zeros_like(dcq_ref)

        lo = _lane((1, LANES)) < 64
        row = lax.broadcasted_iota(jnp.int32, (bt, bt), 0)
        col = lax.broadcasted_iota(jnp.int32, (bt, bt), 1)
        k2, v2 = k_ref[...], v_ref[...]
        ks = (k2.astype(F32) * SCALE).astype(BF16)
        cc = cc_ref[...]
        dk_acc = jnp.zeros((bt, LANES), F32)
        dv_acc = jnp.zeros((bt, LANES), F32)
        dcs = []
        for hh in range(2):
            mask_h = lo if hh == 0 else jnp.logical_not(lo)
            kh = jnp.where(mask_h, ks, jnp.zeros_like(ks))
            ck = _pick_lane(cc, 2 * j + hh)

            def blk(qi, carry, diag):
                dk_a, dv_a, dc_a = carry
                start = pl.multiple_of(qi * bt, bt)
                qblk, doblk = q_ref[pl.ds(start, bt), :], do_ref[pl.ds(start, bt), :]
                qh = jnp.where(mask_h, qblk, jnp.zeros_like(qblk))
                doh = jnp.where(mask_h, doblk, jnp.zeros_like(doblk))
                a_row = cr_ref[pl.ds(hh, 1), pl.ds(start, bt)] - lse_ref[pl.ds(hh, 1), pl.ds(start, bt)]
                st = _nt(ks, qh) + (a_row - ck)
                if diag:
                    pt = jnp.where(col >= row, jnp.exp(jnp.where(col >= row, st, 0.0)), 0.0)
                else:
                    pt = jnp.exp(st)
                dpt = _nt(v2, doh)
                dst = pt * (dpt - dl_ref[pl.ds(hh, 1), pl.ds(start, bt)])
                ptb, dstb = pt.astype(BF16), dst.astype(BF16)
                dv_a = dv_a + jnp.dot(ptb, doh, preferred_element_type=F32)
                dk_a = dk_a + jnp.dot(dstb, qh, preferred_element_type=F32)
                dc_a = dc_a + jnp.sum(dst, axis=1, keepdims=True)
                dq_ref[pl.ds(start, bt), :] += _tn(dstb, kh)
                dcq_ref[pl.ds(hh, 1), pl.ds(start, bt)] += jnp.sum(dst, axis=0, keepdims=True)
                return dk_a, dv_a, dc_a

            carry = blk(kb_i, (dk_acc, dv_acc, jnp.zeros((bt, 1), F32)), True)
            dk_acc, dv_acc, dc_h = lax.fori_loop(kb_i + 1, nk, lambda qi, c: blk(qi, c, False), carry)
            dcs.append(dc_h)
        dk_ref[...] = dk_acc * SCALE
        dv_ref[...] = dv_acc
        dc_ref[...] = -jnp.where(lo, dcs[0], dcs[1])

    rows = pl.BlockSpec((None, None, 2, s), lambda b, j, kb_i: (b, j, 0, 0))
    return pl.pallas_call(
        body, name="fox_bwd", grid=(nb, 4, nk),
        in_specs=[pl.BlockSpec((s, LANES), lambda b, j, kb_i: (b, j)),
                  pl.BlockSpec((s, LANES), lambda b, j, kb_i: (b, j)),
                  pl.BlockSpec((bt, LANES), lambda b, j, kb_i: (b * nk + kb_i, j)),
                  pl.BlockSpec((bt, LANES), lambda b, j, kb_i: (b * nk + kb_i, j)),
                  pl.BlockSpec((bt, LANES), lambda b, j, kb_i: (b * nk + kb_i, 0)), rows, rows, rows],
        out_specs=[pl.BlockSpec((s, LANES), lambda b, j, kb_i: (b, j)),
                   pl.BlockSpec((bt, LANES), lambda b, j, kb_i: (b * nk + kb_i, j)),
                   pl.BlockSpec((bt, LANES), lambda b, j, kb_i: (b * nk + kb_i, j)),
                   pl.BlockSpec((None, bt, LANES), lambda b, j, kb_i: (j, b * nk + kb_i, 0)), rows],
        out_shape=[jax.ShapeDtypeStruct((t, 512), F32), jax.ShapeDtypeStruct((t, 512), F32),
                   jax.ShapeDtypeStruct((t, 512), F32), jax.ShapeDtypeStruct((4, t, LANES), F32),
                   jax.ShapeDtypeStruct((nb, 4, 2, s), F32)],
        compiler_params=_params(("arbitrary", "arbitrary", "arbitrary")),
    )(qb, do_b, kb, vb, c_col, c_rows, lse_rows, delta_rows)


def _mlp_fwd(x2, ma, mb, tgt, wo_a, wo_b, g2, w_up, w_down, tm):
    t = x2.shape[0]

    def body(x_ref, ma_ref, mb_ref, tg_ref, woa_ref, wob_ref, g2_ref, wu_ref, wd_ref,
             h_ref, hn_ref, hid_ref, dy_ref, loss_ref):
        @pl.when(pl.program_id(0) == 0)
        def _():
            loss_ref[...] = jnp.zeros_like(loss_ref)

        h = (x_ref[...] + jnp.dot(ma_ref[...], woa_ref[...], preferred_element_type=F32)
             + jnp.dot(mb_ref[...], wob_ref[...], preferred_element_type=F32))
        h_ref[...] = h
        r = lax.rsqrt(jnp.mean(h * h, axis=-1, keepdims=True) + EPS)
        hn = (h * r * g2_ref[...]).astype(BF16)
        hn_ref[...] = hn
        u = jnp.maximum(jnp.dot(hn, wu_ref[...], preferred_element_type=F32), 0.0)
        hid = (u * u).astype(BF16)
        hid_ref[...] = hid
        y = h + jnp.dot(hid, wd_ref[...], preferred_element_type=F32)
        err = y - tg_ref[...]
        dy_ref[...] = err * (1.0 / D_MODEL)
        part = 0.5 * jnp.sum(jnp.sum(err * err, axis=1, keepdims=True) * (1.0 / D_MODEL), axis=0, keepdims=True)
        loss_ref[...] += part

    def tile(w):
        return pl.BlockSpec((tm, w), lambda i: (i, 0))

    return pl.pallas_call(
        body, name="mlp_fwd", grid=(t // tm,),
        in_specs=[tile(D_MODEL), tile(512), tile(512), tile(D_MODEL), _const_spec((512, D_MODEL)), _const_spec((512, D_MODEL)),
                  _const_spec((1, D_MODEL)), _const_spec((D_MODEL, D_FF)), _const_spec((D_FF, D_MODEL))],
        out_specs=[tile(D_MODEL), tile(D_MODEL), tile(D_FF), tile(D_MODEL), pl.BlockSpec((8, LANES), lambda i: (0, 0))],
        out_shape=[jax.ShapeDtypeStruct((t, D_MODEL), F32), jax.ShapeDtypeStruct((t, D_MODEL), BF16),
                   jax.ShapeDtypeStruct((t, D_FF), BF16), jax.ShapeDtypeStruct((t, D_MODEL), F32),
                   jax.ShapeDtypeStruct((8, LANES), F32)],
        compiler_params=_params(("arbitrary",)),
    )(x2, ma, mb, tgt, wo_a, wo_b, g2, w_up, w_down)


def _mlp_bwd(dy, hid, h, ma, mb, w_down_t, w_up_t, w_out_t, g2, tm):
    t = dy.shape[0]

    def body(dy_ref, hid_ref, h_ref, ma_ref, mb_ref, wdt_ref, wut_ref, wot_ref, g2_ref,
             du_ref, dh_ref, dhb_ref, dma_ref, dmb_ref, dla_ref, dlb_ref, gg_ref):
        @pl.when(pl.program_id(0) == 0)
        def _():
            gg_ref[...] = jnp.zeros_like(gg_ref)

        dy = dy_ref[...]
        d_hid = jnp.dot(dy.astype(BF16), wdt_ref[...], preferred_element_type=F32)
        du = (d_hid * (2.0 * jnp.sqrt(hid_ref[...].astype(F32)))).astype(BF16)
        du_ref[...] = du
        d_hn = jnp.dot(du, wut_ref[...], preferred_element_type=F32)
        h = h_ref[...]
        r = lax.rsqrt(jnp.mean(h * h, axis=-1, keepdims=True) + EPS)
        hat = h * r
        gd = d_hn * g2_ref[...]
        dh = dy + r * (gd - hat * jnp.mean(gd * hat, axis=-1, keepdims=True))
        gg_ref[...] += jnp.sum(d_hn * hat, axis=0, keepdims=True)
        dh_ref[...] = dh
        dhb = dh.astype(BF16)
        dhb_ref[...] = dhb
        dm = jnp.dot(dhb, wot_ref[...], preferred_element_type=F32).astype(BF16)
        dma, dmb = dm[:, 0:512], dm[:, 512:1024]
        dma_ref[...] = dma
        dmb_ref[...] = dmb
        sel = (lax.shift_right_logical(lax.broadcasted_iota(jnp.int32, (512, LANES), 0), 6)
               == lax.broadcasted_iota(jnp.int32, (512, LANES), 1)).astype(BF16)
        dla_ref[...] = _split_dot(dma.astype(F32) * ma_ref[...].astype(F32), sel)
        dlb_ref[...] = _split_dot(dmb.astype(F32) * mb_ref[...].astype(F32), sel)

    def tile(w):
        return pl.BlockSpec((tm, w), lambda i: (i, 0))

    return pl.pallas_call(
        body, name="mlp_bwd", grid=(t // tm,),
        in_specs=[tile(D_MODEL), tile(D_FF), tile(D_MODEL), tile(512), tile(512), _const_spec((D_MODEL, D_FF)),
                  _const_spec((D_FF, D_MODEL)), _const_spec((D_MODEL, D_MODEL)), _const_spec((1, D_MODEL))],
        out_specs=[tile(D_FF), tile(D_MODEL), tile(D_MODEL), tile(512), tile(512), tile(LANES), tile(LANES),
                   pl.BlockSpec((1, D_MODEL), lambda i: (0, 0))],
        out_shape=[jax.ShapeDtypeStruct((t, D_FF), BF16), jax.ShapeDtypeStruct((t, D_MODEL), F32),
                   jax.ShapeDtypeStruct((t, D_MODEL), BF16), jax.ShapeDtypeStruct((t, 512), BF16),
                   jax.ShapeDtypeStruct((t, 512), BF16), jax.ShapeDtypeStruct((t, LANES), F32),
                   jax.ShapeDtypeStruct((t, LANES), F32), jax.ShapeDtypeStruct((1, D_MODEL), F32)],
        compiler_params=_params(("arbitrary",)),
    )(dy, hid, h, ma, mb, w_down_t, w_up_t, w_out_t, g2)


def _wgrad(a, b, name, bm, tk):
    t, m = a.shape
    n = b.shape[1]
    bm = min(bm, m)

    def body(a_ref, b_ref, o_ref):
        @pl.when(pl.program_id(1) == 0)
        def _():
            o_ref[...] = jnp.zeros_like(o_ref)

        o_ref[...] += _tn(a_ref[...].astype(BF16), b_ref[...].astype(BF16))

    return pl.pallas_call(
        body, name=name, grid=(m // bm, t // tk),
        in_specs=[pl.BlockSpec((tk, bm), lambda i, k: (k, i)), pl.BlockSpec((tk, n), lambda i, k: (k, 0))],
        out_specs=pl.BlockSpec((bm, n), lambda i, k: (i, 0)),
        out_shape=jax.ShapeDtypeStruct((m, n), F32),
        compiler_params=_params(("arbitrary", "arbitrary")),
    )(a, b)


def _proj_bwd(raw, dqa, dkae, dvae, dqb, dkb, dvb, dfl, x2, dh, w_main_t, w_f_t, g1, gqa, gka, gqb, gkb, tm):
    t = x2.shape[0]

    def body(raw_ref, dqa_ref, dkae_ref, dvae_ref, dqb_ref, dkb_ref, dvb_ref, dfl_ref, x_ref, dh_ref,
             wmt_ref, wft_ref, g1_ref, gqa_ref, gka_ref, gqb_ref, gkb_ref,
             dx_ref, dp_ref, dfb_ref, ggqa_ref, ggka_ref, ggqb_ref, ggkb_ref, gg1_ref):
        @pl.when(pl.program_id(0) == 0)
        def _():
            for r in (ggqa_ref, ggka_ref, ggqb_ref, ggkb_ref, gg1_ref):
                r[...] = jnp.zeros_like(r)

        raw = raw_ref[...]
        d_qa, p_qa = _head_norm_bwd(raw[:, 0:512], gqa_ref[...], dqa_ref[...])
        d_ka, p_ka = _head_norm_bwd(raw[:, 512:640], gka_ref[...], _fold_kv(dkae_ref[...]))
        d_va = _fold_kv(dvae_ref[...])
        d_qb, p_qb = _head_norm_bwd(raw[:, 768:1280], gqb_ref[...], dqb_ref[...])
        d_kb, p_kb = _head_norm_bwd(raw[:, 1280:1792], gkb_ref[...], dkb_ref[...])
        ggqa_ref[...] += jnp.sum(p_qa, axis=0, keepdims=True)
        ggka_ref[...] += jnp.sum(p_ka, axis=0, keepdims=True)
        ggqb_ref[...] += jnp.sum(p_qb, axis=0, keepdims=True)
        ggkb_ref[...] += jnp.sum(p_kb, axis=0, keepdims=True)
        dproj = jnp.concatenate([d_qa, d_ka, d_va, d_qb, d_kb, dvb_ref[...]], axis=1).astype(BF16)
        dp_ref[...] = dproj
        dfb = dfl_ref[...].astype(BF16)
        dfb_ref[...] = dfb
        d_xn = (jnp.dot(dproj, wmt_ref[...], preferred_element_type=F32)
                + jnp.dot(dfb, wft_ref[...], preferred_element_type=F32))
        x = x_ref[...]
        r = lax.rsqrt(jnp.mean(x * x, axis=-1, keepdims=True) + EPS)
        hat = x * r
        gd = d_xn * g1_ref[...]
        dx_ref[...] = dh_ref[...] + r * (gd - hat * jnp.mean(gd * hat, axis=-1, keepdims=True))
        gg1_ref[...] += jnp.sum(d_xn * hat, axis=0, keepdims=True)

    def tile(w):
        return pl.BlockSpec((tm, w), lambda i: (i, 0))

    def acc(w):
        return pl.BlockSpec((1, w), lambda i: (0, 0))

    return pl.pallas_call(
        body, name="proj_bwd", grid=(t // tm,),
        in_specs=[tile(MAIN_W), tile(512), tile(512), tile(512), tile(512), tile(512), tile(512), tile(LANES),
                  tile(D_MODEL), tile(D_MODEL), _const_spec((MAIN_W, D_MODEL)), _const_spec((LANES, D_MODEL)),
                  _const_spec((1, D_MODEL)), _const_spec((1, 512)), _const_spec((1, 128)), _const_spec((1, 512)),
                  _const_spec((1, 512))],
        out_specs=[tile(D_MODEL), tile(MAIN_W), tile(LANES), acc(512), acc(128), acc(512), acc(512), acc(D_MODEL)],
        out_shape=[jax.ShapeDtypeStruct((t, D_MODEL), F32), jax.ShapeDtypeStruct((t, MAIN_W), BF16),
                   jax.ShapeDtypeStruct((t, LANES), BF16), jax.ShapeDtypeStruct((1, 512), F32),
                   jax.ShapeDtypeStruct((1, 128), F32), jax.ShapeDtypeStruct((1, 512), F32),
                   jax.ShapeDtypeStruct((1, 512), F32), jax.ShapeDtypeStruct((1, D_MODEL), F32)],
        compiler_params=_params(("arbitrary",)),
    )(raw, dqa, dkae, dvae, dqb, dkb, dvb, dfl, x2, dh, w_main_t, w_f_t, g1, gqa, gka, gqb, gkb)


def _pair_rows(a, nb, s):
    two = jnp.stack([a[:, :, 0], a[:, :, 64]], axis=1)
    return jnp.transpose(two.reshape(4, 2, nb, s), (2, 0, 1, 3))


def _head_rows(a, nb, s):
    return jnp.transpose(a[:, 0:8].reshape(nb, s, 4, 2), (0, 2, 3, 1))


def _local_step(x, tgt, w_in_t, w_out, w_up_t, w_down, g1, b_forget, qna, kna, sinks, qnb, knb, g2,
                tm=256, bt=512, tq=512, ts=256, wk=512):
    nb, s, _ = x.shape
    t = nb * s
    x2, tgt2 = x.reshape(t, D_MODEL), tgt.reshape(t, D_MODEL)
    g1r, g2r = g1.reshape(1, D_MODEL), g2.reshape(1, D_MODEL)
    gqa, gka = jnp.tile(qna, 8).reshape(1, 512), jnp.tile(kna, 2).reshape(1, 128)
    gqb, gkb = jnp.tile(qnb, 8).reshape(1, 512), jnp.tile(knb, 8).reshape(1, 512)
    bf_row = jnp.pad(b_forget, (0, LANES - 8)).reshape(1, LANES)
    sink_row = jnp.pad(sinks, (0, LANES - 8)).reshape(1, LANES)
    w_main_t = w_in_t[0:MAIN_W]
    w_f_t = jnp.pad(w_in_t[MAIN_W:IN_W], ((0, LANES - 8), (0, 0)))
    w_main, w_f = w_main_t.T, w_f_t.T
    w_up, w_down_t, w_out_t = w_up_t.T, w_down.T, w_out.T

    xn, raw, fl, qa, kae, vae, qb, kb, vb = _norm_proj(x2, g1r, w_main, w_f, gqa, gka, gqb, gkb, tm)
    c_col = _gate_cumsum(fl, bf_row, nb, s, ts)
    c_rows = _head_rows(c_col, nb, s)
    ma, lse_a = _swa_fwd(qa, kae, vae, sink_row, nb, s, tq)
    mb, lse_b = _fox_fwd(qb, kb, vb, c_col, c_rows, nb, s, bt)
    h, hn, hid, dy, loss_acc = _mlp_fwd(x2, ma, mb, tgt2, w_out[0:512], w_out[512:1024], g2r, w_up, w_down, tm)

    du, dh, dhb, dma, dmb, dla, dlb, gg2 = _mlp_bwd(dy, hid, h, ma, mb, w_down_t, w_up_t, w_out_t, g2r, tm)
    g_down = _wgrad(hid, dy, "wgrad_down", 512, wk)
    g_up_t = _wgrad(du, hn, "wgrad_up", 512, wk)
    g_out = jnp.concatenate([_wgrad(ma, dhb, "wgrad_out_a", 512, wk), _wgrad(mb, dhb, "wgrad_out_b", 512, wk)], axis=0)

    dqa, dkae, dvae, dsink = _swa_bwd(qa, kae, vae, dma, sink_row, _pair_rows(lse_a, nb, s), _head_rows(dla, nb, s), nb, s, tq)
    dqb, dkb, dvb, dc4, dcq = _fox_bwd(qb, kb, vb, dmb, c_col, c_rows, _pair_rows(lse_b, nb, s), _head_rows(dlb, nb, s), nb, s, bt)
    dc_k = jnp.stack([dc4[:, :, 0], dc4[:, :, 64]], axis=-1)
    dc_k = jnp.pad(jnp.transpose(dc_k, (1, 0, 2)).reshape(t, 8), ((0, 0), (0, LANES - 8)))
    dc_q = jnp.pad(jnp.transpose(dcq, (0, 3, 1, 2)).reshape(t, 8), ((0, 0), (0, LANES - 8)))
    dfl, gbf = _gate_cumsum_bwd(dc_k, dc_q, fl, bf_row, nb, s, ts)
    grad_x, dproj, dfb, ggqa, ggka, ggqb, ggkb, gg1 = _proj_bwd(
        raw, dqa, dkae, dvae, dqb, dkb, dvb, dfl, x2, dh, w_main_t, w_f_t, g1r, gqa, gka, gqb, gkb, tm)
    g_in_t = jnp.concatenate([_wgrad(dproj, xn, "wgrad_in", 768, wk), _wgrad(dfb, xn, "wgrad_gate", 128, wk)[0:8]], axis=0)

    small = (gg1.reshape(D_MODEL), gbf[0, 0:8], ggqa.reshape(8, 64).sum(0), ggka.reshape(2, 64).sum(0),
             dsink.sum(0)[:, 0:2, 0].reshape(8), ggqb.reshape(8, 64).sum(0), ggkb.reshape(8, 64).sum(0),
             gg2.reshape(D_MODEL))
    return loss_acc[0, 0], grad_x.reshape(nb, s, D_MODEL), g_in_t, g_out, g_up_t, g_down, small


MESH = pl.DeviceIdType.MESH
ANY = pl.BlockSpec(memory_space=pl.ANY)


def _all_gather(shard):
    r, n = shard.shape

    def body(x_ref, out_ref, send_sems, recv_sems, local_sem):
        x, y, c = lax.axis_index("x"), lax.axis_index("y"), lax.axis_index("c")
        me, sibling = (x, y, c), (x, y, 1 - c)
        chips = [(1 - x, y), (x, 1 - y), (1 - x, 1 - y)]

        def slot(px, py, pc):
            return out_ref.at[4 * px + 2 * py + pc]

        def copy(k, block, to, src=None):
            return pltpu.make_async_remote_copy(
                src_ref=slot(*block) if src is None else src, dst_ref=slot(*block),
                send_sem=send_sems.at[k], recv_sem=recv_sems.at[k], device_id=to, device_id_type=MESH)

        mine = pltpu.make_async_copy(x_ref, slot(*me), local_sem)
        mine.start()
        first = [copy(0, me, sibling, src=x_ref)]
        first += [copy(1 + j, me, (*chip, c), src=x_ref) for j, chip in enumerate(chips)]
        for cp in first:
            cp.start()
        passed = [copy(4 + j, (*chip, c), sibling) for j, chip in enumerate(chips)]
        for j, chip in enumerate(chips):
            copy(1 + j, (*chip, c), me).wait_recv()
            passed[j].start()
        copy(0, sibling, me).wait_recv()
        for j, chip in enumerate(chips):
            copy(4 + j, (*chip, 1 - c), me).wait_recv()
        for cp in first + passed:
            cp.wait_send()
        mine.wait()

    return pl.pallas_call(
        body, name="gather_weights", out_shape=jax.ShapeDtypeStruct((N_DEV, r, n), shard.dtype),
        in_specs=[ANY], out_specs=ANY,
        scratch_shapes=[pltpu.SemaphoreType.DMA((7,)), pltpu.SemaphoreType.DMA((7,)), pltpu.SemaphoreType.DMA(())],
    )(shard)


def _exchange_grads(blocks):
    _, r, n = blocks.shape

    def body(g_ref, out_ref, send_sems, recv_sems, local_sem):
        x, y, c = lax.axis_index("x"), lax.axis_index("y"), lax.axis_index("c")
        my_id = 4 * x + 2 * y + c
        mine = pltpu.make_async_copy(g_ref.at[my_id], out_ref.at[my_id], local_sem)
        mine.start()
        copies = []
        for k in range(1, N_DEV):
            px = 1 - x if k & 4 else x
            py = 1 - y if k & 2 else y
            pc = 1 - c if k & 1 else c
            copies.append(pltpu.make_async_remote_copy(
                src_ref=g_ref.at[4 * px + 2 * py + pc], dst_ref=out_ref.at[my_id],
                send_sem=send_sems.at[k - 1], recv_sem=recv_sems.at[k - 1], device_id=(px, py, pc), device_id_type=MESH))
        for cp in copies:
            cp.start()
        for cp in copies:
            cp.wait_recv()
        for cp in copies:
            cp.wait_send()
        mine.wait()

    return pl.pallas_call(
        body, name="exchange_grads", out_shape=jax.ShapeDtypeStruct(blocks.shape, blocks.dtype),
        in_specs=[ANY], out_specs=ANY,
        scratch_shapes=[pltpu.SemaphoreType.DMA((7,)), pltpu.SemaphoreType.DMA((7,)), pltpu.SemaphoreType.DMA(())],
    )(blocks)


def _sum_adamw(recv, w, m, v, tr):
    _, r, n = recv.shape

    def body(r_ref, w_ref, m_ref, v_ref, g_ref, d_ref, nm_ref, nv_ref):
        g = r_ref[0]
        for s in range(1, N_DEV):
            g = g + r_ref[s]
        g_ref[...] = g
        nm = ADAM_B1 * m_ref[...] + (1.0 - ADAM_B1) * g
        nv = ADAM_B2 * v_ref[...] + (1.0 - ADAM_B2) * (g * g)
        m_hat = nm / (1.0 - ADAM_B1 ** ADAM_STEP)
        v_hat = nv / (1.0 - ADAM_B2 ** ADAM_STEP)
        d_ref[...] = -ADAM_LR * (m_hat / (jnp.sqrt(v_hat) + ADAM_EPS) + ADAM_WD * w_ref[...])
        nm_ref[...] = nm
        nv_ref[...] = nv

    tile = pl.BlockSpec((tr, n), lambda i: (i, 0))
    shp = jax.ShapeDtypeStruct((r, n), F32)
    return pl.pallas_call(
        body, name="sum_adamw", grid=(r // tr,),
        in_specs=[pl.BlockSpec((N_DEV, tr, n), lambda i: (0, i, 0)), tile, tile, tile],
        out_specs=[tile, tile, tile, tile], out_shape=[shp, shp, shp, shp],
        compiler_params=_params(("arbitrary",)),
    )(recv, w, m, v)


def _pack(w_in_s, w_out_s, w_up_s, w_down_s, g1, bf, qna, kna, sk, qnb, knb, g2):
    row2 = jnp.concatenate([bf, qna, kna, sk, qnb, knb])
    small = jnp.zeros((8, D_MODEL), F32).at[0].set(g1).at[1].set(g2).at[2, 0:row2.shape[0]].set(row2)
    return jnp.concatenate([w_out_s, w_up_s.T, w_down_s, w_in_s.T, jnp.zeros((R_SMALL - R_IN - IN_SHARD, D_MODEL), F32),
                            small, jnp.zeros((R_PACK - R_SMALL - 8, D_MODEL), F32)], axis=0)


def _unpack(p):
    row2 = p[R_SMALL + 2]
    return (p[R_SMALL], p[R_IN:R_IN + IN_SHARD].T, row2[0:8], row2[8:72], row2[72:136], row2[136:144], row2[144:208],
            row2[208:272], p[R_OUT:R_UP], p[R_SMALL + 1], p[R_UP:R_DOWN].T, p[R_DOWN:R_IN])


def kernel(x, attn_norm_g, w_in, b_forget, q_norm_a, k_norm_a, sink_logits, q_norm_b, k_norm_b, w_out, mlp_norm_g, w_up, w_down, loss_target, m_attn_norm_g, m_w_in, m_b_forget, m_q_norm_a, m_k_norm_a, m_sink_logits, m_q_norm_b, m_k_norm_b, m_w_out, m_mlp_norm_g, m_w_up, m_w_down, v_attn_norm_g, v_w_in, v_b_forget, v_q_norm_a, v_k_norm_a, v_sink_logits, v_q_norm_b, v_k_norm_b, v_w_out, v_mlp_norm_g, v_w_up, v_w_down):
    w_pack = _pack(w_in, w_out, w_up, w_down, attn_norm_g, b_forget, q_norm_a, k_norm_a, sink_logits, q_norm_b, k_norm_b, mlp_norm_g)
    m_pack = _pack(m_w_in, m_w_out, m_w_up, m_w_down, m_attn_norm_g, m_b_forget, m_q_norm_a, m_k_norm_a, m_sink_logits,
                   m_q_norm_b, m_k_norm_b, m_mlp_norm_g)
    v_pack = _pack(v_w_in, v_w_out, v_w_up, v_w_down, v_attn_norm_g, v_b_forget, v_q_norm_a, v_k_norm_a, v_sink_logits,
                   v_q_norm_b, v_k_norm_b, v_mlp_norm_g)

    full = _all_gather(w_pack.astype(BF16))
    w_out_f = full[:, R_OUT:R_UP].reshape(D_MODEL, D_MODEL)
    w_up_t = full[:, R_UP:R_DOWN].reshape(D_FF, D_MODEL)
    w_down_f = full[:, R_DOWN:R_IN].reshape(D_FF, D_MODEL)
    w_in_t = full[:, R_IN:R_IN + IN_SHARD].reshape(IN_W, D_MODEL)

    loss_part, grad_x, g_in_t, g_out, g_up_t, g_down, small = _local_step(
        x, loss_target, w_in_t, w_out_f, w_up_t, w_down_f, attn_norm_g, b_forget, q_norm_a, k_norm_a, sink_logits,
        q_norm_b, k_norm_b, mlp_norm_g)

    gg1, gbf, gqna, gkna, gsk, gqnb, gknb, gg2 = small
    row2 = jnp.concatenate([gbf, gqna, gkna, gsk, gqnb, gknb])
    small_rows = jnp.zeros((8, D_MODEL), F32).at[0].set(gg1).at[1].set(gg2).at[2, 0:row2.shape[0]].set(row2)
    blocks = jnp.concatenate([
        g_out.reshape(N_DEV, R_UP - R_OUT, D_MODEL), g_up_t.reshape(N_DEV, R_DOWN - R_UP, D_MODEL),
        g_down.reshape(N_DEV, R_IN - R_DOWN, D_MODEL), g_in_t.reshape(N_DEV, IN_SHARD, D_MODEL),
        jnp.zeros((N_DEV, R_SMALL - R_IN - IN_SHARD, D_MODEL), F32), jnp.broadcast_to(small_rows, (N_DEV, 8, D_MODEL)),
        jnp.zeros((N_DEV, R_PACK - R_SMALL - 8, D_MODEL), F32)], axis=1)
    recv = _exchange_grads(blocks)
    g_pack, d_pack, nm_pack, nv_pack = _sum_adamw(recv, w_pack, m_pack, v_pack, 64)

    loss = lax.psum(loss_part, ("x", "y", "c"))
    return (loss, grad_x, *_unpack(g_pack), *_unpack(d_pack), *_unpack(nm_pack), *_unpack(nv_pack))
```

```python
import functools
import math

import jax
import jax.numpy as jnp
from jax import lax
from jax.experimental import pallas as pl
from jax.experimental.pallas import tpu as pltpu

F32 = jnp.float32
BF16 = jnp.bfloat16

D_MODEL = 1024
HEAD_DIM = 64
N_DEV = 8
D_FF = 4096
A_QW = 512
A_KVW = 128
B_W = 512
MAIN_W = 2304
IN_W = 2312
WINDOW = 128
EPS = 1e-6
SCALE = 0.125
LANES = 128
NEG_INF = float("-inf")

ADAM_LR = 0.001
ADAM_B1 = 0.9
ADAM_B2 = 0.999
ADAM_EPS = 1e-08
ADAM_WD = 0.01
ADAM_STEP = 10

R_OUT, R_UP, R_DOWN, R_IN = 0, 128, 640, 1152
IN_SHARD = 289
R_SMALL = 1456
R_PACK = 1472
VMEM_LIMIT = 56 * 1024 * 1024


def _params(sem, vmem=VMEM_LIMIT):
    return pltpu.CompilerParams(dimension_semantics=sem, vmem_limit_bytes=vmem)


def _const_spec(shape):
    nd = len(shape)
    return pl.BlockSpec(shape, lambda *_: (0,) * nd, pipeline_mode=pl.Buffered(1))


def _lane(shape):
    return lax.broadcasted_iota(jnp.int32, shape, len(shape) - 1)


def _split_dot(v, mat):
    hi = v.astype(BF16)
    lo = (v - hi.astype(F32)).astype(BF16)
    return (jnp.dot(hi, mat, preferred_element_type=F32) + jnp.dot(lo, mat, preferred_element_type=F32))


def _head_ones(n):
    r = lax.shift_right_logical(lax.broadcasted_iota(jnp.int32, (n, n), 0), 6)
    c = lax.shift_right_logical(lax.broadcasted_iota(jnp.int32, (n, n), 1), 6)
    return (r == c).astype(BF16)


def _head_sum(v):
    w = v.shape[1]
    if w <= 256:
        return _split_dot(v, _head_ones(w))
    ones = _head_ones(256)
    return jnp.concatenate([_split_dot(v[:, s:s + 256], ones) for s in range(0, w, 256)], axis=1)


def _head_norm(seg, gain):
    rs = lax.rsqrt(_head_sum(seg * seg) * (1.0 / HEAD_DIM) + EPS)
    return seg * rs * gain


def _head_norm_bwd(seg, gain, d_out):
    rs = lax.rsqrt(_head_sum(seg * seg) * (1.0 / HEAD_DIM) + EPS)
    hat = seg * rs
    gd = d_out * gain
    d_seg = rs * (gd - hat * (_head_sum(gd * hat) * (1.0 / HEAD_DIM)))
    return d_seg, d_out * hat


def _expand_kv(v):
    r = pltpu.roll(v, 64, axis=1)
    lo = _lane(v.shape) < 64
    return jnp.concatenate([jnp.where(lo, v, r), jnp.where(lo, r, v)], axis=1)


def _fold_kv(e4):
    t0 = e4[:, 0:128] + e4[:, 128:256]
    t1 = e4[:, 256:384] + e4[:, 384:512]
    t0 = t0 + pltpu.roll(t0, 64, axis=1)
    t1 = t1 + pltpu.roll(t1, 64, axis=1)
    return jnp.where(_lane(t0.shape) < 64, t0, t1)


def _pick_lane(blk, idx):
    return jnp.sum(jnp.where(_lane(blk.shape) == idx, blk, 0.0), axis=1, keepdims=True)


def _nt(a, b):
    return lax.dot_general(a, b, (((1,), (1,)), ((), ())), preferred_element_type=F32)


def _tn(a, b):
    return lax.dot_general(a, b, (((0,), (0,)), ((), ())), preferred_element_type=F32)


def _norm_proj(x2, g1, w_main, w_f, gqa, gka, gqb, gkb, tm):
    t = x2.shape[0]

    def body(x_ref, g1_ref, wm_ref, wf_ref, gqa_ref, gka_ref, gqb_ref, gkb_ref,
             xn_ref, raw_ref, fl_ref, qa_ref, kae_ref, vae_ref, qb_ref, kb_ref, vb_ref):
        x = x_ref[...]
        r = lax.rsqrt(jnp.mean(x * x, axis=-1, keepdims=True) + EPS)
        xn = (x * r * g1_ref[...]).astype(BF16)
        xn_ref[...] = xn
        proj = jnp.dot(xn, wm_ref[...], preferred_element_type=F32)
        raw_ref[...] = proj
        fl_ref[...] = jnp.dot(xn, wf_ref[...], preferred_element_type=F32)
        qa_ref[...] = _head_norm(proj[:, 0:512], gqa_ref[...]).astype(BF16)
        kae_ref[...] = _expand_kv(_head_norm(proj[:, 512:640], gka_ref[...])).astype(BF16)
        vae_ref[...] = _expand_kv(proj[:, 640:768]).astype(BF16)
        qb_ref[...] = _head_norm(proj[:, 768:1280], gqb_ref[...]).astype(BF16)
        kb_ref[...] = _head_norm(proj[:, 1280:1792], gkb_ref[...]).astype(BF16)
        vb_ref[...] = proj[:, 1792:2304].astype(BF16)

    def tile(w):
        return pl.BlockSpec((tm, w), lambda i: (i, 0))

    return pl.pallas_call(
        body, name="norm_proj", grid=(t // tm,),
        in_specs=[tile(D_MODEL), _const_spec((1, D_MODEL)), _const_spec((D_MODEL, MAIN_W)), _const_spec((D_MODEL, LANES)),
                  _const_spec((1, 512)), _const_spec((1, 128)), _const_spec((1, 512)), _const_spec((1, 512))],
        out_specs=[tile(D_MODEL), tile(MAIN_W), tile(LANES), tile(512), tile(256), tile(256), tile(512), tile(512), tile(512)],
        out_shape=[jax.ShapeDtypeStruct((t, D_MODEL), BF16), jax.ShapeDtypeStruct((t, MAIN_W), F32),
                   jax.ShapeDtypeStruct((t, LANES), F32), jax.ShapeDtypeStruct((t, 512), BF16),
                   jax.ShapeDtypeStruct((t, 256), BF16), jax.ShapeDtypeStruct((t, 256), BF16),
                   jax.ShapeDtypeStruct((t, 512), BF16), jax.ShapeDtypeStruct((t, 512), BF16),
                   jax.ShapeDtypeStruct((t, 512), BF16)],
        compiler_params=_params(("arbitrary",)),
    )(x2, g1, w_main, w_f, gqa, gka, gqb, gkb)


def _tri(n, upper):
    r = lax.broadcasted_iota(jnp.int32, (n, n), 0)
    c = lax.broadcasted_iota(jnp.int32, (n, n), 1)
    return ((c >= r) if upper else (c <= r)).astype(F32)


def _gate_cumsum(fl, bf_row, nb, s, ts):
    t = fl.shape[0]
    nt = s // ts

    def body(fl_ref, b_ref, c_ref, carry):
        @pl.when(pl.program_id(1) == 0)
        def _():
            carry[...] = jnp.zeros_like(carry)

        z = fl_ref[...] + b_ref[...]
        e = jnp.exp(-jnp.abs(z))
        u = 1.0 + e
        log1p = jnp.where(u == 1.0, e, jnp.log(u) * (e / (u - 1.0)))
        lf = jnp.minimum(z, 0.0) - log1p
        c_ref[...] = jnp.dot(_tri(ts, False), lf, precision=lax.Precision.HIGHEST, preferred_element_type=F32) + carry[...]
        carry[...] = c_ref[pl.ds(ts - 1, 1), :]

    return pl.pallas_call(
        body, name="gate_cumsum", grid=(nb, nt),
        in_specs=[pl.BlockSpec((ts, LANES), lambda b, i: (b * nt + i, 0)), _const_spec((1, LANES))],
        out_specs=pl.BlockSpec((ts, LANES), lambda b, i: (b * nt + i, 0)),
        out_shape=jax.ShapeDtypeStruct((t, LANES), F32),
        scratch_shapes=[pltpu.VMEM((1, LANES), F32)],
        compiler_params=_params(("arbitrary", "arbitrary")),
    )(fl, bf_row)


def _gate_cumsum_bwd(dc_k, dc_q, fl, bf_row, nb, s, ts):
    t = fl.shape[0]
    nt = s // ts

    def body(dck_ref, dcq_ref, fl_ref, b_ref, df_ref, gb_ref, carry, dlf_ref):
        @pl.when(pl.program_id(1) == 0)
        def _():
            carry[...] = jnp.zeros_like(carry)

        @pl.when((pl.program_id(0) == 0) & (pl.program_id(1) == 0))
        def _():
            gb_ref[...] = jnp.zeros_like(gb_ref)

        dlf_ref[...] = jnp.dot(_tri(ts, True), dck_ref[...] + dcq_ref[...], precision=lax.Precision.HIGHEST,
                               preferred_element_type=F32) + carry[...]
        carry[...] = dlf_ref[pl.ds(0, 1), :]
        dlf = dlf_ref[...]
        z = fl_ref[...] + b_ref[...]
        df = dlf * (1.0 / (1.0 + jnp.exp(z)))
        df_ref[...] = df
        gb_ref[...] += jnp.sum(df, axis=0, keepdims=True)

    def rev(b, i):
        return (b * nt + (nt - 1 - i), 0)

    return pl.pallas_call(
        body, name="gate_cumsum_bwd", grid=(nb, nt),
        in_specs=[pl.BlockSpec((ts, LANES), rev), pl.BlockSpec((ts, LANES), rev), pl.BlockSpec((ts, LANES), rev),
                  _const_spec((1, LANES))],
        out_specs=[pl.BlockSpec((ts, LANES), rev), pl.BlockSpec((1, LANES), lambda b, i: (0, 0))],
        out_shape=[jax.ShapeDtypeStruct((t, LANES), F32), jax.ShapeDtypeStruct((1, LANES), F32)],
        scratch_shapes=[pltpu.VMEM((1, LANES), F32), pltpu.VMEM((ts, LANES), F32)],
        compiler_params=_params(("arbitrary", "arbitrary")),
    )(dc_k, dc_q, fl, bf_row)


def _slope(p, hh):
    out = jnp.float32(2.0 ** -(2 * 3 + hh + 1))
    for pp in (2, 1, 0):
        out = jnp.where(p == pp, jnp.float32(2.0 ** -(2 * pp + hh + 1)), out)
    return out


def _swa_fwd(qa, kae, vae, sink_row, nb, s, tq):
    t = qa.shape[0]
    nq = s // tq
    nsub = tq // WINDOW

    def body(q_ref, k_ref, v_ref, sink_ref, o_ref, lse_ref):
        p, i = pl.program_id(1), pl.program_id(2)
        lo = _lane((1, LANES)) < 64
        row = lax.broadcasted_iota(jnp.int32, (WINDOW, WINDOW), 0)
        col = lax.broadcasted_iota(jnp.int32, (WINDOW, WINDOW), 1)
        rel = (row - col).astype(F32)
        for u in range(nsub):
            g0 = i * nsub + u
            prev = pl.multiple_of(jnp.maximum(g0 - 1, 0) * WINDOW, WINDOW)
            cur = pl.multiple_of(g0 * WINDOW, WINDOW)
            kp, kc = k_ref[pl.ds(prev, WINDOW), :], k_ref[pl.ds(cur, WINDOW), :]
            vp, vc = v_ref[pl.ds(prev, WINDOW), :], v_ref[pl.ds(cur, WINDOW), :]
            qs = (q_ref[u * WINDOW:(u + 1) * WINDOW, :].astype(F32) * SCALE).astype(BF16)
            outs, lses = [], []
            for hh in range(2):
                mask_h = lo if hh == 0 else jnp.logical_not(lo)
                qh = jnp.where(mask_h, qs, jnp.zeros_like(qs))
                slope = _slope(p, hh)
                sink = _pick_lane(sink_ref[...], 2 * p + hh)
                sp = _nt(qh, kp) - slope * (rel + float(WINDOW))
                sc = _nt(qh, kc) - slope * rel
                sp = jnp.where((col > row) & (g0 > 0), sp, NEG_INF)
                sc = jnp.where(col <= row, sc, NEG_INF)
                m = jnp.maximum(jnp.maximum(jnp.max(sp, axis=1, keepdims=True), jnp.max(sc, axis=1, keepdims=True)), sink)
                pp, pc = jnp.exp(sp - m), jnp.exp(sc - m)
                den = jnp.sum(pp, axis=1, keepdims=True) + jnp.sum(pc, axis=1, keepdims=True) + jnp.exp(sink - m)
                pp, pc = (pp / den).astype(BF16), (pc / den).astype(BF16)
                outs.append(jnp.dot(pp, vp, preferred_element_type=F32) + jnp.dot(pc, vc, preferred_element_type=F32))
                lses.append(m + jnp.log(den))
            o_ref[u * WINDOW:(u + 1) * WINDOW, :] = jnp.where(lo, outs[0], outs[1]).astype(BF16)
            lse_ref[u * WINDOW:(u + 1) * WINDOW, :] = jnp.where(lo, lses[0], lses[1])

    return pl.pallas_call(
        body, name="swa_fwd", grid=(nb, 4, nq),
        in_specs=[pl.BlockSpec((tq, LANES), lambda b, p, i: (b * nq + i, p)),
                  pl.BlockSpec((s, LANES), lambda b, p, i: (b, lax.shift_right_logical(p, 1))),
                  pl.BlockSpec((s, LANES), lambda b, p, i: (b, lax.shift_right_logical(p, 1))),
                  pl.BlockSpec((1, LANES), lambda b, p, i: (0, 0))],
        out_specs=[pl.BlockSpec((tq, LANES), lambda b, p, i: (b * nq + i, p)),
                   pl.BlockSpec((None, tq, LANES), lambda b, p, i: (p, b * nq + i, 0))],
        out_shape=[jax.ShapeDtypeStruct((t, 512), BF16), jax.ShapeDtypeStruct((4, t, LANES), F32)],
        compiler_params=_params(("arbitrary", "arbitrary", "arbitrary")),
    )(qa, kae, vae, sink_row)


def _swa_bwd(qa, kae, vae, do_a, sink_row, lse_rows, delta_rows, nb, s, tq):
    t = qa.shape[0]
    nq = s // tq
    nsub = tq // WINDOW

    def body(q_ref, do_ref, k_ref, v_ref, sink_ref, lse_ref, dl_ref, dq_ref, dk_ref, dv_ref, ds_ref):
        p, i = pl.program_id(1), pl.program_id(2)

        @pl.when(i == 0)
        def _():
            dk_ref[...] = jnp.zeros_like(dk_ref)
            dv_ref[...] = jnp.zeros_like(dv_ref)
            ds_ref[...] = jnp.zeros_like(ds_ref)

        lo = _lane((1, LANES)) < 64
        row = lax.broadcasted_iota(jnp.int32, (WINDOW, WINDOW), 0)
        col = lax.broadcasted_iota(jnp.int32, (WINDOW, WINDOW), 1)
        rel = (col - row).astype(F32)
        for u in range(nsub):
            g0 = i * nsub + u
            prev = pl.multiple_of(jnp.maximum(g0 - 1, 0) * WINDOW, WINDOW)
            cur = pl.multiple_of(g0 * WINDOW, WINDOW)
            chunks = []
            for start in (prev, cur):
                kk = k_ref[pl.ds(start, WINDOW), :]
                chunks.append((start, (kk.astype(F32) * SCALE).astype(BF16), v_ref[pl.ds(start, WINDOW), :]))
            qsub = q_ref[u * WINDOW:(u + 1) * WINDOW, :]
            dosub = do_ref[u * WINDOW:(u + 1) * WINDOW, :]
            dq_h = []
            for hh in range(2):
                mask_h = lo if hh == 0 else jnp.logical_not(lo)
                qh = jnp.where(mask_h, qsub, jnp.zeros_like(qsub))
                doh = jnp.where(mask_h, dosub, jnp.zeros_like(dosub))
                slope = _slope(p, hh)
                sink = _pick_lane(sink_ref[...], 2 * p + hh)
                lse = lse_ref[pl.ds(hh, 1), pl.ds(cur, WINDOW)]
                dlt = dl_ref[pl.ds(hh, 1), pl.ds(cur, WINDOW)]
                psink = jnp.exp(sink - lse)
                row_h = lax.broadcasted_iota(jnp.int32, (8, LANES), 0)
                ds_ref[...] += jnp.where(row_h == hh, -jnp.sum(psink * dlt, axis=1, keepdims=True), 0.0)
                dq_acc = jnp.zeros((WINDOW, LANES), F32)
                for ci, (start, ks, vv) in enumerate(chunks):
                    if ci == 0:
                        valid = (row > col) & (g0 > 0)
                        dist = rel + float(WINDOW)
                    else:
                        valid = row <= col
                        dist = rel
                    st = _nt(ks, qh) - slope * dist - lse
                    pt = jnp.where(valid, jnp.exp(jnp.where(valid, st, 0.0)), 0.0)
                    dpt = _nt(vv, doh)
                    dst = pt * (dpt - dlt)
                    ptb, dstb = pt.astype(BF16), dst.astype(BF16)
                    dv_ref[pl.ds(start, WINDOW), :] += jnp.dot(ptb, doh, preferred_element_type=F32)
                    dk_ref[pl.ds(start, WINDOW), :] += jnp.dot(dstb, qh, preferred_element_type=F32) * SCALE
                    dq_acc = dq_acc + _tn(dstb, ks)
                dq_h.append(dq_acc)
            dq_ref[u * WINDOW:(u + 1) * WINDOW, :] = jnp.where(lo, dq_h[0], dq_h[1])

    rows = pl.BlockSpec((None, None, 2, s), lambda b, p, i: (b, p, 0, 0))
    return pl.pallas_call(
        body, name="swa_bwd", grid=(nb, 4, nq),
        in_specs=[pl.BlockSpec((tq, LANES), lambda b, p, i: (b * nq + i, p)),
                  pl.BlockSpec((tq, LANES), lambda b, p, i: (b * nq + i, p)),
                  pl.BlockSpec((s, LANES), lambda b, p, i: (b, lax.shift_right_logical(p, 1))),
                  pl.BlockSpec((s, LANES), lambda b, p, i: (b, lax.shift_right_logical(p, 1))),
                  pl.BlockSpec((1, LANES), lambda b, p, i: (0, 0)), rows, rows],
        out_specs=[pl.BlockSpec((tq, LANES), lambda b, p, i: (b * nq + i, p)),
                   pl.BlockSpec((s, LANES), lambda b, p, i: (b, p)),
                   pl.BlockSpec((s, LANES), lambda b, p, i: (b, p)),
                   pl.BlockSpec((None, None, 8, LANES), lambda b, p, i: (b, p, 0, 0))],
        out_shape=[jax.ShapeDtypeStruct((t, 512), F32), jax.ShapeDtypeStruct((t, 512), F32),
                   jax.ShapeDtypeStruct((t, 512), F32), jax.ShapeDtypeStruct((nb, 4, 8, LANES), F32)],
        compiler_params=_params(("arbitrary", "arbitrary", "arbitrary")),
    )(qa, do_a, kae, vae, sink_row, lse_rows, delta_rows)


MESH = pl.DeviceIdType.MESH
ANY = pl.BlockSpec(memory_space=pl.ANY)
N_SEM = 7


def _gather_steps(x_ref, out_ref, send_sems, recv_sems, local_sem):
    x, y, c = lax.axis_index("x"), lax.axis_index("y"), lax.axis_index("c")
    me, sibling = (x, y, c), (x, y, 1 - c)
    chips = [(1 - x, y), (x, 1 - y), (1 - x, 1 - y)]

    def slot(px, py, pc):
        return out_ref.at[4 * px + 2 * py + pc]

    def copy(k, block, to, src=None):
        return pltpu.make_async_remote_copy(
            src_ref=slot(*block) if src is None else src, dst_ref=slot(*block),
            send_sem=send_sems.at[k], recv_sem=recv_sems.at[k], device_id=to, device_id_type=MESH)

    mine = pltpu.make_async_copy(x_ref, slot(*me), local_sem)
    first = [copy(0, me, sibling, src=x_ref)] + [copy(1 + j, me, (*chip, c), src=x_ref) for j, chip in enumerate(chips)]
    passed = [copy(4 + j, (*chip, c), sibling) for j, chip in enumerate(chips)]

    def start():
        mine.start()
        for cp in first:
            cp.start()

    def forward():
        for j, chip in enumerate(chips):
            copy(1 + j, (*chip, c), me).wait_recv()
            passed[j].start()

    def finish():
        copy(0, sibling, me).wait_recv()
        for j, chip in enumerate(chips):
            copy(4 + j, (*chip, 1 - c), me).wait_recv()
        for cp in first + passed:
            cp.wait_send()
        mine.wait()

    return start, forward, finish


def _exchange_steps(pairs, send_sems, recv_sems, local_sems):
    x, y, c = lax.axis_index("x"), lax.axis_index("y"), lax.axis_index("c")
    my_id = 4 * x + 2 * y + c
    local, remote = [], []
    for a, (src, dst) in enumerate(pairs):
        local.append(pltpu.make_async_copy(src.at[my_id], dst.at[my_id], local_sems.at[a]))
        for k in range(1, N_DEV):
            px = 1 - x if k & 4 else x
            py = 1 - y if k & 2 else y
            pc = 1 - c if k & 1 else c
            remote.append(pltpu.make_async_remote_copy(
                src_ref=src.at[4 * px + 2 * py + pc], dst_ref=dst.at[my_id],
                send_sem=send_sems.at[N_SEM * a + k - 1], recv_sem=recv_sems.at[N_SEM * a + k - 1],
                device_id=(px, py, pc), device_id_type=MESH))

    def start():
        for cp in local + remote:
            cp.start()

    def finish():
        for cp in remote:
            cp.wait_recv()
        for cp in remote:
            cp.wait_send()
        for cp in local:
            cp.wait()

    return start, finish


def _fox_fwd(qb, kb, vb, c_col, c_rows, nb, s, bt, shard=None):
    t = qb.shape[0]
    nq = s // bt
    n_in = 5

    def body(*refs):
        q_ref, k_ref, v_ref, cc_ref, cr_ref = refs[:n_in]
        if shard is None:
            o_ref, lse_ref = refs[n_in:]
        else:
            x_ref, o_ref, lse_ref, full_ref, send_sems, recv_sems, local_sem = refs[n_in:]
            start, forward, finish = _gather_steps(x_ref, full_ref, send_sems, recv_sems, local_sem)
            step = (pl.program_id(0) * 4 + pl.program_id(1)) * nq + pl.program_id(2)
            pl.when(step == 0)(start)
            pl.when(step == nb * 2 * nq)(forward)
        j, i = pl.program_id(1), pl.program_id(2)
        lo = _lane((1, LANES)) < 64
        row = lax.broadcasted_iota(jnp.int32, (bt, bt), 0)
        col = lax.broadcasted_iota(jnp.int32, (bt, bt), 1)
        qs = (q_ref[...].astype(F32) * SCALE).astype(BF16)
        cc = cc_ref[...]
        outs, lses = [], []
        for hh in range(2):
            mask_h = lo if hh == 0 else jnp.logical_not(lo)
            qh = jnp.where(mask_h, qs, jnp.zeros_like(qs))
            cq = _pick_lane(cc, 2 * j + hh)

            def blk(kb_i, carry, diag):
                m, l, acc = carry
                start = pl.multiple_of(kb_i * bt, bt)
                sc = _nt(qh, k_ref[pl.ds(start, bt), :]) + (cq - cr_ref[pl.ds(hh, 1), pl.ds(start, bt)])
                if diag:
                    sc = jnp.where(row >= col, sc, NEG_INF)
                m_new = jnp.maximum(m, jnp.max(sc, axis=1, keepdims=True))
                alpha = jnp.exp(m - m_new)
                pr = jnp.exp(sc - m_new)
                l = alpha * l + jnp.sum(pr, axis=1, keepdims=True)
                acc = alpha * acc + jnp.dot(pr.astype(BF16), v_ref[pl.ds(start, bt), :], preferred_element_type=F32)
                return m_new, l, acc

            init = (jnp.full((bt, 1), NEG_INF, F32), jnp.zeros((bt, 1), F32), jnp.zeros((bt, LANES), F32))
            carry = lax.fori_loop(0, i, lambda kb_i, c: blk(kb_i, c, False), init)
            m, l, acc = blk(i, carry, True)
            outs.append(acc / l)
            lses.append(m + jnp.log(l))
        o_ref[...] = jnp.where(lo, outs[0], outs[1]).astype(BF16)
        lse_ref[...] = jnp.where(lo, lses[0], lses[1])
        if shard is not None:
            pl.when(step == nb * 4 * nq - 1)(finish)

    rows = pl.BlockSpec((None, None, 2, s), lambda b, j, i: (b, j, 0, 0))
    in_specs = [pl.BlockSpec((bt, LANES), lambda b, j, i: (b * nq + i, j)),
                pl.BlockSpec((s, LANES), lambda b, j, i: (b, j)),
                pl.BlockSpec((s, LANES), lambda b, j, i: (b, j)),
                pl.BlockSpec((bt, LANES), lambda b, j, i: (b * nq + i, 0)), rows]
    out_specs = [pl.BlockSpec((bt, LANES), lambda b, j, i: (b * nq + i, j)),
                 pl.BlockSpec((None, bt, LANES), lambda b, j, i: (j, b * nq + i, 0))]
    out_shape = [jax.ShapeDtypeStruct((t, 512), BF16), jax.ShapeDtypeStruct((4, t, LANES), F32)]
    args, scratch = [qb, kb, vb, c_col, c_rows], []
    if shard is not None:
        in_specs.append(ANY)
        out_specs.append(ANY)
        out_shape.append(jax.ShapeDtypeStruct((N_DEV,) + shard.shape, shard.dtype))
        args.append(shard)
        scratch = [pltpu.SemaphoreType.DMA((N_SEM,)), pltpu.SemaphoreType.DMA((N_SEM,)), pltpu.SemaphoreType.DMA(())]
    return pl.pallas_call(
        body, name="fox_fwd", grid=(nb, 4, nq), in_specs=in_specs, out_specs=out_specs, out_shape=out_shape,
        scratch_shapes=scratch, compiler_params=_params(("arbitrary", "arbitrary", "arbitrary")),
    )(*args)


def _fox_bwd(qb, kb, vb, do_b, c_col, c_rows, lse_rows, delta_rows, nb, s, bt, exch=()):
    t = qb.shape[0]
    nk = s // bt
    n_in, n_out, n_ex = 8, 5, len(exch)

    def body(*refs):
        q_ref, do_ref, k_ref, v_ref, cc_ref, cr_ref, lse_ref, dl_ref = refs[:n_in]
        dq_ref, dk_ref, dv_ref, dc_ref, dcq_ref = refs[n_in + n_ex:n_in + n_ex + n_out]
        if exch:
            srcs = refs[n_in:n_in + n_ex]
            dsts = refs[n_in + n_ex + n_out:n_in + 2 * n_ex + n_out]
            start, finish = _exchange_steps(list(zip(srcs, dsts)), *refs[n_in + 2 * n_ex + n_out:])
            step = (pl.program_id(0) * 4 + pl.program_id(1)) * nk + pl.program_id(2)
            pl.when(step == 0)(start)
        j, kb_i = pl.program_id(1), pl.program_id(2)

        @pl.when(kb_i == 0)
        def _():
            dq_ref[...] = jnp.zeros_like(dq_ref)
            dcq_ref[...] = jnp.zeros_like(dcq_ref)

        lo = _lane((1, LANES)) < 64
        row = lax.broadcasted_iota(jnp.int32, (bt, bt), 0)
        col = lax.broadcasted_iota(jnp.int32, (bt, bt), 1)
        k2, v2 = k_ref[...], v_ref[...]
        ks = (k2.astype(F32) * SCALE).astype(BF16)
        cc = cc_ref[...]
        dk_acc = jnp.zeros((bt, LANES), F32)
        dv_acc = jnp.zeros((bt, LANES), F32)
        dcs = []
        for hh in range(2):
            mask_h = lo if hh == 0 else jnp.logical_not(lo)
            kh = jnp.where(mask_h, ks, jnp.zeros_like(ks))
            ck = _pick_lane(cc, 2 * j + hh)

            def blk(qi, carry, diag):
                dk_a, dv_a, dc_a = carry
                start = pl.multiple_of(qi * bt, bt)
                qblk, doblk = q_ref[pl.ds(start, bt), :], do_ref[pl.ds(start, bt), :]
                qh = jnp.where(mask_h, qblk, jnp.zeros_like(qblk))
                doh = jnp.where(mask_h, doblk, jnp.zeros_like(doblk))
                a_row = cr_ref[pl.ds(hh, 1), pl.ds(start, bt)] - lse_ref[pl.ds(hh, 1), pl.ds(start, bt)]
                st = _nt(ks, qh) + (a_row - ck)
                if diag:
                    pt = jnp.where(col >= row, jnp.exp(jnp.where(col >= row, st, 0.0)), 0.0)
                else:
                    pt = jnp.exp(st)
                dpt = _nt(v2, doh)
                dst = pt * (dpt - dl_ref[pl.ds(hh, 1), pl.ds(start, bt)])
                ptb, dstb = pt.astype(BF16), dst.astype(BF16)
                dv_a = dv_a + jnp.dot(ptb, doh, preferred_element_type=F32)
                dk_a = dk_a + jnp.dot(dstb, qh, preferred_element_type=F32)
                dc_a = dc_a + jnp.sum(dst, axis=1, keepdims=True)
                dq_ref[pl.ds(start, bt), :] += _tn(dstb, kh)
                dcq_ref[pl.ds(hh, 1), pl.ds(start, bt)] += jnp.sum(dst, axis=0, keepdims=True)
                return dk_a, dv_a, dc_a

            carry = blk(kb_i, (dk_acc, dv_acc, jnp.zeros((bt, 1), F32)), True)
            dk_acc, dv_acc, dc_h = lax.fori_loop(kb_i + 1, nk, lambda qi, c: blk(qi, c, False), carry)
            dcs.append(dc_h)
        dk_ref[...] = dk_acc * SCALE
        dv_ref[...] = dv_acc
        dc_ref[...] = -jnp.where(lo, dcs[0], dcs[1])
        if exch:
            pl.when(step == nb * 4 * nk - 1)(finish)

    rows = pl.BlockSpec((None, None, 2, s), lambda b, j, kb_i: (b, j, 0, 0))
    scratch = []
    if exch:
        scratch = [pltpu.SemaphoreType.DMA((N_SEM * n_ex,)), pltpu.SemaphoreType.DMA((N_SEM * n_ex,)),
                   pltpu.SemaphoreType.DMA((n_ex,))]
    return pl.pallas_call(
        body, name="fox_bwd", grid=(nb, 4, nk),
        in_specs=[pl.BlockSpec((s, LANES), lambda b, j, kb_i: (b, j)),
                  pl.BlockSpec((s, LANES), lambda b, j, kb_i: (b, j)),
                  pl.BlockSpec((bt, LANES), lambda b, j, kb_i: (b * nk + kb_i, j)),
                  pl.BlockSpec((bt, LANES), lambda b, j, kb_i: (b * nk + kb_i, j)),
                  pl.BlockSpec((bt, LANES), lambda b, j, kb_i: (b * nk + kb_i, 0)), rows, rows, rows] + [ANY] * n_ex,
        out_specs=[pl.BlockSpec((s, LANES), lambda b, j, kb_i: (b, j)),
                   pl.BlockSpec((bt, LANES), lambda b, j, kb_i: (b * nk + kb_i, j)),
                   pl.BlockSpec((bt, LANES), lambda b, j, kb_i: (b * nk + kb_i, j)),
                   pl.BlockSpec((None, bt, LANES), lambda b, j, kb_i: (j, b * nk + kb_i, 0)), rows] + [ANY] * n_ex,
        out_shape=[jax.ShapeDtypeStruct((t, 512), F32), jax.ShapeDtypeStruct((t, 512), F32),
                   jax.ShapeDtypeStruct((t, 512), F32), jax.ShapeDtypeStruct((4, t, LANES), F32),
                   jax.ShapeDtypeStruct((nb, 4, 2, s), F32)] + [jax.ShapeDtypeStruct(e.shape, e.dtype) for e in exch],
        scratch_shapes=scratch, compiler_params=_params(("arbitrary", "arbitrary", "arbitrary")),
    )(qb, do_b, kb, vb, c_col, c_rows, lse_rows, delta_rows, *exch)


def _mlp_fwd(x2, ma, mb, tgt, wo_a, wo_b, g2, w_up, w_down, tm):
    t = x2.shape[0]

    def body(x_ref, ma_ref, mb_ref, tg_ref, woa_ref, wob_ref, g2_ref, wu_ref, wd_ref,
             h_ref, hn_ref, hid_ref, dy_ref, dyb_ref, loss_ref):
        @pl.when(pl.program_id(0) == 0)
        def _():
            loss_ref[...] = jnp.zeros_like(loss_ref)

        h = (x_ref[...] + jnp.dot(ma_ref[...], woa_ref[...], preferred_element_type=F32)
             + jnp.dot(mb_ref[...], wob_ref[...], preferred_element_type=F32))
        h_ref[...] = h
        r = lax.rsqrt(jnp.mean(h * h, axis=-1, keepdims=True) + EPS)
        hn = (h * r * g2_ref[...]).astype(BF16)
        hn_ref[...] = hn
        u = jnp.maximum(jnp.dot(hn, wu_ref[...], preferred_element_type=F32), 0.0)
        hid = (u * u).astype(BF16)
        hid_ref[...] = hid
        y = h + jnp.dot(hid, wd_ref[...], preferred_element_type=F32)
        err = y - tg_ref[...]
        dy = err * (1.0 / D_MODEL)
        dy_ref[...] = dy
        dyb_ref[...] = dy.astype(BF16)
        part =0.5 * jnp.sum(jnp.sum(err * err, axis=1, keepdims=True) * (1.0 / D_MODEL), axis=0, keepdims=True)
        loss_ref[...] += part

    def tile(w):
        return pl.BlockSpec((tm, w), lambda i: (i, 0))

    return pl.pallas_call(
        body, name="mlp_fwd", grid=(t // tm,),
        in_specs=[tile(D_MODEL), tile(512), tile(512), tile(D_MODEL), _const_spec((512, D_MODEL)), _const_spec((512, D_MODEL)),
                  _const_spec((1, D_MODEL)), _const_spec((D_MODEL, D_FF)), _const_spec((D_FF, D_MODEL))],
        out_specs=[tile(D_MODEL), tile(D_MODEL), tile(D_FF), tile(D_MODEL), tile(D_MODEL),
                   pl.BlockSpec((8, LANES), lambda i: (0, 0))],
        out_shape=[jax.ShapeDtypeStruct((t, D_MODEL), F32), jax.ShapeDtypeStruct((t, D_MODEL), BF16),
                   jax.ShapeDtypeStruct((t, D_FF), BF16), jax.ShapeDtypeStruct((t, D_MODEL), F32),
                   jax.ShapeDtypeStruct((t, D_MODEL), BF16), jax.ShapeDtypeStruct((8, LANES), F32)],
        compiler_params=_params(("arbitrary",)),
    )(x2, ma, mb, tgt, wo_a, wo_b, g2, w_up, w_down)


def _mlp_bwd(dy, hid, h, ma, mb, w_down_t, w_up_t, w_out_t, g2, tm):
    t = dy.shape[0]

    def body(dy_ref, hid_ref, h_ref, ma_ref, mb_ref, wdt_ref, wut_ref, wot_ref, g2_ref,
             du_ref, dh_ref, dhb_ref, dma_ref, dmb_ref, dla_ref, dlb_ref, gg_ref):
        @pl.when(pl.program_id(0) == 0)
        def _():
            gg_ref[...] = jnp.zeros_like(gg_ref)

        dy = dy_ref[...]
        d_hid = jnp.dot(dy.astype(BF16), wdt_ref[...], preferred_element_type=F32)
        du = (d_hid * (2.0 * jnp.sqrt(hid_ref[...].astype(F32)))).astype(BF16)
        du_ref[...] = du
        d_hn = jnp.dot(du, wut_ref[...], preferred_element_type=F32)
        h = h_ref[...]
        r = lax.rsqrt(jnp.mean(h * h, axis=-1, keepdims=True) + EPS)
        hat = h * r
        gd = d_hn * g2_ref[...]
        dh = dy + r * (gd - hat * jnp.mean(gd * hat, axis=-1, keepdims=True))
        gg_ref[...] += jnp.sum(d_hn * hat, axis=0, keepdims=True)
        dh_ref[...] = dh
        dhb = dh.astype(BF16)
        dhb_ref[...] = dhb
        dm = jnp.dot(dhb, wot_ref[...], preferred_element_type=F32).astype(BF16)
        dma, dmb = dm[:, 0:512], dm[:, 512:1024]
        dma_ref[...] = dma
        dmb_ref[...] = dmb
        sel = (lax.shift_right_logical(lax.broadcasted_iota(jnp.int32, (512, LANES), 0), 6)
               == lax.broadcasted_iota(jnp.int32, (512, LANES), 1)).astype(BF16)
        dla_ref[...] = _split_dot(dma.astype(F32) * ma_ref[...].astype(F32), sel)
        dlb_ref[...] = _split_dot(dmb.astype(F32) * mb_ref[...].astype(F32), sel)

    def tile(w):
        return pl.BlockSpec((tm, w), lambda i: (i, 0))

    return pl.pallas_call(
        body, name="mlp_bwd", grid=(t // tm,),
        in_specs=[tile(D_MODEL), tile(D_FF), tile(D_MODEL), tile(512), tile(512), _const_spec((D_MODEL, D_FF)),
                  _const_spec((D_FF, D_MODEL)), _const_spec((D_MODEL, D_MODEL)), _const_spec((1, D_MODEL))],
        out_specs=[tile(D_FF), tile(D_MODEL), tile(D_MODEL), tile(512), tile(512), tile(LANES), tile(LANES),
                   pl.BlockSpec((1, D_MODEL), lambda i: (0, 0))],
        out_shape=[jax.ShapeDtypeStruct((t, D_FF), BF16), jax.ShapeDtypeStruct((t, D_MODEL), F32),
                   jax.ShapeDtypeStruct((t, D_MODEL), BF16), jax.ShapeDtypeStruct((t, 512), BF16),
                   jax.ShapeDtypeStruct((t, 512), BF16), jax.ShapeDtypeStruct((t, LANES), F32),
                   jax.ShapeDtypeStruct((t, LANES), F32), jax.ShapeDtypeStruct((1, D_MODEL), F32)],
        compiler_params=_params(("arbitrary",)),
    )(dy, hid, h, ma, mb, w_down_t, w_up_t, w_out_t, g2)


def _wgrad(a, b, name, bm, bn, tk, out_dtype=F32, col_blocks=False):
    t, m = a.shape
    n = b.shape[1]
    bm, bn = min(bm, m), min(bn, n)
    nk = t // tk

    def body(a_ref, b_ref, o_ref, acc):
        @pl.when(pl.program_id(2) == 0)
        def _():
            acc[...] = jnp.zeros_like(acc)

        acc[...] += _tn(a_ref[...], b_ref[...])

        @pl.when(pl.program_id(2) == nk - 1)
        def _():
            o_ref[...] = acc[...].astype(out_dtype)

    if col_blocks:
        out_spec = pl.BlockSpec((None, bm, bn), lambda i, j, k: (j, i, 0))
        out_shape = jax.ShapeDtypeStruct((n // bn, m, bn), out_dtype)
    else:
        out_spec = pl.BlockSpec((bm, bn), lambda i, j, k: (i, j))
        out_shape = jax.ShapeDtypeStruct((m, n), out_dtype)
    return pl.pallas_call(
        body, name=name, grid=(m // bm, n // bn, nk),
        in_specs=[pl.BlockSpec((tk, bm), lambda i, j, k: (k, i)), pl.BlockSpec((tk, bn), lambda i, j, k: (k, j))],
        out_specs=out_spec, out_shape=out_shape, scratch_shapes=[pltpu.VMEM((bm, bn), F32)],
        compiler_params=_params(("arbitrary", "arbitrary", "arbitrary")),
    )(a, b)


def _proj_bwd(raw, dqa, dkae, dvae, dqb, dkb, dvb, dfl, x2, dh, w_main_t, w_f_t, g1, gqa, gka, gqb, gkb, tm):
    t = x2.shape[0]

    def body(raw_ref, dqa_ref, dkae_ref, dvae_ref, dqb_ref, dkb_ref, dvb_ref, dfl_ref, x_ref, dh_ref,
             wmt_ref, wft_ref, g1_ref, gqa_ref, gka_ref, gqb_ref, gkb_ref,
             dx_ref, dp_ref, dfb_ref, ggqa_ref, ggka_ref, ggqb_ref, ggkb_ref, gg1_ref):
        @pl.when(pl.program_id(0) == 0)
        def _():
            for r in (ggqa_ref, ggka_ref, ggqb_ref, ggkb_ref, gg1_ref):
                r[...] = jnp.zeros_like(r)

        raw = raw_ref[...]
        d_qa, p_qa = _head_norm_bwd(raw[:, 0:512], gqa_ref[...], dqa_ref[...])
        d_ka, p_ka = _head_norm_bwd(raw[:, 512:640], gka_ref[...], _fold_kv(dkae_ref[...]))
        d_va = _fold_kv(dvae_ref[...])
        d_qb, p_qb = _head_norm_bwd(raw[:, 768:1280], gqb_ref[...], dqb_ref[...])
        d_kb, p_kb = _head_norm_bwd(raw[:, 1280:1792], gkb_ref[...], dkb_ref[...])
        ggqa_ref[...] += jnp.sum(p_qa, axis=0, keepdims=True)
        ggka_ref[...] += jnp.sum(p_ka, axis=0, keepdims=True)
        ggqb_ref[...] += jnp.sum(p_qb, axis=0, keepdims=True)
        ggkb_ref[...] += jnp.sum(p_kb, axis=0, keepdims=True)
        dproj = jnp.concatenate([d_qa, d_ka, d_va, d_qb, d_kb, dvb_ref[...]], axis=1).astype(BF16)
        dp_ref[...] = dproj
        dfb = dfl_ref[...].astype(BF16)
        dfb_ref[...] = dfb
        d_xn = (jnp.dot(dproj, wmt_ref[...], preferred_element_type=F32)
                + jnp.dot(dfb, wft_ref[...], preferred_element_type=F32))
        x = x_ref[...]
        r = lax.rsqrt(jnp.mean(x * x, axis=-1, keepdims=True) + EPS)
        hat = x * r
        gd = d_xn * g1_ref[...]
        dx_ref[...] = dh_ref[...] + r * (gd - hat * jnp.mean(gd * hat, axis=-1, keepdims=True))
        gg1_ref[...] += jnp.sum(d_xn * hat, axis=0, keepdims=True)

    def tile(w):
        return pl.BlockSpec((tm, w), lambda i: (i, 0))

    def acc(w):
        return pl.BlockSpec((1, w), lambda i: (0, 0))

    return pl.pallas_call(
        body, name="proj_bwd", grid=(t // tm,),
        in_specs=[tile(MAIN_W), tile(512), tile(512), tile(512), tile(512), tile(512), tile(512), tile(LANES),
                  tile(D_MODEL), tile(D_MODEL), _const_spec((MAIN_W, D_MODEL)), _const_spec((LANES, D_MODEL)),
                  _const_spec((1, D_MODEL)), _const_spec((1, 512)), _const_spec((1, 128)), _const_spec((1, 512)),
                  _const_spec((1, 512))],
        out_specs=[tile(D_MODEL), tile(MAIN_W), tile(LANES), acc(512), acc(128), acc(512), acc(512), acc(D_MODEL)],
        out_shape=[jax.ShapeDtypeStruct((t, D_MODEL), F32), jax.ShapeDtypeStruct((t, MAIN_W), BF16),
                   jax.ShapeDtypeStruct((t, LANES), BF16), jax.ShapeDtypeStruct((1, 512), F32),
                   jax.ShapeDtypeStruct((1, 128), F32), jax.ShapeDtypeStruct((1, 512), F32),
                   jax.ShapeDtypeStruct((1, 512), F32), jax.ShapeDtypeStruct((1, D_MODEL), F32)],
        compiler_params=_params(("arbitrary",)),
    )(raw, dqa, dkae, dvae, dqb, dkb, dvb, dfl, x2, dh, w_main_t, w_f_t, g1, gqa, gka, gqb, gkb)


def _pair_rows(a, nb, s):
    two = jnp.stack([a[:, :, 0], a[:, :, 64]], axis=1)
    return jnp.transpose(two.reshape(4, 2, nb, s), (2, 0, 1, 3))


def _head_rows(a, nb, s):
    return jnp.transpose(a[:, 0:8].reshape(nb, s, 4, 2), (0, 2, 3, 1))


R_REST = 128 + 512 + 512
IN_PAD = 304


def _local_step(x, tgt, w_in_t, rest, g1, b_forget, qna, kna, sinks, qnb, knb, g2,
                tm=256, bt=512, tq=512, ts=256, wk=512, distributed=False):
    nb, s, _ = x.shape
    t = nb * s
    x2, tgt2 = x.reshape(t, D_MODEL), tgt.reshape(t, D_MODEL)
    g1r, g2r = g1.reshape(1, D_MODEL), g2.reshape(1, D_MODEL)
    gqa, gka = jnp.tile(qna, 8).reshape(1, 512), jnp.tile(kna, 2).reshape(1, 128)
    gqb, gkb = jnp.tile(qnb, 8).reshape(1, 512), jnp.tile(knb, 8).reshape(1, 512)
    bf_row = jnp.pad(b_forget, (0, LANES - 8)).reshape(1, LANES)
    sink_row = jnp.pad(sinks, (0, LANES - 8)).reshape(1, LANES)
    w_main_t = w_in_t[0:MAIN_W]
    w_f_t = jnp.pad(w_in_t[MAIN_W:IN_W], ((0, LANES - 8), (0, 0)))
    w_main, w_f = w_main_t.T, w_f_t.T

    xn, raw, fl, qa, kae, vae, qb, kb, vb = _norm_proj(x2, g1r, w_main, w_f, gqa, gka, gqb, gkb, tm)
    c_col = _gate_cumsum(fl, bf_row, nb, s, ts)
    c_rows = _head_rows(c_col, nb, s)
    ma, lse_a = _swa_fwd(qa, kae, vae, sink_row, nb, s, tq)
    if distributed:
        mb, lse_b, full = _fox_fwd(qb, kb, vb, c_col, c_rows, nb, s, bt, shard=rest)
        w_out = full[:, 0:128].reshape(D_MODEL, D_MODEL)
        w_up = jnp.transpose(full[:, 128:640].reshape(N_DEV, D_MODEL, 512), (1, 0, 2)).reshape(D_MODEL, D_FF)
        w_down = full[:, 640:R_REST].reshape(D_FF, D_MODEL)
    else:
        mb, lse_b = _fox_fwd(qb, kb, vb, c_col, c_rows, nb, s, bt)
        w_out, w_up, w_down = rest
    h, hn, hid, dy, dyb, loss_acc = _mlp_fwd(x2, ma, mb, tgt2, w_out[0:512], w_out[512:1024], g2r, w_up, w_down, tm)

    du, dh, dhb, dma, dmb, dla, dlb, gg2 = _mlp_bwd(dy, hid, h, ma, mb, w_down.T, w_up.T, w_out.T, g2r, tm)
    g_down = _wgrad(hid, dyb, "wgrad_down", 512, 1024, wk, BF16).reshape(N_DEV, 512, D_MODEL)
    g_up = _wgrad(hn, du, "wgrad_up", 1024, 512, wk, BF16, col_blocks=True)
    g_out = jnp.concatenate([_wgrad(ma, dhb, "wgrad_out_a", 512, 1024, wk, BF16),
                             _wgrad(mb, dhb, "wgrad_out_b", 512, 1024, wk, BF16)], axis=0).reshape(N_DEV, 128, D_MODEL)

    dqa, dkae, dvae, dsink = _swa_bwd(qa, kae, vae, dma, sink_row, _pair_rows(lse_a, nb, s), _head_rows(dla, nb, s), nb, s, tq)
    fox = _fox_bwd(qb, kb, vb, dmb, c_col, c_rows, _pair_rows(lse_b, nb, s), _head_rows(dlb, nb, s), nb, s, bt,
                   exch=(g_out, g_up, g_down) if distributed else ())
    dqb, dkb, dvb, dc4, dcq = fox[:5]
    if distributed:
        g_out, g_up, g_down = fox[5:]
    dc_k = jnp.stack([dc4[:, :, 0], dc4[:, :, 64]], axis=-1)
    dc_k = jnp.pad(jnp.transpose(dc_k, (1, 0, 2)).reshape(t, 8), ((0, 0), (0, LANES - 8)))
    dc_q = jnp.pad(jnp.transpose(dcq, (0, 3, 1, 2)).reshape(t, 8), ((0, 0), (0, LANES - 8)))
    dfl, gbf = _gate_cumsum_bwd(dc_k, dc_q, fl, bf_row, nb, s, ts)
    grad_x, dproj, dfb, ggqa, ggka, ggqb, ggkb, gg1 = _proj_bwd(
        raw, dqa, dkae, dvae, dqb, dkb, dvb, dfl, x2, dh, w_main_t, w_f_t, g1r, gqa, gka, gqb, gkb, tm)
    g_in_t = jnp.concatenate([_wgrad(dproj, xn, "wgrad_in", 768, 1024, wk), _wgrad(dfb, xn, "wgrad_gate", 128, 1024, wk)[0:8]],
                             axis=0)

    small = (gg1.reshape(D_MODEL), gbf[0, 0:8], ggqa.reshape(8, 64).sum(0), ggka.reshape(2, 64).sum(0),
             dsink.sum(0)[:, 0:2, 0].reshape(8), ggqb.reshape(8, 64).sum(0), ggkb.reshape(8, 64).sum(0),
             gg2.reshape(D_MODEL))
    return loss_acc[0, 0], grad_x.reshape(nb, s, D_MODEL), g_in_t, g_out, g_up, g_down, small


def _all_gather(shard):
    def body(x_ref, out_ref, send_sems, recv_sems, local_sem):
        start, forward, finish = _gather_steps(x_ref, out_ref, send_sems, recv_sems, local_sem)
        start()
        forward()
        finish()

    return pl.pallas_call(
        body, name="gather_w_in", out_shape=jax.ShapeDtypeStruct((N_DEV,) + shard.shape, shard.dtype),
        in_specs=[ANY], out_specs=ANY,
        scratch_shapes=[pltpu.SemaphoreType.DMA((N_SEM,)), pltpu.SemaphoreType.DMA((N_SEM,)), pltpu.SemaphoreType.DMA(())],
    )(shard)


def _exchange(*arrays):
    n_ex = len(arrays)

    def body(*refs):
        start, finish = _exchange_steps(list(zip(refs[:n_ex], refs[n_ex:2 * n_ex])), *refs[2 * n_ex:])
        start()
        finish()

    return pl.pallas_call(
        body, name="exchange_tail", out_shape=[jax.ShapeDtypeStruct(a.shape, a.dtype) for a in arrays],
        in_specs=[ANY] * n_ex, out_specs=[ANY] * n_ex,
        scratch_shapes=[pltpu.SemaphoreType.DMA((N_SEM * n_ex,)), pltpu.SemaphoreType.DMA((N_SEM * n_ex,)),
                        pltpu.SemaphoreType.DMA((n_ex,))],
    )(*arrays)


def _sum_adamw(recv, w, m, v, tr, name):
    _, r, n = recv.shape

    def body(r_ref, w_ref, m_ref, v_ref, g_ref, d_ref, nm_ref, nv_ref):
        g = r_ref[0].astype(F32)
        for s in range(1, N_DEV):
            g = g + r_ref[s].astype(F32)
        g_ref[...] = g
        nm = ADAM_B1 * m_ref[...] + (1.0 - ADAM_B1) * g
        nv = ADAM_B2 * v_ref[...] + (1.0 - ADAM_B2) * (g * g)
        m_hat = nm / (1.0 - ADAM_B1 ** ADAM_STEP)
        v_hat = nv / (1.0 - ADAM_B2 ** ADAM_STEP)
        d_ref[...] = -ADAM_LR * (m_hat / (jnp.sqrt(v_hat) + ADAM_EPS) + ADAM_WD * w_ref[...])
        nm_ref[...] = nm
        nv_ref[...] = nv

    tile = pl.BlockSpec((tr, n), lambda i: (i, 0))
    shp = jax.ShapeDtypeStruct((r, n), F32)
    return pl.pallas_call(
        body, name=name, grid=(r // tr,),
        in_specs=[pl.BlockSpec((N_DEV, tr, n), lambda i: (0, i, 0)), tile, tile, tile],
        out_specs=[tile, tile, tile, tile], out_shape=[shp, shp, shp, shp],
        compiler_params=_params(("arbitrary",)),
    )(recv, w, m, v)


def _small_rows(g1, bf, qna, kna, sk, qnb, knb, g2):
    row2 = jnp.concatenate([bf, qna, kna, sk, qnb, knb])
    return jnp.zeros((8, D_MODEL), F32).at[0].set(g1).at[1].set(g2).at[2, 0:row2.shape[0]].set(row2)


def _in_rows(w_in_s):
    return jnp.pad(w_in_s.T, ((0, IN_PAD - IN_SHARD), (0, 0)))


def kernel(x, attn_norm_g, w_in, b_forget, q_norm_a, k_norm_a, sink_logits, q_norm_b, k_norm_b, w_out, mlp_norm_g, w_up, w_down, loss_target, m_attn_norm_g, m_w_in, m_b_forget, m_q_norm_a, m_k_norm_a, m_sink_logits, m_q_norm_b, m_k_norm_b, m_w_out, m_mlp_norm_g, m_w_up, m_w_down, v_attn_norm_g, v_w_in, v_b_forget, v_q_norm_a, v_k_norm_a, v_sink_logits, v_q_norm_b, v_k_norm_b, v_w_out, v_mlp_norm_g, v_w_up, v_w_down):
    w_in_r = _in_rows(w_in)
    w_in_t = _all_gather(w_in_r.astype(BF16))[:, 0:IN_SHARD].reshape(IN_W, D_MODEL)
    rest = jnp.concatenate([w_out, w_up.reshape(512, D_MODEL), w_down], axis=0).astype(BF16)

    loss_part, grad_x, g_in_t, r_out, r_up, r_down, small = _local_step(
        x, loss_target, w_in_t, rest, attn_norm_g, b_forget, q_norm_a, k_norm_a, sink_logits, q_norm_b, k_norm_b, mlp_norm_g,
        distributed=True)

    g_in_blocks = jnp.pad(g_in_t.reshape(N_DEV, IN_SHARD, D_MODEL), ((0, 0), (0, IN_PAD - IN_SHARD), (0, 0))).astype(BF16)
    small_blocks = jnp.broadcast_to(_small_rows(*small), (N_DEV, 8, D_MODEL))
    r_in, r_small = _exchange(g_in_blocks, small_blocks)

    small_w = _small_rows(attn_norm_g, b_forget, q_norm_a, k_norm_a, sink_logits, q_norm_b, k_norm_b, mlp_norm_g)
    small_m = _small_rows(m_attn_norm_g, m_b_forget, m_q_norm_a, m_k_norm_a, m_sink_logits, m_q_norm_b, m_k_norm_b, m_mlp_norm_g)
    small_v = _small_rows(v_attn_norm_g, v_b_forget, v_q_norm_a, v_k_norm_a, v_sink_logits, v_q_norm_b, v_k_norm_b, v_mlp_norm_g)
    o_in = [a[0:IN_SHARD].T for a in _sum_adamw(r_in, w_in_r, _in_rows(m_w_in), _in_rows(v_w_in), IN_PAD, "adamw_in")]
    o_out = _sum_adamw(r_out, w_out, m_w_out, v_w_out, 128, "adamw_out")
    o_up = _sum_adamw(r_up, w_up, m_w_up, v_w_up, 256, "adamw_up")
    o_down = _sum_adamw(r_down, w_down, m_w_down, v_w_down, 128, "adamw_down")
    o_small = _sum_adamw(r_small, small_w, small_m, small_v, 8, "adamw_small")

    def leaves(i):
        row2 = o_small[i][2]
        return (o_small[i][0], o_in[i], row2[0:8], row2[8:72], row2[72:136], row2[136:144], row2[144:208], row2[208:272],
                o_out[i], o_small[i][1], o_up[i], o_down[i])

    loss = lax.psum(loss_part, ("x", "y", "c"))
    return (loss, grad_x, *leaves(0), *leaves(1), *leaves(2), *leaves(3))
```

```python
import functools
import math

import jax
import jax.numpy as jnp
from jax import lax
from jax.experimental import pallas as pl
from jax.experimental.pallas import tpu as pltpu

F32 = jnp.float32
BF16 = jnp.bfloat16

D_MODEL = 1024
HEAD_DIM = 64
N_DEV = 8
D_FF = 4096
A_QW = 512
A_KVW = 128
B_W = 512
MAIN_W = 2304
IN_W = 2312
WINDOW = 128
EPS = 1e-6
SCALE = 0.125
LANES = 128
NEG_INF = float("-inf")

ADAM_LR = 0.001
ADAM_B1 = 0.9
ADAM_B2 = 0.999
ADAM_EPS = 1e-08
ADAM_WD = 0.01
ADAM_STEP = 10

R_OUT, R_UP, R_DOWN, R_IN = 0, 128, 640, 1152
IN_SHARD = 289
R_SMALL = 1456
R_PACK = 1472
VMEM_LIMIT = 56 * 1024 * 1024


def _params(sem, vmem=VMEM_LIMIT):
    return pltpu.CompilerParams(dimension_semantics=sem, vmem_limit_bytes=vmem)


def _const_spec(shape):
    nd = len(shape)
    return pl.BlockSpec(shape, lambda *_: (0,) * nd, pipeline_mode=pl.Buffered(1))


def _lane(shape):
    return lax.broadcasted_iota(jnp.int32, shape, len(shape) - 1)


def _split_dot(v, mat):
    hi = v.astype(BF16)
    lo = (v - hi.astype(F32)).astype(BF16)
    return (jnp.dot(hi, mat, preferred_element_type=F32) + jnp.dot(lo, mat, preferred_element_type=F32))


def _head_ones(n):
    r = lax.shift_right_logical(lax.broadcasted_iota(jnp.int32, (n, n), 0), 6)
    c = lax.shift_right_logical(lax.broadcasted_iota(jnp.int32, (n, n), 1), 6)
    return (r == c).astype(BF16)


def _head_sum(v):
    w = v.shape[1]
    if w <= 256:
        return _split_dot(v, _head_ones(w))
    ones = _head_ones(256)
    return jnp.concatenate([_split_dot(v[:, s:s + 256], ones) for s in range(0, w, 256)], axis=1)


def _head_norm(seg, gain):
    rs = lax.rsqrt(_head_sum(seg * seg) * (1.0 / HEAD_DIM) + EPS)
    return seg * rs * gain


def _head_norm_bwd(seg, gain, d_out):
    rs = lax.rsqrt(_head_sum(seg * seg) * (1.0 / HEAD_DIM) + EPS)
    hat = seg * rs
    gd = d_out * gain
    d_seg = rs * (gd - hat * (_head_sum(gd * hat) * (1.0 / HEAD_DIM)))
    return d_seg, d_out * hat


def _expand_kv(v):
    r = pltpu.roll(v, 64, axis=1)
    lo = _lane(v.shape) < 64
    return jnp.concatenate([jnp.where(lo, v, r), jnp.where(lo, r, v)], axis=1)


def _fold_kv(e4):
    t0 = e4[:, 0:128] + e4[:, 128:256]
    t1 = e4[:, 256:384] + e4[:, 384:512]
    t0 = t0 + pltpu.roll(t0, 64, axis=1)
    t1 = t1 + pltpu.roll(t1, 64, axis=1)
    return jnp.where(_lane(t0.shape) < 64, t0, t1)


def _pick_lane(blk, idx):
    return jnp.sum(jnp.where(_lane(blk.shape) == idx, blk, 0.0), axis=1, keepdims=True)


def _nt(a, b):
    return lax.dot_general(a, b, (((1,), (1,)), ((), ())), preferred_element_type=F32)


def _tn(a, b):
    return lax.dot_general(a, b, (((0,), (0,)), ((), ())), preferred_element_type=F32)


def _norm_proj(x2, g1, w_main, w_f, gqa, gka, gqb, gkb, tm):
    t = x2.shape[0]

    def body(x_ref, g1_ref, wm_ref, wf_ref, gqa_ref, gka_ref, gqb_ref, gkb_ref,
             xn_ref, raw_ref, fl_ref, qa_ref, kae_ref, vae_ref, qb_ref, kb_ref, vb_ref):
        x = x_ref[...]
        r = lax.rsqrt(jnp.mean(x * x, axis=-1, keepdims=True) + EPS)
        xn = (x * r * g1_ref[...]).astype(BF16)
        xn_ref[...] = xn
        proj = jnp.dot(xn, wm_ref[...], preferred_element_type=F32)
        raw_ref[...] = proj
        fl_ref[...] = jnp.dot(xn, wf_ref[...], preferred_element_type=F32)
        qa_ref[...] = _head_norm(proj[:, 0:512], gqa_ref[...]).astype(BF16)
        kae_ref[...] = _expand_kv(_head_norm(proj[:, 512:640], gka_ref[...])).astype(BF16)
        vae_ref[...] = _expand_kv(proj[:, 640:768]).astype(BF16)
        qb_ref[...] = _head_norm(proj[:, 768:1280], gqb_ref[...]).astype(BF16)
        kb_ref[...] = _head_norm(proj[:, 1280:1792], gkb_ref[...]).astype(BF16)
        vb_ref[...] = proj[:, 1792:2304].astype(BF16)

    def tile(w):
        return pl.BlockSpec((tm, w), lambda i: (i, 0))

    return pl.pallas_call(
        body, name="norm_proj", grid=(t // tm,),
        in_specs=[tile(D_MODEL), _const_spec((1, D_MODEL)), _const_spec((D_MODEL, MAIN_W)), _const_spec((D_MODEL, LANES)),
                  _const_spec((1, 512)), _const_spec((1, 128)), _const_spec((1, 512)), _const_spec((1, 512))],
        out_specs=[tile(D_MODEL), tile(MAIN_W), tile(LANES), tile(512), tile(256), tile(256), tile(512), tile(512), tile(512)],
        out_shape=[jax.ShapeDtypeStruct((t, D_MODEL), BF16), jax.ShapeDtypeStruct((t, MAIN_W), F32),
                   jax.ShapeDtypeStruct((t, LANES), F32), jax.ShapeDtypeStruct((t, 512), BF16),
                   jax.ShapeDtypeStruct((t, 256), BF16), jax.ShapeDtypeStruct((t, 256), BF16),
                   jax.ShapeDtypeStruct((t, 512), BF16), jax.ShapeDtypeStruct((t, 512), BF16),
                   jax.ShapeDtypeStruct((t, 512), BF16)],
        compiler_params=_params(("arbitrary",)),
    )(x2, g1, w_main, w_f, gqa, gka, gqb, gkb)


def _tri(n, upper):
    r = lax.broadcasted_iota(jnp.int32, (n, n), 0)
    c = lax.broadcasted_iota(jnp.int32, (n, n), 1)
    return ((c >= r) if upper else (c <= r)).astype(F32)


def _gate_cumsum(fl, bf_row, nb, s, ts):
    t = fl.shape[0]
    nt = s // ts

    def body(fl_ref, b_ref, c_ref, carry):
        @pl.when(pl.program_id(1) == 0)
        def _():
            carry[...] = jnp.zeros_like(carry)

        z = fl_ref[...] + b_ref[...]
        e = jnp.exp(-jnp.abs(z))
        u = 1.0 + e
        log1p = jnp.where(u == 1.0, e, jnp.log(u) * (e / (u - 1.0)))
        lf = jnp.minimum(z, 0.0) - log1p
        c_ref[...] = jnp.dot(_tri(ts, False), lf, precision=lax.Precision.HIGHEST, preferred_element_type=F32) + carry[...]
        carry[...] = c_ref[pl.ds(ts - 1, 1), :]

    return pl.pallas_call(
        body, name="gate_cumsum", grid=(nb, nt),
        in_specs=[pl.BlockSpec((ts, LANES), lambda b, i: (b * nt + i, 0)), _const_spec((1, LANES))],
        out_specs=pl.BlockSpec((ts, LANES), lambda b, i: (b * nt + i, 0)),
        out_shape=jax.ShapeDtypeStruct((t, LANES), F32),
        scratch_shapes=[pltpu.VMEM((1, LANES), F32)],
        compiler_params=_params(("arbitrary", "arbitrary")),
    )(fl, bf_row)


def _gate_cumsum_bwd(dc_k, dc_q, fl, bf_row, nb, s, ts):
    t = fl.shape[0]
    nt = s // ts

    def body(dck_ref, dcq_ref, fl_ref, b_ref, df_ref, gb_ref, carry, dlf_ref):
        @pl.when(pl.program_id(1) == 0)
        def _():
            carry[...] = jnp.zeros_like(carry)

        @pl.when((pl.program_id(0) == 0) & (pl.program_id(1) == 0))
        def _():
            gb_ref[...] = jnp.zeros_like(gb_ref)

        dlf_ref[...] = jnp.dot(_tri(ts, True), dck_ref[...] + dcq_ref[...], precision=lax.Precision.HIGHEST,
                               preferred_element_type=F32) + carry[...]
        carry[...] = dlf_ref[pl.ds(0, 1), :]
        dlf = dlf_ref[...]
        z = fl_ref[...] + b_ref[...]
        df = dlf * (1.0 / (1.0 + jnp.exp(z)))
        df_ref[...] = df
        gb_ref[...] += jnp.sum(df, axis=0, keepdims=True)

    def rev(b, i):
        return (b * nt + (nt - 1 - i), 0)

    return pl.pallas_call(
        body, name="gate_cumsum_bwd", grid=(nb, nt),
        in_specs=[pl.BlockSpec((ts, LANES), rev), pl.BlockSpec((ts, LANES), rev), pl.BlockSpec((ts, LANES), rev),
                  _const_spec((1, LANES))],
        out_specs=[pl.BlockSpec((ts, LANES), rev), pl.BlockSpec((1, LANES), lambda b, i: (0, 0))],
        out_shape=[jax.ShapeDtypeStruct((t, LANES), F32), jax.ShapeDtypeStruct((1, LANES), F32)],
        scratch_shapes=[pltpu.VMEM((1, LANES), F32), pltpu.VMEM((ts, LANES), F32)],
        compiler_params=_params(("arbitrary", "arbitrary")),
    )(dc_k, dc_q, fl, bf_row)


def _slope(p, hh):
    out = jnp.float32(2.0 ** -(2 * 3 + hh + 1))
    for pp in (2, 1, 0):
        out = jnp.where(p == pp, jnp.float32(2.0 ** -(2 * pp + hh + 1)), out)
    return out


def _swa_windows(ref, i, tq):
    nsub = tq // WINDOW
    cur = ref[pl.ds(pl.multiple_of(i * tq, tq), tq), :].reshape(nsub, WINDOW, LANES)
    first = ref[pl.ds(pl.multiple_of(jnp.maximum(i * tq - WINDOW, 0), WINDOW), WINDOW), :].reshape(1, WINDOW, LANES)
    return jnp.concatenate([jnp.concatenate([first, cur[0:nsub - 1]], axis=0), cur], axis=1)


def _both_heads(x3, lo):
    zero = jnp.zeros_like(x3)
    return jnp.concatenate([jnp.where(lo, x3, zero), jnp.where(lo, zero, x3)], axis=0)


def _swa_head_consts(sink_ref, p, i, nsub):
    bidx = lax.broadcasted_iota(jnp.int32, (2 * nsub, 1, 1), 0)
    is_a = bidx < nsub
    slope = jnp.where(is_a, _slope(p, 0), _slope(p, 1))
    sinks = sink_ref[...]
    sink = jnp.where(is_a, _pick_lane(sinks, 2 * p).reshape(1, 1, 1), _pick_lane(sinks, 2 * p + 1).reshape(1, 1, 1))
    first = (i == 0) & ((bidx == 0) | (bidx == nsub))
    return slope, sink, first


def _swa_fwd(qa, kae, vae, sink_row, nb, s, tq):
    t = qa.shape[0]
    nq = s // tq
    nsub = tq // WINDOW

    def body(q_ref, k_ref, v_ref, sink_ref, o_ref, lse_ref):
        p, i = pl.program_id(1), pl.program_id(2)
        lo = _lane((1, 1, LANES)) < 64
        kk, vv = _swa_windows(k_ref, i, tq), _swa_windows(v_ref, i, tq)
        qs = (q_ref[...].astype(F32) * SCALE).astype(BF16).reshape(nsub, WINDOW, LANES)
        q8 = _both_heads(qs, lo)
        s8 = jnp.einsum("bqd,bkd->bqk", q8, jnp.concatenate([kk, kk], axis=0), preferred_element_type=F32)
        row = lax.broadcasted_iota(jnp.int32, (1, WINDOW, 2 * WINDOW), 1)
        col = lax.broadcasted_iota(jnp.int32, (1, WINDOW, 2 * WINDOW), 2)
        dist = row + WINDOW - col
        slope, sink, first = _swa_head_consts(sink_ref, p, i, nsub)
        valid = (dist >= 0) & (dist < WINDOW) & ((col >= WINDOW) | jnp.logical_not(first))
        s8 = jnp.where(valid, s8 - slope * dist.astype(F32), NEG_INF)
        m = jnp.maximum(jnp.max(s8, axis=2, keepdims=True), sink)
        e = jnp.exp(s8 - m)
        den = jnp.sum(e, axis=2, keepdims=True) + jnp.exp(sink - m)
        pr = (e / den).astype(BF16)
        o8 = jnp.einsum("bqk,bkd->bqd", pr, jnp.concatenate([vv, vv], axis=0), preferred_element_type=F32)
        lse8 = m + jnp.log(den)
        o_ref[...] = jnp.where(lo, o8[0:nsub], o8[nsub:]).astype(BF16).reshape(tq, LANES)
        lse_ref[...] = jnp.where(lo, lse8[0:nsub], lse8[nsub:]).reshape(tq, LANES)

    return pl.pallas_call(
        body, name="swa_fwd", grid=(nb, 4, nq),
        in_specs=[pl.BlockSpec((tq, LANES), lambda b, p, i: (b * nq + i, p)),
                  pl.BlockSpec((s, LANES), lambda b, p, i: (b, lax.shift_right_logical(p, 1))),
                  pl.BlockSpec((s, LANES), lambda b, p, i: (b, lax.shift_right_logical(p, 1))),
                  pl.BlockSpec((1, LANES), lambda b, p, i: (0, 0))],
        out_specs=[pl.BlockSpec((tq, LANES), lambda b, p, i: (b * nq + i, p)),
                   pl.BlockSpec((None, tq, LANES), lambda b, p, i: (p, b * nq + i, 0))],
        out_shape=[jax.ShapeDtypeStruct((t, 512), BF16), jax.ShapeDtypeStruct((4, t, LANES), F32)],
        compiler_params=_params(("arbitrary", "arbitrary", "arbitrary")),
    )(qa, kae, vae, sink_row)


def _swa_bwd(qa, kae, vae, do_a, sink_row, lse_rows, delta_rows, nb, s, tq):
    t = qa.shape[0]
    nq = s // tq
    nsub = tq // WINDOW

    def body(q_ref, do_ref, k_ref, v_ref, sink_ref, lse_ref, dl_ref, dq_ref, dk_ref, dv_ref, ds_ref):
        p, i = pl.program_id(1), pl.program_id(2)

        @pl.when(i == 0)
        def _():
            ds_ref[...] = jnp.zeros_like(ds_ref)

        lo = _lane((1, 1, LANES)) < 64
        kk, vv = _swa_windows(k_ref, i, tq), _swa_windows(v_ref, i, tq)
        kks = (kk.astype(F32) * SCALE).astype(BF16)
        k8, v8 = jnp.concatenate([kks, kks], axis=0), jnp.concatenate([vv, vv], axis=0)
        q8 = _both_heads(q_ref[...].reshape(nsub, WINDOW, LANES), lo)
        do8 = _both_heads(do_ref[...].reshape(nsub, WINDOW, LANES), lo)
        cur = pl.multiple_of(i * tq, tq)

        def stat(ref):
            return jnp.concatenate([ref[pl.ds(hh, 1), pl.ds(cur + u * WINDOW, WINDOW)].reshape(1, 1, WINDOW)
                                    for hh in range(2) for u in range(nsub)], axis=0)

        lse8, dl8 = stat(lse_ref), stat(dl_ref)
        row = lax.broadcasted_iota(jnp.int32, (1, 2 * WINDOW, WINDOW), 1)
        col = lax.broadcasted_iota(jnp.int32, (1, 2 * WINDOW, WINDOW), 2)
        dist = col + WINDOW - row
        slope, sink, first = _swa_head_consts(sink_ref, p, i, nsub)
        valid = (dist >= 0) & (dist < WINDOW) & ((row >= WINDOW) | jnp.logical_not(first))
        st = jnp.einsum("bkd,bqd->bkq", k8, q8, preferred_element_type=F32) - slope * dist.astype(F32) - lse8
        pt = jnp.where(valid, jnp.exp(jnp.where(valid, st, 0.0)), 0.0)
        dpt = jnp.einsum("bkd,bqd->bkq", v8, do8, preferred_element_type=F32)
        dst = pt * (dpt - dl8)
        ptb, dstb = pt.astype(BF16), dst.astype(BF16)
        dv8 = jnp.einsum("bkq,bqd->bkd", ptb, do8, preferred_element_type=F32)
        dk8 = jnp.einsum("bkq,bqd->bkd", dstb, q8, preferred_element_type=F32) * SCALE
        dq8 = jnp.einsum("bkq,bkd->bqd", dstb, k8, preferred_element_type=F32)
        dq_ref[...] = jnp.where(lo, dq8[0:nsub], dq8[nsub:]).reshape(tq, LANES)

        psd = jnp.exp(sink - lse8) * dl8
        row_h = lax.broadcasted_iota(jnp.int32, (8, LANES), 0)
        for hh in range(2):
            tot = jnp.sum(jnp.sum(psd[hh * nsub:(hh + 1) * nsub], axis=2, keepdims=True), axis=0, keepdims=True)
            ds_ref[...] += jnp.where(row_h == hh, -tot.reshape(1, 1), 0.0)

        prev = pl.multiple_of(jnp.maximum(i * tq - WINDOW, 0), WINDOW)
        for g8, g_ref in ((dk8, dk_ref), (dv8, dv_ref)):
            g4 = g8[0:nsub] + g8[nsub:]
            own, before = g4[:, WINDOW:, :], g4[:, 0:WINDOW, :]
            shifted = jnp.concatenate([before[1:nsub], jnp.zeros((1, WINDOW, LANES), F32)], axis=0)
            g_ref[pl.ds(cur, tq), :] = (own + shifted).reshape(tq, LANES)
            g_ref[pl.ds(prev, WINDOW), :] += before[0]

    rows = pl.BlockSpec((None, None, 2, s), lambda b, p, i: (b, p, 0, 0))
    return pl.pallas_call(
        body, name="swa_bwd", grid=(nb, 4, nq),
        in_specs=[pl.BlockSpec((tq, LANES), lambda b, p, i: (b * nq + i, p)),
                  pl.BlockSpec((tq, LANES), lambda b, p, i: (b * nq + i, p)),
                  pl.BlockSpec((s, LANES), lambda b, p, i: (b, lax.shift_right_logical(p, 1))),
                  pl.BlockSpec((s, LANES), lambda b, p, i: (b, lax.shift_right_logical(p, 1))),
                  pl.BlockSpec((1, LANES), lambda b, p, i: (0, 0)), rows, rows],
        out_specs=[pl.BlockSpec((tq, LANES), lambda b, p, i: (b * nq + i, p)),
                   pl.BlockSpec((s, LANES), lambda b, p, i: (b, p)),
                   pl.BlockSpec((s, LANES), lambda b, p, i: (b, p)),
                   pl.BlockSpec((None, None, 8, LANES), lambda b, p, i: (b, p, 0, 0))],
        out_shape=[jax.ShapeDtypeStruct((t, 512), F32), jax.ShapeDtypeStruct((t, 512), F32),
                   jax.ShapeDtypeStruct((t, 512), F32), jax.ShapeDtypeStruct((nb, 4, 8, LANES), F32)],
        compiler_params=_params(("arbitrary", "arbitrary", "arbitrary")),
    )(qa, do_a, kae, vae, sink_row, lse_rows, delta_rows)


MESH = pl.DeviceIdType.MESH
ANY = pl.BlockSpec(memory_space=pl.ANY)
N_SEM = 7


def _gather_steps(x_ref, out_ref, send_sems, recv_sems, local_sem):
    x, y, c = lax.axis_index("x"), lax.axis_index("y"), lax.axis_index("c")
    me, sibling = (x, y, c), (x, y, 1 - c)
    chips = [(1 - x, y), (x, 1 - y), (1 - x, 1 - y)]

    def slot(px, py, pc):
        return out_ref.at[4 * px + 2 * py + pc]

    def copy(k, block, to, src=None):
        return pltpu.make_async_remote_copy(
            src_ref=slot(*block) if src is None else src, dst_ref=slot(*block),
            send_sem=send_sems.at[k], recv_sem=recv_sems.at[k], device_id=to, device_id_type=MESH)

    mine = pltpu.make_async_copy(x_ref, slot(*me), local_sem)
    first = [copy(0, me, sibling, src=x_ref)] + [copy(1 + j, me, (*chip, c), src=x_ref) for j, chip in enumerate(chips)]
    passed = [copy(4 + j, (*chip, c), sibling) for j, chip in enumerate(chips)]

    def start():
        mine.start()
        for cp in first:
            cp.start()

    def forward():
        for j, chip in enumerate(chips):
            copy(1 + j, (*chip, c), me).wait_recv()
            passed[j].start()

    def finish():
        copy(0, sibling, me).wait_recv()
        for j, chip in enumerate(chips):
            copy(4 + j, (*chip, 1 - c), me).wait_recv()
        for cp in first + passed:
            cp.wait_send()
        mine.wait()

    return start, forward, finish


def _exchange_steps(pairs, send_sems, recv_sems, local_sems):
    x, y, c = lax.axis_index("x"), lax.axis_index("y"), lax.axis_index("c")
    my_id = 4 * x + 2 * y + c
    local, remote = [], []
    for a, (src, dst) in enumerate(pairs):
        local.append(pltpu.make_async_copy(src.at[my_id], dst.at[my_id], local_sems.at[a]))
        for k in range(1, N_DEV):
            px = 1 - x if k & 4 else x
            py = 1 - y if k & 2 else y
            pc = 1 - c if k & 1 else c
            remote.append(pltpu.make_async_remote_copy(
                src_ref=src.at[4 * px + 2 * py + pc], dst_ref=dst.at[my_id],
                send_sem=send_sems.at[N_SEM * a + k - 1], recv_sem=recv_sems.at[N_SEM * a + k - 1],
                device_id=(px, py, pc), device_id_type=MESH))

    def start():
        for cp in local + remote:
            cp.start()

    def finish():
        for cp in remote:
            cp.wait_recv()
        for cp in remote:
            cp.wait_send()
        for cp in local:
            cp.wait()

    return start, finish


def _fox_fwd(qb, kb, vb, c_col, c_rows, nb, s, bt, shard=None):
    t = qb.shape[0]
    nq = s // bt
    n_in = 5

    def body(*refs):
        q_ref, k_ref, v_ref, cc_ref, cr_ref = refs[:n_in]
        if shard is None:
            o_ref, lse_ref = refs[n_in:]
        else:
            x_ref, o_ref, lse_ref, full_ref, send_sems, recv_sems, local_sem = refs[n_in:]
            start, forward, finish = _gather_steps(x_ref, full_ref, send_sems, recv_sems, local_sem)
            step = (pl.program_id(0) * 4 + pl.program_id(1)) * nq + pl.program_id(2)
            pl.when(step == 0)(start)
            pl.when(step == nb * 2 * nq)(forward)
        j, i = pl.program_id(1), pl.program_id(2)
        lo = _lane((1, LANES)) < 64
        row = lax.broadcasted_iota(jnp.int32, (bt, bt), 0)
        col = lax.broadcasted_iota(jnp.int32, (bt, bt), 1)
        qs = (q_ref[...].astype(F32) * SCALE).astype(BF16)
        cc = cc_ref[...]
        outs, lses = [], []
        for hh in range(2):
            mask_h = lo if hh == 0 else jnp.logical_not(lo)
            qh = jnp.where(mask_h, qs, jnp.zeros_like(qs))
            cq = _pick_lane(cc, 2 * j + hh)

            def blk(kb_i, carry, diag):
                m, l, acc = carry
                start = pl.multiple_of(kb_i * bt, bt)
                sc = _nt(qh, k_ref[pl.ds(start, bt), :]) + (cq - cr_ref[pl.ds(hh, 1), pl.ds(start, bt)])
                if diag:
                    sc = jnp.where(row >= col, sc, NEG_INF)
                m_new = jnp.maximum(m, jnp.max(sc, axis=1, keepdims=True))
                alpha = jnp.exp(m - m_new)
                pr = jnp.exp(sc - m_new)
                l = alpha * l + jnp.sum(pr, axis=1, keepdims=True)
                acc = alpha * acc + jnp.dot(pr.astype(BF16), v_ref[pl.ds(start, bt), :], preferred_element_type=F32)
                return m_new, l, acc

            init = (jnp.full((bt, 1), NEG_INF, F32), jnp.zeros((bt, 1), F32), jnp.zeros((bt, LANES), F32))
            carry = lax.fori_loop(0, i, lambda kb_i, c: blk(kb_i, c, False), init)
            m, l, acc = blk(i, carry, True)
            outs.append(acc / l)
            lses.append(m + jnp.log(l))
        o_ref[...] = jnp.where(lo, outs[0], outs[1]).astype(BF16)
        lse_ref[...] = jnp.where(lo, lses[0], lses[1])
        if shard is not None:
            pl.when(step == nb * 4 * nq - 1)(finish)

    rows = pl.BlockSpec((None, None, 2, s), lambda b, j, i: (b, j, 0, 0))
    in_specs = [pl.BlockSpec((bt, LANES), lambda b, j, i: (b * nq + i, j)),
                pl.BlockSpec((s, LANES), lambda b, j, i: (b, j)),
                pl.BlockSpec((s, LANES), lambda b, j, i: (b, j)),
                pl.BlockSpec((bt, LANES), lambda b, j, i: (b * nq + i, 0)), rows]
    out_specs = [pl.BlockSpec((bt, LANES), lambda b, j, i: (b * nq + i, j)),
                 pl.BlockSpec((None, bt, LANES), lambda b, j, i: (j, b * nq + i, 0))]
    out_shape = [jax.ShapeDtypeStruct((t, 512), BF16), jax.ShapeDtypeStruct((4, t, LANES), F32)]
    args, scratch = [qb, kb, vb, c_col, c_rows], []
    if shard is not None:
        in_specs.append(ANY)
        out_specs.append(ANY)
        out_shape.append(jax.ShapeDtypeStruct((N_DEV,) + shard.shape, shard.dtype))
        args.append(shard)
        scratch = [pltpu.SemaphoreType.DMA((N_SEM,)), pltpu.SemaphoreType.DMA((N_SEM,)), pltpu.SemaphoreType.DMA(())]
    return pl.pallas_call(
        body, name="fox_fwd", grid=(nb, 4, nq), in_specs=in_specs, out_specs=out_specs, out_shape=out_shape,
        scratch_shapes=scratch, compiler_params=_params(("arbitrary", "arbitrary", "arbitrary")),
    )(*args)


def _fox_bwd(qb, kb, vb, do_b, c_col, c_rows, lse_rows, delta_rows, nb, s, bt, exch=()):
    t = qb.shape[0]
    nk = s // bt
    n_in, n_out, n_ex = 8, 5, len(exch)

    def body(*refs):
        q_ref, do_ref, k_ref, v_ref, cc_ref, cr_ref, lse_ref, dl_ref = refs[:n_in]
        dq_ref, dk_ref, dv_ref, dc_ref, dcq_ref = refs[n_in + n_ex:n_in + n_ex + n_out]
        if exch:
            srcs = refs[n_in:n_in + n_ex]
            dsts = refs[n_in + n_ex + n_out:n_in + 2 * n_ex + n_out]
            start, finish = _exchange_steps(list(zip(srcs, dsts)), *refs[n_in + 2 * n_ex + n_out:])
            step = (pl.program_id(0) * 4 + pl.program_id(1)) * nk + pl.program_id(2)
            pl.when(step == 0)(start)
        j, kb_i = pl.program_id(1), pl.program_id(2)

        @pl.when(kb_i == 0)
        def _():
            dq_ref[...] = jnp.zeros_like(dq_ref)
            dcq_ref[...] = jnp.zeros_like(dcq_ref)

        lo = _lane((1, LANES)) < 64
        row = lax.broadcasted_iota(jnp.int32, (bt, bt), 0)
        col = lax.broadcasted_iota(jnp.int32, (bt, bt), 1)
        k2, v2 = k_ref[...], v_ref[...]
        ks = (k2.astype(F32) * SCALE).astype(BF16)
        cc = cc_ref[...]
        dk_acc = jnp.zeros((bt, LANES), F32)
        dv_acc = jnp.zeros((bt, LANES), F32)
        dcs = []
        for hh in range(2):
            mask_h = lo if hh == 0 else jnp.logical_not(lo)
            kh = jnp.where(mask_h, ks, jnp.zeros_like(ks))
            ck = _pick_lane(cc, 2 * j + hh)

            def blk(qi, carry, diag):
                dk_a, dv_a, dc_a = carry
                start = pl.multiple_of(qi * bt, bt)
                qblk, doblk = q_ref[pl.ds(start, bt), :], do_ref[pl.ds(start, bt), :]
                qh = jnp.where(mask_h, qblk, jnp.zeros_like(qblk))
                doh = jnp.where(mask_h, doblk, jnp.zeros_like(doblk))
                a_row = cr_ref[pl.ds(hh, 1), pl.ds(start, bt)] - lse_ref[pl.ds(hh, 1), pl.ds(start, bt)]
                st = _nt(ks, qh) + (a_row - ck)
                if diag:
                    pt = jnp.where(col >= row, jnp.exp(jnp.where(col >= row, st, 0.0)), 0.0)
                else:
                    pt = jnp.exp(st)
                dpt = _nt(v2, doh)
                dst = pt * (dpt - dl_ref[pl.ds(hh, 1), pl.ds(start, bt)])
                ptb, dstb = pt.astype(BF16), dst.astype(BF16)
                dv_a = dv_a + jnp.dot(ptb, doh, preferred_element_type=F32)
                dk_a = dk_a + jnp.dot(dstb, qh, preferred_element_type=F32)
                dc_a = dc_a + jnp.sum(dst, axis=1, keepdims=True)
                dq_ref[pl.ds(start, bt), :] += _tn(dstb, kh)
                dcq_ref[pl.ds(hh, 1), pl.ds(start, bt)] += jnp.sum(dst, axis=0, keepdims=True)
                return dk_a, dv_a, dc_a

            carry = blk(kb_i, (dk_acc, dv_acc, jnp.zeros((bt, 1), F32)), True)
            dk_acc, dv_acc, dc_h = lax.fori_loop(kb_i + 1, nk, lambda qi, c: blk(qi, c, False), carry)
            dcs.append(dc_h)
        dk_ref[...] = dk_acc * SCALE
        dv_ref[...] = dv_acc
        dc_ref[...] = -jnp.where(lo, dcs[0], dcs[1])
        if exch:
            pl.when(step == nb * 4 * nk - 1)(finish)

    rows = pl.BlockSpec((None, None, 2, s), lambda b, j, kb_i: (b, j, 0, 0))
    scratch = []
    if exch:
        scratch = [pltpu.SemaphoreType.DMA((N_SEM * n_ex,)), pltpu.SemaphoreType.DMA((N_SEM * n_ex,)),
                   pltpu.SemaphoreType.DMA((n_ex,))]
    return pl.pallas_call(
        body, name="fox_bwd", grid=(nb, 4, nk),
        in_specs=[pl.BlockSpec((s, LANES), lambda b, j, kb_i: (b, j)),
                  pl.BlockSpec((s, LANES), lambda b, j, kb_i: (b, j)),
                  pl.BlockSpec((bt, LANES), lambda b, j, kb_i: (b * nk + kb_i, j)),
                  pl.BlockSpec((bt, LANES), lambda b, j, kb_i: (b * nk + kb_i, j)),
                  pl.BlockSpec((bt, LANES), lambda b, j, kb_i: (b * nk + kb_i, 0)), rows, rows, rows] + [ANY] * n_ex,
        out_specs=[pl.BlockSpec((s, LANES), lambda b, j, kb_i: (b, j)),
                   pl.BlockSpec((bt, LANES), lambda b, j, kb_i: (b * nk + kb_i, j)),
                   pl.BlockSpec((bt, LANES), lambda b, j, kb_i: (b * nk + kb_i, j)),
                   pl.BlockSpec((None, bt, LANES), lambda b, j, kb_i: (j, b * nk + kb_i, 0)), rows] + [ANY] * n_ex,
        out_shape=[jax.ShapeDtypeStruct((t, 512), F32), jax.ShapeDtypeStruct((t, 512), F32),
                   jax.ShapeDtypeStruct((t, 512), F32), jax.ShapeDtypeStruct((4, t, LANES), F32),
                   jax.ShapeDtypeStruct((nb, 4, 2, s), F32)] + [jax.ShapeDtypeStruct(e.shape, e.dtype) for e in exch],
        scratch_shapes=scratch, compiler_params=_params(("arbitrary", "arbitrary", "arbitrary")),
    )(qb, do_b, kb, vb, c_col, c_rows, lse_rows, delta_rows, *exch)


def _mlp_fwd(x2, ma, mb, tgt, wo_a, wo_b, g2, w_up, w_down, tm):
    t = x2.shape[0]

    def body(x_ref, ma_ref, mb_ref, tg_ref, woa_ref, wob_ref, g2_ref, wu_ref, wd_ref,
             h_ref, hn_ref, hid_ref, dy_ref, dyb_ref, loss_ref):
        @pl.when(pl.program_id(0) == 0)
        def _():
            loss_ref[...] = jnp.zeros_like(loss_ref)

        h = (x_ref[...] + jnp.dot(ma_ref[...], woa_ref[...], preferred_element_type=F32)
             + jnp.dot(mb_ref[...], wob_ref[...], preferred_element_type=F32))
        h_ref[...] = h
        r = lax.rsqrt(jnp.mean(h * h, axis=-1, keepdims=True) + EPS)
        hn = (h * r * g2_ref[...]).astype(BF16)
        hn_ref[...] = hn
        u = jnp.maximum(jnp.dot(hn, wu_ref[...], preferred_element_type=F32), 0.0)
        hid = (u * u).astype(BF16)
        hid_ref[...] = hid
        y = h + jnp.dot(hid, wd_ref[...], preferred_element_type=F32)
        err = y - tg_ref[...]
        dy = err * (1.0 / D_MODEL)
        dy_ref[...] = dy
        dyb_ref[...] = dy.astype(BF16)
        part =0.5 * jnp.sum(jnp.sum(err * err, axis=1, keepdims=True) * (1.0 / D_MODEL), axis=0, keepdims=True)
        loss_ref[...] += part

    def tile(w):
        return pl.BlockSpec((tm, w), lambda i: (i, 0))

    return pl.pallas_call(
        body, name="mlp_fwd", grid=(t // tm,),
        in_specs=[tile(D_MODEL), tile(512), tile(512), tile(D_MODEL), _const_spec((512, D_MODEL)), _const_spec((512, D_MODEL)),
                  _const_spec((1, D_MODEL)), _const_spec((D_MODEL, D_FF)), _const_spec((D_FF, D_MODEL))],
        out_specs=[tile(D_MODEL), tile(D_MODEL), tile(D_FF), tile(D_MODEL), tile(D_MODEL),
                   pl.BlockSpec((8, LANES), lambda i: (0, 0))],
        out_shape=[jax.ShapeDtypeStruct((t, D_MODEL), F32), jax.ShapeDtypeStruct((t, D_MODEL), BF16),
                   jax.ShapeDtypeStruct((t, D_FF), BF16), jax.ShapeDtypeStruct((t, D_MODEL), F32),
                   jax.ShapeDtypeStruct((t, D_MODEL), BF16), jax.ShapeDtypeStruct((8, LANES), F32)],
        compiler_params=_params(("arbitrary",)),
    )(x2, ma, mb, tgt, wo_a, wo_b, g2, w_up, w_down)


def _mlp_bwd(dy, hid, h, ma, mb, w_down_t, w_up_t, w_out_t, g2, tm):
    t = dy.shape[0]

    def body(dy_ref, hid_ref, h_ref, ma_ref, mb_ref, wdt_ref, wut_ref, wot_ref, g2_ref,
             du_ref, dh_ref, dhb_ref, dma_ref, dmb_ref, dla_ref, dlb_ref, gg_ref):
        @pl.when(pl.program_id(0) == 0)
        def _():
            gg_ref[...] = jnp.zeros_like(gg_ref)

        dy = dy_ref[...]
        d_hid = jnp.dot(dy.astype(BF16), wdt_ref[...], preferred_element_type=F32)
        du = (d_hid * (2.0 * jnp.sqrt(hid_ref[...].astype(F32)))).astype(BF16)
        du_ref[...] = du
        d_hn = jnp.dot(du, wut_ref[...], preferred_element_type=F32)
        h = h_ref[...]
        r = lax.rsqrt(jnp.mean(h * h, axis=-1, keepdims=True) + EPS)
        hat = h * r
        gd = d_hn * g2_ref[...]
        dh = dy + r * (gd - hat * jnp.mean(gd * hat, axis=-1, keepdims=True))
        gg_ref[...] += jnp.sum(d_hn * hat, axis=0, keepdims=True)
        dh_ref[...] = dh
        dhb = dh.astype(BF16)
        dhb_ref[...] = dhb
        dm = jnp.dot(dhb, wot_ref[...], preferred_element_type=F32).astype(BF16)
        dma, dmb = dm[:, 0:512], dm[:, 512:1024]
        dma_ref[...] = dma
        dmb_ref[...] = dmb
        sel = (lax.shift_right_logical(lax.broadcasted_iota(jnp.int32, (512, LANES), 0), 6)
               == lax.broadcasted_iota(jnp.int32, (512, LANES), 1)).astype(BF16)
        dla_ref[...] = _split_dot(dma.astype(F32) * ma_ref[...].astype(F32), sel)
        dlb_ref[...] = _split_dot(dmb.astype(F32) * mb_ref[...].astype(F32), sel)

    def tile(w):
        return pl.BlockSpec((tm, w), lambda i: (i, 0))

    return pl.pallas_call(
        body, name="mlp_bwd", grid=(t // tm,),
        in_specs=[tile(D_MODEL), tile(D_FF), tile(D_MODEL), tile(512), tile(512), _const_spec((D_MODEL, D_FF)),
                  _const_spec((D_FF, D_MODEL)), _const_spec((D_MODEL, D_MODEL)), _const_spec((1, D_MODEL))],
        out_specs=[tile(D_FF), tile(D_MODEL), tile(D_MODEL), tile(512), tile(512), tile(LANES), tile(LANES),
                   pl.BlockSpec((1, D_MODEL), lambda i: (0, 0))],
        out_shape=[jax.ShapeDtypeStruct((t, D_FF), BF16), jax.ShapeDtypeStruct((t, D_MODEL), F32),
                   jax.ShapeDtypeStruct((t, D_MODEL), BF16), jax.ShapeDtypeStruct((t, 512), BF16),
                   jax.ShapeDtypeStruct((t, 512), BF16), jax.ShapeDtypeStruct((t, LANES), F32),
                   jax.ShapeDtypeStruct((t, LANES), F32), jax.ShapeDtypeStruct((1, D_MODEL), F32)],
        compiler_params=_params(("arbitrary",)),
    )(dy, hid, h, ma, mb, w_down_t, w_up_t, w_out_t, g2)


def _wgrad(a, b, name, bm, bn, tk, out_dtype=F32, col_blocks=False):
    t, m = a.shape
    n = b.shape[1]
    bm, bn = min(bm, m), min(bn, n)
    nk = t // tk

    def body(a_ref, b_ref, o_ref, acc):
        @pl.when(pl.program_id(2) == 0)
        def _():
            acc[...] = jnp.zeros_like(acc)

        acc[...] += _tn(a_ref[...], b_ref[...])

        @pl.when(pl.program_id(2) == nk - 1)
        def _():
            o_ref[...] = acc[...].astype(out_dtype)

    if col_blocks:
        out_spec = pl.BlockSpec((None, bm, bn), lambda i, j, k: (j, i, 0))
        out_shape = jax.ShapeDtypeStruct((n // bn, m, bn), out_dtype)
    else:
        out_spec = pl.BlockSpec((bm, bn), lambda i, j, k: (i, j))
        out_shape = jax.ShapeDtypeStruct((m, n), out_dtype)
    return pl.pallas_call(
        body, name=name, grid=(m // bm, n // bn, nk),
        in_specs=[pl.BlockSpec((tk, bm), lambda i, j, k: (k, i)), pl.BlockSpec((tk, bn), lambda i, j, k: (k, j))],
        out_specs=out_spec, out_shape=out_shape, scratch_shapes=[pltpu.VMEM((bm, bn), F32)],
        compiler_params=_params(("arbitrary", "arbitrary", "arbitrary")),
    )(a, b)


def _proj_bwd(raw, dqa, dkae, dvae, dqb, dkb, dvb, dfl, x2, dh, w_main_t, w_f_t, g1, gqa, gka, gqb, gkb, tm):
    t = x2.shape[0]

    def body(raw_ref, dqa_ref, dkae_ref, dvae_ref, dqb_ref, dkb_ref, dvb_ref, dfl_ref, x_ref, dh_ref,
             wmt_ref, wft_ref, g1_ref, gqa_ref, gka_ref, gqb_ref, gkb_ref,
             dx_ref, dp_ref, dfb_ref, ggqa_ref, ggka_ref, ggqb_ref, ggkb_ref, gg1_ref):
        @pl.when(pl.program_id(0) == 0)
        def _():
            for r in (ggqa_ref, ggka_ref, ggqb_ref, ggkb_ref, gg1_ref):
                r[...] = jnp.zeros_like(r)

        raw = raw_ref[...]
        d_qa, p_qa = _head_norm_bwd(raw[:, 0:512], gqa_ref[...], dqa_ref[...])
        d_ka, p_ka = _head_norm_bwd(raw[:, 512:640], gka_ref[...], _fold_kv(dkae_ref[...]))
        d_va = _fold_kv(dvae_ref[...])
        d_qb, p_qb = _head_norm_bwd(raw[:, 768:1280], gqb_ref[...], dqb_ref[...])
        d_kb, p_kb = _head_norm_bwd(raw[:, 1280:1792], gkb_ref[...], dkb_ref[...])
        ggqa_ref[...] += jnp.sum(p_qa, axis=0, keepdims=True)
        ggka_ref[...] += jnp.sum(p_ka, axis=0, keepdims=True)
        ggqb_ref[...] += jnp.sum(p_qb, axis=0, keepdims=True)
        ggkb_ref[...] += jnp.sum(p_kb, axis=0, keepdims=True)
        dproj = jnp.concatenate([d_qa, d_ka, d_va, d_qb, d_kb, dvb_ref[...]], axis=1).astype(BF16)
        dp_ref[...] = dproj
        dfb = dfl_ref[...].astype(BF16)
        dfb_ref[...] = dfb
        d_xn = (jnp.dot(dproj, wmt_ref[...], preferred_element_type=F32)
                + jnp.dot(dfb, wft_ref[...], preferred_element_type=F32))
        x = x_ref[...]
        r = lax.rsqrt(jnp.mean(x * x, axis=-1, keepdims=True) + EPS)
        hat = x * r
        gd = d_xn * g1_ref[...]
        dx_ref[...] = dh_ref[...] + r * (gd - hat * jnp.mean(gd * hat, axis=-1, keepdims=True))
        gg1_ref[...] += jnp.sum(d_xn * hat, axis=0, keepdims=True)

    def tile(w):
        return pl.BlockSpec((tm, w), lambda i: (i, 0))

    def acc(w):
        return pl.BlockSpec((1, w), lambda i: (0, 0))

    return pl.pallas_call(
        body, name="proj_bwd", grid=(t // tm,),
        in_specs=[tile(MAIN_W), tile(512), tile(512), tile(512), tile(512), tile(512), tile(512), tile(LANES),
                  tile(D_MODEL), tile(D_MODEL), _const_spec((MAIN_W, D_MODEL)), _const_spec((LANES, D_MODEL)),
                  _const_spec((1, D_MODEL)), _const_spec((1, 512)), _const_spec((1, 128)), _const_spec((1, 512)),
                  _const_spec((1, 512))],
        out_specs=[tile(D_MODEL), tile(MAIN_W), tile(LANES), acc(512), acc(128), acc(512), acc(512), acc(D_MODEL)],
        out_shape=[jax.ShapeDtypeStruct((t, D_MODEL), F32), jax.ShapeDtypeStruct((t, MAIN_W), BF16),
                   jax.ShapeDtypeStruct((t, LANES), BF16), jax.ShapeDtypeStruct((1, 512), F32),
                   jax.ShapeDtypeStruct((1, 128), F32), jax.ShapeDtypeStruct((1, 512), F32),
                   jax.ShapeDtypeStruct((1, 512), F32), jax.ShapeDtypeStruct((1, D_MODEL), F32)],
        compiler_params=_params(("arbitrary",)),
    )(raw, dqa, dkae, dvae, dqb, dkb, dvb, dfl, x2, dh, w_main_t, w_f_t, g1, gqa, gka, gqb, gkb)


def _pair_rows(a, nb, s):
    two = jnp.stack([a[:, :, 0], a[:, :, 64]], axis=1)
    return jnp.transpose(two.reshape(4, 2, nb, s), (2, 0, 1, 3))


def _head_rows(a, nb, s):
    return jnp.transpose(a[:, 0:8].reshape(nb, s, 4, 2), (0, 2, 3, 1))


R_REST = 128 + 512 + 512
IN_PAD = 304


def _local_step(x, tgt, w_in_t, rest, g1, b_forget, qna, kna, sinks, qnb, knb, g2,
                tm=256, bt=512, tq=512, ts=256, wk=2048, distributed=False):
    nb, s, _ = x.shape
    t = nb * s
    x2, tgt2 = x.reshape(t, D_MODEL), tgt.reshape(t, D_MODEL)
    g1r, g2r = g1.reshape(1, D_MODEL), g2.reshape(1, D_MODEL)
    gqa, gka = jnp.tile(qna, 8).reshape(1, 512), jnp.tile(kna, 2).reshape(1, 128)
    gqb, gkb = jnp.tile(qnb, 8).reshape(1, 512), jnp.tile(knb, 8).reshape(1, 512)
    bf_row = jnp.pad(b_forget, (0, LANES - 8)).reshape(1, LANES)
    sink_row = jnp.pad(sinks, (0, LANES - 8)).reshape(1, LANES)
    w_main_t = w_in_t[0:MAIN_W]
    w_f_t = jnp.pad(w_in_t[MAIN_W:IN_W], ((0, LANES - 8), (0, 0)))
    w_main, w_f = w_main_t.T, w_f_t.T

    xn, raw, fl, qa, kae, vae, qb, kb, vb = _norm_proj(x2, g1r, w_main, w_f, gqa, gka, gqb, gkb, tm)
    c_col = _gate_cumsum(fl, bf_row, nb, s, ts)
    c_rows = _head_rows(c_col, nb, s)
    ma, lse_a = _swa_fwd(qa, kae, vae, sink_row, nb, s, tq)
    if distributed:
        mb, lse_b, full = _fox_fwd(qb, kb, vb, c_col, c_rows, nb, s, bt, shard=rest)
        w_out = full[:, 0:128].reshape(D_MODEL, D_MODEL)
        w_up = jnp.transpose(full[:, 128:640].reshape(N_DEV, D_MODEL, 512), (1, 0, 2)).reshape(D_MODEL, D_FF)
        w_down = full[:, 640:R_REST].reshape(D_FF, D_MODEL)
    else:
        mb, lse_b = _fox_fwd(qb, kb, vb, c_col, c_rows, nb, s, bt)
        w_out, w_up, w_down = rest
    h, hn, hid, dy, dyb, loss_acc = _mlp_fwd(x2, ma, mb, tgt2, w_out[0:512], w_out[512:1024], g2r, w_up, w_down, tm)

    du, dh, dhb, dma, dmb, dla, dlb, gg2 = _mlp_bwd(dy, hid, h, ma, mb, w_down.T, w_up.T, w_out.T, g2r, tm)
    g_down = _wgrad(hid, dyb, "wgrad_down", 512, 1024, wk, BF16).reshape(N_DEV, 512, D_MODEL)
    g_up = _wgrad(hn, du, "wgrad_up", 1024, 512, wk, BF16, col_blocks=True)
    g_out = jnp.concatenate([_wgrad(ma, dhb, "wgrad_out_a", 512, 1024, wk, BF16),
                             _wgrad(mb, dhb, "wgrad_out_b", 512, 1024, wk, BF16)], axis=0).reshape(N_DEV, 128, D_MODEL)

    dqa, dkae, dvae, dsink = _swa_bwd(qa, kae, vae, dma, sink_row, _pair_rows(lse_a, nb, s), _head_rows(dla, nb, s), nb, s, tq)
    fox = _fox_bwd(qb, kb, vb, dmb, c_col, c_rows, _pair_rows(lse_b, nb, s), _head_rows(dlb, nb, s), nb, s, bt,
                   exch=(g_out, g_up, g_down) if distributed else ())
    dqb, dkb, dvb, dc4, dcq = fox[:5]
    if distributed:
        g_out, g_up, g_down = fox[5:]
    dc_k = jnp.stack([dc4[:, :, 0], dc4[:, :, 64]], axis=-1)
    dc_k = jnp.pad(jnp.transpose(dc_k, (1, 0, 2)).reshape(t, 8), ((0, 0), (0, LANES - 8)))
    dc_q = jnp.pad(jnp.transpose(dcq, (0, 3, 1, 2)).reshape(t, 8), ((0, 0), (0, LANES - 8)))
    dfl, gbf = _gate_cumsum_bwd(dc_k, dc_q, fl, bf_row, nb, s, ts)
    grad_x, dproj, dfb, ggqa, ggka, ggqb, ggkb, gg1 = _proj_bwd(
        raw, dqa, dkae, dvae, dqb, dkb, dvb, dfl, x2, dh, w_main_t, w_f_t, g1r, gqa, gka, gqb, gkb, tm)
    g_in_t = jnp.concatenate([_wgrad(dproj, xn, "wgrad_in", 768, 1024, wk), _wgrad(dfb, xn, "wgrad_gate", 128, 1024, wk)[0:8]],
                             axis=0)

    small = (gg1.reshape(D_MODEL), gbf[0, 0:8], ggqa.reshape(8, 64).sum(0), ggka.reshape(2, 64).sum(0),
             dsink.sum(0)[:, 0:2, 0].reshape(8), ggqb.reshape(8, 64).sum(0), ggkb.reshape(8, 64).sum(0),
             gg2.reshape(D_MODEL))
    return loss_acc[0, 0], grad_x.reshape(nb, s, D_MODEL), g_in_t, g_out, g_up, g_down, small


def _all_gather(shard):
    def body(x_ref, out_ref, send_sems, recv_sems, local_sem):
        start, forward, finish = _gather_steps(x_ref, out_ref, send_sems, recv_sems, local_sem)
        start()
        forward()
        finish()

    return pl.pallas_call(
        body, name="gather_w_in", out_shape=jax.ShapeDtypeStruct((N_DEV,) + shard.shape, shard.dtype),
        in_specs=[ANY], out_specs=ANY,
        scratch_shapes=[pltpu.SemaphoreType.DMA((N_SEM,)), pltpu.SemaphoreType.DMA((N_SEM,)), pltpu.SemaphoreType.DMA(())],
    )(shard)


def _exchange(*arrays):
    n_ex = len(arrays)

    def body(*refs):
        start, finish = _exchange_steps(list(zip(refs[:n_ex], refs[n_ex:2 * n_ex])), *refs[2 * n_ex:])
        start()
        finish()

    return pl.pallas_call(
        body, name="exchange_tail", out_shape=[jax.ShapeDtypeStruct(a.shape, a.dtype) for a in arrays],
        in_specs=[ANY] * n_ex, out_specs=[ANY] * n_ex,
        scratch_shapes=[pltpu.SemaphoreType.DMA((N_SEM * n_ex,)), pltpu.SemaphoreType.DMA((N_SEM * n_ex,)),
                        pltpu.SemaphoreType.DMA((n_ex,))],
    )(*arrays)


def _sum_adamw(recv, w, m, v, tr, name):
    _, r, n = recv.shape

    def body(r_ref, w_ref, m_ref, v_ref, g_ref, d_ref, nm_ref, nv_ref):
        g = r_ref[0].astype(F32)
        for s in range(1, N_DEV):
            g = g + r_ref[s].astype(F32)
        g_ref[...] = g
        nm = ADAM_B1 * m_ref[...] + (1.0 - ADAM_B1) * g
        nv = ADAM_B2 * v_ref[...] + (1.0 - ADAM_B2) * (g * g)
        m_hat = nm / (1.0 - ADAM_B1 ** ADAM_STEP)
        v_hat = nv / (1.0 - ADAM_B2 ** ADAM_STEP)
        d_ref[...] = -ADAM_LR * (m_hat / (jnp.sqrt(v_hat) + ADAM_EPS) + ADAM_WD * w_ref[...])
        nm_ref[...] = nm
        nv_ref[...] = nv

    tile = pl.BlockSpec((tr, n), lambda i: (i, 0))
    shp = jax.ShapeDtypeStruct((r, n), F32)
    return pl.pallas_call(
        body, name=name, grid=(r // tr,),
        in_specs=[pl.BlockSpec((N_DEV, tr, n), lambda i: (0, i, 0)), tile, tile, tile],
        out_specs=[tile, tile, tile, tile], out_shape=[shp, shp, shp, shp],
        compiler_params=_params(("arbitrary",)),
    )(recv, w, m, v)


def _small_rows(g1, bf, qna, kna, sk, qnb, knb, g2):
    row2 = jnp.concatenate([bf, qna, kna, sk, qnb, knb])
    return jnp.zeros((8, D_MODEL), F32).at[0].set(g1).at[1].set(g2).at[2, 0:row2.shape[0]].set(row2)


def _in_rows(w_in_s):
    return jnp.pad(w_in_s.T, ((0, IN_PAD - IN_SHARD), (0, 0)))


def kernel(x, attn_norm_g, w_in, b_forget, q_norm_a, k_norm_a, sink_logits, q_norm_b, k_norm_b, w_out, mlp_norm_g, w_up, w_down, loss_target, m_attn_norm_g, m_w_in, m_b_forget, m_q_norm_a, m_k_norm_a, m_sink_logits, m_q_norm_b, m_k_norm_b, m_w_out, m_mlp_norm_g, m_w_up, m_w_down, v_attn_norm_g, v_w_in, v_b_forget, v_q_norm_a, v_k_norm_a, v_sink_logits, v_q_norm_b, v_k_norm_b, v_w_out, v_mlp_norm_g, v_w_up, v_w_down):
    w_in_r = _in_rows(w_in)
    w_in_t = _all_gather(w_in_r.astype(BF16))[:, 0:IN_SHARD].reshape(IN_W, D_MODEL)
    rest = jnp.concatenate([w_out, w_up.reshape(512, D_MODEL), w_down], axis=0).astype(BF16)

    loss_part, grad_x, g_in_t, r_out, r_up, r_down, small = _local_step(
        x, loss_target, w_in_t, rest, attn_norm_g, b_forget, q_norm_a, k_norm_a, sink_logits, q_norm_b, k_norm_b, mlp_norm_g,
        distributed=True)

    g_in_blocks = jnp.pad(g_in_t.reshape(N_DEV, IN_SHARD, D_MODEL), ((0, 0), (0, IN_PAD - IN_SHARD), (0, 0))).astype(BF16)
    small_blocks = jnp.broadcast_to(_small_rows(*small), (N_DEV, 8, D_MODEL))
    r_in, r_small = _exchange(g_in_blocks, small_blocks)

    small_w = _small_rows(attn_norm_g, b_forget, q_norm_a, k_norm_a, sink_logits, q_norm_b, k_norm_b, mlp_norm_g)
    small_m = _small_rows(m_attn_norm_g, m_b_forget, m_q_norm_a, m_k_norm_a, m_sink_logits, m_q_norm_b, m_k_norm_b, m_mlp_norm_g)
    small_v = _small_rows(v_attn_norm_g, v_b_forget, v_q_norm_a, v_k_norm_a, v_sink_logits, v_q_norm_b, v_k_norm_b, v_mlp_norm_g)
    o_in = [a[0:IN_SHARD].T for a in _sum_adamw(r_in, w_in_r, _in_rows(m_w_in), _in_rows(v_w_in), IN_PAD, "adamw_in")]
    o_out = _sum_adamw(r_out, w_out, m_w_out, v_w_out, 128, "adamw_out")
    o_up = _sum_adamw(r_up, w_up, m_w_up, v_w_up, 256, "adamw_up")
    o_down = _sum_adamw(r_down, w_down, m_w_down, v_w_down, 128, "adamw_down")
    o_small = _sum_adamw(r_small, small_w, small_m, small_v, 8, "adamw_small")

    def leaves(i):
        row2 = o_small[i][2]
        return (o_small[i][0], o_in[i], row2[0:8], row2[8:72], row2[72:136], row2[136:144], row2[144:208], row2[208:272],
                o_out[i], o_small[i][1], o_up[i], o_down[i])

    loss = lax.psum(loss_part, ("x", "y", "c"))
    return (loss, grad_x, *leaves(0), *leaves(1), *leaves(2), *leaves(3))
```

```python
import functools
import math

import jax
import jax.numpy as jnp
from jax import lax
from jax.experimental import pallas as pl
from jax.experimental.pallas import tpu as pltpu

F32 = jnp.float32
BF16 = jnp.bfloat16

D_MODEL = 1024
HEAD_DIM = 64
N_DEV = 8
D_FF = 4096
A_QW = 512
A_KVW = 128
B_W = 512
MAIN_W = 2304
IN_W = 2312
WINDOW = 128
EPS = 1e-6
SCALE = 0.125
LOG2E = 1.4426950408889634
LANES = 128
NEG_INF = float("-inf")

ADAM_LR = 0.001
ADAM_B1 = 0.9
ADAM_B2 = 0.999
ADAM_EPS = 1e-08
ADAM_WD = 0.01
ADAM_STEP = 10

R_OUT, R_UP, R_DOWN, R_IN = 0, 128, 640, 1152
IN_SHARD = 289
R_SMALL = 1456
R_PACK = 1472
VMEM_LIMIT = 56 * 1024 * 1024


def _params(sem, vmem=VMEM_LIMIT):
    return pltpu.CompilerParams(dimension_semantics=sem, vmem_limit_bytes=vmem)


def _const_spec(shape):
    nd = len(shape)
    return pl.BlockSpec(shape, lambda *_: (0,) * nd, pipeline_mode=pl.Buffered(1))


def _lane(shape):
    return lax.broadcasted_iota(jnp.int32, shape, len(shape) - 1)


def _split_dot(v, mat):
    hi = v.astype(BF16)
    lo = (v - hi.astype(F32)).astype(BF16)
    return (jnp.dot(hi, mat, preferred_element_type=F32) + jnp.dot(lo, mat, preferred_element_type=F32))


def _head_ones(n):
    r = lax.shift_right_logical(lax.broadcasted_iota(jnp.int32, (n, n), 0), 6)
    c = lax.shift_right_logical(lax.broadcasted_iota(jnp.int32, (n, n), 1), 6)
    return (r == c).astype(BF16)


def _head_sum(v):
    w = v.shape[1]
    if w <= 256:
        return _split_dot(v, _head_ones(w))
    ones = _head_ones(256)
    return jnp.concatenate([_split_dot(v[:, s:s + 256], ones) for s in range(0, w, 256)], axis=1)


def _head_norm(seg, gain):
    rs = lax.rsqrt(_head_sum(seg * seg) * (1.0 / HEAD_DIM) + EPS)
    return seg * rs * gain


def _head_norm_bwd(seg, gain, d_out):
    rs = lax.rsqrt(_head_sum(seg * seg) * (1.0 / HEAD_DIM) + EPS)
    hat = seg * rs
    gd = d_out * gain
    d_seg = rs * (gd - hat * (_head_sum(gd * hat) * (1.0 / HEAD_DIM)))
    return d_seg, d_out * hat


def _expand_kv(v):
    r = pltpu.roll(v, 64, axis=1)
    lo = _lane(v.shape) < 64
    return jnp.concatenate([jnp.where(lo, v, r), jnp.where(lo, r, v)], axis=1)


def _fold_kv(e4):
    t0 = e4[:, 0:128] + e4[:, 128:256]
    t1 = e4[:, 256:384] + e4[:, 384:512]
    t0 = t0 + pltpu.roll(t0, 64, axis=1)
    t1 = t1 + pltpu.roll(t1, 64, axis=1)
    return jnp.where(_lane(t0.shape) < 64, t0, t1)


def _pick_lane(blk, idx):
    return jnp.sum(jnp.where(_lane(blk.shape) == idx, blk, 0.0), axis=1, keepdims=True)


def _nt(a, b):
    return lax.dot_general(a, b, (((1,), (1,)), ((), ())), preferred_element_type=F32)


def _tn(a, b):
    return lax.dot_general(a, b, (((0,), (0,)), ((), ())), preferred_element_type=F32)


def _norm_proj(x2, g1, w_main, w_f, gqa, gka, gqb, gkb, tm):
    t = x2.shape[0]

    def body(x_ref, g1_ref, wm_ref, wf_ref, gqa_ref, gka_ref, gqb_ref, gkb_ref,
             xn_ref, raw_ref, fl_ref, qa_ref, kae_ref, vae_ref, qb_ref, kb_ref, vb_ref):
        x = x_ref[...]
        r = lax.rsqrt(jnp.mean(x * x, axis=-1, keepdims=True) + EPS)
        xn = (x * r * g1_ref[...]).astype(BF16)
        xn_ref[...] = xn
        proj = jnp.dot(xn, wm_ref[...], preferred_element_type=F32)
        raw_ref[...] = proj
        fl_ref[...] = jnp.dot(xn, wf_ref[...], preferred_element_type=F32)
        qa_ref[...] = _head_norm(proj[:, 0:512], gqa_ref[...]).astype(BF16)
        kae_ref[...] = _expand_kv(_head_norm(proj[:, 512:640], gka_ref[...])).astype(BF16)
        vae_ref[...] = _expand_kv(proj[:, 640:768]).astype(BF16)
        qb_ref[...] = (_head_norm(proj[:, 768:1280], gqb_ref[...]) * (SCALE * LOG2E)).astype(BF16)
        kb_ref[...] = _head_norm(proj[:, 1280:1792], gkb_ref[...]).astype(BF16)
        vb_ref[...] = proj[:, 1792:2304].astype(BF16)

    def tile(w):
        return pl.BlockSpec((tm, w), lambda i: (i, 0))

    return pl.pallas_call(
        body, name="norm_proj", grid=(t // tm,),
        in_specs=[tile(D_MODEL), _const_spec((1, D_MODEL)), _const_spec((D_MODEL, MAIN_W)), _const_spec((D_MODEL, LANES)),
                  _const_spec((1, 512)), _const_spec((1, 128)), _const_spec((1, 512)), _const_spec((1, 512))],
        out_specs=[tile(D_MODEL), tile(MAIN_W), tile(LANES), tile(512), tile(256), tile(256), tile(512), tile(512), tile(512)],
        out_shape=[jax.ShapeDtypeStruct((t, D_MODEL), BF16), jax.ShapeDtypeStruct((t, MAIN_W), F32),
                   jax.ShapeDtypeStruct((t, LANES), F32), jax.ShapeDtypeStruct((t, 512), BF16),
                   jax.ShapeDtypeStruct((t, 256), BF16), jax.ShapeDtypeStruct((t, 256), BF16),
                   jax.ShapeDtypeStruct((t, 512), BF16), jax.ShapeDtypeStruct((t, 512), BF16),
                   jax.ShapeDtypeStruct((t, 512), BF16)],
        compiler_params=_params(("arbitrary",)),
    )(x2, g1, w_main, w_f, gqa, gka, gqb, gkb)


def _tri(n, upper):
    r = lax.broadcasted_iota(jnp.int32, (n, n), 0)
    c = lax.broadcasted_iota(jnp.int32, (n, n), 1)
    return ((c >= r) if upper else (c <= r)).astype(F32)


def _gate_cumsum(fl, bf_row, nb, s, ts):
    t = fl.shape[0]
    nt = s // ts

    def body(fl_ref, b_ref, c_ref, carry):
        @pl.when(pl.program_id(1) == 0)
        def _():
            carry[...] = jnp.zeros_like(carry)

        z = fl_ref[...] + b_ref[...]
        e = jnp.exp(-jnp.abs(z))
        u = 1.0 + e
        log1p = jnp.where(u == 1.0, e, jnp.log(u) * (e / (u - 1.0)))
        lf = jnp.minimum(z, 0.0) - log1p
        c_ref[...] = jnp.dot(_tri(ts, False), lf, precision=lax.Precision.HIGHEST, preferred_element_type=F32) + carry[...]
        carry[...] = c_ref[pl.ds(ts - 1, 1), :]

    return pl.pallas_call(
        body, name="gate_cumsum", grid=(nb, nt),
        in_specs=[pl.BlockSpec((ts, LANES), lambda b, i: (b * nt + i, 0)), _const_spec((1, LANES))],
        out_specs=pl.BlockSpec((ts, LANES), lambda b, i: (b * nt + i, 0)),
        out_shape=jax.ShapeDtypeStruct((t, LANES), F32),
        scratch_shapes=[pltpu.VMEM((1, LANES), F32)],
        compiler_params=_params(("arbitrary", "arbitrary")),
    )(fl, bf_row)


def _gate_cumsum_bwd(dq_raw, dk_raw, fl, bf_row, nb, s, ts):
    t = fl.shape[0]
    nt = s // ts

    def body(dq_ref, dk_ref, fl_ref, b_ref, df_ref, gb_ref, carry, dlf_ref):
        @pl.when(pl.program_id(1) == 0)
        def _():
            carry[...] = jnp.zeros_like(carry)

        @pl.when((pl.program_id(0) == 0) & (pl.program_id(1) == 0))
        def _():
            gb_ref[...] = jnp.zeros_like(gb_ref)

        lane = _lane((ts, LANES))
        dc = jnp.zeros((ts, LANES), F32)
        for h in range(8):
            col = dq_ref[:, LANES * h + L_CQ:LANES * h + L_CQ + 1] - dk_ref[:, LANES * h + L_CK:LANES * h + L_CK + 1]
            dc = jnp.where(lane == h, col, dc)
        dlf_ref[...] = jnp.dot(_tri(ts, True), dc, precision=lax.Precision.HIGHEST, preferred_element_type=F32) + carry[...]
        carry[...] = dlf_ref[pl.ds(0, 1), :]
        dlf = dlf_ref[...]
        z = fl_ref[...] + b_ref[...]
        df = dlf * (1.0 / (1.0 + jnp.exp(z)))
        df_ref[...] = df
        gb_ref[...] += jnp.sum(df, axis=0, keepdims=True)

    def rev(b, i):
        return (b * nt + (nt - 1 - i), 0)

    return pl.pallas_call(
        body, name="gate_cumsum_bwd", grid=(nb, nt),
        in_specs=[pl.BlockSpec((ts, 8 * LANES), rev), pl.BlockSpec((ts, 8 * LANES), rev), pl.BlockSpec((ts, LANES), rev),
                  _const_spec((1, LANES))],
        out_specs=[pl.BlockSpec((ts, LANES), rev), pl.BlockSpec((1, LANES), lambda b, i: (0, 0))],
        out_shape=[jax.ShapeDtypeStruct((t, LANES), F32), jax.ShapeDtypeStruct((1, LANES), F32)],
        scratch_shapes=[pltpu.VMEM((1, LANES), F32), pltpu.VMEM((ts, LANES), F32)],
        compiler_params=_params(("arbitrary", "arbitrary")),
    )(dq_raw, dk_raw, fl, bf_row)


def _slope(p, hh):
    out = jnp.float32(2.0 ** -(2 * 3 + hh + 1))
    for pp in (2, 1, 0):
        out = jnp.where(p == pp, jnp.float32(2.0 ** -(2 * pp + hh + 1)), out)
    return out


def _swa_windows(ref, i, tq):
    nsub = tq // WINDOW
    cur = ref[pl.ds(pl.multiple_of(i * tq, tq), tq), :].reshape(nsub, WINDOW, LANES)
    first = ref[pl.ds(pl.multiple_of(jnp.maximum(i * tq - WINDOW, 0), WINDOW), WINDOW), :].reshape(1, WINDOW, LANES)
    return jnp.concatenate([jnp.concatenate([first, cur[0:nsub - 1]], axis=0), cur], axis=1)


def _both_heads(x3, lo):
    zero = jnp.zeros_like(x3)
    return jnp.concatenate([jnp.where(lo, x3, zero), jnp.where(lo, zero, x3)], axis=0)


def _swa_head_consts(sink_ref, p, i, nsub):
    bidx = lax.broadcasted_iota(jnp.int32, (2 * nsub, 1, 1), 0)
    is_a = bidx < nsub
    slope = jnp.where(is_a, _slope(p, 0), _slope(p, 1))
    sinks = sink_ref[...]
    sink = jnp.where(is_a, _pick_lane(sinks, 2 * p).reshape(1, 1, 1), _pick_lane(sinks, 2 * p + 1).reshape(1, 1, 1))
    first = (i == 0) & ((bidx == 0) | (bidx == nsub))
    return slope, sink, first


def _swa_fwd(qa, kae, vae, sink_row, nb, s, tq):
    t = qa.shape[0]
    nq = s // tq
    nsub = tq // WINDOW

    def body(q_ref, k_ref, v_ref, sink_ref, o_ref, lse_ref):
        p, i = pl.program_id(1), pl.program_id(2)
        lo = _lane((1, 1, LANES)) < 64
        kk, vv = _swa_windows(k_ref, i, tq), _swa_windows(v_ref, i, tq)
        qs = (q_ref[...].astype(F32) * SCALE).astype(BF16).reshape(nsub, WINDOW, LANES)
        q8 = _both_heads(qs, lo)
        s8 = jnp.einsum("bqd,bkd->bqk", q8, jnp.concatenate([kk, kk], axis=0), preferred_element_type=F32)
        row = lax.broadcasted_iota(jnp.int32, (1, WINDOW, 2 * WINDOW), 1)
        col = lax.broadcasted_iota(jnp.int32, (1, WINDOW, 2 * WINDOW), 2)
        dist = row + WINDOW - col
        slope, sink, first = _swa_head_consts(sink_ref, p, i, nsub)
        valid = (dist >= 0) & (dist < WINDOW) & ((col >= WINDOW) | jnp.logical_not(first))
        s8 = jnp.where(valid, s8 - slope * dist.astype(F32), NEG_INF)
        m = jnp.maximum(jnp.max(s8, axis=2, keepdims=True), sink)
        e = jnp.exp(s8 - m)
        den = jnp.sum(e, axis=2, keepdims=True) + jnp.exp(sink - m)
        pr = (e / den).astype(BF16)
        o8 = jnp.einsum("bqk,bkd->bqd", pr, jnp.concatenate([vv, vv], axis=0), preferred_element_type=F32)
        lse8 = m + jnp.log(den)
        o_ref[...] = jnp.where(lo, o8[0:nsub], o8[nsub:]).astype(BF16).reshape(tq, LANES)
        lse_ref[...] = jnp.where(lo, lse8[0:nsub], lse8[nsub:]).reshape(tq, LANES)

    return pl.pallas_call(
        body, name="swa_fwd", grid=(nb, 4, nq),
        in_specs=[pl.BlockSpec((tq, LANES), lambda b, p, i: (b * nq + i, p)),
                  pl.BlockSpec((s, LANES), lambda b, p, i: (b, lax.shift_right_logical(p, 1))),
                  pl.BlockSpec((s, LANES), lambda b, p, i: (b, lax.shift_right_logical(p, 1))),
                  pl.BlockSpec((1, LANES), lambda b, p, i: (0, 0))],
        out_specs=[pl.BlockSpec((tq, LANES), lambda b, p, i: (b * nq + i, p)),
                   pl.BlockSpec((None, tq, LANES), lambda b, p, i: (p, b * nq + i, 0))],
        out_shape=[jax.ShapeDtypeStruct((t, 512), BF16), jax.ShapeDtypeStruct((4, t, LANES), F32)],
        compiler_params=_params(("arbitrary", "arbitrary", "arbitrary")),
    )(qa, kae, vae, sink_row)


def _swa_bwd(qa, kae, vae, do_a, sink_row, lse_rows, delta_rows, nb, s, tq):
    t = qa.shape[0]
    nq = s // tq
    nsub = tq // WINDOW

    def body(q_ref, do_ref, k_ref, v_ref, sink_ref, lse_ref, dl_ref, dq_ref, dk_ref, dv_ref, ds_ref):
        p, i = pl.program_id(1), pl.program_id(2)

        @pl.when(i == 0)
        def _():
            ds_ref[...] = jnp.zeros_like(ds_ref)

        lo = _lane((1, 1, LANES)) < 64
        kk, vv = _swa_windows(k_ref, i, tq), _swa_windows(v_ref, i, tq)
        kks = (kk.astype(F32) * SCALE).astype(BF16)
        k8, v8 = jnp.concatenate([kks, kks], axis=0), jnp.concatenate([vv, vv], axis=0)
        q8 = _both_heads(q_ref[...].reshape(nsub, WINDOW, LANES), lo)
        do8 = _both_heads(do_ref[...].reshape(nsub, WINDOW, LANES), lo)
        cur = pl.multiple_of(i * tq, tq)

        def stat(ref):
            return jnp.concatenate([ref[pl.ds(hh, 1), pl.ds(cur + u * WINDOW, WINDOW)].reshape(1, 1, WINDOW)
                                    for hh in range(2) for u in range(nsub)], axis=0)

        lse8, dl8 = stat(lse_ref), stat(dl_ref)
        row = lax.broadcasted_iota(jnp.int32, (1, 2 * WINDOW, WINDOW), 1)
        col = lax.broadcasted_iota(jnp.int32, (1, 2 * WINDOW, WINDOW), 2)
        dist = col + WINDOW - row
        slope, sink, first = _swa_head_consts(sink_ref, p, i, nsub)
        valid = (dist >= 0) & (dist < WINDOW) & ((row >= WINDOW) | jnp.logical_not(first))
        st = jnp.einsum("bkd,bqd->bkq", k8, q8, preferred_element_type=F32) - slope * dist.astype(F32) - lse8
        pt = jnp.where(valid, jnp.exp(jnp.where(valid, st, 0.0)), 0.0)
        dpt = jnp.einsum("bkd,bqd->bkq", v8, do8, preferred_element_type=F32)
        dst = pt * (dpt - dl8)
        ptb, dstb = pt.astype(BF16), dst.astype(BF16)
        dv8 = jnp.einsum("bkq,bqd->bkd", ptb, do8, preferred_element_type=F32)
        dk8 = jnp.einsum("bkq,bqd->bkd", dstb, q8, preferred_element_type=F32) * SCALE
        dq8 = jnp.einsum("bkq,bkd->bqd", dstb, k8, preferred_element_type=F32)
        dq_ref[...] = jnp.where(lo, dq8[0:nsub], dq8[nsub:]).reshape(tq, LANES)

        psd = jnp.exp(sink - lse8) * dl8
        row_h = lax.broadcasted_iota(jnp.int32, (8, LANES), 0)
        for hh in range(2):
            tot = jnp.sum(jnp.sum(psd[hh * nsub:(hh + 1) * nsub], axis=2, keepdims=True), axis=0, keepdims=True)
            ds_ref[...] += jnp.where(row_h == hh, -tot.reshape(1, 1), 0.0)

        prev = pl.multiple_of(jnp.maximum(i * tq - WINDOW, 0), WINDOW)
        for g8, g_ref in ((dk8, dk_ref), (dv8, dv_ref)):
            g4 = g8[0:nsub] + g8[nsub:]
            own, before = g4[:, WINDOW:, :], g4[:, 0:WINDOW, :]
            shifted = jnp.concatenate([before[1:nsub], jnp.zeros((1, WINDOW, LANES), F32)], axis=0)
            g_ref[pl.ds(cur, tq), :] = (own + shifted).reshape(tq, LANES)
            g_ref[pl.ds(prev, WINDOW), :] += before[0]

    rows = pl.BlockSpec((None, None, 2, s), lambda b, p, i: (b, p, 0, 0))
    return pl.pallas_call(
        body, name="swa_bwd", grid=(nb, 4, nq),
        in_specs=[pl.BlockSpec((tq, LANES), lambda b, p, i: (b * nq + i, p)),
                  pl.BlockSpec((tq, LANES), lambda b, p, i: (b * nq + i, p)),
                  pl.BlockSpec((s, LANES), lambda b, p, i: (b, lax.shift_right_logical(p, 1))),
                  pl.BlockSpec((s, LANES), lambda b, p, i: (b, lax.shift_right_logical(p, 1))),
                  pl.BlockSpec((1, LANES), lambda b, p, i: (0, 0)), rows, rows],
        out_specs=[pl.BlockSpec((tq, LANES), lambda b, p, i: (b * nq + i, p)),
                   pl.BlockSpec((s, LANES), lambda b, p, i: (b, p)),
                   pl.BlockSpec((s, LANES), lambda b, p, i: (b, p)),
                   pl.BlockSpec((None, None, 8, LANES), lambda b, p, i: (b, p, 0, 0))],
        out_shape=[jax.ShapeDtypeStruct((t, 512), F32), jax.ShapeDtypeStruct((t, 512), F32),
                   jax.ShapeDtypeStruct((t, 512), F32), jax.ShapeDtypeStruct((nb, 4, 8, LANES), F32)],
        compiler_params=_params(("arbitrary", "arbitrary", "arbitrary")),
    )(qa, do_a, kae, vae, sink_row, lse_rows, delta_rows)


MESH = pl.DeviceIdType.MESH
ANY = pl.BlockSpec(memory_space=pl.ANY)
N_SEM = 7


def _gather_steps(x_ref, out_ref, send_sems, recv_sems, local_sem):
    x, y, c = lax.axis_index("x"), lax.axis_index("y"), lax.axis_index("c")
    me, sibling = (x, y, c), (x, y, 1 - c)
    chips = [(1 - x, y), (x, 1 - y), (1 - x, 1 - y)]

    def slot(px, py, pc):
        return out_ref.at[4 * px + 2 * py + pc]

    def copy(k, block, to, src=None):
        return pltpu.make_async_remote_copy(
            src_ref=slot(*block) if src is None else src, dst_ref=slot(*block),
            send_sem=send_sems.at[k], recv_sem=recv_sems.at[k], device_id=to, device_id_type=MESH)

    mine = pltpu.make_async_copy(x_ref, slot(*me), local_sem)
    first = [copy(0, me, sibling, src=x_ref)] + [copy(1 + j, me, (*chip, c), src=x_ref) for j, chip in enumerate(chips)]
    passed = [copy(4 + j, (*chip, c), sibling) for j, chip in enumerate(chips)]

    def start():
        mine.start()
        for cp in first:
            cp.start()

    def forward():
        for j, chip in enumerate(chips):
            copy(1 + j, (*chip, c), me).wait_recv()
            passed[j].start()

    def finish():
        copy(0, sibling, me).wait_recv()
        for j, chip in enumerate(chips):
            copy(4 + j, (*chip, 1 - c), me).wait_recv()
        for cp in first + passed:
            cp.wait_send()
        mine.wait()

    return start, forward, finish


def _exchange_steps(pairs, send_sems, recv_sems, local_sems):
    x, y, c = lax.axis_index("x"), lax.axis_index("y"), lax.axis_index("c")
    my_id = 4 * x + 2 * y + c
    local, remote = [], []
    for a, (src, dst) in enumerate(pairs):
        local.append(pltpu.make_async_copy(src.at[my_id], dst.at[my_id], local_sems.at[a]))
        for k in range(1, N_DEV):
            px = 1 - x if k & 4 else x
            py = 1 - y if k & 2 else y
            pc = 1 - c if k & 1 else c
            remote.append(pltpu.make_async_remote_copy(
                src_ref=src.at[4 * px + 2 * py + pc], dst_ref=dst.at[my_id],
                send_sem=send_sems.at[N_SEM * a + k - 1], recv_sem=recv_sems.at[N_SEM * a + k - 1],
                device_id=(px, py, pc), device_id_type=MESH))

    def start():
        for cp in local + remote:
            cp.start()

    def finish():
        for cp in remote:
            cp.wait_recv()
        for cp in remote:
            cp.wait_send()
        for cp in local:
            cp.wait()

    return start, finish


L_ONE = 64
L_CK = 65
L_CQ = 68
L_LSE = 71
L_DELTA = 74


def _head_block(pair, half):
    y = pair if half == 0 else pltpu.roll(pair, 64, axis=1)
    return jnp.where(_lane(pair.shape) < 64, y, 0.0)


def _put3(blk, lane0, col):
    lane = _lane(blk.shape)
    hi = col.astype(BF16).astype(F32)
    mid = (col - hi).astype(BF16).astype(F32)
    lo = (col - hi - mid).astype(BF16).astype(F32)
    return jnp.where(lane == lane0, hi, jnp.where(lane == lane0 + 1, mid, jnp.where(lane == lane0 + 2, lo, blk)))


def _put_ones(blk, lanes):
    lane = _lane(blk.shape)
    hit = functools.reduce(jnp.logical_or, [lane == ln for ln in lanes])
    return jnp.where(hit, 1.0, blk)


def _to_pairs(ref):
    out = []
    for j in range(4):
        a, b = ref[:, 2 * LANES * j:2 * LANES * j + LANES], ref[:, 2 * LANES * j + LANES:2 * LANES * (j + 1)]
        out.append(jnp.where(_lane(a.shape) < 64, a, pltpu.roll(b, 64, axis=1)))
    return jnp.concatenate(out, axis=1)


def _fox_prep(qb, kb, vb, c_col, tm):
    t = qb.shape[0]

    def body(q_ref, k_ref, v_ref, c_ref, qo_ref, ko_ref, vo_ref):
        c2 = c_ref[...] * LOG2E
        for h in range(8):
            j, half = h // 2, h % 2
            pair, blk = slice(LANES * j, LANES * (j + 1)), slice(LANES * h, LANES * (h + 1))
            ch = c2[:, h:h + 1]
            q = _put_ones(_head_block(q_ref[:, pair].astype(F32), half), (L_CK, L_CK + 1, L_CK + 2))
            qo_ref[:, blk] = _put3(q, L_CQ, ch).astype(BF16)
            k = _put_ones(_head_block(k_ref[:, pair].astype(F32), half), tuple(range(L_CQ, L_CQ + 6)))
            ko_ref[:, blk] = _put3(k, L_CK, -ch).astype(BF16)
            v = _head_block(v_ref[:, pair].astype(F32), half)
            vo_ref[:, blk] = _put_ones(v, (L_ONE, L_DELTA, L_DELTA + 1, L_DELTA + 2)).astype(BF16)

    def tile(w):
        return pl.BlockSpec((tm, w), lambda i: (i, 0))

    shp = jax.ShapeDtypeStruct((t, 8 * LANES), BF16)
    return pl.pallas_call(
        body, name="fox_prep", grid=(t // tm,), in_specs=[tile(512), tile(512), tile(512), tile(LANES)],
        out_specs=[tile(8 * LANES)] * 3, out_shape=[shp, shp, shp], compiler_params=_params(("arbitrary",)),
    )(qb, kb, vb, c_col)


def _fox_fwd(q_aug, k_aug, v_aug, nb, s, bt, shard=None):
    t = q_aug.shape[0]
    nq = s // bt
    n_in = 3

    def body(*refs):
        q_ref, k_ref, v_ref = refs[:n_in]
        if shard is None:
            o_ref, ql_ref = refs[n_in:]
        else:
            x_ref, o_ref, ql_ref, full_ref, send_sems, recv_sems, local_sem = refs[n_in:]
            start, forward, finish = _gather_steps(x_ref, full_ref, send_sems, recv_sems, local_sem)
            step = (pl.program_id(0) * 4 + pl.program_id(1)) * nq + pl.program_id(2)
            pl.when(step == 0)(start)
            pl.when(step == nb * 2 * nq)(forward)
        i = pl.program_id(2)
        row = lax.broadcasted_iota(jnp.int32, (bt, bt), 0)
        col = lax.broadcasted_iota(jnp.int32, (bt, bt), 1)
        outs = []
        for hh in range(2):
            sl = slice(LANES * hh, LANES * (hh + 1))
            qh = q_ref[:, sl]

            def blk(kb_i, carry, diag):
                m, acc = carry
                start = pl.multiple_of(kb_i * bt, bt)
                sc = _nt(qh, k_ref[pl.ds(start, bt), sl])
                if diag:
                    sc = jnp.where(row >= col, sc, NEG_INF)
                m_new = jnp.maximum(m, jnp.max(sc, axis=1, keepdims=True))
                pr = jnp.exp2(sc - m_new).astype(BF16)
                acc = jnp.exp2(m - m_new) * acc + jnp.dot(pr, v_ref[pl.ds(start, bt), sl], preferred_element_type=F32)
                return m_new, acc

            init = (jnp.full((bt, 1), NEG_INF, F32), jnp.zeros((bt, LANES), F32))
            carry = lax.fori_loop(0, i, lambda kb_i, c: blk(kb_i, c, False), init)
            m, acc = blk(i, carry, True)
            l = acc[:, L_ONE:L_ONE + 1]
            outs.append(acc / l)
            ql_ref[:, sl] = _put3(qh.astype(F32), L_LSE, -(m + jnp.log(l) * LOG2E)).astype(BF16)
        o_ref[...] = jnp.where(_lane((1, LANES)) < 64, outs[0], pltpu.roll(outs[1], 64, axis=1)).astype(BF16)
        if shard is not None:
            pl.when(step == nb * 4 * nq - 1)(finish)

    in_specs = [pl.BlockSpec((bt, 2 * LANES), lambda b, j, i: (b * nq + i, j)),
                pl.BlockSpec((s, 2 * LANES), lambda b, j, i: (b, j)),
                pl.BlockSpec((s, 2 * LANES), lambda b, j, i: (b, j))]
    out_specs = [pl.BlockSpec((bt, LANES), lambda b, j, i: (b * nq + i, j)),
                 pl.BlockSpec((bt, 2 * LANES), lambda b, j, i: (b * nq + i, j))]
    out_shape = [jax.ShapeDtypeStruct((t, 512), BF16), jax.ShapeDtypeStruct((t, 8 * LANES), BF16)]
    args, scratch = [q_aug, k_aug, v_aug], []
    if shard is not None:
        in_specs.append(ANY)
        out_specs.append(ANY)
        out_shape.append(jax.ShapeDtypeStruct((N_DEV,) + shard.shape, shard.dtype))
        args.append(shard)
        scratch = [pltpu.SemaphoreType.DMA((N_SEM,)), pltpu.SemaphoreType.DMA((N_SEM,)), pltpu.SemaphoreType.DMA(())]
    return pl.pallas_call(
        body, name="fox_fwd", grid=(nb, 4, nq), in_specs=in_specs, out_specs=out_specs, out_shape=out_shape,
        scratch_shapes=scratch, compiler_params=_params(("arbitrary", "arbitrary", "arbitrary")),
    )(*args)


def _fox_bwd(ql_aug, k_aug, v_aug, do_aug, nb, s, bt, exch=()):
    t = ql_aug.shape[0]
    nk = s // bt
    n_in, n_out, n_ex = 4, 3, len(exch)

    def body(*refs):
        q_ref, do_ref, k_ref, v_ref = refs[:n_in]
        dq_ref, dk_ref, dv_ref = refs[n_in + n_ex:n_in + n_ex + n_out]
        if exch:
            srcs = refs[n_in:n_in + n_ex]
            dsts = refs[n_in + n_ex + n_out:n_in + 2 * n_ex + n_out]
            start, finish = _exchange_steps(list(zip(srcs, dsts)), *refs[n_in + 2 * n_ex + n_out:])
            step = (pl.program_id(0) * 4 + pl.program_id(1)) * nk + pl.program_id(2)
            pl.when(step == 0)(start)
        kb_i = pl.program_id(2)

        @pl.when(kb_i == 0)
        def _():
            dq_ref[...] = jnp.zeros_like(dq_ref)

        row = lax.broadcasted_iota(jnp.int32, (bt, bt), 0)
        col = lax.broadcasted_iota(jnp.int32, (bt, bt), 1)
        for hh in range(2):
            sl = slice(LANES * hh, LANES * (hh + 1))
            kh, vh = k_ref[:, sl], v_ref[:, sl]

            def blk(qi, carry, diag):
                dk_a, dv_a = carry
                start = pl.multiple_of(qi * bt, bt)
                qblk, doblk = q_ref[pl.ds(start, bt), sl], do_ref[pl.ds(start, bt), sl]
                st = _nt(kh, qblk)
                if diag:
                    pt = jnp.where(col >= row, jnp.exp2(jnp.where(col >= row, st, 0.0)), 0.0)
                else:
                    pt = jnp.exp2(st)
                dst = pt * _nt(vh, doblk)
                ptb, dstb = pt.astype(BF16), dst.astype(BF16)
                dv_a = dv_a + jnp.dot(ptb, doblk, preferred_element_type=F32)
                dk_a = dk_a + jnp.dot(dstb, qblk, preferred_element_type=F32)
                dq_ref[pl.ds(start, bt), sl] += _tn(dstb, kh)
                return dk_a, dv_a

            carry = blk(kb_i, (jnp.zeros((bt, LANES), F32), jnp.zeros((bt, LANES), F32)), True)
            dk_acc, dv_acc = lax.fori_loop(kb_i + 1, nk, lambda qi, c: blk(qi, c, False), carry)
            dk_ref[:, sl] = dk_acc
            dv_ref[:, sl] = dv_acc
        if exch:
            pl.when(step == nb * 4 * nk - 1)(finish)

    scratch = []
    if exch:
        scratch = [pltpu.SemaphoreType.DMA((N_SEM * n_ex,)), pltpu.SemaphoreType.DMA((N_SEM * n_ex,)),
                   pltpu.SemaphoreType.DMA((n_ex,))]
    whole = pl.BlockSpec((s, 2 * LANES), lambda b, j, kb_i: (b, j))
    tile = pl.BlockSpec((bt, 2 * LANES), lambda b, j, kb_i: (b * nk + kb_i, j))
    shp = jax.ShapeDtypeStruct((t, 8 * LANES), F32)
    return pl.pallas_call(
        body, name="fox_bwd", grid=(nb, 4, nk),
        in_specs=[whole, whole, tile, tile] + [ANY] * n_ex,
        out_specs=[whole, tile, tile] + [ANY] * n_ex,
        out_shape=[shp, shp, shp] + [jax.ShapeDtypeStruct(e.shape, e.dtype) for e in exch],
        scratch_shapes=scratch, compiler_params=_params(("arbitrary", "arbitrary", "arbitrary")),
    )(ql_aug, do_aug, k_aug, v_aug, *exch)


def _mlp_fwd(x2, ma, mb, tgt, wo_a, wo_b, g2, w_up, w_down, tm):
    t = x2.shape[0]

    def body(x_ref, ma_ref, mb_ref, tg_ref, woa_ref, wob_ref, g2_ref, wu_ref, wd_ref,
             h_ref, hn_ref, hid_ref, dy_ref, dyb_ref, loss_ref):
        @pl.when(pl.program_id(0) == 0)
        def _():
            loss_ref[...] = jnp.zeros_like(loss_ref)

        h = (x_ref[...] + jnp.dot(ma_ref[...], woa_ref[...], preferred_element_type=F32)
             + jnp.dot(mb_ref[...], wob_ref[...], preferred_element_type=F32))
        h_ref[...] = h
        r = lax.rsqrt(jnp.mean(h * h, axis=-1, keepdims=True) + EPS)
        hn = (h * r * g2_ref[...]).astype(BF16)
        hn_ref[...] = hn
        u = jnp.maximum(jnp.dot(hn, wu_ref[...], preferred_element_type=F32), 0.0)
        hid = (u * u).astype(BF16)
        hid_ref[...] = hid
        y = h + jnp.dot(hid, wd_ref[...], preferred_element_type=F32)
        err = y - tg_ref[...]
        dy = err * (1.0 / D_MODEL)
        dy_ref[...] = dy
        dyb_ref[...] = dy.astype(BF16)
        part =0.5 * jnp.sum(jnp.sum(err * err, axis=1, keepdims=True) * (1.0 / D_MODEL), axis=0, keepdims=True)
        loss_ref[...] += part

    def tile(w):
        return pl.BlockSpec((tm, w), lambda i: (i, 0))

    return pl.pallas_call(
        body, name="mlp_fwd", grid=(t // tm,),
        in_specs=[tile(D_MODEL), tile(512), tile(512), tile(D_MODEL), _const_spec((512, D_MODEL)), _const_spec((512, D_MODEL)),
                  _const_spec((1, D_MODEL)), _const_spec((D_MODEL, D_FF)), _const_spec((D_FF, D_MODEL))],
        out_specs=[tile(D_MODEL), tile(D_MODEL), tile(D_FF), tile(D_MODEL), tile(D_MODEL),
                   pl.BlockSpec((8, LANES), lambda i: (0, 0))],
        out_shape=[jax.ShapeDtypeStruct((t, D_MODEL), F32), jax.ShapeDtypeStruct((t, D_MODEL), BF16),
                   jax.ShapeDtypeStruct((t, D_FF), BF16), jax.ShapeDtypeStruct((t, D_MODEL), F32),
                   jax.ShapeDtypeStruct((t, D_MODEL), BF16), jax.ShapeDtypeStruct((8, LANES), F32)],
        compiler_params=_params(("arbitrary",)),
    )(x2, ma, mb, tgt, wo_a, wo_b, g2, w_up, w_down)


def _mlp_bwd(dy, hid, h, ma, mb, w_down_t, w_up_t, w_out_t, g2, tm):
    t = dy.shape[0]

    def body(dy_ref, hid_ref, h_ref, ma_ref, mb_ref, wdt_ref, wut_ref, wot_ref, g2_ref,
             du_ref, dh_ref, dhb_ref, dma_ref, dob_ref, dla_ref, gg_ref):
        @pl.when(pl.program_id(0) == 0)
        def _():
            gg_ref[...] = jnp.zeros_like(gg_ref)

        dy = dy_ref[...]
        d_hid = jnp.dot(dy.astype(BF16), wdt_ref[...], preferred_element_type=F32)
        du = (d_hid * (2.0 * jnp.sqrt(hid_ref[...].astype(F32)))).astype(BF16)
        du_ref[...] = du
        d_hn = jnp.dot(du, wut_ref[...], preferred_element_type=F32)
        h = h_ref[...]
        r = lax.rsqrt(jnp.mean(h * h, axis=-1, keepdims=True) + EPS)
        hat = h * r
        gd = d_hn * g2_ref[...]
        dh = dy + r * (gd - hat * jnp.mean(gd * hat, axis=-1, keepdims=True))
        gg_ref[...] += jnp.sum(d_hn * hat, axis=0, keepdims=True)
        dh_ref[...] = dh
        dhb = dh.astype(BF16)
        dhb_ref[...] = dhb
        dm = jnp.dot(dhb, wot_ref[...], preferred_element_type=F32).astype(BF16)
        dma, dmb = dm[:, 0:512], dm[:, 512:1024]
        dma_ref[...] = dma
        sel = (lax.shift_right_logical(lax.broadcasted_iota(jnp.int32, (512, LANES), 0), 6)
               == lax.broadcasted_iota(jnp.int32, (512, LANES), 1)).astype(BF16)
        dla_ref[...] = _split_dot(dma.astype(F32) * ma_ref[...].astype(F32), sel)
        dmb32 = dmb.astype(F32)
        dlb = _split_dot(dmb32 * mb_ref[...].astype(F32), sel)
        for hd in range(8):
            blk = _head_block(dmb32[:, LANES * (hd // 2):LANES * (hd // 2 + 1)], hd % 2)
            dob_ref[:, LANES * hd:LANES * (hd + 1)] = _put3(blk, L_DELTA, -dlb[:, hd:hd + 1]).astype(BF16)

    def tile(w):
        return pl.BlockSpec((tm, w), lambda i: (i, 0))

    return pl.pallas_call(
        body, name="mlp_bwd", grid=(t // tm,),
        in_specs=[tile(D_MODEL), tile(D_FF), tile(D_MODEL), tile(512), tile(512), _const_spec((D_MODEL, D_FF)),
                  _const_spec((D_FF, D_MODEL)), _const_spec((D_MODEL, D_MODEL)), _const_spec((1, D_MODEL))],
        out_specs=[tile(D_FF), tile(D_MODEL), tile(D_MODEL), tile(512), tile(8 * LANES), tile(LANES),
                   pl.BlockSpec((1, D_MODEL), lambda i: (0, 0))],
        out_shape=[jax.ShapeDtypeStruct((t, D_FF), BF16), jax.ShapeDtypeStruct((t, D_MODEL), F32),
                   jax.ShapeDtypeStruct((t, D_MODEL), BF16), jax.ShapeDtypeStruct((t, 512), BF16),
                   jax.ShapeDtypeStruct((t, 8 * LANES), BF16), jax.ShapeDtypeStruct((t, LANES), F32),
                   jax.ShapeDtypeStruct((1, D_MODEL), F32)],
        compiler_params=_params(("arbitrary",)),
    )(dy, hid, h, ma, mb, w_down_t, w_up_t, w_out_t, g2)


def _wgrad(a, b, name, bm, bn, tk, out_dtype=F32, col_blocks=False):
    t, m = a.shape
    n = b.shape[1]
    bm, bn = min(bm, m), min(bn, n)
    nk = t // tk

    def body(a_ref, b_ref, o_ref, acc):
        @pl.when(pl.program_id(2) == 0)
        def _():
            acc[...] = jnp.zeros_like(acc)

        acc[...] += _tn(a_ref[...], b_ref[...])

        @pl.when(pl.program_id(2) == nk - 1)
        def _():
            o_ref[...] = acc[...].astype(out_dtype)

    if col_blocks:
        out_spec = pl.BlockSpec((None, bm, bn), lambda i, j, k: (j, i, 0))
        out_shape = jax.ShapeDtypeStruct((n // bn, m, bn), out_dtype)
    else:
        out_spec = pl.BlockSpec((bm, bn), lambda i, j, k: (i, j))
        out_shape = jax.ShapeDtypeStruct((m, n), out_dtype)
    return pl.pallas_call(
        body, name=name, grid=(m // bm, n // bn, nk),
        in_specs=[pl.BlockSpec((tk, bm), lambda i, j, k: (k, i)), pl.BlockSpec((tk, bn), lambda i, j, k: (k, j))],
        out_specs=out_spec, out_shape=out_shape, scratch_shapes=[pltpu.VMEM((bm, bn), F32)],
        compiler_params=_params(("arbitrary", "arbitrary", "arbitrary")),
    )(a, b)


def _proj_bwd(raw, dqa, dkae, dvae, dqb, dkb, dvb, dfl, x2, dh, w_main_t, w_f_t, g1, gqa, gka, gqb, gkb, tm):
    t = x2.shape[0]

    def body(raw_ref, dqa_ref, dkae_ref, dvae_ref, dqb_ref, dkb_ref, dvb_ref, dfl_ref, x_ref, dh_ref,
             wmt_ref, wft_ref, g1_ref, gqa_ref, gka_ref, gqb_ref, gkb_ref,
             dx_ref, dp_ref, dfb_ref, ggqa_ref, ggka_ref, ggqb_ref, ggkb_ref, gg1_ref):
        @pl.when(pl.program_id(0) == 0)
        def _():
            for r in (ggqa_ref, ggka_ref, ggqb_ref, ggkb_ref, gg1_ref):
                r[...] = jnp.zeros_like(r)

        raw = raw_ref[...]
        d_qa, p_qa = _head_norm_bwd(raw[:, 0:512], gqa_ref[...], dqa_ref[...])
        d_ka, p_ka = _head_norm_bwd(raw[:, 512:640], gka_ref[...], _fold_kv(dkae_ref[...]))
        d_va = _fold_kv(dvae_ref[...])
        d_qb, p_qb = _head_norm_bwd(raw[:, 768:1280], gqb_ref[...], _to_pairs(dqb_ref) * SCALE)
        d_kb, p_kb = _head_norm_bwd(raw[:, 1280:1792], gkb_ref[...], _to_pairs(dkb_ref) * (1.0 / LOG2E))
        ggqa_ref[...] += jnp.sum(p_qa, axis=0, keepdims=True)
        ggka_ref[...] += jnp.sum(p_ka, axis=0, keepdims=True)
        ggqb_ref[...] += jnp.sum(p_qb, axis=0, keepdims=True)
        ggkb_ref[...] += jnp.sum(p_kb, axis=0, keepdims=True)
        dproj = jnp.concatenate([d_qa, d_ka, d_va, d_qb, d_kb, _to_pairs(dvb_ref)], axis=1).astype(BF16)
        dp_ref[...] = dproj
        dfb = dfl_ref[...].astype(BF16)
        dfb_ref[...] = dfb
        d_xn = (jnp.dot(dproj, wmt_ref[...], preferred_element_type=F32)
                + jnp.dot(dfb, wft_ref[...], preferred_element_type=F32))
        x = x_ref[...]
        r = lax.rsqrt(jnp.mean(x * x, axis=-1, keepdims=True) + EPS)
        hat = x * r
        gd = d_xn * g1_ref[...]
        dx_ref[...] = dh_ref[...] + r * (gd - hat * jnp.mean(gd * hat, axis=-1, keepdims=True))
        gg1_ref[...] += jnp.sum(d_xn * hat, axis=0, keepdims=True)

    def tile(w):
        return pl.BlockSpec((tm, w), lambda i: (i, 0))

    def acc(w):
        return pl.BlockSpec((1, w), lambda i: (0, 0))

    return pl.pallas_call(
        body, name="proj_bwd", grid=(t // tm,),
        in_specs=[tile(MAIN_W), tile(512), tile(512), tile(512), tile(8 * LANES), tile(8 * LANES), tile(8 * LANES), tile(LANES),
                  tile(D_MODEL), tile(D_MODEL), _const_spec((MAIN_W, D_MODEL)), _const_spec((LANES, D_MODEL)),
                  _const_spec((1, D_MODEL)), _const_spec((1, 512)), _const_spec((1, 128)), _const_spec((1, 512)),
                  _const_spec((1, 512))],
        out_specs=[tile(D_MODEL), tile(MAIN_W), tile(LANES), acc(512), acc(128), acc(512), acc(512), acc(D_MODEL)],
        out_shape=[jax.ShapeDtypeStruct((t, D_MODEL), F32), jax.ShapeDtypeStruct((t, MAIN_W), BF16),
                   jax.ShapeDtypeStruct((t, LANES), BF16), jax.ShapeDtypeStruct((1, 512), F32),
                   jax.ShapeDtypeStruct((1, 128), F32), jax.ShapeDtypeStruct((1, 512), F32),
                   jax.ShapeDtypeStruct((1, 512), F32), jax.ShapeDtypeStruct((1, D_MODEL), F32)],
        compiler_params=_params(("arbitrary",)),
    )(raw, dqa, dkae, dvae, dqb, dkb, dvb, dfl, x2, dh, w_main_t, w_f_t, g1, gqa, gka, gqb, gkb)


def _pair_rows(a, nb, s):
    two = jnp.stack([a[:, :, 0], a[:, :, 64]], axis=1)
    return jnp.transpose(two.reshape(4, 2, nb, s), (2, 0, 1, 3))


def _head_rows(a, nb, s):
    return jnp.transpose(a[:, 0:8].reshape(nb, s, 4, 2), (0, 2, 3, 1))


R_REST = 128 + 512 + 512
IN_PAD = 304


def _local_step(x, tgt, w_in_t, rest, g1, b_forget, qna, kna, sinks, qnb, knb, g2,
                tm=256, bt=512, tq=512, ts=256, wk=2048, distributed=False):
    nb, s, _ = x.shape
    t = nb * s
    x2, tgt2 = x.reshape(t, D_MODEL), tgt.reshape(t, D_MODEL)
    g1r, g2r = g1.reshape(1, D_MODEL), g2.reshape(1, D_MODEL)
    gqa, gka = jnp.tile(qna, 8).reshape(1, 512), jnp.tile(kna, 2).reshape(1, 128)
    gqb, gkb = jnp.tile(qnb, 8).reshape(1, 512), jnp.tile(knb, 8).reshape(1, 512)
    bf_row = jnp.pad(b_forget, (0, LANES - 8)).reshape(1, LANES)
    sink_row = jnp.pad(sinks, (0, LANES - 8)).reshape(1, LANES)
    w_main_t = w_in_t[0:MAIN_W]
    w_f_t = jnp.pad(w_in_t[MAIN_W:IN_W], ((0, LANES - 8), (0, 0)))
    w_main, w_f = w_main_t.T, w_f_t.T

    xn, raw, fl, qa, kae, vae, qb, kb, vb = _norm_proj(x2, g1r, w_main, w_f, gqa, gka, gqb, gkb, tm)
    c_col = _gate_cumsum(fl, bf_row, nb, s, ts)
    q_aug, k_aug, v_aug = _fox_prep(qb, kb, vb, c_col, 2 * tm)
    ma, lse_a = _swa_fwd(qa, kae, vae, sink_row, nb, s, tq)
    if distributed:
        mb, ql_aug, full = _fox_fwd(q_aug, k_aug, v_aug, nb, s, bt, shard=rest)
        w_out = full[:, 0:128].reshape(D_MODEL, D_MODEL)
        w_up = jnp.transpose(full[:, 128:640].reshape(N_DEV, D_MODEL, 512), (1, 0, 2)).reshape(D_MODEL, D_FF)
        w_down = full[:, 640:R_REST].reshape(D_FF, D_MODEL)
    else:
        mb, ql_aug = _fox_fwd(q_aug, k_aug, v_aug, nb, s, bt)
        w_out, w_up, w_down = rest
    h, hn, hid, dy, dyb, loss_acc = _mlp_fwd(x2, ma, mb, tgt2, w_out[0:512], w_out[512:1024], g2r, w_up, w_down, tm)

    du, dh, dhb, dma, do_aug, dla, gg2 = _mlp_bwd(dy, hid, h, ma, mb, w_down.T, w_up.T, w_out.T, g2r, tm)
    g_down = _wgrad(hid, dyb, "wgrad_down", 512, 1024, wk, BF16).reshape(N_DEV, 512, D_MODEL)
    g_up = _wgrad(hn, du, "wgrad_up", 1024, 512, wk, BF16, col_blocks=True)
    g_out = jnp.concatenate([_wgrad(ma, dhb, "wgrad_out_a", 512, 1024, wk, BF16),
                             _wgrad(mb, dhb, "wgrad_out_b", 512, 1024, wk, BF16)], axis=0).reshape(N_DEV, 128, D_MODEL)

    dqa, dkae, dvae, dsink = _swa_bwd(qa, kae, vae, dma, sink_row, _pair_rows(lse_a, nb, s), _head_rows(dla, nb, s), nb, s, tq)
    fox = _fox_bwd(ql_aug, k_aug, v_aug, do_aug, nb, s, bt, exch=(g_out, g_up, g_down) if distributed else ())
    dqb, dkb, dvb = fox[:3]
    if distributed:
        g_out, g_up, g_down = fox[3:]
    dfl, gbf = _gate_cumsum_bwd(dqb, dkb, fl, bf_row, nb, s, ts)
    grad_x, dproj, dfb, ggqa, ggka, ggqb, ggkb, gg1 = _proj_bwd(
        raw, dqa, dkae, dvae, dqb, dkb, dvb, dfl, x2, dh, w_main_t, w_f_t, g1r, gqa, gka, gqb, gkb, tm)
    g_in_t = jnp.concatenate([_wgrad(dproj, xn, "wgrad_in", 768, 1024, wk), _wgrad(dfb, xn, "wgrad_gate", 128, 1024, wk)[0:8]],
                             axis=0)

    small = (gg1.reshape(D_MODEL), gbf[0, 0:8], ggqa.reshape(8, 64).sum(0), ggka.reshape(2, 64).sum(0),
             dsink.sum(0)[:, 0:2, 0].reshape(8), ggqb.reshape(8, 64).sum(0), ggkb.reshape(8, 64).sum(0),
             gg2.reshape(D_MODEL))
    return loss_acc[0, 0], grad_x.reshape(nb, s, D_MODEL), g_in_t, g_out, g_up, g_down, small


def _all_gather(shard):
    def body(x_ref, out_ref, send_sems, recv_sems, local_sem):
        start, forward, finish = _gather_steps(x_ref, out_ref, send_sems, recv_sems, local_sem)
        start()
        forward()
        finish()

    return pl.pallas_call(
        body, name="gather_w_in", out_shape=jax.ShapeDtypeStruct((N_DEV,) + shard.shape, shard.dtype),
        in_specs=[ANY], out_specs=ANY,
        scratch_shapes=[pltpu.SemaphoreType.DMA((N_SEM,)), pltpu.SemaphoreType.DMA((N_SEM,)), pltpu.SemaphoreType.DMA(())],
    )(shard)


def _exchange(*arrays):
    n_ex = len(arrays)

    def body(*refs):
        start, finish = _exchange_steps(list(zip(refs[:n_ex], refs[n_ex:2 * n_ex])), *refs[2 * n_ex:])
        start()
        finish()

    return pl.pallas_call(
        body, name="exchange_tail", out_shape=[jax.ShapeDtypeStruct(a.shape, a.dtype) for a in arrays],
        in_specs=[ANY] * n_ex, out_specs=[ANY] * n_ex,
        scratch_shapes=[pltpu.SemaphoreType.DMA((N_SEM * n_ex,)), pltpu.SemaphoreType.DMA((N_SEM * n_ex,)),
                        pltpu.SemaphoreType.DMA((n_ex,))],
    )(*arrays)


def _sum_adamw(recv, w, m, v, tr, name):
    _, r, n = recv.shape

    def body(r_ref, w_ref, m_ref, v_ref, g_ref, d_ref, nm_ref, nv_ref):
        g = r_ref[0].astype(F32)
        for s in range(1, N_DEV):
            g = g + r_ref[s].astype(F32)
        g_ref[...] = g
        nm = ADAM_B1 * m_ref[...] + (1.0 - ADAM_B1) * g
        nv = ADAM_B2 * v_ref[...] + (1.0 - ADAM_B2) * (g * g)
        m_hat = nm / (1.0 - ADAM_B1 ** ADAM_STEP)
        v_hat = nv / (1.0 - ADAM_B2 ** ADAM_STEP)
        d_ref[...] = -ADAM_LR * (m_hat / (jnp.sqrt(v_hat) + ADAM_EPS) + ADAM_WD * w_ref[...])
        nm_ref[...] = nm
        nv_ref[...] = nv

    tile = pl.BlockSpec((tr, n), lambda i: (i, 0))
    shp = jax.ShapeDtypeStruct((r, n), F32)
    return pl.pallas_call(
        body, name=name, grid=(r // tr,),
        in_specs=[pl.BlockSpec((N_DEV, tr, n), lambda i: (0, i, 0)), tile, tile, tile],
        out_specs=[tile, tile, tile, tile], out_shape=[shp, shp, shp, shp],
        compiler_params=_params(("arbitrary",)),
    )(recv, w, m, v)


def _small_rows(g1, bf, qna, kna, sk, qnb, knb, g2):
    row2 = jnp.concatenate([bf, qna, kna, sk, qnb, knb])
    return jnp.zeros((8, D_MODEL), F32).at[0].set(g1).at[1].set(g2).at[2, 0:row2.shape[0]].set(row2)


def _in_rows(w_in_s):
    return jnp.pad(w_in_s.T, ((0, IN_PAD - IN_SHARD), (0, 0)))


def kernel(x, attn_norm_g, w_in, b_forget, q_norm_a, k_norm_a, sink_logits, q_norm_b, k_norm_b, w_out, mlp_norm_g, w_up, w_down, loss_target, m_attn_norm_g, m_w_in, m_b_forget, m_q_norm_a, m_k_norm_a, m_sink_logits, m_q_norm_b, m_k_norm_b, m_w_out, m_mlp_norm_g, m_w_up, m_w_down, v_attn_norm_g, v_w_in, v_b_forget, v_q_norm_a, v_k_norm_a, v_sink_logits, v_q_norm_b, v_k_norm_b, v_w_out, v_mlp_norm_g, v_w_up, v_w_down):
    w_in_r = _in_rows(w_in)
    w_in_t = _all_gather(w_in_r.astype(BF16))[:, 0:IN_SHARD].reshape(IN_W, D_MODEL)
    rest = jnp.concatenate([w_out, w_up.reshape(512, D_MODEL), w_down], axis=0).astype(BF16)

    loss_part, grad_x, g_in_t, r_out, r_up, r_down, small = _local_step(
        x, loss_target, w_in_t, rest, attn_norm_g, b_forget, q_norm_a, k_norm_a, sink_logits, q_norm_b, k_norm_b, mlp_norm_g,
        distributed=True)

    g_in_blocks = jnp.pad(g_in_t.reshape(N_DEV, IN_SHARD, D_MODEL), ((0, 0), (0, IN_PAD - IN_SHARD), (0, 0))).astype(BF16)
    small_blocks = jnp.broadcast_to(_small_rows(*small), (N_DEV, 8, D_MODEL))
    r_in, r_small = _exchange(g_in_blocks, small_blocks)

    small_w = _small_rows(attn_norm_g, b_forget, q_norm_a, k_norm_a, sink_logits, q_norm_b, k_norm_b, mlp_norm_g)
    small_m = _small_rows(m_attn_norm_g, m_b_forget, m_q_norm_a, m_k_norm_a, m_sink_logits, m_q_norm_b, m_k_norm_b, m_mlp_norm_g)
    small_v = _small_rows(v_attn_norm_g, v_b_forget, v_q_norm_a, v_k_norm_a, v_sink_logits, v_q_norm_b, v_k_norm_b, v_mlp_norm_g)
    o_in = [a[0:IN_SHARD].T for a in _sum_adamw(r_in, w_in_r, _in_rows(m_w_in), _in_rows(v_w_in), IN_PAD, "adamw_in")]
    o_out = _sum_adamw(r_out, w_out, m_w_out, v_w_out, 128, "adamw_out")
    o_up = _sum_adamw(r_up, w_up, m_w_up, v_w_up, 256, "adamw_up")
    o_down = _sum_adamw(r_down, w_down, m_w_down, v_w_down, 128, "adamw_down")
    o_small = _sum_adamw(r_small, small_w, small_m, small_v, 8, "adamw_small")

    def leaves(i):
        row2 = o_small[i][2]
        return (o_small[i][0], o_in[i], row2[0:8], row2[8:72], row2[72:136], row2[136:144], row2[144:208], row2[208:272],
                o_out[i], o_small[i][1], o_up[i], o_down[i])

    loss = lax.psum(loss_part, ("x", "y", "c"))
    return (loss, grad_x, *leaves(0), *leaves(1), *leaves(2), *leaves(3))
```

```python
import functools
import math

import jax
import jax.numpy as jnp
from jax import lax
from jax.experimental import pallas as pl
from jax.experimental.pallas import tpu as pltpu

F32 = jnp.float32
BF16 = jnp.bfloat16

D_MODEL = 1024
HEAD_DIM = 64
N_DEV = 8
D_FF = 4096
A_QW = 512
A_KVW = 128
B_W = 512
MAIN_W = 2304
IN_W = 2312
WINDOW = 128
EPS = 1e-6
SCALE = 0.125
LOG2E = 1.4426950408889634
LANES = 128
NEG_INF = float("-inf")

ADAM_LR = 0.001
ADAM_B1 = 0.9
ADAM_B2 = 0.999
ADAM_EPS = 1e-08
ADAM_WD = 0.01
ADAM_STEP = 10

R_OUT, R_UP, R_DOWN, R_IN = 0, 128, 640, 1152
IN_SHARD = 289
R_SMALL = 1456
R_PACK = 1472
VMEM_LIMIT = 56 * 1024 * 1024


def _params(sem, vmem=VMEM_LIMIT):
    return pltpu.CompilerParams(dimension_semantics=sem, vmem_limit_bytes=vmem)


def _const_spec(shape):
    nd = len(shape)
    return pl.BlockSpec(shape, lambda *_: (0,) * nd, pipeline_mode=pl.Buffered(1))


def _lane(shape):
    return lax.broadcasted_iota(jnp.int32, shape, len(shape) - 1)


def _split_dot(v, mat):
    hi = v.astype(BF16)
    lo = (v - hi.astype(F32)).astype(BF16)
    return (jnp.dot(hi, mat, preferred_element_type=F32) + jnp.dot(lo, mat, preferred_element_type=F32))


def _head_ones(n):
    r = lax.shift_right_logical(lax.broadcasted_iota(jnp.int32, (n, n), 0), 6)
    c = lax.shift_right_logical(lax.broadcasted_iota(jnp.int32, (n, n), 1), 6)
    return (r == c).astype(BF16)


def _head_sum(v):
    w = v.shape[1]
    if w <= 256:
        return _split_dot(v, _head_ones(w))
    ones = _head_ones(256)
    return jnp.concatenate([_split_dot(v[:, s:s + 256], ones) for s in range(0, w, 256)], axis=1)


def _head_norm(seg, gain):
    rs = lax.rsqrt(_head_sum(seg * seg) * (1.0 / HEAD_DIM) + EPS)
    return seg * rs * gain


def _head_norm_bwd(seg, gain, d_out):
    rs = lax.rsqrt(_head_sum(seg * seg) * (1.0 / HEAD_DIM) + EPS)
    hat = seg * rs
    gd = d_out * gain
    d_seg = rs * (gd - hat * (_head_sum(gd * hat) * (1.0 / HEAD_DIM)))
    return d_seg, d_out * hat


def _expand_kv(v):
    r = pltpu.roll(v, 64, axis=1)
    lo = _lane(v.shape) < 64
    return jnp.concatenate([jnp.where(lo, v, r), jnp.where(lo, r, v)], axis=1)


def _fold_kv(e4):
    t0 = e4[:, 0:128] + e4[:, 128:256]
    t1 = e4[:, 256:384] + e4[:, 384:512]
    t0 = t0 + pltpu.roll(t0, 64, axis=1)
    t1 = t1 + pltpu.roll(t1, 64, axis=1)
    return jnp.where(_lane(t0.shape) < 64, t0, t1)


def _pick_lane(blk, idx):
    return jnp.sum(jnp.where(_lane(blk.shape) == idx, blk, 0.0), axis=1, keepdims=True)


def _nt(a, b):
    return lax.dot_general(a, b, (((1,), (1,)), ((), ())), preferred_element_type=F32)


def _tn(a, b):
    return lax.dot_general(a, b, (((0,), (0,)), ((), ())), preferred_element_type=F32)


def _norm_proj(x2, g1, w_main, w_f, gqa, gka, gqb, gkb, tm):
    t = x2.shape[0]

    def body(x_ref, g1_ref, wm_ref, wf_ref, gqa_ref, gka_ref, gqb_ref, gkb_ref,
             xn_ref, raw_ref, fl_ref, qa_ref, kae_ref, vae_ref, qb_ref, kb_ref, vb_ref):
        x = x_ref[...]
        r = lax.rsqrt(jnp.mean(x * x, axis=-1, keepdims=True) + EPS)
        xn = (x * r * g1_ref[...]).astype(BF16)
        xn_ref[...] = xn
        proj = jnp.dot(xn, wm_ref[...], preferred_element_type=F32)
        raw_ref[...] = proj
        fl_ref[...] = jnp.dot(xn, wf_ref[...], preferred_element_type=F32)
        qa_ref[...] = _head_norm(proj[:, 0:512], gqa_ref[...]).astype(BF16)
        kae_ref[...] = _expand_kv(_head_norm(proj[:, 512:640], gka_ref[...])).astype(BF16)
        vae_ref[...] = _expand_kv(proj[:, 640:768]).astype(BF16)
        qb_ref[...] = (_head_norm(proj[:, 768:1280], gqb_ref[...]) * (SCALE * LOG2E)).astype(BF16)
        kb_ref[...] = _head_norm(proj[:, 1280:1792], gkb_ref[...]).astype(BF16)
        vb_ref[...] = proj[:, 1792:2304].astype(BF16)

    def tile(w):
        return pl.BlockSpec((tm, w), lambda i: (i, 0))

    return pl.pallas_call(
        body, name="norm_proj", grid=(t // tm,),
        in_specs=[tile(D_MODEL), _const_spec((1, D_MODEL)), _const_spec((D_MODEL, MAIN_W)), _const_spec((D_MODEL, LANES)),
                  _const_spec((1, 512)), _const_spec((1, 128)), _const_spec((1, 512)), _const_spec((1, 512))],
        out_specs=[tile(D_MODEL), tile(MAIN_W), tile(LANES), tile(512), tile(256), tile(256), tile(512), tile(512), tile(512)],
        out_shape=[jax.ShapeDtypeStruct((t, D_MODEL), BF16), jax.ShapeDtypeStruct((t, MAIN_W), F32),
                   jax.ShapeDtypeStruct((t, LANES), F32), jax.ShapeDtypeStruct((t, 512), BF16),
                   jax.ShapeDtypeStruct((t, 256), BF16), jax.ShapeDtypeStruct((t, 256), BF16),
                   jax.ShapeDtypeStruct((t, 512), BF16), jax.ShapeDtypeStruct((t, 512), BF16),
                   jax.ShapeDtypeStruct((t, 512), BF16)],
        compiler_params=_params(("arbitrary",)),
    )(x2, g1, w_main, w_f, gqa, gka, gqb, gkb)


def _tri(n, upper):
    r = lax.broadcasted_iota(jnp.int32, (n, n), 0)
    c = lax.broadcasted_iota(jnp.int32, (n, n), 1)
    return ((c >= r) if upper else (c <= r)).astype(F32)


def _gate_cumsum(fl, bf_row, nb, s, ts):
    t = fl.shape[0]
    nt = s // ts

    def body(fl_ref, b_ref, c_ref, carry):
        @pl.when(pl.program_id(1) == 0)
        def _():
            carry[...] = jnp.zeros_like(carry)

        z = fl_ref[...] + b_ref[...]
        e = jnp.exp(-jnp.abs(z))
        u = 1.0 + e
        log1p = jnp.where(u == 1.0, e, jnp.log(u) * (e / (u - 1.0)))
        lf = jnp.minimum(z, 0.0) - log1p
        c_ref[...] = jnp.dot(_tri(ts, False), lf, precision=lax.Precision.HIGHEST, preferred_element_type=F32) + carry[...]
        carry[...] = c_ref[pl.ds(ts - 1, 1), :]

    return pl.pallas_call(
        body, name="gate_cumsum", grid=(nb, nt),
        in_specs=[pl.BlockSpec((ts, LANES), lambda b, i: (b * nt + i, 0)), _const_spec((1, LANES))],
        out_specs=pl.BlockSpec((ts, LANES), lambda b, i: (b * nt + i, 0)),
        out_shape=jax.ShapeDtypeStruct((t, LANES), F32),
        scratch_shapes=[pltpu.VMEM((1, LANES), F32)],
        compiler_params=_params(("arbitrary", "arbitrary")),
    )(fl, bf_row)


def _gate_cumsum_bwd(dq_raw, dk_raw, fl, bf_row, nb, s, ts):
    t = fl.shape[0]
    nt = s // ts

    def body(dq_ref, dk_ref, fl_ref, b_ref, df_ref, gb_ref, carry, dlf_ref):
        @pl.when(pl.program_id(1) == 0)
        def _():
            carry[...] = jnp.zeros_like(carry)

        @pl.when((pl.program_id(0) == 0) & (pl.program_id(1) == 0))
        def _():
            gb_ref[...] = jnp.zeros_like(gb_ref)

        lane = _lane((ts, LANES))
        dc = jnp.zeros((ts, LANES), F32)
        for h in range(8):
            col = dq_ref[:, LANES * h + L_CQ:LANES * h + L_CQ + 1] - dk_ref[:, LANES * h + L_CK:LANES * h + L_CK + 1]
            dc = jnp.where(lane == h, col, dc)
        dlf_ref[...] = jnp.dot(_tri(ts, True), dc, precision=lax.Precision.HIGHEST, preferred_element_type=F32) + carry[...]
        carry[...] = dlf_ref[pl.ds(0, 1), :]
        dlf = dlf_ref[...]
        z = fl_ref[...] + b_ref[...]
        df = dlf * (1.0 / (1.0 + jnp.exp(z)))
        df_ref[...] = df
        gb_ref[...] += jnp.sum(df, axis=0, keepdims=True)

    def rev(b, i):
        return (b * nt + (nt - 1 - i), 0)

    return pl.pallas_call(
        body, name="gate_cumsum_bwd", grid=(nb, nt),
        in_specs=[pl.BlockSpec((ts, 8 * LANES), rev), pl.BlockSpec((ts, 8 * LANES), rev), pl.BlockSpec((ts, LANES), rev),
                  _const_spec((1, LANES))],
        out_specs=[pl.BlockSpec((ts, LANES), rev), pl.BlockSpec((1, LANES), lambda b, i: (0, 0))],
        out_shape=[jax.ShapeDtypeStruct((t, LANES), F32), jax.ShapeDtypeStruct((1, LANES), F32)],
        scratch_shapes=[pltpu.VMEM((1, LANES), F32), pltpu.VMEM((ts, LANES), F32)],
        compiler_params=_params(("arbitrary", "arbitrary")),
    )(dq_raw, dk_raw, fl, bf_row)


def _slope(p, hh):
    out = jnp.float32(2.0 ** -(2 * 3 + hh + 1))
    for pp in (2, 1, 0):
        out = jnp.where(p == pp, jnp.float32(2.0 ** -(2 * pp + hh + 1)), out)
    return out


def _swa_windows(ref, i, tq):
    nsub = tq // WINDOW
    cur = ref[pl.ds(pl.multiple_of(i * tq, tq), tq), :].reshape(nsub, WINDOW, LANES)
    first = ref[pl.ds(pl.multiple_of(jnp.maximum(i * tq - WINDOW, 0), WINDOW), WINDOW), :].reshape(1, WINDOW, LANES)
    return jnp.concatenate([jnp.concatenate([first, cur[0:nsub - 1]], axis=0), cur], axis=1)


def _both_heads(x3, lo):
    zero = jnp.zeros_like(x3)
    return jnp.concatenate([jnp.where(lo, x3, zero), jnp.where(lo, zero, x3)], axis=0)


def _swa_head_consts(sink_ref, p, i, nsub):
    bidx = lax.broadcasted_iota(jnp.int32, (2 * nsub, 1, 1), 0)
    is_a = bidx < nsub
    slope = jnp.where(is_a, _slope(p, 0), _slope(p, 1))
    sinks = sink_ref[...]
    sink = jnp.where(is_a, _pick_lane(sinks, 2 * p).reshape(1, 1, 1), _pick_lane(sinks, 2 * p + 1).reshape(1, 1, 1))
    first = (i == 0) & ((bidx == 0) | (bidx == nsub))
    return slope, sink, first


def _swa_fwd(qa, kae, vae, sink_row, nb, s, tq):
    t = qa.shape[0]
    nq = s // tq
    nsub = tq // WINDOW

    def body(q_ref, k_ref, v_ref, sink_ref, o_ref, lse_ref):
        p, i = pl.program_id(1), pl.program_id(2)
        lo = _lane((1, 1, LANES)) < 64
        kk, vv = _swa_windows(k_ref, i, tq), _swa_windows(v_ref, i, tq)
        qs = (q_ref[...].astype(F32) * SCALE).astype(BF16).reshape(nsub, WINDOW, LANES)
        q8 = _both_heads(qs, lo)
        s8 = jnp.einsum("bqd,bkd->bqk", q8, jnp.concatenate([kk, kk], axis=0), preferred_element_type=F32)
        row = lax.broadcasted_iota(jnp.int32, (1, WINDOW, 2 * WINDOW), 1)
        col = lax.broadcasted_iota(jnp.int32, (1, WINDOW, 2 * WINDOW), 2)
        dist = row + WINDOW - col
        slope, sink, first = _swa_head_consts(sink_ref, p, i, nsub)
        valid = (dist >= 0) & (dist < WINDOW) & ((col >= WINDOW) | jnp.logical_not(first))
        s8 = jnp.where(valid, s8 - slope * dist.astype(F32), NEG_INF)
        m = jnp.maximum(jnp.max(s8, axis=2, keepdims=True), sink)
        e = jnp.exp(s8 - m)
        den = jnp.sum(e, axis=2, keepdims=True) + jnp.exp(sink - m)
        pr = (e / den).astype(BF16)
        o8 = jnp.einsum("bqk,bkd->bqd", pr, jnp.concatenate([vv, vv], axis=0), preferred_element_type=F32)
        lse8 = m + jnp.log(den)
        o_ref[...] = jnp.where(lo, o8[0:nsub], o8[nsub:]).astype(BF16).reshape(tq, LANES)
        lse_ref[...] = jnp.where(lo, lse8[0:nsub], lse8[nsub:]).reshape(tq, LANES)

    return pl.pallas_call(
        body, name="swa_fwd", grid=(nb, 4, nq),
        in_specs=[pl.BlockSpec((tq, LANES), lambda b, p, i: (b * nq + i, p)),
                  pl.BlockSpec((s, LANES), lambda b, p, i: (b, lax.shift_right_logical(p, 1))),
                  pl.BlockSpec((s, LANES), lambda b, p, i: (b, lax.shift_right_logical(p, 1))),
                  pl.BlockSpec((1, LANES), lambda b, p, i: (0, 0))],
        out_specs=[pl.BlockSpec((tq, LANES), lambda b, p, i: (b * nq + i, p)),
                   pl.BlockSpec((None, tq, LANES), lambda b, p, i: (p, b * nq + i, 0))],
        out_shape=[jax.ShapeDtypeStruct((t, 512), BF16), jax.ShapeDtypeStruct((4, t, LANES), F32)],
        compiler_params=_params(("arbitrary", "arbitrary", "arbitrary")),
    )(qa, kae, vae, sink_row)


def _swa_bwd(qa, kae, vae, do_a, sink_row, lse_rows, delta_rows, nb, s, tq):
    t = qa.shape[0]
    nq = s // tq
    nsub = tq // WINDOW

    def body(q_ref, do_ref, k_ref, v_ref, sink_ref, lse_ref, dl_ref, dq_ref, dk_ref, dv_ref, ds_ref):
        p, i = pl.program_id(1), pl.program_id(2)

        @pl.when(i == 0)
        def _():
            ds_ref[...] = jnp.zeros_like(ds_ref)

        lo = _lane((1, 1, LANES)) < 64
        kk, vv = _swa_windows(k_ref, i, tq), _swa_windows(v_ref, i, tq)
        kks = (kk.astype(F32) * SCALE).astype(BF16)
        k8, v8 = jnp.concatenate([kks, kks], axis=0), jnp.concatenate([vv, vv], axis=0)
        q8 = _both_heads(q_ref[...].reshape(nsub, WINDOW, LANES), lo)
        do8 = _both_heads(do_ref[...].reshape(nsub, WINDOW, LANES), lo)
        cur = pl.multiple_of(i * tq, tq)

        def stat(ref):
            return jnp.concatenate([ref[pl.ds(hh, 1), pl.ds(cur + u * WINDOW, WINDOW)].reshape(1, 1, WINDOW)
                                    for hh in range(2) for u in range(nsub)], axis=0)

        lse8, dl8 = stat(lse_ref), stat(dl_ref)
        row = lax.broadcasted_iota(jnp.int32, (1, 2 * WINDOW, WINDOW), 1)
        col = lax.broadcasted_iota(jnp.int32, (1, 2 * WINDOW, WINDOW), 2)
        dist = col + WINDOW - row
        slope, sink, first = _swa_head_consts(sink_ref, p, i, nsub)
        valid = (dist >= 0) & (dist < WINDOW) & ((row >= WINDOW) | jnp.logical_not(first))
        st = jnp.einsum("bkd,bqd->bkq", k8, q8, preferred_element_type=F32) - slope * dist.astype(F32) - lse8
        pt = jnp.where(valid, jnp.exp(jnp.where(valid, st, 0.0)), 0.0)
        dpt = jnp.einsum("bkd,bqd->bkq", v8, do8, preferred_element_type=F32)
        dst = pt * (dpt - dl8)
        ptb, dstb = pt.astype(BF16), dst.astype(BF16)
        dv8 = jnp.einsum("bkq,bqd->bkd", ptb, do8, preferred_element_type=F32)
        dk8 = jnp.einsum("bkq,bqd->bkd", dstb, q8, preferred_element_type=F32) * SCALE
        dq8 = jnp.einsum("bkq,bkd->bqd", dstb, k8, preferred_element_type=F32)
        dq_ref[...] = jnp.where(lo, dq8[0:nsub], dq8[nsub:]).reshape(tq, LANES)

        psd = jnp.exp(sink - lse8) * dl8
        row_h = lax.broadcasted_iota(jnp.int32, (8, LANES), 0)
        for hh in range(2):
            tot = jnp.sum(jnp.sum(psd[hh * nsub:(hh + 1) * nsub], axis=2, keepdims=True), axis=0, keepdims=True)
            ds_ref[...] += jnp.where(row_h == hh, -tot.reshape(1, 1), 0.0)

        prev = pl.multiple_of(jnp.maximum(i * tq - WINDOW, 0), WINDOW)
        for g8, g_ref in ((dk8, dk_ref), (dv8, dv_ref)):
            g4 = g8[0:nsub] + g8[nsub:]
            own, before = g4[:, WINDOW:, :], g4[:, 0:WINDOW, :]
            shifted = jnp.concatenate([before[1:nsub], jnp.zeros((1, WINDOW, LANES), F32)], axis=0)
            g_ref[pl.ds(cur, tq), :] = (own + shifted).reshape(tq, LANES)
            g_ref[pl.ds(prev, WINDOW), :] += before[0]

    rows = pl.BlockSpec((None, None, 2, s), lambda b, p, i: (b, p, 0, 0))
    return pl.pallas_call(
        body, name="swa_bwd", grid=(nb, 4, nq),
        in_specs=[pl.BlockSpec((tq, LANES), lambda b, p, i: (b * nq + i, p)),
                  pl.BlockSpec((tq, LANES), lambda b, p, i: (b * nq + i, p)),
                  pl.BlockSpec((s, LANES), lambda b, p, i: (b, lax.shift_right_logical(p, 1))),
                  pl.BlockSpec((s, LANES), lambda b, p, i: (b, lax.shift_right_logical(p, 1))),
                  pl.BlockSpec((1, LANES), lambda b, p, i: (0, 0)), rows, rows],
        out_specs=[pl.BlockSpec((tq, LANES), lambda b, p, i: (b * nq + i, p)),
                   pl.BlockSpec((s, LANES), lambda b, p, i: (b, p)),
                   pl.BlockSpec((s, LANES), lambda b, p, i: (b, p)),
                   pl.BlockSpec((None, None, 8, LANES), lambda b, p, i: (b, p, 0, 0))],
        out_shape=[jax.ShapeDtypeStruct((t, 512), F32), jax.ShapeDtypeStruct((t, 512), F32),
                   jax.ShapeDtypeStruct((t, 512), F32), jax.ShapeDtypeStruct((nb, 4, 8, LANES), F32)],
        compiler_params=_params(("arbitrary", "arbitrary", "arbitrary")),
    )(qa, do_a, kae, vae, sink_row, lse_rows, delta_rows)


MESH = pl.DeviceIdType.MESH
ANY = pl.BlockSpec(memory_space=pl.ANY)
N_SEM = 7


def _gather_steps(x_ref, out_ref, send_sems, recv_sems, local_sem):
    x, y, c = lax.axis_index("x"), lax.axis_index("y"), lax.axis_index("c")
    me, sibling = (x, y, c), (x, y, 1 - c)
    chips = [(1 - x, y), (x, 1 - y), (1 - x, 1 - y)]

    def slot(px, py, pc):
        return out_ref.at[4 * px + 2 * py + pc]

    def copy(k, block, to, src=None):
        return pltpu.make_async_remote_copy(
            src_ref=slot(*block) if src is None else src, dst_ref=slot(*block),
            send_sem=send_sems.at[k], recv_sem=recv_sems.at[k], device_id=to, device_id_type=MESH)

    mine = pltpu.make_async_copy(x_ref, slot(*me), local_sem)
    first = [copy(0, me, sibling, src=x_ref)] + [copy(1 + j, me, (*chip, c), src=x_ref) for j, chip in enumerate(chips)]
    passed = [copy(4 + j, (*chip, c), sibling) for j, chip in enumerate(chips)]

    def start():
        mine.start()
        for cp in first:
            cp.start()

    def forward():
        for j, chip in enumerate(chips):
            copy(1 + j, (*chip, c), me).wait_recv()
            passed[j].start()

    def finish():
        copy(0, sibling, me).wait_recv()
        for j, chip in enumerate(chips):
            copy(4 + j, (*chip, 1 - c), me).wait_recv()
        for cp in first + passed:
            cp.wait_send()
        mine.wait()

    return start, forward, finish


def _exchange_steps(pairs, send_sems, recv_sems, local_sems):
    x, y, c = lax.axis_index("x"), lax.axis_index("y"), lax.axis_index("c")
    my_id = 4 * x + 2 * y + c
    local, remote = [], []
    for a, (src, dst) in enumerate(pairs):
        local.append(pltpu.make_async_copy(src.at[my_id], dst.at[my_id], local_sems.at[a]))
        for k in range(1, N_DEV):
            px = 1 - x if k & 4 else x
            py = 1 - y if k & 2 else y
            pc = 1 - c if k & 1 else c
            remote.append(pltpu.make_async_remote_copy(
                src_ref=src.at[4 * px + 2 * py + pc], dst_ref=dst.at[my_id],
                send_sem=send_sems.at[N_SEM * a + k - 1], recv_sem=recv_sems.at[N_SEM * a + k - 1],
                device_id=(px, py, pc), device_id_type=MESH))

    def start():
        for cp in local + remote:
            cp.start()

    def finish():
        for cp in remote:
            cp.wait_recv()
        for cp in remote:
            cp.wait_send()
        for cp in local:
            cp.wait()

    return start, finish


L_ONE = 64
L_CK = 65
L_CQ = 68
L_LSE = 71
L_DELTA = 74


def _head_block(pair, half):
    y = pair if half == 0 else pltpu.roll(pair, 64, axis=1)
    return jnp.where(_lane(pair.shape) < 64, y, 0.0)


def _put3(blk, lane0, col):
    lane = _lane(blk.shape)
    hi = col.astype(BF16).astype(F32)
    mid = (col - hi).astype(BF16).astype(F32)
    lo = (col - hi - mid).astype(BF16).astype(F32)
    return jnp.where(lane == lane0, hi, jnp.where(lane == lane0 + 1, mid, jnp.where(lane == lane0 + 2, lo, blk)))


def _put_ones(blk, lanes):
    lane = _lane(blk.shape)
    hit = functools.reduce(jnp.logical_or, [lane == ln for ln in lanes])
    return jnp.where(hit, 1.0, blk)


def _to_pairs(ref):
    out = []
    for j in range(4):
        a, b = ref[:, 2 * LANES * j:2 * LANES * j + LANES], ref[:, 2 * LANES * j + LANES:2 * LANES * (j + 1)]
        out.append(jnp.where(_lane(a.shape) < 64, a, pltpu.roll(b, 64, axis=1)))
    return jnp.concatenate(out, axis=1)


def _fox_prep(qb, kb, vb, c_col, tm):
    t = qb.shape[0]

    def body(q_ref, k_ref, v_ref, c_ref, qo_ref, ko_ref, vo_ref):
        c2 = c_ref[...] * LOG2E
        for h in range(8):
            j, half = h // 2, h % 2
            pair, blk = slice(LANES * j, LANES * (j + 1)), slice(LANES * h, LANES * (h + 1))
            ch = c2[:, h:h + 1]
            q = _put_ones(_head_block(q_ref[:, pair].astype(F32), half), (L_CK, L_CK + 1, L_CK + 2))
            qo_ref[:, blk] = _put3(q, L_CQ, ch).astype(BF16)
            k = _put_ones(_head_block(k_ref[:, pair].astype(F32), half), tuple(range(L_CQ, L_CQ + 6)))
            ko_ref[:, blk] = _put3(k, L_CK, -ch).astype(BF16)
            v = _head_block(v_ref[:, pair].astype(F32), half)
            vo_ref[:, blk] = _put_ones(v, (L_ONE, L_DELTA, L_DELTA + 1, L_DELTA + 2)).astype(BF16)

    def tile(w):
        return pl.BlockSpec((tm, w), lambda i: (i, 0))

    shp = jax.ShapeDtypeStruct((t, 8 * LANES), BF16)
    return pl.pallas_call(
        body, name="fox_prep", grid=(t // tm,), in_specs=[tile(512), tile(512), tile(512), tile(LANES)],
        out_specs=[tile(8 * LANES)] * 3, out_shape=[shp, shp, shp], compiler_params=_params(("arbitrary",)),
    )(qb, kb, vb, c_col)


def _fox_fwd(q_aug, k_aug, v_aug, nb, s, bt, shard=None):
    t = q_aug.shape[0]
    nq = s // bt
    n_in = 3

    def body(*refs):
        q_ref, k_ref, v_ref = refs[:n_in]
        if shard is None:
            o_ref, ql_ref = refs[n_in:]
        else:
            x_ref, o_ref, ql_ref, full_ref, send_sems, recv_sems, local_sem = refs[n_in:]
            start, forward, finish = _gather_steps(x_ref, full_ref, send_sems, recv_sems, local_sem)
            step = (pl.program_id(0) * 4 + pl.program_id(1)) * nq + pl.program_id(2)
            pl.when(step == 0)(start)
            pl.when(step == nb * 2 * nq)(forward)
        i = pl.program_id(2)
        row = lax.broadcasted_iota(jnp.int32, (bt, bt), 0)
        col = lax.broadcasted_iota(jnp.int32, (bt, bt), 1)
        sls = [slice(LANES * hh, LANES * (hh + 1)) for hh in range(2)]
        qhs = [q_ref[:, sl] for sl in sls]

        def blk(kb_i, carry, diag):
            start = pl.multiple_of(kb_i * bt, bt)
            new = []
            for (m, acc), qh, sl in zip(carry, qhs, sls):
                sc = _nt(qh, k_ref[pl.ds(start, bt), sl])
                if diag:
                    sc = jnp.where(row >= col, sc, NEG_INF)
                m_new = jnp.maximum(m, jnp.max(sc, axis=1, keepdims=True))
                pr = jnp.exp2(sc - m_new).astype(BF16)
                acc = jnp.exp2(m - m_new) * acc + jnp.dot(pr, v_ref[pl.ds(start, bt), sl], preferred_element_type=F32)
                new.append((m_new, acc))
            return tuple(new)

        init = tuple((jnp.full((bt, 1), NEG_INF, F32), jnp.zeros((bt, LANES), F32)) for _ in range(2))
        carry = lax.fori_loop(0, i, lambda kb_i, c: blk(kb_i, c, False), init)
        outs = []
        for (m, acc), qh, sl in zip(blk(i, carry, True), qhs, sls):
            l = acc[:, L_ONE:L_ONE + 1]
            outs.append(acc / l)
            ql_ref[:, sl] = _put3(qh.astype(F32), L_LSE, -(m + jnp.log(l) * LOG2E)).astype(BF16)
        o_ref[...] = jnp.where(_lane((1, LANES)) < 64, outs[0], pltpu.roll(outs[1], 64, axis=1)).astype(BF16)
        if shard is not None:
            pl.when(step == nb * 4 * nq - 1)(finish)

    in_specs = [pl.BlockSpec((bt, 2 * LANES), lambda b, j, i: (b * nq + i, j)),
                pl.BlockSpec((s, 2 * LANES), lambda b, j, i: (b, j)),
                pl.BlockSpec((s, 2 * LANES), lambda b, j, i: (b, j))]
    out_specs = [pl.BlockSpec((bt, LANES), lambda b, j, i: (b * nq + i, j)),
                 pl.BlockSpec((bt, 2 * LANES), lambda b, j, i: (b * nq + i, j))]
    out_shape = [jax.ShapeDtypeStruct((t, 512), BF16), jax.ShapeDtypeStruct((t, 8 * LANES), BF16)]
    args, scratch = [q_aug, k_aug, v_aug], []
    if shard is not None:
        in_specs.append(ANY)
        out_specs.append(ANY)
        out_shape.append(jax.ShapeDtypeStruct((N_DEV,) + shard.shape, shard.dtype))
        args.append(shard)
        scratch = [pltpu.SemaphoreType.DMA((N_SEM,)), pltpu.SemaphoreType.DMA((N_SEM,)), pltpu.SemaphoreType.DMA(())]
    return pl.pallas_call(
        body, name="fox_fwd", grid=(nb, 4, nq), in_specs=in_specs, out_specs=out_specs, out_shape=out_shape,
        scratch_shapes=scratch, compiler_params=_params(("arbitrary", "arbitrary", "arbitrary")),
    )(*args)


def _fox_bwd(ql_aug, k_aug, v_aug, do_aug, nb, s, bt, exch=()):
    t = ql_aug.shape[0]
    nk = s // bt
    n_in, n_out, n_ex = 4, 3, len(exch)

    def body(*refs):
        q_ref, do_ref, k_ref, v_ref = refs[:n_in]
        dq_ref, dk_ref, dv_ref = refs[n_in + n_ex:n_in + n_ex + n_out]
        if exch:
            srcs = refs[n_in:n_in + n_ex]
            dsts = refs[n_in + n_ex + n_out:n_in + 2 * n_ex + n_out]
            start, finish = _exchange_steps(list(zip(srcs, dsts)), *refs[n_in + 2 * n_ex + n_out:])
            step = (pl.program_id(0) * 4 + pl.program_id(1)) * nk + pl.program_id(2)
            pl.when(step == 0)(start)
        kb_i = pl.program_id(2)

        @pl.when(kb_i == 0)
        def _():
            dq_ref[...] = jnp.zeros_like(dq_ref)

        row = lax.broadcasted_iota(jnp.int32, (bt, bt), 0)
        col = lax.broadcasted_iota(jnp.int32, (bt, bt), 1)
        sls = [slice(LANES * hh, LANES * (hh + 1)) for hh in range(2)]
        khs, vhs = [k_ref[:, sl] for sl in sls], [v_ref[:, sl] for sl in sls]

        def blk(qi, carry, diag):
            start = pl.multiple_of(qi * bt, bt)
            new = []
            for (dk_a, dv_a), kh, vh, sl in zip(carry, khs, vhs, sls):
                qblk, doblk = q_ref[pl.ds(start, bt), sl], do_ref[pl.ds(start, bt), sl]
                st = _nt(kh, qblk)
                if diag:
                    pt = jnp.where(col >= row, jnp.exp2(jnp.where(col >= row, st, 0.0)), 0.0)
                else:
                    pt = jnp.exp2(st)
                dst = pt * _nt(vh, doblk)
                ptb, dstb = pt.astype(BF16), dst.astype(BF16)
                dv_a = dv_a + jnp.dot(ptb, doblk, preferred_element_type=F32)
                dk_a = dk_a + jnp.dot(dstb, qblk, preferred_element_type=F32)
                dq_ref[pl.ds(start, bt), sl] += _tn(dstb, kh)
                new.append((dk_a, dv_a))
            return tuple(new)

        zero = jnp.zeros((bt, LANES), F32)
        carry = blk(kb_i, ((zero, zero), (zero, zero)), True)
        carry = lax.fori_loop(kb_i + 1, nk, lambda qi, c: blk(qi, c, False), carry)
        for (dk_acc, dv_acc), sl in zip(carry, sls):
            dk_ref[:, sl] = dk_acc
            dv_ref[:, sl] = dv_acc
        if exch:
            pl.when(step == nb * 4 * nk - 1)(finish)

    scratch = []
    if exch:
        scratch = [pltpu.SemaphoreType.DMA((N_SEM * n_ex,)), pltpu.SemaphoreType.DMA((N_SEM * n_ex,)),
                   pltpu.SemaphoreType.DMA((n_ex,))]
    whole = pl.BlockSpec((s, 2 * LANES), lambda b, j, kb_i: (b, j))
    tile = pl.BlockSpec((bt, 2 * LANES), lambda b, j, kb_i: (b * nk + kb_i, j))
    shp = jax.ShapeDtypeStruct((t, 8 * LANES), F32)
    return pl.pallas_call(
        body, name="fox_bwd", grid=(nb, 4, nk),
        in_specs=[whole, whole, tile, tile] + [ANY] * n_ex,
        out_specs=[whole, tile, tile] + [ANY] * n_ex,
        out_shape=[shp, shp, shp] + [jax.ShapeDtypeStruct(e.shape, e.dtype) for e in exch],
        scratch_shapes=scratch, compiler_params=_params(("arbitrary", "arbitrary", "arbitrary")),
    )(ql_aug, do_aug, k_aug, v_aug, *exch)


def _mlp_fwd(x2, ma, mb, tgt, wo_a, wo_b, g2, w_up, w_down, tm):
    t = x2.shape[0]

    def body(x_ref, ma_ref, mb_ref, tg_ref, woa_ref, wob_ref, g2_ref, wu_ref, wd_ref,
             h_ref, hn_ref, hid_ref, dy_ref, dyb_ref, loss_ref):
        @pl.when(pl.program_id(0) == 0)
        def _():
            loss_ref[...] = jnp.zeros_like(loss_ref)

        h = (x_ref[...] + jnp.dot(ma_ref[...], woa_ref[...], preferred_element_type=F32)
             + jnp.dot(mb_ref[...], wob_ref[...], preferred_element_type=F32))
        h_ref[...] = h
        r = lax.rsqrt(jnp.mean(h * h, axis=-1, keepdims=True) + EPS)
        hn = (h * r * g2_ref[...]).astype(BF16)
        hn_ref[...] = hn
        u = jnp.maximum(jnp.dot(hn, wu_ref[...], preferred_element_type=F32), 0.0)
        hid = (u * u).astype(BF16)
        hid_ref[...] = hid
        y = h + jnp.dot(hid, wd_ref[...], preferred_element_type=F32)
        err = y - tg_ref[...]
        dy = err * (1.0 / D_MODEL)
        dy_ref[...] = dy
        dyb_ref[...] = dy.astype(BF16)
        part =0.5 * jnp.sum(jnp.sum(err * err, axis=1, keepdims=True) * (1.0 / D_MODEL), axis=0, keepdims=True)
        loss_ref[...] += part

    def tile(w):
        return pl.BlockSpec((tm, w), lambda i: (i, 0))

    return pl.pallas_call(
        body, name="mlp_fwd", grid=(t // tm,),
        in_specs=[tile(D_MODEL), tile(512), tile(512), tile(D_MODEL), _const_spec((512, D_MODEL)), _const_spec((512, D_MODEL)),
                  _const_spec((1, D_MODEL)), _const_spec((D_MODEL, D_FF)), _const_spec((D_FF, D_MODEL))],
        out_specs=[tile(D_MODEL), tile(D_MODEL), tile(D_FF), tile(D_MODEL), tile(D_MODEL),
                   pl.BlockSpec((8, LANES), lambda i: (0, 0))],
        out_shape=[jax.ShapeDtypeStruct((t, D_MODEL), F32), jax.ShapeDtypeStruct((t, D_MODEL), BF16),
                   jax.ShapeDtypeStruct((t, D_FF), BF16), jax.ShapeDtypeStruct((t, D_MODEL), F32),
                   jax.ShapeDtypeStruct((t, D_MODEL), BF16), jax.ShapeDtypeStruct((8, LANES), F32)],
        compiler_params=_params(("arbitrary",)),
    )(x2, ma, mb, tgt, wo_a, wo_b, g2, w_up, w_down)


def _mlp_bwd(dy, hid, h, ma, mb, w_down_t, w_up_t, w_out_t, g2, tm):
    t = dy.shape[0]

    def body(dy_ref, hid_ref, h_ref, ma_ref, mb_ref, wdt_ref, wut_ref, wot_ref, g2_ref,
             du_ref, dh_ref, dhb_ref, dma_ref, dob_ref, dla_ref, gg_ref):
        @pl.when(pl.program_id(0) == 0)
        def _():
            gg_ref[...] = jnp.zeros_like(gg_ref)

        dy = dy_ref[...]
        d_hid = jnp.dot(dy.astype(BF16), wdt_ref[...], preferred_element_type=F32)
        du = (d_hid * (2.0 * jnp.sqrt(hid_ref[...].astype(F32)))).astype(BF16)
        du_ref[...] = du
        d_hn = jnp.dot(du, wut_ref[...], preferred_element_type=F32)
        h = h_ref[...]
        r = lax.rsqrt(jnp.mean(h * h, axis=-1, keepdims=True) + EPS)
        hat = h * r
        gd = d_hn * g2_ref[...]
        dh = dy + r * (gd - hat * jnp.mean(gd * hat, axis=-1, keepdims=True))
        gg_ref[...] += jnp.sum(d_hn * hat, axis=0, keepdims=True)
        dh_ref[...] = dh
        dhb = dh.astype(BF16)
        dhb_ref[...] = dhb
        dm = jnp.dot(dhb, wot_ref[...], preferred_element_type=F32).astype(BF16)
        dma, dmb = dm[:, 0:512], dm[:, 512:1024]
        dma_ref[...] = dma
        sel = (lax.shift_right_logical(lax.broadcasted_iota(jnp.int32, (512, LANES), 0), 6)
               == lax.broadcasted_iota(jnp.int32, (512, LANES), 1)).astype(BF16)
        dla_ref[...] = _split_dot(dma.astype(F32) * ma_ref[...].astype(F32), sel)
        dmb32 = dmb.astype(F32)
        dlb = _split_dot(dmb32 * mb_ref[...].astype(F32), sel)
        for hd in range(8):
            blk = _head_block(dmb32[:, LANES * (hd // 2):LANES * (hd // 2 + 1)], hd % 2)
            dob_ref[:, LANES * hd:LANES * (hd + 1)] = _put3(blk, L_DELTA, -dlb[:, hd:hd + 1]).astype(BF16)

    def tile(w):
        return pl.BlockSpec((tm, w), lambda i: (i, 0))

    return pl.pallas_call(
        body, name="mlp_bwd", grid=(t // tm,),
        in_specs=[tile(D_MODEL), tile(D_FF), tile(D_MODEL), tile(512), tile(512), _const_spec((D_MODEL, D_FF)),
                  _const_spec((D_FF, D_MODEL)), _const_spec((D_MODEL, D_MODEL)), _const_spec((1, D_MODEL))],
        out_specs=[tile(D_FF), tile(D_MODEL), tile(D_MODEL), tile(512), tile(8 * LANES), tile(LANES),
                   pl.BlockSpec((1, D_MODEL), lambda i: (0, 0))],
        out_shape=[jax.ShapeDtypeStruct((t, D_FF), BF16), jax.ShapeDtypeStruct((t, D_MODEL), F32),
                   jax.ShapeDtypeStruct((t, D_MODEL), BF16), jax.ShapeDtypeStruct((t, 512), BF16),
                   jax.ShapeDtypeStruct((t, 8 * LANES), BF16), jax.ShapeDtypeStruct((t, LANES), F32),
                   jax.ShapeDtypeStruct((1, D_MODEL), F32)],
        compiler_params=_params(("arbitrary",)),
    )(dy, hid, h, ma, mb, w_down_t, w_up_t, w_out_t, g2)


def _wgrad(a, b, name, bm, bn, tk, out_dtype=F32, col_blocks=False):
    t, m = a.shape
    n = b.shape[1]
    bm, bn = min(bm, m), min(bn, n)
    nk = t // tk

    def body(a_ref, b_ref, o_ref, acc):
        @pl.when(pl.program_id(2) == 0)
        def _():
            acc[...] = jnp.zeros_like(acc)

        acc[...] += _tn(a_ref[...], b_ref[...])

        @pl.when(pl.program_id(2) == nk - 1)
        def _():
            o_ref[...] = acc[...].astype(out_dtype)

    if col_blocks:
        out_spec = pl.BlockSpec((None, bm, bn), lambda i, j, k: (j, i, 0))
        out_shape = jax.ShapeDtypeStruct((n // bn, m, bn), out_dtype)
    else:
        out_spec = pl.BlockSpec((bm, bn), lambda i, j, k: (i, j))
        out_shape = jax.ShapeDtypeStruct((m, n), out_dtype)
    return pl.pallas_call(
        body, name=name, grid=(m // bm, n // bn, nk),
        in_specs=[pl.BlockSpec((tk, bm), lambda i, j, k: (k, i)), pl.BlockSpec((tk, bn), lambda i, j, k: (k, j))],
        out_specs=out_spec, out_shape=out_shape, scratch_shapes=[pltpu.VMEM((bm, bn), F32)],
        compiler_params=_params(("arbitrary", "arbitrary", "arbitrary")),
    )(a, b)


def _proj_bwd(raw, dqa, dkae, dvae, dqb, dkb, dvb, dfl, x2, dh, w_main_t, w_f_t, g1, gqa, gka, gqb, gkb, tm):
    t = x2.shape[0]

    def body(raw_ref, dqa_ref, dkae_ref, dvae_ref, dqb_ref, dkb_ref, dvb_ref, dfl_ref, x_ref, dh_ref,
             wmt_ref, wft_ref, g1_ref, gqa_ref, gka_ref, gqb_ref, gkb_ref,
             dx_ref, dp_ref, dfb_ref, ggqa_ref, ggka_ref, ggqb_ref, ggkb_ref, gg1_ref):
        @pl.when(pl.program_id(0) == 0)
        def _():
            for r in (ggqa_ref, ggka_ref, ggqb_ref, ggkb_ref, gg1_ref):
                r[...] = jnp.zeros_like(r)

        raw = raw_ref[...]
        d_qa, p_qa = _head_norm_bwd(raw[:, 0:512], gqa_ref[...], dqa_ref[...])
        d_ka, p_ka = _head_norm_bwd(raw[:, 512:640], gka_ref[...], _fold_kv(dkae_ref[...]))
        d_va = _fold_kv(dvae_ref[...])
        d_qb, p_qb = _head_norm_bwd(raw[:, 768:1280], gqb_ref[...], _to_pairs(dqb_ref) * SCALE)
        d_kb, p_kb = _head_norm_bwd(raw[:, 1280:1792], gkb_ref[...], _to_pairs(dkb_ref) * (1.0 / LOG2E))
        ggqa_ref[...] += jnp.sum(p_qa, axis=0, keepdims=True)
        ggka_ref[...] += jnp.sum(p_ka, axis=0, keepdims=True)
        ggqb_ref[...] += jnp.sum(p_qb, axis=0, keepdims=True)
        ggkb_ref[...] += jnp.sum(p_kb, axis=0, keepdims=True)
        dproj = jnp.concatenate([d_qa, d_ka, d_va, d_qb, d_kb, _to_pairs(dvb_ref)], axis=1).astype(BF16)
        dp_ref[...] = dproj
        dfb = dfl_ref[...].astype(BF16)
        dfb_ref[...] = dfb
        d_xn = (jnp.dot(dproj, wmt_ref[...], preferred_element_type=F32)
                + jnp.dot(dfb, wft_ref[...], preferred_element_type=F32))
        x = x_ref[...]
        r = lax.rsqrt(jnp.mean(x * x, axis=-1, keepdims=True) + EPS)
        hat = x * r
        gd = d_xn * g1_ref[...]
        dx_ref[...] = dh_ref[...] + r * (gd - hat * jnp.mean(gd * hat, axis=-1, keepdims=True))
        gg1_ref[...] += jnp.sum(d_xn * hat, axis=0, keepdims=True)

    def tile(w):
        return pl.BlockSpec((tm, w), lambda i: (i, 0))

    def acc(w):
        return pl.BlockSpec((1, w), lambda i: (0, 0))

    return pl.pallas_call(
        body, name="proj_bwd", grid=(t // tm,),
        in_specs=[tile(MAIN_W), tile(512), tile(512), tile(512), tile(8 * LANES), tile(8 * LANES), tile(8 * LANES), tile(LANES),
                  tile(D_MODEL), tile(D_MODEL), _const_spec((MAIN_W, D_MODEL)), _const_spec((LANES, D_MODEL)),
                  _const_spec((1, D_MODEL)), _const_spec((1, 512)), _const_spec((1, 128)), _const_spec((1, 512)),
                  _const_spec((1, 512))],
        out_specs=[tile(D_MODEL), tile(MAIN_W), tile(LANES), acc(512), acc(128), acc(512), acc(512), acc(D_MODEL)],
        out_shape=[jax.ShapeDtypeStruct((t, D_MODEL), F32), jax.ShapeDtypeStruct((t, MAIN_W), BF16),
                   jax.ShapeDtypeStruct((t, LANES), BF16), jax.ShapeDtypeStruct((1, 512), F32),
                   jax.ShapeDtypeStruct((1, 128), F32), jax.ShapeDtypeStruct((1, 512), F32),
                   jax.ShapeDtypeStruct((1, 512), F32), jax.ShapeDtypeStruct((1, D_MODEL), F32)],
        compiler_params=_params(("arbitrary",)),
    )(raw, dqa, dkae, dvae, dqb, dkb, dvb, dfl, x2, dh, w_main_t, w_f_t, g1, gqa, gka, gqb, gkb)


def _pair_rows(a, nb, s):
    two = jnp.stack([a[:, :, 0], a[:, :, 64]], axis=1)
    return jnp.transpose(two.reshape(4, 2, nb, s), (2, 0, 1, 3))


def _head_rows(a, nb, s):
    return jnp.transpose(a[:, 0:8].reshape(nb, s, 4, 2), (0, 2, 3, 1))


R_REST = 128 + 512 + 512
IN_PAD = 304


def _local_step(x, tgt, w_in_t, rest, g1, b_forget, qna, kna, sinks, qnb, knb, g2,
                tm=256, bt=512, btf=1024, tq=512, ts=256, wk=2048, distributed=False):
    nb, s, _ = x.shape
    t = nb * s
    x2, tgt2 = x.reshape(t, D_MODEL), tgt.reshape(t, D_MODEL)
    g1r, g2r = g1.reshape(1, D_MODEL), g2.reshape(1, D_MODEL)
    gqa, gka = jnp.tile(qna, 8).reshape(1, 512), jnp.tile(kna, 2).reshape(1, 128)
    gqb, gkb = jnp.tile(qnb, 8).reshape(1, 512), jnp.tile(knb, 8).reshape(1, 512)
    bf_row = jnp.pad(b_forget, (0, LANES - 8)).reshape(1, LANES)
    sink_row = jnp.pad(sinks, (0, LANES - 8)).reshape(1, LANES)
    w_main_t = w_in_t[0:MAIN_W]
    w_f_t = jnp.pad(w_in_t[MAIN_W:IN_W], ((0, LANES - 8), (0, 0)))
    w_main, w_f = w_main_t.T, w_f_t.T

    xn, raw, fl, qa, kae, vae, qb, kb, vb = _norm_proj(x2, g1r, w_main, w_f, gqa, gka, gqb, gkb, 2 * tm)
    c_col = _gate_cumsum(fl, bf_row, nb, s, ts)
    q_aug, k_aug, v_aug = _fox_prep(qb, kb, vb, c_col, 2 * tm)
    ma, lse_a = _swa_fwd(qa, kae, vae, sink_row, nb, s, tq)
    if distributed:
        mb, ql_aug, full = _fox_fwd(q_aug, k_aug, v_aug, nb, s, btf, shard=rest)
        w_out = full[:, 0:128].reshape(D_MODEL, D_MODEL)
        w_up = jnp.transpose(full[:, 128:640].reshape(N_DEV, D_MODEL, 512), (1, 0, 2)).reshape(D_MODEL, D_FF)
        w_down = full[:, 640:R_REST].reshape(D_FF, D_MODEL)
    else:
        mb, ql_aug = _fox_fwd(q_aug, k_aug, v_aug, nb, s, btf)
        w_out, w_up, w_down = rest
    h, hn, hid, dy, dyb, loss_acc = _mlp_fwd(x2, ma, mb, tgt2, w_out[0:512], w_out[512:1024], g2r, w_up, w_down, 2 * tm)

    du, dh, dhb, dma, do_aug, dla, gg2 = _mlp_bwd(dy, hid, h, ma, mb, w_down.T, w_up.T, w_out.T, g2r, tm)
    g_down = _wgrad(hid, dyb, "wgrad_down", 512, 1024, wk, BF16).reshape(N_DEV, 512, D_MODEL)
    g_up = _wgrad(hn, du, "wgrad_up", 1024, 512, wk, BF16, col_blocks=True)
    g_out = jnp.concatenate([_wgrad(ma, dhb, "wgrad_out_a", 512, 1024, wk, BF16),
                             _wgrad(mb, dhb, "wgrad_out_b", 512, 1024, wk, BF16)], axis=0).reshape(N_DEV, 128, D_MODEL)

    dqa, dkae, dvae, dsink = _swa_bwd(qa, kae, vae, dma, sink_row, _pair_rows(lse_a, nb, s), _head_rows(dla, nb, s), nb, s, tq)
    fox = _fox_bwd(ql_aug, k_aug, v_aug, do_aug, nb, s, bt, exch=(g_out, g_up, g_down) if distributed else ())
    dqb, dkb, dvb = fox[:3]
    if distributed:
        g_out, g_up, g_down = fox[3:]
    dfl, gbf = _gate_cumsum_bwd(dqb, dkb, fl, bf_row, nb, s, ts)
    grad_x, dproj, dfb, ggqa, ggka, ggqb, ggkb, gg1 = _proj_bwd(
        raw, dqa, dkae, dvae, dqb, dkb, dvb, dfl, x2, dh, w_main_t, w_f_t, g1r, gqa, gka, gqb, gkb, tm)
    g_in_t = jnp.concatenate([_wgrad(dproj, xn, "wgrad_in", 768, 1024, wk), _wgrad(dfb, xn, "wgrad_gate", 128, 1024, wk)[0:8]],
                             axis=0)

    small = (gg1.reshape(D_MODEL), gbf[0, 0:8], ggqa.reshape(8, 64).sum(0), ggka.reshape(2, 64).sum(0),
             dsink.sum(0)[:, 0:2, 0].reshape(8), ggqb.reshape(8, 64).sum(0), ggkb.reshape(8, 64).sum(0),
             gg2.reshape(D_MODEL))
    return loss_acc[0, 0], grad_x.reshape(nb, s, D_MODEL), g_in_t, g_out, g_up, g_down, small


def _all_gather(shard):
    def body(x_ref, out_ref, send_sems, recv_sems, local_sem):
        start, forward, finish = _gather_steps(x_ref, out_ref, send_sems, recv_sems, local_sem)
        start()
        forward()
        finish()

    return pl.pallas_call(
        body, name="gather_w_in", out_shape=jax.ShapeDtypeStruct((N_DEV,) + shard.shape, shard.dtype),
        in_specs=[ANY], out_specs=ANY,
        scratch_shapes=[pltpu.SemaphoreType.DMA((N_SEM,)), pltpu.SemaphoreType.DMA((N_SEM,)), pltpu.SemaphoreType.DMA(())],
    )(shard)


def _exchange(*arrays):
    n_ex = len(arrays)

    def body(*refs):
        start, finish = _exchange_steps(list(zip(refs[:n_ex], refs[n_ex:2 * n_ex])), *refs[2 * n_ex:])
        start()
        finish()

    return pl.pallas_call(
        body, name="exchange_tail", out_shape=[jax.ShapeDtypeStruct(a.shape, a.dtype) for a in arrays],
        in_specs=[ANY] * n_ex, out_specs=[ANY] * n_ex,
        scratch_shapes=[pltpu.SemaphoreType.DMA((N_SEM * n_ex,)), pltpu.SemaphoreType.DMA((N_SEM * n_ex,)),
                        pltpu.SemaphoreType.DMA((n_ex,))],
    )(*arrays)


def _sum_adamw(recv, w, m, v, tr, name):
    _, r, n = recv.shape

    def body(r_ref, w_ref, m_ref, v_ref, g_ref, d_ref, nm_ref, nv_ref):
        g = r_ref[0].astype(F32)
        for s in range(1, N_DEV):
            g = g + r_ref[s].astype(F32)
        g_ref[...] = g
        nm = ADAM_B1 * m_ref[...] + (1.0 - ADAM_B1) * g
        nv = ADAM_B2 * v_ref[...] + (1.0 - ADAM_B2) * (g * g)
        m_hat = nm / (1.0 - ADAM_B1 ** ADAM_STEP)
        v_hat = nv / (1.0 - ADAM_B2 ** ADAM_STEP)
        d_ref[...] = -ADAM_LR * (m_hat / (jnp.sqrt(v_hat) + ADAM_EPS) + ADAM_WD * w_ref[...])
        nm_ref[...] = nm
        nv_ref[...] = nv

    tile = pl.BlockSpec((tr, n), lambda i: (i, 0))
    shp = jax.ShapeDtypeStruct((r, n), F32)
    return pl.pallas_call(
        body, name=name, grid=(r // tr,),
        in_specs=[pl.BlockSpec((N_DEV, tr, n), lambda i: (0, i, 0)), tile, tile, tile],
        out_specs=[tile, tile, tile, tile], out_shape=[shp, shp, shp, shp],
        compiler_params=_params(("arbitrary",)),
    )(recv, w, m, v)


def _small_rows(g1, bf, qna, kna, sk, qnb, knb, g2):
    row2 = jnp.concatenate([bf, qna, kna, sk, qnb, knb])
    return jnp.zeros((8, D_MODEL), F32).at[0].set(g1).at[1].set(g2).at[2, 0:row2.shape[0]].set(row2)


def _in_rows(w_in_s):
    return jnp.pad(w_in_s.T, ((0, IN_PAD - IN_SHARD), (0, 0)))


def kernel(x, attn_norm_g, w_in, b_forget, q_norm_a, k_norm_a, sink_logits, q_norm_b, k_norm_b, w_out, mlp_norm_g, w_up, w_down, loss_target, m_attn_norm_g, m_w_in, m_b_forget, m_q_norm_a, m_k_norm_a, m_sink_logits, m_q_norm_b, m_k_norm_b, m_w_out, m_mlp_norm_g, m_w_up, m_w_down, v_attn_norm_g, v_w_in, v_b_forget, v_q_norm_a, v_k_norm_a, v_sink_logits, v_q_norm_b, v_k_norm_b, v_w_out, v_mlp_norm_g, v_w_up, v_w_down):
    w_in_r = _in_rows(w_in)
    w_in_t = _all_gather(w_in_r.astype(BF16))[:, 0:IN_SHARD].reshape(IN_W, D_MODEL)
    rest = jnp.concatenate([w_out, w_up.reshape(512, D_MODEL), w_down], axis=0).astype(BF16)

    loss_part, grad_x, g_in_t, r_out, r_up, r_down, small = _local_step(
        x, loss_target, w_in_t, rest, attn_norm_g, b_forget, q_norm_a, k_norm_a, sink_logits, q_norm_b, k_norm_b, mlp_norm_g,
        distributed=True)

    g_in_blocks = jnp.pad(g_in_t.reshape(N_DEV, IN_SHARD, D_MODEL), ((0, 0), (0, IN_PAD - IN_SHARD), (0, 0))).astype(BF16)
    small_blocks = jnp.broadcast_to(_small_rows(*small), (N_DEV, 8, D_MODEL))
    r_in, r_small = _exchange(g_in_blocks, small_blocks)

    small_w = _small_rows(attn_norm_g, b_forget, q_norm_a, k_norm_a, sink_logits, q_norm_b, k_norm_b, mlp_norm_g)
    small_m = _small_rows(m_attn_norm_g, m_b_forget, m_q_norm_a, m_k_norm_a, m_sink_logits, m_q_norm_b, m_k_norm_b, m_mlp_norm_g)
    small_v = _small_rows(v_attn_norm_g, v_b_forget, v_q_norm_a, v_k_norm_a, v_sink_logits, v_q_norm_b, v_k_norm_b, v_mlp_norm_g)
    o_in = [a[0:IN_SHARD].T for a in _sum_adamw(r_in, w_in_r, _in_rows(m_w_in), _in_rows(v_w_in), IN_PAD, "adamw_in")]
    o_out = _sum_adamw(r_out, w_out, m_w_out, v_w_out, 128, "adamw_out")
    o_up = _sum_adamw(r_up, w_up, m_w_up, v_w_up, 256, "adamw_up")
    o_down = _sum_adamw(r_down, w_down, m_w_down, v_w_down, 128, "adamw_down")
    o_small = _sum_adamw(r_small, small_w, small_m, small_v, 8, "adamw_small")

    def leaves(i):
        row2 = o_small[i][2]
        return (o_small[i][0], o_in[i], row2[0:8], row2[8:72], row2[72:136], row2[136:144], row2[144:208], row2[208:272],
                o_out[i], o_small[i][1], o_up[i], o_down[i])

    loss = lax.psum(loss_part, ("x", "y", "c"))
    return (loss, grad_x, *leaves(0), *leaves(1), *leaves(2), *leaves(3))
```

```python
import functools
import math

import jax
import jax.numpy as jnp
from jax import lax
from jax.experimental import pallas as pl
from jax.experimental.pallas import tpu as pltpu

F32 = jnp.float32
BF16 = jnp.bfloat16

D_MODEL = 1024
HEAD_DIM = 64
N_DEV = 8
D_FF = 4096
A_QW = 512
A_KVW = 128
B_W = 512
MAIN_W = 2304
IN_W = 2312
WINDOW = 128
EPS = 1e-6
SCALE = 0.125
LOG2E = 1.4426950408889634
LANES = 128
NEG_INF = float("-inf")

ADAM_LR = 0.001
ADAM_B1 = 0.9
ADAM_B2 = 0.999
ADAM_EPS = 1e-08
ADAM_WD = 0.01
ADAM_STEP = 10

R_OUT, R_UP, R_DOWN, R_IN = 0, 128, 640, 1152
IN_SHARD = 289
R_SMALL = 1456
R_PACK = 1472
VMEM_LIMIT = 56 * 1024 * 1024


def _params(sem, vmem=VMEM_LIMIT):
    return pltpu.CompilerParams(dimension_semantics=sem, vmem_limit_bytes=vmem)


def _const_spec(shape):
    nd = len(shape)
    return pl.BlockSpec(shape, lambda *_: (0,) * nd, pipeline_mode=pl.Buffered(1))


def _lane(shape):
    return lax.broadcasted_iota(jnp.int32, shape, len(shape) - 1)


def _split_dot(v, mat):
    hi = v.astype(BF16)
    lo = (v - hi.astype(F32)).astype(BF16)
    return (jnp.dot(hi, mat, preferred_element_type=F32) + jnp.dot(lo, mat, preferred_element_type=F32))


def _head_ones(n):
    r = lax.shift_right_logical(lax.broadcasted_iota(jnp.int32, (n, n), 0), 6)
    c = lax.shift_right_logical(lax.broadcasted_iota(jnp.int32, (n, n), 1), 6)
    return (r == c).astype(BF16)


def _head_sum(v):
    w = v.shape[1]
    vb = v.astype(BF16)
    if w <= 256:
        return jnp.dot(vb, _head_ones(w), preferred_element_type=F32)
    ones = _head_ones(256)
    return jnp.concatenate([jnp.dot(vb[:, s:s + 256], ones, preferred_element_type=F32) for s in range(0, w, 256)], axis=1)


def _head_norm(seg, gain):
    rs = lax.rsqrt(_head_sum(seg * seg) * (1.0 / HEAD_DIM) + EPS)
    return seg * rs * gain


def _head_norm_bwd(seg, gain, d_out):
    rs = lax.rsqrt(_head_sum(seg * seg) * (1.0 / HEAD_DIM) + EPS)
    hat = seg * rs
    gd = d_out * gain
    d_seg = rs * (gd - hat * (_head_sum(gd * hat) * (1.0 / HEAD_DIM)))
    return d_seg, d_out * hat


def _expand_kv(v):
    r = pltpu.roll(v, 64, axis=1)
    lo = _lane(v.shape) < 64
    return jnp.concatenate([jnp.where(lo, v, r), jnp.where(lo, r, v)], axis=1)


def _fold_kv(e4):
    t0 = e4[:, 0:128] + e4[:, 128:256]
    t1 = e4[:, 256:384] + e4[:, 384:512]
    t0 = t0 + pltpu.roll(t0, 64, axis=1)
    t1 = t1 + pltpu.roll(t1, 64, axis=1)
    return jnp.where(_lane(t0.shape) < 64, t0, t1)


def _pick_lane(blk, idx):
    return jnp.sum(jnp.where(_lane(blk.shape) == idx, blk, 0.0), axis=1, keepdims=True)


def _nt(a, b):
    return lax.dot_general(a, b, (((1,), (1,)), ((), ())), preferred_element_type=F32)


def _tn(a, b):
    return lax.dot_general(a, b, (((0,), (0,)), ((), ())), preferred_element_type=F32)


def _norm_proj(x2, g1, w_main, w_f, gqa, gka, gqb, gkb, tm):
    t = x2.shape[0]

    def body(x_ref, g1_ref, wm_ref, wf_ref, gqa_ref, gka_ref, gqb_ref, gkb_ref,
             xn_ref, raw_ref, fl_ref, qa_ref, kae_ref, vae_ref, qb_ref, kb_ref, vb_ref):
        x = x_ref[...]
        r = lax.rsqrt(jnp.mean(x * x, axis=-1, keepdims=True) + EPS)
        xn = (x * r * g1_ref[...]).astype(BF16)
        xn_ref[...] = xn
        proj = jnp.dot(xn, wm_ref[...], preferred_element_type=F32)
        raw_ref[...] = proj
        fl_ref[...] = jnp.dot(xn, wf_ref[...], preferred_element_type=F32)
        qa_ref[...] = _head_norm(proj[:, 0:512], gqa_ref[...]).astype(BF16)
        kae_ref[...] = _expand_kv(_head_norm(proj[:, 512:640], gka_ref[...])).astype(BF16)
        vae_ref[...] = _expand_kv(proj[:, 640:768]).astype(BF16)
        qb_ref[...] = (_head_norm(proj[:, 768:1280], gqb_ref[...]) * (SCALE * LOG2E)).astype(BF16)
        kb_ref[...] = _head_norm(proj[:, 1280:1792], gkb_ref[...]).astype(BF16)
        vb_ref[...] = proj[:, 1792:2304].astype(BF16)

    def tile(w):
        return pl.BlockSpec((tm, w), lambda i: (i, 0))

    return pl.pallas_call(
        body, name="norm_proj", grid=(t // tm,),
        in_specs=[tile(D_MODEL), _const_spec((1, D_MODEL)), _const_spec((D_MODEL, MAIN_W)), _const_spec((D_MODEL, LANES)),
                  _const_spec((1, 512)), _const_spec((1, 128)), _const_spec((1, 512)), _const_spec((1, 512))],
        out_specs=[tile(D_MODEL), tile(MAIN_W), tile(LANES), tile(512), tile(256), tile(256), tile(512), tile(512), tile(512)],
        out_shape=[jax.ShapeDtypeStruct((t, D_MODEL), BF16), jax.ShapeDtypeStruct((t, MAIN_W), F32),
                   jax.ShapeDtypeStruct((t, LANES), F32), jax.ShapeDtypeStruct((t, 512), BF16),
                   jax.ShapeDtypeStruct((t, 256), BF16), jax.ShapeDtypeStruct((t, 256), BF16),
                   jax.ShapeDtypeStruct((t, 512), BF16), jax.ShapeDtypeStruct((t, 512), BF16),
                   jax.ShapeDtypeStruct((t, 512), BF16)],
        compiler_params=_params(("arbitrary",)),
    )(x2, g1, w_main, w_f, gqa, gka, gqb, gkb)


def _tri(n, upper):
    r = lax.broadcasted_iota(jnp.int32, (n, n), 0)
    c = lax.broadcasted_iota(jnp.int32, (n, n), 1)
    return ((c >= r) if upper else (c <= r)).astype(F32)


def _slope(p, hh):
    out = jnp.float32(2.0 ** -(2 * 3 + hh + 1))
    for pp in (2, 1, 0):
        out = jnp.where(p == pp, jnp.float32(2.0 ** -(2 * pp + hh + 1)), out)
    return out


def _swa_windows(ref, i, tq):
    nsub = tq // WINDOW
    cur = ref[pl.ds(pl.multiple_of(i * tq, tq), tq), :].reshape(nsub, WINDOW, LANES)
    first = ref[pl.ds(pl.multiple_of(jnp.maximum(i * tq - WINDOW, 0), WINDOW), WINDOW), :].reshape(1, WINDOW, LANES)
    return jnp.concatenate([jnp.concatenate([first, cur[0:nsub - 1]], axis=0), cur], axis=1)


def _both_heads(x3, lo):
    zero = jnp.zeros_like(x3)
    return jnp.concatenate([jnp.where(lo, x3, zero), jnp.where(lo, zero, x3)], axis=0)


def _swa_head_consts(sink_ref, p, i, nsub):
    bidx = lax.broadcasted_iota(jnp.int32, (2 * nsub, 1, 1), 0)
    is_a = bidx < nsub
    slope = jnp.where(is_a, _slope(p, 0), _slope(p, 1))
    sinks = sink_ref[...]
    sink = jnp.where(is_a, _pick_lane(sinks, 2 * p).reshape(1, 1, 1), _pick_lane(sinks, 2 * p + 1).reshape(1, 1, 1))
    first = (i == 0) & ((bidx == 0) | (bidx == nsub))
    return slope, sink, first


def _swa_fwd(qa, kae, vae, sink_row, nb, s, tq):
    t = qa.shape[0]
    nq = s // tq
    nsub = tq // WINDOW

    def body(q_ref, k_ref, v_ref, sink_ref, o_ref, lse_ref):
        p, i = pl.program_id(1), pl.program_id(2)
        lo = _lane((1, 1, LANES)) < 64
        kk, vv = _swa_windows(k_ref, i, tq), _swa_windows(v_ref, i, tq)
        qs = (q_ref[...].astype(F32) * SCALE).astype(BF16).reshape(nsub, WINDOW, LANES)
        q8 = _both_heads(qs, lo)
        s8 = jnp.einsum("bqd,bkd->bqk", q8, jnp.concatenate([kk, kk], axis=0), preferred_element_type=F32)
        row = lax.broadcasted_iota(jnp.int32, (1, WINDOW, 2 * WINDOW), 1)
        col = lax.broadcasted_iota(jnp.int32, (1, WINDOW, 2 * WINDOW), 2)
        dist = row + WINDOW - col
        slope, sink, first = _swa_head_consts(sink_ref, p, i, nsub)
        valid = (dist >= 0) & (dist < WINDOW) & ((col >= WINDOW) | jnp.logical_not(first))
        s8 = jnp.where(valid, s8 - slope * dist.astype(F32), NEG_INF)
        m = jnp.maximum(jnp.max(s8, axis=2, keepdims=True), sink)
        e = jnp.exp(s8 - m)
        den = jnp.sum(e, axis=2, keepdims=True) + jnp.exp(sink - m)
        pr = (e / den).astype(BF16)
        o8 = jnp.einsum("bqk,bkd->bqd", pr, jnp.concatenate([vv, vv], axis=0), preferred_element_type=F32)
        lse8 = m + jnp.log(den)
        o_ref[...] = jnp.where(lo, o8[0:nsub], o8[nsub:]).astype(BF16).reshape(tq, LANES)
        lse_ref[...] = jnp.where(lo, lse8[0:nsub], lse8[nsub:]).reshape(tq, LANES)

    return pl.pallas_call(
        body, name="swa_fwd", grid=(nb, 4, nq),
        in_specs=[pl.BlockSpec((tq, LANES), lambda b, p, i: (b * nq + i, p)),
                  pl.BlockSpec((s, LANES), lambda b, p, i: (b, lax.shift_right_logical(p, 1))),
                  pl.BlockSpec((s, LANES), lambda b, p, i: (b, lax.shift_right_logical(p, 1))),
                  pl.BlockSpec((1, LANES), lambda b, p, i: (0, 0))],
        out_specs=[pl.BlockSpec((tq, LANES), lambda b, p, i: (b * nq + i, p)),
                   pl.BlockSpec((None, tq, LANES), lambda b, p, i: (p, b * nq + i, 0))],
        out_shape=[jax.ShapeDtypeStruct((t, 512), BF16), jax.ShapeDtypeStruct((4, t, LANES), F32)],
        compiler_params=_params(("arbitrary", "arbitrary", "arbitrary")),
    )(qa, kae, vae, sink_row)


def _swa_bwd(qa, kae, vae, do_a, sink_row, lse, delta, nb, s, tq):
    t = qa.shape[0]
    nq = s // tq
    nsub = tq // WINDOW

    def body(q_ref, do_ref, k_ref, v_ref, sink_ref, lse_ref, dl_ref, dq_ref, dk_ref, dv_ref, ds_ref):
        p, i = pl.program_id(1), pl.program_id(2)

        @pl.when(i == 0)
        def _():
            ds_ref[...] = jnp.zeros_like(ds_ref)

        lo = _lane((1, 1, LANES)) < 64
        kk, vv = _swa_windows(k_ref, i, tq), _swa_windows(v_ref, i, tq)
        kks = (kk.astype(F32) * SCALE).astype(BF16)
        k8, v8 = jnp.concatenate([kks, kks], axis=0), jnp.concatenate([vv, vv], axis=0)
        q8 = _both_heads(q_ref[...].reshape(nsub, WINDOW, LANES), lo)
        do8 = _both_heads(do_ref[...].reshape(nsub, WINDOW, LANES), lo)
        cur = pl.multiple_of(i * tq, tq)
        lse3 = lse_ref[...].reshape(nsub, WINDOW, LANES)
        lse8 = jnp.concatenate([lse3[:, :, 0:1], lse3[:, :, 64:65]], axis=0)
        dl = dl_ref[...]
        dl8 = jnp.concatenate([_pick_lane(dl, 2 * p).reshape(nsub, WINDOW, 1),
                               _pick_lane(dl, 2 * p + 1).reshape(nsub, WINDOW, 1)], axis=0)
        row = lax.broadcasted_iota(jnp.int32, (1, WINDOW, 2 * WINDOW), 1)
        col = lax.broadcasted_iota(jnp.int32, (1, WINDOW, 2 * WINDOW), 2)
        dist = row + WINDOW - col
        slope, sink, first = _swa_head_consts(sink_ref, p, i, nsub)
        valid = (dist >= 0) & (dist < WINDOW) & ((col >= WINDOW) | jnp.logical_not(first))
        s8 = jnp.einsum("bqd,bkd->bqk", q8, k8, preferred_element_type=F32) - slope * dist.astype(F32) - lse8
        p8 = jnp.where(valid, jnp.exp(jnp.where(valid, s8, 0.0)), 0.0)
        dp8 = jnp.einsum("bqd,bkd->bqk", do8, v8, preferred_element_type=F32)
        ds8 = p8 * (dp8 - dl8)
        pb, dsb = p8.astype(BF16), ds8.astype(BF16)
        dv8 = jnp.einsum("bqk,bqd->bkd", pb, do8, preferred_element_type=F32)
        dk8 = jnp.einsum("bqk,bqd->bkd", dsb, q8, preferred_element_type=F32) * SCALE
        dq8 = jnp.einsum("bqk,bkd->bqd", dsb, k8, preferred_element_type=F32)
        dq_ref[...] = jnp.where(lo, dq8[0:nsub], dq8[nsub:]).reshape(tq, LANES)

        psd = jnp.exp(sink - lse8) * dl8
        row_h = lax.broadcasted_iota(jnp.int32, (8, LANES), 0)
        for hh in range(2):
            tot = jnp.sum(jnp.sum(psd[hh * nsub:(hh + 1) * nsub], axis=1, keepdims=True), axis=0, keepdims=True)
            ds_ref[...] += jnp.where(row_h == hh, -tot.reshape(1, 1), 0.0)

        prev = pl.multiple_of(jnp.maximum(i * tq - WINDOW, 0), WINDOW)
        for g8, g_ref in ((dk8, dk_ref), (dv8, dv_ref)):
            g4 = g8[0:nsub] + g8[nsub:]
            own, before = g4[:, WINDOW:, :], g4[:, 0:WINDOW, :]
            shifted = jnp.concatenate([before[1:nsub], jnp.zeros((1, WINDOW, LANES), F32)], axis=0)
            g_ref[pl.ds(cur, tq), :] = (own + shifted).reshape(tq, LANES)
            g_ref[pl.ds(prev, WINDOW), :] += before[0]

    return pl.pallas_call(
        body, name="swa_bwd", grid=(nb, 4, nq),
        in_specs=[pl.BlockSpec((tq, LANES), lambda b, p, i: (b * nq + i, p)),
                  pl.BlockSpec((tq, LANES), lambda b, p, i: (b * nq + i, p)),
                  pl.BlockSpec((s, LANES), lambda b, p, i: (b, lax.shift_right_logical(p, 1))),
                  pl.BlockSpec((s, LANES), lambda b, p, i: (b, lax.shift_right_logical(p, 1))),
                  pl.BlockSpec((1, LANES), lambda b, p, i: (0, 0)),
                  pl.BlockSpec((None, tq, LANES), lambda b, p, i: (p, b * nq + i, 0)),
                  pl.BlockSpec((tq, LANES), lambda b, p, i: (b * nq + i, 0))],
        out_specs=[pl.BlockSpec((tq, LANES), lambda b, p, i: (b * nq + i, p)),
                   pl.BlockSpec((s, LANES), lambda b, p, i: (b, p)),
                   pl.BlockSpec((s, LANES), lambda b, p, i: (b, p)),
                   pl.BlockSpec((None, None, 8, LANES), lambda b, p, i: (b, p, 0, 0))],
        out_shape=[jax.ShapeDtypeStruct((t, 512), F32), jax.ShapeDtypeStruct((t, 512), F32),
                   jax.ShapeDtypeStruct((t, 512), F32), jax.ShapeDtypeStruct((nb, 4, 8, LANES), F32)],
        compiler_params=_params(("arbitrary", "arbitrary", "arbitrary")),
    )(qa, do_a, kae, vae, sink_row, lse, delta)


MESH = pl.DeviceIdType.MESH
ANY = pl.BlockSpec(memory_space=pl.ANY)
N_SEM = 7


def _gather_steps(pairs, send_sems, recv_sems, local_sems):
    x, y, c = lax.axis_index("x"), lax.axis_index("y"), lax.axis_index("c")
    me, sibling = (x, y, c), (x, y, 1 - c)
    chips = [(1 - x, y), (x, 1 - y), (1 - x, 1 - y)]
    mine, first, passed, landed, last = [], [], [], [], []
    for a, (x_ref, out_ref) in enumerate(pairs):
        def slot(px, py, pc, out_ref=out_ref):
            return out_ref.at[4 * px + 2 * py + pc]

        def copy(k, block, to, src=None, a=a, slot=slot):
            return pltpu.make_async_remote_copy(
                src_ref=slot(*block) if src is None else src, dst_ref=slot(*block),
                send_sem=send_sems.at[N_SEM * a + k], recv_sem=recv_sems.at[N_SEM * a + k], device_id=to, device_id_type=MESH)

        mine.append(pltpu.make_async_copy(x_ref, slot(*me), local_sems.at[a]))
        first += [copy(0, me, sibling, src=x_ref)] + [copy(1 + j, me, (*chip, c), src=x_ref) for j, chip in enumerate(chips)]
        passed += [copy(4 + j, (*chip, c), sibling) for j, chip in enumerate(chips)]
        landed += [copy(1 + j, (*chip, c), me) for j, chip in enumerate(chips)]
        last += [copy(0, sibling, me)] + [copy(4 + j, (*chip, 1 - c), me) for j, chip in enumerate(chips)]

    def start():
        for cp in mine + first:
            cp.start()

    def forward():
        for arrived, onward in zip(landed, passed):
            arrived.wait_recv()
            onward.start()

    def finish():
        for cp in last:
            cp.wait_recv()
        for cp in first + passed:
            cp.wait_send()
        for cp in mine:
            cp.wait()

    return start, forward, finish


def _exchange_steps(pairs, send_sems, recv_sems, local_sems):
    x, y, c = lax.axis_index("x"), lax.axis_index("y"), lax.axis_index("c")
    my_id = 4 * x + 2 * y + c
    local, remote = [], []
    for a, (src, dst) in enumerate(pairs):
        local.append(pltpu.make_async_copy(src.at[my_id], dst.at[my_id], local_sems.at[a]))
        for k in range(1, N_DEV):
            px = 1 - x if k & 4 else x
            py = 1 - y if k & 2 else y
            pc = 1 - c if k & 1 else c
            remote.append(pltpu.make_async_remote_copy(
                src_ref=src.at[4 * px + 2 * py + pc], dst_ref=dst.at[my_id],
                send_sem=send_sems.at[N_SEM * a + k - 1], recv_sem=recv_sems.at[N_SEM * a + k - 1],
                device_id=(px, py, pc), device_id_type=MESH))

    def start():
        for cp in local + remote:
            cp.start()

    def finish():
        for cp in remote:
            cp.wait_recv()
        for cp in remote:
            cp.wait_send()
        for cp in local:
            cp.wait()

    return start, finish


L_ONE = 64
L_CK = 65
L_CQ = 68
L_LSE = 71
L_DELTA = 74


def _head_block(pair, half):
    y = pair if half == 0 else pltpu.roll(pair, 64, axis=1)
    return jnp.where(_lane(pair.shape) < 64, y, 0.0)


def _put3(blk, lane0, col):
    lane = _lane(blk.shape)
    hi = col.astype(BF16).astype(F32)
    mid = (col - hi).astype(BF16).astype(F32)
    lo = (col - hi - mid).astype(BF16).astype(F32)
    return jnp.where(lane == lane0, hi, jnp.where(lane == lane0 + 1, mid, jnp.where(lane == lane0 + 2, lo, blk)))


def _put_ones(blk, lanes):
    lane = _lane(blk.shape)
    hit = functools.reduce(jnp.logical_or, [lane == ln for ln in lanes])
    return jnp.where(hit, 1.0, blk)


def _to_pairs(ref):
    out = []
    for j in range(4):
        a, b = ref[:, 2 * LANES * j:2 * LANES * j + LANES], ref[:, 2 * LANES * j + LANES:2 * LANES * (j + 1)]
        out.append(jnp.where(_lane(a.shape) < 64, a, pltpu.roll(b, 64, axis=1)))
    return jnp.concatenate(out, axis=1)


def _fox_prep(qb, kb, vb, fl, bf_row, nb, s, tm):
    t = qb.shape[0]
    nt = s // tm

    def body(q_ref, k_ref, v_ref, fl_ref, b_ref, qo_ref, ko_ref, vo_ref, carry, c_ref):
        @pl.when(pl.program_id(1) == 0)
        def _():
            carry[...] = jnp.zeros_like(carry)

        z = fl_ref[...] + b_ref[...]
        e = jnp.exp(-jnp.abs(z))
        u = 1.0 + e
        log1p = jnp.where(u == 1.0, e, jnp.log(u) * (e / (u - 1.0)))
        lf = jnp.minimum(z, 0.0) - log1p
        tri = _tri(256, False)
        for r0 in range(0, tm, 256):
            c_ref[r0:r0 + 256, :] = (jnp.dot(tri, lf[r0:r0 + 256], precision=lax.Precision.HIGHEST, preferred_element_type=F32)
                                     + carry[...])
            carry[...] = c_ref[pl.ds(r0 + 255, 1), :]
        c2 = c_ref[...] * LOG2E
        for h in range(8):
            j, half = h // 2, h % 2
            pair, blk = slice(LANES * j, LANES * (j + 1)), slice(LANES * h, LANES * (h + 1))
            ch = c2[:, h:h + 1]
            q = _put_ones(_head_block(q_ref[:, pair].astype(F32), half), (L_CK, L_CK + 1, L_CK + 2))
            qo_ref[:, blk] = _put3(q, L_CQ, ch).astype(BF16)
            k = _put_ones(_head_block(k_ref[:, pair].astype(F32), half), tuple(range(L_CQ, L_CQ + 6)))
            ko_ref[:, blk] = _put3(k, L_CK, -ch).astype(BF16)
            v = _head_block(v_ref[:, pair].astype(F32), half)
            vo_ref[:, blk] = _put_ones(v, (L_ONE, L_DELTA, L_DELTA + 1, L_DELTA + 2)).astype(BF16)

    def tile(w):
        return pl.BlockSpec((tm, w), lambda b, i: (b * nt + i, 0))

    shp = jax.ShapeDtypeStruct((t, 8 * LANES), BF16)
    return pl.pallas_call(
        body, name="fox_prep", grid=(nb, nt),
        in_specs=[tile(512), tile(512), tile(512), tile(LANES), _const_spec((1, LANES))],
        out_specs=[tile(8 * LANES)] * 3, out_shape=[shp, shp, shp],
        scratch_shapes=[pltpu.VMEM((1, LANES), F32), pltpu.VMEM((tm, LANES), F32)],
        compiler_params=_params(("arbitrary", "arbitrary")),
    )(qb, kb, vb, fl, bf_row)


def _fox_fwd(q_aug, k_aug, v_aug, nb, s, bt, shards=()):
    t = q_aug.shape[0]
    nq = s // bt
    n_in, n_sh = 3, len(shards)

    def body(*refs):
        q_ref, k_ref, v_ref = refs[:n_in]
        o_ref, ql_ref = refs[n_in + n_sh:n_in + n_sh + 2]
        if shards:
            srcs, dsts = refs[n_in:n_in + n_sh], refs[n_in + n_sh + 2:n_in + 2 * n_sh + 2]
            start, forward, finish = _gather_steps(list(zip(srcs, dsts)), *refs[n_in + 2 * n_sh + 2:])
            step = (pl.program_id(0) * 4 + pl.program_id(1)) * nq + pl.program_id(2)
            pl.when(step == 0)(start)
            pl.when(step == nb * 2 * nq)(forward)
        i = pl.program_id(2)
        row = lax.broadcasted_iota(jnp.int32, (bt, bt), 0)
        col = lax.broadcasted_iota(jnp.int32, (bt, bt), 1)
        sls = [slice(LANES * hh, LANES * (hh + 1)) for hh in range(2)]
        qhs = [q_ref[:, sl] for sl in sls]

        def blk(kb_i, carry, diag):
            start = pl.multiple_of(kb_i * bt, bt)
            new = []
            for (m, acc), qh, sl in zip(carry, qhs, sls):
                sc = _nt(qh, k_ref[pl.ds(start, bt), sl])
                if diag:
                    sc = jnp.where(row >= col, sc, NEG_INF)
                m_new = jnp.maximum(m, jnp.max(sc, axis=1, keepdims=True))
                pr = jnp.exp2(sc - m_new).astype(BF16)
                acc = jnp.exp2(m - m_new) * acc + jnp.dot(pr, v_ref[pl.ds(start, bt), sl], preferred_element_type=F32)
                new.append((m_new, acc))
            return tuple(new)

        init = tuple((jnp.full((bt, 1), NEG_INF, F32), jnp.zeros((bt, LANES), F32)) for _ in range(2))
        carry = lax.fori_loop(0, i, lambda kb_i, c: blk(kb_i, c, False), init)
        outs = []
        for (m, acc), qh, sl in zip(blk(i, carry, True), qhs, sls):
            l = acc[:, L_ONE:L_ONE + 1]
            outs.append(acc / l)
            ql_ref[:, sl] = _put3(qh.astype(F32), L_LSE, -(m + jnp.log(l) * LOG2E)).astype(BF16)
        o_ref[...] = jnp.where(_lane((1, LANES)) < 64, outs[0], pltpu.roll(outs[1], 64, axis=1)).astype(BF16)
        if shards:
            pl.when(step == nb * 4 * nq - 1)(finish)

    in_specs = [pl.BlockSpec((bt, 2 * LANES), lambda b, j, i: (b * nq + i, j)),
                pl.BlockSpec((s, 2 * LANES), lambda b, j, i: (b, j)),
                pl.BlockSpec((s, 2 * LANES), lambda b, j, i: (b, j))]
    out_specs = [pl.BlockSpec((bt, LANES), lambda b, j, i: (b * nq + i, j)),
                 pl.BlockSpec((bt, 2 * LANES), lambda b, j, i: (b * nq + i, j))]
    out_shape = [jax.ShapeDtypeStruct((t, 512), BF16), jax.ShapeDtypeStruct((t, 8 * LANES), BF16)]
    args, scratch = [q_aug, k_aug, v_aug, *shards], []
    if shards:
        in_specs += [ANY] * n_sh
        out_specs += [ANY] * n_sh
        out_shape += [jax.ShapeDtypeStruct((N_DEV,) + sh.shape, sh.dtype) for sh in shards]
        scratch = [pltpu.SemaphoreType.DMA((N_SEM * n_sh,)), pltpu.SemaphoreType.DMA((N_SEM * n_sh,)),
                   pltpu.SemaphoreType.DMA((n_sh,))]
    return pl.pallas_call(
        body, name="fox_fwd", grid=(nb, 4, nq), in_specs=in_specs, out_specs=out_specs, out_shape=out_shape,
        scratch_shapes=scratch, compiler_params=_params(("arbitrary", "arbitrary", "arbitrary")),
    )(*args)


def _fox_bwd(ql_aug, k_aug, v_aug, do_aug, nb, s, bt, exch=()):
    t = ql_aug.shape[0]
    nk = s // bt
    n_in, n_out, n_ex = 4, 3, len(exch)

    def body(*refs):
        q_ref, do_ref, k_ref, v_ref = refs[:n_in]
        dq_ref, dk_ref, dv_ref = refs[n_in + n_ex:n_in + n_ex + n_out]
        if exch:
            srcs = refs[n_in:n_in + n_ex]
            dsts = refs[n_in + n_ex + n_out:n_in + 2 * n_ex + n_out]
            start, finish = _exchange_steps(list(zip(srcs, dsts)), *refs[n_in + 2 * n_ex + n_out:])
            step = (pl.program_id(0) * 4 + pl.program_id(1)) * nk + pl.program_id(2)
            pl.when(step == 0)(start)
        kb_i = pl.program_id(2)

        @pl.when(kb_i == 0)
        def _():
            dq_ref[...] = jnp.zeros_like(dq_ref)

        row = lax.broadcasted_iota(jnp.int32, (bt, bt), 0)
        col = lax.broadcasted_iota(jnp.int32, (bt, bt), 1)
        sls = [slice(LANES * hh, LANES * (hh + 1)) for hh in range(2)]
        khs, vhs = [k_ref[:, sl] for sl in sls], [v_ref[:, sl] for sl in sls]

        def blk(qi, carry, diag):
            start = pl.multiple_of(qi * bt, bt)
            new = []
            for (dk_a, dv_a), kh, vh, sl in zip(carry, khs, vhs, sls):
                qblk, doblk = q_ref[pl.ds(start, bt), sl], do_ref[pl.ds(start, bt), sl]
                st = _nt(kh, qblk)
                if diag:
                    pt = jnp.where(col >= row, jnp.exp2(jnp.where(col >= row, st, 0.0)), 0.0)
                else:
                    pt = jnp.exp2(st)
                dst = pt * _nt(vh, doblk)
                ptb, dstb = pt.astype(BF16), dst.astype(BF16)
                dv_a = dv_a + jnp.dot(ptb, doblk, preferred_element_type=F32)
                dk_a = dk_a + jnp.dot(dstb, qblk, preferred_element_type=F32)
                dq_ref[pl.ds(start, bt), sl] += _tn(dstb, kh)
                new.append((dk_a, dv_a))
            return tuple(new)

        zero = jnp.zeros((bt, LANES), F32)
        carry = blk(kb_i, ((zero, zero), (zero, zero)), True)
        carry = lax.fori_loop(kb_i + 1, nk, lambda qi, c: blk(qi, c, False), carry)
        for (dk_acc, dv_acc), sl in zip(carry, sls):
            dk_ref[:, sl] = dk_acc
            dv_ref[:, sl] = dv_acc
        if exch:
            pl.when(step == nb * 4 * nk - 1)(finish)

    scratch = []
    if exch:
        scratch = [pltpu.SemaphoreType.DMA((N_SEM * n_ex,)), pltpu.SemaphoreType.DMA((N_SEM * n_ex,)),
                   pltpu.SemaphoreType.DMA((n_ex,))]
    whole = pl.BlockSpec((s, 2 * LANES), lambda b, j, kb_i: (b, j))
    tile = pl.BlockSpec((bt, 2 * LANES), lambda b, j, kb_i: (b * nk + kb_i, j))
    shp = jax.ShapeDtypeStruct((t, 8 * LANES), F32)
    return pl.pallas_call(
        body, name="fox_bwd", grid=(nb, 4, nk),
        in_specs=[whole, whole, tile, tile] + [ANY] * n_ex,
        out_specs=[whole, tile, tile] + [ANY] * n_ex,
        out_shape=[shp, shp, shp] + [jax.ShapeDtypeStruct(e.shape, e.dtype) for e in exch],
        scratch_shapes=scratch, compiler_params=_params(("arbitrary", "arbitrary", "arbitrary")),
    )(ql_aug, do_aug, k_aug, v_aug, *exch)


FF_BLK = D_FF // N_DEV


def _mlp_fwd(x2, ma, mb, tgt, w_out, g2, w_up, w_down, tm):
    t = x2.shape[0]

    def body(x_ref, ma_ref, mb_ref, tg_ref, wo_ref, g2_ref, wu_ref, wd_ref,
             h_ref, hn_ref, hid_ref, dy_ref, dyb_ref, loss_ref):
        @pl.when(pl.program_id(0) == 0)
        def _():
            loss_ref[...] = jnp.zeros_like(loss_ref)

        h = (x_ref[...] + jnp.dot(ma_ref[...], wo_ref[0:512, :], preferred_element_type=F32)
             + jnp.dot(mb_ref[...], wo_ref[512:1024, :], preferred_element_type=F32))
        h_ref[...] = h
        r = lax.rsqrt(jnp.mean(h * h, axis=-1, keepdims=True) + EPS)
        hn = (h * r * g2_ref[...]).astype(BF16)
        hn_ref[...] = hn
        for d in range(N_DEV):
            u = jnp.maximum(jnp.dot(hn, wu_ref[d], preferred_element_type=F32), 0.0)
            hid_ref[:, FF_BLK * d:FF_BLK * (d + 1)] = (u * u).astype(BF16)
        y = h + jnp.dot(hid_ref[...], wd_ref[...], preferred_element_type=F32)
        err = y - tg_ref[...]
        dy = err * (1.0 / D_MODEL)
        dy_ref[...] = dy
        dyb_ref[...] = dy.astype(BF16)
        part =0.5 * jnp.sum(jnp.sum(err * err, axis=1, keepdims=True) * (1.0 / D_MODEL), axis=0, keepdims=True)
        loss_ref[...] += part

    def tile(w):
        return pl.BlockSpec((tm, w), lambda i: (i, 0))

    return pl.pallas_call(
        body, name="mlp_fwd", grid=(t // tm,),
        in_specs=[tile(D_MODEL), tile(512), tile(512), tile(D_MODEL), _const_spec((D_MODEL, D_MODEL)),
                  _const_spec((1, D_MODEL)), _const_spec((N_DEV, D_MODEL, FF_BLK)), _const_spec((D_FF, D_MODEL))],
        out_specs=[tile(D_MODEL), tile(D_MODEL), tile(D_FF), tile(D_MODEL), tile(D_MODEL),
                   pl.BlockSpec((8, LANES), lambda i: (0, 0))],
        out_shape=[jax.ShapeDtypeStruct((t, D_MODEL), F32), jax.ShapeDtypeStruct((t, D_MODEL), BF16),
                   jax.ShapeDtypeStruct((t, D_FF), BF16), jax.ShapeDtypeStruct((t, D_MODEL), F32),
                   jax.ShapeDtypeStruct((t, D_MODEL), BF16), jax.ShapeDtypeStruct((8, LANES), F32)],
        compiler_params=_params(("arbitrary",)),
    )(x2, ma, mb, tgt, w_out, g2, w_up, w_down)


def _mlp_bwd(dy, hid, h, ma, mb, w_down, w_up, w_out, g2, tm):
    t = dy.shape[0]

    def body(dy_ref, hid_ref, h_ref, ma_ref, mb_ref, wd_ref, wu_ref, wo_ref, g2_ref,
             du_ref, dh_ref, dhb_ref, dma_ref, dob_ref, dla_ref, gg_ref, dhn_ref):
        @pl.when(pl.program_id(0) == 0)
        def _():
            gg_ref[...] = jnp.zeros_like(gg_ref)

        dy = dy_ref[...]
        d_hid = _nt(dy.astype(BF16), wd_ref[...])
        du_ref[...] = (d_hid * (2.0 * jnp.sqrt(hid_ref[...].astype(F32)))).astype(BF16)
        dhn_ref[...] = _nt(du_ref[:, 0:FF_BLK], wu_ref[0])
        for d in range(1, N_DEV):
            dhn_ref[...] += _nt(du_ref[:, FF_BLK * d:FF_BLK * (d + 1)], wu_ref[d])
        d_hn = dhn_ref[...]
        h = h_ref[...]
        r = lax.rsqrt(jnp.mean(h * h, axis=-1, keepdims=True) + EPS)
        hat = h * r
        gd = d_hn * g2_ref[...]
        dh = dy + r * (gd - hat * jnp.mean(gd * hat, axis=-1, keepdims=True))
        gg_ref[...] += jnp.sum(d_hn * hat, axis=0, keepdims=True)
        dh_ref[...] = dh
        dhb = dh.astype(BF16)
        dhb_ref[...] = dhb
        dm = _nt(dhb, wo_ref[...]).astype(BF16)
        dma, dmb = dm[:, 0:512], dm[:, 512:1024]
        dma_ref[...] = dma
        sel = (lax.shift_right_logical(lax.broadcasted_iota(jnp.int32, (512, LANES), 0), 6)
               == lax.broadcasted_iota(jnp.int32, (512, LANES), 1)).astype(BF16)
        dla_ref[...] = _split_dot(dma.astype(F32) * ma_ref[...].astype(F32), sel)
        dmb32 = dmb.astype(F32)
        dlb = _split_dot(dmb32 * mb_ref[...].astype(F32), sel)
        for hd in range(8):
            blk = _head_block(dmb32[:, LANES * (hd // 2):LANES * (hd // 2 + 1)], hd % 2)
            dob_ref[:, LANES * hd:LANES * (hd + 1)] = _put3(blk, L_DELTA, -dlb[:, hd:hd + 1]).astype(BF16)

    def tile(w):
        return pl.BlockSpec((tm, w), lambda i: (i, 0))

    return pl.pallas_call(
        body, name="mlp_bwd", grid=(t // tm,),
        in_specs=[tile(D_MODEL), tile(D_FF), tile(D_MODEL), tile(512), tile(512), _const_spec((D_FF, D_MODEL)),
                  _const_spec((N_DEV, D_MODEL, FF_BLK)), _const_spec((D_MODEL, D_MODEL)), _const_spec((1, D_MODEL))],
        scratch_shapes=[pltpu.VMEM((tm, D_MODEL), F32)],
        out_specs=[tile(D_FF), tile(D_MODEL), tile(D_MODEL), tile(512), tile(8 * LANES), tile(LANES),
                   pl.BlockSpec((1, D_MODEL), lambda i: (0, 0))],
        out_shape=[jax.ShapeDtypeStruct((t, D_FF), BF16), jax.ShapeDtypeStruct((t, D_MODEL), F32),
                   jax.ShapeDtypeStruct((t, D_MODEL), BF16), jax.ShapeDtypeStruct((t, 512), BF16),
                   jax.ShapeDtypeStruct((t, 8 * LANES), BF16), jax.ShapeDtypeStruct((t, LANES), F32),
                   jax.ShapeDtypeStruct((1, D_MODEL), F32)],
        compiler_params=_params(("arbitrary",)),
    )(dy, hid, h, ma, mb, w_down, w_up, w_out, g2)


def _wgrad(a, b, name, bm, bn, tk, out_dtype=F32, col_blocks=False):
    t, m = a.shape
    n = b.shape[1]
    bm, bn = min(bm, m), min(bn, n)
    nk = t // tk

    def body(a_ref, b_ref, o_ref, acc):
        @pl.when(pl.program_id(2) == 0)
        def _():
            acc[...] = jnp.zeros_like(acc)

        acc[...] += _tn(a_ref[...], b_ref[...])

        @pl.when(pl.program_id(2) == nk - 1)
        def _():
            o_ref[...] = acc[...].astype(out_dtype)

    if col_blocks:
        out_spec = pl.BlockSpec((None, bm, bn), lambda i, j, k: (j, i, 0))
        out_shape = jax.ShapeDtypeStruct((n // bn, m, bn), out_dtype)
    else:
        out_spec = pl.BlockSpec((bm, bn), lambda i, j, k: (i, j))
        out_shape = jax.ShapeDtypeStruct((m, n), out_dtype)
    return pl.pallas_call(
        body, name=name, grid=(m // bm, n // bn, nk),
        in_specs=[pl.BlockSpec((tk, bm), lambda i, j, k: (k, i)), pl.BlockSpec((tk, bn), lambda i, j, k: (k, j))],
        out_specs=out_spec, out_shape=out_shape, scratch_shapes=[pltpu.VMEM((bm, bn), F32)],
        compiler_params=_params(("arbitrary", "arbitrary", "arbitrary")),
    )(a, b)


def _proj_bwd(raw, dqa, dkae, dvae, dqb, dkb, dvb, fl, bf_row, x2, dh, w_main_t, w_f_t, g1, gqa, gka, gqb, gkb, nb, s, tm):
    t = x2.shape[0]
    nt = s // tm

    def body(raw_ref, dqa_ref, dkae_ref, dvae_ref, dqb_ref, dkb_ref, dvb_ref, fl_ref, b_ref, x_ref, dh_ref,
             wmt_ref, wft_ref, g1_ref, gqa_ref, gka_ref, gqb_ref, gkb_ref,
             dx_ref, dp_ref, dfb_ref, ggqa_ref, ggka_ref, ggqb_ref, ggkb_ref, gg1_ref, gb_ref, carry, dlf_ref):
        @pl.when((pl.program_id(0) == 0) & (pl.program_id(1) == 0))
        def _():
            for r in (ggqa_ref, ggka_ref, ggqb_ref, ggkb_ref, gg1_ref, gb_ref):
                r[...] = jnp.zeros_like(r)

        @pl.when(pl.program_id(1) == 0)
        def _():
            carry[...] = jnp.zeros_like(carry)

        lane = _lane((tm, LANES))
        dc = jnp.zeros((tm, LANES), F32)
        for hd in range(8):
            col = (dqb_ref[:, LANES * hd + L_CQ:LANES * hd + L_CQ + 1] - dkb_ref[:, LANES * hd + L_CK:LANES * hd + L_CK + 1])
            dc = jnp.where(lane == hd, col, dc)
        dlf_ref[...] = jnp.dot(_tri(tm, True), dc, precision=lax.Precision.HIGHEST, preferred_element_type=F32) + carry[...]
        carry[...] = dlf_ref[pl.ds(0, 1), :]
        dfl = dlf_ref[...] * (1.0 / (1.0 + jnp.exp(fl_ref[...] + b_ref[...])))
        gb_ref[...] += jnp.sum(dfl, axis=0, keepdims=True)

        raw = raw_ref[...]
        d_qa, p_qa = _head_norm_bwd(raw[:, 0:512], gqa_ref[...], dqa_ref[...])
        d_ka, p_ka = _head_norm_bwd(raw[:, 512:640], gka_ref[...], _fold_kv(dkae_ref[...]))
        d_va = _fold_kv(dvae_ref[...])
        d_qb, p_qb = _head_norm_bwd(raw[:, 768:1280], gqb_ref[...], _to_pairs(dqb_ref) * SCALE)
        d_kb, p_kb = _head_norm_bwd(raw[:, 1280:1792], gkb_ref[...], _to_pairs(dkb_ref) * (1.0 / LOG2E))
        ggqa_ref[...] += jnp.sum(p_qa, axis=0, keepdims=True)
        ggka_ref[...] += jnp.sum(p_ka, axis=0, keepdims=True)
        ggqb_ref[...] += jnp.sum(p_qb, axis=0, keepdims=True)
        ggkb_ref[...] += jnp.sum(p_kb, axis=0, keepdims=True)
        dproj = jnp.concatenate([d_qa, d_ka, d_va, d_qb, d_kb, _to_pairs(dvb_ref)], axis=1).astype(BF16)
        dp_ref[...] = dproj
        dfb = dfl.astype(BF16)
        dfb_ref[...] = dfb
        d_xn = (jnp.dot(dproj, wmt_ref[...], preferred_element_type=F32)
                + jnp.dot(dfb, wft_ref[...], preferred_element_type=F32))
        x = x_ref[...]
        r = lax.rsqrt(jnp.mean(x * x, axis=-1, keepdims=True) + EPS)
        hat = x * r
        gd = d_xn * g1_ref[...]
        dx_ref[...] = dh_ref[...] + r * (gd - hat * jnp.mean(gd * hat, axis=-1, keepdims=True))
        gg1_ref[...] += jnp.sum(d_xn * hat, axis=0, keepdims=True)

    def tile(w):
        return pl.BlockSpec((tm, w), lambda b, i: (b * nt + (nt - 1 - i), 0))

    def acc(w):
        return pl.BlockSpec((1, w), lambda b, i: (0, 0))

    return pl.pallas_call(
        body, name="proj_bwd", grid=(nb, nt),
        in_specs=[tile(MAIN_W), tile(512), tile(512), tile(512), tile(8 * LANES), tile(8 * LANES), tile(8 * LANES), tile(LANES),
                  _const_spec((1, LANES)), tile(D_MODEL), tile(D_MODEL), _const_spec((MAIN_W, D_MODEL)),
                  _const_spec((LANES, D_MODEL)), _const_spec((1, D_MODEL)), _const_spec((1, 512)), _const_spec((1, 128)),
                  _const_spec((1, 512)), _const_spec((1, 512))],
        out_specs=[tile(D_MODEL), tile(MAIN_W), tile(LANES), acc(512), acc(128), acc(512), acc(512), acc(D_MODEL), acc(LANES)],
        out_shape=[jax.ShapeDtypeStruct((t, D_MODEL), F32), jax.ShapeDtypeStruct((t, MAIN_W), BF16),
                   jax.ShapeDtypeStruct((t, LANES), BF16), jax.ShapeDtypeStruct((1, 512), F32),
                   jax.ShapeDtypeStruct((1, 128), F32), jax.ShapeDtypeStruct((1, 512), F32),
                   jax.ShapeDtypeStruct((1, 512), F32), jax.ShapeDtypeStruct((1, D_MODEL), F32),
                   jax.ShapeDtypeStruct((1, LANES), F32)],
        scratch_shapes=[pltpu.VMEM((1, LANES), F32), pltpu.VMEM((tm, LANES), F32)],
        compiler_params=_params(("arbitrary", "arbitrary")),
    )(raw, dqa, dkae, dvae, dqb, dkb, dvb, fl, bf_row, x2, dh, w_main_t, w_f_t, g1, gqa, gka, gqb, gkb)


IN_PAD = 304


def _local_step(x, tgt, w_in_t, rest, g1, b_forget, qna, kna, sinks, qnb, knb, g2,
                tm=256, bt=512, btf=1024, tq=512, wk=2048, distributed=False):
    nb, s, _ = x.shape
    t = nb * s
    x2, tgt2 = x.reshape(t, D_MODEL), tgt.reshape(t, D_MODEL)
    g1r, g2r = g1.reshape(1, D_MODEL), g2.reshape(1, D_MODEL)
    gqa, gka = jnp.tile(qna, 8).reshape(1, 512), jnp.tile(kna, 2).reshape(1, 128)
    gqb, gkb = jnp.tile(qnb, 8).reshape(1, 512), jnp.tile(knb, 8).reshape(1, 512)
    bf_row = jnp.pad(b_forget, (0, LANES - 8)).reshape(1, LANES)
    sink_row = jnp.pad(sinks, (0, LANES - 8)).reshape(1, LANES)
    w_main_t = w_in_t[0:MAIN_W]
    w_f_t = jnp.pad(w_in_t[MAIN_W:IN_W], ((0, LANES - 8), (0, 0)))

    xn, raw, fl, qa, kae, vae, qb, kb, vb = _norm_proj(x2, g1r, w_main_t.T, w_f_t.T, gqa, gka, gqb, gkb, 2 * tm)
    q_aug, k_aug, v_aug = _fox_prep(qb, kb, vb, fl, bf_row, nb, s, 2 * tm)
    ma, lse_a = _swa_fwd(qa, kae, vae, sink_row, nb, s, tq)
    if distributed:
        mb, ql_aug, w_out, w_up, w_down = _fox_fwd(q_aug, k_aug, v_aug, nb, s, btf, shards=rest)
    else:
        mb, ql_aug = _fox_fwd(q_aug, k_aug, v_aug, nb, s, btf)
        w_out, w_up, w_down = rest
    w_out, w_down = w_out.reshape(D_MODEL, D_MODEL), w_down.reshape(D_FF, D_MODEL)
    h, hn, hid, dy, dyb, loss_acc = _mlp_fwd(x2, ma, mb, tgt2, w_out, g2r, w_up, w_down, 2 * tm)

    du, dh, dhb, dma, do_aug, dla, gg2 = _mlp_bwd(dy, hid, h, ma, mb, w_down, w_up, w_out, g2r, tm)
    g_down = _wgrad(hid, dyb, "wgrad_down", 512, 1024, wk, BF16).reshape(N_DEV, 512, D_MODEL)
    g_up = _wgrad(hn, du, "wgrad_up", 1024, 512, wk, BF16, col_blocks=True)
    g_out = jnp.concatenate([_wgrad(ma, dhb, "wgrad_out_a", 512, 1024, wk, BF16),
                             _wgrad(mb, dhb, "wgrad_out_b", 512, 1024, wk, BF16)], axis=0).reshape(N_DEV, 128, D_MODEL)

    dqa, dkae, dvae, dsink = _swa_bwd(qa, kae, vae, dma, sink_row, lse_a, dla, nb, s, tq)
    fox = _fox_bwd(ql_aug, k_aug, v_aug, do_aug, nb, s, bt, exch=(g_out, g_up, g_down) if distributed else ())
    dqb, dkb, dvb = fox[:3]
    if distributed:
        g_out, g_up, g_down = fox[3:]
    grad_x, dproj, dfb, ggqa, ggka, ggqb, ggkb, gg1, gbf = _proj_bwd(
        raw, dqa, dkae, dvae, dqb, dkb, dvb, fl, bf_row, x2, dh, w_main_t, w_f_t, g1r, gqa, gka, gqb, gkb, nb, s, tm)
    g_in_t = jnp.concatenate([_wgrad(dproj, xn, "wgrad_in", 768, 1024, wk), _wgrad(dfb, xn, "wgrad_gate", 128, 1024, wk)[0:8]],
                             axis=0)

    small = (gg1.reshape(D_MODEL), gbf[0, 0:8], ggqa.reshape(8, 64).sum(0), ggka.reshape(2, 64).sum(0),
             dsink.sum(0)[:, 0:2, 0].reshape(8), ggqb.reshape(8, 64).sum(0), ggkb.reshape(8, 64).sum(0),
             gg2.reshape(D_MODEL))
    return loss_acc[0, 0], grad_x.reshape(nb, s, D_MODEL), g_in_t, g_out, g_up, g_down, small


def _all_gather(shard):
    def body(x_ref, out_ref, send_sems, recv_sems, local_sem):
        start, forward, finish = _gather_steps([(x_ref, out_ref)], send_sems, recv_sems, local_sem)
        start()
        forward()
        finish()

    return pl.pallas_call(
        body, name="gather_w_in", out_shape=jax.ShapeDtypeStruct((N_DEV,) + shard.shape, shard.dtype),
        in_specs=[ANY], out_specs=ANY,
        scratch_shapes=[pltpu.SemaphoreType.DMA((N_SEM,)), pltpu.SemaphoreType.DMA((N_SEM,)), pltpu.SemaphoreType.DMA((1,))],
    )(shard)


def _exchange(*arrays):
    n_ex = len(arrays)

    def body(*refs):
        start, finish = _exchange_steps(list(zip(refs[:n_ex], refs[n_ex:2 * n_ex])), *refs[2 * n_ex:])
        start()
        finish()

    return pl.pallas_call(
        body, name="exchange_tail", out_shape=[jax.ShapeDtypeStruct(a.shape, a.dtype) for a in arrays],
        in_specs=[ANY] * n_ex, out_specs=[ANY] * n_ex,
        scratch_shapes=[pltpu.SemaphoreType.DMA((N_SEM * n_ex,)), pltpu.SemaphoreType.DMA((N_SEM * n_ex,)),
                        pltpu.SemaphoreType.DMA((n_ex,))],
    )(*arrays)


def _sum_adamw(recv, w, m, v, tr, name):
    _, r, n = recv.shape

    def body(r_ref, w_ref, m_ref, v_ref, g_ref, d_ref, nm_ref, nv_ref):
        g = r_ref[0].astype(F32)
        for s in range(1, N_DEV):
            g = g + r_ref[s].astype(F32)
        g_ref[...] = g
        nm = ADAM_B1 * m_ref[...] + (1.0 - ADAM_B1) * g
        nv = ADAM_B2 * v_ref[...] + (1.0 - ADAM_B2) * (g * g)
        m_hat = nm / (1.0 - ADAM_B1 ** ADAM_STEP)
        v_hat = nv / (1.0 - ADAM_B2 ** ADAM_STEP)
        d_ref[...] = -ADAM_LR * (m_hat / (jnp.sqrt(v_hat) + ADAM_EPS) + ADAM_WD * w_ref[...])
        nm_ref[...] = nm
        nv_ref[...] = nv

    tile = pl.BlockSpec((tr, n), lambda i: (i, 0))
    shp = jax.ShapeDtypeStruct((r, n), F32)
    return pl.pallas_call(
        body, name=name, grid=(r // tr,),
        in_specs=[pl.BlockSpec((N_DEV, tr, n), lambda i: (0, i, 0)), tile, tile, tile],
        out_specs=[tile, tile, tile, tile], out_shape=[shp, shp, shp, shp],
        compiler_params=_params(("arbitrary",)),
    )(recv, w, m, v)


def _small_rows(g1, bf, qna, kna, sk, qnb, knb, g2):
    row2 = jnp.concatenate([bf, qna, kna, sk, qnb, knb])
    return jnp.zeros((8, D_MODEL), F32).at[0].set(g1).at[1].set(g2).at[2, 0:row2.shape[0]].set(row2)


def _in_rows(w_in_s):
    return jnp.pad(w_in_s.T, ((0, IN_PAD - IN_SHARD), (0, 0)))


def kernel(x, attn_norm_g, w_in, b_forget, q_norm_a, k_norm_a, sink_logits, q_norm_b, k_norm_b, w_out, mlp_norm_g, w_up, w_down, loss_target, m_attn_norm_g, m_w_in, m_b_forget, m_q_norm_a, m_k_norm_a, m_sink_logits, m_q_norm_b, m_k_norm_b, m_w_out, m_mlp_norm_g, m_w_up, m_w_down, v_attn_norm_g, v_w_in, v_b_forget, v_q_norm_a, v_k_norm_a, v_sink_logits, v_q_norm_b, v_k_norm_b, v_w_out, v_mlp_norm_g, v_w_up, v_w_down):
    w_in_r = _in_rows(w_in)
    w_in_t = _all_gather(w_in_r.astype(BF16))[:, 0:IN_SHARD].reshape(IN_W, D_MODEL)
    rest = (w_out.astype(BF16), w_up.astype(BF16), w_down.astype(BF16))

    loss_part, grad_x, g_in_t, r_out, r_up, r_down, small = _local_step(
        x, loss_target, w_in_t, rest, attn_norm_g, b_forget, q_norm_a, k_norm_a, sink_logits, q_norm_b, k_norm_b, mlp_norm_g,
        distributed=True)

    g_in_blocks = jnp.pad(g_in_t.reshape(N_DEV, IN_SHARD, D_MODEL), ((0, 0), (0, IN_PAD - IN_SHARD), (0, 0))).astype(BF16)
    small_blocks = jnp.broadcast_to(_small_rows(*small).at[3, 0].set(loss_part), (N_DEV, 8, D_MODEL))
    r_in, r_small = _exchange(g_in_blocks, small_blocks)

    small_w = _small_rows(attn_norm_g, b_forget, q_norm_a, k_norm_a, sink_logits, q_norm_b, k_norm_b, mlp_norm_g)
    small_m = _small_rows(m_attn_norm_g, m_b_forget, m_q_norm_a, m_k_norm_a, m_sink_logits, m_q_norm_b, m_k_norm_b, m_mlp_norm_g)
    small_v = _small_rows(v_attn_norm_g, v_b_forget, v_q_norm_a, v_k_norm_a, v_sink_logits, v_q_norm_b, v_k_norm_b, v_mlp_norm_g)
    o_in = [a[0:IN_SHARD].T for a in _sum_adamw(r_in, w_in_r, _in_rows(m_w_in), _in_rows(v_w_in), IN_PAD, "adamw_in")]
    o_out = _sum_adamw(r_out, w_out, m_w_out, v_w_out, 128, "adamw_out")
    o_up = _sum_adamw(r_up, w_up, m_w_up, v_w_up, 256, "adamw_up")
    o_down = _sum_adamw(r_down, w_down, m_w_down, v_w_down, 128, "adamw_down")
    o_small = _sum_adamw(r_small, small_w, small_m, small_v, 8, "adamw_small")

    def leaves(i):
        row2 = o_small[i][2]
        return (o_small[i][0], o_in[i], row2[0:8], row2[8:72], row2[72:136], row2[136:144], row2[144:208], row2[208:272],
                o_out[i], o_small[i][1], o_up[i], o_down[i])

    return (o_small[0][3, 0], grad_x, *leaves(0), *leaves(1), *leaves(2), *leaves(3))
```

```python
import functools
import math

import jax
import jax.numpy as jnp
from jax import lax
from jax.experimental import pallas as pl
from jax.experimental.pallas import tpu as pltpu

F32 = jnp.float32
BF16 = jnp.bfloat16

D_MODEL = 1024
HEAD_DIM = 64
N_DEV = 8
D_FF = 4096
A_QW = 512
A_KVW = 128
B_W = 512
MAIN_W = 2304
IN_W = 2312
WINDOW = 128
EPS = 1e-6
SCALE = 0.125
LOG2E = 1.4426950408889634
LANES = 128
NEG_INF = float("-inf")

ADAM_LR = 0.001
ADAM_B1 = 0.9
ADAM_B2 = 0.999
ADAM_EPS = 1e-08
ADAM_WD = 0.01
ADAM_STEP = 10

R_OUT, R_UP, R_DOWN, R_IN = 0, 128, 640, 1152
IN_SHARD = 289
R_SMALL = 1456
R_PACK = 1472
VMEM_LIMIT = 56 * 1024 * 1024


def _params(sem, vmem=VMEM_LIMIT):
    return pltpu.CompilerParams(dimension_semantics=sem, vmem_limit_bytes=vmem)


def _const_spec(shape):
    nd = len(shape)
    return pl.BlockSpec(shape, lambda *_: (0,) * nd, pipeline_mode=pl.Buffered(1))


def _lane(shape):
    return lax.broadcasted_iota(jnp.int32, shape, len(shape) - 1)


def _split_dot(v, mat):
    hi = v.astype(BF16)
    lo = (v - hi.astype(F32)).astype(BF16)
    return (jnp.dot(hi, mat, preferred_element_type=F32) + jnp.dot(lo, mat, preferred_element_type=F32))


def _head_ones(n):
    r = lax.shift_right_logical(lax.broadcasted_iota(jnp.int32, (n, n), 0), 6)
    c = lax.shift_right_logical(lax.broadcasted_iota(jnp.int32, (n, n), 1), 6)
    return (r == c).astype(BF16)


def _head_sum(v):
    w = v.shape[1]
    vb = v.astype(BF16)
    if w <= 256:
        return jnp.dot(vb, _head_ones(w), preferred_element_type=F32)
    ones = _head_ones(256)
    return jnp.concatenate([jnp.dot(vb[:, s:s + 256], ones, preferred_element_type=F32) for s in range(0, w, 256)], axis=1)


def _head_norm(seg, gain):
    rs = lax.rsqrt(_head_sum(seg * seg) * (1.0 / HEAD_DIM) + EPS)
    return seg * rs * gain


def _head_norm_bwd(seg, gain, d_out):
    rs = lax.rsqrt(_head_sum(seg * seg) * (1.0 / HEAD_DIM) + EPS)
    hat = seg * rs
    gd = d_out * gain
    d_seg = rs * (gd - hat * (_head_sum(gd * hat) * (1.0 / HEAD_DIM)))
    return d_seg, d_out * hat


def _expand_kv(v):
    r = pltpu.roll(v, 64, axis=1)
    lo = _lane(v.shape) < 64
    return jnp.concatenate([jnp.where(lo, v, r), jnp.where(lo, r, v)], axis=1)


def _fold_kv(e4):
    t0 = e4[:, 0:128] + e4[:, 128:256]
    t1 = e4[:, 256:384] + e4[:, 384:512]
    t0 = t0 + pltpu.roll(t0, 64, axis=1)
    t1 = t1 + pltpu.roll(t1, 64, axis=1)
    return jnp.where(_lane(t0.shape) < 64, t0, t1)


def _pick_lane(blk, idx):
    return jnp.sum(jnp.where(_lane(blk.shape) == idx, blk, 0.0), axis=1, keepdims=True)


def _nt(a, b):
    return lax.dot_general(a, b, (((1,), (1,)), ((), ())), preferred_element_type=F32)


def _tn(a, b):
    return lax.dot_general(a, b, (((0,), (0,)), ((), ())), preferred_element_type=F32)


def _norm_proj(x2, g1, w_main, w_f, gqa, gka, gqb, gkb, tm):
    t = x2.shape[0]

    def body(x_ref, g1_ref, wm_ref, wf_ref, gqa_ref, gka_ref, gqb_ref, gkb_ref,
             xn_ref, raw_ref, fl_ref, qa_ref, kae_ref, vae_ref, qb_ref, kb_ref, vb_ref):
        x = x_ref[...]
        r = lax.rsqrt(jnp.mean(x * x, axis=-1, keepdims=True) + EPS)
        xn = (x * r * g1_ref[...]).astype(BF16)
        xn_ref[...] = xn
        proj = _nt(xn, wm_ref[...])
        raw_ref[...] = proj
        fl_ref[...] = _nt(xn, wf_ref[...])
        qa_ref[...] = _head_norm(proj[:, 0:512], gqa_ref[...]).astype(BF16)
        kae_ref[...] = _expand_kv(_head_norm(proj[:, 512:640], gka_ref[...])).astype(BF16)
        vae_ref[...] = _expand_kv(proj[:, 640:768]).astype(BF16)
        qb_ref[...] = (_head_norm(proj[:, 768:1280], gqb_ref[...]) * (SCALE * LOG2E)).astype(BF16)
        kb_ref[...] = _head_norm(proj[:, 1280:1792], gkb_ref[...]).astype(BF16)
        vb_ref[...] = proj[:, 1792:2304].astype(BF16)

    def tile(w):
        return pl.BlockSpec((tm, w), lambda i: (i, 0))

    return pl.pallas_call(
        body, name="norm_proj", grid=(t // tm,),
        in_specs=[tile(D_MODEL), _const_spec((1, D_MODEL)), _const_spec((MAIN_W, D_MODEL)), _const_spec((LANES, D_MODEL)),
                  _const_spec((1, 512)), _const_spec((1, 128)), _const_spec((1, 512)), _const_spec((1, 512))],
        out_specs=[tile(D_MODEL), tile(MAIN_W), tile(LANES), tile(512), tile(256), tile(256), tile(512), tile(512), tile(512)],
        out_shape=[jax.ShapeDtypeStruct((t, D_MODEL), BF16), jax.ShapeDtypeStruct((t, MAIN_W), F32),
                   jax.ShapeDtypeStruct((t, LANES), F32), jax.ShapeDtypeStruct((t, 512), BF16),
                   jax.ShapeDtypeStruct((t, 256), BF16), jax.ShapeDtypeStruct((t, 256), BF16),
                   jax.ShapeDtypeStruct((t, 512), BF16), jax.ShapeDtypeStruct((t, 512), BF16),
                   jax.ShapeDtypeStruct((t, 512), BF16)],
        compiler_params=_params(("arbitrary",)),
    )(x2, g1, w_main, w_f, gqa, gka, gqb, gkb)


def _tri(n, upper):
    r = lax.broadcasted_iota(jnp.int32, (n, n), 0)
    c = lax.broadcasted_iota(jnp.int32, (n, n), 1)
    return ((c >= r) if upper else (c <= r)).astype(F32)


def _slope(p, hh):
    out = jnp.float32(2.0 ** -(2 * 3 + hh + 1))
    for pp in (2, 1, 0):
        out = jnp.where(p == pp, jnp.float32(2.0 ** -(2 * pp + hh + 1)), out)
    return out


def _swa_windows(ref, i, tq):
    nsub = tq // WINDOW
    cur = ref[pl.ds(pl.multiple_of(i * tq, tq), tq), :].reshape(nsub, WINDOW, LANES)
    first = ref[pl.ds(pl.multiple_of(jnp.maximum(i * tq - WINDOW, 0), WINDOW), WINDOW), :].reshape(1, WINDOW, LANES)
    return jnp.concatenate([jnp.concatenate([first, cur[0:nsub - 1]], axis=0), cur], axis=1)


def _both_heads(x3, lo):
    zero = jnp.zeros_like(x3)
    return jnp.concatenate([jnp.where(lo, x3, zero), jnp.where(lo, zero, x3)], axis=0)


def _swa_head_consts(sink_ref, p, i, nsub):
    bidx = lax.broadcasted_iota(jnp.int32, (2 * nsub, 1, 1), 0)
    is_a = bidx < nsub
    slope = jnp.where(is_a, _slope(p, 0), _slope(p, 1))
    sinks = sink_ref[...]
    sink = jnp.where(is_a, _pick_lane(sinks, 2 * p).reshape(1, 1, 1), _pick_lane(sinks, 2 * p + 1).reshape(1, 1, 1))
    first = (i == 0) & ((bidx == 0) | (bidx == nsub))
    return slope, sink, first


def _swa_fwd(qa, kae, vae, sink_row, nb, s, tq):
    t = qa.shape[0]
    nq = s // tq
    nsub = tq // WINDOW

    def body(q_ref, k_ref, v_ref, sink_ref, o_ref, lse_ref):
        p, i = pl.program_id(1), pl.program_id(2)
        lo = _lane((1, 1, LANES)) < 64
        kk, vv = _swa_windows(k_ref, i, tq), _swa_windows(v_ref, i, tq)
        qs = (q_ref[...].astype(F32) * SCALE).astype(BF16).reshape(nsub, WINDOW, LANES)
        q8 = _both_heads(qs, lo)
        s8 = jnp.einsum("bqd,bkd->bqk", q8, jnp.concatenate([kk, kk], axis=0), preferred_element_type=F32)
        row = lax.broadcasted_iota(jnp.int32, (1, WINDOW, 2 * WINDOW), 1)
        col = lax.broadcasted_iota(jnp.int32, (1, WINDOW, 2 * WINDOW), 2)
        dist = row + WINDOW - col
        slope, sink, first = _swa_head_consts(sink_ref, p, i, nsub)
        valid = (dist >= 0) & (dist < WINDOW) & ((col >= WINDOW) | jnp.logical_not(first))
        s8 = jnp.where(valid, s8 - slope * dist.astype(F32), NEG_INF)
        m = jnp.maximum(jnp.max(s8, axis=2, keepdims=True), sink)
        e = jnp.exp(s8 - m)
        den = jnp.sum(e, axis=2, keepdims=True) + jnp.exp(sink - m)
        pr = (e / den).astype(BF16)
        o8 = jnp.einsum("bqk,bkd->bqd", pr, jnp.concatenate([vv, vv], axis=0), preferred_element_type=F32)
        lse8 = m + jnp.log(den)
        o_ref[...] = jnp.where(lo, o8[0:nsub], o8[nsub:]).astype(BF16).reshape(tq, LANES)
        lse_ref[...] = jnp.where(lo, lse8[0:nsub], lse8[nsub:]).reshape(tq, LANES)

    return pl.pallas_call(
        body, name="swa_fwd", grid=(nb, 4, nq),
        in_specs=[pl.BlockSpec((tq, LANES), lambda b, p, i: (b * nq + i, p)),
                  pl.BlockSpec((s, LANES), lambda b, p, i: (b, lax.shift_right_logical(p, 1))),
                  pl.BlockSpec((s, LANES), lambda b, p, i: (b, lax.shift_right_logical(p, 1))),
                  pl.BlockSpec((1, LANES), lambda b, p, i: (0, 0))],
        out_specs=[pl.BlockSpec((tq, LANES), lambda b, p, i: (b * nq + i, p)),
                   pl.BlockSpec((None, tq, LANES), lambda b, p, i: (p, b * nq + i, 0))],
        out_shape=[jax.ShapeDtypeStruct((t, 512), BF16), jax.ShapeDtypeStruct((4, t, LANES), F32)],
        compiler_params=_params(("arbitrary", "arbitrary", "arbitrary")),
    )(qa, kae, vae, sink_row)


def _swa_bwd(qa, kae, vae, do_a, sink_row, lse, delta, nb, s, tq):
    t = qa.shape[0]
    nq = s // tq
    nsub = tq // WINDOW

    def body(q_ref, do_ref, k_ref, v_ref, sink_ref, lse_ref, dl_ref, dq_ref, dk_ref, dv_ref, ds_ref):
        p, i = pl.program_id(1), pl.program_id(2)

        @pl.when(i == 0)
        def _():
            ds_ref[...] = jnp.zeros_like(ds_ref)

        lo = _lane((1, 1, LANES)) < 64
        kk, vv = _swa_windows(k_ref, i, tq), _swa_windows(v_ref, i, tq)
        kks = (kk.astype(F32) * SCALE).astype(BF16)
        k8, v8 = jnp.concatenate([kks, kks], axis=0), jnp.concatenate([vv, vv], axis=0)
        q8 = _both_heads(q_ref[...].reshape(nsub, WINDOW, LANES), lo)
        do8 = _both_heads(do_ref[...].reshape(nsub, WINDOW, LANES), lo)
        cur = pl.multiple_of(i * tq, tq)
        sub = lax.broadcasted_iota(jnp.int32, (WINDOW, WINDOW), 0)
        lse_t = [lse_ref[u * WINDOW:(u + 1) * WINDOW, :].T for u in range(nsub)]
        dl_t = [dl_ref[u * WINDOW:(u + 1) * WINDOW, :].T for u in range(nsub)]
        lse8 = jnp.concatenate([t_[64 * hh:64 * hh + 1, :].reshape(1, 1, WINDOW) for hh in range(2) for t_ in lse_t], axis=0)
        dl8 = jnp.concatenate([jnp.sum(jnp.where(sub == 2 * p + hh, t_, 0.0), axis=0, keepdims=True).reshape(1, 1, WINDOW)
                               for hh in range(2) for t_ in dl_t], axis=0)
        row = lax.broadcasted_iota(jnp.int32, (1, 2 * WINDOW, WINDOW), 1)
        col = lax.broadcasted_iota(jnp.int32, (1, 2 * WINDOW, WINDOW), 2)
        dist = col + WINDOW - row
        slope, sink, first = _swa_head_consts(sink_ref, p, i, nsub)
        valid = (dist >= 0) & (dist < WINDOW) & ((row >= WINDOW) | jnp.logical_not(first))
        st = jnp.einsum("bkd,bqd->bkq", k8, q8, preferred_element_type=F32) - slope * dist.astype(F32) - lse8
        pt = jnp.where(valid, jnp.exp(jnp.where(valid, st, 0.0)), 0.0)
        dpt = jnp.einsum("bkd,bqd->bkq", v8, do8, preferred_element_type=F32)
        dst = pt * (dpt - dl8)
        ptb, dstb = pt.astype(BF16), dst.astype(BF16)
        dv8 = jnp.einsum("bkq,bqd->bkd", ptb, do8, preferred_element_type=F32)
        dk8 = jnp.einsum("bkq,bqd->bkd", dstb, q8, preferred_element_type=F32) * SCALE
        dq8 = jnp.einsum("bkq,bkd->bqd", dstb, k8, preferred_element_type=F32)
        dq_ref[...] = jnp.where(lo, dq8[0:nsub], dq8[nsub:]).reshape(tq, LANES)

        psd = jnp.exp(sink - lse8) * dl8
        row_h = lax.broadcasted_iota(jnp.int32, (8, LANES), 0)
        for hh in range(2):
            tot = jnp.sum(jnp.sum(psd[hh * nsub:(hh + 1) * nsub], axis=2, keepdims=True), axis=0, keepdims=True)
            ds_ref[...] += jnp.where(row_h == hh, -tot.reshape(1, 1), 0.0)

        prev = pl.multiple_of(jnp.maximum(i * tq - WINDOW, 0), WINDOW)
        for g8, g_ref in ((dk8, dk_ref), (dv8, dv_ref)):
            g4 = g8[0:nsub] + g8[nsub:]
            own, before = g4[:, WINDOW:, :], g4[:, 0:WINDOW, :]
            shifted = jnp.concatenate([before[1:nsub], jnp.zeros((1, WINDOW, LANES), F32)], axis=0)
            g_ref[pl.ds(cur, tq), :] = (own + shifted).reshape(tq, LANES)
            g_ref[pl.ds(prev, WINDOW), :] += before[0]

    return pl.pallas_call(
        body, name="swa_bwd", grid=(nb, 4, nq),
        in_specs=[pl.BlockSpec((tq, LANES), lambda b, p, i: (b * nq + i, p)),
                  pl.BlockSpec((tq, LANES), lambda b, p, i: (b * nq + i, p)),
                  pl.BlockSpec((s, LANES), lambda b, p, i: (b, lax.shift_right_logical(p, 1))),
                  pl.BlockSpec((s, LANES), lambda b, p, i: (b, lax.shift_right_logical(p, 1))),
                  pl.BlockSpec((1, LANES), lambda b, p, i: (0, 0)),
                  pl.BlockSpec((None, tq, LANES), lambda b, p, i: (p, b * nq + i, 0)),
                  pl.BlockSpec((tq, LANES), lambda b, p, i: (b * nq + i, 0))],
        out_specs=[pl.BlockSpec((tq, LANES), lambda b, p, i: (b * nq + i, p)),
                   pl.BlockSpec((s, LANES), lambda b, p, i: (b, p)),
                   pl.BlockSpec((s, LANES), lambda b, p, i: (b, p)),
                   pl.BlockSpec((None, None, 8, LANES), lambda b, p, i: (b, p, 0, 0))],
        out_shape=[jax.ShapeDtypeStruct((t, 512), F32), jax.ShapeDtypeStruct((t, 512), F32),
                   jax.ShapeDtypeStruct((t, 512), F32), jax.ShapeDtypeStruct((nb, 4, 8, LANES), F32)],
        compiler_params=_params(("arbitrary", "arbitrary", "arbitrary")),
    )(qa, do_a, kae, vae, sink_row, lse, delta)


MESH = pl.DeviceIdType.MESH
ANY = pl.BlockSpec(memory_space=pl.ANY)
N_SEM = 7


def _gather_steps(pairs, send_sems, recv_sems, local_sems):
    x, y, c = lax.axis_index("x"), lax.axis_index("y"), lax.axis_index("c")
    me, sibling = (x, y, c), (x, y, 1 - c)
    chips = [(1 - x, y), (x, 1 - y), (1 - x, 1 - y)]
    mine, first, passed, landed, last = [], [], [], [], []
    for a, (x_ref, out_ref) in enumerate(pairs):
        def slot(px, py, pc, out_ref=out_ref):
            return out_ref.at[4 * px + 2 * py + pc]

        def copy(k, block, to, src=None, a=a, slot=slot):
            return pltpu.make_async_remote_copy(
                src_ref=slot(*block) if src is None else src, dst_ref=slot(*block),
                send_sem=send_sems.at[N_SEM * a + k], recv_sem=recv_sems.at[N_SEM * a + k], device_id=to, device_id_type=MESH)

        mine.append(pltpu.make_async_copy(x_ref, slot(*me), local_sems.at[a]))
        first += [copy(0, me, sibling, src=x_ref)] + [copy(1 + j, me, (*chip, c), src=x_ref) for j, chip in enumerate(chips)]
        passed += [copy(4 + j, (*chip, c), sibling) for j, chip in enumerate(chips)]
        landed += [copy(1 + j, (*chip, c), me) for j, chip in enumerate(chips)]
        last += [copy(0, sibling, me)] + [copy(4 + j, (*chip, 1 - c), me) for j, chip in enumerate(chips)]

    def start():
        for cp in mine + first:
            cp.start()

    def forward():
        for arrived, onward in zip(landed, passed):
            arrived.wait_recv()
            onward.start()

    def finish():
        for cp in last:
            cp.wait_recv()
        for cp in first + passed:
            cp.wait_send()
        for cp in mine:
            cp.wait()

    return start, forward, finish


def _exchange_steps(pairs, send_sems, recv_sems, local_sems):
    x, y, c = lax.axis_index("x"), lax.axis_index("y"), lax.axis_index("c")
    my_id = 4 * x + 2 * y + c
    local, remote = [], []
    for a, (src, dst) in enumerate(pairs):
        local.append(pltpu.make_async_copy(src.at[my_id], dst.at[my_id], local_sems.at[a]))
        for k in range(1, N_DEV):
            px = 1 - x if k & 4 else x
            py = 1 - y if k & 2 else y
            pc = 1 - c if k & 1 else c
            remote.append(pltpu.make_async_remote_copy(
                src_ref=src.at[4 * px + 2 * py + pc], dst_ref=dst.at[my_id],
                send_sem=send_sems.at[N_SEM * a + k - 1], recv_sem=recv_sems.at[N_SEM * a + k - 1],
                device_id=(px, py, pc), device_id_type=MESH))

    def start():
        for cp in local + remote:
            cp.start()

    def finish():
        for cp in remote:
            cp.wait_recv()
        for cp in remote:
            cp.wait_send()
        for cp in local:
            cp.wait()

    return start, finish


L_ONE = 64
L_CK = 65
L_CQ = 68
L_LSE = 71
L_DELTA = 74


def _head_block(pair, half):
    y = pair if half == 0 else pltpu.roll(pair, 64, axis=1)
    return jnp.where(_lane(pair.shape) < 64, y, 0.0)


def _put3(blk, lane0, col):
    lane = _lane(blk.shape)
    hi = col.astype(BF16).astype(F32)
    mid = (col - hi).astype(BF16).astype(F32)
    lo = (col - hi - mid).astype(BF16).astype(F32)
    return jnp.where(lane == lane0, hi, jnp.where(lane == lane0 + 1, mid, jnp.where(lane == lane0 + 2, lo, blk)))


def _spread3(col, shape, lane0s):
    lane = _lane(shape)
    hi = col.astype(BF16).astype(F32)
    mid = (col - hi).astype(BF16).astype(F32)
    lo = (col - hi - mid).astype(BF16).astype(F32)

    def at(k):
        return functools.reduce(jnp.logical_or, [lane == ln + k for ln in lane0s])

    return jnp.where(at(0), hi, jnp.where(at(1), mid, jnp.where(at(2), lo, 0.0)))


def _put_ones(blk, lanes):
    lane = _lane(blk.shape)
    hit = functools.reduce(jnp.logical_or, [lane == ln for ln in lanes])
    return jnp.where(hit, 1.0, blk)


def _to_pairs(ref):
    out = []
    for j in range(4):
        a, b = ref[:, 2 * LANES * j:2 * LANES * j + LANES], ref[:, 2 * LANES * j + LANES:2 * LANES * (j + 1)]
        out.append(jnp.where(_lane(a.shape) < 64, a, pltpu.roll(b, 64, axis=1)))
    return jnp.concatenate(out, axis=1)


def _fox_prep(qb, kb, vb, fl, bf_row, nb, s, tm):
    t = qb.shape[0]
    nt = s // tm

    def body(q_ref, k_ref, v_ref, fl_ref, b_ref, qo_ref, ko_ref, vo_ref, carry, c_ref):
        @pl.when(pl.program_id(1) == 0)
        def _():
            carry[...] = jnp.zeros_like(carry)

        z = fl_ref[...] + b_ref[...]
        e = jnp.exp(-jnp.abs(z))
        u = 1.0 + e
        log1p = jnp.where(u == 1.0, e, jnp.log(u) * (e / (u - 1.0)))
        lf = jnp.minimum(z, 0.0) - log1p
        tri = _tri(256, False)
        for r0 in range(0, tm, 256):
            c_ref[r0:r0 + 256, :] = (jnp.dot(tri, lf[r0:r0 + 256], precision=lax.Precision.HIGHEST, preferred_element_type=F32)
                                     + carry[...])
            carry[...] = c_ref[pl.ds(r0 + 255, 1), :]
        c2 = c_ref[...] * LOG2E
        for h in range(8):
            j, half = h // 2, h % 2
            pair, blk = slice(LANES * j, LANES * (j + 1)), slice(LANES * h, LANES * (h + 1))
            feat = _spread3(c2[:, h:h + 1], (tm, LANES), (L_CK, L_CQ))
            lane = _lane((tm, LANES))
            q = _put_ones(_head_block(q_ref[:, pair].astype(F32), half), (L_CK, L_CK + 1, L_CK + 2))
            qo_ref[:, blk] = jnp.where((lane >= L_CQ) & (lane < L_CQ + 3), feat, q).astype(BF16)
            k = _put_ones(_head_block(k_ref[:, pair].astype(F32), half), tuple(range(L_CQ, L_CQ + 6)))
            ko_ref[:, blk] = jnp.where((lane >= L_CK) & (lane < L_CK + 3), -feat, k).astype(BF16)
            v = _head_block(v_ref[:, pair].astype(F32), half)
            vo_ref[:, blk] = _put_ones(v, (L_ONE, L_DELTA, L_DELTA + 1, L_DELTA + 2)).astype(BF16)

    def tile(w):
        return pl.BlockSpec((tm, w), lambda b, i: (b * nt + i, 0))

    shp = jax.ShapeDtypeStruct((t, 8 * LANES), BF16)
    return pl.pallas_call(
        body, name="fox_prep", grid=(nb, nt),
        in_specs=[tile(512), tile(512), tile(512), tile(LANES), _const_spec((1, LANES))],
        out_specs=[tile(8 * LANES)] * 3, out_shape=[shp, shp, shp],
        scratch_shapes=[pltpu.VMEM((1, LANES), F32), pltpu.VMEM((tm, LANES), F32)],
        compiler_params=_params(("arbitrary", "arbitrary")),
    )(qb, kb, vb, fl, bf_row)


def _fox_fwd(q_aug, k_aug, v_aug, nb, s, bt, shards=()):
    t = q_aug.shape[0]
    nq = s // bt
    n_in, n_sh = 3, len(shards)

    def body(*refs):
        q_ref, k_ref, v_ref = refs[:n_in]
        o_ref, ql_ref = refs[n_in + n_sh:n_in + n_sh + 2]
        if shards:
            srcs, dsts = refs[n_in:n_in + n_sh], refs[n_in + n_sh + 2:n_in + 2 * n_sh + 2]
            start, forward, finish = _gather_steps(list(zip(srcs, dsts)), *refs[n_in + 2 * n_sh + 2:])
            step = (pl.program_id(0) * 4 + pl.program_id(1)) * nq + pl.program_id(2)
            pl.when(step == 0)(start)
            pl.when(step == nb * 2 * nq)(forward)
        i = pl.program_id(2)
        row = lax.broadcasted_iota(jnp.int32, (bt, bt), 0)
        col = lax.broadcasted_iota(jnp.int32, (bt, bt), 1)
        sls = [slice(LANES * hh, LANES * (hh + 1)) for hh in range(2)]
        qhs = [q_ref[:, sl] for sl in sls]

        def blk(kb_i, carry, diag):
            start = pl.multiple_of(kb_i * bt, bt)
            new = []
            for (m, acc), qh, sl in zip(carry, qhs, sls):
                sc = _nt(qh, k_ref[pl.ds(start, bt), sl])
                if diag:
                    sc = jnp.where(row >= col, sc, NEG_INF)
                m_new = jnp.maximum(m, jnp.max(sc, axis=1, keepdims=True))
                pr = jnp.exp2(sc - m_new).astype(BF16)
                acc = jnp.exp2(m - m_new) * acc + jnp.dot(pr, v_ref[pl.ds(start, bt), sl], preferred_element_type=F32)
                new.append((m_new, acc))
            return tuple(new)

        init = tuple((jnp.full((bt, 1), NEG_INF, F32), jnp.zeros((bt, LANES), F32)) for _ in range(2))
        carry = lax.fori_loop(0, i, lambda kb_i, c: blk(kb_i, c, False), init)
        outs = []
        for (m, acc), qh, sl in zip(blk(i, carry, True), qhs, sls):
            l = acc[:, L_ONE:L_ONE + 1]
            outs.append(acc / l)
            ql_ref[:, sl] = _put3(qh.astype(F32), L_LSE, -(m + jnp.log(l) * LOG2E)).astype(BF16)
        o_ref[...] = jnp.where(_lane((1, LANES)) < 64, outs[0], pltpu.roll(outs[1], 64, axis=1)).astype(BF16)
        if shards:
            pl.when(step == nb * 4 * nq - 1)(finish)

    in_specs = [pl.BlockSpec((bt, 2 * LANES), lambda b, j, i: (b * nq + i, j)),
                pl.BlockSpec((s, 2 * LANES), lambda b, j, i: (b, j)),
                pl.BlockSpec((s, 2 * LANES), lambda b, j, i: (b, j))]
    out_specs = [pl.BlockSpec((bt, LANES), lambda b, j, i: (b * nq + i, j)),
                 pl.BlockSpec((bt, 2 * LANES), lambda b, j, i: (b * nq + i, j))]
    out_shape = [jax.ShapeDtypeStruct((t, 512), BF16), jax.ShapeDtypeStruct((t, 8 * LANES), BF16)]
    args, scratch = [q_aug, k_aug, v_aug, *shards], []
    if shards:
        in_specs += [ANY] * n_sh
        out_specs += [ANY] * n_sh
        out_shape += [jax.ShapeDtypeStruct((N_DEV,) + sh.shape, sh.dtype) for sh in shards]
        scratch = [pltpu.SemaphoreType.DMA((N_SEM * n_sh,)), pltpu.SemaphoreType.DMA((N_SEM * n_sh,)),
                   pltpu.SemaphoreType.DMA((n_sh,))]
    return pl.pallas_call(
        body, name="fox_fwd", grid=(nb, 4, nq), in_specs=in_specs, out_specs=out_specs, out_shape=out_shape,
        scratch_shapes=scratch, compiler_params=_params(("arbitrary", "arbitrary", "arbitrary")),
    )(*args)


def _fox_bwd(ql_aug, k_aug, v_aug, do_aug, nb, s, bt, exch=()):
    t = ql_aug.shape[0]
    nk = s // bt
    n_in, n_out, n_ex = 4, 3, len(exch)

    def body(*refs):
        q_ref, do_ref, k_ref, v_ref = refs[:n_in]
        dq_ref, dk_ref, dv_ref = refs[n_in + n_ex:n_in + n_ex + n_out]
        if exch:
            srcs = refs[n_in:n_in + n_ex]
            dsts = refs[n_in + n_ex + n_out:n_in + 2 * n_ex + n_out]
            start, finish = _exchange_steps(list(zip(srcs, dsts)), *refs[n_in + 2 * n_ex + n_out:])
            step = (pl.program_id(0) * 4 + pl.program_id(1)) * nk + pl.program_id(2)
            pl.when(step == 0)(start)
        kb_i = pl.program_id(2)

        @pl.when(kb_i == 0)
        def _():
            dq_ref[...] = jnp.zeros_like(dq_ref)

        row = lax.broadcasted_iota(jnp.int32, (bt, bt), 0)
        col = lax.broadcasted_iota(jnp.int32, (bt, bt), 1)
        sls = [slice(LANES * hh, LANES * (hh + 1)) for hh in range(2)]
        khs, vhs = [k_ref[:, sl] for sl in sls], [v_ref[:, sl] for sl in sls]

        def blk(qi, carry, diag):
            start = pl.multiple_of(qi * bt, bt)
            new = []
            for (dk_a, dv_a), kh, vh, sl in zip(carry, khs, vhs, sls):
                qblk, doblk = q_ref[pl.ds(start, bt), sl], do_ref[pl.ds(start, bt), sl]
                st = _nt(kh, qblk)
                if diag:
                    pt = jnp.where(col >= row, jnp.exp2(jnp.where(col >= row, st, 0.0)), 0.0)
                else:
                    pt = jnp.exp2(st)
                dst = pt * _nt(vh, doblk)
                ptb, dstb = pt.astype(BF16), dst.astype(BF16)
                dv_a = dv_a + jnp.dot(ptb, doblk, preferred_element_type=F32)
                dk_a = dk_a + jnp.dot(dstb, qblk, preferred_element_type=F32)
                dq_ref[pl.ds(start, bt), sl] += _tn(dstb, kh)
                new.append((dk_a, dv_a))
            return tuple(new)

        zero = jnp.zeros((bt, LANES), F32)
        carry = blk(kb_i, ((zero, zero), (zero, zero)), True)
        carry = lax.fori_loop(kb_i + 1, nk, lambda qi, c: blk(qi, c, False), carry)
        for (dk_acc, dv_acc), sl in zip(carry, sls):
            dk_ref[:, sl] = dk_acc
            dv_ref[:, sl] = dv_acc
        if exch:
            pl.when(step == nb * 4 * nk - 1)(finish)

    scratch = []
    if exch:
        scratch = [pltpu.SemaphoreType.DMA((N_SEM * n_ex,)), pltpu.SemaphoreType.DMA((N_SEM * n_ex,)),
                   pltpu.SemaphoreType.DMA((n_ex,))]
    whole = pl.BlockSpec((s, 2 * LANES), lambda b, j, kb_i: (b, j))
    tile = pl.BlockSpec((bt, 2 * LANES), lambda b, j, kb_i: (b * nk + kb_i, j))
    shp = jax.ShapeDtypeStruct((t, 8 * LANES), F32)
    return pl.pallas_call(
        body, name="fox_bwd", grid=(nb, 4, nk),
        in_specs=[whole, whole, tile, tile] + [ANY] * n_ex,
        out_specs=[whole, tile, tile] + [ANY] * n_ex,
        out_shape=[shp, shp, shp] + [jax.ShapeDtypeStruct(e.shape, e.dtype) for e in exch],
        scratch_shapes=scratch, compiler_params=_params(("arbitrary", "arbitrary", "arbitrary")),
    )(ql_aug, do_aug, k_aug, v_aug, *exch)


FF_BLK = D_FF // N_DEV


def _mlp_fwd(x2, ma, mb, tgt, w_out, g2, w_up, w_down, tm):
    t = x2.shape[0]

    def body(x_ref, ma_ref, mb_ref, tg_ref, wo_ref, g2_ref, wu_ref, wd_ref,
             h_ref, hn_ref, hid_ref, dy_ref, dyb_ref, loss_ref):
        @pl.when(pl.program_id(0) == 0)
        def _():
            loss_ref[...] = jnp.zeros_like(loss_ref)

        h = (x_ref[...] + jnp.dot(ma_ref[...], wo_ref[0:512, :], preferred_element_type=F32)
             + jnp.dot(mb_ref[...], wo_ref[512:1024, :], preferred_element_type=F32))
        h_ref[...] = h
        r = lax.rsqrt(jnp.mean(h * h, axis=-1, keepdims=True) + EPS)
        hn = (h * r * g2_ref[...]).astype(BF16)
        hn_ref[...] = hn
        for d in range(N_DEV):
            u = jnp.maximum(jnp.dot(hn, wu_ref[d], preferred_element_type=F32), 0.0)
            hid_ref[:, FF_BLK * d:FF_BLK * (d + 1)] = (u * u).astype(BF16)
        y = h + jnp.dot(hid_ref[...], wd_ref[...], preferred_element_type=F32)
        err = y - tg_ref[...]
        dy = err * (1.0 / D_MODEL)
        dy_ref[...] = dy
        dyb_ref[...] = dy.astype(BF16)
        part =0.5 * jnp.sum(jnp.sum(err * err, axis=1, keepdims=True) * (1.0 / D_MODEL), axis=0, keepdims=True)
        loss_ref[...] += part

    def tile(w):
        return pl.BlockSpec((tm, w), lambda i: (i, 0))

    return pl.pallas_call(
        body, name="mlp_fwd", grid=(t // tm,),
        in_specs=[tile(D_MODEL), tile(512), tile(512), tile(D_MODEL), _const_spec((D_MODEL, D_MODEL)),
                  _const_spec((1, D_MODEL)), _const_spec((N_DEV, D_MODEL, FF_BLK)), _const_spec((D_FF, D_MODEL))],
        out_specs=[tile(D_MODEL), tile(D_MODEL), tile(D_FF), tile(D_MODEL), tile(D_MODEL),
                   pl.BlockSpec((8, LANES), lambda i: (0, 0))],
        out_shape=[jax.ShapeDtypeStruct((t, D_MODEL), F32), jax.ShapeDtypeStruct((t, D_MODEL), BF16),
                   jax.ShapeDtypeStruct((t, D_FF), BF16), jax.ShapeDtypeStruct((t, D_MODEL), F32),
                   jax.ShapeDtypeStruct((t, D_MODEL), BF16), jax.ShapeDtypeStruct((8, LANES), F32)],
        compiler_params=_params(("arbitrary",)),
    )(x2, ma, mb, tgt, w_out, g2, w_up, w_down)


def _mlp_bwd(dy, hid, h, ma, mb, w_down, w_up_t, w_out, g2, tm):
    t = dy.shape[0]

    def body(dy_ref, hid_ref, h_ref, ma_ref, mb_ref, wd_ref, wut_ref, wo_ref, g2_ref,
             du_ref, dh_ref, dhb_ref, dma_ref, dob_ref, dla_ref, gg_ref):
        @pl.when(pl.program_id(0) == 0)
        def _():
            gg_ref[...] = jnp.zeros_like(gg_ref)

        dy = dy_ref[...]
        d_hid = _nt(dy.astype(BF16), wd_ref[...])
        du = (d_hid * (2.0 * jnp.sqrt(hid_ref[...].astype(F32)))).astype(BF16)
        du_ref[...] = du
        d_hn = jnp.dot(du, wut_ref[...], preferred_element_type=F32)
        h = h_ref[...]
        r = lax.rsqrt(jnp.mean(h * h, axis=-1, keepdims=True) + EPS)
        hat = h * r
        gd = d_hn * g2_ref[...]
        dh = dy + r * (gd - hat * jnp.mean(gd * hat, axis=-1, keepdims=True))
        gg_ref[...] += jnp.sum(d_hn * hat, axis=0, keepdims=True)
        dh_ref[...] = dh
        dhb = dh.astype(BF16)
        dhb_ref[...] = dhb
        dm = _nt(dhb, wo_ref[...]).astype(BF16)
        dma, dmb = dm[:, 0:512], dm[:, 512:1024]
        dma_ref[...] = dma
        sel = (lax.shift_right_logical(lax.broadcasted_iota(jnp.int32, (512, LANES), 0), 6)
               == lax.broadcasted_iota(jnp.int32, (512, LANES), 1)).astype(BF16)
        dla_ref[...] = _split_dot(dma.astype(F32) * ma_ref[...].astype(F32), sel)
        dmb32 = dmb.astype(F32)
        dlb = _split_dot(dmb32 * mb_ref[...].astype(F32), sel)
        for hd in range(8):
            blk = _head_block(dmb32[:, LANES * (hd // 2):LANES * (hd // 2 + 1)], hd % 2)
            dob_ref[:, LANES * hd:LANES * (hd + 1)] = _put3(blk, L_DELTA, -dlb[:, hd:hd + 1]).astype(BF16)

    def tile(w):
        return pl.BlockSpec((tm, w), lambda i: (i, 0))

    return pl.pallas_call(
        body, name="mlp_bwd", grid=(t // tm,),
        in_specs=[tile(D_MODEL), tile(D_FF), tile(D_MODEL), tile(512), tile(512), _const_spec((D_FF, D_MODEL)),
                  _const_spec((D_FF, D_MODEL)), _const_spec((D_MODEL, D_MODEL)), _const_spec((1, D_MODEL))],
        out_specs=[tile(D_FF), tile(D_MODEL), tile(D_MODEL), tile(512), tile(8 * LANES), tile(LANES),
                   pl.BlockSpec((1, D_MODEL), lambda i: (0, 0))],
        out_shape=[jax.ShapeDtypeStruct((t, D_FF), BF16), jax.ShapeDtypeStruct((t, D_MODEL), F32),
                   jax.ShapeDtypeStruct((t, D_MODEL), BF16), jax.ShapeDtypeStruct((t, 512), BF16),
                   jax.ShapeDtypeStruct((t, 8 * LANES), BF16), jax.ShapeDtypeStruct((t, LANES), F32),
                   jax.ShapeDtypeStruct((1, D_MODEL), F32)],
        compiler_params=_params(("arbitrary",)),
    )(dy, hid, h, ma, mb, w_down, w_up_t, w_out, g2)


def _wgrad(a, b, name, bm, bn, tk, out_dtype=F32, col_blocks=False):
    t, m = a.shape
    n = b.shape[1]
    bm, bn = min(bm, m), min(bn, n)
    nk = t // tk

    def body(a_ref, b_ref, o_ref, acc):
        @pl.when(pl.program_id(2) == 0)
        def _():
            acc[...] = jnp.zeros_like(acc)

        acc[...] += _tn(a_ref[...], b_ref[...])

        @pl.when(pl.program_id(2) == nk - 1)
        def _():
            o_ref[...] = acc[...].astype(out_dtype)

    if col_blocks:
        out_spec = pl.BlockSpec((None, bm, bn), lambda i, j, k: (j, i, 0))
        out_shape = jax.ShapeDtypeStruct((n // bn, m, bn), out_dtype)
    else:
        out_spec = pl.BlockSpec((bm, bn), lambda i, j, k: (i, j))
        out_shape = jax.ShapeDtypeStruct((m, n), out_dtype)
    return pl.pallas_call(
        body, name=name, grid=(m // bm, n // bn, nk),
        in_specs=[pl.BlockSpec((tk, bm), lambda i, j, k: (k, i)), pl.BlockSpec((tk, bn), lambda i, j, k: (k, j))],
        out_specs=out_spec, out_shape=out_shape, scratch_shapes=[pltpu.VMEM((bm, bn), F32)],
        compiler_params=_params(("arbitrary", "arbitrary", "arbitrary")),
    )(a, b)


def _proj_bwd(raw, dqa, dkae, dvae, dqb, dkb, dvb, fl, bf_row, x2, dh, w_main_t, w_f_t, g1, gqa, gka, gqb, gkb, nb, s, tm):
    t = x2.shape[0]
    nt = s // tm

    def body(raw_ref, dqa_ref, dkae_ref, dvae_ref, dqb_ref, dkb_ref, dvb_ref, fl_ref, b_ref, x_ref, dh_ref,
             wmt_ref, wft_ref, g1_ref, gqa_ref, gka_ref, gqb_ref, gkb_ref,
             dx_ref, dp_ref, dfb_ref, ggqa_ref, ggka_ref, ggqb_ref, ggkb_ref, gg1_ref, gb_ref, carry, dlf_ref):
        @pl.when((pl.program_id(0) == 0) & (pl.program_id(1) == 0))
        def _():
            for r in (ggqa_ref, ggka_ref, ggqb_ref, ggkb_ref, gg1_ref, gb_ref):
                r[...] = jnp.zeros_like(r)

        @pl.when(pl.program_id(1) == 0)
        def _():
            carry[...] = jnp.zeros_like(carry)

        lane = _lane((tm, LANES))
        dc = jnp.zeros((tm, LANES), F32)
        for hd in range(8):
            col = (dqb_ref[:, LANES * hd + L_CQ:LANES * hd + L_CQ + 1] - dkb_ref[:, LANES * hd + L_CK:LANES * hd + L_CK + 1])
            dc = jnp.where(lane == hd, col, dc)
        dlf_ref[...] = jnp.dot(_tri(tm, True), dc, precision=lax.Precision.HIGHEST, preferred_element_type=F32) + carry[...]
        carry[...] = dlf_ref[pl.ds(0, 1), :]
        dfl = dlf_ref[...] * (1.0 / (1.0 + jnp.exp(fl_ref[...] + b_ref[...])))
        gb_ref[...] += jnp.sum(dfl, axis=0, keepdims=True)

        raw = raw_ref[...]
        d_qa, p_qa = _head_norm_bwd(raw[:, 0:512], gqa_ref[...], dqa_ref[...])
        d_ka, p_ka = _head_norm_bwd(raw[:, 512:640], gka_ref[...], _fold_kv(dkae_ref[...]))
        d_va = _fold_kv(dvae_ref[...])
        d_qb, p_qb = _head_norm_bwd(raw[:, 768:1280], gqb_ref[...], _to_pairs(dqb_ref) * SCALE)
        d_kb, p_kb = _head_norm_bwd(raw[:, 1280:1792], gkb_ref[...], _to_pairs(dkb_ref) * (1.0 / LOG2E))
        ggqa_ref[...] += jnp.sum(p_qa, axis=0, keepdims=True)
        ggka_ref[...] += jnp.sum(p_ka, axis=0, keepdims=True)
        ggqb_ref[...] += jnp.sum(p_qb, axis=0, keepdims=True)
        ggkb_ref[...] += jnp.sum(p_kb, axis=0, keepdims=True)
        dproj = jnp.concatenate([d_qa, d_ka, d_va, d_qb, d_kb, _to_pairs(dvb_ref)], axis=1).astype(BF16)
        dp_ref[...] = dproj
        dfb = dfl.astype(BF16)
        dfb_ref[...] = dfb
        d_xn = (jnp.dot(dproj, wmt_ref[...], preferred_element_type=F32)
                + jnp.dot(dfb, wft_ref[...], preferred_element_type=F32))
        x = x_ref[...]
        r = lax.rsqrt(jnp.mean(x * x, axis=-1, keepdims=True) + EPS)
        hat = x * r
        gd = d_xn * g1_ref[...]
        dx_ref[...] = dh_ref[...] + r * (gd - hat * jnp.mean(gd * hat, axis=-1, keepdims=True))
        gg1_ref[...] += jnp.sum(d_xn * hat, axis=0, keepdims=True)

    def tile(w):
        return pl.BlockSpec((tm, w), lambda b, i: (b * nt + (nt - 1 - i), 0))

    def acc(w):
        return pl.BlockSpec((1, w), lambda b, i: (0, 0))

    return pl.pallas_call(
        body, name="proj_bwd", grid=(nb, nt),
        in_specs=[tile(MAIN_W), tile(512), tile(512), tile(512), tile(8 * LANES), tile(8 * LANES), tile(8 * LANES), tile(LANES),
                  _const_spec((1, LANES)), tile(D_MODEL), tile(D_MODEL), _const_spec((MAIN_W, D_MODEL)),
                  _const_spec((LANES, D_MODEL)), _const_spec((1, D_MODEL)), _const_spec((1, 512)), _const_spec((1, 128)),
                  _const_spec((1, 512)), _const_spec((1, 512))],
        out_specs=[tile(D_MODEL), tile(MAIN_W), tile(LANES), acc(512), acc(128), acc(512), acc(512), acc(D_MODEL), acc(LANES)],
        out_shape=[jax.ShapeDtypeStruct((t, D_MODEL), F32), jax.ShapeDtypeStruct((t, MAIN_W), BF16),
                   jax.ShapeDtypeStruct((t, LANES), BF16), jax.ShapeDtypeStruct((1, 512), F32),
                   jax.ShapeDtypeStruct((1, 128), F32), jax.ShapeDtypeStruct((1, 512), F32),
                   jax.ShapeDtypeStruct((1, 512), F32), jax.ShapeDtypeStruct((1, D_MODEL), F32),
                   jax.ShapeDtypeStruct((1, LANES), F32)],
        scratch_shapes=[pltpu.VMEM((1, LANES), F32), pltpu.VMEM((tm, LANES), F32)],
        compiler_params=_params(("arbitrary", "arbitrary")),
    )(raw, dqa, dkae, dvae, dqb, dkb, dvb, fl, bf_row, x2, dh, w_main_t, w_f_t, g1, gqa, gka, gqb, gkb)


IN_PAD = 304


def _local_step(x, tgt, w_in_t, rest, g1, b_forget, qna, kna, sinks, qnb, knb, g2,
                tm=256, bt=512, btf=1024, tq=512, wk=4096, distributed=False):
    nb, s, _ = x.shape
    t = nb * s
    x2, tgt2 = x.reshape(t, D_MODEL), tgt.reshape(t, D_MODEL)
    g1r, g2r = g1.reshape(1, D_MODEL), g2.reshape(1, D_MODEL)
    gqa, gka = jnp.tile(qna, 8).reshape(1, 512), jnp.tile(kna, 2).reshape(1, 128)
    gqb, gkb = jnp.tile(qnb, 8).reshape(1, 512), jnp.tile(knb, 8).reshape(1, 512)
    bf_row = jnp.pad(b_forget, (0, LANES - 8)).reshape(1, LANES)
    sink_row = jnp.pad(sinks, (0, LANES - 8)).reshape(1, LANES)
    w_main_t = w_in_t[0:MAIN_W]
    w_f_t = jnp.pad(w_in_t[MAIN_W:IN_W], ((0, LANES - 8), (0, 0)))

    xn, raw, fl, qa, kae, vae, qb, kb, vb = _norm_proj(x2, g1r, w_main_t, w_f_t, gqa, gka, gqb, gkb, 2 * tm)
    q_aug, k_aug, v_aug = _fox_prep(qb, kb, vb, fl, bf_row, nb, s, 2 * tm)
    ma, lse_a = _swa_fwd(qa, kae, vae, sink_row, nb, s, tq)
    if distributed:
        mb, ql_aug, w_out, w_up, w_down, w_up_t = _fox_fwd(q_aug, k_aug, v_aug, nb, s, btf, shards=rest)
    else:
        mb, ql_aug = _fox_fwd(q_aug, k_aug, v_aug, nb, s, btf)
        w_out, w_up, w_down, w_up_t = rest
    w_out, w_down = w_out.reshape(D_MODEL, D_MODEL), w_down.reshape(D_FF, D_MODEL)
    h, hn, hid, dy, dyb, loss_acc = _mlp_fwd(x2, ma, mb, tgt2, w_out, g2r, w_up, w_down, 2 * tm)

    du, dh, dhb, dma, do_aug, dla, gg2 = _mlp_bwd(dy, hid, h, ma, mb, w_down, w_up_t.reshape(D_FF, D_MODEL), w_out, g2r, tm)
    g_down = _wgrad(hid, dyb, "wgrad_down", 512, 1024, wk, BF16).reshape(N_DEV, 512, D_MODEL)
    g_up = _wgrad(hn, du, "wgrad_up", 1024, 512, wk, BF16, col_blocks=True)
    g_out = jnp.concatenate([_wgrad(ma, dhb, "wgrad_out_a", 512, 1024, wk, BF16),
                             _wgrad(mb, dhb, "wgrad_out_b", 512, 1024, wk, BF16)], axis=0).reshape(N_DEV, 128, D_MODEL)

    dqa, dkae, dvae, dsink = _swa_bwd(qa, kae, vae, dma, sink_row, lse_a, dla, nb, s, tq)
    fox = _fox_bwd(ql_aug, k_aug, v_aug, do_aug, nb, s, bt, exch=(g_out, g_up, g_down) if distributed else ())
    dqb, dkb, dvb = fox[:3]
    if distributed:
        g_out, g_up, g_down = fox[3:]
    grad_x, dproj, dfb, ggqa, ggka, ggqb, ggkb, gg1, gbf = _proj_bwd(
        raw, dqa, dkae, dvae, dqb, dkb, dvb, fl, bf_row, x2, dh, w_main_t, w_f_t, g1r, gqa, gka, gqb, gkb, nb, s, tm)
    g_in_t = jnp.concatenate([_wgrad(dproj, xn, "wgrad_in", 768, 1024, wk), _wgrad(dfb, xn, "wgrad_gate", 128, 1024, wk)[0:8]],
                             axis=0)

    small = (gg1.reshape(D_MODEL), gbf[0, 0:8], ggqa.reshape(8, 64).sum(0), ggka.reshape(2, 64).sum(0),
             dsink.sum(0)[:, 0:2, 0].reshape(8), ggqb.reshape(8, 64).sum(0), ggkb.reshape(8, 64).sum(0),
             gg2.reshape(D_MODEL))
    return loss_acc[0, 0], grad_x.reshape(nb, s, D_MODEL), g_in_t, g_out, g_up, g_down, small


def _all_gather(shard):
    def body(x_ref, out_ref, send_sems, recv_sems, local_sem):
        start, forward, finish = _gather_steps([(x_ref, out_ref)], send_sems, recv_sems, local_sem)
        start()
        forward()
        finish()

    return pl.pallas_call(
        body, name="gather_w_in", out_shape=jax.ShapeDtypeStruct((N_DEV,) + shard.shape, shard.dtype),
        in_specs=[ANY], out_specs=ANY,
        scratch_shapes=[pltpu.SemaphoreType.DMA((N_SEM,)), pltpu.SemaphoreType.DMA((N_SEM,)), pltpu.SemaphoreType.DMA((1,))],
    )(shard)


def _exchange(*arrays):
    n_ex = len(arrays)

    def body(*refs):
        start, finish = _exchange_steps(list(zip(refs[:n_ex], refs[n_ex:2 * n_ex])), *refs[2 * n_ex:])
        start()
        finish()

    return pl.pallas_call(
        body, name="exchange_tail", out_shape=[jax.ShapeDtypeStruct(a.shape, a.dtype) for a in arrays],
        in_specs=[ANY] * n_ex, out_specs=[ANY] * n_ex,
        scratch_shapes=[pltpu.SemaphoreType.DMA((N_SEM * n_ex,)), pltpu.SemaphoreType.DMA((N_SEM * n_ex,)),
                        pltpu.SemaphoreType.DMA((n_ex,))],
    )(*arrays)


def _sum_adamw(recv, w, m, v, tr, name):
    _, r, n = recv.shape

    def body(r_ref, w_ref, m_ref, v_ref, g_ref, d_ref, nm_ref, nv_ref):
        g = r_ref[0].astype(F32)
        for s in range(1, N_DEV):
            g = g + r_ref[s].astype(F32)
        g_ref[...] = g
        nm = ADAM_B1 * m_ref[...] + (1.0 - ADAM_B1) * g
        nv = ADAM_B2 * v_ref[...] + (1.0 - ADAM_B2) * (g * g)
        m_hat = nm / (1.0 - ADAM_B1 ** ADAM_STEP)
        v_hat = nv / (1.0 - ADAM_B2 ** ADAM_STEP)
        d_ref[...] = -ADAM_LR * (m_hat / (jnp.sqrt(v_hat) + ADAM_EPS) + ADAM_WD * w_ref[...])
        nm_ref[...] = nm
        nv_ref[...] = nv

    tile = pl.BlockSpec((tr, n), lambda i: (i, 0))
    shp = jax.ShapeDtypeStruct((r, n), F32)
    return pl.pallas_call(
        body, name=name, grid=(r // tr,),
        in_specs=[pl.BlockSpec((N_DEV, tr, n), lambda i: (0, i, 0)), tile, tile, tile],
        out_specs=[tile, tile, tile, tile], out_shape=[shp, shp, shp, shp],
        compiler_params=_params(("arbitrary",)),
    )(recv, w, m, v)


def _small_rows(g1, bf, qna, kna, sk, qnb, knb, g2):
    row2 = jnp.concatenate([bf, qna, kna, sk, qnb, knb])
    return jnp.zeros((8, D_MODEL), F32).at[0].set(g1).at[1].set(g2).at[2, 0:row2.shape[0]].set(row2)


def _in_rows(w_in_s):
    return jnp.pad(w_in_s.T, ((0, IN_PAD - IN_SHARD), (0, 0)))


def kernel(x, attn_norm_g, w_in, b_forget, q_norm_a, k_norm_a, sink_logits, q_norm_b, k_norm_b, w_out, mlp_norm_g, w_up, w_down, loss_target, m_attn_norm_g, m_w_in, m_b_forget, m_q_norm_a, m_k_norm_a, m_sink_logits, m_q_norm_b, m_k_norm_b, m_w_out, m_mlp_norm_g, m_w_up, m_w_down, v_attn_norm_g, v_w_in, v_b_forget, v_q_norm_a, v_k_norm_a, v_sink_logits, v_q_norm_b, v_k_norm_b, v_w_out, v_mlp_norm_g, v_w_up, v_w_down):
    w_in_r = _in_rows(w_in)
    w_in_t = _all_gather(w_in_r.astype(BF16))[:, 0:IN_SHARD].reshape(IN_W, D_MODEL)
    w_up_b = w_up.astype(BF16)
    rest = (w_out.astype(BF16), w_up_b, w_down.astype(BF16), w_up_b.T)

    loss_part, grad_x, g_in_t, r_out, r_up, r_down, small = _local_step(
        x, loss_target, w_in_t, rest, attn_norm_g, b_forget, q_norm_a, k_norm_a, sink_logits, q_norm_b, k_norm_b, mlp_norm_g,
        distributed=True)

    g_in_blocks = jnp.pad(g_in_t.reshape(N_DEV, IN_SHARD, D_MODEL), ((0, 0), (0, IN_PAD - IN_SHARD), (0, 0))).astype(BF16)
    small_blocks = jnp.broadcast_to(_small_rows(*small).at[3, 0].set(loss_part), (N_DEV, 8, D_MODEL))
    r_in, r_small = _exchange(g_in_blocks, small_blocks)

    small_w = _small_rows(attn_norm_g, b_forget, q_norm_a, k_norm_a, sink_logits, q_norm_b, k_norm_b, mlp_norm_g)
    small_m = _small_rows(m_attn_norm_g, m_b_forget, m_q_norm_a, m_k_norm_a, m_sink_logits, m_q_norm_b, m_k_norm_b, m_mlp_norm_g)
    small_v = _small_rows(v_attn_norm_g, v_b_forget, v_q_norm_a, v_k_norm_a, v_sink_logits, v_q_norm_b, v_k_norm_b, v_mlp_norm_g)
    o_in = [a[0:IN_SHARD].T for a in _sum_adamw(r_in, w_in_r, _in_rows(m_w_in), _in_rows(v_w_in), IN_PAD, "adamw_in")]
    o_out = _sum_adamw(r_out, w_out, m_w_out, v_w_out, 128, "adamw_out")
    o_up = _sum_adamw(r_up, w_up, m_w_up, v_w_up, 256, "adamw_up")
    o_down = _sum_adamw(r_down, w_down, m_w_down, v_w_down, 128, "adamw_down")
    o_small = _sum_adamw(r_small, small_w, small_m, small_v, 8, "adamw_small")

    def leaves(i):
        row2 = o_small[i][2]
        return (o_small[i][0], o_in[i], row2[0:8], row2[8:72], row2[72:136], row2[136:144], row2[144:208], row2[208:272],
                o_out[i], o_small[i][1], o_up[i], o_down[i])

    return (o_small[0][3, 0], grad_x, *leaves(0), *leaves(1), *leaves(2), *leaves(3))
```

```python
import functools
import math

import jax
import jax.numpy as jnp
from jax import lax
from jax.experimental import pallas as pl
from jax.experimental.pallas import tpu as pltpu

F32 = jnp.float32
BF16 = jnp.bfloat16

D_MODEL = 1024
HEAD_DIM = 64
N_DEV = 8
D_FF = 4096
A_QW = 512
A_KVW = 128
B_W = 512
MAIN_W = 2304
IN_W = 2312
WINDOW = 128
EPS = 1e-6
SCALE = 0.125
LOG2E = 1.4426950408889634
LANES = 128
NEG_INF = float("-inf")

ADAM_LR = 0.001
ADAM_B1 = 0.9
ADAM_B2 = 0.999
ADAM_EPS = 1e-08
ADAM_WD = 0.01
ADAM_STEP = 10

R_OUT, R_UP, R_DOWN, R_IN = 0, 128, 640, 1152
IN_SHARD = 289
R_SMALL = 1456
R_PACK = 1472
VMEM_LIMIT = 56 * 1024 * 1024


def _params(sem, vmem=VMEM_LIMIT):
    return pltpu.CompilerParams(dimension_semantics=sem, vmem_limit_bytes=vmem)


def _const_spec(shape):
    nd = len(shape)
    return pl.BlockSpec(shape, lambda *_: (0,) * nd, pipeline_mode=pl.Buffered(1))


def _lane(shape):
    return lax.broadcasted_iota(jnp.int32, shape, len(shape) - 1)


def _split_dot(v, mat):
    hi = v.astype(BF16)
    lo = (v - hi.astype(F32)).astype(BF16)
    return (jnp.dot(hi, mat, preferred_element_type=F32) + jnp.dot(lo, mat, preferred_element_type=F32))


def _head_ones(n):
    r = lax.shift_right_logical(lax.broadcasted_iota(jnp.int32, (n, n), 0), 6)
    c = lax.shift_right_logical(lax.broadcasted_iota(jnp.int32, (n, n), 1), 6)
    return (r == c).astype(BF16)


def _head_sum(v):
    w = v.shape[1]
    vb = v.astype(BF16)
    if w <= 256:
        return jnp.dot(vb, _head_ones(w), preferred_element_type=F32)
    ones = _head_ones(256)
    return jnp.concatenate([jnp.dot(vb[:, s:s + 256], ones, preferred_element_type=F32) for s in range(0, w, 256)], axis=1)


def _head_norm(seg, gain):
    rs = lax.rsqrt(_head_sum(seg * seg) * (1.0 / HEAD_DIM) + EPS)
    return seg * rs * gain


def _head_norm_bwd(seg, gain, d_out):
    rs = lax.rsqrt(_head_sum(seg * seg) * (1.0 / HEAD_DIM) + EPS)
    hat = seg * rs
    gd = d_out * gain
    d_seg = rs * (gd - hat * (_head_sum(gd * hat) * (1.0 / HEAD_DIM)))
    return d_seg, d_out * hat


def _expand_kv(v):
    r = pltpu.roll(v, 64, axis=1)
    lo = _lane(v.shape) < 64
    return jnp.concatenate([jnp.where(lo, v, r), jnp.where(lo, r, v)], axis=1)


def _fold_kv(e4):
    t0 = e4[:, 0:128] + e4[:, 128:256]
    t1 = e4[:, 256:384] + e4[:, 384:512]
    t0 = t0 + pltpu.roll(t0, 64, axis=1)
    t1 = t1 + pltpu.roll(t1, 64, axis=1)
    return jnp.where(_lane(t0.shape) < 64, t0, t1)


def _pick_lane(blk, idx):
    return jnp.sum(jnp.where(_lane(blk.shape) == idx, blk, 0.0), axis=1, keepdims=True)


def _nt(a, b):
    return lax.dot_general(a, b, (((1,), (1,)), ((), ())), preferred_element_type=F32)


def _tn(a, b):
    return lax.dot_general(a, b, (((0,), (0,)), ((), ())), preferred_element_type=F32)


def _norm_proj(x2, g1, w_main, w_f, gqa, gka, gqb, gkb, tm):
    t = x2.shape[0]

    def body(x_ref, g1_ref, wm_ref, wf_ref, gqa_ref, gka_ref, gqb_ref, gkb_ref,
             xn_ref, raw_ref, fl_ref, qa_ref, kae_ref, vae_ref, qb_ref, kb_ref, vb_ref):
        x = x_ref[...]
        r = lax.rsqrt(jnp.mean(x * x, axis=-1, keepdims=True) + EPS)
        xn = (x * r * g1_ref[...]).astype(BF16)
        xn_ref[...] = xn
        proj = _nt(xn, wm_ref[...])
        raw_ref[...] = proj
        fl_ref[...] = _nt(xn, wf_ref[...])
        qa_ref[...] = _head_norm(proj[:, 0:512], gqa_ref[...]).astype(BF16)
        kae_ref[...] = _expand_kv(_head_norm(proj[:, 512:640], gka_ref[...])).astype(BF16)
        vae_ref[...] = _expand_kv(proj[:, 640:768]).astype(BF16)
        qb_ref[...] = (_head_norm(proj[:, 768:1280], gqb_ref[...]) * (SCALE * LOG2E)).astype(BF16)
        kb_ref[...] = _head_norm(proj[:, 1280:1792], gkb_ref[...]).astype(BF16)
        vb_ref[...] = proj[:, 1792:2304].astype(BF16)

    def tile(w):
        return pl.BlockSpec((tm, w), lambda i: (i, 0))

    return pl.pallas_call(
        body, name="norm_proj", grid=(t // tm,),
        in_specs=[tile(D_MODEL), _const_spec((1, D_MODEL)), _const_spec((MAIN_W, D_MODEL)), _const_spec((LANES, D_MODEL)),
                  _const_spec((1, 512)), _const_spec((1, 128)), _const_spec((1, 512)), _const_spec((1, 512))],
        out_specs=[tile(D_MODEL), tile(MAIN_W), tile(LANES), tile(512), tile(256), tile(256), tile(512), tile(512), tile(512)],
        out_shape=[jax.ShapeDtypeStruct((t, D_MODEL), BF16), jax.ShapeDtypeStruct((t, MAIN_W), F32),
                   jax.ShapeDtypeStruct((t, LANES), F32), jax.ShapeDtypeStruct((t, 512), BF16),
                   jax.ShapeDtypeStruct((t, 256), BF16), jax.ShapeDtypeStruct((t, 256), BF16),
                   jax.ShapeDtypeStruct((t, 512), BF16), jax.ShapeDtypeStruct((t, 512), BF16),
                   jax.ShapeDtypeStruct((t, 512), BF16)],
        compiler_params=_params(("arbitrary",)),
    )(x2, g1, w_main, w_f, gqa, gka, gqb, gkb)


def _tri(n, upper):
    r = lax.broadcasted_iota(jnp.int32, (n, n), 0)
    c = lax.broadcasted_iota(jnp.int32, (n, n), 1)
    return ((c >= r) if upper else (c <= r)).astype(F32)


def _slope(p, hh):
    out = jnp.float32(2.0 ** -(2 * 3 + hh + 1))
    for pp in (2, 1, 0):
        out = jnp.where(p == pp, jnp.float32(2.0 ** -(2 * pp + hh + 1)), out)
    return out


def _swa_windows(ref, i, tq):
    nsub = tq // WINDOW
    cur = ref[pl.ds(pl.multiple_of(i * tq, tq), tq), :].reshape(nsub, WINDOW, LANES)
    first = ref[pl.ds(pl.multiple_of(jnp.maximum(i * tq - WINDOW, 0), WINDOW), WINDOW), :].reshape(1, WINDOW, LANES)
    return jnp.concatenate([jnp.concatenate([first, cur[0:nsub - 1]], axis=0), cur], axis=1)


def _both_heads(x3, lo):
    zero = jnp.zeros_like(x3)
    return jnp.concatenate([jnp.where(lo, x3, zero), jnp.where(lo, zero, x3)], axis=0)


def _swa_head_consts(sink_ref, p, i, nsub):
    bidx = lax.broadcasted_iota(jnp.int32, (2 * nsub, 1, 1), 0)
    is_a = bidx < nsub
    slope = jnp.where(is_a, _slope(p, 0), _slope(p, 1))
    sinks = sink_ref[...]
    sink = jnp.where(is_a, _pick_lane(sinks, 2 * p).reshape(1, 1, 1), _pick_lane(sinks, 2 * p + 1).reshape(1, 1, 1))
    first = (i == 0) & ((bidx == 0) | (bidx == nsub))
    return slope, sink, first


def _swa_fwd(qa, kae, vae, sink_row, nb, s, tq):
    t = qa.shape[0]
    nq = s // tq
    nsub = tq // WINDOW

    def body(q_ref, k_ref, v_ref, sink_ref, o_ref, lse_ref):
        p, i = pl.program_id(1), pl.program_id(2)
        lo = _lane((1, 1, LANES)) < 64
        kk, vv = _swa_windows(k_ref, i, tq), _swa_windows(v_ref, i, tq)
        qs = (q_ref[...].astype(F32) * SCALE).astype(BF16).reshape(nsub, WINDOW, LANES)
        q8 = _both_heads(qs, lo)
        s8 = jnp.einsum("bqd,bkd->bqk", q8, jnp.concatenate([kk, kk], axis=0), preferred_element_type=F32)
        row = lax.broadcasted_iota(jnp.int32, (1, WINDOW, 2 * WINDOW), 1)
        col = lax.broadcasted_iota(jnp.int32, (1, WINDOW, 2 * WINDOW), 2)
        dist = row + WINDOW - col
        slope, sink, first = _swa_head_consts(sink_ref, p, i, nsub)
        valid = (dist >= 0) & (dist < WINDOW) & ((col >= WINDOW) | jnp.logical_not(first))
        s8 = jnp.where(valid, s8 - slope * dist.astype(F32), NEG_INF)
        m = jnp.maximum(jnp.max(s8, axis=2, keepdims=True), sink)
        e = jnp.exp(s8 - m)
        den = jnp.sum(e, axis=2, keepdims=True) + jnp.exp(sink - m)
        pr = (e / den).astype(BF16)
        o8 = jnp.einsum("bqk,bkd->bqd", pr, jnp.concatenate([vv, vv], axis=0), preferred_element_type=F32)
        lse8 = m + jnp.log(den)
        o_ref[...] = jnp.where(lo, o8[0:nsub], o8[nsub:]).astype(BF16).reshape(tq, LANES)
        lse_ref[...] = jnp.where(lo, lse8[0:nsub], lse8[nsub:]).reshape(tq, LANES)

    return pl.pallas_call(
        body, name="swa_fwd", grid=(nb, 4, nq),
        in_specs=[pl.BlockSpec((tq, LANES), lambda b, p, i: (b * nq + i, p)),
                  pl.BlockSpec((s, LANES), lambda b, p, i: (b, lax.shift_right_logical(p, 1))),
                  pl.BlockSpec((s, LANES), lambda b, p, i: (b, lax.shift_right_logical(p, 1))),
                  pl.BlockSpec((1, LANES), lambda b, p, i: (0, 0))],
        out_specs=[pl.BlockSpec((tq, LANES), lambda b, p, i: (b * nq + i, p)),
                   pl.BlockSpec((None, tq, LANES), lambda b, p, i: (p, b * nq + i, 0))],
        out_shape=[jax.ShapeDtypeStruct((t, 512), BF16), jax.ShapeDtypeStruct((4, t, LANES), F32)],
        compiler_params=_params(("arbitrary", "arbitrary", "arbitrary")),
    )(qa, kae, vae, sink_row)


def _swa_bwd(qa, kae, vae, do_a, sink_row, lse, delta, nb, s, tq):
    t = qa.shape[0]
    nq = s // tq
    nsub = tq // WINDOW

    def body(q_ref, do_ref, k_ref, v_ref, sink_ref, lse_ref, dl_ref, dq_ref, dk_ref, dv_ref, ds_ref):
        p, i = pl.program_id(1), pl.program_id(2)

        @pl.when(i == 0)
        def _():
            ds_ref[...] = jnp.zeros_like(ds_ref)

        lo = _lane((1, 1, LANES)) < 64
        kk, vv = _swa_windows(k_ref, i, tq), _swa_windows(v_ref, i, tq)
        kks = (kk.astype(F32) * SCALE).astype(BF16)
        k8, v8 = jnp.concatenate([kks, kks], axis=0), jnp.concatenate([vv, vv], axis=0)
        q8 = _both_heads(q_ref[...].reshape(nsub, WINDOW, LANES), lo)
        do8 = _both_heads(do_ref[...].reshape(nsub, WINDOW, LANES), lo)
        cur = pl.multiple_of(i * tq, tq)
        sub = lax.broadcasted_iota(jnp.int32, (WINDOW, WINDOW), 0)
        lse_t = [lse_ref[u * WINDOW:(u + 1) * WINDOW, :].T for u in range(nsub)]
        dl_t = [dl_ref[u * WINDOW:(u + 1) * WINDOW, :].T for u in range(nsub)]
        lse8 = jnp.concatenate([t_[64 * hh:64 * hh + 1, :].reshape(1, 1, WINDOW) for hh in range(2) for t_ in lse_t], axis=0)
        dl8 = jnp.concatenate([jnp.sum(jnp.where(sub == 2 * p + hh, t_, 0.0), axis=0, keepdims=True).reshape(1, 1, WINDOW)
                               for hh in range(2) for t_ in dl_t], axis=0)
        row = lax.broadcasted_iota(jnp.int32, (1, 2 * WINDOW, WINDOW), 1)
        col = lax.broadcasted_iota(jnp.int32, (1, 2 * WINDOW, WINDOW), 2)
        dist = col + WINDOW - row
        slope, sink, first = _swa_head_consts(sink_ref, p, i, nsub)
        valid = (dist >= 0) & (dist < WINDOW) & ((row >= WINDOW) | jnp.logical_not(first))
        st = jnp.einsum("bkd,bqd->bkq", k8, q8, preferred_element_type=F32) - slope * dist.astype(F32) - lse8
        pt = jnp.where(valid, jnp.exp(jnp.where(valid, st, 0.0)), 0.0)
        dpt = jnp.einsum("bkd,bqd->bkq", v8, do8, preferred_element_type=F32)
        dst = pt * (dpt - dl8)
        ptb, dstb = pt.astype(BF16), dst.astype(BF16)
        dv8 = jnp.einsum("bkq,bqd->bkd", ptb, do8, preferred_element_type=F32)
        dk8 = jnp.einsum("bkq,bqd->bkd", dstb, q8, preferred_element_type=F32) * SCALE
        dq8 = jnp.einsum("bkq,bkd->bqd", dstb, k8, preferred_element_type=F32)
        dq_ref[...] = jnp.where(lo, dq8[0:nsub], dq8[nsub:]).reshape(tq, LANES)

        psd = jnp.exp(sink - lse8) * dl8
        row_h = lax.broadcasted_iota(jnp.int32, (8, LANES), 0)
        for hh in range(2):
            tot = jnp.sum(jnp.sum(psd[hh * nsub:(hh + 1) * nsub], axis=2, keepdims=True), axis=0, keepdims=True)
            ds_ref[...] += jnp.where(row_h == hh, -tot.reshape(1, 1), 0.0)

        prev = pl.multiple_of(jnp.maximum(i * tq - WINDOW, 0), WINDOW)
        for g8, g_ref in ((dk8, dk_ref), (dv8, dv_ref)):
            g4 = g8[0:nsub] + g8[nsub:]
            own, before = g4[:, WINDOW:, :], g4[:, 0:WINDOW, :]
            shifted = jnp.concatenate([before[1:nsub], jnp.zeros((1, WINDOW, LANES), F32)], axis=0)
            g_ref[pl.ds(cur, tq), :] = (own + shifted).reshape(tq, LANES)
            g_ref[pl.ds(prev, WINDOW), :] += before[0]

    return pl.pallas_call(
        body, name="swa_bwd", grid=(nb, 4, nq),
        in_specs=[pl.BlockSpec((tq, LANES), lambda b, p, i: (b * nq + i, p)),
                  pl.BlockSpec((tq, LANES), lambda b, p, i: (b * nq + i, p)),
                  pl.BlockSpec((s, LANES), lambda b, p, i: (b, lax.shift_right_logical(p, 1))),
                  pl.BlockSpec((s, LANES), lambda b, p, i: (b, lax.shift_right_logical(p, 1))),
                  pl.BlockSpec((1, LANES), lambda b, p, i: (0, 0)),
                  pl.BlockSpec((None, tq, LANES), lambda b, p, i: (p, b * nq + i, 0)),
                  pl.BlockSpec((tq, LANES), lambda b, p, i: (b * nq + i, 0))],
        out_specs=[pl.BlockSpec((tq, LANES), lambda b, p, i: (b * nq + i, p)),
                   pl.BlockSpec((s, LANES), lambda b, p, i: (b, p)),
                   pl.BlockSpec((s, LANES), lambda b, p, i: (b, p)),
                   pl.BlockSpec((None, None, 8, LANES), lambda b, p, i: (b, p, 0, 0))],
        out_shape=[jax.ShapeDtypeStruct((t, 512), F32), jax.ShapeDtypeStruct((t, 512), F32),
                   jax.ShapeDtypeStruct((t, 512), F32), jax.ShapeDtypeStruct((nb, 4, 8, LANES), F32)],
        compiler_params=_params(("arbitrary", "arbitrary", "arbitrary")),
    )(qa, do_a, kae, vae, sink_row, lse, delta)


MESH = pl.DeviceIdType.MESH
ANY = pl.BlockSpec(memory_space=pl.ANY)
N_SEM = 7


def _gather_steps(pairs, send_sems, recv_sems, local_sems):
    x, y, c = lax.axis_index("x"), lax.axis_index("y"), lax.axis_index("c")
    me, sibling = (x, y, c), (x, y, 1 - c)
    chips = [(1 - x, y), (x, 1 - y), (1 - x, 1 - y)]
    mine, first, passed, landed, last = [], [], [], [], []
    for a, (x_ref, out_ref) in enumerate(pairs):
        def slot(px, py, pc, out_ref=out_ref):
            return out_ref.at[4 * px + 2 * py + pc]

        def copy(k, block, to, src=None, a=a, slot=slot):
            return pltpu.make_async_remote_copy(
                src_ref=slot(*block) if src is None else src, dst_ref=slot(*block),
                send_sem=send_sems.at[N_SEM * a + k], recv_sem=recv_sems.at[N_SEM * a + k], device_id=to, device_id_type=MESH)

        mine.append(pltpu.make_async_copy(x_ref, slot(*me), local_sems.at[a]))
        first += [copy(0, me, sibling, src=x_ref)] + [copy(1 + j, me, (*chip, c), src=x_ref) for j, chip in enumerate(chips)]
        passed += [copy(4 + j, (*chip, c), sibling) for j, chip in enumerate(chips)]
        landed += [copy(1 + j, (*chip, c), me) for j, chip in enumerate(chips)]
        last += [copy(0, sibling, me)] + [copy(4 + j, (*chip, 1 - c), me) for j, chip in enumerate(chips)]

    def start():
        for cp in mine + first:
            cp.start()

    def forward():
        for arrived, onward in zip(landed, passed):
            arrived.wait_recv()
            onward.start()

    def finish():
        for cp in last:
            cp.wait_recv()
        for cp in first + passed:
            cp.wait_send()
        for cp in mine:
            cp.wait()

    return start, forward, finish


def _exchange_steps(pairs, send_sems, recv_sems, local_sems):
    x, y, c = lax.axis_index("x"), lax.axis_index("y"), lax.axis_index("c")
    my_id = 4 * x + 2 * y + c
    local, remote = [], []
    for a, (src, dst) in enumerate(pairs):
        local.append(pltpu.make_async_copy(src.at[my_id], dst.at[my_id], local_sems.at[a]))
        for k in range(1, N_DEV):
            px = 1 - x if k & 4 else x
            py = 1 - y if k & 2 else y
            pc = 1 - c if k & 1 else c
            remote.append(pltpu.make_async_remote_copy(
                src_ref=src.at[4 * px + 2 * py + pc], dst_ref=dst.at[my_id],
                send_sem=send_sems.at[N_SEM * a + k - 1], recv_sem=recv_sems.at[N_SEM * a + k - 1],
                device_id=(px, py, pc), device_id_type=MESH))

    def start():
        for cp in local + remote:
            cp.start()

    def finish():
        for cp in remote:
            cp.wait_recv()
        for cp in remote:
            cp.wait_send()
        for cp in local:
            cp.wait()

    return start, finish


L_ONE = 64
L_CK = 65
L_CQ = 68
L_LSE = 71
L_DELTA = 74


def _head_block(pair, half):
    y = pair if half == 0 else pltpu.roll(pair, 64, axis=1)
    return jnp.where(_lane(pair.shape) < 64, y, 0.0)


def _put3(blk, lane0, col):
    lane = _lane(blk.shape)
    hi = col.astype(BF16).astype(F32)
    mid = (col - hi).astype(BF16).astype(F32)
    lo = (col - hi - mid).astype(BF16).astype(F32)
    return jnp.where(lane == lane0, hi, jnp.where(lane == lane0 + 1, mid, jnp.where(lane == lane0 + 2, lo, blk)))


def _spread3(col, shape, lane0s):
    lane = _lane(shape)
    hi = col.astype(BF16).astype(F32)
    mid = (col - hi).astype(BF16).astype(F32)
    lo = (col - hi - mid).astype(BF16).astype(F32)

    def at(k):
        return functools.reduce(jnp.logical_or, [lane == ln + k for ln in lane0s])

    return jnp.where(at(0), hi, jnp.where(at(1), mid, jnp.where(at(2), lo, 0.0)))


def _put_ones(blk, lanes):
    lane = _lane(blk.shape)
    hit = functools.reduce(jnp.logical_or, [lane == ln for ln in lanes])
    return jnp.where(hit, 1.0, blk)


def _to_pairs(ref):
    out = []
    for j in range(4):
        a, b = ref[:, 2 * LANES * j:2 * LANES * j + LANES], ref[:, 2 * LANES * j + LANES:2 * LANES * (j + 1)]
        out.append(jnp.where(_lane(a.shape) < 64, a, pltpu.roll(b, 64, axis=1)))
    return jnp.concatenate(out, axis=1)


def _fox_prep(qb, kb, vb, fl, bf_row, nb, s, tm):
    t = qb.shape[0]
    nt = s // tm

    def body(q_ref, k_ref, v_ref, fl_ref, b_ref, qo_ref, ko_ref, vo_ref, carry, c_ref):
        @pl.when(pl.program_id(1) == 0)
        def _():
            carry[...] = jnp.zeros_like(carry)

        z = fl_ref[...] + b_ref[...]
        e = jnp.exp(-jnp.abs(z))
        u = 1.0 + e
        log1p = jnp.where(u == 1.0, e, jnp.log(u) * (e / (u - 1.0)))
        lf = jnp.minimum(z, 0.0) - log1p
        tri = _tri(256, False)
        for r0 in range(0, tm, 256):
            c_ref[r0:r0 + 256, :] = (jnp.dot(tri, lf[r0:r0 + 256], precision=lax.Precision.HIGHEST, preferred_element_type=F32)
                                     + carry[...])
            carry[...] = c_ref[pl.ds(r0 + 255, 1), :]
        c2 = c_ref[...] * LOG2E
        for h in range(8):
            j, half = h // 2, h % 2
            pair, blk = slice(LANES * j, LANES * (j + 1)), slice(LANES * h, LANES * (h + 1))
            feat = _spread3(c2[:, h:h + 1], (tm, LANES), (L_CK, L_CQ))
            lane = _lane((tm, LANES))
            q = _put_ones(_head_block(q_ref[:, pair].astype(F32), half), (L_CK, L_CK + 1, L_CK + 2))
            qo_ref[:, blk] = jnp.where((lane >= L_CQ) & (lane < L_CQ + 3), feat, q).astype(BF16)
            k = _put_ones(_head_block(k_ref[:, pair].astype(F32), half), tuple(range(L_CQ, L_CQ + 6)))
            ko_ref[:, blk] = jnp.where((lane >= L_CK) & (lane < L_CK + 3), -feat, k).astype(BF16)
            v = _head_block(v_ref[:, pair].astype(F32), half)
            vo_ref[:, blk] = _put_ones(v, (L_ONE, L_DELTA, L_DELTA + 1, L_DELTA + 2)).astype(BF16)

    def tile(w):
        return pl.BlockSpec((tm, w), lambda b, i: (b * nt + i, 0))

    shp = jax.ShapeDtypeStruct((t, 8 * LANES), BF16)
    return pl.pallas_call(
        body, name="fox_prep", grid=(nb, nt),
        in_specs=[tile(512), tile(512), tile(512), tile(LANES), _const_spec((1, LANES))],
        out_specs=[tile(8 * LANES)] * 3, out_shape=[shp, shp, shp],
        scratch_shapes=[pltpu.VMEM((1, LANES), F32), pltpu.VMEM((tm, LANES), F32)],
        compiler_params=_params(("arbitrary", "arbitrary")),
    )(qb, kb, vb, fl, bf_row)


def _fox_fwd(q_aug, k_aug, v_aug, nb, s, bt, shards=()):
    t = q_aug.shape[0]
    nq = s // bt
    n_in, n_sh = 3, len(shards)

    def body(*refs):
        q_ref, k_ref, v_ref = refs[:n_in]
        o_ref, ql_ref = refs[n_in + n_sh:n_in + n_sh + 2]
        if shards:
            srcs, dsts = refs[n_in:n_in + n_sh], refs[n_in + n_sh + 2:n_in + 2 * n_sh + 2]
            start, forward, finish = _gather_steps(list(zip(srcs, dsts)), *refs[n_in + 2 * n_sh + 2:])
            step = (pl.program_id(0) * 4 + pl.program_id(1)) * nq + pl.program_id(2)
            pl.when(step == 0)(start)
            pl.when(step == nb * 3 * nq)(forward)
        i = pl.program_id(2)
        row = lax.broadcasted_iota(jnp.int32, (bt, bt), 0)
        col = lax.broadcasted_iota(jnp.int32, (bt, bt), 1)
        sls = [slice(LANES * hh, LANES * (hh + 1)) for hh in range(2)]
        qhs = [q_ref[:, sl] for sl in sls]

        def blk(kb_i, carry, diag):
            start = pl.multiple_of(kb_i * bt, bt)
            new = []
            for (m, acc), qh, sl in zip(carry, qhs, sls):
                sc = _nt(qh, k_ref[pl.ds(start, bt), sl])
                if diag:
                    sc = jnp.where(row >= col, sc, NEG_INF)
                m_new = jnp.maximum(m, jnp.max(sc, axis=1, keepdims=True))
                pr = jnp.exp2(sc - m_new).astype(BF16)
                acc = jnp.exp2(m - m_new) * acc + jnp.dot(pr, v_ref[pl.ds(start, bt), sl], preferred_element_type=F32)
                new.append((m_new, acc))
            return tuple(new)

        init = tuple((jnp.full((bt, 1), NEG_INF, F32), jnp.zeros((bt, LANES), F32)) for _ in range(2))
        carry = lax.fori_loop(0, i, lambda kb_i, c: blk(kb_i, c, False), init)
        outs = []
        for (m, acc), qh, sl in zip(blk(i, carry, True), qhs, sls):
            l = acc[:, L_ONE:L_ONE + 1]
            outs.append(acc / l)
            ql_ref[:, sl] = _put3(qh.astype(F32), L_LSE, -(m + jnp.log(l) * LOG2E)).astype(BF16)
        o_ref[...] = jnp.where(_lane((1, LANES)) < 64, outs[0], pltpu.roll(outs[1], 64, axis=1)).astype(BF16)
        if shards:
            pl.when(step == nb * 4 * nq - 1)(finish)

    in_specs = [pl.BlockSpec((bt, 2 * LANES), lambda b, j, i: (b * nq + i, j)),
                pl.BlockSpec((s, 2 * LANES), lambda b, j, i: (b, j)),
                pl.BlockSpec((s, 2 * LANES), lambda b, j, i: (b, j))]
    out_specs = [pl.BlockSpec((bt, LANES), lambda b, j, i: (b * nq + i, j)),
                 pl.BlockSpec((bt, 2 * LANES), lambda b, j, i: (b * nq + i, j))]
    out_shape = [jax.ShapeDtypeStruct((t, 512), BF16), jax.ShapeDtypeStruct((t, 8 * LANES), BF16)]
    args, scratch = [q_aug, k_aug, v_aug, *shards], []
    if shards:
        in_specs += [ANY] * n_sh
        out_specs += [ANY] * n_sh
        out_shape += [jax.ShapeDtypeStruct((N_DEV,) + sh.shape, sh.dtype) for sh in shards]
        scratch = [pltpu.SemaphoreType.DMA((N_SEM * n_sh,)), pltpu.SemaphoreType.DMA((N_SEM * n_sh,)),
                   pltpu.SemaphoreType.DMA((n_sh,))]
    return pl.pallas_call(
        body, name="fox_fwd", grid=(nb, 4, nq), in_specs=in_specs, out_specs=out_specs, out_shape=out_shape,
        scratch_shapes=scratch, compiler_params=_params(("arbitrary", "arbitrary", "arbitrary")),
    )(*args)


def _fox_bwd(ql_aug, k_aug, v_aug, do_aug, nb, s, bt, exch=()):
    t = ql_aug.shape[0]
    nk = s // bt
    n_in, n_out, n_ex = 4, 3, len(exch)

    def body(*refs):
        q_ref, do_ref, k_ref, v_ref = refs[:n_in]
        dq_ref, dk_ref, dv_ref = refs[n_in + n_ex:n_in + n_ex + n_out]
        if exch:
            srcs = refs[n_in:n_in + n_ex]
            dsts = refs[n_in + n_ex + n_out:n_in + 2 * n_ex + n_out]
            start, finish = _exchange_steps(list(zip(srcs, dsts)), *refs[n_in + 2 * n_ex + n_out:])
            step = (pl.program_id(0) * 4 + pl.program_id(1)) * nk + pl.program_id(2)
            pl.when(step == 0)(start)
        kb_i = pl.program_id(2)

        @pl.when(kb_i == 0)
        def _():
            dq_ref[...] = jnp.zeros_like(dq_ref)

        row = lax.broadcasted_iota(jnp.int32, (bt, bt), 0)
        col = lax.broadcasted_iota(jnp.int32, (bt, bt), 1)
        sls = [slice(LANES * hh, LANES * (hh + 1)) for hh in range(2)]
        khs, vhs = [k_ref[:, sl] for sl in sls], [v_ref[:, sl] for sl in sls]

        def blk(qi, carry, diag):
            start = pl.multiple_of(qi * bt, bt)
            new = []
            for (dk_a, dv_a), kh, vh, sl in zip(carry, khs, vhs, sls):
                qblk, doblk = q_ref[pl.ds(start, bt), sl], do_ref[pl.ds(start, bt), sl]
                st = _nt(kh, qblk)
                if diag:
                    pt = jnp.where(col >= row, jnp.exp2(jnp.where(col >= row, st, 0.0)), 0.0)
                else:
                    pt = jnp.exp2(st)
                dst = pt * _nt(vh, doblk)
                ptb, dstb = pt.astype(BF16), dst.astype(BF16)
                dv_a = dv_a + jnp.dot(ptb, doblk, preferred_element_type=F32)
                dk_a = dk_a + jnp.dot(dstb, qblk, preferred_element_type=F32)
                dq_ref[pl.ds(start, bt), sl] += _tn(dstb, kh)
                new.append((dk_a, dv_a))
            return tuple(new)

        zero = jnp.zeros((bt, LANES), F32)
        carry = blk(kb_i, ((zero, zero), (zero, zero)), True)
        carry = lax.fori_loop(kb_i + 1, nk, lambda qi, c: blk(qi, c, False), carry)
        for (dk_acc, dv_acc), sl in zip(carry, sls):
            dk_ref[:, sl] = dk_acc
            dv_ref[:, sl] = dv_acc
        if exch:
            pl.when(step == nb * 4 * nk - 1)(finish)

    scratch = []
    if exch:
        scratch = [pltpu.SemaphoreType.DMA((N_SEM * n_ex,)), pltpu.SemaphoreType.DMA((N_SEM * n_ex,)),
                   pltpu.SemaphoreType.DMA((n_ex,))]
    whole = pl.BlockSpec((s, 2 * LANES), lambda b, j, kb_i: (b, j))
    tile = pl.BlockSpec((bt, 2 * LANES), lambda b, j, kb_i: (b * nk + kb_i, j))
    shp = jax.ShapeDtypeStruct((t, 8 * LANES), F32)
    return pl.pallas_call(
        body, name="fox_bwd", grid=(nb, 4, nk),
        in_specs=[whole, whole, tile, tile] + [ANY] * n_ex,
        out_specs=[whole, tile, tile] + [ANY] * n_ex,
        out_shape=[shp, shp, shp] + [jax.ShapeDtypeStruct(e.shape, e.dtype) for e in exch],
        scratch_shapes=scratch, compiler_params=_params(("arbitrary", "arbitrary", "arbitrary")),
    )(ql_aug, do_aug, k_aug, v_aug, *exch)


FF_BLK = D_FF // N_DEV


def _mlp_fwd(x2, ma, mb, tgt, w_out, g2, w_up, w_down, tm):
    t = x2.shape[0]

    def body(x_ref, ma_ref, mb_ref, tg_ref, wo_ref, g2_ref, wu_ref, wd_ref,
             h_ref, hn_ref, hid_ref, dy_ref, dyb_ref, loss_ref):
        @pl.when(pl.program_id(0) == 0)
        def _():
            loss_ref[...] = jnp.zeros_like(loss_ref)

        h = (x_ref[...] + jnp.dot(ma_ref[...], wo_ref[0:512, :], preferred_element_type=F32)
             + jnp.dot(mb_ref[...], wo_ref[512:1024, :], preferred_element_type=F32))
        h_ref[...] = h
        r = lax.rsqrt(jnp.mean(h * h, axis=-1, keepdims=True) + EPS)
        hn = (h * r * g2_ref[...]).astype(BF16)
        hn_ref[...] = hn
        for d in range(N_DEV):
            u = jnp.maximum(jnp.dot(hn, wu_ref[d], preferred_element_type=F32), 0.0)
            hid_ref[:, FF_BLK * d:FF_BLK * (d + 1)] = (u * u).astype(BF16)
        y = h + jnp.dot(hid_ref[...], wd_ref[...], preferred_element_type=F32)
        err = y - tg_ref[...]
        dy = err * (1.0 / D_MODEL)
        dy_ref[...] = dy
        dyb_ref[...] = dy.astype(BF16)
        part =0.5 * jnp.sum(jnp.sum(err * err, axis=1, keepdims=True) * (1.0 / D_MODEL), axis=0, keepdims=True)
        loss_ref[...] += part

    def tile(w):
        return pl.BlockSpec((tm, w), lambda i: (i, 0))

    return pl.pallas_call(
        body, name="mlp_fwd", grid=(t // tm,),
        in_specs=[tile(D_MODEL), tile(512), tile(512), tile(D_MODEL), _const_spec((D_MODEL, D_MODEL)),
                  _const_spec((1, D_MODEL)), _const_spec((N_DEV, D_MODEL, FF_BLK)), _const_spec((D_FF, D_MODEL))],
        out_specs=[tile(D_MODEL), tile(D_MODEL), tile(D_FF), tile(D_MODEL), tile(D_MODEL),
                   pl.BlockSpec((8, LANES), lambda i: (0, 0))],
        out_shape=[jax.ShapeDtypeStruct((t, D_MODEL), F32), jax.ShapeDtypeStruct((t, D_MODEL), BF16),
                   jax.ShapeDtypeStruct((t, D_FF), BF16), jax.ShapeDtypeStruct((t, D_MODEL), F32),
                   jax.ShapeDtypeStruct((t, D_MODEL), BF16), jax.ShapeDtypeStruct((8, LANES), F32)],
        compiler_params=_params(("arbitrary",)),
    )(x2, ma, mb, tgt, w_out, g2, w_up, w_down)


def _mlp_bwd(dy, hid, h, ma, mb, w_down, w_up_t, w_out, g2, tm):
    t = dy.shape[0]

    def body(dy_ref, hid_ref, h_ref, ma_ref, mb_ref, wd_ref, wut_ref, wo_ref, g2_ref,
             du_ref, dh_ref, dhb_ref, dma_ref, dob_ref, dla_ref, gg_ref):
        @pl.when(pl.program_id(0) == 0)
        def _():
            gg_ref[...] = jnp.zeros_like(gg_ref)

        dy = dy_ref[...]
        d_hid = _nt(dy.astype(BF16), wd_ref[...])
        du = (d_hid * (2.0 * jnp.sqrt(hid_ref[...].astype(F32)))).astype(BF16)
        du_ref[...] = du
        d_hn = jnp.dot(du, wut_ref[...], preferred_element_type=F32)
        h = h_ref[...]
        r = lax.rsqrt(jnp.mean(h * h, axis=-1, keepdims=True) + EPS)
        hat = h * r
        gd = d_hn * g2_ref[...]
        dh = dy + r * (gd - hat * jnp.mean(gd * hat, axis=-1, keepdims=True))
        gg_ref[...] += jnp.sum(d_hn * hat, axis=0, keepdims=True)
        dh_ref[...] = dh
        dhb = dh.astype(BF16)
        dhb_ref[...] = dhb
        dm = _nt(dhb, wo_ref[...]).astype(BF16)
        dma, dmb = dm[:, 0:512], dm[:, 512:1024]
        dma_ref[...] = dma
        sel = (lax.shift_right_logical(lax.broadcasted_iota(jnp.int32, (512, LANES), 0), 6)
               == lax.broadcasted_iota(jnp.int32, (512, LANES), 1)).astype(BF16)
        dla_ref[...] = _split_dot(dma.astype(F32) * ma_ref[...].astype(F32), sel)
        dmb32 = dmb.astype(F32)
        dlb = _split_dot(dmb32 * mb_ref[...].astype(F32), sel)
        for hd in range(8):
            blk = _head_block(dmb32[:, LANES * (hd // 2):LANES * (hd // 2 + 1)], hd % 2)
            dob_ref[:, LANES * hd:LANES * (hd + 1)] = _put3(blk, L_DELTA, -dlb[:, hd:hd + 1]).astype(BF16)

    def tile(w):
        return pl.BlockSpec((tm, w), lambda i: (i, 0))

    return pl.pallas_call(
        body, name="mlp_bwd", grid=(t // tm,),
        in_specs=[tile(D_MODEL), tile(D_FF), tile(D_MODEL), tile(512), tile(512), _const_spec((D_FF, D_MODEL)),
                  _const_spec((D_FF, D_MODEL)), _const_spec((D_MODEL, D_MODEL)), _const_spec((1, D_MODEL))],
        out_specs=[tile(D_FF), tile(D_MODEL), tile(D_MODEL), tile(512), tile(8 * LANES), tile(LANES),
                   pl.BlockSpec((1, D_MODEL), lambda i: (0, 0))],
        out_shape=[jax.ShapeDtypeStruct((t, D_FF), BF16), jax.ShapeDtypeStruct((t, D_MODEL), F32),
                   jax.ShapeDtypeStruct((t, D_MODEL), BF16), jax.ShapeDtypeStruct((t, 512), BF16),
                   jax.ShapeDtypeStruct((t, 8 * LANES), BF16), jax.ShapeDtypeStruct((t, LANES), F32),
                   jax.ShapeDtypeStruct((1, D_MODEL), F32)],
        compiler_params=_params(("arbitrary",)),
    )(dy, hid, h, ma, mb, w_down, w_up_t, w_out, g2)


def _wgrad(a, b, name, bm, bn, tk, out_dtype=F32, col_blocks=False):
    t, m = a.shape
    n = b.shape[1]
    bm, bn = min(bm, m), min(bn, n)
    nk = t // tk

    def body(a_ref, b_ref, o_ref, acc):
        @pl.when(pl.program_id(2) == 0)
        def _():
            acc[...] = jnp.zeros_like(acc)

        acc[...] += _tn(a_ref[...], b_ref[...])

        @pl.when(pl.program_id(2) == nk - 1)
        def _():
            o_ref[...] = acc[...].astype(out_dtype)

    if col_blocks:
        out_spec = pl.BlockSpec((None, bm, bn), lambda i, j, k: (j, i, 0))
        out_shape = jax.ShapeDtypeStruct((n // bn, m, bn), out_dtype)
    else:
        out_spec = pl.BlockSpec((bm, bn), lambda i, j, k: (i, j))
        out_shape = jax.ShapeDtypeStruct((m, n), out_dtype)
    return pl.pallas_call(
        body, name=name, grid=(m // bm, n // bn, nk),
        in_specs=[pl.BlockSpec((tk, bm), lambda i, j, k: (k, i)), pl.BlockSpec((tk, bn), lambda i, j, k: (k, j))],
        out_specs=out_spec, out_shape=out_shape, scratch_shapes=[pltpu.VMEM((bm, bn), F32)],
        compiler_params=_params(("arbitrary", "arbitrary", "arbitrary")),
    )(a, b)


def _proj_bwd(raw, dqa, dkae, dvae, dqb, dkb, dvb, fl, bf_row, x2, dh, w_main_t, w_f_t, g1, gqa, gka, gqb, gkb, nb, s, tm):
    t = x2.shape[0]
    nt = s // tm

    def body(raw_ref, dqa_ref, dkae_ref, dvae_ref, dqb_ref, dkb_ref, dvb_ref, fl_ref, b_ref, x_ref, dh_ref,
             wmt_ref, wft_ref, g1_ref, gqa_ref, gka_ref, gqb_ref, gkb_ref,
             dx_ref, dp_ref, dfb_ref, ggqa_ref, ggka_ref, ggqb_ref, ggkb_ref, gg1_ref, gb_ref, carry, dlf_ref):
        @pl.when((pl.program_id(0) == 0) & (pl.program_id(1) == 0))
        def _():
            for r in (ggqa_ref, ggka_ref, ggqb_ref, ggkb_ref, gg1_ref, gb_ref):
                r[...] = jnp.zeros_like(r)

        @pl.when(pl.program_id(1) == 0)
        def _():
            carry[...] = jnp.zeros_like(carry)

        lane = _lane((tm, LANES))
        dc = jnp.zeros((tm, LANES), F32)
        for hd in range(8):
            col = (dqb_ref[:, LANES * hd + L_CQ:LANES * hd + L_CQ + 1] - dkb_ref[:, LANES * hd + L_CK:LANES * hd + L_CK + 1])
            dc = jnp.where(lane == hd, col, dc)
        dlf_ref[...] = jnp.dot(_tri(tm, True), dc, precision=lax.Precision.HIGHEST, preferred_element_type=F32) + carry[...]
        carry[...] = dlf_ref[pl.ds(0, 1), :]
        dfl = dlf_ref[...] * (1.0 / (1.0 + jnp.exp(fl_ref[...] + b_ref[...])))
        gb_ref[...] += jnp.sum(dfl, axis=0, keepdims=True)

        raw = raw_ref[...]
        d_qa, p_qa = _head_norm_bwd(raw[:, 0:512], gqa_ref[...], dqa_ref[...])
        d_ka, p_ka = _head_norm_bwd(raw[:, 512:640], gka_ref[...], _fold_kv(dkae_ref[...]))
        d_va = _fold_kv(dvae_ref[...])
        d_qb, p_qb = _head_norm_bwd(raw[:, 768:1280], gqb_ref[...], _to_pairs(dqb_ref) * SCALE)
        d_kb, p_kb = _head_norm_bwd(raw[:, 1280:1792], gkb_ref[...], _to_pairs(dkb_ref) * (1.0 / LOG2E))
        ggqa_ref[...] += jnp.sum(p_qa, axis=0, keepdims=True)
        ggka_ref[...] += jnp.sum(p_ka, axis=0, keepdims=True)
        ggqb_ref[...] += jnp.sum(p_qb, axis=0, keepdims=True)
        ggkb_ref[...] += jnp.sum(p_kb, axis=0, keepdims=True)
        dproj = jnp.concatenate([d_qa, d_ka, d_va, d_qb, d_kb, _to_pairs(dvb_ref)], axis=1).astype(BF16)
        dp_ref[...] = dproj
        dfb = dfl.astype(BF16)
        dfb_ref[...] = dfb
        d_xn = (jnp.dot(dproj, wmt_ref[...], preferred_element_type=F32)
                + jnp.dot(dfb, wft_ref[...], preferred_element_type=F32))
        x = x_ref[...]
        r = lax.rsqrt(jnp.mean(x * x, axis=-1, keepdims=True) + EPS)
        hat = x * r
        gd = d_xn * g1_ref[...]
        dx_ref[...] = dh_ref[...] + r * (gd - hat * jnp.mean(gd * hat, axis=-1, keepdims=True))
        gg1_ref[...] += jnp.sum(d_xn * hat, axis=0, keepdims=True)

    def tile(w):
        return pl.BlockSpec((tm, w), lambda b, i: (b * nt + (nt - 1 - i), 0))

    def acc(w):
        return pl.BlockSpec((1, w), lambda b, i: (0, 0))

    return pl.pallas_call(
        body, name="proj_bwd", grid=(nb, nt),
        in_specs=[tile(MAIN_W), tile(512), tile(512), tile(512), tile(8 * LANES), tile(8 * LANES), tile(8 * LANES), tile(LANES),
                  _const_spec((1, LANES)), tile(D_MODEL), tile(D_MODEL), _const_spec((MAIN_W, D_MODEL)),
                  _const_spec((LANES, D_MODEL)), _const_spec((1, D_MODEL)), _const_spec((1, 512)), _const_spec((1, 128)),
                  _const_spec((1, 512)), _const_spec((1, 512))],
        out_specs=[tile(D_MODEL), tile(MAIN_W), tile(LANES), acc(512), acc(128), acc(512), acc(512), acc(D_MODEL), acc(LANES)],
        out_shape=[jax.ShapeDtypeStruct((t, D_MODEL), F32), jax.ShapeDtypeStruct((t, MAIN_W), BF16),
                   jax.ShapeDtypeStruct((t, LANES), BF16), jax.ShapeDtypeStruct((1, 512), F32),
                   jax.ShapeDtypeStruct((1, 128), F32), jax.ShapeDtypeStruct((1, 512), F32),
                   jax.ShapeDtypeStruct((1, 512), F32), jax.ShapeDtypeStruct((1, D_MODEL), F32),
                   jax.ShapeDtypeStruct((1, LANES), F32)],
        scratch_shapes=[pltpu.VMEM((1, LANES), F32), pltpu.VMEM((tm, LANES), F32)],
        compiler_params=_params(("arbitrary", "arbitrary")),
    )(raw, dqa, dkae, dvae, dqb, dkb, dvb, fl, bf_row, x2, dh, w_main_t, w_f_t, g1, gqa, gka, gqb, gkb)


IN_PAD = 304


def _local_step(x, tgt, w_in_t, rest, g1, b_forget, qna, kna, sinks, qnb, knb, g2,
                tm=256, bt=512, btf=1024, tq=512, wk=4096, distributed=False):
    nb, s, _ = x.shape
    t = nb * s
    x2, tgt2 = x.reshape(t, D_MODEL), tgt.reshape(t, D_MODEL)
    g1r, g2r = g1.reshape(1, D_MODEL), g2.reshape(1, D_MODEL)
    gqa, gka = jnp.tile(qna, 8).reshape(1, 512), jnp.tile(kna, 2).reshape(1, 128)
    gqb, gkb = jnp.tile(qnb, 8).reshape(1, 512), jnp.tile(knb, 8).reshape(1, 512)
    bf_row = jnp.pad(b_forget, (0, LANES - 8)).reshape(1, LANES)
    sink_row = jnp.pad(sinks, (0, LANES - 8)).reshape(1, LANES)
    w_main_t = w_in_t[0:MAIN_W]
    w_f_t = jnp.pad(w_in_t[MAIN_W:IN_W], ((0, LANES - 8), (0, 0)))

    xn, raw, fl, qa, kae, vae, qb, kb, vb = _norm_proj(x2, g1r, w_main_t, w_f_t, gqa, gka, gqb, gkb, 2 * tm)
    q_aug, k_aug, v_aug = _fox_prep(qb, kb, vb, fl, bf_row, nb, s, 2 * tm)
    ma, lse_a = _swa_fwd(qa, kae, vae, sink_row, nb, s, tq)
    if distributed:
        mb, ql_aug, w_out, w_up, w_down, w_up_t = _fox_fwd(q_aug, k_aug, v_aug, nb, s, btf, shards=rest)
    else:
        mb, ql_aug = _fox_fwd(q_aug, k_aug, v_aug, nb, s, btf)
        w_out, w_up, w_down, w_up_t = rest
    w_out, w_down = w_out.reshape(D_MODEL, D_MODEL), w_down.reshape(D_FF, D_MODEL)
    h, hn, hid, dy, dyb, loss_acc = _mlp_fwd(x2, ma, mb, tgt2, w_out, g2r, w_up, w_down, 2 * tm)

    du, dh, dhb, dma, do_aug, dla, gg2 = _mlp_bwd(dy, hid, h, ma, mb, w_down, w_up_t.reshape(D_FF, D_MODEL), w_out, g2r, tm)
    g_down = _wgrad(hid, dyb, "wgrad_down", 512, 1024, wk, BF16).reshape(N_DEV, 512, D_MODEL)
    g_up = _wgrad(hn, du, "wgrad_up", 1024, 512, wk, BF16, col_blocks=True)
    g_out = jnp.concatenate([_wgrad(ma, dhb, "wgrad_out_a", 512, 1024, wk, BF16),
                             _wgrad(mb, dhb, "wgrad_out_b", 512, 1024, wk, BF16)], axis=0).reshape(N_DEV, 128, D_MODEL)

    dqa, dkae, dvae, dsink = _swa_bwd(qa, kae, vae, dma, sink_row, lse_a, dla, nb, s, tq)
    fox = _fox_bwd(ql_aug, k_aug, v_aug, do_aug, nb, s, bt, exch=(g_out, g_up, g_down) if distributed else ())
    dqb, dkb, dvb = fox[:3]
    if distributed:
        g_out, g_up, g_down = fox[3:]
    grad_x, dproj, dfb, ggqa, ggka, ggqb, ggkb, gg1, gbf = _proj_bwd(
        raw, dqa, dkae, dvae, dqb, dkb, dvb, fl, bf_row, x2, dh, w_main_t, w_f_t, g1r, gqa, gka, gqb, gkb, nb, s, tm)
    g_in_t = jnp.concatenate([_wgrad(dproj, xn, "wgrad_in", 768, 1024, wk), _wgrad(dfb, xn, "wgrad_gate", 128, 1024, wk)[0:8]],
                             axis=0)

    small = (gg1.reshape(D_MODEL), gbf[0, 0:8], ggqa.reshape(8, 64).sum(0), ggka.reshape(2, 64).sum(0),
             dsink.sum(0)[:, 0:2, 0].reshape(8), ggqb.reshape(8, 64).sum(0), ggkb.reshape(8, 64).sum(0),
             gg2.reshape(D_MODEL))
    return loss_acc[0, 0], grad_x.reshape(nb, s, D_MODEL), g_in_t, g_out, g_up, g_down, small


def _all_gather(shard):
    def body(x_ref, out_ref, send_sems, recv_sems, local_sem):
        start, forward, finish = _gather_steps([(x_ref, out_ref)], send_sems, recv_sems, local_sem)
        start()
        forward()
        finish()

    return pl.pallas_call(
        body, name="gather_w_in", out_shape=jax.ShapeDtypeStruct((N_DEV,) + shard.shape, shard.dtype),
        in_specs=[ANY], out_specs=ANY,
        scratch_shapes=[pltpu.SemaphoreType.DMA((N_SEM,)), pltpu.SemaphoreType.DMA((N_SEM,)), pltpu.SemaphoreType.DMA((1,))],
    )(shard)


def _exchange(*arrays):
    n_ex = len(arrays)

    def body(*refs):
        start, finish = _exchange_steps(list(zip(refs[:n_ex], refs[n_ex:2 * n_ex])), *refs[2 * n_ex:])
        start()
        finish()

    return pl.pallas_call(
        body, name="exchange_tail", out_shape=[jax.ShapeDtypeStruct(a.shape, a.dtype) for a in arrays],
        in_specs=[ANY] * n_ex, out_specs=[ANY] * n_ex,
        scratch_shapes=[pltpu.SemaphoreType.DMA((N_SEM * n_ex,)), pltpu.SemaphoreType.DMA((N_SEM * n_ex,)),
                        pltpu.SemaphoreType.DMA((n_ex,))],
    )(*arrays)


def _sum_adamw(recv, w, m, v, tr, name):
    _, r, n = recv.shape

    def body(r_ref, w_ref, m_ref, v_ref, g_ref, d_ref, nm_ref, nv_ref):
        g = r_ref[0].astype(F32)
        for s in range(1, N_DEV):
            g = g + r_ref[s].astype(F32)
        g_ref[...] = g
        nm = ADAM_B1 * m_ref[...] + (1.0 - ADAM_B1) * g
        nv = ADAM_B2 * v_ref[...] + (1.0 - ADAM_B2) * (g * g)
        m_hat = nm / (1.0 - ADAM_B1 ** ADAM_STEP)
        v_hat = nv / (1.0 - ADAM_B2 ** ADAM_STEP)
        d_ref[...] = -ADAM_LR * (m_hat / (jnp.sqrt(v_hat) + ADAM_EPS) + ADAM_WD * w_ref[...])
        nm_ref[...] = nm
        nv_ref[...] = nv

    tile = pl.BlockSpec((tr, n), lambda i: (i, 0))
    shp = jax.ShapeDtypeStruct((r, n), F32)
    return pl.pallas_call(
        body, name=name, grid=(r // tr,),
        in_specs=[pl.BlockSpec((N_DEV, tr, n), lambda i: (0, i, 0)), tile, tile, tile],
        out_specs=[tile, tile, tile, tile], out_shape=[shp, shp, shp, shp],
        compiler_params=_params(("arbitrary",)),
    )(recv, w, m, v)


def _small_rows(g1, bf, qna, kna, sk, qnb, knb, g2):
    row2 = jnp.concatenate([bf, qna, kna, sk, qnb, knb])
    return jnp.zeros((8, D_MODEL), F32).at[0].set(g1).at[1].set(g2).at[2, 0:row2.shape[0]].set(row2)


def _in_rows(w_in_s):
    return jnp.pad(w_in_s.T, ((0, IN_PAD - IN_SHARD), (0, 0)))


def kernel(x, attn_norm_g, w_in, b_forget, q_norm_a, k_norm_a, sink_logits, q_norm_b, k_norm_b, w_out, mlp_norm_g, w_up, w_down, loss_target, m_attn_norm_g, m_w_in, m_b_forget, m_q_norm_a, m_k_norm_a, m_sink_logits, m_q_norm_b, m_k_norm_b, m_w_out, m_mlp_norm_g, m_w_up, m_w_down, v_attn_norm_g, v_w_in, v_b_forget, v_q_norm_a, v_k_norm_a, v_sink_logits, v_q_norm_b, v_k_norm_b, v_w_out, v_mlp_norm_g, v_w_up, v_w_down):
    w_in_r = _in_rows(w_in)
    w_in_t = _all_gather(w_in_r.astype(BF16))[:, 0:IN_SHARD].reshape(IN_W, D_MODEL)
    w_up_b = w_up.astype(BF16)
    rest = (w_out.astype(BF16), w_up_b, w_down.astype(BF16), w_up_b.T)

    loss_part, grad_x, g_in_t, r_out, r_up, r_down, small = _local_step(
        x, loss_target, w_in_t, rest, attn_norm_g, b_forget, q_norm_a, k_norm_a, sink_logits, q_norm_b, k_norm_b, mlp_norm_g,
        distributed=True)

    g_in_blocks = jnp.pad(g_in_t.reshape(N_DEV, IN_SHARD, D_MODEL), ((0, 0), (0, IN_PAD - IN_SHARD), (0, 0))).astype(BF16)
    small_blocks = jnp.broadcast_to(_small_rows(*small).at[3, 0].set(loss_part), (N_DEV, 8, D_MODEL))
    r_in, r_small = _exchange(g_in_blocks, small_blocks)

    small_w = _small_rows(attn_norm_g, b_forget, q_norm_a, k_norm_a, sink_logits, q_norm_b, k_norm_b, mlp_norm_g)
    small_m = _small_rows(m_attn_norm_g, m_b_forget, m_q_norm_a, m_k_norm_a, m_sink_logits, m_q_norm_b, m_k_norm_b, m_mlp_norm_g)
    small_v = _small_rows(v_attn_norm_g, v_b_forget, v_q_norm_a, v_k_norm_a, v_sink_logits, v_q_norm_b, v_k_norm_b, v_mlp_norm_g)
    o_in = [a[0:IN_SHARD].T for a in _sum_adamw(r_in, w_in_r, _in_rows(m_w_in), _in_rows(v_w_in), IN_PAD, "adamw_in")]
    o_out = _sum_adamw(r_out, w_out, m_w_out, v_w_out, 128, "adamw_out")
    o_up = _sum_adamw(r_up, w_up, m_w_up, v_w_up, 256, "adamw_up")
    o_down = _sum_adamw(r_down, w_down, m_w_down, v_w_down, 128, "adamw_down")
    o_small = _sum_adamw(r_small, small_w, small_m, small_v, 8, "adamw_small")

    def leaves(i):
        row2 = o_small[i][2]
        return (o_small[i][0], o_in[i], row2[0:8], row2[8:72], row2[72:136], row2[136:144], row2[144:208], row2[208:272],
                o_out[i], o_small[i][1], o_up[i], o_down[i])

    return (o_small[0][3, 0], grad_x, *leaves(0), *leaves(1), *leaves(2), *leaves(3))
```

```python
import functools
import math

import jax
import jax.numpy as jnp
from jax import lax
from jax.experimental import pallas as pl
from jax.experimental.pallas import tpu as pltpu

F32 = jnp.float32
BF16 = jnp.bfloat16

D_MODEL = 1024
HEAD_DIM = 64
N_DEV = 8
D_FF = 4096
A_QW = 512
A_KVW = 128
B_W = 512
MAIN_W = 2304
IN_W = 2312
WINDOW = 128
EPS = 1e-6
SCALE = 0.125
LOG2E = 1.4426950408889634
LANES = 128
NEG_INF = float("-inf")

ADAM_LR = 0.001
ADAM_B1 = 0.9
ADAM_B2 = 0.999
ADAM_EPS = 1e-08
ADAM_WD = 0.01
ADAM_STEP = 10

R_OUT, R_UP, R_DOWN, R_IN = 0, 128, 640, 1152
IN_SHARD = 289
R_SMALL = 1456
R_PACK = 1472
VMEM_LIMIT = 56 * 1024 * 1024
VMEM_LIMIT_WIDE = 62 * 1024 * 1024


def _params(sem, vmem=VMEM_LIMIT):
    return pltpu.CompilerParams(dimension_semantics=sem, vmem_limit_bytes=vmem)


def _const_spec(shape):
    nd = len(shape)
    return pl.BlockSpec(shape, lambda *_: (0,) * nd, pipeline_mode=pl.Buffered(1))


def _lane(shape):
    return lax.broadcasted_iota(jnp.int32, shape, len(shape) - 1)


def _split_dot(v, mat):
    hi = v.astype(BF16)
    lo = (v - hi.astype(F32)).astype(BF16)
    return (jnp.dot(hi, mat, preferred_element_type=F32) + jnp.dot(lo, mat, preferred_element_type=F32))


def _head_ones(n):
    r = lax.shift_right_logical(lax.broadcasted_iota(jnp.int32, (n, n), 0), 6)
    c = lax.shift_right_logical(lax.broadcasted_iota(jnp.int32, (n, n), 1), 6)
    return (r == c).astype(BF16)


def _head_sum(v):
    w = v.shape[1]
    vb = v.astype(BF16)
    if w <= 256:
        return jnp.dot(vb, _head_ones(w), preferred_element_type=F32)
    ones = _head_ones(256)
    return jnp.concatenate([jnp.dot(vb[:, s:s + 256], ones, preferred_element_type=F32) for s in range(0, w, 256)], axis=1)


def _head_norm(seg, gain):
    rs = lax.rsqrt(_head_sum(seg * seg) * (1.0 / HEAD_DIM) + EPS)
    return seg * rs * gain


def _head_norm_bwd(seg, gain, d_out):
    rs = lax.rsqrt(_head_sum(seg * seg) * (1.0 / HEAD_DIM) + EPS)
    hat = seg * rs
    gd = d_out * gain
    d_seg = rs * (gd - hat * (_head_sum(gd * hat) * (1.0 / HEAD_DIM)))
    return d_seg, d_out * hat


def _expand_kv(v):
    r = pltpu.roll(v, 64, axis=1)
    lo = _lane(v.shape) < 64
    return jnp.concatenate([jnp.where(lo, v, r), jnp.where(lo, r, v)], axis=1)


def _fold_kv(e4):
    t0 = e4[:, 0:128] + e4[:, 128:256]
    t1 = e4[:, 256:384] + e4[:, 384:512]
    t0 = t0 + pltpu.roll(t0, 64, axis=1)
    t1 = t1 + pltpu.roll(t1, 64, axis=1)
    return jnp.where(_lane(t0.shape) < 64, t0, t1)


def _pick_lane(blk, idx):
    return jnp.sum(jnp.where(_lane(blk.shape) == idx, blk, 0.0), axis=1, keepdims=True)


def _nt(a, b):
    return lax.dot_general(a, b, (((1,), (1,)), ((), ())), preferred_element_type=F32)


def _tn(a, b):
    return lax.dot_general(a, b, (((0,), (0,)), ((), ())), preferred_element_type=F32)


def _norm_proj(x2, g1, w_main, w_f, gqa, gka, gqb, gkb, tm):
    t = x2.shape[0]

    def body(x_ref, g1_ref, wm_ref, wf_ref, gqa_ref, gka_ref, gqb_ref, gkb_ref,
             xn_ref, raw_ref, fl_ref, qa_ref, kae_ref, vae_ref, qb_ref, kb_ref, vb_ref):
        x = x_ref[...]
        r = lax.rsqrt(jnp.mean(x * x, axis=-1, keepdims=True) + EPS)
        xn = (x * r * g1_ref[...]).astype(BF16)
        xn_ref[...] = xn
        proj = _nt(xn, wm_ref[...])
        raw_ref[...] = proj
        fl_ref[...] = _nt(xn, wf_ref[...])
        qa_ref[...] = _head_norm(proj[:, 0:512], gqa_ref[...]).astype(BF16)
        kae_ref[...] = _expand_kv(_head_norm(proj[:, 512:640], gka_ref[...])).astype(BF16)
        vae_ref[...] = _expand_kv(proj[:, 640:768]).astype(BF16)
        qb_ref[...] = (_head_norm(proj[:, 768:1280], gqb_ref[...]) * (SCALE * LOG2E)).astype(BF16)
        kb_ref[...] = _head_norm(proj[:, 1280:1792], gkb_ref[...]).astype(BF16)
        vb_ref[...] = proj[:, 1792:2304].astype(BF16)

    def tile(w):
        return pl.BlockSpec((tm, w), lambda i: (i, 0))

    return pl.pallas_call(
        body, name="norm_proj", grid=(t // tm,),
        in_specs=[tile(D_MODEL), _const_spec((1, D_MODEL)), _const_spec((MAIN_W, D_MODEL)), _const_spec((LANES, D_MODEL)),
                  _const_spec((1, 512)), _const_spec((1, 128)), _const_spec((1, 512)), _const_spec((1, 512))],
        out_specs=[tile(D_MODEL), tile(MAIN_W), tile(LANES), tile(512), tile(256), tile(256), tile(512), tile(512), tile(512)],
        out_shape=[jax.ShapeDtypeStruct((t, D_MODEL), BF16), jax.ShapeDtypeStruct((t, MAIN_W), F32),
                   jax.ShapeDtypeStruct((t, LANES), F32), jax.ShapeDtypeStruct((t, 512), BF16),
                   jax.ShapeDtypeStruct((t, 256), BF16), jax.ShapeDtypeStruct((t, 256), BF16),
                   jax.ShapeDtypeStruct((t, 512), BF16), jax.ShapeDtypeStruct((t, 512), BF16),
                   jax.ShapeDtypeStruct((t, 512), BF16)],
        compiler_params=_params(("arbitrary",)),
    )(x2, g1, w_main, w_f, gqa, gka, gqb, gkb)


def _tri(n, upper):
    r = lax.broadcasted_iota(jnp.int32, (n, n), 0)
    c = lax.broadcasted_iota(jnp.int32, (n, n), 1)
    return ((c >= r) if upper else (c <= r)).astype(F32)


def _slope(p, hh):
    out = jnp.float32(2.0 ** -(2 * 3 + hh + 1))
    for pp in (2, 1, 0):
        out = jnp.where(p == pp, jnp.float32(2.0 ** -(2 * pp + hh + 1)), out)
    return out


def _swa_windows(ref, i, tq):
    nsub = tq // WINDOW
    cur = ref[pl.ds(pl.multiple_of(i * tq, tq), tq), :].reshape(nsub, WINDOW, LANES)
    first = ref[pl.ds(pl.multiple_of(jnp.maximum(i * tq - WINDOW, 0), WINDOW), WINDOW), :].reshape(1, WINDOW, LANES)
    return jnp.concatenate([jnp.concatenate([first, cur[0:nsub - 1]], axis=0), cur], axis=1)


def _both_heads(x3, lo):
    zero = jnp.zeros_like(x3)
    return jnp.concatenate([jnp.where(lo, x3, zero), jnp.where(lo, zero, x3)], axis=0)


def _swa_head_consts(sink_ref, p, i, nsub):
    bidx = lax.broadcasted_iota(jnp.int32, (2 * nsub, 1, 1), 0)
    is_a = bidx < nsub
    slope = jnp.where(is_a, _slope(p, 0), _slope(p, 1))
    sinks = sink_ref[...]
    sink = jnp.where(is_a, _pick_lane(sinks, 2 * p).reshape(1, 1, 1), _pick_lane(sinks, 2 * p + 1).reshape(1, 1, 1))
    first = (i == 0) & ((bidx == 0) | (bidx == nsub))
    return slope, sink, first


def _swa_fwd(qa, kae, vae, sink_row, nb, s, tq):
    t = qa.shape[0]
    nq = s // tq
    nsub = tq // WINDOW

    def body(q_ref, k_ref, v_ref, sink_ref, o_ref, lse_ref):
        p, i = pl.program_id(1), pl.program_id(2)
        lo = _lane((1, 1, LANES)) < 64
        kk, vv = _swa_windows(k_ref, i, tq), _swa_windows(v_ref, i, tq)
        qs = (q_ref[...].astype(F32) * SCALE).astype(BF16).reshape(nsub, WINDOW, LANES)
        q8 = _both_heads(qs, lo)
        s8 = jnp.einsum("bqd,bkd->bqk", q8, jnp.concatenate([kk, kk], axis=0), preferred_element_type=F32)
        row = lax.broadcasted_iota(jnp.int32, (1, WINDOW, 2 * WINDOW), 1)
        col = lax.broadcasted_iota(jnp.int32, (1, WINDOW, 2 * WINDOW), 2)
        dist = row + WINDOW - col
        slope, sink, first = _swa_head_consts(sink_ref, p, i, nsub)
        valid = (dist >= 0) & (dist < WINDOW) & ((col >= WINDOW) | jnp.logical_not(first))
        s8 = jnp.where(valid, s8 - slope * dist.astype(F32), NEG_INF)
        m = jnp.maximum(jnp.max(s8, axis=2, keepdims=True), sink)
        e = jnp.exp(s8 - m)
        den = jnp.sum(e, axis=2, keepdims=True) + jnp.exp(sink - m)
        pr = (e / den).astype(BF16)
        o8 = jnp.einsum("bqk,bkd->bqd", pr, jnp.concatenate([vv, vv], axis=0), preferred_element_type=F32)
        lse8 = m + jnp.log(den)
        o_ref[...] = jnp.where(lo, o8[0:nsub], o8[nsub:]).astype(BF16).reshape(tq, LANES)
        lse_ref[...] = jnp.where(lo, lse8[0:nsub], lse8[nsub:]).reshape(tq, LANES)

    return pl.pallas_call(
        body, name="swa_fwd", grid=(nb, 4, nq),
        in_specs=[pl.BlockSpec((tq, LANES), lambda b, p, i: (b * nq + i, p)),
                  pl.BlockSpec((s, LANES), lambda b, p, i: (b, lax.shift_right_logical(p, 1))),
                  pl.BlockSpec((s, LANES), lambda b, p, i: (b, lax.shift_right_logical(p, 1))),
                  pl.BlockSpec((1, LANES), lambda b, p, i: (0, 0))],
        out_specs=[pl.BlockSpec((tq, LANES), lambda b, p, i: (b * nq + i, p)),
                   pl.BlockSpec((None, tq, LANES), lambda b, p, i: (p, b * nq + i, 0))],
        out_shape=[jax.ShapeDtypeStruct((t, 512), BF16), jax.ShapeDtypeStruct((4, t, LANES), F32)],
        compiler_params=_params(("arbitrary", "arbitrary", "arbitrary")),
    )(qa, kae, vae, sink_row)


def _swa_bwd(qa, kae, vae, do_a, sink_row, lse, delta, nb, s, tq):
    t = qa.shape[0]
    nq = s // tq
    nsub = tq // WINDOW

    def body(q_ref, do_ref, k_ref, v_ref, sink_ref, lse_ref, dl_ref, dq_ref, dk_ref, dv_ref, ds_ref):
        p, i = pl.program_id(1), pl.program_id(2)

        @pl.when(i == 0)
        def _():
            ds_ref[...] = jnp.zeros_like(ds_ref)

        lo = _lane((1, 1, LANES)) < 64
        kk, vv = _swa_windows(k_ref, i, tq), _swa_windows(v_ref, i, tq)
        kks = (kk.astype(F32) * SCALE).astype(BF16)
        k8, v8 = jnp.concatenate([kks, kks], axis=0), jnp.concatenate([vv, vv], axis=0)
        q8 = _both_heads(q_ref[...].reshape(nsub, WINDOW, LANES), lo)
        do8 = _both_heads(do_ref[...].reshape(nsub, WINDOW, LANES), lo)
        cur = pl.multiple_of(i * tq, tq)
        sub = lax.broadcasted_iota(jnp.int32, (WINDOW, WINDOW), 0)
        lse_t = [lse_ref[u * WINDOW:(u + 1) * WINDOW, :].T for u in range(nsub)]
        dl_t = [dl_ref[u * WINDOW:(u + 1) * WINDOW, :].T for u in range(nsub)]
        lse8 = jnp.concatenate([t_[64 * hh:64 * hh + 1, :].reshape(1, 1, WINDOW) for hh in range(2) for t_ in lse_t], axis=0)
        dl8 = jnp.concatenate([jnp.sum(jnp.where(sub == 2 * p + hh, t_, 0.0), axis=0, keepdims=True).reshape(1, 1, WINDOW)
                               for hh in range(2) for t_ in dl_t], axis=0)
        row = lax.broadcasted_iota(jnp.int32, (1, 2 * WINDOW, WINDOW), 1)
        col = lax.broadcasted_iota(jnp.int32, (1, 2 * WINDOW, WINDOW), 2)
        dist = col + WINDOW - row
        slope, sink, first = _swa_head_consts(sink_ref, p, i, nsub)
        valid = (dist >= 0) & (dist < WINDOW) & ((row >= WINDOW) | jnp.logical_not(first))
        st = jnp.einsum("bkd,bqd->bkq", k8, q8, preferred_element_type=F32) - slope * dist.astype(F32) - lse8
        pt = jnp.where(valid, jnp.exp(jnp.where(valid, st, 0.0)), 0.0)
        dpt = jnp.einsum("bkd,bqd->bkq", v8, do8, preferred_element_type=F32)
        dst = pt * (dpt - dl8)
        ptb, dstb = pt.astype(BF16), dst.astype(BF16)
        dv8 = jnp.einsum("bkq,bqd->bkd", ptb, do8, preferred_element_type=F32)
        dk8 = jnp.einsum("bkq,bqd->bkd", dstb, q8, preferred_element_type=F32) * SCALE
        dq8 = jnp.einsum("bkq,bkd->bqd", dstb, k8, preferred_element_type=F32)
        dq_ref[...] = jnp.where(lo, dq8[0:nsub], dq8[nsub:]).reshape(tq, LANES)

        psd = jnp.exp(sink - lse8) * dl8
        row_h = lax.broadcasted_iota(jnp.int32, (8, LANES), 0)
        for hh in range(2):
            tot = jnp.sum(jnp.sum(psd[hh * nsub:(hh + 1) * nsub], axis=2, keepdims=True), axis=0, keepdims=True)
            ds_ref[...] += jnp.where(row_h == hh, -tot.reshape(1, 1), 0.0)

        prev = pl.multiple_of(jnp.maximum(i * tq - WINDOW, 0), WINDOW)
        for g8, g_ref in ((dk8, dk_ref), (dv8, dv_ref)):
            g4 = g8[0:nsub] + g8[nsub:]
            own, before = g4[:, WINDOW:, :], g4[:, 0:WINDOW, :]
            shifted = jnp.concatenate([before[1:nsub], jnp.zeros((1, WINDOW, LANES), F32)], axis=0)
            g_ref[pl.ds(cur, tq), :] = (own + shifted).reshape(tq, LANES)
            g_ref[pl.ds(prev, WINDOW), :] += before[0]

    return pl.pallas_call(
        body, name="swa_bwd", grid=(nb, 4, nq),
        in_specs=[pl.BlockSpec((tq, LANES), lambda b, p, i: (b * nq + i, p)),
                  pl.BlockSpec((tq, LANES), lambda b, p, i: (b * nq + i, p)),
                  pl.BlockSpec((s, LANES), lambda b, p, i: (b, lax.shift_right_logical(p, 1))),
                  pl.BlockSpec((s, LANES), lambda b, p, i: (b, lax.shift_right_logical(p, 1))),
                  pl.BlockSpec((1, LANES), lambda b, p, i: (0, 0)),
                  pl.BlockSpec((None, tq, LANES), lambda b, p, i: (p, b * nq + i, 0)),
                  pl.BlockSpec((tq, LANES), lambda b, p, i: (b * nq + i, 0))],
        out_specs=[pl.BlockSpec((tq, LANES), lambda b, p, i: (b * nq + i, p)),
                   pl.BlockSpec((s, LANES), lambda b, p, i: (b, p)),
                   pl.BlockSpec((s, LANES), lambda b, p, i: (b, p)),
                   pl.BlockSpec((None, None, 8, LANES), lambda b, p, i: (b, p, 0, 0))],
        out_shape=[jax.ShapeDtypeStruct((t, 512), F32), jax.ShapeDtypeStruct((t, 512), F32),
                   jax.ShapeDtypeStruct((t, 512), F32), jax.ShapeDtypeStruct((nb, 4, 8, LANES), F32)],
        compiler_params=_params(("arbitrary", "arbitrary", "arbitrary")),
    )(qa, do_a, kae, vae, sink_row, lse, delta)


MESH = pl.DeviceIdType.MESH
ANY = pl.BlockSpec(memory_space=pl.ANY)
N_SEM = 7


def _gather_steps(pairs, send_sems, recv_sems, local_sems):
    x, y, c = lax.axis_index("x"), lax.axis_index("y"), lax.axis_index("c")
    me, sibling = (x, y, c), (x, y, 1 - c)
    chips = [(1 - x, y), (x, 1 - y), (1 - x, 1 - y)]
    mine, first, passed, landed, last = [], [], [], [], []
    for a, (x_ref, out_ref) in enumerate(pairs):
        def slot(px, py, pc, out_ref=out_ref):
            return out_ref.at[4 * px + 2 * py + pc]

        def copy(k, block, to, src=None, a=a, slot=slot):
            return pltpu.make_async_remote_copy(
                src_ref=slot(*block) if src is None else src, dst_ref=slot(*block),
                send_sem=send_sems.at[N_SEM * a + k], recv_sem=recv_sems.at[N_SEM * a + k], device_id=to, device_id_type=MESH)

        mine.append(pltpu.make_async_copy(x_ref, slot(*me), local_sems.at[a]))
        first += [copy(0, me, sibling, src=x_ref)] + [copy(1 + j, me, (*chip, c), src=x_ref) for j, chip in enumerate(chips)]
        passed += [copy(4 + j, (*chip, c), sibling) for j, chip in enumerate(chips)]
        landed += [copy(1 + j, (*chip, c), me) for j, chip in enumerate(chips)]
        last += [copy(0, sibling, me)] + [copy(4 + j, (*chip, 1 - c), me) for j, chip in enumerate(chips)]

    def start():
        for cp in mine + first:
            cp.start()

    def forward():
        for arrived, onward in zip(landed, passed):
            arrived.wait_recv()
            onward.start()

    def finish():
        for cp in last:
            cp.wait_recv()
        for cp in first + passed:
            cp.wait_send()
        for cp in mine:
            cp.wait()

    return start, forward, finish


def _exchange_steps(pairs, send_sems, recv_sems, local_sems):
    x, y, c = lax.axis_index("x"), lax.axis_index("y"), lax.axis_index("c")
    my_id = 4 * x + 2 * y + c
    local, remote = [], []
    for a, (src, dst) in enumerate(pairs):
        local.append(pltpu.make_async_copy(src.at[my_id], dst.at[my_id], local_sems.at[a]))
        for k in range(1, N_DEV):
            px = 1 - x if k & 4 else x
            py = 1 - y if k & 2 else y
            pc = 1 - c if k & 1 else c
            remote.append(pltpu.make_async_remote_copy(
                src_ref=src.at[4 * px + 2 * py + pc], dst_ref=dst.at[my_id],
                send_sem=send_sems.at[N_SEM * a + k - 1], recv_sem=recv_sems.at[N_SEM * a + k - 1],
                device_id=(px, py, pc), device_id_type=MESH))

    def start():
        for cp in local + remote:
            cp.start()

    def finish():
        for cp in remote:
            cp.wait_recv()
        for cp in remote:
            cp.wait_send()
        for cp in local:
            cp.wait()

    return start, finish


L_ONE = 64
L_CK = 65
L_CQ = 68
L_LSE = 71
L_DELTA = 74


def _head_block(pair, half):
    y = pair if half == 0 else pltpu.roll(pair, 64, axis=1)
    return jnp.where(_lane(pair.shape) < 64, y, 0.0)


def _put3(blk, lane0, col):
    lane = _lane(blk.shape)
    hi = col.astype(BF16).astype(F32)
    mid = (col - hi).astype(BF16).astype(F32)
    lo = (col - hi - mid).astype(BF16).astype(F32)
    return jnp.where(lane == lane0, hi, jnp.where(lane == lane0 + 1, mid, jnp.where(lane == lane0 + 2, lo, blk)))


def _spread3(col, shape, lane0s):
    lane = _lane(shape)
    hi = col.astype(BF16).astype(F32)
    mid = (col - hi).astype(BF16).astype(F32)
    lo = (col - hi - mid).astype(BF16).astype(F32)

    def at(k):
        return functools.reduce(jnp.logical_or, [lane == ln + k for ln in lane0s])

    return jnp.where(at(0), hi, jnp.where(at(1), mid, jnp.where(at(2), lo, 0.0)))


def _put_ones(blk, lanes):
    lane = _lane(blk.shape)
    hit = functools.reduce(jnp.logical_or, [lane == ln for ln in lanes])
    return jnp.where(hit, 1.0, blk)


def _to_pairs(ref):
    out = []
    for j in range(4):
        a, b = ref[:, 2 * LANES * j:2 * LANES * j + LANES], ref[:, 2 * LANES * j + LANES:2 * LANES * (j + 1)]
        out.append(jnp.where(_lane(a.shape) < 64, a, pltpu.roll(b, 64, axis=1)))
    return jnp.concatenate(out, axis=1)


def _fox_prep(qb, kb, vb, fl, bf_row, nb, s, tm):
    t = qb.shape[0]
    nt = s // tm

    def body(q_ref, k_ref, v_ref, fl_ref, b_ref, qo_ref, ko_ref, vo_ref, carry, c_ref):
        @pl.when(pl.program_id(1) == 0)
        def _():
            carry[...] = jnp.zeros_like(carry)

        z = fl_ref[...] + b_ref[...]
        e = jnp.exp(-jnp.abs(z))
        u = 1.0 + e
        log1p = jnp.where(u == 1.0, e, jnp.log(u) * (e / (u - 1.0)))
        lf = jnp.minimum(z, 0.0) - log1p
        tri = _tri(256, False)
        for r0 in range(0, tm, 256):
            c_ref[r0:r0 + 256, :] = (jnp.dot(tri, lf[r0:r0 + 256], precision=lax.Precision.HIGHEST, preferred_element_type=F32)
                                     + carry[...])
            carry[...] = c_ref[pl.ds(r0 + 255, 1), :]
        c2 = c_ref[...] * LOG2E
        for h in range(8):
            j, half = h // 2, h % 2
            pair, blk = slice(LANES * j, LANES * (j + 1)), slice(LANES * h, LANES * (h + 1))
            feat = _spread3(c2[:, h:h + 1], (tm, LANES), (L_CK, L_CQ))
            lane = _lane((tm, LANES))
            q = _put_ones(_head_block(q_ref[:, pair].astype(F32), half), (L_CK, L_CK + 1, L_CK + 2))
            qo_ref[:, blk] = jnp.where((lane >= L_CQ) & (lane < L_CQ + 3), feat, q).astype(BF16)
            k = _put_ones(_head_block(k_ref[:, pair].astype(F32), half), tuple(range(L_CQ, L_CQ + 6)))
            ko_ref[:, blk] = jnp.where((lane >= L_CK) & (lane < L_CK + 3), -feat, k).astype(BF16)
            v = _head_block(v_ref[:, pair].astype(F32), half)
            vo_ref[:, blk] = _put_ones(v, (L_ONE, L_DELTA, L_DELTA + 1, L_DELTA + 2)).astype(BF16)

    def tile(w):
        return pl.BlockSpec((tm, w), lambda b, i: (b * nt + i, 0))

    shp = jax.ShapeDtypeStruct((t, 8 * LANES), BF16)
    return pl.pallas_call(
        body, name="fox_prep", grid=(nb, nt),
        in_specs=[tile(512), tile(512), tile(512), tile(LANES), _const_spec((1, LANES))],
        out_specs=[tile(8 * LANES)] * 3, out_shape=[shp, shp, shp],
        scratch_shapes=[pltpu.VMEM((1, LANES), F32), pltpu.VMEM((tm, LANES), F32)],
        compiler_params=_params(("arbitrary", "arbitrary")),
    )(qb, kb, vb, fl, bf_row)


def _fox_fwd(q_aug, k_aug, v_aug, nb, s, bt, shards=()):
    t = q_aug.shape[0]
    nq = s // bt
    n_in, n_sh = 3, len(shards)

    def body(*refs):
        q_ref, k_ref, v_ref = refs[:n_in]
        o_ref, ql_ref = refs[n_in + n_sh:n_in + n_sh + 2]
        if shards:
            srcs, dsts = refs[n_in:n_in + n_sh], refs[n_in + n_sh + 2:n_in + 2 * n_sh + 2]
            start, forward, finish = _gather_steps(list(zip(srcs, dsts)), *refs[n_in + 2 * n_sh + 2:])
            step = (pl.program_id(0) * 4 + pl.program_id(1)) * nq + pl.program_id(2)
            pl.when(step == 0)(start)
            pl.when(step == nb * 3 * nq)(forward)
        i = pl.program_id(2)
        row = lax.broadcasted_iota(jnp.int32, (bt, bt), 0)
        col = lax.broadcasted_iota(jnp.int32, (bt, bt), 1)
        sls = [slice(LANES * hh, LANES * (hh + 1)) for hh in range(2)]
        qhs = [q_ref[:, sl] for sl in sls]

        def blk(kb_i, carry, diag):
            start = pl.multiple_of(kb_i * bt, bt)
            new = []
            for (m, acc), qh, sl in zip(carry, qhs, sls):
                sc = _nt(qh, k_ref[pl.ds(start, bt), sl])
                if diag:
                    sc = jnp.where(row >= col, sc, NEG_INF)
                m_new = jnp.maximum(m, jnp.max(sc, axis=1, keepdims=True))
                pr = jnp.exp2(sc - m_new).astype(BF16)
                acc = jnp.exp2(m - m_new) * acc + jnp.dot(pr, v_ref[pl.ds(start, bt), sl], preferred_element_type=F32)
                new.append((m_new, acc))
            return tuple(new)

        init = tuple((jnp.full((bt, 1), NEG_INF, F32), jnp.zeros((bt, LANES), F32)) for _ in range(2))
        carry = lax.fori_loop(0, i, lambda kb_i, c: blk(kb_i, c, False), init)
        outs = []
        for (m, acc), qh, sl in zip(blk(i, carry, True), qhs, sls):
            l = acc[:, L_ONE:L_ONE + 1]
            outs.append(acc / l)
            ql_ref[:, sl] = _put3(qh.astype(F32), L_LSE, -(m + jnp.log(l) * LOG2E)).astype(BF16)
        o_ref[...] = jnp.where(_lane((1, LANES)) < 64, outs[0], pltpu.roll(outs[1], 64, axis=1)).astype(BF16)
        if shards:
            pl.when(step == nb * 4 * nq - 1)(finish)

    in_specs = [pl.BlockSpec((bt, 2 * LANES), lambda b, j, i: (b * nq + i, j)),
                pl.BlockSpec((s, 2 * LANES), lambda b, j, i: (b, j)),
                pl.BlockSpec((s, 2 * LANES), lambda b, j, i: (b, j))]
    out_specs = [pl.BlockSpec((bt, LANES), lambda b, j, i: (b * nq + i, j)),
                 pl.BlockSpec((bt, 2 * LANES), lambda b, j, i: (b * nq + i, j))]
    out_shape = [jax.ShapeDtypeStruct((t, 512), BF16), jax.ShapeDtypeStruct((t, 8 * LANES), BF16)]
    args, scratch = [q_aug, k_aug, v_aug, *shards], []
    if shards:
        in_specs += [ANY] * n_sh
        out_specs += [ANY] * n_sh
        out_shape += [jax.ShapeDtypeStruct((N_DEV,) + sh.shape, sh.dtype) for sh in shards]
        scratch = [pltpu.SemaphoreType.DMA((N_SEM * n_sh,)), pltpu.SemaphoreType.DMA((N_SEM * n_sh,)),
                   pltpu.SemaphoreType.DMA((n_sh,))]
    return pl.pallas_call(
        body, name="fox_fwd", grid=(nb, 4, nq), in_specs=in_specs, out_specs=out_specs, out_shape=out_shape,
        scratch_shapes=scratch, compiler_params=_params(("arbitrary", "arbitrary", "arbitrary")),
    )(*args)


def _fox_bwd(ql_aug, k_aug, v_aug, do_aug, nb, s, bt, exch=()):
    t = ql_aug.shape[0]
    nk = s // bt
    n_in, n_out, n_ex = 4, 3, len(exch)

    def body(*refs):
        q_ref, do_ref, k_ref, v_ref = refs[:n_in]
        dq_ref, dk_ref, dv_ref = refs[n_in + n_ex:n_in + n_ex + n_out]
        if exch:
            srcs = refs[n_in:n_in + n_ex]
            dsts = refs[n_in + n_ex + n_out:n_in + 2 * n_ex + n_out]
            start, finish = _exchange_steps(list(zip(srcs, dsts)), *refs[n_in + 2 * n_ex + n_out:])
            step = (pl.program_id(0) * 4 + pl.program_id(1)) * nk + pl.program_id(2)
            pl.when(step == 0)(start)
        kb_i = pl.program_id(2)

        @pl.when(kb_i == 0)
        def _():
            dq_ref[...] = jnp.zeros_like(dq_ref)

        row = lax.broadcasted_iota(jnp.int32, (bt, bt), 0)
        col = lax.broadcasted_iota(jnp.int32, (bt, bt), 1)
        sls = [slice(LANES * hh, LANES * (hh + 1)) for hh in range(2)]
        khs, vhs = [k_ref[:, sl] for sl in sls], [v_ref[:, sl] for sl in sls]

        def blk(qi, carry, diag):
            start = pl.multiple_of(qi * bt, bt)
            new = []
            for (dk_a, dv_a), kh, vh, sl in zip(carry, khs, vhs, sls):
                qblk, doblk = q_ref[pl.ds(start, bt), sl], do_ref[pl.ds(start, bt), sl]
                st = _nt(kh, qblk)
                if diag:
                    pt = jnp.where(col >= row, jnp.exp2(jnp.where(col >= row, st, 0.0)), 0.0)
                else:
                    pt = jnp.exp2(st)
                dst = pt * _nt(vh, doblk)
                ptb, dstb = pt.astype(BF16), dst.astype(BF16)
                dv_a = dv_a + jnp.dot(ptb, doblk, preferred_element_type=F32)
                dk_a = dk_a + jnp.dot(dstb, qblk, preferred_element_type=F32)
                dq_ref[pl.ds(start, bt), sl] += _tn(dstb, kh)
                new.append((dk_a, dv_a))
            return tuple(new)

        zero = jnp.zeros((bt, LANES), F32)
        carry = blk(kb_i, ((zero, zero), (zero, zero)), True)
        carry = lax.fori_loop(kb_i + 1, nk, lambda qi, c: blk(qi, c, False), carry)
        for (dk_acc, dv_acc), sl in zip(carry, sls):
            dk_ref[:, sl] = dk_acc
            dv_ref[:, sl] = dv_acc
        if exch:
            pl.when(step == nb * 4 * nk - 1)(finish)

    scratch = []
    if exch:
        scratch = [pltpu.SemaphoreType.DMA((N_SEM * n_ex,)), pltpu.SemaphoreType.DMA((N_SEM * n_ex,)),
                   pltpu.SemaphoreType.DMA((n_ex,))]
    whole = pl.BlockSpec((s, 2 * LANES), lambda b, j, kb_i: (b, j))
    tile = pl.BlockSpec((bt, 2 * LANES), lambda b, j, kb_i: (b * nk + kb_i, j))
    shp = jax.ShapeDtypeStruct((t, 8 * LANES), F32)
    return pl.pallas_call(
        body, name="fox_bwd", grid=(nb, 4, nk),
        in_specs=[whole, whole, tile, tile] + [ANY] * n_ex,
        out_specs=[whole, tile, tile] + [ANY] * n_ex,
        out_shape=[shp, shp, shp] + [jax.ShapeDtypeStruct(e.shape, e.dtype) for e in exch],
        scratch_shapes=scratch, compiler_params=_params(("arbitrary", "arbitrary", "arbitrary")),
    )(ql_aug, do_aug, k_aug, v_aug, *exch)


FF_BLK = D_FF // N_DEV


def _mlp_fwd(x2, ma, mb, tgt, w_out, g2, w_up, w_down, tm):
    t = x2.shape[0]

    def body(x_ref, ma_ref, mb_ref, tg_ref, wo_ref, g2_ref, wu_ref, wd_ref,
             h_ref, hn_ref, hid_ref, dy_ref, dyb_ref, loss_ref):
        @pl.when(pl.program_id(0) == 0)
        def _():
            loss_ref[...] = jnp.zeros_like(loss_ref)

        h = (x_ref[...] + jnp.dot(ma_ref[...], wo_ref[0:512, :], preferred_element_type=F32)
             + jnp.dot(mb_ref[...], wo_ref[512:1024, :], preferred_element_type=F32))
        h_ref[...] = h
        r = lax.rsqrt(jnp.mean(h * h, axis=-1, keepdims=True) + EPS)
        hn = (h * r * g2_ref[...]).astype(BF16)
        hn_ref[...] = hn
        for d in range(N_DEV):
            u = jnp.maximum(jnp.dot(hn, wu_ref[d], preferred_element_type=F32), 0.0)
            hid_ref[:, FF_BLK * d:FF_BLK * (d + 1)] = (u * u).astype(BF16)
        y = h + jnp.dot(hid_ref[...], wd_ref[...], preferred_element_type=F32)
        err = y - tg_ref[...]
        dy = err * (1.0 / D_MODEL)
        dy_ref[...] = dy
        dyb_ref[...] = dy.astype(BF16)
        part =0.5 * jnp.sum(jnp.sum(err * err, axis=1, keepdims=True) * (1.0 / D_MODEL), axis=0, keepdims=True)
        loss_ref[...] += part

    def tile(w):
        return pl.BlockSpec((tm, w), lambda i: (i, 0))

    return pl.pallas_call(
        body, name="mlp_fwd", grid=(t // tm,),
        in_specs=[tile(D_MODEL), tile(512), tile(512), tile(D_MODEL), _const_spec((D_MODEL, D_MODEL)),
                  _const_spec((1, D_MODEL)), _const_spec((N_DEV, D_MODEL, FF_BLK)), _const_spec((D_FF, D_MODEL))],
        out_specs=[tile(D_MODEL), tile(D_MODEL), tile(D_FF), tile(D_MODEL), tile(D_MODEL),
                   pl.BlockSpec((8, LANES), lambda i: (0, 0))],
        out_shape=[jax.ShapeDtypeStruct((t, D_MODEL), F32), jax.ShapeDtypeStruct((t, D_MODEL), BF16),
                   jax.ShapeDtypeStruct((t, D_FF), BF16), jax.ShapeDtypeStruct((t, D_MODEL), F32),
                   jax.ShapeDtypeStruct((t, D_MODEL), BF16), jax.ShapeDtypeStruct((8, LANES), F32)],
        compiler_params=_params(("arbitrary",)),
    )(x2, ma, mb, tgt, w_out, g2, w_up, w_down)


def _mlp_bwd(dy, hid, h, ma, mb, w_down, w_up_t, w_out, g2, tm):
    t = dy.shape[0]

    def body(dy_ref, hid_ref, h_ref, ma_ref, mb_ref, wd_ref, wut_ref, wo_ref, g2_ref,
             du_ref, dh_ref, dhb_ref, dma_ref, dob_ref, dla_ref, gg_ref):
        @pl.when(pl.program_id(0) == 0)
        def _():
            gg_ref[...] = jnp.zeros_like(gg_ref)

        dy = dy_ref[...]
        d_hid = _nt(dy.astype(BF16), wd_ref[...])
        du = (d_hid * (2.0 * jnp.sqrt(hid_ref[...].astype(F32)))).astype(BF16)
        du_ref[...] = du
        d_hn = jnp.dot(du, wut_ref[...], preferred_element_type=F32)
        h = h_ref[...]
        r = lax.rsqrt(jnp.mean(h * h, axis=-1, keepdims=True) + EPS)
        hat = h * r
        gd = d_hn * g2_ref[...]
        dh = dy + r * (gd - hat * jnp.mean(gd * hat, axis=-1, keepdims=True))
        gg_ref[...] += jnp.sum(d_hn * hat, axis=0, keepdims=True)
        dh_ref[...] = dh
        dhb = dh.astype(BF16)
        dhb_ref[...] = dhb
        dm = _nt(dhb, wo_ref[...]).astype(BF16)
        dma, dmb = dm[:, 0:512], dm[:, 512:1024]
        dma_ref[...] = dma
        sel = (lax.shift_right_logical(lax.broadcasted_iota(jnp.int32, (512, LANES), 0), 6)
               == lax.broadcasted_iota(jnp.int32, (512, LANES), 1)).astype(BF16)
        dla_ref[...] = _split_dot(dma.astype(F32) * ma_ref[...].astype(F32), sel)
        dmb32 = dmb.astype(F32)
        dlb = _split_dot(dmb32 * mb_ref[...].astype(F32), sel)
        for hd in range(8):
            blk = _head_block(dmb32[:, LANES * (hd // 2):LANES * (hd // 2 + 1)], hd % 2)
            dob_ref[:, LANES * hd:LANES * (hd + 1)] = _put3(blk, L_DELTA, -dlb[:, hd:hd + 1]).astype(BF16)

    def tile(w):
        return pl.BlockSpec((tm, w), lambda i: (i, 0))

    return pl.pallas_call(
        body, name="mlp_bwd", grid=(t // tm,),
        in_specs=[tile(D_MODEL), tile(D_FF), tile(D_MODEL), tile(512), tile(512), _const_spec((D_FF, D_MODEL)),
                  _const_spec((D_FF, D_MODEL)), _const_spec((D_MODEL, D_MODEL)), _const_spec((1, D_MODEL))],
        out_specs=[tile(D_FF), tile(D_MODEL), tile(D_MODEL), tile(512), tile(8 * LANES), tile(LANES),
                   pl.BlockSpec((1, D_MODEL), lambda i: (0, 0))],
        out_shape=[jax.ShapeDtypeStruct((t, D_FF), BF16), jax.ShapeDtypeStruct((t, D_MODEL), F32),
                   jax.ShapeDtypeStruct((t, D_MODEL), BF16), jax.ShapeDtypeStruct((t, 512), BF16),
                   jax.ShapeDtypeStruct((t, 8 * LANES), BF16), jax.ShapeDtypeStruct((t, LANES), F32),
                   jax.ShapeDtypeStruct((1, D_MODEL), F32)],
        compiler_params=_params(("arbitrary",), VMEM_LIMIT_WIDE),
    )(dy, hid, h, ma, mb, w_down, w_up_t, w_out, g2)


def _wgrad(a, b, name, bm, bn, tk, out_dtype=F32, col_blocks=False):
    t, m = a.shape
    n = b.shape[1]
    bm, bn = min(bm, m), min(bn, n)
    nk = t // tk

    def body(a_ref, b_ref, o_ref, acc):
        @pl.when(pl.program_id(2) == 0)
        def _():
            acc[...] = jnp.zeros_like(acc)

        acc[...] += _tn(a_ref[...], b_ref[...])

        @pl.when(pl.program_id(2) == nk - 1)
        def _():
            o_ref[...] = acc[...].astype(out_dtype)

    if col_blocks:
        out_spec = pl.BlockSpec((None, bm, bn), lambda i, j, k: (j, i, 0))
        out_shape = jax.ShapeDtypeStruct((n // bn, m, bn), out_dtype)
    else:
        out_spec = pl.BlockSpec((bm, bn), lambda i, j, k: (i, j))
        out_shape = jax.ShapeDtypeStruct((m, n), out_dtype)
    return pl.pallas_call(
        body, name=name, grid=(m // bm, n // bn, nk),
        in_specs=[pl.BlockSpec((tk, bm), lambda i, j, k: (k, i)), pl.BlockSpec((tk, bn), lambda i, j, k: (k, j))],
        out_specs=out_spec, out_shape=out_shape, scratch_shapes=[pltpu.VMEM((bm, bn), F32)],
        compiler_params=_params(("arbitrary", "arbitrary", "arbitrary")),
    )(a, b)


def _proj_bwd(raw, dqa, dkae, dvae, dqb, dkb, dvb, fl, bf_row, x2, dh, w_main_t, w_f_t, g1, gqa, gka, gqb, gkb, nb, s, tm):
    t = x2.shape[0]
    nt = s // tm

    def body(raw_ref, dqa_ref, dkae_ref, dvae_ref, dqb_ref, dkb_ref, dvb_ref, fl_ref, b_ref, x_ref, dh_ref,
             wmt_ref, wft_ref, g1_ref, gqa_ref, gka_ref, gqb_ref, gkb_ref,
             dx_ref, dp_ref, dfb_ref, ggqa_ref, ggka_ref, ggqb_ref, ggkb_ref, gg1_ref, gb_ref, carry, dlf_ref):
        @pl.when((pl.program_id(0) == 0) & (pl.program_id(1) == 0))
        def _():
            for r in (ggqa_ref, ggka_ref, ggqb_ref, ggkb_ref, gg1_ref, gb_ref):
                r[...] = jnp.zeros_like(r)

        @pl.when(pl.program_id(1) == 0)
        def _():
            carry[...] = jnp.zeros_like(carry)

        lane = _lane((tm, LANES))
        dc = jnp.zeros((tm, LANES), F32)
        for hd in range(8):
            col = (dqb_ref[:, LANES * hd + L_CQ:LANES * hd + L_CQ + 1] - dkb_ref[:, LANES * hd + L_CK:LANES * hd + L_CK + 1])
            dc = jnp.where(lane == hd, col, dc)
        dlf_ref[...] = jnp.dot(_tri(tm, True), dc, precision=lax.Precision.HIGHEST, preferred_element_type=F32) + carry[...]
        carry[...] = dlf_ref[pl.ds(0, 1), :]
        dfl = dlf_ref[...] * (1.0 / (1.0 + jnp.exp(fl_ref[...] + b_ref[...])))
        gb_ref[...] += jnp.sum(dfl, axis=0, keepdims=True)

        raw = raw_ref[...]
        d_qa, p_qa = _head_norm_bwd(raw[:, 0:512], gqa_ref[...], dqa_ref[...])
        d_ka, p_ka = _head_norm_bwd(raw[:, 512:640], gka_ref[...], _fold_kv(dkae_ref[...]))
        d_va = _fold_kv(dvae_ref[...])
        d_qb, p_qb = _head_norm_bwd(raw[:, 768:1280], gqb_ref[...], _to_pairs(dqb_ref) * SCALE)
        d_kb, p_kb = _head_norm_bwd(raw[:, 1280:1792], gkb_ref[...], _to_pairs(dkb_ref) * (1.0 / LOG2E))
        ggqa_ref[...] += jnp.sum(p_qa, axis=0, keepdims=True)
        ggka_ref[...] += jnp.sum(p_ka, axis=0, keepdims=True)
        ggqb_ref[...] += jnp.sum(p_qb, axis=0, keepdims=True)
        ggkb_ref[...] += jnp.sum(p_kb, axis=0, keepdims=True)
        dproj = jnp.concatenate([d_qa, d_ka, d_va, d_qb, d_kb, _to_pairs(dvb_ref)], axis=1).astype(BF16)
        dp_ref[...] = dproj
        dfb = dfl.astype(BF16)
        dfb_ref[...] = dfb
        d_xn = (jnp.dot(dproj, wmt_ref[...], preferred_element_type=F32)
                + jnp.dot(dfb, wft_ref[...], preferred_element_type=F32))
        x = x_ref[...]
        r = lax.rsqrt(jnp.mean(x * x, axis=-1, keepdims=True) + EPS)
        hat = x * r
        gd = d_xn * g1_ref[...]
        dx_ref[...] = dh_ref[...] + r * (gd - hat * jnp.mean(gd * hat, axis=-1, keepdims=True))
        gg1_ref[...] += jnp.sum(d_xn * hat, axis=0, keepdims=True)

    def tile(w):
        return pl.BlockSpec((tm, w), lambda b, i: (b * nt + (nt - 1 - i), 0))

    def acc(w):
        return pl.BlockSpec((1, w), lambda b, i: (0, 0))

    return pl.pallas_call(
        body, name="proj_bwd", grid=(nb, nt),
        in_specs=[tile(MAIN_W), tile(512), tile(512), tile(512), tile(8 * LANES), tile(8 * LANES), tile(8 * LANES), tile(LANES),
                  _const_spec((1, LANES)), tile(D_MODEL), tile(D_MODEL), _const_spec((MAIN_W, D_MODEL)),
                  _const_spec((LANES, D_MODEL)), _const_spec((1, D_MODEL)), _const_spec((1, 512)), _const_spec((1, 128)),
                  _const_spec((1, 512)), _const_spec((1, 512))],
        out_specs=[tile(D_MODEL), tile(MAIN_W), tile(LANES), acc(512), acc(128), acc(512), acc(512), acc(D_MODEL), acc(LANES)],
        out_shape=[jax.ShapeDtypeStruct((t, D_MODEL), F32), jax.ShapeDtypeStruct((t, MAIN_W), BF16),
                   jax.ShapeDtypeStruct((t, LANES), BF16), jax.ShapeDtypeStruct((1, 512), F32),
                   jax.ShapeDtypeStruct((1, 128), F32), jax.ShapeDtypeStruct((1, 512), F32),
                   jax.ShapeDtypeStruct((1, 512), F32), jax.ShapeDtypeStruct((1, D_MODEL), F32),
                   jax.ShapeDtypeStruct((1, LANES), F32)],
        scratch_shapes=[pltpu.VMEM((1, LANES), F32), pltpu.VMEM((tm, LANES), F32)],
        compiler_params=_params(("arbitrary", "arbitrary"), VMEM_LIMIT_WIDE),
    )(raw, dqa, dkae, dvae, dqb, dkb, dvb, fl, bf_row, x2, dh, w_main_t, w_f_t, g1, gqa, gka, gqb, gkb)


IN_PAD = 304


def _local_step(x, tgt, w_in_t, rest, g1, b_forget, qna, kna, sinks, qnb, knb, g2,
                tm=512, bt=512, btf=1024, tq=1024, wk=4096, distributed=False):
    nb, s, _ = x.shape
    t = nb * s
    x2, tgt2 = x.reshape(t, D_MODEL), tgt.reshape(t, D_MODEL)
    g1r, g2r = g1.reshape(1, D_MODEL), g2.reshape(1, D_MODEL)
    gqa, gka = jnp.tile(qna, 8).reshape(1, 512), jnp.tile(kna, 2).reshape(1, 128)
    gqb, gkb = jnp.tile(qnb, 8).reshape(1, 512), jnp.tile(knb, 8).reshape(1, 512)
    bf_row = jnp.pad(b_forget, (0, LANES - 8)).reshape(1, LANES)
    sink_row = jnp.pad(sinks, (0, LANES - 8)).reshape(1, LANES)
    w_main_t = w_in_t[0:MAIN_W]
    w_f_t = jnp.pad(w_in_t[MAIN_W:IN_W], ((0, LANES - 8), (0, 0)))

    xn, raw, fl, qa, kae, vae, qb, kb, vb = _norm_proj(x2, g1r, w_main_t, w_f_t, gqa, gka, gqb, gkb, tm)
    q_aug, k_aug, v_aug = _fox_prep(qb, kb, vb, fl, bf_row, nb, s, tm)
    ma, lse_a = _swa_fwd(qa, kae, vae, sink_row, nb, s, tq)
    if distributed:
        mb, ql_aug, w_out, w_up, w_down, w_up_t = _fox_fwd(q_aug, k_aug, v_aug, nb, s, btf, shards=rest)
    else:
        mb, ql_aug = _fox_fwd(q_aug, k_aug, v_aug, nb, s, btf)
        w_out, w_up, w_down, w_up_t = rest
    w_out, w_down = w_out.reshape(D_MODEL, D_MODEL), w_down.reshape(D_FF, D_MODEL)
    h, hn, hid, dy, dyb, loss_acc = _mlp_fwd(x2, ma, mb, tgt2, w_out, g2r, w_up, w_down, tm)

    du, dh, dhb, dma, do_aug, dla, gg2 = _mlp_bwd(dy, hid, h, ma, mb, w_down, w_up_t.reshape(D_FF, D_MODEL), w_out, g2r, tm)
    g_down = _wgrad(hid, dyb, "wgrad_down", 512, 1024, wk, BF16).reshape(N_DEV, 512, D_MODEL)
    g_up = _wgrad(hn, du, "wgrad_up", 1024, 512, wk, BF16, col_blocks=True)
    g_out = jnp.concatenate([_wgrad(ma, dhb, "wgrad_out_a", 512, 1024, wk, BF16),
                             _wgrad(mb, dhb, "wgrad_out_b", 512, 1024, wk, BF16)], axis=0).reshape(N_DEV, 128, D_MODEL)

    dqa, dkae, dvae, dsink = _swa_bwd(qa, kae, vae, dma, sink_row, lse_a, dla, nb, s, tq)
    fox = _fox_bwd(ql_aug, k_aug, v_aug, do_aug, nb, s, bt, exch=(g_out, g_up, g_down) if distributed else ())
    dqb, dkb, dvb = fox[:3]
    if distributed:
        g_out, g_up, g_down = fox[3:]
    grad_x, dproj, dfb, ggqa, ggka, ggqb, ggkb, gg1, gbf = _proj_bwd(
        raw, dqa, dkae, dvae, dqb, dkb, dvb, fl, bf_row, x2, dh, w_main_t, w_f_t, g1r, gqa, gka, gqb, gkb, nb, s, tm)
    g_in_t = jnp.concatenate([_wgrad(dproj, xn, "wgrad_in", 768, 1024, wk), _wgrad(dfb, xn, "wgrad_gate", 128, 1024, wk)[0:8]],
                             axis=0)

    small = (gg1.reshape(D_MODEL), gbf[0, 0:8], ggqa.reshape(8, 64).sum(0), ggka.reshape(2, 64).sum(0),
             dsink.sum(0)[:, 0:2, 0].reshape(8), ggqb.reshape(8, 64).sum(0), ggkb.reshape(8, 64).sum(0),
             gg2.reshape(D_MODEL))
    return loss_acc[0, 0], grad_x.reshape(nb, s, D_MODEL), g_in_t, g_out, g_up, g_down, small


def _all_gather(shard):
    def body(x_ref, out_ref, send_sems, recv_sems, local_sem):
        start, forward, finish = _gather_steps([(x_ref, out_ref)], send_sems, recv_sems, local_sem)
        start()
        forward()
        finish()

    return pl.pallas_call(
        body, name="gather_w_in", out_shape=jax.ShapeDtypeStruct((N_DEV,) + shard.shape, shard.dtype),
        in_specs=[ANY], out_specs=ANY,
        scratch_shapes=[pltpu.SemaphoreType.DMA((N_SEM,)), pltpu.SemaphoreType.DMA((N_SEM,)), pltpu.SemaphoreType.DMA((1,))],
    )(shard)


def _exchange(*arrays):
    n_ex = len(arrays)

    def body(*refs):
        start, finish = _exchange_steps(list(zip(refs[:n_ex], refs[n_ex:2 * n_ex])), *refs[2 * n_ex:])
        start()
        finish()

    return pl.pallas_call(
        body, name="exchange_tail", out_shape=[jax.ShapeDtypeStruct(a.shape, a.dtype) for a in arrays],
        in_specs=[ANY] * n_ex, out_specs=[ANY] * n_ex,
        scratch_shapes=[pltpu.SemaphoreType.DMA((N_SEM * n_ex,)), pltpu.SemaphoreType.DMA((N_SEM * n_ex,)),
                        pltpu.SemaphoreType.DMA((n_ex,))],
    )(*arrays)


def _sum_adamw(recv, w, m, v, tr, name):
    _, r, n = recv.shape

    def body(r_ref, w_ref, m_ref, v_ref, g_ref, d_ref, nm_ref, nv_ref):
        g = r_ref[0].astype(F32)
        for s in range(1, N_DEV):
            g = g + r_ref[s].astype(F32)
        g_ref[...] = g
        nm = ADAM_B1 * m_ref[...] + (1.0 - ADAM_B1) * g
        nv = ADAM_B2 * v_ref[...] + (1.0 - ADAM_B2) * (g * g)
        m_hat = nm / (1.0 - ADAM_B1 ** ADAM_STEP)
        v_hat = nv / (1.0 - ADAM_B2 ** ADAM_STEP)
        d_ref[...] = -ADAM_LR * (m_hat / (jnp.sqrt(v_hat) + ADAM_EPS) + ADAM_WD * w_ref[...])
        nm_ref[...] = nm
        nv_ref[...] = nv

    tile = pl.BlockSpec((tr, n), lambda i: (i, 0))
    shp = jax.ShapeDtypeStruct((r, n), F32)
    return pl.pallas_call(
        body, name=name, grid=(r // tr,),
        in_specs=[pl.BlockSpec((N_DEV, tr, n), lambda i: (0, i, 0)), tile, tile, tile],
        out_specs=[tile, tile, tile, tile], out_shape=[shp, shp, shp, shp],
        compiler_params=_params(("arbitrary",)),
    )(recv, w, m, v)


def _small_rows(g1, bf, qna, kna, sk, qnb, knb, g2):
    row2 = jnp.concatenate([bf, qna, kna, sk, qnb, knb])
    return jnp.zeros((8, D_MODEL), F32).at[0].set(g1).at[1].set(g2).at[2, 0:row2.shape[0]].set(row2)


def _in_rows(w_in_s):
    return jnp.pad(w_in_s.T, ((0, IN_PAD - IN_SHARD), (0, 0)))


def kernel(x, attn_norm_g, w_in, b_forget, q_norm_a, k_norm_a, sink_logits, q_norm_b, k_norm_b, w_out, mlp_norm_g, w_up, w_down, loss_target, m_attn_norm_g, m_w_in, m_b_forget, m_q_norm_a, m_k_norm_a, m_sink_logits, m_q_norm_b, m_k_norm_b, m_w_out, m_mlp_norm_g, m_w_up, m_w_down, v_attn_norm_g, v_w_in, v_b_forget, v_q_norm_a, v_k_norm_a, v_sink_logits, v_q_norm_b, v_k_norm_b, v_w_out, v_mlp_norm_g, v_w_up, v_w_down):
    w_in_r = _in_rows(w_in)
    w_in_t = _all_gather(w_in_r.astype(BF16))[:, 0:IN_SHARD].reshape(IN_W, D_MODEL)
    w_up_b = w_up.astype(BF16)
    rest = (w_out.astype(BF16), w_up_b, w_down.astype(BF16), w_up_b.T)

    loss_part, grad_x, g_in_t, r_out, r_up, r_down, small = _local_step(
        x, loss_target, w_in_t, rest, attn_norm_g, b_forget, q_norm_a, k_norm_a, sink_logits, q_norm_b, k_norm_b, mlp_norm_g,
        distributed=True)

    g_in_blocks = jnp.pad(g_in_t.reshape(N_DEV, IN_SHARD, D_MODEL), ((0, 0), (0, IN_PAD - IN_SHARD), (0, 0))).astype(BF16)
    small_blocks = jnp.broadcast_to(_small_rows(*small).at[3, 0].set(loss_part), (N_DEV, 8, D_MODEL))
    r_in, r_small = _exchange(g_in_blocks, small_blocks)

    small_w = _small_rows(attn_norm_g, b_forget, q_norm_a, k_norm_a, sink_logits, q_norm_b, k_norm_b, mlp_norm_g)
    small_m = _small_rows(m_attn_norm_g, m_b_forget, m_q_norm_a, m_k_norm_a, m_sink_logits, m_q_norm_b, m_k_norm_b, m_mlp_norm_g)
    small_v = _small_rows(v_attn_norm_g, v_b_forget, v_q_norm_a, v_k_norm_a, v_sink_logits, v_q_norm_b, v_k_norm_b, v_mlp_norm_g)
    o_in = [a[0:IN_SHARD].T for a in _sum_adamw(r_in, w_in_r, _in_rows(m_w_in), _in_rows(v_w_in), IN_PAD, "adamw_in")]
    o_out = _sum_adamw(r_out, w_out, m_w_out, v_w_out, 128, "adamw_out")
    o_up = _sum_adamw(r_up, w_up, m_w_up, v_w_up, 256, "adamw_up")
    o_down = _sum_adamw(r_down, w_down, m_w_down, v_w_down, 128, "adamw_down")
    o_small = _sum_adamw(r_small, small_w, small_m, small_v, 8, "adamw_small")

    def leaves(i):
        row2 = o_small[i][2]
        return (o_small[i][0], o_in[i], row2[0:8], row2[8:72], row2[72:136], row2[136:144], row2[144:208], row2[208:272],
                o_out[i], o_small[i][1], o_up[i], o_down[i])

    return (o_small[0][3, 0], grad_x, *leaves(0), *leaves(1), *leaves(2), *leaves(3))
```

```python
import functools

import jax
import jax.numpy as jnp
from jax import lax
from jax.experimental import pallas as pl
from jax.experimental.pallas import tpu as pltpu

F32 = jnp.float32
BF16 = jnp.bfloat16

D_MODEL = 1024
HEAD_DIM = 64
N_DEV = 8
D_FF = 4096
MAIN_W = 2304
IN_W = 2312
IN_SHARD = 289
WINDOW = 128
EPS = 1e-6
SCALE = 0.125
LOG2E = 1.4426950408889634
LANES = 128
NEG_INF = float("-inf")

ADAM_LR = 0.001
ADAM_B1 = 0.9
ADAM_B2 = 0.999
ADAM_EPS = 1e-08
ADAM_WD = 0.01
ADAM_STEP = 10

VMEM_LIMIT = 56 * 1024 * 1024
VMEM_LIMIT_WIDE = 62 * 1024 * 1024


def _params(sem, vmem=VMEM_LIMIT):
    return pltpu.CompilerParams(dimension_semantics=sem, vmem_limit_bytes=vmem)


def _const_spec(shape):
    nd = len(shape)
    return pl.BlockSpec(shape, lambda *_: (0,) * nd, pipeline_mode=pl.Buffered(1))


def _lane(shape):
    return lax.broadcasted_iota(jnp.int32, shape, len(shape) - 1)


def _split_dot(v, mat):
    hi = v.astype(BF16)
    lo = (v - hi.astype(F32)).astype(BF16)
    return (jnp.dot(hi, mat, preferred_element_type=F32) + jnp.dot(lo, mat, preferred_element_type=F32))


def _head_ones(n):
    r = lax.shift_right_logical(lax.broadcasted_iota(jnp.int32, (n, n), 0), 6)
    c = lax.shift_right_logical(lax.broadcasted_iota(jnp.int32, (n, n), 1), 6)
    return (r == c).astype(BF16)


def _head_sum(v):
    w = v.shape[1]
    vb = v.astype(BF16)
    if w <= 256:
        return jnp.dot(vb, _head_ones(w), preferred_element_type=F32)
    ones = _head_ones(256)
    return jnp.concatenate([jnp.dot(vb[:, s:s + 256], ones, preferred_element_type=F32) for s in range(0, w, 256)], axis=1)


def _head_norm(seg, gain):
    rs = lax.rsqrt(_head_sum(seg * seg) * (1.0 / HEAD_DIM) + EPS)
    return seg * rs * gain


def _head_norm_bwd(seg, gain, d_out):
    rs = lax.rsqrt(_head_sum(seg * seg) * (1.0 / HEAD_DIM) + EPS)
    hat = seg * rs
    gd = d_out * gain
    d_seg = rs * (gd - hat * (_head_sum(gd * hat) * (1.0 / HEAD_DIM)))
    return d_seg, d_out * hat


def _expand_kv(v):
    r = pltpu.roll(v, 64, axis=1)
    lo = _lane(v.shape) < 64
    return jnp.concatenate([jnp.where(lo, v, r), jnp.where(lo, r, v)], axis=1)


def _fold_kv(e4):
    t0 = e4[:, 0:128] + e4[:, 128:256]
    t1 = e4[:, 256:384] + e4[:, 384:512]
    t0 = t0 + pltpu.roll(t0, 64, axis=1)
    t1 = t1 + pltpu.roll(t1, 64, axis=1)
    return jnp.where(_lane(t0.shape) < 64, t0, t1)


def _pick_lane(blk, idx):
    return jnp.sum(jnp.where(_lane(blk.shape) == idx, blk, 0.0), axis=1, keepdims=True)


def _nt(a, b):
    return lax.dot_general(a, b, (((1,), (1,)), ((), ())), preferred_element_type=F32)


def _tn(a, b):
    return lax.dot_general(a, b, (((0,), (0,)), ((), ())), preferred_element_type=F32)


def _norm_proj(x2, g1, w_main_t, w_f_t, gqa, gka, gqb, gkb, bf_row, s, tm):
    t = x2.shape[0]
    nt = s // tm

    def body(x_ref, g1_ref, wm_ref, wf_ref, gqa_ref, gka_ref, gqb_ref, gkb_ref, b_ref,
             xn_ref, raw_ref, fl_ref, qa_ref, kae_ref, vae_ref, qo_ref, ko_ref, vo_ref, carry, c_ref):
        @pl.when(lax.rem(pl.program_id(0), nt) == 0)
        def _():
            carry[...] = jnp.zeros_like(carry)

        x = x_ref[...]
        r = lax.rsqrt(jnp.mean(x * x, axis=-1, keepdims=True) + EPS)
        xn = (x * r * g1_ref[...]).astype(BF16)
        xn_ref[...] = xn
        proj = _nt(xn, wm_ref[...])
        raw_ref[...] = proj
        fl = _nt(xn, wf_ref[...])
        fl_ref[...] = fl
        qa_ref[...] = _head_norm(proj[:, 0:512], gqa_ref[...]).astype(BF16)
        kae_ref[...] = _expand_kv(_head_norm(proj[:, 512:640], gka_ref[...])).astype(BF16)
        vae_ref[...] = _expand_kv(proj[:, 640:768]).astype(BF16)

        z = fl + b_ref[...]
        e = jnp.exp(-jnp.abs(z))
        u = 1.0 + e
        log1p = jnp.where(u == 1.0, e, jnp.log(u) * (e / (u - 1.0)))
        lf = jnp.minimum(z, 0.0) - log1p
        tri = _tri(256, False)
        for r0 in range(0, tm, 256):
            c_ref[r0:r0 + 256, :] = (jnp.dot(tri, lf[r0:r0 + 256], precision=lax.Precision.HIGHEST, preferred_element_type=F32)
                                     + carry[...])
            carry[...] = c_ref[pl.ds(r0 + 255, 1), :]
        c2 = c_ref[...] * LOG2E
        qb = _head_norm(proj[:, 768:1280], gqb_ref[...]) * (SCALE * LOG2E)
        kb = _head_norm(proj[:, 1280:1792], gkb_ref[...])
        lane = _lane((tm, LANES))
        for h in range(8):
            j, half = h // 2, h % 2
            pair, blk = slice(LANES * j, LANES * (j + 1)), slice(LANES * h, LANES * (h + 1))
            feat = _spread3(c2[:, h:h + 1], (tm, LANES), (L_CK, L_CQ))
            q = _put_ones(_head_block(qb[:, pair], half), (L_CK, L_CK + 1, L_CK + 2))
            qo_ref[:, blk] = jnp.where((lane >= L_CQ) & (lane < L_CQ + 3), feat, q).astype(BF16)
            k = _put_ones(_head_block(kb[:, pair], half), tuple(range(L_CQ, L_CQ + 6)))
            ko_ref[:, blk] = jnp.where((lane >= L_CK) & (lane < L_CK + 3), -feat, k).astype(BF16)
            v = _head_block(proj[:, 1792 + LANES * j:1792 + LANES * (j + 1)], half)
            vo_ref[:, blk] = _put_ones(v, (L_ONE, L_DELTA, L_DELTA + 1, L_DELTA + 2)).astype(BF16)

    def tile(w):
        return pl.BlockSpec((tm, w), lambda i: (i, 0))

    aug = jax.ShapeDtypeStruct((t, 8 * LANES), BF16)
    return pl.pallas_call(
        body, name="norm_proj", grid=(t // tm,),
        in_specs=[tile(D_MODEL), _const_spec((1, D_MODEL)), _const_spec((MAIN_W, D_MODEL)), _const_spec((LANES, D_MODEL)),
                  _const_spec((1, 512)), _const_spec((1, 128)), _const_spec((1, 512)), _const_spec((1, 512)),
                  _const_spec((1, LANES))],
        out_specs=[tile(D_MODEL), tile(MAIN_W), tile(LANES), tile(512), tile(256), tile(256)] + [tile(8 * LANES)] * 3,
        out_shape=[jax.ShapeDtypeStruct((t, D_MODEL), BF16), jax.ShapeDtypeStruct((t, MAIN_W), F32),
                   jax.ShapeDtypeStruct((t, LANES), F32), jax.ShapeDtypeStruct((t, 512), BF16),
                   jax.ShapeDtypeStruct((t, 256), BF16), jax.ShapeDtypeStruct((t, 256), BF16), aug, aug, aug],
        scratch_shapes=[pltpu.VMEM((1, LANES), F32), pltpu.VMEM((tm, LANES), F32)],
        compiler_params=_params(("arbitrary",)),
    )(x2, g1, w_main_t, w_f_t, gqa, gka, gqb, gkb, bf_row)


def _tri(n, upper):
    r = lax.broadcasted_iota(jnp.int32, (n, n), 0)
    c = lax.broadcasted_iota(jnp.int32, (n, n), 1)
    return ((c >= r) if upper else (c <= r)).astype(F32)


def _slope(p, hh):
    out = jnp.float32(2.0 ** -(2 * 3 + hh + 1))
    for pp in (2, 1, 0):
        out = jnp.where(p == pp, jnp.float32(2.0 ** -(2 * pp + hh + 1)), out)
    return out


def _swa_windows(ref, i, tq):
    nsub = tq // WINDOW
    cur = ref[pl.ds(pl.multiple_of(i * tq, tq), tq), :].reshape(nsub, WINDOW, LANES)
    first = ref[pl.ds(pl.multiple_of(jnp.maximum(i * tq - WINDOW, 0), WINDOW), WINDOW), :].reshape(1, WINDOW, LANES)
    return jnp.concatenate([jnp.concatenate([first, cur[0:nsub - 1]], axis=0), cur], axis=1)


def _both_heads(x3, lo):
    zero = jnp.zeros_like(x3)
    return jnp.concatenate([jnp.where(lo, x3, zero), jnp.where(lo, zero, x3)], axis=0)


def _swa_head_consts(sink_ref, p, i, nsub):
    bidx = lax.broadcasted_iota(jnp.int32, (2 * nsub, 1, 1), 0)
    is_a = bidx < nsub
    slope = jnp.where(is_a, _slope(p, 0), _slope(p, 1))
    sinks = sink_ref[...]
    sink = jnp.where(is_a, _pick_lane(sinks, 2 * p).reshape(1, 1, 1), _pick_lane(sinks, 2 * p + 1).reshape(1, 1, 1))
    first = (i == 0) & ((bidx == 0) | (bidx == nsub))
    return slope, sink, first


def _swa_fwd(qa, kae, vae, sink_row, nb, s, tq):
    t = qa.shape[0]
    nq = s // tq
    nsub = tq // WINDOW

    def body(q_ref, k_ref, v_ref, sink_ref, o_ref, lse_ref):
        p, i = pl.program_id(1), pl.program_id(2)
        lo = _lane((1, 1, LANES)) < 64
        kk, vv = _swa_windows(k_ref, i, tq), _swa_windows(v_ref, i, tq)
        qs = (q_ref[...].astype(F32) * SCALE).astype(BF16).reshape(nsub, WINDOW, LANES)
        q8 = _both_heads(qs, lo)
        s8 = jnp.einsum("bqd,bkd->bqk", q8, jnp.concatenate([kk, kk], axis=0), preferred_element_type=F32)
        row = lax.broadcasted_iota(jnp.int32, (1, WINDOW, 2 * WINDOW), 1)
        col = lax.broadcasted_iota(jnp.int32, (1, WINDOW, 2 * WINDOW), 2)
        dist = row + WINDOW - col
        slope, sink, first = _swa_head_consts(sink_ref, p, i, nsub)
        valid = (dist >= 0) & (dist < WINDOW) & ((col >= WINDOW) | jnp.logical_not(first))
        s8 = jnp.where(valid, s8 - slope * dist.astype(F32), NEG_INF)
        m = jnp.maximum(jnp.max(s8, axis=2, keepdims=True), sink)
        e = jnp.exp(s8 - m)
        den = jnp.sum(e, axis=2, keepdims=True) + jnp.exp(sink - m)
        pr = (e / den).astype(BF16)
        o8 = jnp.einsum("bqk,bkd->bqd", pr, jnp.concatenate([vv, vv], axis=0), preferred_element_type=F32)
        lse8 = m + jnp.log(den)
        o_ref[...] = jnp.where(lo, o8[0:nsub], o8[nsub:]).astype(BF16).reshape(tq, LANES)
        lse_ref[...] = jnp.where(lo, lse8[0:nsub], lse8[nsub:]).reshape(tq, LANES)

    return pl.pallas_call(
        body, name="swa_fwd", grid=(nb, 4, nq),
        in_specs=[pl.BlockSpec((tq, LANES), lambda b, p, i: (b * nq + i, p)),
                  pl.BlockSpec((s, LANES), lambda b, p, i: (b, lax.shift_right_logical(p, 1))),
                  pl.BlockSpec((s, LANES), lambda b, p, i: (b, lax.shift_right_logical(p, 1))),
                  pl.BlockSpec((1, LANES), lambda b, p, i: (0, 0))],
        out_specs=[pl.BlockSpec((tq, LANES), lambda b, p, i: (b * nq + i, p)),
                   pl.BlockSpec((None, tq, LANES), lambda b, p, i: (p, b * nq + i, 0))],
        out_shape=[jax.ShapeDtypeStruct((t, 512), BF16), jax.ShapeDtypeStruct((4, t, LANES), F32)],
        compiler_params=_params(("arbitrary", "arbitrary", "arbitrary")),
    )(qa, kae, vae, sink_row)


def _swa_bwd(qa, kae, vae, do_a, sink_row, lse, delta, nb, s, tq):
    t = qa.shape[0]
    nq = s // tq
    nsub = tq // WINDOW

    def body(q_ref, do_ref, k_ref, v_ref, sink_ref, lse_ref, dl_ref, dq_ref, dk_ref, dv_ref, ds_ref):
        p, i = pl.program_id(1), pl.program_id(2)

        @pl.when(i == 0)
        def _():
            ds_ref[...] = jnp.zeros_like(ds_ref)

        lo = _lane((1, 1, LANES)) < 64
        kk, vv = _swa_windows(k_ref, i, tq), _swa_windows(v_ref, i, tq)
        kks = (kk.astype(F32) * SCALE).astype(BF16)
        k8, v8 = jnp.concatenate([kks, kks], axis=0), jnp.concatenate([vv, vv], axis=0)
        q8 = _both_heads(q_ref[...].reshape(nsub, WINDOW, LANES), lo)
        do8 = _both_heads(do_ref[...].reshape(nsub, WINDOW, LANES), lo)
        cur = pl.multiple_of(i * tq, tq)
        sub = lax.broadcasted_iota(jnp.int32, (WINDOW, WINDOW), 0)
        lse_t = [lse_ref[u * WINDOW:(u + 1) * WINDOW, :].T for u in range(nsub)]
        dl_t = [dl_ref[u * WINDOW:(u + 1) * WINDOW, :].T for u in range(nsub)]
        lse8 = jnp.concatenate([t_[64 * hh:64 * hh + 1, :].reshape(1, 1, WINDOW) for hh in range(2) for t_ in lse_t], axis=0)
        dl8 = jnp.concatenate([jnp.sum(jnp.where(sub == 2 * p + hh, t_, 0.0), axis=0, keepdims=True).reshape(1, 1, WINDOW)
                               for hh in range(2) for t_ in dl_t], axis=0)
        row = lax.broadcasted_iota(jnp.int32, (1, 2 * WINDOW, WINDOW), 1)
        col = lax.broadcasted_iota(jnp.int32, (1, 2 * WINDOW, WINDOW), 2)
        dist = col + WINDOW - row
        slope, sink, first = _swa_head_consts(sink_ref, p, i, nsub)
        valid = (dist >= 0) & (dist < WINDOW) & ((row >= WINDOW) | jnp.logical_not(first))
        st = jnp.einsum("bkd,bqd->bkq", k8, q8, preferred_element_type=F32) - slope * dist.astype(F32) - lse8
        pt = jnp.where(valid, jnp.exp(jnp.where(valid, st, 0.0)), 0.0)
        dpt = jnp.einsum("bkd,bqd->bkq", v8, do8, preferred_element_type=F32)
        dst = pt * (dpt - dl8)
        ptb, dstb = pt.astype(BF16), dst.astype(BF16)
        dv8 = jnp.einsum("bkq,bqd->bkd", ptb, do8, preferred_element_type=F32)
        dk8 = jnp.einsum("bkq,bqd->bkd", dstb, q8, preferred_element_type=F32) * SCALE
        dq8 = jnp.einsum("bkq,bkd->bqd", dstb, k8, preferred_element_type=F32)
        dq_ref[...] = jnp.where(lo, dq8[0:nsub], dq8[nsub:]).reshape(tq, LANES)

        psd = jnp.exp(sink - lse8) * dl8
        row_h = lax.broadcasted_iota(jnp.int32, (8, LANES), 0)
        for hh in range(2):
            tot = jnp.sum(jnp.sum(psd[hh * nsub:(hh + 1) * nsub], axis=2, keepdims=True), axis=0, keepdims=True)
            ds_ref[...] += jnp.where(row_h == hh, -tot.reshape(1, 1), 0.0)

        prev = pl.multiple_of(jnp.maximum(i * tq - WINDOW, 0), WINDOW)
        for g8, g_ref in ((dk8, dk_ref), (dv8, dv_ref)):
            g4 = g8[0:nsub] + g8[nsub:]
            own, before = g4[:, WINDOW:, :], g4[:, 0:WINDOW, :]
            shifted = jnp.concatenate([before[1:nsub], jnp.zeros((1, WINDOW, LANES), F32)], axis=0)
            g_ref[pl.ds(cur, tq), :] = (own + shifted).reshape(tq, LANES)
            g_ref[pl.ds(prev, WINDOW), :] += before[0]

    return pl.pallas_call(
        body, name="swa_bwd", grid=(nb, 4, nq),
        in_specs=[pl.BlockSpec((tq, LANES), lambda b, p, i: (b * nq + i, p)),
                  pl.BlockSpec((tq, LANES), lambda b, p, i: (b * nq + i, p)),
                  pl.BlockSpec((s, LANES), lambda b, p, i: (b, lax.shift_right_logical(p, 1))),
                  pl.BlockSpec((s, LANES), lambda b, p, i: (b, lax.shift_right_logical(p, 1))),
                  pl.BlockSpec((1, LANES), lambda b, p, i: (0, 0)),
                  pl.BlockSpec((None, tq, LANES), lambda b, p, i: (p, b * nq + i, 0)),
                  pl.BlockSpec((tq, LANES), lambda b, p, i: (b * nq + i, 0))],
        out_specs=[pl.BlockSpec((tq, LANES), lambda b, p, i: (b * nq + i, p)),
                   pl.BlockSpec((s, LANES), lambda b, p, i: (b, p)),
                   pl.BlockSpec((s, LANES), lambda b, p, i: (b, p)),
                   pl.BlockSpec((None, None, 8, LANES), lambda b, p, i: (b, p, 0, 0))],
        out_shape=[jax.ShapeDtypeStruct((t, 512), F32), jax.ShapeDtypeStruct((t, 512), F32),
                   jax.ShapeDtypeStruct((t, 512), F32), jax.ShapeDtypeStruct((nb, 4, 8, LANES), F32)],
        compiler_params=_params(("arbitrary", "arbitrary", "arbitrary")),
    )(qa, do_a, kae, vae, sink_row, lse, delta)


MESH = pl.DeviceIdType.MESH
ANY = pl.BlockSpec(memory_space=pl.ANY)
N_SEM = 7


def _gather_steps(pairs, send_sems, recv_sems, local_sems):
    x, y, c = lax.axis_index("x"), lax.axis_index("y"), lax.axis_index("c")
    me, sibling = (x, y, c), (x, y, 1 - c)
    chips = [(1 - x, y), (x, 1 - y), (1 - x, 1 - y)]
    mine, first, passed, landed, last = [], [], [], [], []
    for a, (x_ref, out_ref) in enumerate(pairs):
        def slot(px, py, pc, out_ref=out_ref):
            return out_ref.at[4 * px + 2 * py + pc]

        def copy(k, block, to, src=None, a=a, slot=slot):
            return pltpu.make_async_remote_copy(
                src_ref=slot(*block) if src is None else src, dst_ref=slot(*block),
                send_sem=send_sems.at[N_SEM * a + k], recv_sem=recv_sems.at[N_SEM * a + k], device_id=to, device_id_type=MESH)

        mine.append(pltpu.make_async_copy(x_ref, slot(*me), local_sems.at[a]))
        first += [copy(0, me, sibling, src=x_ref)] + [copy(1 + j, me, (*chip, c), src=x_ref) for j, chip in enumerate(chips)]
        passed += [copy(4 + j, (*chip, c), sibling) for j, chip in enumerate(chips)]
        landed += [copy(1 + j, (*chip, c), me) for j, chip in enumerate(chips)]
        last += [copy(0, sibling, me)] + [copy(4 + j, (*chip, 1 - c), me) for j, chip in enumerate(chips)]

    def start():
        for cp in mine + first:
            cp.start()

    def forward():
        for arrived, onward in zip(landed, passed):
            arrived.wait_recv()
            onward.start()

    def finish():
        for cp in last:
            cp.wait_recv()
        for cp in first + passed:
            cp.wait_send()
        for cp in mine:
            cp.wait()

    return start, forward, finish


def _exchange_steps(pairs, send_sems, recv_sems, local_sems):
    x, y, c = lax.axis_index("x"), lax.axis_index("y"), lax.axis_index("c")
    my_id = 4 * x + 2 * y + c
    local, remote = [], []
    for a, (src, dst) in enumerate(pairs):
        local.append(pltpu.make_async_copy(src.at[my_id], dst.at[my_id], local_sems.at[a]))
        for k in range(1, N_DEV):
            px = 1 - x if k & 4 else x
            py = 1 - y if k & 2 else y
            pc = 1 - c if k & 1 else c
            remote.append(pltpu.make_async_remote_copy(
                src_ref=src.at[4 * px + 2 * py + pc], dst_ref=dst.at[my_id],
                send_sem=send_sems.at[N_SEM * a + k - 1], recv_sem=recv_sems.at[N_SEM * a + k - 1],
                device_id=(px, py, pc), device_id_type=MESH))

    def start():
        for cp in local + remote:
            cp.start()

    def finish():
        for cp in remote:
            cp.wait_recv()
        for cp in remote:
            cp.wait_send()
        for cp in local:
            cp.wait()

    return start, finish


L_ONE = 64
L_CK = 65
L_CQ = 68
L_LSE = 71
L_DELTA = 74


def _head_block(pair, half):
    y = pair if half == 0 else pltpu.roll(pair, 64, axis=1)
    return jnp.where(_lane(pair.shape) < 64, y, 0.0)


def _put3(blk, lane0, col):
    lane = _lane(blk.shape)
    hi = col.astype(BF16).astype(F32)
    mid = (col - hi).astype(BF16).astype(F32)
    lo = (col - hi - mid).astype(BF16).astype(F32)
    return jnp.where(lane == lane0, hi, jnp.where(lane == lane0 + 1, mid, jnp.where(lane == lane0 + 2, lo, blk)))


def _spread3(col, shape, lane0s):
    lane = _lane(shape)
    hi = col.astype(BF16).astype(F32)
    mid = (col - hi).astype(BF16).astype(F32)
    lo = (col - hi - mid).astype(BF16).astype(F32)

    def at(k):
        return functools.reduce(jnp.logical_or, [lane == ln + k for ln in lane0s])

    return jnp.where(at(0), hi, jnp.where(at(1), mid, jnp.where(at(2), lo, 0.0)))


def _put_ones(blk, lanes):
    lane = _lane(blk.shape)
    hit = functools.reduce(jnp.logical_or, [lane == ln for ln in lanes])
    return jnp.where(hit, 1.0, blk)


def _to_pairs(ref):
    out = []
    for j in range(4):
        a, b = ref[:, 2 * LANES * j:2 * LANES * j + LANES], ref[:, 2 * LANES * j + LANES:2 * LANES * (j + 1)]
        out.append(jnp.where(_lane(a.shape) < 64, a, pltpu.roll(b, 64, axis=1)))
    return jnp.concatenate(out, axis=1)


def _fox_fwd(q_aug, k_aug, v_aug, nb, s, bt, shards=()):
    t = q_aug.shape[0]
    nq = s // bt
    n_in, n_sh = 3, len(shards)

    def body(*refs):
        q_ref, k_ref, v_ref = refs[:n_in]
        o_ref, ql_ref = refs[n_in + n_sh:n_in + n_sh + 2]
        if shards:
            srcs, dsts = refs[n_in:n_in + n_sh], refs[n_in + n_sh + 2:n_in + 2 * n_sh + 2]
            start, forward, finish = _gather_steps(list(zip(srcs, dsts)), *refs[n_in + 2 * n_sh + 2:])
            step = (pl.program_id(0) * 4 + pl.program_id(1)) * nq + pl.program_id(2)
            pl.when(step == 0)(start)
            pl.when(step == nb * 3 * nq)(forward)
        i = pl.program_id(2)
        row = lax.broadcasted_iota(jnp.int32, (bt, bt), 0)
        col = lax.broadcasted_iota(jnp.int32, (bt, bt), 1)
        sls = [slice(LANES * hh, LANES * (hh + 1)) for hh in range(2)]
        qhs = [q_ref[:, sl] for sl in sls]

        def blk(kb_i, carry, diag):
            start = pl.multiple_of(kb_i * bt, bt)
            new = []
            for (m, acc), qh, sl in zip(carry, qhs, sls):
                sc = _nt(qh, k_ref[pl.ds(start, bt), sl])
                if diag:
                    sc = jnp.where(row >= col, sc, NEG_INF)
                m_new = jnp.maximum(m, jnp.max(sc, axis=1, keepdims=True))
                pr = jnp.exp2(sc - m_new).astype(BF16)
                acc = jnp.exp2(m - m_new) * acc + jnp.dot(pr, v_ref[pl.ds(start, bt), sl], preferred_element_type=F32)
                new.append((m_new, acc))
            return tuple(new)

        init = tuple((jnp.full((bt, 1), NEG_INF, F32), jnp.zeros((bt, LANES), F32)) for _ in range(2))
        carry = lax.fori_loop(0, i, lambda kb_i, c: blk(kb_i, c, False), init)
        outs = []
        for (m, acc), qh, sl in zip(blk(i, carry, True), qhs, sls):
            l = acc[:, L_ONE:L_ONE + 1]
            outs.append(acc / l)
            ql_ref[:, sl] = _put3(qh.astype(F32), L_LSE, -(m + jnp.log(l) * LOG2E)).astype(BF16)
        o_ref[...] = jnp.where(_lane((1, LANES)) < 64, outs[0], pltpu.roll(outs[1], 64, axis=1)).astype(BF16)
        if shards:
            pl.when(step == nb * 4 * nq - 1)(finish)

    in_specs = [pl.BlockSpec((bt, 2 * LANES), lambda b, j, i: (b * nq + i, j)),
                pl.BlockSpec((s, 2 * LANES), lambda b, j, i: (b, j)),
                pl.BlockSpec((s, 2 * LANES), lambda b, j, i: (b, j))]
    out_specs = [pl.BlockSpec((bt, LANES), lambda b, j, i: (b * nq + i, j)),
                 pl.BlockSpec((bt, 2 * LANES), lambda b, j, i: (b * nq + i, j))]
    out_shape = [jax.ShapeDtypeStruct((t, 512), BF16), jax.ShapeDtypeStruct((t, 8 * LANES), BF16)]
    args, scratch = [q_aug, k_aug, v_aug, *shards], []
    if shards:
        in_specs += [ANY] * n_sh
        out_specs += [ANY] * n_sh
        out_shape += [jax.ShapeDtypeStruct((N_DEV,) + sh.shape, sh.dtype) for sh in shards]
        scratch = [pltpu.SemaphoreType.DMA((N_SEM * n_sh,)), pltpu.SemaphoreType.DMA((N_SEM * n_sh,)),
                   pltpu.SemaphoreType.DMA((n_sh,))]
    return pl.pallas_call(
        body, name="fox_fwd", grid=(nb, 4, nq), in_specs=in_specs, out_specs=out_specs, out_shape=out_shape,
        scratch_shapes=scratch, compiler_params=_params(("arbitrary", "arbitrary", "arbitrary")),
    )(*args)


def _fox_bwd(ql_aug, k_aug, v_aug, do_aug, nb, s, bt, exch=()):
    t = ql_aug.shape[0]
    nk = s // bt
    n_in, n_out, n_ex = 4, 3, len(exch)

    def body(*refs):
        q_ref, do_ref, k_ref, v_ref = refs[:n_in]
        dq_ref, dk_ref, dv_ref = refs[n_in + n_ex:n_in + n_ex + n_out]
        if exch:
            srcs = refs[n_in:n_in + n_ex]
            dsts = refs[n_in + n_ex + n_out:n_in + 2 * n_ex + n_out]
            start, finish = _exchange_steps(list(zip(srcs, dsts)), *refs[n_in + 2 * n_ex + n_out:])
            step = (pl.program_id(0) * 4 + pl.program_id(1)) * nk + pl.program_id(2)
            pl.when(step == 0)(start)
        kb_i = pl.program_id(2)

        @pl.when(kb_i == 0)
        def _():
            dq_ref[...] = jnp.zeros_like(dq_ref)

        row = lax.broadcasted_iota(jnp.int32, (bt, bt), 0)
        col = lax.broadcasted_iota(jnp.int32, (bt, bt), 1)
        sls = [slice(LANES * hh, LANES * (hh + 1)) for hh in range(2)]
        khs, vhs = [k_ref[:, sl] for sl in sls], [v_ref[:, sl] for sl in sls]

        def blk(qi, carry, diag):
            start = pl.multiple_of(qi * bt, bt)
            new = []
            for (dk_a, dv_a), kh, vh, sl in zip(carry, khs, vhs, sls):
                qblk, doblk = q_ref[pl.ds(start, bt), sl], do_ref[pl.ds(start, bt), sl]
                st = _nt(kh, qblk)
                if diag:
                    pt = jnp.where(col >= row, jnp.exp2(jnp.where(col >= row, st, 0.0)), 0.0)
                else:
                    pt = jnp.exp2(st)
                dst = pt * _nt(vh, doblk)
                ptb, dstb = pt.astype(BF16), dst.astype(BF16)
                dv_a = dv_a + jnp.dot(ptb, doblk, preferred_element_type=F32)
                dk_a = dk_a + jnp.dot(dstb, qblk, preferred_element_type=F32)
                dq_ref[pl.ds(start, bt), sl] += _tn(dstb, kh)
                new.append((dk_a, dv_a))
            return tuple(new)

        zero = jnp.zeros((bt, LANES), F32)
        carry = blk(kb_i, ((zero, zero), (zero, zero)), True)
        carry = lax.fori_loop(kb_i + 1, nk, lambda qi, c: blk(qi, c, False), carry)
        for (dk_acc, dv_acc), sl in zip(carry, sls):
            dk_ref[:, sl] = dk_acc
            dv_ref[:, sl] = dv_acc
        if exch:
            pl.when(step == nb * 4 * nk - 1)(finish)

    scratch = []
    if exch:
        scratch = [pltpu.SemaphoreType.DMA((N_SEM * n_ex,)), pltpu.SemaphoreType.DMA((N_SEM * n_ex,)),
                   pltpu.SemaphoreType.DMA((n_ex,))]
    whole = pl.BlockSpec((s, 2 * LANES), lambda b, j, kb_i: (b, j))
    tile = pl.BlockSpec((bt, 2 * LANES), lambda b, j, kb_i: (b * nk + kb_i, j))
    shp = jax.ShapeDtypeStruct((t, 8 * LANES), F32)
    return pl.pallas_call(
        body, name="fox_bwd", grid=(nb, 4, nk),
        in_specs=[whole, whole, tile, tile] + [ANY] * n_ex,
        out_specs=[whole, tile, tile] + [ANY] * n_ex,
        out_shape=[shp, shp, shp] + [jax.ShapeDtypeStruct(e.shape, e.dtype) for e in exch],
        scratch_shapes=scratch, compiler_params=_params(("arbitrary", "arbitrary", "arbitrary")),
    )(ql_aug, do_aug, k_aug, v_aug, *exch)


FF_BLK = D_FF // N_DEV


def _mlp_fwd(x2, ma, mb, tgt, w_out, g2, w_up, w_down, tm):
    t = x2.shape[0]

    def body(x_ref, ma_ref, mb_ref, tg_ref, wo_ref, g2_ref, wu_ref, wd_ref,
             h_ref, hn_ref, hid_ref, dy_ref, dyb_ref, loss_ref):
        @pl.when(pl.program_id(0) == 0)
        def _():
            loss_ref[...] = jnp.zeros_like(loss_ref)

        h = (x_ref[...] + jnp.dot(ma_ref[...], wo_ref[0:512, :], preferred_element_type=F32)
             + jnp.dot(mb_ref[...], wo_ref[512:1024, :], preferred_element_type=F32))
        h_ref[...] = h
        r = lax.rsqrt(jnp.mean(h * h, axis=-1, keepdims=True) + EPS)
        hn = (h * r * g2_ref[...]).astype(BF16)
        hn_ref[...] = hn
        for d in range(N_DEV):
            u = jnp.maximum(jnp.dot(hn, wu_ref[d], preferred_element_type=F32), 0.0)
            hid_ref[:, FF_BLK * d:FF_BLK * (d + 1)] = (u * u).astype(BF16)
        y = h + jnp.dot(hid_ref[...], wd_ref[...], preferred_element_type=F32)
        err = y - tg_ref[...]
        dy = err * (1.0 / D_MODEL)
        dy_ref[...] = dy
        dyb_ref[...] = dy.astype(BF16)
        part =0.5 * jnp.sum(jnp.sum(err * err, axis=1, keepdims=True) * (1.0 / D_MODEL), axis=0, keepdims=True)
        loss_ref[...] += part

    def tile(w):
        return pl.BlockSpec((tm, w), lambda i: (i, 0))

    return pl.pallas_call(
        body, name="mlp_fwd", grid=(t // tm,),
        in_specs=[tile(D_MODEL), tile(512), tile(512), tile(D_MODEL), _const_spec((D_MODEL, D_MODEL)),
                  _const_spec((1, D_MODEL)), _const_spec((N_DEV, D_MODEL, FF_BLK)), _const_spec((D_FF, D_MODEL))],
        out_specs=[tile(D_MODEL), tile(D_MODEL), tile(D_FF), tile(D_MODEL), tile(D_MODEL),
                   pl.BlockSpec((8, LANES), lambda i: (0, 0))],
        out_shape=[jax.ShapeDtypeStruct((t, D_MODEL), F32), jax.ShapeDtypeStruct((t, D_MODEL), BF16),
                   jax.ShapeDtypeStruct((t, D_FF), BF16), jax.ShapeDtypeStruct((t, D_MODEL), F32),
                   jax.ShapeDtypeStruct((t, D_MODEL), BF16), jax.ShapeDtypeStruct((8, LANES), F32)],
        compiler_params=_params(("arbitrary",)),
    )(x2, ma, mb, tgt, w_out, g2, w_up, w_down)


def _mlp_bwd(dy, hid, h, ma, mb, w_down, w_up_t, w_out, g2, tm):
    t = dy.shape[0]

    def body(dy_ref, hid_ref, h_ref, ma_ref, mb_ref, wd_ref, wut_ref, wo_ref, g2_ref,
             du_ref, dh_ref, dhb_ref, dma_ref, dob_ref, dla_ref, gg_ref):
        @pl.when(pl.program_id(0) == 0)
        def _():
            gg_ref[...] = jnp.zeros_like(gg_ref)

        dy = dy_ref[...]
        d_hid = _nt(dy.astype(BF16), wd_ref[...])
        du = (d_hid * (2.0 * jnp.sqrt(hid_ref[...].astype(F32)))).astype(BF16)
        du_ref[...] = du
        d_hn = jnp.dot(du, wut_ref[...], preferred_element_type=F32)
        h = h_ref[...]
        r = lax.rsqrt(jnp.mean(h * h, axis=-1, keepdims=True) + EPS)
        hat = h * r
        gd = d_hn * g2_ref[...]
        dh = dy + r * (gd - hat * jnp.mean(gd * hat, axis=-1, keepdims=True))
        gg_ref[...] += jnp.sum(d_hn * hat, axis=0, keepdims=True)
        dh_ref[...] = dh
        dhb = dh.astype(BF16)
        dhb_ref[...] = dhb
        dm = _nt(dhb, wo_ref[...]).astype(BF16)
        dma, dmb = dm[:, 0:512], dm[:, 512:1024]
        dma_ref[...] = dma
        sel = (lax.shift_right_logical(lax.broadcasted_iota(jnp.int32, (512, LANES), 0), 6)
               == lax.broadcasted_iota(jnp.int32, (512, LANES), 1)).astype(BF16)
        dla_ref[...] = _split_dot(dma.astype(F32) * ma_ref[...].astype(F32), sel)
        dmb32 = dmb.astype(F32)
        dlb = _split_dot(dmb32 * mb_ref[...].astype(F32), sel)
        for hd in range(8):
            blk = _head_block(dmb32[:, LANES * (hd // 2):LANES * (hd // 2 + 1)], hd % 2)
            dob_ref[:, LANES * hd:LANES * (hd + 1)] = _put3(blk, L_DELTA, -dlb[:, hd:hd + 1]).astype(BF16)

    def tile(w):
        return pl.BlockSpec((tm, w), lambda i: (i, 0))

    return pl.pallas_call(
        body, name="mlp_bwd", grid=(t // tm,),
        in_specs=[tile(D_MODEL), tile(D_FF), tile(D_MODEL), tile(512), tile(512), _const_spec((D_FF, D_MODEL)),
                  _const_spec((D_FF, D_MODEL)), _const_spec((D_MODEL, D_MODEL)), _const_spec((1, D_MODEL))],
        out_specs=[tile(D_FF), tile(D_MODEL), tile(D_MODEL), tile(512), tile(8 * LANES), tile(LANES),
                   pl.BlockSpec((1, D_MODEL), lambda i: (0, 0))],
        out_shape=[jax.ShapeDtypeStruct((t, D_FF), BF16), jax.ShapeDtypeStruct((t, D_MODEL), F32),
                   jax.ShapeDtypeStruct((t, D_MODEL), BF16), jax.ShapeDtypeStruct((t, 512), BF16),
                   jax.ShapeDtypeStruct((t, 8 * LANES), BF16), jax.ShapeDtypeStruct((t, LANES), F32),
                   jax.ShapeDtypeStruct((1, D_MODEL), F32)],
        compiler_params=_params(("arbitrary",), VMEM_LIMIT_WIDE),
    )(dy, hid, h, ma, mb, w_down, w_up_t, w_out, g2)


def _wgrad(a, b, name, bm, bn, tk, out_dtype=F32, col_blocks=False):
    t, m = a.shape
    n = b.shape[1]
    bm, bn = min(bm, m), min(bn, n)
    nk = t // tk

    def body(a_ref, b_ref, o_ref, acc):
        @pl.when(pl.program_id(2) == 0)
        def _():
            acc[...] = jnp.zeros_like(acc)

        acc[...] += _tn(a_ref[...], b_ref[...])

        @pl.when(pl.program_id(2) == nk - 1)
        def _():
            o_ref[...] = acc[...].astype(out_dtype)

    if col_blocks:
        out_spec = pl.BlockSpec((None, bm, bn), lambda i, j, k: (j, i, 0))
        out_shape = jax.ShapeDtypeStruct((n // bn, m, bn), out_dtype)
    else:
        out_spec = pl.BlockSpec((bm, bn), lambda i, j, k: (i, j))
        out_shape = jax.ShapeDtypeStruct((m, n), out_dtype)
    return pl.pallas_call(
        body, name=name, grid=(m // bm, n // bn, nk),
        in_specs=[pl.BlockSpec((tk, bm), lambda i, j, k: (k, i)), pl.BlockSpec((tk, bn), lambda i, j, k: (k, j))],
        out_specs=out_spec, out_shape=out_shape, scratch_shapes=[pltpu.VMEM((bm, bn), F32)],
        compiler_params=_params(("arbitrary", "arbitrary", "arbitrary")),
    )(a, b)


def _proj_bwd(raw, dqa, dkae, dvae, dqb, dkb, dvb, fl, bf_row, x2, dh, w_main_t, w_f_t, g1, gqa, gka, gqb, gkb, nb, s, tm):
    t = x2.shape[0]
    nt = s // tm

    def body(raw_ref, dqa_ref, dkae_ref, dvae_ref, dqb_ref, dkb_ref, dvb_ref, fl_ref, b_ref, x_ref, dh_ref,
             wmt_ref, wft_ref, g1_ref, gqa_ref, gka_ref, gqb_ref, gkb_ref,
             dx_ref, dp_ref, dfb_ref, ggqa_ref, ggka_ref, ggqb_ref, ggkb_ref, gg1_ref, gb_ref, carry, dlf_ref):
        @pl.when((pl.program_id(0) == 0) & (pl.program_id(1) == 0))
        def _():
            for r in (ggqa_ref, ggka_ref, ggqb_ref, ggkb_ref, gg1_ref, gb_ref):
                r[...] = jnp.zeros_like(r)

        @pl.when(pl.program_id(1) == 0)
        def _():
            carry[...] = jnp.zeros_like(carry)

        lane = _lane((tm, LANES))
        dc = jnp.zeros((tm, LANES), F32)
        for hd in range(8):
            col = (dqb_ref[:, LANES * hd + L_CQ:LANES * hd + L_CQ + 1] - dkb_ref[:, LANES * hd + L_CK:LANES * hd + L_CK + 1])
            dc = jnp.where(lane == hd, col, dc)
        dlf_ref[...] = jnp.dot(_tri(tm, True), dc, precision=lax.Precision.HIGHEST, preferred_element_type=F32) + carry[...]
        carry[...] = dlf_ref[pl.ds(0, 1), :]
        dfl = dlf_ref[...] * (1.0 / (1.0 + jnp.exp(fl_ref[...] + b_ref[...])))
        gb_ref[...] += jnp.sum(dfl, axis=0, keepdims=True)

        raw = raw_ref[...]
        d_qa, p_qa = _head_norm_bwd(raw[:, 0:512], gqa_ref[...], dqa_ref[...])
        d_ka, p_ka = _head_norm_bwd(raw[:, 512:640], gka_ref[...], _fold_kv(dkae_ref[...]))
        d_va = _fold_kv(dvae_ref[...])
        d_qb, p_qb = _head_norm_bwd(raw[:, 768:1280], gqb_ref[...], _to_pairs(dqb_ref) * SCALE)
        d_kb, p_kb = _head_norm_bwd(raw[:, 1280:1792], gkb_ref[...], _to_pairs(dkb_ref) * (1.0 / LOG2E))
        ggqa_ref[...] += jnp.sum(p_qa, axis=0, keepdims=True)
        ggka_ref[...] += jnp.sum(p_ka, axis=0, keepdims=True)
        ggqb_ref[...] += jnp.sum(p_qb, axis=0, keepdims=True)
        ggkb_ref[...] += jnp.sum(p_kb, axis=0, keepdims=True)
        dproj = jnp.concatenate([d_qa, d_ka, d_va, d_qb, d_kb, _to_pairs(dvb_ref)], axis=1).astype(BF16)
        dp_ref[...] = dproj
        dfb = dfl.astype(BF16)
        dfb_ref[...] = dfb
        d_xn = (jnp.dot(dproj, wmt_ref[...], preferred_element_type=F32)
                + jnp.dot(dfb, wft_ref[...], preferred_element_type=F32))
        x = x_ref[...]
        r = lax.rsqrt(jnp.mean(x * x, axis=-1, keepdims=True) + EPS)
        hat = x * r
        gd = d_xn * g1_ref[...]
        dx_ref[...] = dh_ref[...] + r * (gd - hat * jnp.mean(gd * hat, axis=-1, keepdims=True))
        gg1_ref[...] += jnp.sum(d_xn * hat, axis=0, keepdims=True)

    def tile(w):
        return pl.BlockSpec((tm, w), lambda b, i: (b * nt + (nt - 1 - i), 0))

    def acc(w):
        return pl.BlockSpec((1, w), lambda b, i: (0, 0))

    return pl.pallas_call(
        body, name="proj_bwd", grid=(nb, nt),
        in_specs=[tile(MAIN_W), tile(512), tile(512), tile(512), tile(8 * LANES), tile(8 * LANES), tile(8 * LANES), tile(LANES),
                  _const_spec((1, LANES)), tile(D_MODEL), tile(D_MODEL), _const_spec((MAIN_W, D_MODEL)),
                  _const_spec((LANES, D_MODEL)), _const_spec((1, D_MODEL)), _const_spec((1, 512)), _const_spec((1, 128)),
                  _const_spec((1, 512)), _const_spec((1, 512))],
        out_specs=[tile(D_MODEL), tile(MAIN_W), tile(LANES), acc(512), acc(128), acc(512), acc(512), acc(D_MODEL), acc(LANES)],
        out_shape=[jax.ShapeDtypeStruct((t, D_MODEL), F32), jax.ShapeDtypeStruct((t, MAIN_W), BF16),
                   jax.ShapeDtypeStruct((t, LANES), BF16), jax.ShapeDtypeStruct((1, 512), F32),
                   jax.ShapeDtypeStruct((1, 128), F32), jax.ShapeDtypeStruct((1, 512), F32),
                   jax.ShapeDtypeStruct((1, 512), F32), jax.ShapeDtypeStruct((1, D_MODEL), F32),
                   jax.ShapeDtypeStruct((1, LANES), F32)],
        scratch_shapes=[pltpu.VMEM((1, LANES), F32), pltpu.VMEM((tm, LANES), F32)],
        compiler_params=_params(("arbitrary", "arbitrary"), VMEM_LIMIT_WIDE),
    )(raw, dqa, dkae, dvae, dqb, dkb, dvb, fl, bf_row, x2, dh, w_main_t, w_f_t, g1, gqa, gka, gqb, gkb)


IN_PAD = 304


def _local_step(x, tgt, w_in_t, rest, g1, b_forget, qna, kna, sinks, qnb, knb, g2,
                tm=512, bt=512, btf=1024, tq=1024, wk=4096, distributed=False):
    nb, s, _ = x.shape
    t = nb * s
    x2, tgt2 = x.reshape(t, D_MODEL), tgt.reshape(t, D_MODEL)
    g1r, g2r = g1.reshape(1, D_MODEL), g2.reshape(1, D_MODEL)
    gqa, gka = jnp.tile(qna, 8).reshape(1, 512), jnp.tile(kna, 2).reshape(1, 128)
    gqb, gkb = jnp.tile(qnb, 8).reshape(1, 512), jnp.tile(knb, 8).reshape(1, 512)
    bf_row = jnp.pad(b_forget, (0, LANES - 8)).reshape(1, LANES)
    sink_row = jnp.pad(sinks, (0, LANES - 8)).reshape(1, LANES)
    w_main_t = w_in_t[0:MAIN_W]
    w_f_t = jnp.pad(w_in_t[MAIN_W:IN_W], ((0, LANES - 8), (0, 0)))

    xn, raw, fl, qa, kae, vae, q_aug, k_aug, v_aug = _norm_proj(x2, g1r, w_main_t, w_f_t, gqa, gka, gqb, gkb, bf_row, s, tm)
    ma, lse_a = _swa_fwd(qa, kae, vae, sink_row, nb, s, tq)
    if distributed:
        mb, ql_aug, w_out, w_up, w_down, w_up_t = _fox_fwd(q_aug, k_aug, v_aug, nb, s, btf, shards=rest)
    else:
        mb, ql_aug = _fox_fwd(q_aug, k_aug, v_aug, nb, s, btf)
        w_out, w_up, w_down, w_up_t = rest
    w_out, w_down = w_out.reshape(D_MODEL, D_MODEL), w_down.reshape(D_FF, D_MODEL)
    h, hn, hid, dy, dyb, loss_acc = _mlp_fwd(x2, ma, mb, tgt2, w_out, g2r, w_up, w_down, tm)

    du, dh, dhb, dma, do_aug, dla, gg2 = _mlp_bwd(dy, hid, h, ma, mb, w_down, w_up_t.reshape(D_FF, D_MODEL), w_out, g2r, tm)
    g_down = _wgrad(hid, dyb, "wgrad_down", 512, 1024, wk, BF16).reshape(N_DEV, 512, D_MODEL)
    g_up = _wgrad(hn, du, "wgrad_up", 1024, 512, wk, BF16, col_blocks=True)
    g_out = jnp.concatenate([_wgrad(ma, dhb, "wgrad_out_a", 512, 1024, wk, BF16),
                             _wgrad(mb, dhb, "wgrad_out_b", 512, 1024, wk, BF16)], axis=0).reshape(N_DEV, 128, D_MODEL)

    dqa, dkae, dvae, dsink = _swa_bwd(qa, kae, vae, dma, sink_row, lse_a, dla, nb, s, tq)
    fox = _fox_bwd(ql_aug, k_aug, v_aug, do_aug, nb, s, bt, exch=(g_out, g_up, g_down) if distributed else ())
    dqb, dkb, dvb = fox[:3]
    if distributed:
        g_out, g_up, g_down = fox[3:]
    grad_x, dproj, dfb, ggqa, ggka, ggqb, ggkb, gg1, gbf = _proj_bwd(
        raw, dqa, dkae, dvae, dqb, dkb, dvb, fl, bf_row, x2, dh, w_main_t, w_f_t, g1r, gqa, gka, gqb, gkb, nb, s, tm)
    g_in_t = jnp.concatenate([_wgrad(dproj, xn, "wgrad_in", 768, 1024, wk), _wgrad(dfb, xn, "wgrad_gate", 128, 1024, wk)[0:8]],
                             axis=0)

    small = (gg1.reshape(D_MODEL), gbf[0, 0:8], ggqa.reshape(8, 64).sum(0), ggka.reshape(2, 64).sum(0),
             dsink.sum(0)[:, 0:2, 0].reshape(8), ggqb.reshape(8, 64).sum(0), ggkb.reshape(8, 64).sum(0),
             gg2.reshape(D_MODEL))
    return loss_acc[0, 0], grad_x.reshape(nb, s, D_MODEL), g_in_t, g_out, g_up, g_down, small


def _all_gather(shard):
    def body(x_ref, out_ref, send_sems, recv_sems, local_sem):
        start, forward, finish = _gather_steps([(x_ref, out_ref)], send_sems, recv_sems, local_sem)
        start()
        forward()
        finish()

    return pl.pallas_call(
        body, name="gather_w_in", out_shape=jax.ShapeDtypeStruct((N_DEV,) + shard.shape, shard.dtype),
        in_specs=[ANY], out_specs=ANY,
        scratch_shapes=[pltpu.SemaphoreType.DMA((N_SEM,)), pltpu.SemaphoreType.DMA((N_SEM,)), pltpu.SemaphoreType.DMA((1,))],
    )(shard)


def _exchange(*arrays):
    n_ex = len(arrays)

    def body(*refs):
        start, finish = _exchange_steps(list(zip(refs[:n_ex], refs[n_ex:2 * n_ex])), *refs[2 * n_ex:])
        start()
        finish()

    return pl.pallas_call(
        body, name="exchange_tail", out_shape=[jax.ShapeDtypeStruct(a.shape, a.dtype) for a in arrays],
        in_specs=[ANY] * n_ex, out_specs=[ANY] * n_ex,
        scratch_shapes=[pltpu.SemaphoreType.DMA((N_SEM * n_ex,)), pltpu.SemaphoreType.DMA((N_SEM * n_ex,)),
                        pltpu.SemaphoreType.DMA((n_ex,))],
    )(*arrays)


def _sum_adamw(recv, w, m, v, tr, name):
    _, r, n = recv.shape

    def body(r_ref, w_ref, m_ref, v_ref, g_ref, d_ref, nm_ref, nv_ref):
        g = r_ref[0].astype(F32)
        for s in range(1, N_DEV):
            g = g + r_ref[s].astype(F32)
        g_ref[...] = g
        nm = ADAM_B1 * m_ref[...] + (1.0 - ADAM_B1) * g
        nv = ADAM_B2 * v_ref[...] + (1.0 - ADAM_B2) * (g * g)
        m_hat = nm / (1.0 - ADAM_B1 ** ADAM_STEP)
        v_hat = nv / (1.0 - ADAM_B2 ** ADAM_STEP)
        d_ref[...] = -ADAM_LR * (m_hat / (jnp.sqrt(v_hat) + ADAM_EPS) + ADAM_WD * w_ref[...])
        nm_ref[...] = nm
        nv_ref[...] = nv

    tile = pl.BlockSpec((tr, n), lambda i: (i, 0))
    shp = jax.ShapeDtypeStruct((r, n), F32)
    return pl.pallas_call(
        body, name=name, grid=(r // tr,),
        in_specs=[pl.BlockSpec((N_DEV, tr, n), lambda i: (0, i, 0)), tile, tile, tile],
        out_specs=[tile, tile, tile, tile], out_shape=[shp, shp, shp, shp],
        compiler_params=_params(("arbitrary",)),
    )(recv, w, m, v)


def _small_rows(g1, bf, qna, kna, sk, qnb, knb, g2):
    row2 = jnp.concatenate([bf, qna, kna, sk, qnb, knb])
    return jnp.zeros((8, D_MODEL), F32).at[0].set(g1).at[1].set(g2).at[2, 0:row2.shape[0]].set(row2)


def _in_rows(w_in_s):
    return jnp.pad(w_in_s.T, ((0, IN_PAD - IN_SHARD), (0, 0)))


def kernel(x, attn_norm_g, w_in, b_forget, q_norm_a, k_norm_a, sink_logits, q_norm_b, k_norm_b, w_out, mlp_norm_g, w_up, w_down, loss_target, m_attn_norm_g, m_w_in, m_b_forget, m_q_norm_a, m_k_norm_a, m_sink_logits, m_q_norm_b, m_k_norm_b, m_w_out, m_mlp_norm_g, m_w_up, m_w_down, v_attn_norm_g, v_w_in, v_b_forget, v_q_norm_a, v_k_norm_a, v_sink_logits, v_q_norm_b, v_k_norm_b, v_w_out, v_mlp_norm_g, v_w_up, v_w_down):
    w_in_r = _in_rows(w_in)
    w_in_t = _all_gather(w_in_r.astype(BF16))[:, 0:IN_SHARD].reshape(IN_W, D_MODEL)
    w_up_b = w_up.astype(BF16)
    rest = (w_out.astype(BF16), w_up_b, w_down.astype(BF16), w_up_b.T)

    loss_part, grad_x, g_in_t, r_out, r_up, r_down, small = _local_step(
        x, loss_target, w_in_t, rest, attn_norm_g, b_forget, q_norm_a, k_norm_a, sink_logits, q_norm_b, k_norm_b, mlp_norm_g,
        distributed=True)

    g_in_blocks = jnp.pad(g_in_t.reshape(N_DEV, IN_SHARD, D_MODEL), ((0, 0), (0, IN_PAD - IN_SHARD), (0, 0))).astype(BF16)
    small_blocks = jnp.broadcast_to(_small_rows(*small).at[3, 0].set(loss_part), (N_DEV, 8, D_MODEL))
    r_in, r_small = _exchange(g_in_blocks, small_blocks)

    small_w = _small_rows(attn_norm_g, b_forget, q_norm_a, k_norm_a, sink_logits, q_norm_b, k_norm_b, mlp_norm_g)
    small_m = _small_rows(m_attn_norm_g, m_b_forget, m_q_norm_a, m_k_norm_a, m_sink_logits, m_q_norm_b, m_k_norm_b, m_mlp_norm_g)
    small_v = _small_rows(v_attn_norm_g, v_b_forget, v_q_norm_a, v_k_norm_a, v_sink_logits, v_q_norm_b, v_k_norm_b, v_mlp_norm_g)
    o_in = [a[0:IN_SHARD].T for a in _sum_adamw(r_in, w_in_r, _in_rows(m_w_in), _in_rows(v_w_in), IN_PAD, "adamw_in")]
    o_out = _sum_adamw(r_out, w_out, m_w_out, v_w_out, 128, "adamw_out")
    o_up = _sum_adamw(r_up, w_up, m_w_up, v_w_up, 256, "adamw_up")
    o_down = _sum_adamw(r_down, w_down, m_w_down, v_w_down, 128, "adamw_down")
    o_small = _sum_adamw(r_small, small_w, small_m, small_v, 8, "adamw_small")

    def leaves(i):
        row2 = o_small[i][2]
        return (o_small[i][0], o_in[i], row2[0:8], row2[8:72], row2[72:136], row2[136:144], row2[144:208], row2[208:272],
                o_out[i], o_small[i][1], o_up[i], o_down[i])

    return (o_small[0][3, 0], grad_x, *leaves(0), *leaves(1), *leaves(2), *leaves(3))
```

```python
import functools

import jax
import jax.numpy as jnp
from jax import lax
from jax.experimental import pallas as pl
from jax.experimental.pallas import tpu as pltpu

F32 = jnp.float32
BF16 = jnp.bfloat16

D_MODEL = 1024
HEAD_DIM = 64
N_DEV = 8
D_FF = 4096
MAIN_W = 2304
IN_W = 2312
IN_SHARD = 289
WINDOW = 128
EPS = 1e-6
SCALE = 0.125
LOG2E = 1.4426950408889634
LANES = 128
NEG_INF = float("-inf")

ADAM_LR = 0.001
ADAM_B1 = 0.9
ADAM_B2 = 0.999
ADAM_EPS = 1e-08
ADAM_WD = 0.01
ADAM_STEP = 10

VMEM_LIMIT = 56 * 1024 * 1024
VMEM_LIMIT_WIDE = 62 * 1024 * 1024


def _params(sem, vmem=VMEM_LIMIT):
    return pltpu.CompilerParams(dimension_semantics=sem, vmem_limit_bytes=vmem)


def _const_spec(shape):
    nd = len(shape)
    return pl.BlockSpec(shape, lambda *_: (0,) * nd, pipeline_mode=pl.Buffered(1))


def _lane(shape):
    return lax.broadcasted_iota(jnp.int32, shape, len(shape) - 1)


def _split_dot(v, mat):
    hi = v.astype(BF16)
    lo = (v - hi.astype(F32)).astype(BF16)
    return (jnp.dot(hi, mat, preferred_element_type=F32) + jnp.dot(lo, mat, preferred_element_type=F32))


def _head_ones(n):
    r = lax.shift_right_logical(lax.broadcasted_iota(jnp.int32, (n, n), 0), 6)
    c = lax.shift_right_logical(lax.broadcasted_iota(jnp.int32, (n, n), 1), 6)
    return (r == c).astype(BF16)


def _head_sum(v):
    w = v.shape[1]
    vb = v.astype(BF16)
    if w <= 256:
        return jnp.dot(vb, _head_ones(w), preferred_element_type=F32)
    ones = _head_ones(256)
    return jnp.concatenate([jnp.dot(vb[:, s:s + 256], ones, preferred_element_type=F32) for s in range(0, w, 256)], axis=1)


def _head_norm(seg, gain):
    rs = lax.rsqrt(_head_sum(seg * seg) * (1.0 / HEAD_DIM) + EPS)
    return seg * rs * gain


def _head_norm_bwd(seg, gain, d_out):
    rs = lax.rsqrt(_head_sum(seg * seg) * (1.0 / HEAD_DIM) + EPS)
    hat = seg * rs
    gd = d_out * gain
    d_seg = rs * (gd - hat * (_head_sum(gd * hat) * (1.0 / HEAD_DIM)))
    return d_seg, d_out * hat


def _expand_kv(v):
    r = pltpu.roll(v, 64, axis=1)
    lo = _lane(v.shape) < 64
    return jnp.concatenate([jnp.where(lo, v, r), jnp.where(lo, r, v)], axis=1)


def _fold_kv(e4):
    t0 = e4[:, 0:128] + e4[:, 128:256]
    t1 = e4[:, 256:384] + e4[:, 384:512]
    t0 = t0 + pltpu.roll(t0, 64, axis=1)
    t1 = t1 + pltpu.roll(t1, 64, axis=1)
    return jnp.where(_lane(t0.shape) < 64, t0, t1)


def _pick_lane(blk, idx):
    return jnp.sum(jnp.where(_lane(blk.shape) == idx, blk, 0.0), axis=1, keepdims=True)


def _nt(a, b):
    return lax.dot_general(a, b, (((1,), (1,)), ((), ())), preferred_element_type=F32)


def _tn(a, b):
    return lax.dot_general(a, b, (((0,), (0,)), ((), ())), preferred_element_type=F32)


def _norm_proj(x2, g1, w_main_t, w_f_t, gqa, gka, gqb, gkb, bf_row, s, tm):
    t = x2.shape[0]
    nt = s // tm

    def body(x_ref, g1_ref, wm_ref, wf_ref, gqa_ref, gka_ref, gqb_ref, gkb_ref, b_ref,
             xn_ref, raw_ref, fl_ref, qa_ref, kae_ref, vae_ref, qo_ref, ko_ref, vo_ref, carry, c_ref):
        @pl.when(lax.rem(pl.program_id(0), nt) == 0)
        def _():
            carry[...] = jnp.zeros_like(carry)

        x = x_ref[...]
        r = lax.rsqrt(jnp.mean(x * x, axis=-1, keepdims=True) + EPS)
        xn = (x * r * g1_ref[...]).astype(BF16)
        xn_ref[...] = xn
        proj = _nt(xn, wm_ref[...])
        raw_ref[...] = proj
        fl = _nt(xn, wf_ref[...])
        fl_ref[...] = fl
        qa_ref[...] = _head_norm(proj[:, 0:512], gqa_ref[...]).astype(BF16)
        kae_ref[...] = _expand_kv(_head_norm(proj[:, 512:640], gka_ref[...])).astype(BF16)
        vae_ref[...] = _expand_kv(proj[:, 640:768]).astype(BF16)

        z = fl + b_ref[...]
        e = jnp.exp(-jnp.abs(z))
        u = 1.0 + e
        log1p = jnp.where(u == 1.0, e, jnp.log(u) * (e / (u - 1.0)))
        lf = jnp.minimum(z, 0.0) - log1p
        tri = _tri(256, False)
        for r0 in range(0, tm, 256):
            c_ref[r0:r0 + 256, :] = (jnp.dot(tri, lf[r0:r0 + 256], precision=lax.Precision.HIGHEST, preferred_element_type=F32)
                                     + carry[...])
            carry[...] = c_ref[pl.ds(r0 + 255, 1), :]
        c2 = c_ref[...] * LOG2E
        qb = _head_norm(proj[:, 768:1280], gqb_ref[...]) * (SCALE * LOG2E)
        kb = _head_norm(proj[:, 1280:1792], gkb_ref[...])
        lane = _lane((tm, LANES))
        for h in range(8):
            j, half = h // 2, h % 2
            pair, blk = slice(LANES * j, LANES * (j + 1)), slice(LANES * h, LANES * (h + 1))
            feat = _spread3(c2[:, h:h + 1], (tm, LANES), (L_CK, L_CQ))
            q = _put_ones(_head_block(qb[:, pair], half), (L_CK, L_CK + 1, L_CK + 2))
            qo_ref[:, blk] = jnp.where((lane >= L_CQ) & (lane < L_CQ + 3), feat, q).astype(BF16)
            k = _put_ones(_head_block(kb[:, pair], half), tuple(range(L_CQ, L_CQ + 6)))
            ko_ref[:, blk] = jnp.where((lane >= L_CK) & (lane < L_CK + 3), -feat, k).astype(BF16)
            v = _head_block(proj[:, 1792 + LANES * j:1792 + LANES * (j + 1)], half)
            vo_ref[:, blk] = _put_ones(v, (L_ONE, L_DELTA, L_DELTA + 1, L_DELTA + 2)).astype(BF16)

    def tile(w):
        return pl.BlockSpec((tm, w), lambda i: (i, 0))

    aug = jax.ShapeDtypeStruct((t, 8 * LANES), BF16)
    return pl.pallas_call(
        body, name="norm_proj", grid=(t // tm,),
        in_specs=[tile(D_MODEL), _const_spec((1, D_MODEL)), _const_spec((MAIN_W, D_MODEL)), _const_spec((LANES, D_MODEL)),
                  _const_spec((1, 512)), _const_spec((1, 128)), _const_spec((1, 512)), _const_spec((1, 512)),
                  _const_spec((1, LANES))],
        out_specs=[tile(D_MODEL), tile(MAIN_W), tile(LANES), tile(512), tile(256), tile(256)] + [tile(8 * LANES)] * 3,
        out_shape=[jax.ShapeDtypeStruct((t, D_MODEL), BF16), jax.ShapeDtypeStruct((t, MAIN_W), F32),
                   jax.ShapeDtypeStruct((t, LANES), F32), jax.ShapeDtypeStruct((t, 512), BF16),
                   jax.ShapeDtypeStruct((t, 256), BF16), jax.ShapeDtypeStruct((t, 256), BF16), aug, aug, aug],
        scratch_shapes=[pltpu.VMEM((1, LANES), F32), pltpu.VMEM((tm, LANES), F32)],
        compiler_params=_params(("arbitrary",)),
    )(x2, g1, w_main_t, w_f_t, gqa, gka, gqb, gkb, bf_row)


def _tri(n, upper):
    r = lax.broadcasted_iota(jnp.int32, (n, n), 0)
    c = lax.broadcasted_iota(jnp.int32, (n, n), 1)
    return ((c >= r) if upper else (c <= r)).astype(F32)


def _slope(p, hh):
    out = jnp.float32(2.0 ** -(2 * 3 + hh + 1))
    for pp in (2, 1, 0):
        out = jnp.where(p == pp, jnp.float32(2.0 ** -(2 * pp + hh + 1)), out)
    return out


def _swa_windows(ref, i, tq):
    nsub = tq // WINDOW
    cur = ref[pl.ds(pl.multiple_of(i * tq, tq), tq), :].reshape(nsub, WINDOW, LANES)
    first = ref[pl.ds(pl.multiple_of(jnp.maximum(i * tq - WINDOW, 0), WINDOW), WINDOW), :].reshape(1, WINDOW, LANES)
    return jnp.concatenate([jnp.concatenate([first, cur[0:nsub - 1]], axis=0), cur], axis=1)


def _both_heads(x3, lo):
    zero = jnp.zeros_like(x3)
    return jnp.concatenate([jnp.where(lo, x3, zero), jnp.where(lo, zero, x3)], axis=0)


def _swa_head_consts(sink_ref, p, i, nsub):
    bidx = lax.broadcasted_iota(jnp.int32, (2 * nsub, 1, 1), 0)
    is_a = bidx < nsub
    slope = jnp.where(is_a, _slope(p, 0), _slope(p, 1))
    sinks = sink_ref[...]
    sink = jnp.where(is_a, _pick_lane(sinks, 2 * p).reshape(1, 1, 1), _pick_lane(sinks, 2 * p + 1).reshape(1, 1, 1))
    first = (i == 0) & ((bidx == 0) | (bidx == nsub))
    return slope, sink, first


def _swa_fwd(qa, kae, vae, sink_row, nb, s, tq):
    t = qa.shape[0]
    nq = s // tq
    nsub = tq // WINDOW

    def body(q_ref, k_ref, v_ref, sink_ref, o_ref, lse_ref):
        p, i = pl.program_id(1), pl.program_id(2)
        lo = _lane((1, 1, LANES)) < 64
        kk, vv = _swa_windows(k_ref, i, tq), _swa_windows(v_ref, i, tq)
        qs = (q_ref[...].astype(F32) * SCALE).astype(BF16).reshape(nsub, WINDOW, LANES)
        q8 = _both_heads(qs, lo)
        s8 = jnp.einsum("bqd,bkd->bqk", q8, jnp.concatenate([kk, kk], axis=0), preferred_element_type=F32)
        row = lax.broadcasted_iota(jnp.int32, (1, WINDOW, 2 * WINDOW), 1)
        col = lax.broadcasted_iota(jnp.int32, (1, WINDOW, 2 * WINDOW), 2)
        dist = row + WINDOW - col
        slope, sink, first = _swa_head_consts(sink_ref, p, i, nsub)
        valid = (dist >= 0) & (dist < WINDOW) & ((col >= WINDOW) | jnp.logical_not(first))
        s8 = jnp.where(valid, s8 - slope * dist.astype(F32), NEG_INF)
        m = jnp.maximum(jnp.max(s8, axis=2, keepdims=True), sink)
        e = jnp.exp(s8 - m)
        den = jnp.sum(e, axis=2, keepdims=True) + jnp.exp(sink - m)
        pr = (e * (1.0 / den)).astype(BF16)
        o8 = jnp.einsum("bqk,bkd->bqd", pr, jnp.concatenate([vv, vv], axis=0), preferred_element_type=F32)
        lse8 = m + jnp.log(den)
        o_ref[...] = jnp.where(lo, o8[0:nsub], o8[nsub:]).astype(BF16).reshape(tq, LANES)
        lse_ref[...] = jnp.where(lo, lse8[0:nsub], lse8[nsub:]).reshape(tq, LANES)

    return pl.pallas_call(
        body, name="swa_fwd", grid=(nb, 4, nq),
        in_specs=[pl.BlockSpec((tq, LANES), lambda b, p, i: (b * nq + i, p)),
                  pl.BlockSpec((s, LANES), lambda b, p, i: (b, lax.shift_right_logical(p, 1))),
                  pl.BlockSpec((s, LANES), lambda b, p, i: (b, lax.shift_right_logical(p, 1))),
                  pl.BlockSpec((1, LANES), lambda b, p, i: (0, 0))],
        out_specs=[pl.BlockSpec((tq, LANES), lambda b, p, i: (b * nq + i, p)),
                   pl.BlockSpec((None, tq, LANES), lambda b, p, i: (p, b * nq + i, 0))],
        out_shape=[jax.ShapeDtypeStruct((t, 512), BF16), jax.ShapeDtypeStruct((4, t, LANES), F32)],
        compiler_params=_params(("arbitrary", "arbitrary", "arbitrary")),
    )(qa, kae, vae, sink_row)


def _swa_bwd(qa, kae, vae, do_a, sink_row, lse, delta, nb, s, tq):
    t = qa.shape[0]
    nq = s // tq
    nsub = tq // WINDOW

    def body(q_ref, do_ref, k_ref, v_ref, sink_ref, lse_ref, dl_ref, dq_ref, dk_ref, dv_ref, ds_ref):
        p, i = pl.program_id(1), pl.program_id(2)

        @pl.when(i == 0)
        def _():
            ds_ref[...] = jnp.zeros_like(ds_ref)

        lo = _lane((1, 1, LANES)) < 64
        kk, vv = _swa_windows(k_ref, i, tq), _swa_windows(v_ref, i, tq)
        kks = (kk.astype(F32) * SCALE).astype(BF16)
        k8, v8 = jnp.concatenate([kks, kks], axis=0), jnp.concatenate([vv, vv], axis=0)
        q8 = _both_heads(q_ref[...].reshape(nsub, WINDOW, LANES), lo)
        do8 = _both_heads(do_ref[...].reshape(nsub, WINDOW, LANES), lo)
        cur = pl.multiple_of(i * tq, tq)
        sub = lax.broadcasted_iota(jnp.int32, (WINDOW, WINDOW), 0)
        lse_t = [lse_ref[u * WINDOW:(u + 1) * WINDOW, :].T for u in range(nsub)]
        dl_t = [dl_ref[u * WINDOW:(u + 1) * WINDOW, :].T for u in range(nsub)]
        lse8 = jnp.concatenate([t_[64 * hh:64 * hh + 1, :].reshape(1, 1, WINDOW) for hh in range(2) for t_ in lse_t], axis=0)
        dl8 = jnp.concatenate([jnp.sum(jnp.where(sub == 2 * p + hh, t_, 0.0), axis=0, keepdims=True).reshape(1, 1, WINDOW)
                               for hh in range(2) for t_ in dl_t], axis=0)
        row = lax.broadcasted_iota(jnp.int32, (1, 2 * WINDOW, WINDOW), 1)
        col = lax.broadcasted_iota(jnp.int32, (1, 2 * WINDOW, WINDOW), 2)
        dist = col + WINDOW - row
        slope, sink, first = _swa_head_consts(sink_ref, p, i, nsub)
        valid = (dist >= 0) & (dist < WINDOW) & ((row >= WINDOW) | jnp.logical_not(first))
        st = jnp.einsum("bkd,bqd->bkq", k8, q8, preferred_element_type=F32) - slope * dist.astype(F32) - lse8
        pt = jnp.where(valid, jnp.exp(jnp.where(valid, st, 0.0)), 0.0)
        dpt = jnp.einsum("bkd,bqd->bkq", v8, do8, preferred_element_type=F32)
        dst = pt * (dpt - dl8)
        ptb, dstb = pt.astype(BF16), dst.astype(BF16)
        dv8 = jnp.einsum("bkq,bqd->bkd", ptb, do8, preferred_element_type=F32)
        dk8 = jnp.einsum("bkq,bqd->bkd", dstb, q8, preferred_element_type=F32) * SCALE
        dq8 = jnp.einsum("bkq,bkd->bqd", dstb, k8, preferred_element_type=F32)
        dq_ref[...] = jnp.where(lo, dq8[0:nsub], dq8[nsub:]).reshape(tq, LANES)

        psd = jnp.exp(sink - lse8) * dl8
        row_h = lax.broadcasted_iota(jnp.int32, (8, LANES), 0)
        for hh in range(2):
            tot = jnp.sum(jnp.sum(psd[hh * nsub:(hh + 1) * nsub], axis=2, keepdims=True), axis=0, keepdims=True)
            ds_ref[...] += jnp.where(row_h == hh, -tot.reshape(1, 1), 0.0)

        prev = pl.multiple_of(jnp.maximum(i * tq - WINDOW, 0), WINDOW)
        for g8, g_ref in ((dk8, dk_ref), (dv8, dv_ref)):
            g4 = g8[0:nsub] + g8[nsub:]
            own, before = g4[:, WINDOW:, :], g4[:, 0:WINDOW, :]
            shifted = jnp.concatenate([before[1:nsub], jnp.zeros((1, WINDOW, LANES), F32)], axis=0)
            g_ref[pl.ds(cur, tq), :] = (own + shifted).reshape(tq, LANES)
            g_ref[pl.ds(prev, WINDOW), :] += before[0]

    return pl.pallas_call(
        body, name="swa_bwd", grid=(nb, 4, nq),
        in_specs=[pl.BlockSpec((tq, LANES), lambda b, p, i: (b * nq + i, p)),
                  pl.BlockSpec((tq, LANES), lambda b, p, i: (b * nq + i, p)),
                  pl.BlockSpec((s, LANES), lambda b, p, i: (b, lax.shift_right_logical(p, 1))),
                  pl.BlockSpec((s, LANES), lambda b, p, i: (b, lax.shift_right_logical(p, 1))),
                  pl.BlockSpec((1, LANES), lambda b, p, i: (0, 0)),
                  pl.BlockSpec((None, tq, LANES), lambda b, p, i: (p, b * nq + i, 0)),
                  pl.BlockSpec((tq, LANES), lambda b, p, i: (b * nq + i, 0))],
        out_specs=[pl.BlockSpec((tq, LANES), lambda b, p, i: (b * nq + i, p)),
                   pl.BlockSpec((s, LANES), lambda b, p, i: (b, p)),
                   pl.BlockSpec((s, LANES), lambda b, p, i: (b, p)),
                   pl.BlockSpec((None, None, 8, LANES), lambda b, p, i: (b, p, 0, 0))],
        out_shape=[jax.ShapeDtypeStruct((t, 512), F32), jax.ShapeDtypeStruct((t, 512), F32),
                   jax.ShapeDtypeStruct((t, 512), F32), jax.ShapeDtypeStruct((nb, 4, 8, LANES), F32)],
        compiler_params=_params(("arbitrary", "arbitrary", "arbitrary")),
    )(qa, do_a, kae, vae, sink_row, lse, delta)


MESH = pl.DeviceIdType.MESH
ANY = pl.BlockSpec(memory_space=pl.ANY)
N_SEM = 7


def _gather_steps(pairs, send_sems, recv_sems, local_sems):
    x, y, c = lax.axis_index("x"), lax.axis_index("y"), lax.axis_index("c")
    me, sibling = (x, y, c), (x, y, 1 - c)
    chips = [(1 - x, y), (x, 1 - y), (1 - x, 1 - y)]
    mine, first, passed, landed, last = [], [], [], [], []
    for a, (x_ref, out_ref) in enumerate(pairs):
        def slot(px, py, pc, out_ref=out_ref):
            return out_ref.at[4 * px + 2 * py + pc]

        def copy(k, block, to, src=None, a=a, slot=slot):
            return pltpu.make_async_remote_copy(
                src_ref=slot(*block) if src is None else src, dst_ref=slot(*block),
                send_sem=send_sems.at[N_SEM * a + k], recv_sem=recv_sems.at[N_SEM * a + k], device_id=to, device_id_type=MESH)

        mine.append(pltpu.make_async_copy(x_ref, slot(*me), local_sems.at[a]))
        first += [copy(0, me, sibling, src=x_ref)] + [copy(1 + j, me, (*chip, c), src=x_ref) for j, chip in enumerate(chips)]
        passed += [copy(4 + j, (*chip, c), sibling) for j, chip in enumerate(chips)]
        landed += [copy(1 + j, (*chip, c), me) for j, chip in enumerate(chips)]
        last += [copy(0, sibling, me)] + [copy(4 + j, (*chip, 1 - c), me) for j, chip in enumerate(chips)]

    def start():
        for cp in mine + first:
            cp.start()

    def forward():
        for arrived, onward in zip(landed, passed):
            arrived.wait_recv()
            onward.start()

    def finish():
        for cp in last:
            cp.wait_recv()
        for cp in first + passed:
            cp.wait_send()
        for cp in mine:
            cp.wait()

    return start, forward, finish


def _exchange_steps(pairs, send_sems, recv_sems, local_sems):
    x, y, c = lax.axis_index("x"), lax.axis_index("y"), lax.axis_index("c")
    my_id = 4 * x + 2 * y + c
    local, remote = [], []
    for a, (src, dst) in enumerate(pairs):
        local.append(pltpu.make_async_copy(src.at[my_id], dst.at[my_id], local_sems.at[a]))
        for k in range(1, N_DEV):
            px = 1 - x if k & 4 else x
            py = 1 - y if k & 2 else y
            pc = 1 - c if k & 1 else c
            remote.append(pltpu.make_async_remote_copy(
                src_ref=src.at[4 * px + 2 * py + pc], dst_ref=dst.at[my_id],
                send_sem=send_sems.at[N_SEM * a + k - 1], recv_sem=recv_sems.at[N_SEM * a + k - 1],
                device_id=(px, py, pc), device_id_type=MESH))

    def start():
        for cp in local + remote:
            cp.start()

    def finish():
        for cp in remote:
            cp.wait_recv()
        for cp in remote:
            cp.wait_send()
        for cp in local:
            cp.wait()

    return start, finish


L_ONE = 64
L_CK = 65
L_CQ = 68
L_LSE = 71
L_DELTA = 74


def _head_block(pair, half):
    y = pair if half == 0 else pltpu.roll(pair, 64, axis=1)
    return jnp.where(_lane(pair.shape) < 64, y, 0.0)


def _put3(blk, lane0, col):
    lane = _lane(blk.shape)
    hi = col.astype(BF16).astype(F32)
    mid = (col - hi).astype(BF16).astype(F32)
    lo = (col - hi - mid).astype(BF16).astype(F32)
    return jnp.where(lane == lane0, hi, jnp.where(lane == lane0 + 1, mid, jnp.where(lane == lane0 + 2, lo, blk)))


def _spread3(col, shape, lane0s):
    lane = _lane(shape)
    hi = col.astype(BF16).astype(F32)
    mid = (col - hi).astype(BF16).astype(F32)
    lo = (col - hi - mid).astype(BF16).astype(F32)

    def at(k):
        return functools.reduce(jnp.logical_or, [lane == ln + k for ln in lane0s])

    return jnp.where(at(0), hi, jnp.where(at(1), mid, jnp.where(at(2), lo, 0.0)))


def _put_ones(blk, lanes):
    lane = _lane(blk.shape)
    hit = functools.reduce(jnp.logical_or, [lane == ln for ln in lanes])
    return jnp.where(hit, 1.0, blk)


def _to_pairs(ref):
    out = []
    for j in range(4):
        a, b = ref[:, 2 * LANES * j:2 * LANES * j + LANES], ref[:, 2 * LANES * j + LANES:2 * LANES * (j + 1)]
        out.append(jnp.where(_lane(a.shape) < 64, a, pltpu.roll(b, 64, axis=1)))
    return jnp.concatenate(out, axis=1)


def _fox_fwd(q_aug, k_aug, v_aug, nb, s, bt, shards=()):
    t = q_aug.shape[0]
    nq = s // bt
    n_in, n_sh = 3, len(shards)

    def body(*refs):
        q_ref, k_ref, v_ref = refs[:n_in]
        o_ref, ql_ref = refs[n_in + n_sh:n_in + n_sh + 2]
        if shards:
            srcs, dsts = refs[n_in:n_in + n_sh], refs[n_in + n_sh + 2:n_in + 2 * n_sh + 2]
            start, forward, finish = _gather_steps(list(zip(srcs, dsts)), *refs[n_in + 2 * n_sh + 2:])
            step = (pl.program_id(0) * 4 + pl.program_id(1)) * nq + pl.program_id(2)
            pl.when(step == 0)(start)
            pl.when(step == nb * 3 * nq)(forward)
        i = pl.program_id(2)
        row = lax.broadcasted_iota(jnp.int32, (bt, bt), 0)
        col = lax.broadcasted_iota(jnp.int32, (bt, bt), 1)
        sls = [slice(LANES * hh, LANES * (hh + 1)) for hh in range(2)]
        qhs = [q_ref[:, sl] for sl in sls]

        def blk(kb_i, carry, diag):
            start = pl.multiple_of(kb_i * bt, bt)
            new = []
            for (m, acc), qh, sl in zip(carry, qhs, sls):
                sc = _nt(qh, k_ref[pl.ds(start, bt), sl])
                if diag:
                    sc = jnp.where(row >= col, sc, NEG_INF)
                m_new = jnp.maximum(m, jnp.max(sc, axis=1, keepdims=True))
                pr = jnp.exp2(sc - m_new).astype(BF16)
                acc = jnp.exp2(m - m_new) * acc + jnp.dot(pr, v_ref[pl.ds(start, bt), sl], preferred_element_type=F32)
                new.append((m_new, acc))
            return tuple(new)

        init = tuple((jnp.full((bt, 1), NEG_INF, F32), jnp.zeros((bt, LANES), F32)) for _ in range(2))
        carry = lax.fori_loop(0, i, lambda kb_i, c: blk(kb_i, c, False), init)
        outs = []
        for (m, acc), qh, sl in zip(blk(i, carry, True), qhs, sls):
            l = acc[:, L_ONE:L_ONE + 1]
            outs.append(acc * (1.0 / l))
            ql_ref[:, sl] = _put3(qh.astype(F32), L_LSE, -(m + jnp.log(l) * LOG2E)).astype(BF16)
        o_ref[...] = jnp.where(_lane((1, LANES)) < 64, outs[0], pltpu.roll(outs[1], 64, axis=1)).astype(BF16)
        if shards:
            pl.when(step == nb * 4 * nq - 1)(finish)

    in_specs = [pl.BlockSpec((bt, 2 * LANES), lambda b, j, i: (b * nq + i, j)),
                pl.BlockSpec((s, 2 * LANES), lambda b, j, i: (b, j)),
                pl.BlockSpec((s, 2 * LANES), lambda b, j, i: (b, j))]
    out_specs = [pl.BlockSpec((bt, LANES), lambda b, j, i: (b * nq + i, j)),
                 pl.BlockSpec((bt, 2 * LANES), lambda b, j, i: (b * nq + i, j))]
    out_shape = [jax.ShapeDtypeStruct((t, 512), BF16), jax.ShapeDtypeStruct((t, 8 * LANES), BF16)]
    args, scratch = [q_aug, k_aug, v_aug, *shards], []
    if shards:
        in_specs += [ANY] * n_sh
        out_specs += [ANY] * n_sh
        out_shape += [jax.ShapeDtypeStruct((N_DEV,) + sh.shape, sh.dtype) for sh in shards]
        scratch = [pltpu.SemaphoreType.DMA((N_SEM * n_sh,)), pltpu.SemaphoreType.DMA((N_SEM * n_sh,)),
                   pltpu.SemaphoreType.DMA((n_sh,))]
    return pl.pallas_call(
        body, name="fox_fwd", grid=(nb, 4, nq), in_specs=in_specs, out_specs=out_specs, out_shape=out_shape,
        scratch_shapes=scratch, compiler_params=_params(("arbitrary", "arbitrary", "arbitrary")),
    )(*args)


def _fox_bwd(ql_aug, k_aug, v_aug, do_aug, nb, s, bt, exch=()):
    t = ql_aug.shape[0]
    nk = s // bt
    n_in, n_out, n_ex = 4, 3, len(exch)

    def body(*refs):
        q_ref, do_ref, k_ref, v_ref = refs[:n_in]
        dq_ref, dk_ref, dv_ref = refs[n_in + n_ex:n_in + n_ex + n_out]
        if exch:
            srcs = refs[n_in:n_in + n_ex]
            dsts = refs[n_in + n_ex + n_out:n_in + 2 * n_ex + n_out]
            start, finish = _exchange_steps(list(zip(srcs, dsts)), *refs[n_in + 2 * n_ex + n_out:])
            step = (pl.program_id(0) * 4 + pl.program_id(1)) * nk + pl.program_id(2)
            pl.when(step == 0)(start)
        kb_i = pl.program_id(2)

        @pl.when(kb_i == 0)
        def _():
            dq_ref[...] = jnp.zeros_like(dq_ref)

        row = lax.broadcasted_iota(jnp.int32, (bt, bt), 0)
        col = lax.broadcasted_iota(jnp.int32, (bt, bt), 1)
        sls = [slice(LANES * hh, LANES * (hh + 1)) for hh in range(2)]
        khs, vhs = [k_ref[:, sl] for sl in sls], [v_ref[:, sl] for sl in sls]

        def blk(qi, carry, diag):
            start = pl.multiple_of(qi * bt, bt)
            new = []
            for (dk_a, dv_a), kh, vh, sl in zip(carry, khs, vhs, sls):
                qblk, doblk = q_ref[pl.ds(start, bt), sl], do_ref[pl.ds(start, bt), sl]
                st = _nt(kh, qblk)
                if diag:
                    pt = jnp.where(col >= row, jnp.exp2(jnp.where(col >= row, st, 0.0)), 0.0)
                else:
                    pt = jnp.exp2(st)
                dst = pt * _nt(vh, doblk)
                ptb, dstb = pt.astype(BF16), dst.astype(BF16)
                dv_a = dv_a + jnp.dot(ptb, doblk, preferred_element_type=F32)
                dk_a = dk_a + jnp.dot(dstb, qblk, preferred_element_type=F32)
                dq_ref[pl.ds(start, bt), sl] += _tn(dstb, kh)
                new.append((dk_a, dv_a))
            return tuple(new)

        zero = jnp.zeros((bt, LANES), F32)
        carry = blk(kb_i, ((zero, zero), (zero, zero)), True)
        carry = lax.fori_loop(kb_i + 1, nk, lambda qi, c: blk(qi, c, False), carry)
        for (dk_acc, dv_acc), sl in zip(carry, sls):
            dk_ref[:, sl] = dk_acc
            dv_ref[:, sl] = dv_acc
        if exch:
            pl.when(step == nb * 4 * nk - 1)(finish)

    scratch = []
    if exch:
        scratch = [pltpu.SemaphoreType.DMA((N_SEM * n_ex,)), pltpu.SemaphoreType.DMA((N_SEM * n_ex,)),
                   pltpu.SemaphoreType.DMA((n_ex,))]
    whole = pl.BlockSpec((s, 2 * LANES), lambda b, j, kb_i: (b, j))
    tile = pl.BlockSpec((bt, 2 * LANES), lambda b, j, kb_i: (b * nk + kb_i, j))
    shp = jax.ShapeDtypeStruct((t, 8 * LANES), F32)
    return pl.pallas_call(
        body, name="fox_bwd", grid=(nb, 4, nk),
        in_specs=[whole, whole, tile, tile] + [ANY] * n_ex,
        out_specs=[whole, tile, tile] + [ANY] * n_ex,
        out_shape=[shp, shp, shp] + [jax.ShapeDtypeStruct(e.shape, e.dtype) for e in exch],
        scratch_shapes=scratch, compiler_params=_params(("arbitrary", "arbitrary", "arbitrary")),
    )(ql_aug, do_aug, k_aug, v_aug, *exch)


FF_BLK = D_FF // N_DEV


def _mlp_fwd(x2, ma, mb, tgt, w_out, g2, w_up, w_down, tm):
    t = x2.shape[0]

    def body(x_ref, ma_ref, mb_ref, tg_ref, wo_ref, g2_ref, wu_ref, wd_ref,
             h_ref, hn_ref, hid_ref, dy_ref, dyb_ref, loss_ref):
        @pl.when(pl.program_id(0) == 0)
        def _():
            loss_ref[...] = jnp.zeros_like(loss_ref)

        h = (x_ref[...] + jnp.dot(ma_ref[...], wo_ref[0:512, :], preferred_element_type=F32)
             + jnp.dot(mb_ref[...], wo_ref[512:1024, :], preferred_element_type=F32))
        h_ref[...] = h
        r = lax.rsqrt(jnp.mean(h * h, axis=-1, keepdims=True) + EPS)
        hn = (h * r * g2_ref[...]).astype(BF16)
        hn_ref[...] = hn
        for d in range(N_DEV):
            u = jnp.maximum(jnp.dot(hn, wu_ref[d], preferred_element_type=F32), 0.0)
            hid_ref[:, FF_BLK * d:FF_BLK * (d + 1)] = (u * u).astype(BF16)
        y = h + jnp.dot(hid_ref[...], wd_ref[...], preferred_element_type=F32)
        err = y - tg_ref[...]
        dy = err * (1.0 / D_MODEL)
        dy_ref[...] = dy
        dyb_ref[...] = dy.astype(BF16)
        part =0.5 * jnp.sum(jnp.sum(err * err, axis=1, keepdims=True) * (1.0 / D_MODEL), axis=0, keepdims=True)
        loss_ref[...] += part

    def tile(w):
        return pl.BlockSpec((tm, w), lambda i: (i, 0))

    return pl.pallas_call(
        body, name="mlp_fwd", grid=(t // tm,),
        in_specs=[tile(D_MODEL), tile(512), tile(512), tile(D_MODEL), _const_spec((D_MODEL, D_MODEL)),
                  _const_spec((1, D_MODEL)), _const_spec((N_DEV, D_MODEL, FF_BLK)), _const_spec((D_FF, D_MODEL))],
        out_specs=[tile(D_MODEL), tile(D_MODEL), tile(D_FF), tile(D_MODEL), tile(D_MODEL),
                   pl.BlockSpec((8, LANES), lambda i: (0, 0))],
        out_shape=[jax.ShapeDtypeStruct((t, D_MODEL), F32), jax.ShapeDtypeStruct((t, D_MODEL), BF16),
                   jax.ShapeDtypeStruct((t, D_FF), BF16), jax.ShapeDtypeStruct((t, D_MODEL), F32),
                   jax.ShapeDtypeStruct((t, D_MODEL), BF16), jax.ShapeDtypeStruct((8, LANES), F32)],
        compiler_params=_params(("arbitrary",)),
    )(x2, ma, mb, tgt, w_out, g2, w_up, w_down)


def _mlp_bwd(dy, hid, h, ma, mb, w_down, w_up_t, w_out, g2, tm):
    t = dy.shape[0]

    def body(dy_ref, hid_ref, h_ref, ma_ref, mb_ref, wd_ref, wut_ref, wo_ref, g2_ref,
             du_ref, dh_ref, dhb_ref, dma_ref, dob_ref, dla_ref, gg_ref):
        @pl.when(pl.program_id(0) == 0)
        def _():
            gg_ref[...] = jnp.zeros_like(gg_ref)

        dy = dy_ref[...]
        d_hid = _nt(dy.astype(BF16), wd_ref[...])
        du = (d_hid * (2.0 * jnp.sqrt(hid_ref[...].astype(F32)))).astype(BF16)
        du_ref[...] = du
        d_hn = jnp.dot(du, wut_ref[...], preferred_element_type=F32)
        h = h_ref[...]
        r = lax.rsqrt(jnp.mean(h * h, axis=-1, keepdims=True) + EPS)
        hat = h * r
        gd = d_hn * g2_ref[...]
        dh = dy + r * (gd - hat * jnp.mean(gd * hat, axis=-1, keepdims=True))
        gg_ref[...] += jnp.sum(d_hn * hat, axis=0, keepdims=True)
        dh_ref[...] = dh
        dhb = dh.astype(BF16)
        dhb_ref[...] = dhb
        dm = _nt(dhb, wo_ref[...]).astype(BF16)
        dma, dmb = dm[:, 0:512], dm[:, 512:1024]
        dma_ref[...] = dma
        sel = (lax.shift_right_logical(lax.broadcasted_iota(jnp.int32, (512, LANES), 0), 6)
               == lax.broadcasted_iota(jnp.int32, (512, LANES), 1)).astype(BF16)
        dla_ref[...] = _split_dot(dma.astype(F32) * ma_ref[...].astype(F32), sel)
        dmb32 = dmb.astype(F32)
        dlb = _split_dot(dmb32 * mb_ref[...].astype(F32), sel)
        for hd in range(8):
            blk = _head_block(dmb32[:, LANES * (hd // 2):LANES * (hd // 2 + 1)], hd % 2)
            dob_ref[:, LANES * hd:LANES * (hd + 1)] = _put3(blk, L_DELTA, -dlb[:, hd:hd + 1]).astype(BF16)

    def tile(w):
        return pl.BlockSpec((tm, w), lambda i: (i, 0))

    return pl.pallas_call(
        body, name="mlp_bwd", grid=(t // tm,),
        in_specs=[tile(D_MODEL), tile(D_FF), tile(D_MODEL), tile(512), tile(512), _const_spec((D_FF, D_MODEL)),
                  _const_spec((D_FF, D_MODEL)), _const_spec((D_MODEL, D_MODEL)), _const_spec((1, D_MODEL))],
        out_specs=[tile(D_FF), tile(D_MODEL), tile(D_MODEL), tile(512), tile(8 * LANES), tile(LANES),
                   pl.BlockSpec((1, D_MODEL), lambda i: (0, 0))],
        out_shape=[jax.ShapeDtypeStruct((t, D_FF), BF16), jax.ShapeDtypeStruct((t, D_MODEL), F32),
                   jax.ShapeDtypeStruct((t, D_MODEL), BF16), jax.ShapeDtypeStruct((t, 512), BF16),
                   jax.ShapeDtypeStruct((t, 8 * LANES), BF16), jax.ShapeDtypeStruct((t, LANES), F32),
                   jax.ShapeDtypeStruct((1, D_MODEL), F32)],
        compiler_params=_params(("arbitrary",), VMEM_LIMIT_WIDE),
    )(dy, hid, h, ma, mb, w_down, w_up_t, w_out, g2)


def _wgrad(a, b, name, bm, bn, tk, out_dtype=F32, col_blocks=False):
    t, m = a.shape
    n = b.shape[1]
    bm, bn = min(bm, m), min(bn, n)
    nk = t // tk

    def body(a_ref, b_ref, o_ref, acc):
        @pl.when(pl.program_id(2) == 0)
        def _():
            acc[...] = jnp.zeros_like(acc)

        acc[...] += _tn(a_ref[...], b_ref[...])

        @pl.when(pl.program_id(2) == nk - 1)
        def _():
            o_ref[...] = acc[...].astype(out_dtype)

    if col_blocks:
        out_spec = pl.BlockSpec((None, bm, bn), lambda i, j, k: (j, i, 0))
        out_shape = jax.ShapeDtypeStruct((n // bn, m, bn), out_dtype)
    else:
        out_spec = pl.BlockSpec((bm, bn), lambda i, j, k: (i, j))
        out_shape = jax.ShapeDtypeStruct((m, n), out_dtype)
    return pl.pallas_call(
        body, name=name, grid=(m // bm, n // bn, nk),
        in_specs=[pl.BlockSpec((tk, bm), lambda i, j, k: (k, i)), pl.BlockSpec((tk, bn), lambda i, j, k: (k, j))],
        out_specs=out_spec, out_shape=out_shape, scratch_shapes=[pltpu.VMEM((bm, bn), F32)],
        compiler_params=_params(("arbitrary", "arbitrary", "arbitrary")),
    )(a, b)


def _proj_bwd(raw, dqa, dkae, dvae, dqb, dkb, dvb, fl, bf_row, x2, dh, w_main_t, w_f_t, g1, gqa, gka, gqb, gkb, nb, s, tm):
    t = x2.shape[0]
    nt = s // tm

    def body(raw_ref, dqa_ref, dkae_ref, dvae_ref, dqb_ref, dkb_ref, dvb_ref, fl_ref, b_ref, x_ref, dh_ref,
             wmt_ref, wft_ref, g1_ref, gqa_ref, gka_ref, gqb_ref, gkb_ref,
             dx_ref, dp_ref, dfb_ref, ggqa_ref, ggka_ref, ggqb_ref, ggkb_ref, gg1_ref, gb_ref, carry, dlf_ref):
        @pl.when((pl.program_id(0) == 0) & (pl.program_id(1) == 0))
        def _():
            for r in (ggqa_ref, ggka_ref, ggqb_ref, ggkb_ref, gg1_ref, gb_ref):
                r[...] = jnp.zeros_like(r)

        @pl.when(pl.program_id(1) == 0)
        def _():
            carry[...] = jnp.zeros_like(carry)

        lane = _lane((tm, LANES))
        dc = jnp.zeros((tm, LANES), F32)
        for hd in range(8):
            col = (dqb_ref[:, LANES * hd + L_CQ:LANES * hd + L_CQ + 1] - dkb_ref[:, LANES * hd + L_CK:LANES * hd + L_CK + 1])
            dc = jnp.where(lane == hd, col, dc)
        dlf_ref[...] = jnp.dot(_tri(tm, True), dc, precision=lax.Precision.HIGHEST, preferred_element_type=F32) + carry[...]
        carry[...] = dlf_ref[pl.ds(0, 1), :]
        dfl = dlf_ref[...] * (1.0 / (1.0 + jnp.exp(fl_ref[...] + b_ref[...])))
        gb_ref[...] += jnp.sum(dfl, axis=0, keepdims=True)

        raw = raw_ref[...]
        d_qa, p_qa = _head_norm_bwd(raw[:, 0:512], gqa_ref[...], dqa_ref[...])
        d_ka, p_ka = _head_norm_bwd(raw[:, 512:640], gka_ref[...], _fold_kv(dkae_ref[...]))
        d_va = _fold_kv(dvae_ref[...])
        d_qb, p_qb = _head_norm_bwd(raw[:, 768:1280], gqb_ref[...], _to_pairs(dqb_ref) * SCALE)
        d_kb, p_kb = _head_norm_bwd(raw[:, 1280:1792], gkb_ref[...], _to_pairs(dkb_ref) * (1.0 / LOG2E))
        ggqa_ref[...] += jnp.sum(p_qa, axis=0, keepdims=True)
        ggka_ref[...] += jnp.sum(p_ka, axis=0, keepdims=True)
        ggqb_ref[...] += jnp.sum(p_qb, axis=0, keepdims=True)
        ggkb_ref[...] += jnp.sum(p_kb, axis=0, keepdims=True)
        dproj = jnp.concatenate([d_qa, d_ka, d_va, d_qb, d_kb, _to_pairs(dvb_ref)], axis=1).astype(BF16)
        dp_ref[...] = dproj
        dfb = dfl.astype(BF16)
        dfb_ref[...] = dfb
        d_xn = (jnp.dot(dproj, wmt_ref[...], preferred_element_type=F32)
                + jnp.dot(dfb, wft_ref[...], preferred_element_type=F32))
        x = x_ref[...]
        r = lax.rsqrt(jnp.mean(x * x, axis=-1, keepdims=True) + EPS)
        hat = x * r
        gd = d_xn * g1_ref[...]
        dx_ref[...] = dh_ref[...] + r * (gd - hat * jnp.mean(gd * hat, axis=-1, keepdims=True))
        gg1_ref[...] += jnp.sum(d_xn * hat, axis=0, keepdims=True)

    def tile(w):
        return pl.BlockSpec((tm, w), lambda b, i: (b * nt + (nt - 1 - i), 0))

    def acc(w):
        return pl.BlockSpec((1, w), lambda b, i: (0, 0))

    return pl.pallas_call(
        body, name="proj_bwd", grid=(nb, nt),
        in_specs=[tile(MAIN_W), tile(512), tile(512), tile(512), tile(8 * LANES), tile(8 * LANES), tile(8 * LANES), tile(LANES),
                  _const_spec((1, LANES)), tile(D_MODEL), tile(D_MODEL), _const_spec((MAIN_W, D_MODEL)),
                  _const_spec((LANES, D_MODEL)), _const_spec((1, D_MODEL)), _const_spec((1, 512)), _const_spec((1, 128)),
                  _const_spec((1, 512)), _const_spec((1, 512))],
        out_specs=[tile(D_MODEL), tile(MAIN_W), tile(LANES), acc(512), acc(128), acc(512), acc(512), acc(D_MODEL), acc(LANES)],
        out_shape=[jax.ShapeDtypeStruct((t, D_MODEL), F32), jax.ShapeDtypeStruct((t, MAIN_W), BF16),
                   jax.ShapeDtypeStruct((t, LANES), BF16), jax.ShapeDtypeStruct((1, 512), F32),
                   jax.ShapeDtypeStruct((1, 128), F32), jax.ShapeDtypeStruct((1, 512), F32),
                   jax.ShapeDtypeStruct((1, 512), F32), jax.ShapeDtypeStruct((1, D_MODEL), F32),
                   jax.ShapeDtypeStruct((1, LANES), F32)],
        scratch_shapes=[pltpu.VMEM((1, LANES), F32), pltpu.VMEM((tm, LANES), F32)],
        compiler_params=_params(("arbitrary", "arbitrary"), VMEM_LIMIT_WIDE),
    )(raw, dqa, dkae, dvae, dqb, dkb, dvb, fl, bf_row, x2, dh, w_main_t, w_f_t, g1, gqa, gka, gqb, gkb)


IN_PAD = 304


def _local_step(x, tgt, w_in_t, rest, g1, b_forget, qna, kna, sinks, qnb, knb, g2,
                tm=512, bt=512, btf=1024, tq=2048, wk=4096, distributed=False):
    nb, s, _ = x.shape
    t = nb * s
    x2, tgt2 = x.reshape(t, D_MODEL), tgt.reshape(t, D_MODEL)
    g1r, g2r = g1.reshape(1, D_MODEL), g2.reshape(1, D_MODEL)
    gqa, gka = jnp.tile(qna, 8).reshape(1, 512), jnp.tile(kna, 2).reshape(1, 128)
    gqb, gkb = jnp.tile(qnb, 8).reshape(1, 512), jnp.tile(knb, 8).reshape(1, 512)
    bf_row = jnp.pad(b_forget, (0, LANES - 8)).reshape(1, LANES)
    sink_row = jnp.pad(sinks, (0, LANES - 8)).reshape(1, LANES)
    w_main_t = w_in_t[0:MAIN_W]
    w_f_t = jnp.pad(w_in_t[MAIN_W:IN_W], ((0, LANES - 8), (0, 0)))

    xn, raw, fl, qa, kae, vae, q_aug, k_aug, v_aug = _norm_proj(x2, g1r, w_main_t, w_f_t, gqa, gka, gqb, gkb, bf_row, s, tm)
    ma, lse_a = _swa_fwd(qa, kae, vae, sink_row, nb, s, tq)
    if distributed:
        mb, ql_aug, w_out, w_up, w_down, w_up_t = _fox_fwd(q_aug, k_aug, v_aug, nb, s, btf, shards=rest)
    else:
        mb, ql_aug = _fox_fwd(q_aug, k_aug, v_aug, nb, s, btf)
        w_out, w_up, w_down, w_up_t = rest
    w_out, w_down = w_out.reshape(D_MODEL, D_MODEL), w_down.reshape(D_FF, D_MODEL)
    h, hn, hid, dy, dyb, loss_acc = _mlp_fwd(x2, ma, mb, tgt2, w_out, g2r, w_up, w_down, tm)

    du, dh, dhb, dma, do_aug, dla, gg2 = _mlp_bwd(dy, hid, h, ma, mb, w_down, w_up_t.reshape(D_FF, D_MODEL), w_out, g2r, tm)
    g_down = _wgrad(hid, dyb, "wgrad_down", 512, 1024, wk, BF16).reshape(N_DEV, 512, D_MODEL)
    g_up = _wgrad(hn, du, "wgrad_up", 1024, 512, wk, BF16, col_blocks=True)
    g_out = jnp.concatenate([_wgrad(ma, dhb, "wgrad_out_a", 512, 1024, wk, BF16),
                             _wgrad(mb, dhb, "wgrad_out_b", 512, 1024, wk, BF16)], axis=0).reshape(N_DEV, 128, D_MODEL)

    dqa, dkae, dvae, dsink = _swa_bwd(qa, kae, vae, dma, sink_row, lse_a, dla, nb, s, tq)
    fox = _fox_bwd(ql_aug, k_aug, v_aug, do_aug, nb, s, bt, exch=(g_out, g_up, g_down) if distributed else ())
    dqb, dkb, dvb = fox[:3]
    if distributed:
        g_out, g_up, g_down = fox[3:]
    grad_x, dproj, dfb, ggqa, ggka, ggqb, ggkb, gg1, gbf = _proj_bwd(
        raw, dqa, dkae, dvae, dqb, dkb, dvb, fl, bf_row, x2, dh, w_main_t, w_f_t, g1r, gqa, gka, gqb, gkb, nb, s, tm)
    g_in_t = jnp.concatenate([_wgrad(dproj, xn, "wgrad_in", 768, 1024, wk), _wgrad(dfb, xn, "wgrad_gate", 128, 1024, wk)[0:8]],
                             axis=0)

    small = (gg1.reshape(D_MODEL), gbf[0, 0:8], ggqa.reshape(8, 64).sum(0), ggka.reshape(2, 64).sum(0),
             dsink.sum(0)[:, 0:2, 0].reshape(8), ggqb.reshape(8, 64).sum(0), ggkb.reshape(8, 64).sum(0),
             gg2.reshape(D_MODEL))
    return loss_acc[0, 0], grad_x.reshape(nb, s, D_MODEL), g_in_t, g_out, g_up, g_down, small


def _all_gather(shard):
    def body(x_ref, out_ref, send_sems, recv_sems, local_sem):
        start, forward, finish = _gather_steps([(x_ref, out_ref)], send_sems, recv_sems, local_sem)
        start()
        forward()
        finish()

    return pl.pallas_call(
        body, name="gather_w_in", out_shape=jax.ShapeDtypeStruct((N_DEV,) + shard.shape, shard.dtype),
        in_specs=[ANY], out_specs=ANY,
        scratch_shapes=[pltpu.SemaphoreType.DMA((N_SEM,)), pltpu.SemaphoreType.DMA((N_SEM,)), pltpu.SemaphoreType.DMA((1,))],
    )(shard)


def _exchange(*arrays):
    n_ex = len(arrays)

    def body(*refs):
        start, finish = _exchange_steps(list(zip(refs[:n_ex], refs[n_ex:2 * n_ex])), *refs[2 * n_ex:])
        start()
        finish()

    return pl.pallas_call(
        body, name="exchange_tail", out_shape=[jax.ShapeDtypeStruct(a.shape, a.dtype) for a in arrays],
        in_specs=[ANY] * n_ex, out_specs=[ANY] * n_ex,
        scratch_shapes=[pltpu.SemaphoreType.DMA((N_SEM * n_ex,)), pltpu.SemaphoreType.DMA((N_SEM * n_ex,)),
                        pltpu.SemaphoreType.DMA((n_ex,))],
    )(*arrays)


def _sum_adamw(recv, w, m, v, tr, name):
    _, r, n = recv.shape

    def body(r_ref, w_ref, m_ref, v_ref, g_ref, d_ref, nm_ref, nv_ref):
        g = r_ref[0].astype(F32)
        for s in range(1, N_DEV):
            g = g + r_ref[s].astype(F32)
        g_ref[...] = g
        nm = ADAM_B1 * m_ref[...] + (1.0 - ADAM_B1) * g
        nv = ADAM_B2 * v_ref[...] + (1.0 - ADAM_B2) * (g * g)
        m_hat = nm / (1.0 - ADAM_B1 ** ADAM_STEP)
        v_hat = nv / (1.0 - ADAM_B2 ** ADAM_STEP)
        d_ref[...] = -ADAM_LR * (m_hat / (jnp.sqrt(v_hat) + ADAM_EPS) + ADAM_WD * w_ref[...])
        nm_ref[...] = nm
        nv_ref[...] = nv

    tile = pl.BlockSpec((tr, n), lambda i: (i, 0))
    shp = jax.ShapeDtypeStruct((r, n), F32)
    return pl.pallas_call(
        body, name=name, grid=(r // tr,),
        in_specs=[pl.BlockSpec((N_DEV, tr, n), lambda i: (0, i, 0)), tile, tile, tile],
        out_specs=[tile, tile, tile, tile], out_shape=[shp, shp, shp, shp],
        compiler_params=_params(("arbitrary",)),
    )(recv, w, m, v)


def _small_rows(g1, bf, qna, kna, sk, qnb, knb, g2):
    row2 = jnp.concatenate([bf, qna, kna, sk, qnb, knb])
    return jnp.zeros((8, D_MODEL), F32).at[0].set(g1).at[1].set(g2).at[2, 0:row2.shape[0]].set(row2)


def _in_rows(w_in_s):
    return jnp.pad(w_in_s.T, ((0, IN_PAD - IN_SHARD), (0, 0)))


def kernel(x, attn_norm_g, w_in, b_forget, q_norm_a, k_norm_a, sink_logits, q_norm_b, k_norm_b, w_out, mlp_norm_g, w_up, w_down, loss_target, m_attn_norm_g, m_w_in, m_b_forget, m_q_norm_a, m_k_norm_a, m_sink_logits, m_q_norm_b, m_k_norm_b, m_w_out, m_mlp_norm_g, m_w_up, m_w_down, v_attn_norm_g, v_w_in, v_b_forget, v_q_norm_a, v_k_norm_a, v_sink_logits, v_q_norm_b, v_k_norm_b, v_w_out, v_mlp_norm_g, v_w_up, v_w_down):
    w_in_r = _in_rows(w_in)
    w_in_t = _all_gather(w_in_r.astype(BF16))[:, 0:IN_SHARD].reshape(IN_W, D_MODEL)
    w_up_b = w_up.astype(BF16)
    rest = (w_out.astype(BF16), w_up_b, w_down.astype(BF16), w_up_b.T)

    loss_part, grad_x, g_in_t, r_out, r_up, r_down, small = _local_step(
        x, loss_target, w_in_t, rest, attn_norm_g, b_forget, q_norm_a, k_norm_a, sink_logits, q_norm_b, k_norm_b, mlp_norm_g,
        distributed=True)

    g_in_blocks = jnp.pad(g_in_t.reshape(N_DEV, IN_SHARD, D_MODEL), ((0, 0), (0, IN_PAD - IN_SHARD), (0, 0))).astype(BF16)
    small_blocks = jnp.broadcast_to(_small_rows(*small).at[3, 0].set(loss_part), (N_DEV, 8, D_MODEL))
    r_in, r_small = _exchange(g_in_blocks, small_blocks)

    small_w = _small_rows(attn_norm_g, b_forget, q_norm_a, k_norm_a, sink_logits, q_norm_b, k_norm_b, mlp_norm_g)
    small_m = _small_rows(m_attn_norm_g, m_b_forget, m_q_norm_a, m_k_norm_a, m_sink_logits, m_q_norm_b, m_k_norm_b, m_mlp_norm_g)
    small_v = _small_rows(v_attn_norm_g, v_b_forget, v_q_norm_a, v_k_norm_a, v_sink_logits, v_q_norm_b, v_k_norm_b, v_mlp_norm_g)
    o_in = [a[0:IN_SHARD].T for a in _sum_adamw(r_in, w_in_r, _in_rows(m_w_in), _in_rows(v_w_in), IN_PAD, "adamw_in")]
    o_out = _sum_adamw(r_out, w_out, m_w_out, v_w_out, 128, "adamw_out")
    o_up = _sum_adamw(r_up, w_up, m_w_up, v_w_up, 256, "adamw_up")
    o_down = _sum_adamw(r_down, w_down, m_w_down, v_w_down, 128, "adamw_down")
    o_small = _sum_adamw(r_small, small_w, small_m, small_v, 8, "adamw_small")

    def leaves(i):
        row2 = o_small[i][2]
        return (o_small[i][0], o_in[i], row2[0:8], row2[8:72], row2[72:136], row2[136:144], row2[144:208], row2[208:272],
                o_out[i], o_small[i][1], o_up[i], o_down[i])

    return (o_small[0][3, 0], grad_x, *leaves(0), *leaves(1), *leaves(2), *leaves(3))
```

```python
import functools

import jax
import jax.numpy as jnp
from jax import lax
from jax.experimental import pallas as pl
from jax.experimental.pallas import tpu as pltpu

F32 = jnp.float32
BF16 = jnp.bfloat16

D_MODEL = 1024
HEAD_DIM = 64
N_DEV = 8
D_FF = 4096
MAIN_W = 2304
IN_W = 2312
IN_SHARD = 289
WINDOW = 128
EPS = 1e-6
SCALE = 0.125
LOG2E = 1.4426950408889634
LANES = 128
NEG_INF = float("-inf")

ADAM_LR = 0.001
ADAM_B1 = 0.9
ADAM_B2 = 0.999
ADAM_EPS = 1e-08
ADAM_WD = 0.01
ADAM_STEP = 10

VMEM_LIMIT = 56 * 1024 * 1024
VMEM_LIMIT_WIDE = 62 * 1024 * 1024


def _params(sem, vmem=VMEM_LIMIT):
    return pltpu.CompilerParams(dimension_semantics=sem, vmem_limit_bytes=vmem)


def _const_spec(shape):
    nd = len(shape)
    return pl.BlockSpec(shape, lambda *_: (0,) * nd, pipeline_mode=pl.Buffered(1))


def _lane(shape):
    return lax.broadcasted_iota(jnp.int32, shape, len(shape) - 1)


def _split_dot(v, mat):
    hi = v.astype(BF16)
    lo = (v - hi.astype(F32)).astype(BF16)
    return (jnp.dot(hi, mat, preferred_element_type=F32) + jnp.dot(lo, mat, preferred_element_type=F32))


def _head_ones(n):
    r = lax.shift_right_logical(lax.broadcasted_iota(jnp.int32, (n, n), 0), 6)
    c = lax.shift_right_logical(lax.broadcasted_iota(jnp.int32, (n, n), 1), 6)
    return (r == c).astype(BF16)


def _head_sum(v):
    w = v.shape[1]
    vb = v.astype(BF16)
    if w <= 256:
        return jnp.dot(vb, _head_ones(w), preferred_element_type=F32)
    ones = _head_ones(256)
    return jnp.concatenate([jnp.dot(vb[:, s:s + 256], ones, preferred_element_type=F32) for s in range(0, w, 256)], axis=1)


def _head_norm(seg, gain):
    rs = lax.rsqrt(_head_sum(seg * seg) * (1.0 / HEAD_DIM) + EPS)
    return seg * rs * gain


def _head_norm_bwd(seg, gain, d_out):
    rs = lax.rsqrt(_head_sum(seg * seg) * (1.0 / HEAD_DIM) + EPS)
    hat = seg * rs
    gd = d_out * gain
    d_seg = rs * (gd - hat * (_head_sum(gd * hat) * (1.0 / HEAD_DIM)))
    return d_seg, d_out * hat


def _expand_kv(v):
    r = pltpu.roll(v, 64, axis=1)
    lo = _lane(v.shape) < 64
    return jnp.concatenate([jnp.where(lo, v, r), jnp.where(lo, r, v)], axis=1)


def _fold_kv(e4):
    t0 = e4[:, 0:128] + e4[:, 128:256]
    t1 = e4[:, 256:384] + e4[:, 384:512]
    t0 = t0 + pltpu.roll(t0, 64, axis=1)
    t1 = t1 + pltpu.roll(t1, 64, axis=1)
    return jnp.where(_lane(t0.shape) < 64, t0, t1)


def _pick_lane(blk, idx):
    return jnp.sum(jnp.where(_lane(blk.shape) == idx, blk, 0.0), axis=1, keepdims=True)


def _nt(a, b):
    return lax.dot_general(a, b, (((1,), (1,)), ((), ())), preferred_element_type=F32)


def _tn(a, b):
    return lax.dot_general(a, b, (((0,), (0,)), ((), ())), preferred_element_type=F32)


def _norm_proj(x2, g1, w_main_t, w_f_t, gqa, gka, gqb, gkb, bf_row, s, tm):
    t = x2.shape[0]
    nt = s // tm

    def body(x_ref, g1_ref, wm_ref, wf_ref, gqa_ref, gka_ref, gqb_ref, gkb_ref, b_ref,
             xn_ref, raw_ref, fl_ref, qa_ref, kae_ref, vae_ref, qo_ref, ko_ref, vo_ref, carry, c_ref):
        @pl.when(lax.rem(pl.program_id(0), nt) == 0)
        def _():
            carry[...] = jnp.zeros_like(carry)

        x = x_ref[...]
        r = lax.rsqrt(jnp.mean(x * x, axis=-1, keepdims=True) + EPS)
        xn = (x * r * g1_ref[...]).astype(BF16)
        xn_ref[...] = xn
        proj = _nt(xn, wm_ref[...])
        raw_ref[...] = proj
        fl = _nt(xn, wf_ref[...])
        fl_ref[...] = fl
        qa_ref[...] = _head_norm(proj[:, 0:512], gqa_ref[...]).astype(BF16)
        kae_ref[...] = _expand_kv(_head_norm(proj[:, 512:640], gka_ref[...])).astype(BF16)
        vae_ref[...] = _expand_kv(proj[:, 640:768]).astype(BF16)

        z = fl + b_ref[...]
        e = jnp.exp(-jnp.abs(z))
        u = 1.0 + e
        log1p = jnp.where(u == 1.0, e, jnp.log(u) * (e / (u - 1.0)))
        lf = jnp.minimum(z, 0.0) - log1p
        for r0 in range(0, tm, 256):
            c_ref[r0:r0 + 256, :] = _tri_dot(256, False, lf[r0:r0 + 256]) + carry[...]
            carry[...] = c_ref[pl.ds(r0 + 255, 1), :]
        c2 = c_ref[...] * LOG2E
        qb = _head_norm(proj[:, 768:1280], gqb_ref[...]) * (SCALE * LOG2E)
        kb = _head_norm(proj[:, 1280:1792], gkb_ref[...])
        lane = _lane((tm, LANES))
        for h in range(8):
            j, half = h // 2, h % 2
            pair, blk = slice(LANES * j, LANES * (j + 1)), slice(LANES * h, LANES * (h + 1))
            feat = _spread3(c2[:, h:h + 1], (tm, LANES), (L_CK, L_CQ))
            q = _put_ones(_head_block(qb[:, pair], half), (L_CK, L_CK + 1, L_CK + 2))
            qo_ref[:, blk] = jnp.where((lane >= L_CQ) & (lane < L_CQ + 3), feat, q).astype(BF16)
            k = _put_ones(_head_block(kb[:, pair], half), tuple(range(L_CQ, L_CQ + 6)))
            ko_ref[:, blk] = jnp.where((lane >= L_CK) & (lane < L_CK + 3), -feat, k).astype(BF16)
            v = _head_block(proj[:, 1792 + LANES * j:1792 + LANES * (j + 1)], half)
            vo_ref[:, blk] = _put_ones(v, (L_ONE, L_DELTA, L_DELTA + 1, L_DELTA + 2)).astype(BF16)

    def tile(w):
        return pl.BlockSpec((tm, w), lambda i: (i, 0))

    aug = jax.ShapeDtypeStruct((t, 8 * LANES), BF16)
    return pl.pallas_call(
        body, name="norm_proj", grid=(t // tm,),
        in_specs=[tile(D_MODEL), _const_spec((1, D_MODEL)), _const_spec((MAIN_W, D_MODEL)), _const_spec((LANES, D_MODEL)),
                  _const_spec((1, 512)), _const_spec((1, 128)), _const_spec((1, 512)), _const_spec((1, 512)),
                  _const_spec((1, LANES))],
        out_specs=[tile(D_MODEL), tile(MAIN_W), tile(LANES), tile(512), tile(256), tile(256)] + [tile(8 * LANES)] * 3,
        out_shape=[jax.ShapeDtypeStruct((t, D_MODEL), BF16), jax.ShapeDtypeStruct((t, MAIN_W), F32),
                   jax.ShapeDtypeStruct((t, LANES), F32), jax.ShapeDtypeStruct((t, 512), BF16),
                   jax.ShapeDtypeStruct((t, 256), BF16), jax.ShapeDtypeStruct((t, 256), BF16), aug, aug, aug],
        scratch_shapes=[pltpu.VMEM((1, LANES), F32), pltpu.VMEM((tm, LANES), F32)],
        compiler_params=_params(("arbitrary",)),
    )(x2, g1, w_main_t, w_f_t, gqa, gka, gqb, gkb, bf_row)


def _tri_dot(n, upper, v):
    r = lax.broadcasted_iota(jnp.int32, (n, n), 0)
    c = lax.broadcasted_iota(jnp.int32, (n, n), 1)
    tri = ((c >= r) if upper else (c <= r)).astype(BF16)
    hi = v.astype(BF16)
    mid = (v - hi.astype(F32)).astype(BF16)
    lo = (v - hi.astype(F32) - mid.astype(F32)).astype(BF16)
    return (jnp.dot(tri, hi, preferred_element_type=F32) + jnp.dot(tri, mid, preferred_element_type=F32)
            + jnp.dot(tri, lo, preferred_element_type=F32))


def _slope(p, hh):
    out = jnp.float32(2.0 ** -(2 * 3 + hh + 1))
    for pp in (2, 1, 0):
        out = jnp.where(p == pp, jnp.float32(2.0 ** -(2 * pp + hh + 1)), out)
    return out


def _swa_windows(ref, i, tq):
    nsub = tq // WINDOW
    cur = ref[pl.ds(pl.multiple_of(i * tq, tq), tq), :].reshape(nsub, WINDOW, LANES)
    first = ref[pl.ds(pl.multiple_of(jnp.maximum(i * tq - WINDOW, 0), WINDOW), WINDOW), :].reshape(1, WINDOW, LANES)
    return jnp.concatenate([jnp.concatenate([first, cur[0:nsub - 1]], axis=0), cur], axis=1)


def _both_heads(x3, lo):
    zero = jnp.zeros_like(x3)
    return jnp.concatenate([jnp.where(lo, x3, zero), jnp.where(lo, zero, x3)], axis=0)


def _swa_head_consts(sink_ref, p, i, nsub):
    bidx = lax.broadcasted_iota(jnp.int32, (2 * nsub, 1, 1), 0)
    is_a = bidx < nsub
    slope = jnp.where(is_a, _slope(p, 0), _slope(p, 1))
    sinks = sink_ref[...]
    sink = jnp.where(is_a, _pick_lane(sinks, 2 * p).reshape(1, 1, 1), _pick_lane(sinks, 2 * p + 1).reshape(1, 1, 1))
    first = (i == 0) & ((bidx == 0) | (bidx == nsub))
    return slope, sink, first


def _swa_fwd(qa, kae, vae, sink_row, nb, s, tq):
    t = qa.shape[0]
    nq = s // tq
    nsub = tq // WINDOW

    def body(q_ref, k_ref, v_ref, sink_ref, o_ref, lse_ref):
        p, i = pl.program_id(1), pl.program_id(2)
        lo = _lane((1, 1, LANES)) < 64
        kk, vv = _swa_windows(k_ref, i, tq), _swa_windows(v_ref, i, tq)
        qs = (q_ref[...].astype(F32) * SCALE).astype(BF16).reshape(nsub, WINDOW, LANES)
        q8 = _both_heads(qs, lo)
        s8 = jnp.einsum("bqd,bkd->bqk", q8, jnp.concatenate([kk, kk], axis=0), preferred_element_type=F32)
        row = lax.broadcasted_iota(jnp.int32, (1, WINDOW, 2 * WINDOW), 1)
        col = lax.broadcasted_iota(jnp.int32, (1, WINDOW, 2 * WINDOW), 2)
        dist = row + WINDOW - col
        slope, sink, first = _swa_head_consts(sink_ref, p, i, nsub)
        valid = (dist >= 0) & (dist < WINDOW) & ((col >= WINDOW) | jnp.logical_not(first))
        s8 = jnp.where(valid, s8 - slope * dist.astype(F32), NEG_INF)
        m = jnp.maximum(jnp.max(s8, axis=2, keepdims=True), sink)
        e = jnp.exp(s8 - m)
        den = jnp.sum(e, axis=2, keepdims=True) + jnp.exp(sink - m)
        pr = (e * (1.0 / den)).astype(BF16)
        o8 = jnp.einsum("bqk,bkd->bqd", pr, jnp.concatenate([vv, vv], axis=0), preferred_element_type=F32)
        lse8 = m + jnp.log(den)
        o_ref[...] = jnp.where(lo, o8[0:nsub], o8[nsub:]).astype(BF16).reshape(tq, LANES)
        lse_ref[...] = jnp.where(lo, lse8[0:nsub], lse8[nsub:]).reshape(tq, LANES)

    return pl.pallas_call(
        body, name="swa_fwd", grid=(nb, 4, nq),
        in_specs=[pl.BlockSpec((tq, LANES), lambda b, p, i: (b * nq + i, p)),
                  pl.BlockSpec((s, LANES), lambda b, p, i: (b, lax.shift_right_logical(p, 1))),
                  pl.BlockSpec((s, LANES), lambda b, p, i: (b, lax.shift_right_logical(p, 1))),
                  pl.BlockSpec((1, LANES), lambda b, p, i: (0, 0))],
        out_specs=[pl.BlockSpec((tq, LANES), lambda b, p, i: (b * nq + i, p)),
                   pl.BlockSpec((None, tq, LANES), lambda b, p, i: (p, b * nq + i, 0))],
        out_shape=[jax.ShapeDtypeStruct((t, 512), BF16), jax.ShapeDtypeStruct((4, t, LANES), F32)],
        compiler_params=_params(("arbitrary", "arbitrary", "arbitrary")),
    )(qa, kae, vae, sink_row)


def _swa_bwd(qa, kae, vae, do_a, sink_row, lse, delta, nb, s, tq):
    t = qa.shape[0]
    nq = s // tq
    nsub = tq // WINDOW

    def body(q_ref, do_ref, k_ref, v_ref, sink_ref, lse_ref, dl_ref, dq_ref, dk_ref, dv_ref, ds_ref):
        p, i = pl.program_id(1), pl.program_id(2)

        @pl.when(i == 0)
        def _():
            ds_ref[...] = jnp.zeros_like(ds_ref)

        lo = _lane((1, 1, LANES)) < 64
        kk, vv = _swa_windows(k_ref, i, tq), _swa_windows(v_ref, i, tq)
        kks = (kk.astype(F32) * SCALE).astype(BF16)
        k8, v8 = jnp.concatenate([kks, kks], axis=0), jnp.concatenate([vv, vv], axis=0)
        q8 = _both_heads(q_ref[...].reshape(nsub, WINDOW, LANES), lo)
        do8 = _both_heads(do_ref[...].reshape(nsub, WINDOW, LANES), lo)
        cur = pl.multiple_of(i * tq, tq)
        sub = lax.broadcasted_iota(jnp.int32, (WINDOW, WINDOW), 0)
        lse_t = [lse_ref[u * WINDOW:(u + 1) * WINDOW, :].T for u in range(nsub)]
        dl_t = [dl_ref[u * WINDOW:(u + 1) * WINDOW, :].T for u in range(nsub)]
        lse8 = jnp.concatenate([t_[64 * hh:64 * hh + 1, :].reshape(1, 1, WINDOW) for hh in range(2) for t_ in lse_t], axis=0)
        dl8 = jnp.concatenate([jnp.sum(jnp.where(sub == 2 * p + hh, t_, 0.0), axis=0, keepdims=True).reshape(1, 1, WINDOW)
                               for hh in range(2) for t_ in dl_t], axis=0)
        row = lax.broadcasted_iota(jnp.int32, (1, 2 * WINDOW, WINDOW), 1)
        col = lax.broadcasted_iota(jnp.int32, (1, 2 * WINDOW, WINDOW), 2)
        dist = col + WINDOW - row
        slope, sink, first = _swa_head_consts(sink_ref, p, i, nsub)
        valid = (dist >= 0) & (dist < WINDOW) & ((row >= WINDOW) | jnp.logical_not(first))
        st = jnp.einsum("bkd,bqd->bkq", k8, q8, preferred_element_type=F32) - slope * dist.astype(F32) - lse8
        pt = jnp.where(valid, jnp.exp(jnp.where(valid, st, 0.0)), 0.0)
        dpt = jnp.einsum("bkd,bqd->bkq", v8, do8, preferred_element_type=F32)
        dst = pt * (dpt - dl8)
        ptb, dstb = pt.astype(BF16), dst.astype(BF16)
        dv8 = jnp.einsum("bkq,bqd->bkd", ptb, do8, preferred_element_type=F32)
        dk8 = jnp.einsum("bkq,bqd->bkd", dstb, q8, preferred_element_type=F32) * SCALE
        dq8 = jnp.einsum("bkq,bkd->bqd", dstb, k8, preferred_element_type=F32)
        dq_ref[...] = jnp.where(lo, dq8[0:nsub], dq8[nsub:]).reshape(tq, LANES)

        psd = jnp.exp(sink - lse8) * dl8
        row_h = lax.broadcasted_iota(jnp.int32, (8, LANES), 0)
        for hh in range(2):
            tot = jnp.sum(jnp.sum(psd[hh * nsub:(hh + 1) * nsub], axis=2, keepdims=True), axis=0, keepdims=True)
            ds_ref[...] += jnp.where(row_h == hh, -tot.reshape(1, 1), 0.0)

        prev = pl.multiple_of(jnp.maximum(i * tq - WINDOW, 0), WINDOW)
        for g8, g_ref in ((dk8, dk_ref), (dv8, dv_ref)):
            g4 = g8[0:nsub] + g8[nsub:]
            own, before = g4[:, WINDOW:, :], g4[:, 0:WINDOW, :]
            shifted = jnp.concatenate([before[1:nsub], jnp.zeros((1, WINDOW, LANES), F32)], axis=0)
            g_ref[pl.ds(cur, tq), :] = (own + shifted).reshape(tq, LANES)
            g_ref[pl.ds(prev, WINDOW), :] += before[0]

    return pl.pallas_call(
        body, name="swa_bwd", grid=(nb, 4, nq),
        in_specs=[pl.BlockSpec((tq, LANES), lambda b, p, i: (b * nq + i, p)),
                  pl.BlockSpec((tq, LANES), lambda b, p, i: (b * nq + i, p)),
                  pl.BlockSpec((s, LANES), lambda b, p, i: (b, lax.shift_right_logical(p, 1))),
                  pl.BlockSpec((s, LANES), lambda b, p, i: (b, lax.shift_right_logical(p, 1))),
                  pl.BlockSpec((1, LANES), lambda b, p, i: (0, 0)),
                  pl.BlockSpec((None, tq, LANES), lambda b, p, i: (p, b * nq + i, 0)),
                  pl.BlockSpec((tq, LANES), lambda b, p, i: (b * nq + i, 0))],
        out_specs=[pl.BlockSpec((tq, LANES), lambda b, p, i: (b * nq + i, p)),
                   pl.BlockSpec((s, LANES), lambda b, p, i: (b, p)),
                   pl.BlockSpec((s, LANES), lambda b, p, i: (b, p)),
                   pl.BlockSpec((None, None, 8, LANES), lambda b, p, i: (b, p, 0, 0))],
        out_shape=[jax.ShapeDtypeStruct((t, 512), F32), jax.ShapeDtypeStruct((t, 512), F32),
                   jax.ShapeDtypeStruct((t, 512), F32), jax.ShapeDtypeStruct((nb, 4, 8, LANES), F32)],
        compiler_params=_params(("arbitrary", "arbitrary", "arbitrary")),
    )(qa, do_a, kae, vae, sink_row, lse, delta)


MESH = pl.DeviceIdType.MESH
ANY = pl.BlockSpec(memory_space=pl.ANY)
N_SEM = 7


def _gather_steps(pairs, send_sems, recv_sems, local_sems):
    x, y, c = lax.axis_index("x"), lax.axis_index("y"), lax.axis_index("c")
    me, sibling = (x, y, c), (x, y, 1 - c)
    chips = [(1 - x, y), (x, 1 - y), (1 - x, 1 - y)]
    mine, first, passed, landed, last = [], [], [], [], []
    for a, (x_ref, out_ref) in enumerate(pairs):
        def slot(px, py, pc, out_ref=out_ref):
            return out_ref.at[4 * px + 2 * py + pc]

        def copy(k, block, to, src=None, a=a, slot=slot):
            return pltpu.make_async_remote_copy(
                src_ref=slot(*block) if src is None else src, dst_ref=slot(*block),
                send_sem=send_sems.at[N_SEM * a + k], recv_sem=recv_sems.at[N_SEM * a + k], device_id=to, device_id_type=MESH)

        mine.append(pltpu.make_async_copy(x_ref, slot(*me), local_sems.at[a]))
        first += [copy(0, me, sibling, src=x_ref)] + [copy(1 + j, me, (*chip, c), src=x_ref) for j, chip in enumerate(chips)]
        passed += [copy(4 + j, (*chip, c), sibling) for j, chip in enumerate(chips)]
        landed += [copy(1 + j, (*chip, c), me) for j, chip in enumerate(chips)]
        last += [copy(0, sibling, me)] + [copy(4 + j, (*chip, 1 - c), me) for j, chip in enumerate(chips)]

    def start():
        for cp in mine + first:
            cp.start()

    def forward():
        for arrived, onward in zip(landed, passed):
            arrived.wait_recv()
            onward.start()

    def finish():
        for cp in last:
            cp.wait_recv()
        for cp in first + passed:
            cp.wait_send()
        for cp in mine:
            cp.wait()

    return start, forward, finish


def _exchange_steps(pairs, send_sems, recv_sems, local_sems):
    x, y, c = lax.axis_index("x"), lax.axis_index("y"), lax.axis_index("c")
    my_id = 4 * x + 2 * y + c
    local, remote = [], []
    for a, (src, dst) in enumerate(pairs):
        local.append(pltpu.make_async_copy(src.at[my_id], dst.at[my_id], local_sems.at[a]))
        for k in range(1, N_DEV):
            px = 1 - x if k & 4 else x
            py = 1 - y if k & 2 else y
            pc = 1 - c if k & 1 else c
            remote.append(pltpu.make_async_remote_copy(
                src_ref=src.at[4 * px + 2 * py + pc], dst_ref=dst.at[my_id],
                send_sem=send_sems.at[N_SEM * a + k - 1], recv_sem=recv_sems.at[N_SEM * a + k - 1],
                device_id=(px, py, pc), device_id_type=MESH))

    def start():
        for cp in local + remote:
            cp.start()

    def finish():
        for cp in remote:
            cp.wait_recv()
        for cp in remote:
            cp.wait_send()
        for cp in local:
            cp.wait()

    return start, finish


L_ONE = 64
L_CK = 65
L_CQ = 68
L_LSE = 71
L_DELTA = 74


def _head_block(pair, half):
    y = pair if half == 0 else pltpu.roll(pair, 64, axis=1)
    return jnp.where(_lane(pair.shape) < 64, y, 0.0)


def _put3(blk, lane0, col):
    lane = _lane(blk.shape)
    hi = col.astype(BF16).astype(F32)
    mid = (col - hi).astype(BF16).astype(F32)
    lo = (col - hi - mid).astype(BF16).astype(F32)
    return jnp.where(lane == lane0, hi, jnp.where(lane == lane0 + 1, mid, jnp.where(lane == lane0 + 2, lo, blk)))


def _spread3(col, shape, lane0s):
    lane = _lane(shape)
    hi = col.astype(BF16).astype(F32)
    mid = (col - hi).astype(BF16).astype(F32)
    lo = (col - hi - mid).astype(BF16).astype(F32)

    def at(k):
        return functools.reduce(jnp.logical_or, [lane == ln + k for ln in lane0s])

    return jnp.where(at(0), hi, jnp.where(at(1), mid, jnp.where(at(2), lo, 0.0)))


def _put_ones(blk, lanes):
    lane = _lane(blk.shape)
    hit = functools.reduce(jnp.logical_or, [lane == ln for ln in lanes])
    return jnp.where(hit, 1.0, blk)


def _to_pairs(ref):
    out = []
    for j in range(4):
        a, b = ref[:, 2 * LANES * j:2 * LANES * j + LANES], ref[:, 2 * LANES * j + LANES:2 * LANES * (j + 1)]
        out.append(jnp.where(_lane(a.shape) < 64, a, pltpu.roll(b, 64, axis=1)))
    return jnp.concatenate(out, axis=1)


def _fox_fwd(q_aug, k_aug, v_aug, nb, s, bt, shards=()):
    t = q_aug.shape[0]
    nq = s // bt
    n_in, n_sh = 3, len(shards)

    def body(*refs):
        q_ref, k_ref, v_ref = refs[:n_in]
        o_ref, ql_ref = refs[n_in + n_sh:n_in + n_sh + 2]
        if shards:
            srcs, dsts = refs[n_in:n_in + n_sh], refs[n_in + n_sh + 2:n_in + 2 * n_sh + 2]
            start, forward, finish = _gather_steps(list(zip(srcs, dsts)), *refs[n_in + 2 * n_sh + 2:])
            step = (pl.program_id(0) * 4 + pl.program_id(1)) * nq + pl.program_id(2)
            pl.when(step == 0)(start)
            pl.when(step == nb * 3 * nq)(forward)
        i = pl.program_id(2)
        row = lax.broadcasted_iota(jnp.int32, (bt, bt), 0)
        col = lax.broadcasted_iota(jnp.int32, (bt, bt), 1)
        sls = [slice(LANES * hh, LANES * (hh + 1)) for hh in range(2)]
        qhs = [q_ref[:, sl] for sl in sls]

        def blk(kb_i, carry, diag):
            start = pl.multiple_of(kb_i * bt, bt)
            new = []
            for (m, acc), qh, sl in zip(carry, qhs, sls):
                sc = _nt(qh, k_ref[pl.ds(start, bt), sl])
                if diag:
                    sc = jnp.where(row >= col, sc, NEG_INF)
                m_new = jnp.maximum(m, jnp.max(sc, axis=1, keepdims=True))
                pr = jnp.exp2(sc - m_new).astype(BF16)
                acc = jnp.exp2(m - m_new) * acc + jnp.dot(pr, v_ref[pl.ds(start, bt), sl], preferred_element_type=F32)
                new.append((m_new, acc))
            return tuple(new)

        init = tuple((jnp.full((bt, 1), NEG_INF, F32), jnp.zeros((bt, LANES), F32)) for _ in range(2))
        carry = lax.fori_loop(0, i, lambda kb_i, c: blk(kb_i, c, False), init)
        outs = []
        for (m, acc), qh, sl in zip(blk(i, carry, True), qhs, sls):
            l = acc[:, L_ONE:L_ONE + 1]
            outs.append(acc * (1.0 / l))
            ql_ref[:, sl] = _put3(qh.astype(F32), L_LSE, -(m + jnp.log(l) * LOG2E)).astype(BF16)
        o_ref[...] = jnp.where(_lane((1, LANES)) < 64, outs[0], pltpu.roll(outs[1], 64, axis=1)).astype(BF16)
        if shards:
            pl.when(step == nb * 4 * nq - 1)(finish)

    in_specs = [pl.BlockSpec((bt, 2 * LANES), lambda b, j, i: (b * nq + i, j)),
                pl.BlockSpec((s, 2 * LANES), lambda b, j, i: (b, j)),
                pl.BlockSpec((s, 2 * LANES), lambda b, j, i: (b, j))]
    out_specs = [pl.BlockSpec((bt, LANES), lambda b, j, i: (b * nq + i, j)),
                 pl.BlockSpec((bt, 2 * LANES), lambda b, j, i: (b * nq + i, j))]
    out_shape = [jax.ShapeDtypeStruct((t, 512), BF16), jax.ShapeDtypeStruct((t, 8 * LANES), BF16)]
    args, scratch = [q_aug, k_aug, v_aug, *shards], []
    if shards:
        in_specs += [ANY] * n_sh
        out_specs += [ANY] * n_sh
        out_shape += [jax.ShapeDtypeStruct((N_DEV,) + sh.shape, sh.dtype) for sh in shards]
        scratch = [pltpu.SemaphoreType.DMA((N_SEM * n_sh,)), pltpu.SemaphoreType.DMA((N_SEM * n_sh,)),
                   pltpu.SemaphoreType.DMA((n_sh,))]
    return pl.pallas_call(
        body, name="fox_fwd", grid=(nb, 4, nq), in_specs=in_specs, out_specs=out_specs, out_shape=out_shape,
        scratch_shapes=scratch, compiler_params=_params(("arbitrary", "arbitrary", "arbitrary")),
    )(*args)


def _fox_bwd(ql_aug, k_aug, v_aug, do_aug, nb, s, bt, exch=()):
    t = ql_aug.shape[0]
    nk = s // bt
    n_in, n_out, n_ex = 4, 3, len(exch)

    def body(*refs):
        q_ref, do_ref, k_ref, v_ref = refs[:n_in]
        dq_ref, dk_ref, dv_ref = refs[n_in + n_ex:n_in + n_ex + n_out]
        if exch:
            srcs = refs[n_in:n_in + n_ex]
            dsts = refs[n_in + n_ex + n_out:n_in + 2 * n_ex + n_out]
            start, finish = _exchange_steps(list(zip(srcs, dsts)), *refs[n_in + 2 * n_ex + n_out:])
            step = (pl.program_id(0) * 4 + pl.program_id(1)) * nk + pl.program_id(2)
            pl.when(step == 0)(start)
        kb_i = pl.program_id(2)

        @pl.when(kb_i == 0)
        def _():
            dq_ref[...] = jnp.zeros_like(dq_ref)

        row = lax.broadcasted_iota(jnp.int32, (bt, bt), 0)
        col = lax.broadcasted_iota(jnp.int32, (bt, bt), 1)
        sls = [slice(LANES * hh, LANES * (hh + 1)) for hh in range(2)]
        khs, vhs = [k_ref[:, sl] for sl in sls], [v_ref[:, sl] for sl in sls]

        def blk(qi, carry, diag):
            start = pl.multiple_of(qi * bt, bt)
            new = []
            for (dk_a, dv_a), kh, vh, sl in zip(carry, khs, vhs, sls):
                qblk, doblk = q_ref[pl.ds(start, bt), sl], do_ref[pl.ds(start, bt), sl]
                st = _nt(kh, qblk)
                if diag:
                    pt = jnp.where(col >= row, jnp.exp2(jnp.where(col >= row, st, 0.0)), 0.0)
                else:
                    pt = jnp.exp2(st)
                dst = pt * _nt(vh, doblk)
                ptb, dstb = pt.astype(BF16), dst.astype(BF16)
                dv_a = dv_a + jnp.dot(ptb, doblk, preferred_element_type=F32)
                dk_a = dk_a + jnp.dot(dstb, qblk, preferred_element_type=F32)
                dq_ref[pl.ds(start, bt), sl] += _tn(dstb, kh)
                new.append((dk_a, dv_a))
            return tuple(new)

        zero = jnp.zeros((bt, LANES), F32)
        carry = blk(kb_i, ((zero, zero), (zero, zero)), True)
        carry = lax.fori_loop(kb_i + 1, nk, lambda qi, c: blk(qi, c, False), carry)
        for (dk_acc, dv_acc), sl in zip(carry, sls):
            dk_ref[:, sl] = dk_acc
            dv_ref[:, sl] = dv_acc
        if exch:
            pl.when(step == nb * 4 * nk - 1)(finish)

    scratch = []
    if exch:
        scratch = [pltpu.SemaphoreType.DMA((N_SEM * n_ex,)), pltpu.SemaphoreType.DMA((N_SEM * n_ex,)),
                   pltpu.SemaphoreType.DMA((n_ex,))]
    whole = pl.BlockSpec((s, 2 * LANES), lambda b, j, kb_i: (b, j))
    tile = pl.BlockSpec((bt, 2 * LANES), lambda b, j, kb_i: (b * nk + kb_i, j))
    shp = jax.ShapeDtypeStruct((t, 8 * LANES), F32)
    return pl.pallas_call(
        body, name="fox_bwd", grid=(nb, 4, nk),
        in_specs=[whole, whole, tile, tile] + [ANY] * n_ex,
        out_specs=[whole, tile, tile] + [ANY] * n_ex,
        out_shape=[shp, shp, shp] + [jax.ShapeDtypeStruct(e.shape, e.dtype) for e in exch],
        scratch_shapes=scratch, compiler_params=_params(("arbitrary", "arbitrary", "arbitrary")),
    )(ql_aug, do_aug, k_aug, v_aug, *exch)


FF_BLK = D_FF // N_DEV


def _mlp_fwd(x2, ma, mb, tgt, w_out, g2, w_up, w_down, tm):
    t = x2.shape[0]

    def body(x_ref, ma_ref, mb_ref, tg_ref, wo_ref, g2_ref, wu_ref, wd_ref,
             h_ref, hn_ref, hid_ref, dy_ref, dyb_ref, loss_ref):
        @pl.when(pl.program_id(0) == 0)
        def _():
            loss_ref[...] = jnp.zeros_like(loss_ref)

        h = (x_ref[...] + jnp.dot(ma_ref[...], wo_ref[0:512, :], preferred_element_type=F32)
             + jnp.dot(mb_ref[...], wo_ref[512:1024, :], preferred_element_type=F32))
        h_ref[...] = h
        r = lax.rsqrt(jnp.mean(h * h, axis=-1, keepdims=True) + EPS)
        hn = (h * r * g2_ref[...]).astype(BF16)
        hn_ref[...] = hn
        for d in range(N_DEV):
            u = jnp.maximum(jnp.dot(hn, wu_ref[d], preferred_element_type=F32), 0.0)
            hid_ref[:, FF_BLK * d:FF_BLK * (d + 1)] = (u * u).astype(BF16)
        y = h + jnp.dot(hid_ref[...], wd_ref[...], preferred_element_type=F32)
        err = y - tg_ref[...]
        dy = err * (1.0 / D_MODEL)
        dy_ref[...] = dy
        dyb_ref[...] = dy.astype(BF16)
        part =0.5 * jnp.sum(jnp.sum(err * err, axis=1, keepdims=True) * (1.0 / D_MODEL), axis=0, keepdims=True)
        loss_ref[...] += part

    def tile(w):
        return pl.BlockSpec((tm, w), lambda i: (i, 0))

    return pl.pallas_call(
        body, name="mlp_fwd", grid=(t // tm,),
        in_specs=[tile(D_MODEL), tile(512), tile(512), tile(D_MODEL), _const_spec((D_MODEL, D_MODEL)),
                  _const_spec((1, D_MODEL)), _const_spec((N_DEV, D_MODEL, FF_BLK)), _const_spec((D_FF, D_MODEL))],
        out_specs=[tile(D_MODEL), tile(D_MODEL), tile(D_FF), tile(D_MODEL), tile(D_MODEL),
                   pl.BlockSpec((8, LANES), lambda i: (0, 0))],
        out_shape=[jax.ShapeDtypeStruct((t, D_MODEL), F32), jax.ShapeDtypeStruct((t, D_MODEL), BF16),
                   jax.ShapeDtypeStruct((t, D_FF), BF16), jax.ShapeDtypeStruct((t, D_MODEL), F32),
                   jax.ShapeDtypeStruct((t, D_MODEL), BF16), jax.ShapeDtypeStruct((8, LANES), F32)],
        compiler_params=_params(("arbitrary",)),
    )(x2, ma, mb, tgt, w_out, g2, w_up, w_down)


def _mlp_bwd(dy, hid, h, ma, mb, w_down, w_up_t, w_out, g2, tm):
    t = dy.shape[0]

    def body(dy_ref, hid_ref, h_ref, ma_ref, mb_ref, wd_ref, wut_ref, wo_ref, g2_ref,
             du_ref, dh_ref, dhb_ref, dma_ref, dob_ref, dla_ref, gg_ref):
        @pl.when(pl.program_id(0) == 0)
        def _():
            gg_ref[...] = jnp.zeros_like(gg_ref)

        dy = dy_ref[...]
        d_hid = _nt(dy.astype(BF16), wd_ref[...])
        du = (d_hid * (2.0 * jnp.sqrt(hid_ref[...].astype(F32)))).astype(BF16)
        du_ref[...] = du
        d_hn = jnp.dot(du, wut_ref[...], preferred_element_type=F32)
        h = h_ref[...]
        r = lax.rsqrt(jnp.mean(h * h, axis=-1, keepdims=True) + EPS)
        hat = h * r
        gd = d_hn * g2_ref[...]
        dh = dy + r * (gd - hat * jnp.mean(gd * hat, axis=-1, keepdims=True))
        gg_ref[...] += jnp.sum(d_hn * hat, axis=0, keepdims=True)
        dh_ref[...] = dh
        dhb = dh.astype(BF16)
        dhb_ref[...] = dhb
        dm = _nt(dhb, wo_ref[...]).astype(BF16)
        dma, dmb = dm[:, 0:512], dm[:, 512:1024]
        dma_ref[...] = dma
        sel = (lax.shift_right_logical(lax.broadcasted_iota(jnp.int32, (512, LANES), 0), 6)
               == lax.broadcasted_iota(jnp.int32, (512, LANES), 1)).astype(BF16)
        dla_ref[...] = _split_dot(dma.astype(F32) * ma_ref[...].astype(F32), sel)
        dmb32 = dmb.astype(F32)
        dlb = _split_dot(dmb32 * mb_ref[...].astype(F32), sel)
        for hd in range(8):
            blk = _head_block(dmb32[:, LANES * (hd // 2):LANES * (hd // 2 + 1)], hd % 2)
            dob_ref[:, LANES * hd:LANES * (hd + 1)] = _put3(blk, L_DELTA, -dlb[:, hd:hd + 1]).astype(BF16)

    def tile(w):
        return pl.BlockSpec((tm, w), lambda i: (i, 0))

    return pl.pallas_call(
        body, name="mlp_bwd", grid=(t // tm,),
        in_specs=[tile(D_MODEL), tile(D_FF), tile(D_MODEL), tile(512), tile(512), _const_spec((D_FF, D_MODEL)),
                  _const_spec((D_FF, D_MODEL)), _const_spec((D_MODEL, D_MODEL)), _const_spec((1, D_MODEL))],
        out_specs=[tile(D_FF), tile(D_MODEL), tile(D_MODEL), tile(512), tile(8 * LANES), tile(LANES),
                   pl.BlockSpec((1, D_MODEL), lambda i: (0, 0))],
        out_shape=[jax.ShapeDtypeStruct((t, D_FF), BF16), jax.ShapeDtypeStruct((t, D_MODEL), F32),
                   jax.ShapeDtypeStruct((t, D_MODEL), BF16), jax.ShapeDtypeStruct((t, 512), BF16),
                   jax.ShapeDtypeStruct((t, 8 * LANES), BF16), jax.ShapeDtypeStruct((t, LANES), F32),
                   jax.ShapeDtypeStruct((1, D_MODEL), F32)],
        compiler_params=_params(("arbitrary",), VMEM_LIMIT_WIDE),
    )(dy, hid, h, ma, mb, w_down, w_up_t, w_out, g2)


def _wgrad(a, b, name, bm, bn, tk, out_dtype=F32, col_blocks=False, a2=None):
    t, m = a.shape
    n = b.shape[1]
    bm, bn = min(bm, m), min(bn, n)
    nk = t // tk

    def body(*refs):
        if a2 is None:
            a_ref, b_ref, o_ref, acc = refs
        else:
            a_ref, b_ref, a2_ref, o_ref, o2_ref, acc, acc2 = refs
        i, k = pl.program_id(0), pl.program_id(2)

        @pl.when(k == 0)
        def _():
            acc[...] = jnp.zeros_like(acc)

        acc[...] += _tn(a_ref[...], b_ref[...])

        @pl.when(k == nk - 1)
        def _():
            o_ref[...] = acc[...].astype(out_dtype)

        if a2 is not None:
            @pl.when((i == 0) & (k == 0))
            def _():
                acc2[...] = jnp.zeros_like(acc2)

            @pl.when(i == 0)
            def _():
                acc2[...] += _tn(a2_ref[...], b_ref[...])

            @pl.when((i == 0) & (k == nk - 1))
            def _():
                o2_ref[...] = acc2[...]

    if col_blocks:
        out_spec = pl.BlockSpec((None, bm, bn), lambda i, j, k: (j, i, 0))
        out_shape = jax.ShapeDtypeStruct((n // bn, m, bn), out_dtype)
    else:
        out_spec = pl.BlockSpec((bm, bn), lambda i, j, k: (i, j))
        out_shape = jax.ShapeDtypeStruct((m, n), out_dtype)
    in_specs = [pl.BlockSpec((tk, bm), lambda i, j, k: (k, i)), pl.BlockSpec((tk, bn), lambda i, j, k: (k, j))]
    out_specs, out_shapes, scratch, args = [out_spec], [out_shape], [pltpu.VMEM((bm, bn), F32)], [a, b]
    if a2 is not None:
        m2 = a2.shape[1]
        in_specs.append(pl.BlockSpec((tk, m2), lambda i, j, k: (k, 0)))
        out_specs.append(pl.BlockSpec((m2, n), lambda i, j, k: (0, 0)))
        out_shapes.append(jax.ShapeDtypeStruct((m2, n), F32))
        scratch.append(pltpu.VMEM((m2, n), F32))
        args.append(a2)
    out = pl.pallas_call(
        body, name=name, grid=(m // bm, n // bn, nk), in_specs=in_specs, out_specs=out_specs, out_shape=out_shapes,
        scratch_shapes=scratch, compiler_params=_params(("arbitrary", "arbitrary", "arbitrary")),
    )(*args)
    return out[0] if a2 is None else out


def _proj_bwd(raw, dqa, dkae, dvae, dqb, dkb, dvb, fl, bf_row, x2, dh, w_main_t, w_f_t, g1, gqa, gka, gqb, gkb, nb, s, tm):
    t = x2.shape[0]
    nt = s // tm

    def body(raw_ref, dqa_ref, dkae_ref, dvae_ref, dqb_ref, dkb_ref, dvb_ref, fl_ref, b_ref, x_ref, dh_ref,
             wmt_ref, wft_ref, g1_ref, gqa_ref, gka_ref, gqb_ref, gkb_ref,
             dx_ref, dp_ref, dfb_ref, ggqa_ref, ggka_ref, ggqb_ref, ggkb_ref, gg1_ref, gb_ref, carry, dlf_ref):
        @pl.when((pl.program_id(0) == 0) & (pl.program_id(1) == 0))
        def _():
            for r in (ggqa_ref, ggka_ref, ggqb_ref, ggkb_ref, gg1_ref, gb_ref):
                r[...] = jnp.zeros_like(r)

        @pl.when(pl.program_id(1) == 0)
        def _():
            carry[...] = jnp.zeros_like(carry)

        lane = _lane((tm, LANES))
        dc = jnp.zeros((tm, LANES), F32)
        for hd in range(8):
            col = (dqb_ref[:, LANES * hd + L_CQ:LANES * hd + L_CQ + 1] - dkb_ref[:, LANES * hd + L_CK:LANES * hd + L_CK + 1])
            dc = jnp.where(lane == hd, col, dc)
        dlf_ref[...] = _tri_dot(tm, True, dc) + carry[...]
        carry[...] = dlf_ref[pl.ds(0, 1), :]
        dfl = dlf_ref[...] * (1.0 / (1.0 + jnp.exp(fl_ref[...] + b_ref[...])))
        gb_ref[...] += jnp.sum(dfl, axis=0, keepdims=True)

        raw = raw_ref[...]
        d_qa, p_qa = _head_norm_bwd(raw[:, 0:512], gqa_ref[...], dqa_ref[...])
        d_ka, p_ka = _head_norm_bwd(raw[:, 512:640], gka_ref[...], _fold_kv(dkae_ref[...]))
        d_va = _fold_kv(dvae_ref[...])
        d_qb, p_qb = _head_norm_bwd(raw[:, 768:1280], gqb_ref[...], _to_pairs(dqb_ref) * SCALE)
        d_kb, p_kb = _head_norm_bwd(raw[:, 1280:1792], gkb_ref[...], _to_pairs(dkb_ref) * (1.0 / LOG2E))
        ggqa_ref[...] += jnp.sum(p_qa, axis=0, keepdims=True)
        ggka_ref[...] += jnp.sum(p_ka, axis=0, keepdims=True)
        ggqb_ref[...] += jnp.sum(p_qb, axis=0, keepdims=True)
        ggkb_ref[...] += jnp.sum(p_kb, axis=0, keepdims=True)
        dproj = jnp.concatenate([d_qa, d_ka, d_va, d_qb, d_kb, _to_pairs(dvb_ref)], axis=1).astype(BF16)
        dp_ref[...] = dproj
        dfb = dfl.astype(BF16)
        dfb_ref[...] = dfb
        d_xn = (jnp.dot(dproj, wmt_ref[...], preferred_element_type=F32)
                + jnp.dot(dfb, wft_ref[...], preferred_element_type=F32))
        x = x_ref[...]
        r = lax.rsqrt(jnp.mean(x * x, axis=-1, keepdims=True) + EPS)
        hat = x * r
        gd = d_xn * g1_ref[...]
        dx_ref[...] = dh_ref[...] + r * (gd - hat * jnp.mean(gd * hat, axis=-1, keepdims=True))
        gg1_ref[...] += jnp.sum(d_xn * hat, axis=0, keepdims=True)

    def tile(w):
        return pl.BlockSpec((tm, w), lambda b, i: (b * nt + (nt - 1 - i), 0))

    def acc(w):
        return pl.BlockSpec((1, w), lambda b, i: (0, 0))

    return pl.pallas_call(
        body, name="proj_bwd", grid=(nb, nt),
        in_specs=[tile(MAIN_W), tile(512), tile(512), tile(512), tile(8 * LANES), tile(8 * LANES), tile(8 * LANES), tile(LANES),
                  _const_spec((1, LANES)), tile(D_MODEL), tile(D_MODEL), _const_spec((MAIN_W, D_MODEL)),
                  _const_spec((LANES, D_MODEL)), _const_spec((1, D_MODEL)), _const_spec((1, 512)), _const_spec((1, 128)),
                  _const_spec((1, 512)), _const_spec((1, 512))],
        out_specs=[tile(D_MODEL), tile(MAIN_W), tile(LANES), acc(512), acc(128), acc(512), acc(512), acc(D_MODEL), acc(LANES)],
        out_shape=[jax.ShapeDtypeStruct((t, D_MODEL), F32), jax.ShapeDtypeStruct((t, MAIN_W), BF16),
                   jax.ShapeDtypeStruct((t, LANES), BF16), jax.ShapeDtypeStruct((1, 512), F32),
                   jax.ShapeDtypeStruct((1, 128), F32), jax.ShapeDtypeStruct((1, 512), F32),
                   jax.ShapeDtypeStruct((1, 512), F32), jax.ShapeDtypeStruct((1, D_MODEL), F32),
                   jax.ShapeDtypeStruct((1, LANES), F32)],
        scratch_shapes=[pltpu.VMEM((1, LANES), F32), pltpu.VMEM((tm, LANES), F32)],
        compiler_params=_params(("arbitrary", "arbitrary"), VMEM_LIMIT_WIDE),
    )(raw, dqa, dkae, dvae, dqb, dkb, dvb, fl, bf_row, x2, dh, w_main_t, w_f_t, g1, gqa, gka, gqb, gkb)


IN_PAD = 304


def _local_step(x, tgt, w_in_t, rest, g1, b_forget, qna, kna, sinks, qnb, knb, g2,
                tm=512, bt=512, btf=1024, tq=2048, wk=4096, distributed=False):
    nb, s, _ = x.shape
    t = nb * s
    x2, tgt2 = x.reshape(t, D_MODEL), tgt.reshape(t, D_MODEL)
    g1r, g2r = g1.reshape(1, D_MODEL), g2.reshape(1, D_MODEL)
    gqa, gka = jnp.tile(qna, 8).reshape(1, 512), jnp.tile(kna, 2).reshape(1, 128)
    gqb, gkb = jnp.tile(qnb, 8).reshape(1, 512), jnp.tile(knb, 8).reshape(1, 512)
    bf_row = jnp.pad(b_forget, (0, LANES - 8)).reshape(1, LANES)
    sink_row = jnp.pad(sinks, (0, LANES - 8)).reshape(1, LANES)
    w_main_t = w_in_t[0:MAIN_W]
    w_f_t = jnp.pad(w_in_t[MAIN_W:IN_W], ((0, LANES - 8), (0, 0)))

    xn, raw, fl, qa, kae, vae, q_aug, k_aug, v_aug = _norm_proj(x2, g1r, w_main_t, w_f_t, gqa, gka, gqb, gkb, bf_row, s, tm)
    ma, lse_a = _swa_fwd(qa, kae, vae, sink_row, nb, s, tq)
    if distributed:
        mb, ql_aug, w_out, w_up, w_down, w_up_t = _fox_fwd(q_aug, k_aug, v_aug, nb, s, btf, shards=rest)
    else:
        mb, ql_aug = _fox_fwd(q_aug, k_aug, v_aug, nb, s, btf)
        w_out, w_up, w_down, w_up_t = rest
    w_out, w_down = w_out.reshape(D_MODEL, D_MODEL), w_down.reshape(D_FF, D_MODEL)
    h, hn, hid, dy, dyb, loss_acc = _mlp_fwd(x2, ma, mb, tgt2, w_out, g2r, w_up, w_down, tm)

    du, dh, dhb, dma, do_aug, dla, gg2 = _mlp_bwd(dy, hid, h, ma, mb, w_down, w_up_t.reshape(D_FF, D_MODEL), w_out, g2r, tm)
    g_down = _wgrad(hid, dyb, "wgrad_down", 512, 1024, wk, BF16).reshape(N_DEV, 512, D_MODEL)
    g_up = _wgrad(hn, du, "wgrad_up", 1024, 512, wk, BF16, col_blocks=True)
    g_out = jnp.concatenate([_wgrad(ma, dhb, "wgrad_out_a", 512, 1024, wk, BF16),
                             _wgrad(mb, dhb, "wgrad_out_b", 512, 1024, wk, BF16)], axis=0).reshape(N_DEV, 128, D_MODEL)

    dqa, dkae, dvae, dsink = _swa_bwd(qa, kae, vae, dma, sink_row, lse_a, dla, nb, s, tq)
    fox = _fox_bwd(ql_aug, k_aug, v_aug, do_aug, nb, s, bt, exch=(g_out, g_up, g_down) if distributed else ())
    dqb, dkb, dvb = fox[:3]
    if distributed:
        g_out, g_up, g_down = fox[3:]
    grad_x, dproj, dfb, ggqa, ggka, ggqb, ggkb, gg1, gbf = _proj_bwd(
        raw, dqa, dkae, dvae, dqb, dkb, dvb, fl, bf_row, x2, dh, w_main_t, w_f_t, g1r, gqa, gka, gqb, gkb, nb, s, tm)
    g_main_t, g_gate_t = _wgrad(dproj, xn, "wgrad_in", 768, 1024, wk, a2=dfb)
    g_in_t = jnp.concatenate([g_main_t, g_gate_t[0:8]], axis=0)

    small = (gg1.reshape(D_MODEL), gbf[0, 0:8], ggqa.reshape(8, 64).sum(0), ggka.reshape(2, 64).sum(0),
             dsink.sum(0)[:, 0:2, 0].reshape(8), ggqb.reshape(8, 64).sum(0), ggkb.reshape(8, 64).sum(0),
             gg2.reshape(D_MODEL))
    return loss_acc[0, 0], grad_x.reshape(nb, s, D_MODEL), g_in_t, g_out, g_up, g_down, small


def _all_gather(shard):
    def body(x_ref, out_ref, send_sems, recv_sems, local_sem):
        start, forward, finish = _gather_steps([(x_ref, out_ref)], send_sems, recv_sems, local_sem)
        start()
        forward()
        finish()

    return pl.pallas_call(
        body, name="gather_w_in", out_shape=jax.ShapeDtypeStruct((N_DEV,) + shard.shape, shard.dtype),
        in_specs=[ANY], out_specs=ANY,
        scratch_shapes=[pltpu.SemaphoreType.DMA((N_SEM,)), pltpu.SemaphoreType.DMA((N_SEM,)), pltpu.SemaphoreType.DMA((1,))],
    )(shard)


def _exchange(*arrays):
    n_ex = len(arrays)

    def body(*refs):
        start, finish = _exchange_steps(list(zip(refs[:n_ex], refs[n_ex:2 * n_ex])), *refs[2 * n_ex:])
        start()
        finish()

    return pl.pallas_call(
        body, name="exchange_tail", out_shape=[jax.ShapeDtypeStruct(a.shape, a.dtype) for a in arrays],
        in_specs=[ANY] * n_ex, out_specs=[ANY] * n_ex,
        scratch_shapes=[pltpu.SemaphoreType.DMA((N_SEM * n_ex,)), pltpu.SemaphoreType.DMA((N_SEM * n_ex,)),
                        pltpu.SemaphoreType.DMA((n_ex,))],
    )(*arrays)


def _sum_adamw(recv, w, m, v, tr, name):
    _, r, n = recv.shape

    def body(r_ref, w_ref, m_ref, v_ref, g_ref, d_ref, nm_ref, nv_ref):
        g = r_ref[0].astype(F32)
        for s in range(1, N_DEV):
            g = g + r_ref[s].astype(F32)
        g_ref[...] = g
        nm = ADAM_B1 * m_ref[...] + (1.0 - ADAM_B1) * g
        nv = ADAM_B2 * v_ref[...] + (1.0 - ADAM_B2) * (g * g)
        m_hat = nm / (1.0 - ADAM_B1 ** ADAM_STEP)
        v_hat = nv / (1.0 - ADAM_B2 ** ADAM_STEP)
        d_ref[...] = -ADAM_LR * (m_hat / (jnp.sqrt(v_hat) + ADAM_EPS) + ADAM_WD * w_ref[...])
        nm_ref[...] = nm
        nv_ref[...] = nv

    tile = pl.BlockSpec((tr, n), lambda i: (i, 0))
    shp = jax.ShapeDtypeStruct((r, n), F32)
    return pl.pallas_call(
        body, name=name, grid=(r // tr,),
        in_specs=[pl.BlockSpec((N_DEV, tr, n), lambda i: (0, i, 0)), tile, tile, tile],
        out_specs=[tile, tile, tile, tile], out_shape=[shp, shp, shp, shp],
        compiler_params=_params(("arbitrary",)),
    )(recv, w, m, v)


def _small_rows(g1, bf, qna, kna, sk, qnb, knb, g2):
    row2 = jnp.concatenate([bf, qna, kna, sk, qnb, knb])
    return jnp.zeros((8, D_MODEL), F32).at[0].set(g1).at[1].set(g2).at[2, 0:row2.shape[0]].set(row2)


def _in_rows(w_in_s):
    return jnp.pad(w_in_s.T, ((0, IN_PAD - IN_SHARD), (0, 0)))


def kernel(x, attn_norm_g, w_in, b_forget, q_norm_a, k_norm_a, sink_logits, q_norm_b, k_norm_b, w_out, mlp_norm_g, w_up, w_down, loss_target, m_attn_norm_g, m_w_in, m_b_forget, m_q_norm_a, m_k_norm_a, m_sink_logits, m_q_norm_b, m_k_norm_b, m_w_out, m_mlp_norm_g, m_w_up, m_w_down, v_attn_norm_g, v_w_in, v_b_forget, v_q_norm_a, v_k_norm_a, v_sink_logits, v_q_norm_b, v_k_norm_b, v_w_out, v_mlp_norm_g, v_w_up, v_w_down):
    w_in_r = _in_rows(w_in)
    w_in_t = _all_gather(w_in_r.astype(BF16))[:, 0:IN_SHARD].reshape(IN_W, D_MODEL)
    w_up_b = w_up.astype(BF16)
    rest = (w_out.astype(BF16), w_up_b, w_down.astype(BF16), w_up_b.T)

    loss_part, grad_x, g_in_t, r_out, r_up, r_down, small = _local_step(
        x, loss_target, w_in_t, rest, attn_norm_g, b_forget, q_norm_a, k_norm_a, sink_logits, q_norm_b, k_norm_b, mlp_norm_g,
        distributed=True)

    g_in_blocks = jnp.pad(g_in_t.reshape(N_DEV, IN_SHARD, D_MODEL), ((0, 0), (0, IN_PAD - IN_SHARD), (0, 0))).astype(BF16)
    small_blocks = jnp.broadcast_to(_small_rows(*small).at[3, 0].set(loss_part), (N_DEV, 8, D_MODEL))
    r_in, r_small = _exchange(g_in_blocks, small_blocks)

    small_w = _small_rows(attn_norm_g, b_forget, q_norm_a, k_norm_a, sink_logits, q_norm_b, k_norm_b, mlp_norm_g)
    small_m = _small_rows(m_attn_norm_g, m_b_forget, m_q_norm_a, m_k_norm_a, m_sink_logits, m_q_norm_b, m_k_norm_b, m_mlp_norm_g)
    small_v = _small_rows(v_attn_norm_g, v_b_forget, v_q_norm_a, v_k_norm_a, v_sink_logits, v_q_norm_b, v_k_norm_b, v_mlp_norm_g)
    o_in = [a[0:IN_SHARD].T for a in _sum_adamw(r_in, w_in_r, _in_rows(m_w_in), _in_rows(v_w_in), IN_PAD, "adamw_in")]
    o_out = _sum_adamw(r_out, w_out, m_w_out, v_w_out, 128, "adamw_out")
    o_up = _sum_adamw(r_up, w_up, m_w_up, v_w_up, 256, "adamw_up")
    o_down = _sum_adamw(r_down, w_down, m_w_down, v_w_down, 128, "adamw_down")
    o_small = _sum_adamw(r_small, small_w, small_m, small_v, 8, "adamw_small")

    def leaves(i):
        row2 = o_small[i][2]
        return (o_small[i][0], o_in[i], row2[0:8], row2[8:72], row2[72:136], row2[136:144], row2[144:208], row2[208:272],
                o_out[i], o_small[i][1], o_up[i], o_down[i])

    return (o_small[0][3, 0], grad_x, *leaves(0), *leaves(1), *leaves(2), *leaves(3))
```

```python
import functools

import jax
import jax.numpy as jnp
from jax import lax
from jax.experimental import pallas as pl
from jax.experimental.pallas import tpu as pltpu

F32 = jnp.float32
BF16 = jnp.bfloat16

D_MODEL = 1024
HEAD_DIM = 64
N_DEV = 8
D_FF = 4096
MAIN_W = 2304
IN_W = 2312
IN_SHARD = 289
WINDOW = 128
EPS = 1e-6
SCALE = 0.125
LOG2E = 1.4426950408889634
LANES = 128
NEG_INF = float("-inf")

ADAM_LR = 0.001
ADAM_B1 = 0.9
ADAM_B2 = 0.999
ADAM_EPS = 1e-08
ADAM_WD = 0.01
ADAM_STEP = 10

VMEM_LIMIT = 56 * 1024 * 1024
VMEM_LIMIT_WIDE = 62 * 1024 * 1024


def _params(sem, vmem=VMEM_LIMIT):
    return pltpu.CompilerParams(dimension_semantics=sem, vmem_limit_bytes=vmem)


def _const_spec(shape):
    nd = len(shape)
    return pl.BlockSpec(shape, lambda *_: (0,) * nd, pipeline_mode=pl.Buffered(1))


def _lane(shape):
    return lax.broadcasted_iota(jnp.int32, shape, len(shape) - 1)


def _split_dot(v, mat):
    hi = v.astype(BF16)
    lo = (v - hi.astype(F32)).astype(BF16)
    return (jnp.dot(hi, mat, preferred_element_type=F32) + jnp.dot(lo, mat, preferred_element_type=F32))


def _head_ones(n):
    r = lax.shift_right_logical(lax.broadcasted_iota(jnp.int32, (n, n), 0), 6)
    c = lax.shift_right_logical(lax.broadcasted_iota(jnp.int32, (n, n), 1), 6)
    return (r == c).astype(BF16)


def _head_sum(v):
    w = v.shape[1]
    vb = v.astype(BF16)
    if w <= 256:
        return jnp.dot(vb, _head_ones(w), preferred_element_type=F32)
    ones = _head_ones(256)
    return jnp.concatenate([jnp.dot(vb[:, s:s + 256], ones, preferred_element_type=F32) for s in range(0, w, 256)], axis=1)


def _head_norm(seg, gain):
    rs = lax.rsqrt(_head_sum(seg * seg) * (1.0 / HEAD_DIM) + EPS)
    return seg * rs * gain


def _head_norm_bwd(seg, gain, d_out):
    rs = lax.rsqrt(_head_sum(seg * seg) * (1.0 / HEAD_DIM) + EPS)
    hat = seg * rs
    gd = d_out * gain
    d_seg = rs * (gd - hat * (_head_sum(gd * hat) * (1.0 / HEAD_DIM)))
    return d_seg, d_out * hat


def _expand_kv(v):
    r = pltpu.roll(v, 64, axis=1)
    lo = _lane(v.shape) < 64
    return jnp.concatenate([jnp.where(lo, v, r), jnp.where(lo, r, v)], axis=1)


def _fold_kv(e4):
    t0 = e4[:, 0:128] + e4[:, 128:256]
    t1 = e4[:, 256:384] + e4[:, 384:512]
    t0 = t0 + pltpu.roll(t0, 64, axis=1)
    t1 = t1 + pltpu.roll(t1, 64, axis=1)
    return jnp.where(_lane(t0.shape) < 64, t0, t1)


def _pick_lane(blk, idx):
    return jnp.sum(jnp.where(_lane(blk.shape) == idx, blk, 0.0), axis=1, keepdims=True)


def _nt(a, b):
    return lax.dot_general(a, b, (((1,), (1,)), ((), ())), preferred_element_type=F32)


def _tn(a, b):
    return lax.dot_general(a, b, (((0,), (0,)), ((), ())), preferred_element_type=F32)


def _norm_proj(x2, g1, w_main_t, w_f_t, gqa, gka, gqb, gkb, bf_row, s, tm):
    t = x2.shape[0]
    nt = s // tm

    def body(x_ref, g1_ref, wm_ref, wf_ref, gqa_ref, gka_ref, gqb_ref, gkb_ref, b_ref,
             xn_ref, raw_ref, fl_ref, qa_ref, kae_ref, vae_ref, qo_ref, ko_ref, vo_ref, carry, c_ref):
        @pl.when(lax.rem(pl.program_id(0), nt) == 0)
        def _():
            carry[...] = jnp.zeros_like(carry)

        x = x_ref[...]
        r = lax.rsqrt(jnp.mean(x * x, axis=-1, keepdims=True) + EPS)
        xn = (x * r * g1_ref[...]).astype(BF16)
        xn_ref[...] = xn
        proj = _nt(xn, wm_ref[...])
        raw_ref[...] = proj
        fl = _nt(xn, wf_ref[...])
        fl_ref[...] = fl
        qa_ref[...] = _head_norm(proj[:, 0:512], gqa_ref[...]).astype(BF16)
        kae_ref[...] = _expand_kv(_head_norm(proj[:, 512:640], gka_ref[...])).astype(BF16)
        vae_ref[...] = _expand_kv(proj[:, 640:768]).astype(BF16)

        z = fl + b_ref[...]
        e = jnp.exp(-jnp.abs(z))
        u = 1.0 + e
        log1p = jnp.where(u == 1.0, e, jnp.log(u) * (e / (u - 1.0)))
        lf = jnp.minimum(z, 0.0) - log1p
        for r0 in range(0, tm, 256):
            c_ref[r0:r0 + 256, :] = _tri_dot(256, False, lf[r0:r0 + 256]) + carry[...]
            carry[...] = c_ref[pl.ds(r0 + 255, 1), :]
        c2 = c_ref[...] * LOG2E
        qb = _head_norm(proj[:, 768:1280], gqb_ref[...]) * (SCALE * LOG2E)
        kb = _head_norm(proj[:, 1280:1792], gkb_ref[...])
        lane = _lane((tm, LANES))
        for h in range(8):
            j, half = h // 2, h % 2
            pair, blk = slice(LANES * j, LANES * (j + 1)), slice(LANES * h, LANES * (h + 1))
            feat = _spread3(c2[:, h:h + 1], (tm, LANES), (L_CK, L_CQ))
            q = _put_ones(_head_block(qb[:, pair], half), (L_CK, L_CK + 1, L_CK + 2))
            qo_ref[:, blk] = jnp.where((lane >= L_CQ) & (lane < L_CQ + 3), feat, q).astype(BF16)
            k = _put_ones(_head_block(kb[:, pair], half), tuple(range(L_CQ, L_CQ + 6)))
            ko_ref[:, blk] = jnp.where((lane >= L_CK) & (lane < L_CK + 3), -feat, k).astype(BF16)
            v = _head_block(proj[:, 1792 + LANES * j:1792 + LANES * (j + 1)], half)
            vo_ref[:, blk] = _put_ones(v, (L_ONE, L_DELTA, L_DELTA + 1, L_DELTA + 2)).astype(BF16)

    def tile(w):
        return pl.BlockSpec((tm, w), lambda i: (i, 0))

    aug = jax.ShapeDtypeStruct((t, 8 * LANES), BF16)
    return pl.pallas_call(
        body, name="norm_proj", grid=(t // tm,),
        in_specs=[tile(D_MODEL), _const_spec((1, D_MODEL)), _const_spec((MAIN_W, D_MODEL)), _const_spec((LANES, D_MODEL)),
                  _const_spec((1, 512)), _const_spec((1, 128)), _const_spec((1, 512)), _const_spec((1, 512)),
                  _const_spec((1, LANES))],
        out_specs=[tile(D_MODEL), tile(MAIN_W), tile(LANES), tile(512), tile(256), tile(256)] + [tile(8 * LANES)] * 3,
        out_shape=[jax.ShapeDtypeStruct((t, D_MODEL), BF16), jax.ShapeDtypeStruct((t, MAIN_W), F32),
                   jax.ShapeDtypeStruct((t, LANES), F32), jax.ShapeDtypeStruct((t, 512), BF16),
                   jax.ShapeDtypeStruct((t, 256), BF16), jax.ShapeDtypeStruct((t, 256), BF16), aug, aug, aug],
        scratch_shapes=[pltpu.VMEM((1, LANES), F32), pltpu.VMEM((tm, LANES), F32)],
        compiler_params=_params(("arbitrary",)),
    )(x2, g1, w_main_t, w_f_t, gqa, gka, gqb, gkb, bf_row)


def _tri_dot(n, upper, v):
    r = lax.broadcasted_iota(jnp.int32, (n, n), 0)
    c = lax.broadcasted_iota(jnp.int32, (n, n), 1)
    tri = ((c >= r) if upper else (c <= r)).astype(BF16)
    hi = v.astype(BF16)
    mid = (v - hi.astype(F32)).astype(BF16)
    lo = (v - hi.astype(F32) - mid.astype(F32)).astype(BF16)
    return (jnp.dot(tri, hi, preferred_element_type=F32) + jnp.dot(tri, mid, preferred_element_type=F32)
            + jnp.dot(tri, lo, preferred_element_type=F32))


def _slope(p, hh):
    out = jnp.float32(2.0 ** -(2 * 3 + hh + 1))
    for pp in (2, 1, 0):
        out = jnp.where(p == pp, jnp.float32(2.0 ** -(2 * pp + hh + 1)), out)
    return out


def _swa_windows(ref, i, tq):
    nsub = tq // WINDOW
    cur = ref[pl.ds(pl.multiple_of(i * tq, tq), tq), :].reshape(nsub, WINDOW, LANES)
    first = ref[pl.ds(pl.multiple_of(jnp.maximum(i * tq - WINDOW, 0), WINDOW), WINDOW), :].reshape(1, WINDOW, LANES)
    return jnp.concatenate([jnp.concatenate([first, cur[0:nsub - 1]], axis=0), cur], axis=1)


def _both_heads(x3, lo):
    zero = jnp.zeros_like(x3)
    return jnp.concatenate([jnp.where(lo, x3, zero), jnp.where(lo, zero, x3)], axis=0)


def _swa_head_consts(sink_ref, p, i, nsub):
    bidx = lax.broadcasted_iota(jnp.int32, (2 * nsub, 1, 1), 0)
    is_a = bidx < nsub
    slope = jnp.where(is_a, _slope(p, 0), _slope(p, 1))
    sinks = sink_ref[...]
    sink = jnp.where(is_a, _pick_lane(sinks, 2 * p).reshape(1, 1, 1), _pick_lane(sinks, 2 * p + 1).reshape(1, 1, 1))
    first = (i == 0) & ((bidx == 0) | (bidx == nsub))
    return slope, sink, first


def _swa_fwd(qa, kae, vae, sink_row, nb, s, tq):
    t = qa.shape[0]
    nq = s // tq
    nsub = tq // WINDOW

    def body(q_ref, k_ref, v_ref, sink_ref, o_ref, lse_ref):
        p, i = pl.program_id(1), pl.program_id(2)
        lo = _lane((1, 1, LANES)) < 64
        kk, vv = _swa_windows(k_ref, i, tq), _swa_windows(v_ref, i, tq)
        qs = (q_ref[...].astype(F32) * SCALE).astype(BF16).reshape(nsub, WINDOW, LANES)
        q8 = _both_heads(qs, lo)
        s8 = jnp.einsum("bqd,bkd->bqk", q8, jnp.concatenate([kk, kk], axis=0), preferred_element_type=F32)
        row = lax.broadcasted_iota(jnp.int32, (1, WINDOW, 2 * WINDOW), 1)
        col = lax.broadcasted_iota(jnp.int32, (1, WINDOW, 2 * WINDOW), 2)
        dist = row + WINDOW - col
        slope, sink, first = _swa_head_consts(sink_ref, p, i, nsub)
        valid = (dist >= 0) & (dist < WINDOW) & ((col >= WINDOW) | jnp.logical_not(first))
        s8 = jnp.where(valid, s8 - slope * dist.astype(F32), NEG_INF)
        m = jnp.maximum(jnp.max(s8, axis=2, keepdims=True), sink)
        e = jnp.exp(s8 - m)
        den = jnp.sum(e, axis=2, keepdims=True) + jnp.exp(sink - m)
        pr = (e * (1.0 / den)).astype(BF16)
        o8 = jnp.einsum("bqk,bkd->bqd", pr, jnp.concatenate([vv, vv], axis=0), preferred_element_type=F32)
        lse8 = m + jnp.log(den)
        o_ref[...] = jnp.where(lo, o8[0:nsub], o8[nsub:]).astype(BF16).reshape(tq, LANES)
        lse_ref[...] = jnp.where(lo, lse8[0:nsub], lse8[nsub:]).reshape(tq, LANES)

    return pl.pallas_call(
        body, name="swa_fwd", grid=(nb, 4, nq),
        in_specs=[pl.BlockSpec((tq, LANES), lambda b, p, i: (b * nq + i, p)),
                  pl.BlockSpec((s, LANES), lambda b, p, i: (b, lax.shift_right_logical(p, 1))),
                  pl.BlockSpec((s, LANES), lambda b, p, i: (b, lax.shift_right_logical(p, 1))),
                  pl.BlockSpec((1, LANES), lambda b, p, i: (0, 0))],
        out_specs=[pl.BlockSpec((tq, LANES), lambda b, p, i: (b * nq + i, p)),
                   pl.BlockSpec((None, tq, LANES), lambda b, p, i: (p, b * nq + i, 0))],
        out_shape=[jax.ShapeDtypeStruct((t, 512), BF16), jax.ShapeDtypeStruct((4, t, LANES), F32)],
        compiler_params=_params(("arbitrary", "arbitrary", "arbitrary")),
    )(qa, kae, vae, sink_row)


def _swa_bwd(qa, kae, vae, do_a, sink_row, lse, delta, nb, s, tq):
    t = qa.shape[0]
    nq = s // tq
    nsub = tq // WINDOW

    def body(q_ref, do_ref, k_ref, v_ref, sink_ref, lse_ref, dl_ref, dq_ref, dk_ref, dv_ref, ds_ref):
        p, i = pl.program_id(1), pl.program_id(2)

        @pl.when(i == 0)
        def _():
            ds_ref[...] = jnp.zeros_like(ds_ref)

        lo = _lane((1, 1, LANES)) < 64
        kk, vv = _swa_windows(k_ref, i, tq), _swa_windows(v_ref, i, tq)
        kks = (kk.astype(F32) * SCALE).astype(BF16)
        k8, v8 = jnp.concatenate([kks, kks], axis=0), jnp.concatenate([vv, vv], axis=0)
        q8 = _both_heads(q_ref[...].reshape(nsub, WINDOW, LANES), lo)
        do8 = _both_heads(do_ref[...].reshape(nsub, WINDOW, LANES), lo)
        cur = pl.multiple_of(i * tq, tq)
        sub = lax.broadcasted_iota(jnp.int32, (WINDOW, WINDOW), 0)
        lse_t = [lse_ref[u * WINDOW:(u + 1) * WINDOW, :].T for u in range(nsub)]
        dl_t = [dl_ref[u * WINDOW:(u + 1) * WINDOW, :].T for u in range(nsub)]
        lse8 = jnp.concatenate([t_[64 * hh:64 * hh + 1, :].reshape(1, 1, WINDOW) for hh in range(2) for t_ in lse_t], axis=0)
        dl8 = jnp.concatenate([jnp.sum(jnp.where(sub == 2 * p + hh, t_, 0.0), axis=0, keepdims=True).reshape(1, 1, WINDOW)
                               for hh in range(2) for t_ in dl_t], axis=0)
        row = lax.broadcasted_iota(jnp.int32, (1, 2 * WINDOW, WINDOW), 1)
        col = lax.broadcasted_iota(jnp.int32, (1, 2 * WINDOW, WINDOW), 2)
        dist = col + WINDOW - row
        slope, sink, first = _swa_head_consts(sink_ref, p, i, nsub)
        valid = (dist >= 0) & (dist < WINDOW) & ((row >= WINDOW) | jnp.logical_not(first))
        st = jnp.einsum("bkd,bqd->bkq", k8, q8, preferred_element_type=F32) - slope * dist.astype(F32) - lse8
        pt = jnp.where(valid, jnp.exp(jnp.where(valid, st, 0.0)), 0.0)
        dpt = jnp.einsum("bkd,bqd->bkq", v8, do8, preferred_element_type=F32)
        dst = pt * (dpt - dl8)
        ptb, dstb = pt.astype(BF16), dst.astype(BF16)
        dv8 = jnp.einsum("bkq,bqd->bkd", ptb, do8, preferred_element_type=F32)
        dk8 = jnp.einsum("bkq,bqd->bkd", dstb, q8, preferred_element_type=F32) * SCALE
        dq8 = jnp.einsum("bkq,bkd->bqd", dstb, k8, preferred_element_type=F32)
        dq_ref[...] = jnp.where(lo, dq8[0:nsub], dq8[nsub:]).reshape(tq, LANES)

        psd = jnp.exp(sink - lse8) * dl8
        row_h = lax.broadcasted_iota(jnp.int32, (8, LANES), 0)
        for hh in range(2):
            tot = jnp.sum(jnp.sum(psd[hh * nsub:(hh + 1) * nsub], axis=2, keepdims=True), axis=0, keepdims=True)
            ds_ref[...] += jnp.where(row_h == hh, -tot.reshape(1, 1), 0.0)

        prev = pl.multiple_of(jnp.maximum(i * tq - WINDOW, 0), WINDOW)
        for g8, g_ref in ((dk8, dk_ref), (dv8, dv_ref)):
            g4 = g8[0:nsub] + g8[nsub:]
            own, before = g4[:, WINDOW:, :], g4[:, 0:WINDOW, :]
            shifted = jnp.concatenate([before[1:nsub], jnp.zeros((1, WINDOW, LANES), F32)], axis=0)
            g_ref[pl.ds(cur, tq), :] = (own + shifted).reshape(tq, LANES)
            g_ref[pl.ds(prev, WINDOW), :] += before[0]

    return pl.pallas_call(
        body, name="swa_bwd", grid=(nb, 4, nq),
        in_specs=[pl.BlockSpec((tq, LANES), lambda b, p, i: (b * nq + i, p)),
                  pl.BlockSpec((tq, LANES), lambda b, p, i: (b * nq + i, p)),
                  pl.BlockSpec((s, LANES), lambda b, p, i: (b, lax.shift_right_logical(p, 1))),
                  pl.BlockSpec((s, LANES), lambda b, p, i: (b, lax.shift_right_logical(p, 1))),
                  pl.BlockSpec((1, LANES), lambda b, p, i: (0, 0)),
                  pl.BlockSpec((None, tq, LANES), lambda b, p, i: (p, b * nq + i, 0)),
                  pl.BlockSpec((tq, LANES), lambda b, p, i: (b * nq + i, 0))],
        out_specs=[pl.BlockSpec((tq, LANES), lambda b, p, i: (b * nq + i, p)),
                   pl.BlockSpec((s, LANES), lambda b, p, i: (b, p)),
                   pl.BlockSpec((s, LANES), lambda b, p, i: (b, p)),
                   pl.BlockSpec((None, None, 8, LANES), lambda b, p, i: (b, p, 0, 0))],
        out_shape=[jax.ShapeDtypeStruct((t, 512), F32), jax.ShapeDtypeStruct((t, 512), F32),
                   jax.ShapeDtypeStruct((t, 512), F32), jax.ShapeDtypeStruct((nb, 4, 8, LANES), F32)],
        compiler_params=_params(("arbitrary", "arbitrary", "arbitrary")),
    )(qa, do_a, kae, vae, sink_row, lse, delta)


MESH = pl.DeviceIdType.MESH
ANY = pl.BlockSpec(memory_space=pl.ANY)
N_SEM = 7


def _gather_steps(pairs, send_sems, recv_sems, local_sems):
    x, y, c = lax.axis_index("x"), lax.axis_index("y"), lax.axis_index("c")
    me, sibling = (x, y, c), (x, y, 1 - c)
    chips = [(1 - x, y), (x, 1 - y), (1 - x, 1 - y)]
    mine, first, passed, landed, last = [], [], [], [], []
    for a, (x_ref, out_ref) in enumerate(pairs):
        def slot(px, py, pc, out_ref=out_ref):
            return out_ref.at[4 * px + 2 * py + pc]

        def copy(k, block, to, src=None, a=a, slot=slot):
            return pltpu.make_async_remote_copy(
                src_ref=slot(*block) if src is None else src, dst_ref=slot(*block),
                send_sem=send_sems.at[N_SEM * a + k], recv_sem=recv_sems.at[N_SEM * a + k], device_id=to, device_id_type=MESH)

        mine.append(pltpu.make_async_copy(x_ref, slot(*me), local_sems.at[a]))
        first += [copy(0, me, sibling, src=x_ref)] + [copy(1 + j, me, (*chip, c), src=x_ref) for j, chip in enumerate(chips)]
        passed += [copy(4 + j, (*chip, c), sibling) for j, chip in enumerate(chips)]
        landed += [copy(1 + j, (*chip, c), me) for j, chip in enumerate(chips)]
        last += [copy(0, sibling, me)] + [copy(4 + j, (*chip, 1 - c), me) for j, chip in enumerate(chips)]

    def start():
        for cp in mine + first:
            cp.start()

    def forward():
        for arrived, onward in zip(landed, passed):
            arrived.wait_recv()
            onward.start()

    def finish():
        for cp in last:
            cp.wait_recv()
        for cp in first + passed:
            cp.wait_send()
        for cp in mine:
            cp.wait()

    return start, forward, finish


def _exchange_steps(pairs, send_sems, recv_sems, local_sems):
    x, y, c = lax.axis_index("x"), lax.axis_index("y"), lax.axis_index("c")
    my_id = 4 * x + 2 * y + c
    local, remote = [], []
    for a, (src, dst) in enumerate(pairs):
        local.append(pltpu.make_async_copy(src.at[my_id], dst.at[my_id], local_sems.at[a]))
        for k in range(1, N_DEV):
            px = 1 - x if k & 4 else x
            py = 1 - y if k & 2 else y
            pc = 1 - c if k & 1 else c
            remote.append(pltpu.make_async_remote_copy(
                src_ref=src.at[4 * px + 2 * py + pc], dst_ref=dst.at[my_id],
                send_sem=send_sems.at[N_SEM * a + k - 1], recv_sem=recv_sems.at[N_SEM * a + k - 1],
                device_id=(px, py, pc), device_id_type=MESH))

    def start():
        for cp in local + remote:
            cp.start()

    def finish():
        for cp in remote:
            cp.wait_recv()
        for cp in remote:
            cp.wait_send()
        for cp in local:
            cp.wait()

    return start, finish


L_ONE = 64
L_CK = 65
L_CQ = 68
L_LSE = 71
L_DELTA = 74


def _head_block(pair, half):
    y = pair if half == 0 else pltpu.roll(pair, 64, axis=1)
    return jnp.where(_lane(pair.shape) < 64, y, 0.0)


def _put3(blk, lane0, col):
    lane = _lane(blk.shape)
    hi = col.astype(BF16).astype(F32)
    mid = (col - hi).astype(BF16).astype(F32)
    lo = (col - hi - mid).astype(BF16).astype(F32)
    return jnp.where(lane == lane0, hi, jnp.where(lane == lane0 + 1, mid, jnp.where(lane == lane0 + 2, lo, blk)))


def _spread3(col, shape, lane0s):
    lane = _lane(shape)
    hi = col.astype(BF16).astype(F32)
    mid = (col - hi).astype(BF16).astype(F32)
    lo = (col - hi - mid).astype(BF16).astype(F32)

    def at(k):
        return functools.reduce(jnp.logical_or, [lane == ln + k for ln in lane0s])

    return jnp.where(at(0), hi, jnp.where(at(1), mid, jnp.where(at(2), lo, 0.0)))


def _put_ones(blk, lanes):
    lane = _lane(blk.shape)
    hit = functools.reduce(jnp.logical_or, [lane == ln for ln in lanes])
    return jnp.where(hit, 1.0, blk)


def _to_pairs(ref):
    out = []
    for j in range(4):
        a, b = ref[:, 2 * LANES * j:2 * LANES * j + LANES], ref[:, 2 * LANES * j + LANES:2 * LANES * (j + 1)]
        out.append(jnp.where(_lane(a.shape) < 64, a, pltpu.roll(b, 64, axis=1)))
    return jnp.concatenate(out, axis=1)


def _fox_fwd(q_aug, k_aug, v_aug, nb, s, bt, shards=()):
    t = q_aug.shape[0]
    nq = s // bt
    n_in, n_sh = 3, len(shards)

    def body(*refs):
        q_ref, k_ref, v_ref = refs[:n_in]
        o_ref, ql_ref = refs[n_in + n_sh:n_in + n_sh + 2]
        if shards:
            srcs, dsts = refs[n_in:n_in + n_sh], refs[n_in + n_sh + 2:n_in + 2 * n_sh + 2]
            start, forward, finish = _gather_steps(list(zip(srcs, dsts)), *refs[n_in + 2 * n_sh + 2:])
            step = (pl.program_id(0) * 4 + pl.program_id(1)) * nq + pl.program_id(2)
            pl.when(step == 0)(start)
            pl.when(step == nb * 3 * nq)(forward)
        i = pl.program_id(2)
        row = lax.broadcasted_iota(jnp.int32, (bt, bt), 0)
        col = lax.broadcasted_iota(jnp.int32, (bt, bt), 1)
        sls = [slice(LANES * hh, LANES * (hh + 1)) for hh in range(2)]
        qhs = [q_ref[:, sl] for sl in sls]

        def blk(kb_i, carry, diag):
            start = pl.multiple_of(kb_i * bt, bt)
            new = []
            for (m, acc), qh, sl in zip(carry, qhs, sls):
                sc = _nt(qh, k_ref[pl.ds(start, bt), sl])
                if diag:
                    sc = jnp.where(row >= col, sc, NEG_INF)
                m_new = jnp.maximum(m, jnp.max(sc, axis=1, keepdims=True))
                pr = jnp.exp2(sc - m_new).astype(BF16)
                acc = jnp.exp2(m - m_new) * acc + jnp.dot(pr, v_ref[pl.ds(start, bt), sl], preferred_element_type=F32)
                new.append((m_new, acc))
            return tuple(new)

        init = tuple((jnp.full((bt, 1), NEG_INF, F32), jnp.zeros((bt, LANES), F32)) for _ in range(2))
        carry = lax.fori_loop(0, i, lambda kb_i, c: blk(kb_i, c, False), init)
        outs = []
        for (m, acc), qh, sl in zip(blk(i, carry, True), qhs, sls):
            l = acc[:, L_ONE:L_ONE + 1]
            outs.append(acc * (1.0 / l))
            ql_ref[:, sl] = _put3(qh.astype(F32), L_LSE, -(m + jnp.log(l) * LOG2E)).astype(BF16)
        o_ref[...] = jnp.where(_lane((1, LANES)) < 64, outs[0], pltpu.roll(outs[1], 64, axis=1)).astype(BF16)
        if shards:
            pl.when(step == nb * 4 * nq - 1)(finish)

    in_specs = [pl.BlockSpec((bt, 2 * LANES), lambda b, j, i: (b * nq + i, j)),
                pl.BlockSpec((s, 2 * LANES), lambda b, j, i: (b, j)),
                pl.BlockSpec((s, 2 * LANES), lambda b, j, i: (b, j))]
    out_specs = [pl.BlockSpec((bt, LANES), lambda b, j, i: (b * nq + i, j)),
                 pl.BlockSpec((bt, 2 * LANES), lambda b, j, i: (b * nq + i, j))]
    out_shape = [jax.ShapeDtypeStruct((t, 512), BF16), jax.ShapeDtypeStruct((t, 8 * LANES), BF16)]
    args, scratch = [q_aug, k_aug, v_aug, *shards], []
    if shards:
        in_specs += [ANY] * n_sh
        out_specs += [ANY] * n_sh
        out_shape += [jax.ShapeDtypeStruct((N_DEV,) + sh.shape, sh.dtype) for sh in shards]
        scratch = [pltpu.SemaphoreType.DMA((N_SEM * n_sh,)), pltpu.SemaphoreType.DMA((N_SEM * n_sh,)),
                   pltpu.SemaphoreType.DMA((n_sh,))]
    return pl.pallas_call(
        body, name="fox_fwd", grid=(nb, 4, nq), in_specs=in_specs, out_specs=out_specs, out_shape=out_shape,
        scratch_shapes=scratch, compiler_params=_params(("arbitrary", "arbitrary", "arbitrary")),
    )(*args)


def _fox_bwd(ql_aug, k_aug, v_aug, do_aug, nb, s, bt, exch=()):
    t = ql_aug.shape[0]
    nk = s // bt
    n_in, n_out, n_ex = 4, 3, len(exch)

    def body(*refs):
        q_ref, do_ref, k_ref, v_ref = refs[:n_in]
        dq_ref, dk_ref, dv_ref = refs[n_in + n_ex:n_in + n_ex + n_out]
        if exch:
            srcs = refs[n_in:n_in + n_ex]
            dsts = refs[n_in + n_ex + n_out:n_in + 2 * n_ex + n_out]
            start, finish = _exchange_steps(list(zip(srcs, dsts)), *refs[n_in + 2 * n_ex + n_out:])
            step = (pl.program_id(0) * 4 + pl.program_id(1)) * nk + pl.program_id(2)
            pl.when(step == 0)(start)
        kb_i = pl.program_id(2)

        @pl.when(kb_i == 0)
        def _():
            dq_ref[...] = jnp.zeros_like(dq_ref)

        row = lax.broadcasted_iota(jnp.int32, (bt, bt), 0)
        col = lax.broadcasted_iota(jnp.int32, (bt, bt), 1)
        sls = [slice(LANES * hh, LANES * (hh + 1)) for hh in range(2)]
        khs, vhs = [k_ref[:, sl] for sl in sls], [v_ref[:, sl] for sl in sls]

        def blk(qi, carry, diag):
            start = pl.multiple_of(qi * bt, bt)
            new = []
            for (dk_a, dv_a), kh, vh, sl in zip(carry, khs, vhs, sls):
                qblk, doblk = q_ref[pl.ds(start, bt), sl], do_ref[pl.ds(start, bt), sl]
                st = _nt(kh, qblk)
                if diag:
                    pt = jnp.where(col >= row, jnp.exp2(jnp.where(col >= row, st, 0.0)), 0.0)
                else:
                    pt = jnp.exp2(st)
                dst = pt * _nt(vh, doblk)
                ptb, dstb = pt.astype(BF16), dst.astype(BF16)
                dv_a = dv_a + jnp.dot(ptb, doblk, preferred_element_type=F32)
                dk_a = dk_a + jnp.dot(dstb, qblk, preferred_element_type=F32)
                dq_ref[pl.ds(start, bt), sl] += _tn(dstb, kh)
                new.append((dk_a, dv_a))
            return tuple(new)

        zero = jnp.zeros((bt, LANES), F32)
        carry = blk(kb_i, ((zero, zero), (zero, zero)), True)
        carry = lax.fori_loop(kb_i + 1, nk, lambda qi, c: blk(qi, c, False), carry)
        for (dk_acc, dv_acc), sl in zip(carry, sls):
            dk_ref[:, sl] = dk_acc
            dv_ref[:, sl] = dv_acc
        if exch:
            pl.when(step == nb * 4 * nk - 1)(finish)

    scratch = []
    if exch:
        scratch = [pltpu.SemaphoreType.DMA((N_SEM * n_ex,)), pltpu.SemaphoreType.DMA((N_SEM * n_ex,)),
                   pltpu.SemaphoreType.DMA((n_ex,))]
    whole = pl.BlockSpec((s, 2 * LANES), lambda b, j, kb_i: (b, j))
    tile = pl.BlockSpec((bt, 2 * LANES), lambda b, j, kb_i: (b * nk + kb_i, j))
    shp = jax.ShapeDtypeStruct((t, 8 * LANES), F32)
    return pl.pallas_call(
        body, name="fox_bwd", grid=(nb, 4, nk),
        in_specs=[whole, whole, tile, tile] + [ANY] * n_ex,
        out_specs=[whole, tile, tile] + [ANY] * n_ex,
        out_shape=[shp, shp, shp] + [jax.ShapeDtypeStruct(e.shape, e.dtype) for e in exch],
        scratch_shapes=scratch, compiler_params=_params(("arbitrary", "arbitrary", "arbitrary")),
    )(ql_aug, do_aug, k_aug, v_aug, *exch)


FF_BLK = D_FF // N_DEV


def _mlp_fwd(x2, ma, mb, tgt, w_out, g2, w_up, w_down, tm):
    t = x2.shape[0]

    def body(x_ref, ma_ref, mb_ref, tg_ref, wo_ref, g2_ref, wu_ref, wd_ref,
             h_ref, hn_ref, hid_ref, dy_ref, dyb_ref, loss_ref):
        @pl.when(pl.program_id(0) == 0)
        def _():
            loss_ref[...] = jnp.zeros_like(loss_ref)

        h = (x_ref[...] + jnp.dot(ma_ref[...], wo_ref[0:512, :], preferred_element_type=F32)
             + jnp.dot(mb_ref[...], wo_ref[512:1024, :], preferred_element_type=F32))
        h_ref[...] = h
        r = lax.rsqrt(jnp.mean(h * h, axis=-1, keepdims=True) + EPS)
        hn = (h * r * g2_ref[...]).astype(BF16)
        hn_ref[...] = hn
        for d in range(N_DEV):
            u = jnp.maximum(jnp.dot(hn, wu_ref[d], preferred_element_type=F32), 0.0)
            hid_ref[:, FF_BLK * d:FF_BLK * (d + 1)] = (u * u).astype(BF16)
        y = h + jnp.dot(hid_ref[...], wd_ref[...], preferred_element_type=F32)
        err = y - tg_ref[...]
        dy = err * (1.0 / D_MODEL)
        dy_ref[...] = dy
        dyb_ref[...] = dy.astype(BF16)
        part =0.5 * jnp.sum(jnp.sum(err * err, axis=1, keepdims=True) * (1.0 / D_MODEL), axis=0, keepdims=True)
        loss_ref[...] += part

    def tile(w):
        return pl.BlockSpec((tm, w), lambda i: (i, 0))

    return pl.pallas_call(
        body, name="mlp_fwd", grid=(t // tm,),
        in_specs=[tile(D_MODEL), tile(512), tile(512), tile(D_MODEL), _const_spec((D_MODEL, D_MODEL)),
                  _const_spec((1, D_MODEL)), _const_spec((N_DEV, D_MODEL, FF_BLK)), _const_spec((D_FF, D_MODEL))],
        out_specs=[tile(D_MODEL), tile(D_MODEL), tile(D_FF), tile(D_MODEL), tile(D_MODEL),
                   pl.BlockSpec((8, LANES), lambda i: (0, 0))],
        out_shape=[jax.ShapeDtypeStruct((t, D_MODEL), F32), jax.ShapeDtypeStruct((t, D_MODEL), BF16),
                   jax.ShapeDtypeStruct((t, D_FF), BF16), jax.ShapeDtypeStruct((t, D_MODEL), F32),
                   jax.ShapeDtypeStruct((t, D_MODEL), BF16), jax.ShapeDtypeStruct((8, LANES), F32)],
        compiler_params=_params(("arbitrary",)),
    )(x2, ma, mb, tgt, w_out, g2, w_up, w_down)


def _mlp_bwd(dy, hid, h, ma, mb, w_down, w_up_t, w_out, g2, tm):
    t = dy.shape[0]

    def body(dy_ref, hid_ref, h_ref, ma_ref, mb_ref, wd_ref, wut_ref, wo_ref, g2_ref,
             du_ref, dh_ref, dhb_ref, dma_ref, dob_ref, dla_ref, gg_ref):
        @pl.when(pl.program_id(0) == 0)
        def _():
            gg_ref[...] = jnp.zeros_like(gg_ref)

        dy = dy_ref[...]
        d_hid = _nt(dy.astype(BF16), wd_ref[...])
        du = (d_hid * (2.0 * jnp.sqrt(hid_ref[...].astype(F32)))).astype(BF16)
        du_ref[...] = du
        d_hn = jnp.dot(du, wut_ref[...], preferred_element_type=F32)
        h = h_ref[...]
        r = lax.rsqrt(jnp.mean(h * h, axis=-1, keepdims=True) + EPS)
        hat = h * r
        gd = d_hn * g2_ref[...]
        dh = dy + r * (gd - hat * jnp.mean(gd * hat, axis=-1, keepdims=True))
        gg_ref[...] += jnp.sum(d_hn * hat, axis=0, keepdims=True)
        dh_ref[...] = dh
        dhb = dh.astype(BF16)
        dhb_ref[...] = dhb
        dm = _nt(dhb, wo_ref[...]).astype(BF16)
        dma, dmb = dm[:, 0:512], dm[:, 512:1024]
        dma_ref[...] = dma
        sel = (lax.shift_right_logical(lax.broadcasted_iota(jnp.int32, (512, LANES), 0), 6)
               == lax.broadcasted_iota(jnp.int32, (512, LANES), 1)).astype(BF16)
        dla_ref[...] = _split_dot(dma.astype(F32) * ma_ref[...].astype(F32), sel)
        dmb32 = dmb.astype(F32)
        dlb = _split_dot(dmb32 * mb_ref[...].astype(F32), sel)
        for hd in range(8):
            blk = _head_block(dmb32[:, LANES * (hd // 2):LANES * (hd // 2 + 1)], hd % 2)
            dob_ref[:, LANES * hd:LANES * (hd + 1)] = _put3(blk, L_DELTA, -dlb[:, hd:hd + 1]).astype(BF16)

    def tile(w):
        return pl.BlockSpec((tm, w), lambda i: (i, 0))

    return pl.pallas_call(
        body, name="mlp_bwd", grid=(t // tm,),
        in_specs=[tile(D_MODEL), tile(D_FF), tile(D_MODEL), tile(512), tile(512), _const_spec((D_FF, D_MODEL)),
                  _const_spec((D_FF, D_MODEL)), _const_spec((D_MODEL, D_MODEL)), _const_spec((1, D_MODEL))],
        out_specs=[tile(D_FF), tile(D_MODEL), tile(D_MODEL), tile(512), tile(8 * LANES), tile(LANES),
                   pl.BlockSpec((1, D_MODEL), lambda i: (0, 0))],
        out_shape=[jax.ShapeDtypeStruct((t, D_FF), BF16), jax.ShapeDtypeStruct((t, D_MODEL), F32),
                   jax.ShapeDtypeStruct((t, D_MODEL), BF16), jax.ShapeDtypeStruct((t, 512), BF16),
                   jax.ShapeDtypeStruct((t, 8 * LANES), BF16), jax.ShapeDtypeStruct((t, LANES), F32),
                   jax.ShapeDtypeStruct((1, D_MODEL), F32)],
        compiler_params=_params(("arbitrary",), VMEM_LIMIT_WIDE),
    )(dy, hid, h, ma, mb, w_down, w_up_t, w_out, g2)


def _wgrad(a, b, name, bm, bn, tk, out_dtype=F32, col_blocks=False, a2=None):
    t, m = a.shape
    n = b.shape[1]
    bm, bn = min(bm, m), min(bn, n)
    nk = t // tk

    def body(*refs):
        if a2 is None:
            a_ref, b_ref, o_ref, acc = refs
        else:
            a_ref, b_ref, a2_ref, o_ref, o2_ref, acc, acc2 = refs
        i, k = pl.program_id(0), pl.program_id(2)

        @pl.when(k == 0)
        def _():
            acc[...] = jnp.zeros_like(acc)

        acc[...] += _tn(a_ref[...], b_ref[...])

        @pl.when(k == nk - 1)
        def _():
            o_ref[...] = acc[...].astype(out_dtype)

        if a2 is not None:
            @pl.when((i == 0) & (k == 0))
            def _():
                acc2[...] = jnp.zeros_like(acc2)

            @pl.when(i == 0)
            def _():
                acc2[...] += _tn(a2_ref[...], b_ref[...])

            @pl.when((i == 0) & (k == nk - 1))
            def _():
                o2_ref[...] = acc2[...]

    if col_blocks:
        out_spec = pl.BlockSpec((None, bm, bn), lambda i, j, k: (j, i, 0))
        out_shape = jax.ShapeDtypeStruct((n // bn, m, bn), out_dtype)
    else:
        out_spec = pl.BlockSpec((bm, bn), lambda i, j, k: (i, j))
        out_shape = jax.ShapeDtypeStruct((m, n), out_dtype)
    in_specs = [pl.BlockSpec((tk, bm), lambda i, j, k: (k, i)), pl.BlockSpec((tk, bn), lambda i, j, k: (k, j))]
    out_specs, out_shapes, scratch, args = [out_spec], [out_shape], [pltpu.VMEM((bm, bn), F32)], [a, b]
    if a2 is not None:
        m2 = a2.shape[1]
        in_specs.append(pl.BlockSpec((tk, m2), lambda i, j, k: (k, 0)))
        out_specs.append(pl.BlockSpec((m2, n), lambda i, j, k: (0, 0)))
        out_shapes.append(jax.ShapeDtypeStruct((m2, n), F32))
        scratch.append(pltpu.VMEM((m2, n), F32))
        args.append(a2)
    out = pl.pallas_call(
        body, name=name, grid=(m // bm, n // bn, nk), in_specs=in_specs, out_specs=out_specs, out_shape=out_shapes,
        scratch_shapes=scratch, compiler_params=_params(("arbitrary", "arbitrary", "arbitrary")),
    )(*args)
    return out[0] if a2 is None else out


def _proj_bwd(raw, dqa, dkae, dvae, dqb, dkb, dvb, fl, bf_row, x2, dh, w_main_t, w_f_t, g1, gqa, gka, gqb, gkb, nb, s, tm):
    t = x2.shape[0]
    nt = s // tm

    def body(raw_ref, dqa_ref, dkae_ref, dvae_ref, dqb_ref, dkb_ref, dvb_ref, fl_ref, b_ref, x_ref, dh_ref,
             wmt_ref, wft_ref, g1_ref, gqa_ref, gka_ref, gqb_ref, gkb_ref,
             dx_ref, dp_ref, dfb_ref, ggqa_ref, ggka_ref, ggqb_ref, ggkb_ref, gg1_ref, gb_ref, carry, dlf_ref):
        @pl.when((pl.program_id(0) == 0) & (pl.program_id(1) == 0))
        def _():
            for r in (ggqa_ref, ggka_ref, ggqb_ref, ggkb_ref, gg1_ref, gb_ref):
                r[...] = jnp.zeros_like(r)

        @pl.when(pl.program_id(1) == 0)
        def _():
            carry[...] = jnp.zeros_like(carry)

        lane = _lane((tm, LANES))
        dc = jnp.zeros((tm, LANES), F32)
        for hd in range(8):
            col = (dqb_ref[:, LANES * hd + L_CQ:LANES * hd + L_CQ + 1] - dkb_ref[:, LANES * hd + L_CK:LANES * hd + L_CK + 1])
            dc = jnp.where(lane == hd, col, dc)
        dlf_ref[...] = _tri_dot(tm, True, dc) + carry[...]
        carry[...] = dlf_ref[pl.ds(0, 1), :]
        dfl = dlf_ref[...] * (1.0 / (1.0 + jnp.exp(fl_ref[...] + b_ref[...])))
        gb_ref[...] += jnp.sum(dfl, axis=0, keepdims=True)

        raw = raw_ref[...]
        d_qa, p_qa = _head_norm_bwd(raw[:, 0:512], gqa_ref[...], dqa_ref[...])
        d_ka, p_ka = _head_norm_bwd(raw[:, 512:640], gka_ref[...], _fold_kv(dkae_ref[...]))
        d_va = _fold_kv(dvae_ref[...])
        d_qb, p_qb = _head_norm_bwd(raw[:, 768:1280], gqb_ref[...], _to_pairs(dqb_ref) * SCALE)
        d_kb, p_kb = _head_norm_bwd(raw[:, 1280:1792], gkb_ref[...], _to_pairs(dkb_ref) * (1.0 / LOG2E))
        ggqa_ref[...] += jnp.sum(p_qa, axis=0, keepdims=True)
        ggka_ref[...] += jnp.sum(p_ka, axis=0, keepdims=True)
        ggqb_ref[...] += jnp.sum(p_qb, axis=0, keepdims=True)
        ggkb_ref[...] += jnp.sum(p_kb, axis=0, keepdims=True)
        dproj = jnp.concatenate([d_qa, d_ka, d_va, d_qb, d_kb, _to_pairs(dvb_ref)], axis=1).astype(BF16)
        dp_ref[...] = dproj
        dfb = dfl.astype(BF16)
        dfb_ref[...] = dfb
        d_xn = (jnp.dot(dproj, wmt_ref[...], preferred_element_type=F32)
                + jnp.dot(dfb, wft_ref[...], preferred_element_type=F32))
        x = x_ref[...]
        r = lax.rsqrt(jnp.mean(x * x, axis=-1, keepdims=True) + EPS)
        hat = x * r
        gd = d_xn * g1_ref[...]
        dx_ref[...] = dh_ref[...] + r * (gd - hat * jnp.mean(gd * hat, axis=-1, keepdims=True))
        gg1_ref[...] += jnp.sum(d_xn * hat, axis=0, keepdims=True)

    def tile(w):
        return pl.BlockSpec((tm, w), lambda b, i: (b * nt + (nt - 1 - i), 0))

    def acc(w):
        return pl.BlockSpec((1, w), lambda b, i: (0, 0))

    return pl.pallas_call(
        body, name="proj_bwd", grid=(nb, nt),
        in_specs=[tile(MAIN_W), tile(512), tile(512), tile(512), tile(8 * LANES), tile(8 * LANES), tile(8 * LANES), tile(LANES),
                  _const_spec((1, LANES)), tile(D_MODEL), tile(D_MODEL), _const_spec((MAIN_W, D_MODEL)),
                  _const_spec((LANES, D_MODEL)), _const_spec((1, D_MODEL)), _const_spec((1, 512)), _const_spec((1, 128)),
                  _const_spec((1, 512)), _const_spec((1, 512))],
        out_specs=[tile(D_MODEL), tile(MAIN_W), tile(LANES), acc(512), acc(128), acc(512), acc(512), acc(D_MODEL), acc(LANES)],
        out_shape=[jax.ShapeDtypeStruct((t, D_MODEL), F32), jax.ShapeDtypeStruct((t, MAIN_W), BF16),
                   jax.ShapeDtypeStruct((t, LANES), BF16), jax.ShapeDtypeStruct((1, 512), F32),
                   jax.ShapeDtypeStruct((1, 128), F32), jax.ShapeDtypeStruct((1, 512), F32),
                   jax.ShapeDtypeStruct((1, 512), F32), jax.ShapeDtypeStruct((1, D_MODEL), F32),
                   jax.ShapeDtypeStruct((1, LANES), F32)],
        scratch_shapes=[pltpu.VMEM((1, LANES), F32), pltpu.VMEM((tm, LANES), F32)],
        compiler_params=_params(("arbitrary", "arbitrary"), VMEM_LIMIT_WIDE),
    )(raw, dqa, dkae, dvae, dqb, dkb, dvb, fl, bf_row, x2, dh, w_main_t, w_f_t, g1, gqa, gka, gqb, gkb)


IN_PAD = 304


def _local_step(x, tgt, w_in_t, rest, g1, b_forget, qna, kna, sinks, qnb, knb, g2,
                tm=512, bt=1024, btf=1024, tq=2048, wk=4096, distributed=False):
    nb, s, _ = x.shape
    t = nb * s
    x2, tgt2 = x.reshape(t, D_MODEL), tgt.reshape(t, D_MODEL)
    g1r, g2r = g1.reshape(1, D_MODEL), g2.reshape(1, D_MODEL)
    gqa, gka = jnp.tile(qna, 8).reshape(1, 512), jnp.tile(kna, 2).reshape(1, 128)
    gqb, gkb = jnp.tile(qnb, 8).reshape(1, 512), jnp.tile(knb, 8).reshape(1, 512)
    bf_row = jnp.pad(b_forget, (0, LANES - 8)).reshape(1, LANES)
    sink_row = jnp.pad(sinks, (0, LANES - 8)).reshape(1, LANES)
    w_main_t = w_in_t[0:MAIN_W]
    w_f_t = jnp.pad(w_in_t[MAIN_W:IN_W], ((0, LANES - 8), (0, 0)))

    xn, raw, fl, qa, kae, vae, q_aug, k_aug, v_aug = _norm_proj(x2, g1r, w_main_t, w_f_t, gqa, gka, gqb, gkb, bf_row, s, tm)
    ma, lse_a = _swa_fwd(qa, kae, vae, sink_row, nb, s, tq)
    if distributed:
        mb, ql_aug, w_out, w_up, w_down, w_up_t = _fox_fwd(q_aug, k_aug, v_aug, nb, s, btf, shards=rest)
    else:
        mb, ql_aug = _fox_fwd(q_aug, k_aug, v_aug, nb, s, btf)
        w_out, w_up, w_down, w_up_t = rest
    w_out, w_down = w_out.reshape(D_MODEL, D_MODEL), w_down.reshape(D_FF, D_MODEL)
    h, hn, hid, dy, dyb, loss_acc = _mlp_fwd(x2, ma, mb, tgt2, w_out, g2r, w_up, w_down, tm)

    du, dh, dhb, dma, do_aug, dla, gg2 = _mlp_bwd(dy, hid, h, ma, mb, w_down, w_up_t.reshape(D_FF, D_MODEL), w_out, g2r, tm)
    g_down = _wgrad(hid, dyb, "wgrad_down", 512, 1024, wk, BF16).reshape(N_DEV, 512, D_MODEL)
    g_up = _wgrad(hn, du, "wgrad_up", 1024, 512, wk, BF16, col_blocks=True)
    g_out = jnp.concatenate([_wgrad(ma, dhb, "wgrad_out_a", 512, 1024, wk, BF16),
                             _wgrad(mb, dhb, "wgrad_out_b", 512, 1024, wk, BF16)], axis=0).reshape(N_DEV, 128, D_MODEL)

    dqa, dkae, dvae, dsink = _swa_bwd(qa, kae, vae, dma, sink_row, lse_a, dla, nb, s, tq)
    fox = _fox_bwd(ql_aug, k_aug, v_aug, do_aug, nb, s, bt, exch=(g_out, g_up, g_down) if distributed else ())
    dqb, dkb, dvb = fox[:3]
    if distributed:
        g_out, g_up, g_down = fox[3:]
    grad_x, dproj, dfb, ggqa, ggka, ggqb, ggkb, gg1, gbf = _proj_bwd(
        raw, dqa, dkae, dvae, dqb, dkb, dvb, fl, bf_row, x2, dh, w_main_t, w_f_t, g1r, gqa, gka, gqb, gkb, nb, s, tm)
    g_main_t, g_gate_t = _wgrad(dproj, xn, "wgrad_in", 768, 1024, wk, a2=dfb)
    g_in_t = jnp.concatenate([g_main_t, g_gate_t[0:8]], axis=0)

    small = (gg1.reshape(D_MODEL), gbf[0, 0:8], ggqa.reshape(8, 64).sum(0), ggka.reshape(2, 64).sum(0),
             dsink.sum(0)[:, 0:2, 0].reshape(8), ggqb.reshape(8, 64).sum(0), ggkb.reshape(8, 64).sum(0),
             gg2.reshape(D_MODEL))
    return loss_acc[0, 0], grad_x.reshape(nb, s, D_MODEL), g_in_t, g_out, g_up, g_down, small


def _all_gather(shard):
    def body(x_ref, out_ref, send_sems, recv_sems, local_sem):
        start, forward, finish = _gather_steps([(x_ref, out_ref)], send_sems, recv_sems, local_sem)
        start()
        forward()
        finish()

    return pl.pallas_call(
        body, name="gather_w_in", out_shape=jax.ShapeDtypeStruct((N_DEV,) + shard.shape, shard.dtype),
        in_specs=[ANY], out_specs=ANY,
        scratch_shapes=[pltpu.SemaphoreType.DMA((N_SEM,)), pltpu.SemaphoreType.DMA((N_SEM,)), pltpu.SemaphoreType.DMA((1,))],
    )(shard)


def _exchange(*arrays):
    n_ex = len(arrays)

    def body(*refs):
        start, finish = _exchange_steps(list(zip(refs[:n_ex], refs[n_ex:2 * n_ex])), *refs[2 * n_ex:])
        start()
        finish()

    return pl.pallas_call(
        body, name="exchange_tail", out_shape=[jax.ShapeDtypeStruct(a.shape, a.dtype) for a in arrays],
        in_specs=[ANY] * n_ex, out_specs=[ANY] * n_ex,
        scratch_shapes=[pltpu.SemaphoreType.DMA((N_SEM * n_ex,)), pltpu.SemaphoreType.DMA((N_SEM * n_ex,)),
                        pltpu.SemaphoreType.DMA((n_ex,))],
    )(*arrays)


def _sum_adamw(recv, w, m, v, tr, name):
    _, r, n = recv.shape

    def body(r_ref, w_ref, m_ref, v_ref, g_ref, d_ref, nm_ref, nv_ref):
        g = r_ref[0].astype(F32)
        for s in range(1, N_DEV):
            g = g + r_ref[s].astype(F32)
        g_ref[...] = g
        nm = ADAM_B1 * m_ref[...] + (1.0 - ADAM_B1) * g
        nv = ADAM_B2 * v_ref[...] + (1.0 - ADAM_B2) * (g * g)
        m_hat = nm / (1.0 - ADAM_B1 ** ADAM_STEP)
        v_hat = nv / (1.0 - ADAM_B2 ** ADAM_STEP)
        d_ref[...] = -ADAM_LR * (m_hat / (jnp.sqrt(v_hat) + ADAM_EPS) + ADAM_WD * w_ref[...])
        nm_ref[...] = nm
        nv_ref[...] = nv

    tile = pl.BlockSpec((tr, n), lambda i: (i, 0))
    shp = jax.ShapeDtypeStruct((r, n), F32)
    return pl.pallas_call(
        body, name=name, grid=(r // tr,),
        in_specs=[pl.BlockSpec((N_DEV, tr, n), lambda i: (0, i, 0)), tile, tile, tile],
        out_specs=[tile, tile, tile, tile], out_shape=[shp, shp, shp, shp],
        compiler_params=_params(("arbitrary",)),
    )(recv, w, m, v)


def _small_rows(g1, bf, qna, kna, sk, qnb, knb, g2):
    row2 = jnp.concatenate([bf, qna, kna, sk, qnb, knb])
    return jnp.zeros((8, D_MODEL), F32).at[0].set(g1).at[1].set(g2).at[2, 0:row2.shape[0]].set(row2)


def _in_rows(w_in_s):
    return jnp.pad(w_in_s.T, ((0, IN_PAD - IN_SHARD), (0, 0)))


def kernel(x, attn_norm_g, w_in, b_forget, q_norm_a, k_norm_a, sink_logits, q_norm_b, k_norm_b, w_out, mlp_norm_g, w_up, w_down, loss_target, m_attn_norm_g, m_w_in, m_b_forget, m_q_norm_a, m_k_norm_a, m_sink_logits, m_q_norm_b, m_k_norm_b, m_w_out, m_mlp_norm_g, m_w_up, m_w_down, v_attn_norm_g, v_w_in, v_b_forget, v_q_norm_a, v_k_norm_a, v_sink_logits, v_q_norm_b, v_k_norm_b, v_w_out, v_mlp_norm_g, v_w_up, v_w_down):
    w_in_r = _in_rows(w_in)
    w_in_t = _all_gather(w_in_r.astype(BF16))[:, 0:IN_SHARD].reshape(IN_W, D_MODEL)
    w_up_b = w_up.astype(BF16)
    rest = (w_out.astype(BF16), w_up_b, w_down.astype(BF16), w_up_b.T)

    loss_part, grad_x, g_in_t, r_out, r_up, r_down, small = _local_step(
        x, loss_target, w_in_t, rest, attn_norm_g, b_forget, q_norm_a, k_norm_a, sink_logits, q_norm_b, k_norm_b, mlp_norm_g,
        distributed=True)

    g_in_blocks = jnp.pad(g_in_t.reshape(N_DEV, IN_SHARD, D_MODEL), ((0, 0), (0, IN_PAD - IN_SHARD), (0, 0))).astype(BF16)
    small_blocks = jnp.broadcast_to(_small_rows(*small).at[3, 0].set(loss_part), (N_DEV, 8, D_MODEL))
    r_in, r_small = _exchange(g_in_blocks, small_blocks)

    small_w = _small_rows(attn_norm_g, b_forget, q_norm_a, k_norm_a, sink_logits, q_norm_b, k_norm_b, mlp_norm_g)
    small_m = _small_rows(m_attn_norm_g, m_b_forget, m_q_norm_a, m_k_norm_a, m_sink_logits, m_q_norm_b, m_k_norm_b, m_mlp_norm_g)
    small_v = _small_rows(v_attn_norm_g, v_b_forget, v_q_norm_a, v_k_norm_a, v_sink_logits, v_q_norm_b, v_k_norm_b, v_mlp_norm_g)
    o_in = [a[0:IN_SHARD].T for a in _sum_adamw(r_in, w_in_r, _in_rows(m_w_in), _in_rows(v_w_in), IN_PAD, "adamw_in")]
    o_out = _sum_adamw(r_out, w_out, m_w_out, v_w_out, 128, "adamw_out")
    o_up = _sum_adamw(r_up, w_up, m_w_up, v_w_up, 256, "adamw_up")
    o_down = _sum_adamw(r_down, w_down, m_w_down, v_w_down, 128, "adamw_down")
    o_small = _sum_adamw(r_small, small_w, small_m, small_v, 8, "adamw_small")

    def leaves(i):
        row2 = o_small[i][2]
        return (o_small[i][0], o_in[i], row2[0:8], row2[8:72], row2[72:136], row2[136:144], row2[144:208], row2[208:272],
                o_out[i], o_small[i][1], o_up[i], o_down[i])

    return (o_small[0][3, 0], grad_x, *leaves(0), *leaves(1), *leaves(2), *leaves(3))
```

```python
import functools

import jax
import jax.numpy as jnp
from jax import lax
from jax.experimental import pallas as pl
from jax.experimental.pallas import tpu as pltpu

F32 = jnp.float32
BF16 = jnp.bfloat16

D_MODEL = 1024
HEAD_DIM = 64
N_DEV = 8
D_FF = 4096
MAIN_W = 2304
IN_W = 2312
IN_SHARD = 289
WINDOW = 128
EPS = 1e-6
SCALE = 0.125
LOG2E = 1.4426950408889634
LANES = 128
NEG_INF = float("-inf")

ADAM_LR = 0.001
ADAM_B1 = 0.9
ADAM_B2 = 0.999
ADAM_EPS = 1e-08
ADAM_WD = 0.01
ADAM_STEP = 10

VMEM_LIMIT = 56 * 1024 * 1024
VMEM_LIMIT_WIDE = 62 * 1024 * 1024


def _params(sem, vmem=VMEM_LIMIT):
    return pltpu.CompilerParams(dimension_semantics=sem, vmem_limit_bytes=vmem)


def _const_spec(shape):
    nd = len(shape)
    return pl.BlockSpec(shape, lambda *_: (0,) * nd, pipeline_mode=pl.Buffered(1))


def _lane(shape):
    return lax.broadcasted_iota(jnp.int32, shape, len(shape) - 1)


def _split_dot(v, mat):
    hi = v.astype(BF16)
    lo = (v - hi.astype(F32)).astype(BF16)
    return (jnp.dot(hi, mat, preferred_element_type=F32) + jnp.dot(lo, mat, preferred_element_type=F32))


def _head_ones(n):
    r = lax.shift_right_logical(lax.broadcasted_iota(jnp.int32, (n, n), 0), 6)
    c = lax.shift_right_logical(lax.broadcasted_iota(jnp.int32, (n, n), 1), 6)
    return (r == c).astype(BF16)


def _head_sum(v):
    w = v.shape[1]
    vb = v.astype(BF16)
    if w <= 256:
        return jnp.dot(vb, _head_ones(w), preferred_element_type=F32)
    ones = _head_ones(256)
    return jnp.concatenate([jnp.dot(vb[:, s:s + 256], ones, preferred_element_type=F32) for s in range(0, w, 256)], axis=1)


def _head_norm(seg, gain):
    rs = lax.rsqrt(_head_sum(seg * seg) * (1.0 / HEAD_DIM) + EPS)
    return seg * rs * gain


def _head_norm_bwd(seg, gain, d_out):
    rs = lax.rsqrt(_head_sum(seg * seg) * (1.0 / HEAD_DIM) + EPS)
    hat = seg * rs
    gd = d_out * gain
    d_seg = rs * (gd - hat * (_head_sum(gd * hat) * (1.0 / HEAD_DIM)))
    return d_seg, d_out * hat


def _expand_kv(v):
    r = pltpu.roll(v, 64, axis=1)
    lo = _lane(v.shape) < 64
    return jnp.concatenate([jnp.where(lo, v, r), jnp.where(lo, r, v)], axis=1)


def _fold_kv(e4):
    t0 = e4[:, 0:128] + e4[:, 128:256]
    t1 = e4[:, 256:384] + e4[:, 384:512]
    t0 = t0 + pltpu.roll(t0, 64, axis=1)
    t1 = t1 + pltpu.roll(t1, 64, axis=1)
    return jnp.where(_lane(t0.shape) < 64, t0, t1)


def _pick_lane(blk, idx):
    return jnp.sum(jnp.where(_lane(blk.shape) == idx, blk, 0.0), axis=1, keepdims=True)


def _nt(a, b):
    return lax.dot_general(a, b, (((1,), (1,)), ((), ())), preferred_element_type=F32)


def _tn(a, b):
    return lax.dot_general(a, b, (((0,), (0,)), ((), ())), preferred_element_type=F32)


def _norm_proj(x2, g1, w_main_t, w_f_t, gqa, gka, gqb, gkb, bf_row, s, tm):
    t = x2.shape[0]
    nt = s // tm

    def body(x_ref, g1_ref, wm_ref, wf_ref, gqa_ref, gka_ref, gqb_ref, gkb_ref, b_ref,
             xn_ref, raw_ref, fl_ref, qa_ref, kae_ref, vae_ref, qo_ref, ko_ref, vo_ref, carry, c_ref):
        @pl.when(lax.rem(pl.program_id(0), nt) == 0)
        def _():
            carry[...] = jnp.zeros_like(carry)

        x = x_ref[...]
        r = lax.rsqrt(jnp.mean(x * x, axis=-1, keepdims=True) + EPS)
        xn = (x * r * g1_ref[...]).astype(BF16)
        xn_ref[...] = xn
        proj = _nt(xn, wm_ref[...])
        raw_ref[...] = proj
        fl = _nt(xn, wf_ref[...])
        fl_ref[...] = fl
        qa_ref[...] = _head_norm(proj[:, 0:512], gqa_ref[...]).astype(BF16)
        kae_ref[...] = _expand_kv(_head_norm(proj[:, 512:640], gka_ref[...])).astype(BF16)
        vae_ref[...] = _expand_kv(proj[:, 640:768]).astype(BF16)

        z = fl + b_ref[...]
        e = jnp.exp(-jnp.abs(z))
        u = 1.0 + e
        log1p = jnp.where(u == 1.0, e, jnp.log(u) * (e / (u - 1.0)))
        lf = jnp.minimum(z, 0.0) - log1p
        for r0 in range(0, tm, 256):
            c_ref[r0:r0 + 256, :] = _tri_dot(256, False, lf[r0:r0 + 256]) + carry[...]
            carry[...] = c_ref[pl.ds(r0 + 255, 1), :]
        c2 = c_ref[...] * LOG2E
        qb = _head_norm(proj[:, 768:1280], gqb_ref[...]) * (SCALE * LOG2E)
        kb = _head_norm(proj[:, 1280:1792], gkb_ref[...])
        lane = _lane((tm, LANES))
        for h in range(8):
            j, half = h // 2, h % 2
            pair, blk = slice(LANES * j, LANES * (j + 1)), slice(LANES * h, LANES * (h + 1))
            feat = _spread3(c2[:, h:h + 1], (tm, LANES), (L_CK, L_CQ))
            q = _put_ones(_head_block(qb[:, pair], half), (L_CK, L_CK + 1, L_CK + 2))
            qo_ref[:, blk] = jnp.where((lane >= L_CQ) & (lane < L_CQ + 3), feat, q).astype(BF16)
            k = _put_ones(_head_block(kb[:, pair], half), tuple(range(L_CQ, L_CQ + 6)))
            ko_ref[:, blk] = jnp.where((lane >= L_CK) & (lane < L_CK + 3), -feat, k).astype(BF16)
            v = _head_block(proj[:, 1792 + LANES * j:1792 + LANES * (j + 1)], half)
            vo_ref[:, blk] = _put_ones(v, (L_ONE, L_DELTA, L_DELTA + 1, L_DELTA + 2)).astype(BF16)

    def tile(w):
        return pl.BlockSpec((tm, w), lambda i: (i, 0))

    aug = jax.ShapeDtypeStruct((t, 8 * LANES), BF16)
    return pl.pallas_call(
        body, name="norm_proj", grid=(t // tm,),
        in_specs=[tile(D_MODEL), _const_spec((1, D_MODEL)), _const_spec((MAIN_W, D_MODEL)), _const_spec((LANES, D_MODEL)),
                  _const_spec((1, 512)), _const_spec((1, 128)), _const_spec((1, 512)), _const_spec((1, 512)),
                  _const_spec((1, LANES))],
        out_specs=[tile(D_MODEL), tile(MAIN_W), tile(LANES), tile(512), tile(256), tile(256)] + [tile(8 * LANES)] * 3,
        out_shape=[jax.ShapeDtypeStruct((t, D_MODEL), BF16), jax.ShapeDtypeStruct((t, MAIN_W), F32),
                   jax.ShapeDtypeStruct((t, LANES), F32), jax.ShapeDtypeStruct((t, 512), BF16),
                   jax.ShapeDtypeStruct((t, 256), BF16), jax.ShapeDtypeStruct((t, 256), BF16), aug, aug, aug],
        scratch_shapes=[pltpu.VMEM((1, LANES), F32), pltpu.VMEM((tm, LANES), F32)],
        compiler_params=_params(("arbitrary",)),
    )(x2, g1, w_main_t, w_f_t, gqa, gka, gqb, gkb, bf_row)


def _tri_dot(n, upper, v):
    r = lax.broadcasted_iota(jnp.int32, (n, n), 0)
    c = lax.broadcasted_iota(jnp.int32, (n, n), 1)
    tri = ((c >= r) if upper else (c <= r)).astype(BF16)
    hi = v.astype(BF16)
    mid = (v - hi.astype(F32)).astype(BF16)
    lo = (v - hi.astype(F32) - mid.astype(F32)).astype(BF16)
    return (jnp.dot(tri, hi, preferred_element_type=F32) + jnp.dot(tri, mid, preferred_element_type=F32)
            + jnp.dot(tri, lo, preferred_element_type=F32))


def _slope(p, hh):
    out = jnp.float32(2.0 ** -(2 * 3 + hh + 1))
    for pp in (2, 1, 0):
        out = jnp.where(p == pp, jnp.float32(2.0 ** -(2 * pp + hh + 1)), out)
    return out


def _swa_windows(ref, i, tq):
    nsub = tq // WINDOW
    cur = ref[pl.ds(pl.multiple_of(i * tq, tq), tq), :].reshape(nsub, WINDOW, LANES)
    first = ref[pl.ds(pl.multiple_of(jnp.maximum(i * tq - WINDOW, 0), WINDOW), WINDOW), :].reshape(1, WINDOW, LANES)
    return jnp.concatenate([jnp.concatenate([first, cur[0:nsub - 1]], axis=0), cur], axis=1)


def _both_heads(x3, lo):
    zero = jnp.zeros_like(x3)
    return jnp.concatenate([jnp.where(lo, x3, zero), jnp.where(lo, zero, x3)], axis=0)


def _swa_head_consts(sink_ref, p, i, nsub):
    bidx = lax.broadcasted_iota(jnp.int32, (2 * nsub, 1, 1), 0)
    is_a = bidx < nsub
    slope = jnp.where(is_a, _slope(p, 0), _slope(p, 1))
    sinks = sink_ref[...]
    sink = jnp.where(is_a, _pick_lane(sinks, 2 * p).reshape(1, 1, 1), _pick_lane(sinks, 2 * p + 1).reshape(1, 1, 1))
    first = (i == 0) & ((bidx == 0) | (bidx == nsub))
    return slope, sink, first


def _swa_fwd(qa, kae, vae, sink_row, nb, s, tq):
    t = qa.shape[0]
    nq = s // tq
    nsub = tq // WINDOW

    def body(q_ref, k_ref, v_ref, sink_ref, o_ref, lse_ref):
        p, i = pl.program_id(1), pl.program_id(2)
        lo = _lane((1, 1, LANES)) < 64
        kk, vv = _swa_windows(k_ref, i, tq), _swa_windows(v_ref, i, tq)
        qs = (q_ref[...].astype(F32) * SCALE).astype(BF16).reshape(nsub, WINDOW, LANES)
        q8 = _both_heads(qs, lo)
        s8 = jnp.einsum("bqd,bkd->bqk", q8, jnp.concatenate([kk, kk], axis=0), preferred_element_type=F32)
        row = lax.broadcasted_iota(jnp.int32, (1, WINDOW, 2 * WINDOW), 1)
        col = lax.broadcasted_iota(jnp.int32, (1, WINDOW, 2 * WINDOW), 2)
        dist = row + WINDOW - col
        slope, sink, first = _swa_head_consts(sink_ref, p, i, nsub)
        valid = (dist >= 0) & (dist < WINDOW) & ((col >= WINDOW) | jnp.logical_not(first))
        s8 = jnp.where(valid, s8 - slope * dist.astype(F32), NEG_INF)
        m = jnp.maximum(jnp.max(s8, axis=2, keepdims=True), sink)
        e = jnp.exp(s8 - m)
        den = jnp.sum(e, axis=2, keepdims=True) + jnp.exp(sink - m)
        pr = (e * (1.0 / den)).astype(BF16)
        o8 = jnp.einsum("bqk,bkd->bqd", pr, jnp.concatenate([vv, vv], axis=0), preferred_element_type=F32)
        lse8 = m + jnp.log(den)
        o_ref[...] = jnp.where(lo, o8[0:nsub], o8[nsub:]).astype(BF16).reshape(tq, LANES)
        lse_ref[...] = jnp.where(lo, lse8[0:nsub], lse8[nsub:]).reshape(tq, LANES)

    return pl.pallas_call(
        body, name="swa_fwd", grid=(nb, 4, nq),
        in_specs=[pl.BlockSpec((tq, LANES), lambda b, p, i: (b * nq + i, p)),
                  pl.BlockSpec((s, LANES), lambda b, p, i: (b, lax.shift_right_logical(p, 1))),
                  pl.BlockSpec((s, LANES), lambda b, p, i: (b, lax.shift_right_logical(p, 1))),
                  pl.BlockSpec((1, LANES), lambda b, p, i: (0, 0))],
        out_specs=[pl.BlockSpec((tq, LANES), lambda b, p, i: (b * nq + i, p)),
                   pl.BlockSpec((None, tq, LANES), lambda b, p, i: (p, b * nq + i, 0))],
        out_shape=[jax.ShapeDtypeStruct((t, 512), BF16), jax.ShapeDtypeStruct((4, t, LANES), F32)],
        compiler_params=_params(("arbitrary", "arbitrary", "arbitrary")),
    )(qa, kae, vae, sink_row)


def _swa_bwd(qa, kae, vae, do_a, sink_row, lse, delta, nb, s, tq):
    t = qa.shape[0]
    nq = s // tq
    nsub = tq // WINDOW

    def body(q_ref, do_ref, k_ref, v_ref, sink_ref, lse_ref, dl_ref, dq_ref, dk_ref, dv_ref, ds_ref):
        p, i = pl.program_id(1), pl.program_id(2)

        @pl.when(i == 0)
        def _():
            ds_ref[...] = jnp.zeros_like(ds_ref)

        lo = _lane((1, 1, LANES)) < 64
        kk, vv = _swa_windows(k_ref, i, tq), _swa_windows(v_ref, i, tq)
        kks = (kk.astype(F32) * SCALE).astype(BF16)
        k8, v8 = jnp.concatenate([kks, kks], axis=0), jnp.concatenate([vv, vv], axis=0)
        q8 = _both_heads(q_ref[...].reshape(nsub, WINDOW, LANES), lo)
        do8 = _both_heads(do_ref[...].reshape(nsub, WINDOW, LANES), lo)
        cur = pl.multiple_of(i * tq, tq)
        sub = lax.broadcasted_iota(jnp.int32, (WINDOW, WINDOW), 0)
        lse_t = [lse_ref[u * WINDOW:(u + 1) * WINDOW, :].T for u in range(nsub)]
        dl_t = [dl_ref[u * WINDOW:(u + 1) * WINDOW, :].T for u in range(nsub)]
        lse8 = jnp.concatenate([t_[64 * hh:64 * hh + 1, :].reshape(1, 1, WINDOW) for hh in range(2) for t_ in lse_t], axis=0)
        dl8 = jnp.concatenate([jnp.sum(jnp.where(sub == 2 * p + hh, t_, 0.0), axis=0, keepdims=True).reshape(1, 1, WINDOW)
                               for hh in range(2) for t_ in dl_t], axis=0)
        row = lax.broadcasted_iota(jnp.int32, (1, 2 * WINDOW, WINDOW), 1)
        col = lax.broadcasted_iota(jnp.int32, (1, 2 * WINDOW, WINDOW), 2)
        dist = col + WINDOW - row
        slope, sink, first = _swa_head_consts(sink_ref, p, i, nsub)
        valid = (dist >= 0) & (dist < WINDOW) & ((row >= WINDOW) | jnp.logical_not(first))
        st = jnp.einsum("bkd,bqd->bkq", k8, q8, preferred_element_type=F32) - slope * dist.astype(F32) - lse8
        pt = jnp.where(valid, jnp.exp(jnp.where(valid, st, 0.0)), 0.0)
        dpt = jnp.einsum("bkd,bqd->bkq", v8, do8, preferred_element_type=F32)
        dst = pt * (dpt - dl8)
        ptb, dstb = pt.astype(BF16), dst.astype(BF16)
        dv8 = jnp.einsum("bkq,bqd->bkd", ptb, do8, preferred_element_type=F32)
        dk8 = jnp.einsum("bkq,bqd->bkd", dstb, q8, preferred_element_type=F32) * SCALE
        dq8 = jnp.einsum("bkq,bkd->bqd", dstb, k8, preferred_element_type=F32)
        dq_ref[...] = jnp.where(lo, dq8[0:nsub], dq8[nsub:]).reshape(tq, LANES)

        psd = jnp.exp(sink - lse8) * dl8
        row_h = lax.broadcasted_iota(jnp.int32, (8, LANES), 0)
        for hh in range(2):
            tot = jnp.sum(jnp.sum(psd[hh * nsub:(hh + 1) * nsub], axis=2, keepdims=True), axis=0, keepdims=True)
            ds_ref[...] += jnp.where(row_h == hh, -tot.reshape(1, 1), 0.0)

        prev = pl.multiple_of(jnp.maximum(i * tq - WINDOW, 0), WINDOW)
        for g8, g_ref in ((dk8, dk_ref), (dv8, dv_ref)):
            g4 = g8[0:nsub] + g8[nsub:]
            own, before = g4[:, WINDOW:, :], g4[:, 0:WINDOW, :]
            shifted = jnp.concatenate([before[1:nsub], jnp.zeros((1, WINDOW, LANES), F32)], axis=0)
            g_ref[pl.ds(cur, tq), :] = (own + shifted).reshape(tq, LANES)
            g_ref[pl.ds(prev, WINDOW), :] += before[0]

    return pl.pallas_call(
        body, name="swa_bwd", grid=(nb, 4, nq),
        in_specs=[pl.BlockSpec((tq, LANES), lambda b, p, i: (b * nq + i, p)),
                  pl.BlockSpec((tq, LANES), lambda b, p, i: (b * nq + i, p)),
                  pl.BlockSpec((s, LANES), lambda b, p, i: (b, lax.shift_right_logical(p, 1))),
                  pl.BlockSpec((s, LANES), lambda b, p, i: (b, lax.shift_right_logical(p, 1))),
                  pl.BlockSpec((1, LANES), lambda b, p, i: (0, 0)),
                  pl.BlockSpec((None, tq, LANES), lambda b, p, i: (p, b * nq + i, 0)),
                  pl.BlockSpec((tq, LANES), lambda b, p, i: (b * nq + i, 0))],
        out_specs=[pl.BlockSpec((tq, LANES), lambda b, p, i: (b * nq + i, p)),
                   pl.BlockSpec((s, LANES), lambda b, p, i: (b, p)),
                   pl.BlockSpec((s, LANES), lambda b, p, i: (b, p)),
                   pl.BlockSpec((None, None, 8, LANES), lambda b, p, i: (b, p, 0, 0))],
        out_shape=[jax.ShapeDtypeStruct((t, 512), F32), jax.ShapeDtypeStruct((t, 512), F32),
                   jax.ShapeDtypeStruct((t, 512), F32), jax.ShapeDtypeStruct((nb, 4, 8, LANES), F32)],
        compiler_params=_params(("arbitrary", "arbitrary", "arbitrary")),
    )(qa, do_a, kae, vae, sink_row, lse, delta)


MESH = pl.DeviceIdType.MESH
ANY = pl.BlockSpec(memory_space=pl.ANY)
N_SEM = 7


def _gather_steps(pairs, send_sems, recv_sems, local_sems):
    x, y, c = lax.axis_index("x"), lax.axis_index("y"), lax.axis_index("c")
    me, sibling = (x, y, c), (x, y, 1 - c)
    chips = [(1 - x, y), (x, 1 - y), (1 - x, 1 - y)]
    mine, first, passed, landed, last = [], [], [], [], []
    for a, (x_ref, out_ref) in enumerate(pairs):
        def slot(px, py, pc, out_ref=out_ref):
            return out_ref.at[4 * px + 2 * py + pc]

        def copy(k, block, to, src=None, a=a, slot=slot):
            return pltpu.make_async_remote_copy(
                src_ref=slot(*block) if src is None else src, dst_ref=slot(*block),
                send_sem=send_sems.at[N_SEM * a + k], recv_sem=recv_sems.at[N_SEM * a + k], device_id=to, device_id_type=MESH)

        mine.append(pltpu.make_async_copy(x_ref, slot(*me), local_sems.at[a]))
        first += [copy(0, me, sibling, src=x_ref)] + [copy(1 + j, me, (*chip, c), src=x_ref) for j, chip in enumerate(chips)]
        passed += [copy(4 + j, (*chip, c), sibling) for j, chip in enumerate(chips)]
        landed += [copy(1 + j, (*chip, c), me) for j, chip in enumerate(chips)]
        last += [copy(0, sibling, me)] + [copy(4 + j, (*chip, 1 - c), me) for j, chip in enumerate(chips)]

    def start():
        for cp in mine + first:
            cp.start()

    def forward():
        for arrived, onward in zip(landed, passed):
            arrived.wait_recv()
            onward.start()

    def finish():
        for cp in last:
            cp.wait_recv()
        for cp in first + passed:
            cp.wait_send()
        for cp in mine:
            cp.wait()

    return start, forward, finish


def _exchange_steps(pairs, send_sems, recv_sems, local_sems):
    x, y, c = lax.axis_index("x"), lax.axis_index("y"), lax.axis_index("c")
    my_id = 4 * x + 2 * y + c
    local, remote = [], []
    for a, (src, dst) in enumerate(pairs):
        local.append(pltpu.make_async_copy(src.at[my_id], dst.at[my_id], local_sems.at[a]))
        for k in range(1, N_DEV):
            px = 1 - x if k & 4 else x
            py = 1 - y if k & 2 else y
            pc = 1 - c if k & 1 else c
            remote.append(pltpu.make_async_remote_copy(
                src_ref=src.at[4 * px + 2 * py + pc], dst_ref=dst.at[my_id],
                send_sem=send_sems.at[N_SEM * a + k - 1], recv_sem=recv_sems.at[N_SEM * a + k - 1],
                device_id=(px, py, pc), device_id_type=MESH))

    def start():
        for cp in local + remote:
            cp.start()

    def finish():
        for cp in remote:
            cp.wait_recv()
        for cp in remote:
            cp.wait_send()
        for cp in local:
            cp.wait()

    return start, finish


L_ONE = 64
L_CK = 65
L_CQ = 68
L_LSE = 71
L_DELTA = 74


def _head_block(pair, half):
    y = pair if half == 0 else pltpu.roll(pair, 64, axis=1)
    return jnp.where(_lane(pair.shape) < 64, y, 0.0)


def _put3(blk, lane0, col):
    lane = _lane(blk.shape)
    hi = col.astype(BF16).astype(F32)
    mid = (col - hi).astype(BF16).astype(F32)
    lo = (col - hi - mid).astype(BF16).astype(F32)
    return jnp.where(lane == lane0, hi, jnp.where(lane == lane0 + 1, mid, jnp.where(lane == lane0 + 2, lo, blk)))


def _spread3(col, shape, lane0s):
    lane = _lane(shape)
    hi = col.astype(BF16).astype(F32)
    mid = (col - hi).astype(BF16).astype(F32)
    lo = (col - hi - mid).astype(BF16).astype(F32)

    def at(k):
        return functools.reduce(jnp.logical_or, [lane == ln + k for ln in lane0s])

    return jnp.where(at(0), hi, jnp.where(at(1), mid, jnp.where(at(2), lo, 0.0)))


def _put_ones(blk, lanes):
    lane = _lane(blk.shape)
    hit = functools.reduce(jnp.logical_or, [lane == ln for ln in lanes])
    return jnp.where(hit, 1.0, blk)


def _to_pairs(ref):
    out = []
    for j in range(4):
        a, b = ref[:, 2 * LANES * j:2 * LANES * j + LANES], ref[:, 2 * LANES * j + LANES:2 * LANES * (j + 1)]
        out.append(jnp.where(_lane(a.shape) < 64, a, pltpu.roll(b, 64, axis=1)))
    return jnp.concatenate(out, axis=1)


def _fox_fwd(q_aug, k_aug, v_aug, nb, s, bt, shards=()):
    t = q_aug.shape[0]
    nq = s // bt
    n_in, n_sh = 3, len(shards)

    def body(*refs):
        q_ref, k_ref, v_ref = refs[:n_in]
        o_ref, ql_ref = refs[n_in + n_sh:n_in + n_sh + 2]
        if shards:
            srcs, dsts = refs[n_in:n_in + n_sh], refs[n_in + n_sh + 2:n_in + 2 * n_sh + 2]
            start, forward, finish = _gather_steps(list(zip(srcs, dsts)), *refs[n_in + 2 * n_sh + 2:])
            step = (pl.program_id(0) * 4 + pl.program_id(1)) * nq + pl.program_id(2)
            pl.when(step == 0)(start)
            pl.when(step == nb * 3 * nq)(forward)
        i = pl.program_id(2)
        sls = [slice(LANES * hh, LANES * (hh + 1)) for hh in range(2)]
        qhs = [q_ref[:, sl] for sl in sls]

        def update(m, acc, qrows, start, size, sl, causal):
            sc = _nt(qrows, k_ref[pl.ds(start, size), sl])
            if causal:
                row = lax.broadcasted_iota(jnp.int32, sc.shape, 0)
                col = lax.broadcasted_iota(jnp.int32, sc.shape, 1)
                sc = jnp.where(row >= col, sc, NEG_INF)
            m_new = jnp.maximum(m, jnp.max(sc, axis=1, keepdims=True))
            pr = jnp.exp2(sc - m_new).astype(BF16)
            acc = jnp.exp2(m - m_new) * acc + jnp.dot(pr, v_ref[pl.ds(start, size), sl], preferred_element_type=F32)
            return m_new, acc

        def blk(kb_i, carry):
            start = pl.multiple_of(kb_i * bt, bt)
            return tuple(update(m, acc, qh, start, bt, sl, False) for (m, acc), qh, sl in zip(carry, qhs, sls))

        def diag_blk(carry):
            start = pl.multiple_of(i * bt, bt)
            return tuple(update(m, acc, qh, start, bt, sl, True) for (m, acc), qh, sl in zip(carry, qhs, sls))

        init = tuple((jnp.full((bt, 1), NEG_INF, F32), jnp.zeros((bt, LANES), F32)) for _ in range(2))
        carry = lax.fori_loop(0, i, blk, init)
        outs = []
        for (m, acc), qh, sl in zip(diag_blk(carry), qhs, sls):
            l = acc[:, L_ONE:L_ONE + 1]
            outs.append(acc * (1.0 / l))
            ql_ref[:, sl] = _put3(qh.astype(F32), L_LSE, -(m + jnp.log(l) * LOG2E)).astype(BF16)
        o_ref[...] = jnp.where(_lane((1, LANES)) < 64, outs[0], pltpu.roll(outs[1], 64, axis=1)).astype(BF16)
        if shards:
            pl.when(step == nb * 4 * nq - 1)(finish)

    in_specs = [pl.BlockSpec((bt, 2 * LANES), lambda b, j, i: (b * nq + i, j)),
                pl.BlockSpec((s, 2 * LANES), lambda b, j, i: (b, j)),
                pl.BlockSpec((s, 2 * LANES), lambda b, j, i: (b, j))]
    out_specs = [pl.BlockSpec((bt, LANES), lambda b, j, i: (b * nq + i, j)),
                 pl.BlockSpec((bt, 2 * LANES), lambda b, j, i: (b * nq + i, j))]
    out_shape = [jax.ShapeDtypeStruct((t, 512), BF16), jax.ShapeDtypeStruct((t, 8 * LANES), BF16)]
    args, scratch = [q_aug, k_aug, v_aug, *shards], []
    if shards:
        in_specs += [ANY] * n_sh
        out_specs += [ANY] * n_sh
        out_shape += [jax.ShapeDtypeStruct((N_DEV,) + sh.shape, sh.dtype) for sh in shards]
        scratch = [pltpu.SemaphoreType.DMA((N_SEM * n_sh,)), pltpu.SemaphoreType.DMA((N_SEM * n_sh,)),
                   pltpu.SemaphoreType.DMA((n_sh,))]
    return pl.pallas_call(
        body, name="fox_fwd", grid=(nb, 4, nq), in_specs=in_specs, out_specs=out_specs, out_shape=out_shape,
        scratch_shapes=scratch, compiler_params=_params(("arbitrary", "arbitrary", "arbitrary")),
    )(*args)


def _fox_bwd(ql_aug, k_aug, v_aug, do_aug, nb, s, bt, exch=()):
    t = ql_aug.shape[0]
    nk = s // bt
    n_in, n_out, n_ex = 4, 3, len(exch)

    def body(*refs):
        q_ref, do_ref, k_ref, v_ref = refs[:n_in]
        dq_ref, dk_ref, dv_ref = refs[n_in + n_ex:n_in + n_ex + n_out]
        if exch:
            srcs = refs[n_in:n_in + n_ex]
            dsts = refs[n_in + n_ex + n_out:n_in + 2 * n_ex + n_out]
            start, finish = _exchange_steps(list(zip(srcs, dsts)), *refs[n_in + 2 * n_ex + n_out:])
            step = (pl.program_id(0) * 4 + pl.program_id(1)) * nk + pl.program_id(2)
            pl.when(step == 0)(start)
        kb_i = pl.program_id(2)

        @pl.when(kb_i == 0)
        def _():
            dq_ref[...] = jnp.zeros_like(dq_ref)

        row = lax.broadcasted_iota(jnp.int32, (bt, bt), 0)
        col = lax.broadcasted_iota(jnp.int32, (bt, bt), 1)
        sls = [slice(LANES * hh, LANES * (hh + 1)) for hh in range(2)]
        khs, vhs = [k_ref[:, sl] for sl in sls], [v_ref[:, sl] for sl in sls]

        def blk(qi, carry, diag):
            start = pl.multiple_of(qi * bt, bt)
            new = []
            for (dk_a, dv_a), kh, vh, sl in zip(carry, khs, vhs, sls):
                qblk, doblk = q_ref[pl.ds(start, bt), sl], do_ref[pl.ds(start, bt), sl]
                st = _nt(kh, qblk)
                if diag:
                    pt = jnp.where(col >= row, jnp.exp2(jnp.where(col >= row, st, 0.0)), 0.0)
                else:
                    pt = jnp.exp2(st)
                dst = pt * _nt(vh, doblk)
                ptb, dstb = pt.astype(BF16), dst.astype(BF16)
                dv_a = dv_a + jnp.dot(ptb, doblk, preferred_element_type=F32)
                dk_a = dk_a + jnp.dot(dstb, qblk, preferred_element_type=F32)
                dq_ref[pl.ds(start, bt), sl] += _tn(dstb, kh)
                new.append((dk_a, dv_a))
            return tuple(new)

        zero = jnp.zeros((bt, LANES), F32)
        carry = blk(kb_i, ((zero, zero), (zero, zero)), True)
        carry = lax.fori_loop(kb_i + 1, nk, lambda qi, c: blk(qi, c, False), carry)
        for (dk_acc, dv_acc), sl in zip(carry, sls):
            dk_ref[:, sl] = dk_acc
            dv_ref[:, sl] = dv_acc
        if exch:
            pl.when(step == nb * 4 * nk - 1)(finish)

    scratch = []
    if exch:
        scratch = [pltpu.SemaphoreType.DMA((N_SEM * n_ex,)), pltpu.SemaphoreType.DMA((N_SEM * n_ex,)),
                   pltpu.SemaphoreType.DMA((n_ex,))]
    whole = pl.BlockSpec((s, 2 * LANES), lambda b, j, kb_i: (b, j))
    tile = pl.BlockSpec((bt, 2 * LANES), lambda b, j, kb_i: (b * nk + kb_i, j))
    shp = jax.ShapeDtypeStruct((t, 8 * LANES), F32)
    return pl.pallas_call(
        body, name="fox_bwd", grid=(nb, 4, nk),
        in_specs=[whole, whole, tile, tile] + [ANY] * n_ex,
        out_specs=[whole, tile, tile] + [ANY] * n_ex,
        out_shape=[shp, shp, shp] + [jax.ShapeDtypeStruct(e.shape, e.dtype) for e in exch],
        scratch_shapes=scratch, compiler_params=_params(("arbitrary", "arbitrary", "arbitrary")),
    )(ql_aug, do_aug, k_aug, v_aug, *exch)


FF_BLK = D_FF // N_DEV


def _mlp_fwd(x2, ma, mb, tgt, w_out, g2, w_up, w_down, tm):
    t = x2.shape[0]

    def body(x_ref, ma_ref, mb_ref, tg_ref, wo_ref, g2_ref, wu_ref, wd_ref,
             h_ref, hn_ref, hid_ref, dy_ref, dyb_ref, loss_ref):
        @pl.when(pl.program_id(0) == 0)
        def _():
            loss_ref[...] = jnp.zeros_like(loss_ref)

        h = (x_ref[...] + jnp.dot(ma_ref[...], wo_ref[0:512, :], preferred_element_type=F32)
             + jnp.dot(mb_ref[...], wo_ref[512:1024, :], preferred_element_type=F32))
        h_ref[...] = h
        r = lax.rsqrt(jnp.mean(h * h, axis=-1, keepdims=True) + EPS)
        hn = (h * r * g2_ref[...]).astype(BF16)
        hn_ref[...] = hn
        for d in range(N_DEV):
            u = jnp.maximum(jnp.dot(hn, wu_ref[d], preferred_element_type=F32), 0.0)
            hid_ref[:, FF_BLK * d:FF_BLK * (d + 1)] = (u * u).astype(BF16)
        y = h + jnp.dot(hid_ref[...], wd_ref[...], preferred_element_type=F32)
        err = y - tg_ref[...]
        dy = err * (1.0 / D_MODEL)
        dy_ref[...] = dy
        dyb_ref[...] = dy.astype(BF16)
        part =0.5 * jnp.sum(jnp.sum(err * err, axis=1, keepdims=True) * (1.0 / D_MODEL), axis=0, keepdims=True)
        loss_ref[...] += part

    def tile(w):
        return pl.BlockSpec((tm, w), lambda i: (i, 0))

    return pl.pallas_call(
        body, name="mlp_fwd", grid=(t // tm,),
        in_specs=[tile(D_MODEL), tile(512), tile(512), tile(D_MODEL), _const_spec((D_MODEL, D_MODEL)),
                  _const_spec((1, D_MODEL)), _const_spec((N_DEV, D_MODEL, FF_BLK)), _const_spec((D_FF, D_MODEL))],
        out_specs=[tile(D_MODEL), tile(D_MODEL), tile(D_FF), tile(D_MODEL), tile(D_MODEL),
                   pl.BlockSpec((8, LANES), lambda i: (0, 0))],
        out_shape=[jax.ShapeDtypeStruct((t, D_MODEL), F32), jax.ShapeDtypeStruct((t, D_MODEL), BF16),
                   jax.ShapeDtypeStruct((t, D_FF), BF16), jax.ShapeDtypeStruct((t, D_MODEL), F32),
                   jax.ShapeDtypeStruct((t, D_MODEL), BF16), jax.ShapeDtypeStruct((8, LANES), F32)],
        compiler_params=_params(("arbitrary",)),
    )(x2, ma, mb, tgt, w_out, g2, w_up, w_down)


def _mlp_bwd(dy, hid, h, ma, mb, w_down, w_up_t, w_out, g2, tm):
    t = dy.shape[0]

    def body(dy_ref, hid_ref, h_ref, ma_ref, mb_ref, wd_ref, wut_ref, wo_ref, g2_ref,
             du_ref, dh_ref, dhb_ref, dma_ref, dob_ref, dla_ref, gg_ref):
        @pl.when(pl.program_id(0) == 0)
        def _():
            gg_ref[...] = jnp.zeros_like(gg_ref)

        dy = dy_ref[...]
        d_hid = _nt(dy.astype(BF16), wd_ref[...])
        du = (d_hid * (2.0 * jnp.sqrt(hid_ref[...].astype(F32)))).astype(BF16)
        du_ref[...] = du
        d_hn = jnp.dot(du, wut_ref[...], preferred_element_type=F32)
        h = h_ref[...]
        r = lax.rsqrt(jnp.mean(h * h, axis=-1, keepdims=True) + EPS)
        hat = h * r
        gd = d_hn * g2_ref[...]
        dh = dy + r * (gd - hat * jnp.mean(gd * hat, axis=-1, keepdims=True))
        gg_ref[...] += jnp.sum(d_hn * hat, axis=0, keepdims=True)
        dh_ref[...] = dh
        dhb = dh.astype(BF16)
        dhb_ref[...] = dhb
        dm = _nt(dhb, wo_ref[...]).astype(BF16)
        dma, dmb = dm[:, 0:512], dm[:, 512:1024]
        dma_ref[...] = dma
        sel = (lax.shift_right_logical(lax.broadcasted_iota(jnp.int32, (512, LANES), 0), 6)
               == lax.broadcasted_iota(jnp.int32, (512, LANES), 1)).astype(BF16)
        dla_ref[...] = _split_dot(dma.astype(F32) * ma_ref[...].astype(F32), sel)
        dmb32 = dmb.astype(F32)
        dlb = _split_dot(dmb32 * mb_ref[...].astype(F32), sel)
        for hd in range(8):
            blk = _head_block(dmb32[:, LANES * (hd // 2):LANES * (hd // 2 + 1)], hd % 2)
            dob_ref[:, LANES * hd:LANES * (hd + 1)] = _put3(blk, L_DELTA, -dlb[:, hd:hd + 1]).astype(BF16)

    def tile(w):
        return pl.BlockSpec((tm, w), lambda i: (i, 0))

    return pl.pallas_call(
        body, name="mlp_bwd", grid=(t // tm,),
        in_specs=[tile(D_MODEL), tile(D_FF), tile(D_MODEL), tile(512), tile(512), _const_spec((D_FF, D_MODEL)),
                  _const_spec((D_FF, D_MODEL)), _const_spec((D_MODEL, D_MODEL)), _const_spec((1, D_MODEL))],
        out_specs=[tile(D_FF), tile(D_MODEL), tile(D_MODEL), tile(512), tile(8 * LANES), tile(LANES),
                   pl.BlockSpec((1, D_MODEL), lambda i: (0, 0))],
        out_shape=[jax.ShapeDtypeStruct((t, D_FF), BF16), jax.ShapeDtypeStruct((t, D_MODEL), F32),
                   jax.ShapeDtypeStruct((t, D_MODEL), BF16), jax.ShapeDtypeStruct((t, 512), BF16),
                   jax.ShapeDtypeStruct((t, 8 * LANES), BF16), jax.ShapeDtypeStruct((t, LANES), F32),
                   jax.ShapeDtypeStruct((1, D_MODEL), F32)],
        compiler_params=_params(("arbitrary",), VMEM_LIMIT_WIDE),
    )(dy, hid, h, ma, mb, w_down, w_up_t, w_out, g2)


def _wgrad(a, b, name, bm, bn, tk, out_dtype=F32, col_blocks=False, a2=None):
    t, m = a.shape
    n = b.shape[1]
    bm, bn = min(bm, m), min(bn, n)
    nk = t // tk

    def body(*refs):
        if a2 is None:
            a_ref, b_ref, o_ref, acc = refs
        else:
            a_ref, b_ref, a2_ref, o_ref, o2_ref, acc, acc2 = refs
        i, k = pl.program_id(0), pl.program_id(2)

        @pl.when(k == 0)
        def _():
            acc[...] = jnp.zeros_like(acc)

        acc[...] += _tn(a_ref[...], b_ref[...])

        @pl.when(k == nk - 1)
        def _():
            o_ref[...] = acc[...].astype(out_dtype)

        if a2 is not None:
            @pl.when((i == 0) & (k == 0))
            def _():
                acc2[...] = jnp.zeros_like(acc2)

            @pl.when(i == 0)
            def _():
                acc2[...] += _tn(a2_ref[...], b_ref[...])

            @pl.when((i == 0) & (k == nk - 1))
            def _():
                o2_ref[...] = acc2[...]

    if col_blocks:
        out_spec = pl.BlockSpec((None, bm, bn), lambda i, j, k: (j, i, 0))
        out_shape = jax.ShapeDtypeStruct((n // bn, m, bn), out_dtype)
    else:
        out_spec = pl.BlockSpec((bm, bn), lambda i, j, k: (i, j))
        out_shape = jax.ShapeDtypeStruct((m, n), out_dtype)
    in_specs = [pl.BlockSpec((tk, bm), lambda i, j, k: (k, i)), pl.BlockSpec((tk, bn), lambda i, j, k: (k, j))]
    out_specs, out_shapes, scratch, args = [out_spec], [out_shape], [pltpu.VMEM((bm, bn), F32)], [a, b]
    if a2 is not None:
        m2 = a2.shape[1]
        in_specs.append(pl.BlockSpec((tk, m2), lambda i, j, k: (k, 0)))
        out_specs.append(pl.BlockSpec((m2, n), lambda i, j, k: (0, 0)))
        out_shapes.append(jax.ShapeDtypeStruct((m2, n), F32))
        scratch.append(pltpu.VMEM((m2, n), F32))
        args.append(a2)
    out = pl.pallas_call(
        body, name=name, grid=(m // bm, n // bn, nk), in_specs=in_specs, out_specs=out_specs, out_shape=out_shapes,
        scratch_shapes=scratch, compiler_params=_params(("arbitrary", "arbitrary", "arbitrary")),
    )(*args)
    return out[0] if a2 is None else out


def _proj_bwd(raw, dqa, dkae, dvae, dqb, dkb, dvb, fl, bf_row, x2, dh, w_main_t, w_f_t, g1, gqa, gka, gqb, gkb, nb, s, tm):
    t = x2.shape[0]
    nt = s // tm

    def body(raw_ref, dqa_ref, dkae_ref, dvae_ref, dqb_ref, dkb_ref, dvb_ref, fl_ref, b_ref, x_ref, dh_ref,
             wmt_ref, wft_ref, g1_ref, gqa_ref, gka_ref, gqb_ref, gkb_ref,
             dx_ref, dp_ref, dfb_ref, ggqa_ref, ggka_ref, ggqb_ref, ggkb_ref, gg1_ref, gb_ref, carry, dlf_ref):
        @pl.when((pl.program_id(0) == 0) & (pl.program_id(1) == 0))
        def _():
            for r in (ggqa_ref, ggka_ref, ggqb_ref, ggkb_ref, gg1_ref, gb_ref):
                r[...] = jnp.zeros_like(r)

        @pl.when(pl.program_id(1) == 0)
        def _():
            carry[...] = jnp.zeros_like(carry)

        lane = _lane((tm, LANES))
        dc = jnp.zeros((tm, LANES), F32)
        for hd in range(8):
            col = (dqb_ref[:, LANES * hd + L_CQ:LANES * hd + L_CQ + 1] - dkb_ref[:, LANES * hd + L_CK:LANES * hd + L_CK + 1])
            dc = jnp.where(lane == hd, col, dc)
        dlf_ref[...] = _tri_dot(tm, True, dc) + carry[...]
        carry[...] = dlf_ref[pl.ds(0, 1), :]
        dfl = dlf_ref[...] * (1.0 / (1.0 + jnp.exp(fl_ref[...] + b_ref[...])))
        gb_ref[...] += jnp.sum(dfl, axis=0, keepdims=True)

        raw = raw_ref[...]
        d_qa, p_qa = _head_norm_bwd(raw[:, 0:512], gqa_ref[...], dqa_ref[...])
        d_ka, p_ka = _head_norm_bwd(raw[:, 512:640], gka_ref[...], _fold_kv(dkae_ref[...]))
        d_va = _fold_kv(dvae_ref[...])
        d_qb, p_qb = _head_norm_bwd(raw[:, 768:1280], gqb_ref[...], _to_pairs(dqb_ref) * SCALE)
        d_kb, p_kb = _head_norm_bwd(raw[:, 1280:1792], gkb_ref[...], _to_pairs(dkb_ref) * (1.0 / LOG2E))
        ggqa_ref[...] += jnp.sum(p_qa, axis=0, keepdims=True)
        ggka_ref[...] += jnp.sum(p_ka, axis=0, keepdims=True)
        ggqb_ref[...] += jnp.sum(p_qb, axis=0, keepdims=True)
        ggkb_ref[...] += jnp.sum(p_kb, axis=0, keepdims=True)
        dproj = jnp.concatenate([d_qa, d_ka, d_va, d_qb, d_kb, _to_pairs(dvb_ref)], axis=1).astype(BF16)
        dp_ref[...] = dproj
        dfb = dfl.astype(BF16)
        dfb_ref[...] = dfb
        d_xn = (jnp.dot(dproj, wmt_ref[...], preferred_element_type=F32)
                + jnp.dot(dfb, wft_ref[...], preferred_element_type=F32))
        x = x_ref[...]
        r = lax.rsqrt(jnp.mean(x * x, axis=-1, keepdims=True) + EPS)
        hat = x * r
        gd = d_xn * g1_ref[...]
        dx_ref[...] = dh_ref[...] + r * (gd - hat * jnp.mean(gd * hat, axis=-1, keepdims=True))
        gg1_ref[...] += jnp.sum(d_xn * hat, axis=0, keepdims=True)

    def tile(w):
        return pl.BlockSpec((tm, w), lambda b, i: (b * nt + (nt - 1 - i), 0))

    def acc(w):
        return pl.BlockSpec((1, w), lambda b, i: (0, 0))

    return pl.pallas_call(
        body, name="proj_bwd", grid=(nb, nt),
        in_specs=[tile(MAIN_W), tile(512), tile(512), tile(512), tile(8 * LANES), tile(8 * LANES), tile(8 * LANES), tile(LANES),
                  _const_spec((1, LANES)), tile(D_MODEL), tile(D_MODEL), _const_spec((MAIN_W, D_MODEL)),
                  _const_spec((LANES, D_MODEL)), _const_spec((1, D_MODEL)), _const_spec((1, 512)), _const_spec((1, 128)),
                  _const_spec((1, 512)), _const_spec((1, 512))],
        out_specs=[tile(D_MODEL), tile(MAIN_W), tile(LANES), acc(512), acc(128), acc(512), acc(512), acc(D_MODEL), acc(LANES)],
        out_shape=[jax.ShapeDtypeStruct((t, D_MODEL), F32), jax.ShapeDtypeStruct((t, MAIN_W), BF16),
                   jax.ShapeDtypeStruct((t, LANES), BF16), jax.ShapeDtypeStruct((1, 512), F32),
                   jax.ShapeDtypeStruct((1, 128), F32), jax.ShapeDtypeStruct((1, 512), F32),
                   jax.ShapeDtypeStruct((1, 512), F32), jax.ShapeDtypeStruct((1, D_MODEL), F32),
                   jax.ShapeDtypeStruct((1, LANES), F32)],
        scratch_shapes=[pltpu.VMEM((1, LANES), F32), pltpu.VMEM((tm, LANES), F32)],
        compiler_params=_params(("arbitrary", "arbitrary"), VMEM_LIMIT_WIDE),
    )(raw, dqa, dkae, dvae, dqb, dkb, dvb, fl, bf_row, x2, dh, w_main_t, w_f_t, g1, gqa, gka, gqb, gkb)


IN_PAD = 304


def _local_step(x, tgt, w_in_t, rest, g1, b_forget, qna, kna, sinks, qnb, knb, g2,
                tm=512, bt=1024, btf=1024, tq=4096, wk=4096, distributed=False):
    nb, s, _ = x.shape
    t = nb * s
    x2, tgt2 = x.reshape(t, D_MODEL), tgt.reshape(t, D_MODEL)
    g1r, g2r = g1.reshape(1, D_MODEL), g2.reshape(1, D_MODEL)
    gqa, gka = jnp.tile(qna, 8).reshape(1, 512), jnp.tile(kna, 2).reshape(1, 128)
    gqb, gkb = jnp.tile(qnb, 8).reshape(1, 512), jnp.tile(knb, 8).reshape(1, 512)
    bf_row = jnp.pad(b_forget, (0, LANES - 8)).reshape(1, LANES)
    sink_row = jnp.pad(sinks, (0, LANES - 8)).reshape(1, LANES)
    w_main_t = w_in_t[0:MAIN_W]
    w_f_t = jnp.pad(w_in_t[MAIN_W:IN_W], ((0, LANES - 8), (0, 0)))

    xn, raw, fl, qa, kae, vae, q_aug, k_aug, v_aug = _norm_proj(x2, g1r, w_main_t, w_f_t, gqa, gka, gqb, gkb, bf_row, s, tm)
    ma, lse_a = _swa_fwd(qa, kae, vae, sink_row, nb, s, tq)
    if distributed:
        mb, ql_aug, w_out, w_up, w_down, w_up_t = _fox_fwd(q_aug, k_aug, v_aug, nb, s, btf, shards=rest)
    else:
        mb, ql_aug = _fox_fwd(q_aug, k_aug, v_aug, nb, s, btf)
        w_out, w_up, w_down, w_up_t = rest
    w_out, w_down = w_out.reshape(D_MODEL, D_MODEL), w_down.reshape(D_FF, D_MODEL)
    h, hn, hid, dy, dyb, loss_acc = _mlp_fwd(x2, ma, mb, tgt2, w_out, g2r, w_up, w_down, tm)

    du, dh, dhb, dma, do_aug, dla, gg2 = _mlp_bwd(dy, hid, h, ma, mb, w_down, w_up_t.reshape(D_FF, D_MODEL), w_out, g2r, tm)
    g_down = _wgrad(hid, dyb, "wgrad_down", 512, 1024, wk, BF16).reshape(N_DEV, 512, D_MODEL)
    g_up = _wgrad(hn, du, "wgrad_up", 1024, 512, wk, BF16, col_blocks=True)
    g_out = jnp.concatenate([_wgrad(ma, dhb, "wgrad_out_a", 512, 1024, wk, BF16),
                             _wgrad(mb, dhb, "wgrad_out_b", 512, 1024, wk, BF16)], axis=0).reshape(N_DEV, 128, D_MODEL)

    dqa, dkae, dvae, dsink = _swa_bwd(qa, kae, vae, dma, sink_row, lse_a, dla, nb, s, tq)
    fox = _fox_bwd(ql_aug, k_aug, v_aug, do_aug, nb, s, bt, exch=(g_out, g_up, g_down) if distributed else ())
    dqb, dkb, dvb = fox[:3]
    if distributed:
        g_out, g_up, g_down = fox[3:]
    grad_x, dproj, dfb, ggqa, ggka, ggqb, ggkb, gg1, gbf = _proj_bwd(
        raw, dqa, dkae, dvae, dqb, dkb, dvb, fl, bf_row, x2, dh, w_main_t, w_f_t, g1r, gqa, gka, gqb, gkb, nb, s, tm)
    g_main_t, g_gate_t = _wgrad(dproj, xn, "wgrad_in", 768, 1024, wk, a2=dfb)
    g_in_t = jnp.concatenate([g_main_t, g_gate_t[0:8]], axis=0)

    small = (gg1.reshape(D_MODEL), gbf[0, 0:8], ggqa.reshape(8, 64).sum(0), ggka.reshape(2, 64).sum(0),
             dsink.sum(0)[:, 0:2, 0].reshape(8), ggqb.reshape(8, 64).sum(0), ggkb.reshape(8, 64).sum(0),
             gg2.reshape(D_MODEL))
    return loss_acc[0, 0], grad_x.reshape(nb, s, D_MODEL), g_in_t, g_out, g_up, g_down, small


def _all_gather(shard):
    def body(x_ref, out_ref, send_sems, recv_sems, local_sem):
        start, forward, finish = _gather_steps([(x_ref, out_ref)], send_sems, recv_sems, local_sem)
        start()
        forward()
        finish()

    return pl.pallas_call(
        body, name="gather_w_in", out_shape=jax.ShapeDtypeStruct((N_DEV,) + shard.shape, shard.dtype),
        in_specs=[ANY], out_specs=ANY,
        scratch_shapes=[pltpu.SemaphoreType.DMA((N_SEM,)), pltpu.SemaphoreType.DMA((N_SEM,)), pltpu.SemaphoreType.DMA((1,))],
    )(shard)


def _exchange(*arrays):
    n_ex = len(arrays)

    def body(*refs):
        start, finish = _exchange_steps(list(zip(refs[:n_ex], refs[n_ex:2 * n_ex])), *refs[2 * n_ex:])
        start()
        finish()

    return pl.pallas_call(
        body, name="exchange_tail", out_shape=[jax.ShapeDtypeStruct(a.shape, a.dtype) for a in arrays],
        in_specs=[ANY] * n_ex, out_specs=[ANY] * n_ex,
        scratch_shapes=[pltpu.SemaphoreType.DMA((N_SEM * n_ex,)), pltpu.SemaphoreType.DMA((N_SEM * n_ex,)),
                        pltpu.SemaphoreType.DMA((n_ex,))],
    )(*arrays)


def _sum_adamw(recv, w, m, v, tr, name):
    _, r, n = recv.shape

    def body(r_ref, w_ref, m_ref, v_ref, g_ref, d_ref, nm_ref, nv_ref):
        g = r_ref[0].astype(F32)
        for s in range(1, N_DEV):
            g = g + r_ref[s].astype(F32)
        g_ref[...] = g
        nm = ADAM_B1 * m_ref[...] + (1.0 - ADAM_B1) * g
        nv = ADAM_B2 * v_ref[...] + (1.0 - ADAM_B2) * (g * g)
        m_hat = nm / (1.0 - ADAM_B1 ** ADAM_STEP)
        v_hat = nv / (1.0 - ADAM_B2 ** ADAM_STEP)
        d_ref[...] = -ADAM_LR * (m_hat / (jnp.sqrt(v_hat) + ADAM_EPS) + ADAM_WD * w_ref[...])
        nm_ref[...] = nm
        nv_ref[...] = nv

    tile = pl.BlockSpec((tr, n), lambda i: (i, 0))
    shp = jax.ShapeDtypeStruct((r, n), F32)
    return pl.pallas_call(
        body, name=name, grid=(r // tr,),
        in_specs=[pl.BlockSpec((N_DEV, tr, n), lambda i: (0, i, 0)), tile, tile, tile],
        out_specs=[tile, tile, tile, tile], out_shape=[shp, shp, shp, shp],
        compiler_params=_params(("arbitrary",)),
    )(recv, w, m, v)


def _small_rows(g1, bf, qna, kna, sk, qnb, knb, g2):
    row2 = jnp.concatenate([bf, qna, kna, sk, qnb, knb])
    return jnp.zeros((8, D_MODEL), F32).at[0].set(g1).at[1].set(g2).at[2, 0:row2.shape[0]].set(row2)


def _in_rows(w_in_s):
    return jnp.pad(w_in_s.T, ((0, IN_PAD - IN_SHARD), (0, 0)))


def kernel(x, attn_norm_g, w_in, b_forget, q_norm_a, k_norm_a, sink_logits, q_norm_b, k_norm_b, w_out, mlp_norm_g, w_up, w_down, loss_target, m_attn_norm_g, m_w_in, m_b_forget, m_q_norm_a, m_k_norm_a, m_sink_logits, m_q_norm_b, m_k_norm_b, m_w_out, m_mlp_norm_g, m_w_up, m_w_down, v_attn_norm_g, v_w_in, v_b_forget, v_q_norm_a, v_k_norm_a, v_sink_logits, v_q_norm_b, v_k_norm_b, v_w_out, v_mlp_norm_g, v_w_up, v_w_down):
    w_in_r = _in_rows(w_in)
    w_in_t = _all_gather(w_in_r.astype(BF16))[:, 0:IN_SHARD].reshape(IN_W, D_MODEL)
    w_up_b = w_up.astype(BF16)
    rest = (w_out.astype(BF16), w_up_b, w_down.astype(BF16), w_up_b.T)

    loss_part, grad_x, g_in_t, r_out, r_up, r_down, small = _local_step(
        x, loss_target, w_in_t, rest, attn_norm_g, b_forget, q_norm_a, k_norm_a, sink_logits, q_norm_b, k_norm_b, mlp_norm_g,
        distributed=True)

    g_in_blocks = jnp.pad(g_in_t.reshape(N_DEV, IN_SHARD, D_MODEL), ((0, 0), (0, IN_PAD - IN_SHARD), (0, 0))).astype(BF16)
    small_blocks = jnp.broadcast_to(_small_rows(*small).at[3, 0].set(loss_part), (N_DEV, 8, D_MODEL))
    r_in, r_small = _exchange(g_in_blocks, small_blocks)

    small_w = _small_rows(attn_norm_g, b_forget, q_norm_a, k_norm_a, sink_logits, q_norm_b, k_norm_b, mlp_norm_g)
    small_m = _small_rows(m_attn_norm_g, m_b_forget, m_q_norm_a, m_k_norm_a, m_sink_logits, m_q_norm_b, m_k_norm_b, m_mlp_norm_g)
    small_v = _small_rows(v_attn_norm_g, v_b_forget, v_q_norm_a, v_k_norm_a, v_sink_logits, v_q_norm_b, v_k_norm_b, v_mlp_norm_g)
    o_in = [a[0:IN_SHARD].T for a in _sum_adamw(r_in, w_in_r, _in_rows(m_w_in), _in_rows(v_w_in), IN_PAD, "adamw_in")]
    o_out = _sum_adamw(r_out, w_out, m_w_out, v_w_out, 128, "adamw_out")
    o_up = _sum_adamw(r_up, w_up, m_w_up, v_w_up, 256, "adamw_up")
    o_down = _sum_adamw(r_down, w_down, m_w_down, v_w_down, 128, "adamw_down")
    o_small = _sum_adamw(r_small, small_w, small_m, small_v, 8, "adamw_small")

    def leaves(i):
        row2 = o_small[i][2]
        return (o_small[i][0], o_in[i], row2[0:8], row2[8:72], row2[72:136], row2[136:144], row2[144:208], row2[208:272],
                o_out[i], o_small[i][1], o_up[i], o_down[i])

    return (o_small[0][3, 0], grad_x, *leaves(0), *leaves(1), *leaves(2), *leaves(3))
```

```python
import functools

import jax
import jax.numpy as jnp
from jax import lax
from jax.experimental import pallas as pl
from jax.experimental.pallas import tpu as pltpu

F32 = jnp.float32
BF16 = jnp.bfloat16

D_MODEL = 1024
HEAD_DIM = 64
N_DEV = 8
D_FF = 4096
MAIN_W = 2304
IN_W = 2312
IN_SHARD = 289
WINDOW = 128
EPS = 1e-6
SCALE = 0.125
LOG2E = 1.4426950408889634
LANES = 128
NEG_INF = float("-inf")

ADAM_LR = 0.001
ADAM_B1 = 0.9
ADAM_B2 = 0.999
ADAM_EPS = 1e-08
ADAM_WD = 0.01
ADAM_STEP = 10

VMEM_LIMIT = 56 * 1024 * 1024
VMEM_LIMIT_WIDE = 62 * 1024 * 1024


def _params(sem, vmem=VMEM_LIMIT):
    return pltpu.CompilerParams(dimension_semantics=sem, vmem_limit_bytes=vmem)


def _const_spec(shape):
    nd = len(shape)
    return pl.BlockSpec(shape, lambda *_: (0,) * nd, pipeline_mode=pl.Buffered(1))


def _lane(shape):
    return lax.broadcasted_iota(jnp.int32, shape, len(shape) - 1)


def _split_dot(v, mat):
    hi = v.astype(BF16)
    lo = (v - hi.astype(F32)).astype(BF16)
    return (jnp.dot(hi, mat, preferred_element_type=F32) + jnp.dot(lo, mat, preferred_element_type=F32))


def _head_ones(n):
    r = lax.shift_right_logical(lax.broadcasted_iota(jnp.int32, (n, n), 0), 6)
    c = lax.shift_right_logical(lax.broadcasted_iota(jnp.int32, (n, n), 1), 6)
    return (r == c).astype(BF16)


def _head_sum(v):
    w = v.shape[1]
    vb = v.astype(BF16)
    if w <= 256:
        return jnp.dot(vb, _head_ones(w), preferred_element_type=F32)
    ones = _head_ones(256)
    return jnp.concatenate([jnp.dot(vb[:, s:s + 256], ones, preferred_element_type=F32) for s in range(0, w, 256)], axis=1)


def _head_norm(seg, gain):
    rs = lax.rsqrt(_head_sum(seg * seg) * (1.0 / HEAD_DIM) + EPS)
    return seg * rs * gain


def _head_norm_bwd(seg, gain, d_out):
    rs = lax.rsqrt(_head_sum(seg * seg) * (1.0 / HEAD_DIM) + EPS)
    hat = seg * rs
    gd = d_out * gain
    d_seg = rs * (gd - hat * (_head_sum(gd * hat) * (1.0 / HEAD_DIM)))
    return d_seg, d_out * hat


def _expand_kv(v):
    r = pltpu.roll(v, 64, axis=1)
    lo = _lane(v.shape) < 64
    return jnp.concatenate([jnp.where(lo, v, r), jnp.where(lo, r, v)], axis=1)


def _fold_kv(e4):
    t0 = e4[:, 0:128] + e4[:, 128:256]
    t1 = e4[:, 256:384] + e4[:, 384:512]
    t0 = t0 + pltpu.roll(t0, 64, axis=1)
    t1 = t1 + pltpu.roll(t1, 64, axis=1)
    return jnp.where(_lane(t0.shape) < 64, t0, t1)


def _pick_lane(blk, idx):
    return jnp.sum(jnp.where(_lane(blk.shape) == idx, blk, 0.0), axis=1, keepdims=True)


def _nt(a, b):
    return lax.dot_general(a, b, (((1,), (1,)), ((), ())), preferred_element_type=F32)


def _tn(a, b):
    return lax.dot_general(a, b, (((0,), (0,)), ((), ())), preferred_element_type=F32)


def _norm_proj(x2, g1, w_main_t, w_f_t, gqa, gka, gqb, gkb, bf_row, s, tm):
    t = x2.shape[0]
    nt = s // tm

    def body(x_ref, g1_ref, wm_ref, wf_ref, gqa_ref, gka_ref, gqb_ref, gkb_ref, b_ref,
             xn_ref, raw_ref, fl_ref, qa_ref, kae_ref, vae_ref, qo_ref, ko_ref, vo_ref, carry, c_ref):
        @pl.when(lax.rem(pl.program_id(0), nt) == 0)
        def _():
            carry[...] = jnp.zeros_like(carry)

        x = x_ref[...]
        r = lax.rsqrt(jnp.mean(x * x, axis=-1, keepdims=True) + EPS)
        xn = (x * r * g1_ref[...]).astype(BF16)
        xn_ref[...] = xn
        proj = _nt(xn, wm_ref[...])
        raw_ref[...] = proj
        fl = _nt(xn, wf_ref[...])
        fl_ref[...] = fl
        qa_ref[...] = _head_norm(proj[:, 0:512], gqa_ref[...]).astype(BF16)
        kae_ref[...] = _expand_kv(_head_norm(proj[:, 512:640], gka_ref[...])).astype(BF16)
        vae_ref[...] = _expand_kv(proj[:, 640:768]).astype(BF16)

        z = fl + b_ref[...]
        e = jnp.exp(-jnp.abs(z))
        u = 1.0 + e
        log1p = jnp.where(u == 1.0, e, jnp.log(u) * (e / (u - 1.0)))
        lf = jnp.minimum(z, 0.0) - log1p
        for r0 in range(0, tm, 256):
            c_ref[r0:r0 + 256, :] = _tri_dot(256, False, lf[r0:r0 + 256]) + carry[...]
            carry[...] = c_ref[pl.ds(r0 + 255, 1), :]
        c2 = c_ref[...] * LOG2E
        qb = _head_norm(proj[:, 768:1280], gqb_ref[...]) * (SCALE * LOG2E)
        kb = _head_norm(proj[:, 1280:1792], gkb_ref[...])
        lane = _lane((tm, LANES))
        for h in range(8):
            j, half = h // 2, h % 2
            pair, blk = slice(LANES * j, LANES * (j + 1)), slice(LANES * h, LANES * (h + 1))
            feat = _spread3(c2[:, h:h + 1], (tm, LANES), (L_CK, L_CQ))
            q = _put_ones(_head_block(qb[:, pair], half), (L_CK, L_CK + 1, L_CK + 2))
            qo_ref[:, blk] = jnp.where((lane >= L_CQ) & (lane < L_CQ + 3), feat, q).astype(BF16)
            k = _put_ones(_head_block(kb[:, pair], half), tuple(range(L_CQ, L_CQ + 6)))
            ko_ref[:, blk] = jnp.where((lane >= L_CK) & (lane < L_CK + 3), -feat, k).astype(BF16)
            v = _head_block(proj[:, 1792 + LANES * j:1792 + LANES * (j + 1)], half)
            vo_ref[:, blk] = _put_ones(v, (L_ONE, L_DELTA, L_DELTA + 1, L_DELTA + 2)).astype(BF16)

    def tile(w):
        return pl.BlockSpec((tm, w), lambda i: (i, 0))

    aug = jax.ShapeDtypeStruct((t, 8 * LANES), BF16)
    return pl.pallas_call(
        body, name="norm_proj", grid=(t // tm,),
        in_specs=[tile(D_MODEL), _const_spec((1, D_MODEL)), _const_spec((MAIN_W, D_MODEL)), _const_spec((LANES, D_MODEL)),
                  _const_spec((1, 512)), _const_spec((1, 128)), _const_spec((1, 512)), _const_spec((1, 512)),
                  _const_spec((1, LANES))],
        out_specs=[tile(D_MODEL), tile(MAIN_W), tile(LANES), tile(512), tile(256), tile(256)] + [tile(8 * LANES)] * 3,
        out_shape=[jax.ShapeDtypeStruct((t, D_MODEL), BF16), jax.ShapeDtypeStruct((t, MAIN_W), F32),
                   jax.ShapeDtypeStruct((t, LANES), F32), jax.ShapeDtypeStruct((t, 512), BF16),
                   jax.ShapeDtypeStruct((t, 256), BF16), jax.ShapeDtypeStruct((t, 256), BF16), aug, aug, aug],
        scratch_shapes=[pltpu.VMEM((1, LANES), F32), pltpu.VMEM((tm, LANES), F32)],
        compiler_params=_params(("arbitrary",)),
    )(x2, g1, w_main_t, w_f_t, gqa, gka, gqb, gkb, bf_row)


def _tri_dot(n, upper, v):
    r = lax.broadcasted_iota(jnp.int32, (n, n), 0)
    c = lax.broadcasted_iota(jnp.int32, (n, n), 1)
    tri = ((c >= r) if upper else (c <= r)).astype(BF16)
    hi = v.astype(BF16)
    mid = (v - hi.astype(F32)).astype(BF16)
    lo = (v - hi.astype(F32) - mid.astype(F32)).astype(BF16)
    return (jnp.dot(tri, hi, preferred_element_type=F32) + jnp.dot(tri, mid, preferred_element_type=F32)
            + jnp.dot(tri, lo, preferred_element_type=F32))


def _slope(p, hh):
    out = jnp.float32(2.0 ** -(2 * 3 + hh + 1))
    for pp in (2, 1, 0):
        out = jnp.where(p == pp, jnp.float32(2.0 ** -(2 * pp + hh + 1)), out)
    return out


def _swa_windows(ref, i, tq):
    nsub = tq // WINDOW
    cur = ref[pl.ds(pl.multiple_of(i * tq, tq), tq), :].reshape(nsub, WINDOW, LANES)
    first = ref[pl.ds(pl.multiple_of(jnp.maximum(i * tq - WINDOW, 0), WINDOW), WINDOW), :].reshape(1, WINDOW, LANES)
    return jnp.concatenate([jnp.concatenate([first, cur[0:nsub - 1]], axis=0), cur], axis=1)


def _both_heads(x3, lo):
    zero = jnp.zeros_like(x3)
    return jnp.concatenate([jnp.where(lo, x3, zero), jnp.where(lo, zero, x3)], axis=0)


def _swa_head_consts(sink_ref, p, i, nsub):
    bidx = lax.broadcasted_iota(jnp.int32, (2 * nsub, 1, 1), 0)
    is_a = bidx < nsub
    slope = jnp.where(is_a, _slope(p, 0), _slope(p, 1))
    sinks = sink_ref[...]
    sink = jnp.where(is_a, _pick_lane(sinks, 2 * p).reshape(1, 1, 1), _pick_lane(sinks, 2 * p + 1).reshape(1, 1, 1))
    first = (i == 0) & ((bidx == 0) | (bidx == nsub))
    return slope, sink, first


def _swa_fwd(qa, kae, vae, sink_row, nb, s, tq):
    t = qa.shape[0]
    nq = s // tq
    nsub = tq // WINDOW

    def body(q_ref, k_ref, v_ref, sink_ref, o_ref, lse_ref):
        p, i = pl.program_id(1), pl.program_id(2)
        lo = _lane((1, 1, LANES)) < 64
        kk, vv = _swa_windows(k_ref, i, tq), _swa_windows(v_ref, i, tq)
        qs = (q_ref[...].astype(F32) * SCALE).astype(BF16).reshape(nsub, WINDOW, LANES)
        q8 = _both_heads(qs, lo)
        s8 = jnp.einsum("bqd,bkd->bqk", q8, jnp.concatenate([kk, kk], axis=0), preferred_element_type=F32)
        row = lax.broadcasted_iota(jnp.int32, (1, WINDOW, 2 * WINDOW), 1)
        col = lax.broadcasted_iota(jnp.int32, (1, WINDOW, 2 * WINDOW), 2)
        dist = row + WINDOW - col
        slope, sink, first = _swa_head_consts(sink_ref, p, i, nsub)
        valid = (dist >= 0) & (dist < WINDOW) & ((col >= WINDOW) | jnp.logical_not(first))
        s8 = jnp.where(valid, s8 - slope * dist.astype(F32), NEG_INF)
        m = jnp.maximum(jnp.max(s8, axis=2, keepdims=True), sink)
        e = jnp.exp(s8 - m)
        den = jnp.sum(e, axis=2, keepdims=True) + jnp.exp(sink - m)
        pr = (e * (1.0 / den)).astype(BF16)
        o8 = jnp.einsum("bqk,bkd->bqd", pr, jnp.concatenate([vv, vv], axis=0), preferred_element_type=F32)
        lse8 = m + jnp.log(den)
        o_ref[...] = jnp.where(lo, o8[0:nsub], o8[nsub:]).astype(BF16).reshape(tq, LANES)
        lse_ref[...] = jnp.where(lo, lse8[0:nsub], lse8[nsub:]).reshape(tq, LANES)

    return pl.pallas_call(
        body, name="swa_fwd", grid=(nb, 4, nq),
        in_specs=[pl.BlockSpec((tq, LANES), lambda b, p, i: (b * nq + i, p)),
                  pl.BlockSpec((s, LANES), lambda b, p, i: (b, lax.shift_right_logical(p, 1))),
                  pl.BlockSpec((s, LANES), lambda b, p, i: (b, lax.shift_right_logical(p, 1))),
                  pl.BlockSpec((1, LANES), lambda b, p, i: (0, 0))],
        out_specs=[pl.BlockSpec((tq, LANES), lambda b, p, i: (b * nq + i, p)),
                   pl.BlockSpec((None, tq, LANES), lambda b, p, i: (p, b * nq + i, 0))],
        out_shape=[jax.ShapeDtypeStruct((t, 512), BF16), jax.ShapeDtypeStruct((4, t, LANES), F32)],
        compiler_params=_params(("arbitrary", "arbitrary", "arbitrary")),
    )(qa, kae, vae, sink_row)


def _swa_bwd(qa, kae, vae, do_a, sink_row, lse, delta, nb, s, tq):
    t = qa.shape[0]
    nq = s // tq
    nsub = tq // WINDOW

    def body(q_ref, do_ref, k_ref, v_ref, sink_ref, lse_ref, dl_ref, dq_ref, dk_ref, dv_ref, ds_ref):
        p, i = pl.program_id(1), pl.program_id(2)

        @pl.when(i == 0)
        def _():
            ds_ref[...] = jnp.zeros_like(ds_ref)

        lo = _lane((1, 1, LANES)) < 64
        kk, vv = _swa_windows(k_ref, i, tq), _swa_windows(v_ref, i, tq)
        kks = (kk.astype(F32) * SCALE).astype(BF16)
        k8, v8 = jnp.concatenate([kks, kks], axis=0), jnp.concatenate([vv, vv], axis=0)
        q8 = _both_heads(q_ref[...].reshape(nsub, WINDOW, LANES), lo)
        do8 = _both_heads(do_ref[...].reshape(nsub, WINDOW, LANES), lo)
        cur = pl.multiple_of(i * tq, tq)
        sub = lax.broadcasted_iota(jnp.int32, (WINDOW, WINDOW), 0)
        lse_t = [lse_ref[u * WINDOW:(u + 1) * WINDOW, :].T for u in range(nsub)]
        dl_t = [dl_ref[u * WINDOW:(u + 1) * WINDOW, :].T for u in range(nsub)]
        lse8 = jnp.concatenate([t_[64 * hh:64 * hh + 1, :].reshape(1, 1, WINDOW) for hh in range(2) for t_ in lse_t], axis=0)
        dl8 = jnp.concatenate([jnp.sum(jnp.where(sub == 2 * p + hh, t_, 0.0), axis=0, keepdims=True).reshape(1, 1, WINDOW)
                               for hh in range(2) for t_ in dl_t], axis=0)
        row = lax.broadcasted_iota(jnp.int32, (1, 2 * WINDOW, WINDOW), 1)
        col = lax.broadcasted_iota(jnp.int32, (1, 2 * WINDOW, WINDOW), 2)
        dist = col + WINDOW - row
        slope, sink, first = _swa_head_consts(sink_ref, p, i, nsub)
        valid = (dist >= 0) & (dist < WINDOW) & ((row >= WINDOW) | jnp.logical_not(first))
        st = jnp.einsum("bkd,bqd->bkq", k8, q8, preferred_element_type=F32) - slope * dist.astype(F32) - lse8
        pt = jnp.where(valid, jnp.exp(jnp.where(valid, st, 0.0)), 0.0)
        dpt = jnp.einsum("bkd,bqd->bkq", v8, do8, preferred_element_type=F32)
        dst = pt * (dpt - dl8)
        ptb, dstb = pt.astype(BF16), dst.astype(BF16)
        dv8 = jnp.einsum("bkq,bqd->bkd", ptb, do8, preferred_element_type=F32)
        dk8 = jnp.einsum("bkq,bqd->bkd", dstb, q8, preferred_element_type=F32) * SCALE
        dq8 = jnp.einsum("bkq,bkd->bqd", dstb, k8, preferred_element_type=F32)
        dq_ref[...] = jnp.where(lo, dq8[0:nsub], dq8[nsub:]).reshape(tq, LANES)

        psd = jnp.exp(sink - lse8) * dl8
        row_h = lax.broadcasted_iota(jnp.int32, (8, LANES), 0)
        for hh in range(2):
            tot = jnp.sum(jnp.sum(psd[hh * nsub:(hh + 1) * nsub], axis=2, keepdims=True), axis=0, keepdims=True)
            ds_ref[...] += jnp.where(row_h == hh, -tot.reshape(1, 1), 0.0)

        prev = pl.multiple_of(jnp.maximum(i * tq - WINDOW, 0), WINDOW)
        for g8, g_ref in ((dk8, dk_ref), (dv8, dv_ref)):
            g4 = g8[0:nsub] + g8[nsub:]
            own, before = g4[:, WINDOW:, :], g4[:, 0:WINDOW, :]
            shifted = jnp.concatenate([before[1:nsub], jnp.zeros((1, WINDOW, LANES), F32)], axis=0)
            g_ref[pl.ds(cur, tq), :] = (own + shifted).reshape(tq, LANES)
            g_ref[pl.ds(prev, WINDOW), :] += before[0]

    return pl.pallas_call(
        body, name="swa_bwd", grid=(nb, 4, nq),
        in_specs=[pl.BlockSpec((tq, LANES), lambda b, p, i: (b * nq + i, p)),
                  pl.BlockSpec((tq, LANES), lambda b, p, i: (b * nq + i, p)),
                  pl.BlockSpec((s, LANES), lambda b, p, i: (b, lax.shift_right_logical(p, 1))),
                  pl.BlockSpec((s, LANES), lambda b, p, i: (b, lax.shift_right_logical(p, 1))),
                  pl.BlockSpec((1, LANES), lambda b, p, i: (0, 0)),
                  pl.BlockSpec((None, tq, LANES), lambda b, p, i: (p, b * nq + i, 0)),
                  pl.BlockSpec((tq, LANES), lambda b, p, i: (b * nq + i, 0))],
        out_specs=[pl.BlockSpec((tq, LANES), lambda b, p, i: (b * nq + i, p)),
                   pl.BlockSpec((s, LANES), lambda b, p, i: (b, p)),
                   pl.BlockSpec((s, LANES), lambda b, p, i: (b, p)),
                   pl.BlockSpec((None, None, 8, LANES), lambda b, p, i: (b, p, 0, 0))],
        out_shape=[jax.ShapeDtypeStruct((t, 512), F32), jax.ShapeDtypeStruct((t, 512), F32),
                   jax.ShapeDtypeStruct((t, 512), F32), jax.ShapeDtypeStruct((nb, 4, 8, LANES), F32)],
        compiler_params=_params(("arbitrary", "arbitrary", "arbitrary")),
    )(qa, do_a, kae, vae, sink_row, lse, delta)


MESH = pl.DeviceIdType.MESH
ANY = pl.BlockSpec(memory_space=pl.ANY)
N_SEM = 7


def _gather_steps(pairs, send_sems, recv_sems, local_sems):
    x, y, c = lax.axis_index("x"), lax.axis_index("y"), lax.axis_index("c")
    me, sibling = (x, y, c), (x, y, 1 - c)
    chips = [(1 - x, y), (x, 1 - y), (1 - x, 1 - y)]
    mine, first, passed, landed, last = [], [], [], [], []
    for a, (x_ref, out_ref) in enumerate(pairs):
        def slot(px, py, pc, out_ref=out_ref):
            return out_ref.at[4 * px + 2 * py + pc]

        def copy(k, block, to, src=None, a=a, slot=slot):
            return pltpu.make_async_remote_copy(
                src_ref=slot(*block) if src is None else src, dst_ref=slot(*block),
                send_sem=send_sems.at[N_SEM * a + k], recv_sem=recv_sems.at[N_SEM * a + k], device_id=to, device_id_type=MESH)

        mine.append(pltpu.make_async_copy(x_ref, slot(*me), local_sems.at[a]))
        first += [copy(0, me, sibling, src=x_ref)] + [copy(1 + j, me, (*chip, c), src=x_ref) for j, chip in enumerate(chips)]
        passed += [copy(4 + j, (*chip, c), sibling) for j, chip in enumerate(chips)]
        landed += [copy(1 + j, (*chip, c), me) for j, chip in enumerate(chips)]
        last += [copy(0, sibling, me)] + [copy(4 + j, (*chip, 1 - c), me) for j, chip in enumerate(chips)]

    def start():
        for cp in mine + first:
            cp.start()

    def forward():
        for arrived, onward in zip(landed, passed):
            arrived.wait_recv()
            onward.start()

    def finish():
        for cp in last:
            cp.wait_recv()
        for cp in first + passed:
            cp.wait_send()
        for cp in mine:
            cp.wait()

    return start, forward, finish


def _exchange_steps(pairs, send_sems, recv_sems, local_sems):
    x, y, c = lax.axis_index("x"), lax.axis_index("y"), lax.axis_index("c")
    my_id = 4 * x + 2 * y + c
    local, remote = [], []
    for a, (src, dst) in enumerate(pairs):
        local.append(pltpu.make_async_copy(src.at[my_id], dst.at[my_id], local_sems.at[a]))
        for k in range(1, N_DEV):
            px = 1 - x if k & 4 else x
            py = 1 - y if k & 2 else y
            pc = 1 - c if k & 1 else c
            remote.append(pltpu.make_async_remote_copy(
                src_ref=src.at[4 * px + 2 * py + pc], dst_ref=dst.at[my_id],
                send_sem=send_sems.at[N_SEM * a + k - 1], recv_sem=recv_sems.at[N_SEM * a + k - 1],
                device_id=(px, py, pc), device_id_type=MESH))

    def start():
        for cp in local + remote:
            cp.start()

    def finish():
        for cp in remote:
            cp.wait_recv()
        for cp in remote:
            cp.wait_send()
        for cp in local:
            cp.wait()

    return start, finish


L_ONE = 64
L_CK = 65
L_CQ = 68
L_LSE = 71
L_DELTA = 74


def _head_block(pair, half):
    y = pair if half == 0 else pltpu.roll(pair, 64, axis=1)
    return jnp.where(_lane(pair.shape) < 64, y, 0.0)


def _put3(blk, lane0, col):
    lane = _lane(blk.shape)
    hi = col.astype(BF16).astype(F32)
    mid = (col - hi).astype(BF16).astype(F32)
    lo = (col - hi - mid).astype(BF16).astype(F32)
    return jnp.where(lane == lane0, hi, jnp.where(lane == lane0 + 1, mid, jnp.where(lane == lane0 + 2, lo, blk)))


def _spread3(col, shape, lane0s):
    lane = _lane(shape)
    hi = col.astype(BF16).astype(F32)
    mid = (col - hi).astype(BF16).astype(F32)
    lo = (col - hi - mid).astype(BF16).astype(F32)

    def at(k):
        return functools.reduce(jnp.logical_or, [lane == ln + k for ln in lane0s])

    return jnp.where(at(0), hi, jnp.where(at(1), mid, jnp.where(at(2), lo, 0.0)))


def _put_ones(blk, lanes):
    lane = _lane(blk.shape)
    hit = functools.reduce(jnp.logical_or, [lane == ln for ln in lanes])
    return jnp.where(hit, 1.0, blk)


def _to_pairs(ref):
    out = []
    for j in range(4):
        a, b = ref[:, 2 * LANES * j:2 * LANES * j + LANES], ref[:, 2 * LANES * j + LANES:2 * LANES * (j + 1)]
        out.append(jnp.where(_lane(a.shape) < 64, a, pltpu.roll(b, 64, axis=1)))
    return jnp.concatenate(out, axis=1)


def _fox_fwd(q_aug, k_aug, v_aug, nb, s, bt, shards=()):
    t = q_aug.shape[0]
    nq = s // bt
    n_in, n_sh = 3, len(shards)

    def body(*refs):
        q_ref, k_ref, v_ref = refs[:n_in]
        o_ref, ql_ref = refs[n_in + n_sh:n_in + n_sh + 2]
        if shards:
            srcs, dsts = refs[n_in:n_in + n_sh], refs[n_in + n_sh + 2:n_in + 2 * n_sh + 2]
            start, forward, finish = _gather_steps(list(zip(srcs, dsts)), *refs[n_in + 2 * n_sh + 2:])
            step = (pl.program_id(0) * 4 + pl.program_id(1)) * nq + pl.program_id(2)
            pl.when(step == 0)(start)
            pl.when(step == nb * 3 * nq)(forward)
        i = pl.program_id(2)
        sls = [slice(LANES * hh, LANES * (hh + 1)) for hh in range(2)]
        qhs = [q_ref[:, sl] for sl in sls]

        def update(m, acc, qrows, start, size, sl, causal):
            sc = _nt(qrows, k_ref[pl.ds(start, size), sl])
            if causal:
                row = lax.broadcasted_iota(jnp.int32, sc.shape, 0)
                col = lax.broadcasted_iota(jnp.int32, sc.shape, 1)
                sc = jnp.where(row >= col, sc, NEG_INF)
            m_new = jnp.maximum(m, jnp.max(sc, axis=1, keepdims=True))
            pr = jnp.exp2(sc - m_new).astype(BF16)
            acc = jnp.exp2(m - m_new) * acc + jnp.dot(pr, v_ref[pl.ds(start, size), sl], preferred_element_type=F32)
            return m_new, acc

        def blk(kb_i, carry):
            start = pl.multiple_of(kb_i * bt, bt)
            return tuple(update(m, acc, qh, start, bt, sl, False) for (m, acc), qh, sl in zip(carry, qhs, sls))

        def diag_blk(carry):
            start = pl.multiple_of(i * bt, bt)
            return tuple(update(m, acc, qh, start, bt, sl, True) for (m, acc), qh, sl in zip(carry, qhs, sls))

        init = tuple((jnp.full((bt, 1), NEG_INF, F32), jnp.zeros((bt, LANES), F32)) for _ in range(2))
        carry = lax.fori_loop(0, i, blk, init)
        outs = []
        for (m, acc), qh, sl in zip(diag_blk(carry), qhs, sls):
            l = acc[:, L_ONE:L_ONE + 1]
            outs.append(acc * (1.0 / l))
            ql_ref[:, sl] = _put3(qh.astype(F32), L_LSE, -(m + jnp.log(l) * LOG2E)).astype(BF16)
        o_ref[...] = jnp.where(_lane((1, LANES)) < 64, outs[0], pltpu.roll(outs[1], 64, axis=1)).astype(BF16)
        if shards:
            pl.when(step == nb * 4 * nq - 1)(finish)

    in_specs = [pl.BlockSpec((bt, 2 * LANES), lambda b, j, i: (b * nq + i, j)),
                pl.BlockSpec((s, 2 * LANES), lambda b, j, i: (b, j)),
                pl.BlockSpec((s, 2 * LANES), lambda b, j, i: (b, j))]
    out_specs = [pl.BlockSpec((bt, LANES), lambda b, j, i: (b * nq + i, j)),
                 pl.BlockSpec((bt, 2 * LANES), lambda b, j, i: (b * nq + i, j))]
    out_shape = [jax.ShapeDtypeStruct((t, 512), BF16), jax.ShapeDtypeStruct((t, 8 * LANES), BF16)]
    args, scratch = [q_aug, k_aug, v_aug, *shards], []
    if shards:
        in_specs += [ANY] * n_sh
        out_specs += [ANY] * n_sh
        out_shape += [jax.ShapeDtypeStruct((N_DEV,) + sh.shape, sh.dtype) for sh in shards]
        scratch = [pltpu.SemaphoreType.DMA((N_SEM * n_sh,)), pltpu.SemaphoreType.DMA((N_SEM * n_sh,)),
                   pltpu.SemaphoreType.DMA((n_sh,))]
    return pl.pallas_call(
        body, name="fox_fwd", grid=(nb, 4, nq), in_specs=in_specs, out_specs=out_specs, out_shape=out_shape,
        scratch_shapes=scratch, compiler_params=_params(("arbitrary", "arbitrary", "arbitrary")),
    )(*args)


def _fox_bwd(ql_aug, k_aug, v_aug, do_aug, nb, s, bt, exch=()):
    t = ql_aug.shape[0]
    nk = s // bt
    n_in, n_out, n_ex = 4, 3, len(exch)

    def body(*refs):
        q_ref, do_ref, k_ref, v_ref = refs[:n_in]
        dq_ref, dk_ref, dv_ref = refs[n_in + n_ex:n_in + n_ex + n_out]
        if exch:
            srcs = refs[n_in:n_in + n_ex]
            dsts = refs[n_in + n_ex + n_out:n_in + 2 * n_ex + n_out]
            start, finish = _exchange_steps(list(zip(srcs, dsts)), *refs[n_in + 2 * n_ex + n_out:])
            step = (pl.program_id(0) * 4 + pl.program_id(1)) * nk + pl.program_id(2)
            pl.when(step == 0)(start)
        kb_i = pl.program_id(2)

        @pl.when(kb_i == 0)
        def _():
            dq_ref[...] = jnp.zeros_like(dq_ref)

        row = lax.broadcasted_iota(jnp.int32, (bt, bt), 0)
        col = lax.broadcasted_iota(jnp.int32, (bt, bt), 1)
        sls = [slice(LANES * hh, LANES * (hh + 1)) for hh in range(2)]
        khs, vhs = [k_ref[:, sl] for sl in sls], [v_ref[:, sl] for sl in sls]

        def blk(qi, carry, diag):
            start = pl.multiple_of(qi * bt, bt)
            new = []
            for (dk_a, dv_a), kh, vh, sl in zip(carry, khs, vhs, sls):
                qblk, doblk = q_ref[pl.ds(start, bt), sl], do_ref[pl.ds(start, bt), sl]
                st = _nt(kh, qblk)
                if diag:
                    pt = jnp.where(col >= row, jnp.exp2(jnp.where(col >= row, st, 0.0)), 0.0)
                else:
                    pt = jnp.exp2(st)
                dst = pt * _nt(vh, doblk)
                ptb, dstb = pt.astype(BF16), dst.astype(BF16)
                dv_a = dv_a + jnp.dot(ptb, doblk, preferred_element_type=F32)
                dk_a = dk_a + jnp.dot(dstb, qblk, preferred_element_type=F32)
                dq_ref[pl.ds(start, bt), sl] += _tn(dstb, kh)
                new.append((dk_a, dv_a))
            return tuple(new)

        zero = jnp.zeros((bt, LANES), F32)
        carry = blk(kb_i, ((zero, zero), (zero, zero)), True)
        carry = lax.fori_loop(kb_i + 1, nk, lambda qi, c: blk(qi, c, False), carry)
        for (dk_acc, dv_acc), sl in zip(carry, sls):
            dk_ref[:, sl] = dk_acc
            dv_ref[:, sl] = dv_acc
        if exch:
            pl.when(step == nb * 4 * nk - 1)(finish)

    scratch = []
    if exch:
        scratch = [pltpu.SemaphoreType.DMA((N_SEM * n_ex,)), pltpu.SemaphoreType.DMA((N_SEM * n_ex,)),
                   pltpu.SemaphoreType.DMA((n_ex,))]
    whole = pl.BlockSpec((s, 2 * LANES), lambda b, j, kb_i: (b, j))
    tile = pl.BlockSpec((bt, 2 * LANES), lambda b, j, kb_i: (b * nk + kb_i, j))
    shp = jax.ShapeDtypeStruct((t, 8 * LANES), F32)
    return pl.pallas_call(
        body, name="fox_bwd", grid=(nb, 4, nk),
        in_specs=[whole, whole, tile, tile] + [ANY] * n_ex,
        out_specs=[whole, tile, tile] + [ANY] * n_ex,
        out_shape=[shp, shp, shp] + [jax.ShapeDtypeStruct(e.shape, e.dtype) for e in exch],
        scratch_shapes=scratch, compiler_params=_params(("arbitrary", "arbitrary", "arbitrary")),
    )(ql_aug, do_aug, k_aug, v_aug, *exch)


FF_BLK = D_FF // N_DEV


def _mlp_fwd(x2, ma, mb, tgt, w_out, g2, w_up, w_down, tm):
    t = x2.shape[0]

    def body(x_ref, ma_ref, mb_ref, tg_ref, wo_ref, g2_ref, wu_ref, wd_ref,
             h_ref, hn_ref, hid_ref, dy_ref, dyb_ref, loss_ref):
        @pl.when(pl.program_id(0) == 0)
        def _():
            loss_ref[...] = jnp.zeros_like(loss_ref)

        h = (x_ref[...] + jnp.dot(ma_ref[...], wo_ref[0:512, :], preferred_element_type=F32)
             + jnp.dot(mb_ref[...], wo_ref[512:1024, :], preferred_element_type=F32))
        h_ref[...] = h
        r = lax.rsqrt(jnp.mean(h * h, axis=-1, keepdims=True) + EPS)
        hn = (h * r * g2_ref[...]).astype(BF16)
        hn_ref[...] = hn
        for d in range(N_DEV):
            u = jnp.maximum(jnp.dot(hn, wu_ref[d], preferred_element_type=F32), 0.0)
            hid_ref[:, FF_BLK * d:FF_BLK * (d + 1)] = (u * u).astype(BF16)
        y = h + jnp.dot(hid_ref[...], wd_ref[...], preferred_element_type=F32)
        err = y - tg_ref[...]
        dy = err * (1.0 / D_MODEL)
        dy_ref[...] = dy
        dyb_ref[...] = dy.astype(BF16)
        part =0.5 * jnp.sum(jnp.sum(err * err, axis=1, keepdims=True) * (1.0 / D_MODEL), axis=0, keepdims=True)
        loss_ref[...] += part

    def tile(w):
        return pl.BlockSpec((tm, w), lambda i: (i, 0))

    return pl.pallas_call(
        body, name="mlp_fwd", grid=(t // tm,),
        in_specs=[tile(D_MODEL), tile(512), tile(512), tile(D_MODEL), _const_spec((D_MODEL, D_MODEL)),
                  _const_spec((1, D_MODEL)), _const_spec((N_DEV, D_MODEL, FF_BLK)), _const_spec((D_FF, D_MODEL))],
        out_specs=[tile(D_MODEL), tile(D_MODEL), tile(D_FF), tile(D_MODEL), tile(D_MODEL),
                   pl.BlockSpec((8, LANES), lambda i: (0, 0))],
        out_shape=[jax.ShapeDtypeStruct((t, D_MODEL), F32), jax.ShapeDtypeStruct((t, D_MODEL), BF16),
                   jax.ShapeDtypeStruct((t, D_FF), BF16), jax.ShapeDtypeStruct((t, D_MODEL), F32),
                   jax.ShapeDtypeStruct((t, D_MODEL), BF16), jax.ShapeDtypeStruct((8, LANES), F32)],
        compiler_params=_params(("arbitrary",)),
    )(x2, ma, mb, tgt, w_out, g2, w_up, w_down)


def _mlp_bwd(dy, hid, h, ma, mb, w_down, w_up_t, w_out, g2, tm):
    t = dy.shape[0]

    def body(dy_ref, hid_ref, h_ref, ma_ref, mb_ref, wd_ref, wut_ref, wo_ref, g2_ref,
             du_ref, dh_ref, dhb_ref, dma_ref, dob_ref, dla_ref, gg_ref):
        @pl.when(pl.program_id(0) == 0)
        def _():
            gg_ref[...] = jnp.zeros_like(gg_ref)

        dy = dy_ref[...]
        d_hid = _nt(dy.astype(BF16), wd_ref[...])
        du = (d_hid * (2.0 * jnp.sqrt(hid_ref[...].astype(F32)))).astype(BF16)
        du_ref[...] = du
        d_hn = jnp.dot(du, wut_ref[...], preferred_element_type=F32)
        h = h_ref[...]
        r = lax.rsqrt(jnp.mean(h * h, axis=-1, keepdims=True) + EPS)
        hat = h * r
        gd = d_hn * g2_ref[...]
        dh = dy + r * (gd - hat * jnp.mean(gd * hat, axis=-1, keepdims=True))
        gg_ref[...] += jnp.sum(d_hn * hat, axis=0, keepdims=True)
        dh_ref[...] = dh
        dhb = dh.astype(BF16)
        dhb_ref[...] = dhb
        dm = _nt(dhb, wo_ref[...]).astype(BF16)
        dma, dmb = dm[:, 0:512], dm[:, 512:1024]
        dma_ref[...] = dma
        sel = (lax.shift_right_logical(lax.broadcasted_iota(jnp.int32, (512, LANES), 0), 6)
               == lax.broadcasted_iota(jnp.int32, (512, LANES), 1)).astype(BF16)
        dla_ref[...] = _split_dot(dma.astype(F32) * ma_ref[...].astype(F32), sel)
        dmb32 = dmb.astype(F32)
        dlb = _split_dot(dmb32 * mb_ref[...].astype(F32), sel)
        for hd in range(8):
            blk = _head_block(dmb32[:, LANES * (hd // 2):LANES * (hd // 2 + 1)], hd % 2)
            dob_ref[:, LANES * hd:LANES * (hd + 1)] = _put3(blk, L_DELTA, -dlb[:, hd:hd + 1]).astype(BF16)

    def tile(w):
        return pl.BlockSpec((tm, w), lambda i: (i, 0))

    return pl.pallas_call(
        body, name="mlp_bwd", grid=(t // tm,),
        in_specs=[tile(D_MODEL), tile(D_FF), tile(D_MODEL), tile(512), tile(512), _const_spec((D_FF, D_MODEL)),
                  _const_spec((D_FF, D_MODEL)), _const_spec((D_MODEL, D_MODEL)), _const_spec((1, D_MODEL))],
        out_specs=[tile(D_FF), tile(D_MODEL), tile(D_MODEL), tile(512), tile(8 * LANES), tile(LANES),
                   pl.BlockSpec((1, D_MODEL), lambda i: (0, 0))],
        out_shape=[jax.ShapeDtypeStruct((t, D_FF), BF16), jax.ShapeDtypeStruct((t, D_MODEL), F32),
                   jax.ShapeDtypeStruct((t, D_MODEL), BF16), jax.ShapeDtypeStruct((t, 512), BF16),
                   jax.ShapeDtypeStruct((t, 8 * LANES), BF16), jax.ShapeDtypeStruct((t, LANES), F32),
                   jax.ShapeDtypeStruct((1, D_MODEL), F32)],
        compiler_params=_params(("arbitrary",), VMEM_LIMIT_WIDE),
    )(dy, hid, h, ma, mb, w_down, w_up_t, w_out, g2)


def _wgrad(a, b, name, bm, bn, tk, out_dtype=F32, col_blocks=False, a2=None):
    t, m = a.shape
    n = b.shape[1]
    bm, bn = min(bm, m), min(bn, n)
    nk = t // tk

    def body(*refs):
        if a2 is None:
            a_ref, b_ref, o_ref, acc = refs
        else:
            a_ref, b_ref, a2_ref, o_ref, o2_ref, acc, acc2 = refs
        i, k = pl.program_id(0), pl.program_id(2)

        @pl.when(k == 0)
        def _():
            acc[...] = jnp.zeros_like(acc)

        acc[...] += _tn(a_ref[...], b_ref[...])

        @pl.when(k == nk - 1)
        def _():
            o_ref[...] = acc[...].astype(out_dtype)

        if a2 is not None:
            @pl.when((i == 0) & (k == 0))
            def _():
                acc2[...] = jnp.zeros_like(acc2)

            @pl.when(i == 0)
            def _():
                acc2[...] += _tn(a2_ref[...], b_ref[...])

            @pl.when((i == 0) & (k == nk - 1))
            def _():
                o2_ref[...] = acc2[...]

    if col_blocks:
        out_spec = pl.BlockSpec((None, bm, bn), lambda i, j, k: (j, i, 0))
        out_shape = jax.ShapeDtypeStruct((n // bn, m, bn), out_dtype)
    else:
        out_spec = pl.BlockSpec((bm, bn), lambda i, j, k: (i, j))
        out_shape = jax.ShapeDtypeStruct((m, n), out_dtype)
    in_specs = [pl.BlockSpec((tk, bm), lambda i, j, k: (k, i)), pl.BlockSpec((tk, bn), lambda i, j, k: (k, j))]
    out_specs, out_shapes, scratch, args = [out_spec], [out_shape], [pltpu.VMEM((bm, bn), F32)], [a, b]
    if a2 is not None:
        m2 = a2.shape[1]
        in_specs.append(pl.BlockSpec((tk, m2), lambda i, j, k: (k, 0)))
        out_specs.append(pl.BlockSpec((m2, n), lambda i, j, k: (0, 0)))
        out_shapes.append(jax.ShapeDtypeStruct((m2, n), F32))
        scratch.append(pltpu.VMEM((m2, n), F32))
        args.append(a2)
    out = pl.pallas_call(
        body, name=name, grid=(m // bm, n // bn, nk), in_specs=in_specs, out_specs=out_specs, out_shape=out_shapes,
        scratch_shapes=scratch, compiler_params=_params(("arbitrary", "arbitrary", "arbitrary")),
    )(*args)
    return out[0] if a2 is None else out


def _proj_bwd(raw, dqa, dkae, dvae, dqb, dkb, dvb, fl, bf_row, x2, dh, w_main_t, w_f_t, g1, gqa, gka, gqb, gkb, nb, s, tm):
    t = x2.shape[0]
    nt = s // tm

    def body(raw_ref, dqa_ref, dkae_ref, dvae_ref, dqb_ref, dkb_ref, dvb_ref, fl_ref, b_ref, x_ref, dh_ref,
             wmt_ref, wft_ref, g1_ref, gqa_ref, gka_ref, gqb_ref, gkb_ref,
             dx_ref, dp_ref, dfb_ref, ggqa_ref, ggka_ref, ggqb_ref, ggkb_ref, gg1_ref, gb_ref, carry, dlf_ref):
        @pl.when((pl.program_id(0) == 0) & (pl.program_id(1) == 0))
        def _():
            for r in (ggqa_ref, ggka_ref, ggqb_ref, ggkb_ref, gg1_ref, gb_ref):
                r[...] = jnp.zeros_like(r)

        @pl.when(pl.program_id(1) == 0)
        def _():
            carry[...] = jnp.zeros_like(carry)

        lane = _lane((tm, LANES))
        dc = jnp.zeros((tm, LANES), F32)
        for hd in range(8):
            col = (dqb_ref[:, LANES * hd + L_CQ:LANES * hd + L_CQ + 1] - dkb_ref[:, LANES * hd + L_CK:LANES * hd + L_CK + 1])
            dc = jnp.where(lane == hd, col, dc)
        dlf_ref[...] = _tri_dot(tm, True, dc) + carry[...]
        carry[...] = dlf_ref[pl.ds(0, 1), :]
        dfl = dlf_ref[...] * (1.0 / (1.0 + jnp.exp(fl_ref[...] + b_ref[...])))
        gb_ref[...] += jnp.sum(dfl, axis=0, keepdims=True)

        raw = raw_ref[...]
        d_qa, p_qa = _head_norm_bwd(raw[:, 0:512], gqa_ref[...], dqa_ref[...])
        d_ka, p_ka = _head_norm_bwd(raw[:, 512:640], gka_ref[...], _fold_kv(dkae_ref[...]))
        d_va = _fold_kv(dvae_ref[...])
        d_qb, p_qb = _head_norm_bwd(raw[:, 768:1280], gqb_ref[...], _to_pairs(dqb_ref) * SCALE)
        d_kb, p_kb = _head_norm_bwd(raw[:, 1280:1792], gkb_ref[...], _to_pairs(dkb_ref) * (1.0 / LOG2E))
        ggqa_ref[...] += jnp.sum(p_qa, axis=0, keepdims=True)
        ggka_ref[...] += jnp.sum(p_ka, axis=0, keepdims=True)
        ggqb_ref[...] += jnp.sum(p_qb, axis=0, keepdims=True)
        ggkb_ref[...] += jnp.sum(p_kb, axis=0, keepdims=True)
        dproj = jnp.concatenate([d_qa, d_ka, d_va, d_qb, d_kb, _to_pairs(dvb_ref)], axis=1).astype(BF16)
        dp_ref[...] = dproj
        dfb = dfl.astype(BF16)
        dfb_ref[...] = dfb
        d_xn = (jnp.dot(dproj, wmt_ref[...], preferred_element_type=F32)
                + jnp.dot(dfb, wft_ref[...], preferred_element_type=F32))
        x = x_ref[...]
        r = lax.rsqrt(jnp.mean(x * x, axis=-1, keepdims=True) + EPS)
        hat = x * r
        gd = d_xn * g1_ref[...]
        dx_ref[...] = dh_ref[...] + r * (gd - hat * jnp.mean(gd * hat, axis=-1, keepdims=True))
        gg1_ref[...] += jnp.sum(d_xn * hat, axis=0, keepdims=True)

    def tile(w):
        return pl.BlockSpec((tm, w), lambda b, i: (b * nt + (nt - 1 - i), 0))

    def acc(w):
        return pl.BlockSpec((1, w), lambda b, i: (0, 0))

    return pl.pallas_call(
        body, name="proj_bwd", grid=(nb, nt),
        in_specs=[tile(MAIN_W), tile(512), tile(512), tile(512), tile(8 * LANES), tile(8 * LANES), tile(8 * LANES), tile(LANES),
                  _const_spec((1, LANES)), tile(D_MODEL), tile(D_MODEL), _const_spec((MAIN_W, D_MODEL)),
                  _const_spec((LANES, D_MODEL)), _const_spec((1, D_MODEL)), _const_spec((1, 512)), _const_spec((1, 128)),
                  _const_spec((1, 512)), _const_spec((1, 512))],
        out_specs=[tile(D_MODEL), tile(MAIN_W), tile(LANES), acc(512), acc(128), acc(512), acc(512), acc(D_MODEL), acc(LANES)],
        out_shape=[jax.ShapeDtypeStruct((t, D_MODEL), F32), jax.ShapeDtypeStruct((t, MAIN_W), BF16),
                   jax.ShapeDtypeStruct((t, LANES), BF16), jax.ShapeDtypeStruct((1, 512), F32),
                   jax.ShapeDtypeStruct((1, 128), F32), jax.ShapeDtypeStruct((1, 512), F32),
                   jax.ShapeDtypeStruct((1, 512), F32), jax.ShapeDtypeStruct((1, D_MODEL), F32),
                   jax.ShapeDtypeStruct((1, LANES), F32)],
        scratch_shapes=[pltpu.VMEM((1, LANES), F32), pltpu.VMEM((tm, LANES), F32)],
        compiler_params=_params(("arbitrary", "arbitrary"), VMEM_LIMIT_WIDE),
    )(raw, dqa, dkae, dvae, dqb, dkb, dvb, fl, bf_row, x2, dh, w_main_t, w_f_t, g1, gqa, gka, gqb, gkb)


IN_PAD = 304


def _local_step(x, tgt, w_in_t, rest, g1, b_forget, qna, kna, sinks, qnb, knb, g2,
                tm=512, bt=1024, btf=1024, tq=4096, wk=4096, wkb=8192, distributed=False):
    nb, s, _ = x.shape
    t = nb * s
    x2, tgt2 = x.reshape(t, D_MODEL), tgt.reshape(t, D_MODEL)
    g1r, g2r = g1.reshape(1, D_MODEL), g2.reshape(1, D_MODEL)
    gqa, gka = jnp.tile(qna, 8).reshape(1, 512), jnp.tile(kna, 2).reshape(1, 128)
    gqb, gkb = jnp.tile(qnb, 8).reshape(1, 512), jnp.tile(knb, 8).reshape(1, 512)
    bf_row = jnp.pad(b_forget, (0, LANES - 8)).reshape(1, LANES)
    sink_row = jnp.pad(sinks, (0, LANES - 8)).reshape(1, LANES)
    w_main_t = w_in_t[0:MAIN_W]
    w_f_t = jnp.pad(w_in_t[MAIN_W:IN_W], ((0, LANES - 8), (0, 0)))

    xn, raw, fl, qa, kae, vae, q_aug, k_aug, v_aug = _norm_proj(x2, g1r, w_main_t, w_f_t, gqa, gka, gqb, gkb, bf_row, s, tm)
    ma, lse_a = _swa_fwd(qa, kae, vae, sink_row, nb, s, tq)
    if distributed:
        mb, ql_aug, w_out, w_up, w_down, w_up_t = _fox_fwd(q_aug, k_aug, v_aug, nb, s, btf, shards=rest)
    else:
        mb, ql_aug = _fox_fwd(q_aug, k_aug, v_aug, nb, s, btf)
        w_out, w_up, w_down, w_up_t = rest
    w_out, w_down = w_out.reshape(D_MODEL, D_MODEL), w_down.reshape(D_FF, D_MODEL)
    h, hn, hid, dy, dyb, loss_acc = _mlp_fwd(x2, ma, mb, tgt2, w_out, g2r, w_up, w_down, tm)

    du, dh, dhb, dma, do_aug, dla, gg2 = _mlp_bwd(dy, hid, h, ma, mb, w_down, w_up_t.reshape(D_FF, D_MODEL), w_out, g2r, tm)
    g_down = _wgrad(hid, dyb, "wgrad_down", 512, 1024, wkb, BF16).reshape(N_DEV, 512, D_MODEL)
    g_up = _wgrad(hn, du, "wgrad_up", 1024, 512, wkb, BF16, col_blocks=True)
    g_out = jnp.concatenate([_wgrad(ma, dhb, "wgrad_out_a", 512, 1024, wkb, BF16),
                             _wgrad(mb, dhb, "wgrad_out_b", 512, 1024, wkb, BF16)], axis=0).reshape(N_DEV, 128, D_MODEL)

    dqa, dkae, dvae, dsink = _swa_bwd(qa, kae, vae, dma, sink_row, lse_a, dla, nb, s, tq)
    fox = _fox_bwd(ql_aug, k_aug, v_aug, do_aug, nb, s, bt, exch=(g_out, g_up, g_down) if distributed else ())
    dqb, dkb, dvb = fox[:3]
    if distributed:
        g_out, g_up, g_down = fox[3:]
    grad_x, dproj, dfb, ggqa, ggka, ggqb, ggkb, gg1, gbf = _proj_bwd(
        raw, dqa, dkae, dvae, dqb, dkb, dvb, fl, bf_row, x2, dh, w_main_t, w_f_t, g1r, gqa, gka, gqb, gkb, nb, s, tm)
    g_main_t, g_gate_t = _wgrad(dproj, xn, "wgrad_in", 768, 1024, wk, a2=dfb)
    g_in_t = jnp.concatenate([g_main_t, g_gate_t[0:8]], axis=0)

    small = (gg1.reshape(D_MODEL), gbf[0, 0:8], ggqa.reshape(8, 64).sum(0), ggka.reshape(2, 64).sum(0),
             dsink.sum(0)[:, 0:2, 0].reshape(8), ggqb.reshape(8, 64).sum(0), ggkb.reshape(8, 64).sum(0),
             gg2.reshape(D_MODEL))
    return loss_acc[0, 0], grad_x.reshape(nb, s, D_MODEL), g_in_t, g_out, g_up, g_down, small


def _all_gather(shard):
    def body(x_ref, out_ref, send_sems, recv_sems, local_sem):
        start, forward, finish = _gather_steps([(x_ref, out_ref)], send_sems, recv_sems, local_sem)
        start()
        forward()
        finish()

    return pl.pallas_call(
        body, name="gather_w_in", out_shape=jax.ShapeDtypeStruct((N_DEV,) + shard.shape, shard.dtype),
        in_specs=[ANY], out_specs=ANY,
        scratch_shapes=[pltpu.SemaphoreType.DMA((N_SEM,)), pltpu.SemaphoreType.DMA((N_SEM,)), pltpu.SemaphoreType.DMA((1,))],
    )(shard)


def _exchange(*arrays):
    n_ex = len(arrays)

    def body(*refs):
        start, finish = _exchange_steps(list(zip(refs[:n_ex], refs[n_ex:2 * n_ex])), *refs[2 * n_ex:])
        start()
        finish()

    return pl.pallas_call(
        body, name="exchange_tail", out_shape=[jax.ShapeDtypeStruct(a.shape, a.dtype) for a in arrays],
        in_specs=[ANY] * n_ex, out_specs=[ANY] * n_ex,
        scratch_shapes=[pltpu.SemaphoreType.DMA((N_SEM * n_ex,)), pltpu.SemaphoreType.DMA((N_SEM * n_ex,)),
                        pltpu.SemaphoreType.DMA((n_ex,))],
    )(*arrays)


def _sum_adamw(recv, w, m, v, tr, name):
    _, r, n = recv.shape

    def body(r_ref, w_ref, m_ref, v_ref, g_ref, d_ref, nm_ref, nv_ref):
        g = r_ref[0].astype(F32)
        for s in range(1, N_DEV):
            g = g + r_ref[s].astype(F32)
        g_ref[...] = g
        nm = ADAM_B1 * m_ref[...] + (1.0 - ADAM_B1) * g
        nv = ADAM_B2 * v_ref[...] + (1.0 - ADAM_B2) * (g * g)
        m_hat = nm / (1.0 - ADAM_B1 ** ADAM_STEP)
        v_hat = nv / (1.0 - ADAM_B2 ** ADAM_STEP)
        d_ref[...] = -ADAM_LR * (m_hat / (jnp.sqrt(v_hat) + ADAM_EPS) + ADAM_WD * w_ref[...])
        nm_ref[...] = nm
        nv_ref[...] = nv

    tile = pl.BlockSpec((tr, n), lambda i: (i, 0))
    shp = jax.ShapeDtypeStruct((r, n), F32)
    return pl.pallas_call(
        body, name=name, grid=(r // tr,),
        in_specs=[pl.BlockSpec((N_DEV, tr, n), lambda i: (0, i, 0)), tile, tile, tile],
        out_specs=[tile, tile, tile, tile], out_shape=[shp, shp, shp, shp],
        compiler_params=_params(("arbitrary",)),
    )(recv, w, m, v)


def _small_rows(g1, bf, qna, kna, sk, qnb, knb, g2):
    row2 = jnp.concatenate([bf, qna, kna, sk, qnb, knb])
    return jnp.zeros((8, D_MODEL), F32).at[0].set(g1).at[1].set(g2).at[2, 0:row2.shape[0]].set(row2)


def _in_rows(w_in_s):
    return jnp.pad(w_in_s.T, ((0, IN_PAD - IN_SHARD), (0, 0)))


def kernel(x, attn_norm_g, w_in, b_forget, q_norm_a, k_norm_a, sink_logits, q_norm_b, k_norm_b, w_out, mlp_norm_g, w_up, w_down, loss_target, m_attn_norm_g, m_w_in, m_b_forget, m_q_norm_a, m_k_norm_a, m_sink_logits, m_q_norm_b, m_k_norm_b, m_w_out, m_mlp_norm_g, m_w_up, m_w_down, v_attn_norm_g, v_w_in, v_b_forget, v_q_norm_a, v_k_norm_a, v_sink_logits, v_q_norm_b, v_k_norm_b, v_w_out, v_mlp_norm_g, v_w_up, v_w_down):
    w_in_r = _in_rows(w_in)
    w_in_t = _all_gather(w_in_r.astype(BF16))[:, 0:IN_SHARD].reshape(IN_W, D_MODEL)
    w_up_b = w_up.astype(BF16)
    rest = (w_out.astype(BF16), w_up_b, w_down.astype(BF16), w_up_b.T)

    loss_part, grad_x, g_in_t, r_out, r_up, r_down, small = _local_step(
        x, loss_target, w_in_t, rest, attn_norm_g, b_forget, q_norm_a, k_norm_a, sink_logits, q_norm_b, k_norm_b, mlp_norm_g,
        distributed=True)

    g_in_blocks = jnp.pad(g_in_t.reshape(N_DEV, IN_SHARD, D_MODEL), ((0, 0), (0, IN_PAD - IN_SHARD), (0, 0))).astype(BF16)
    small_blocks = jnp.broadcast_to(_small_rows(*small).at[3, 0].set(loss_part), (N_DEV, 8, D_MODEL))
    r_in, r_small = _exchange(g_in_blocks, small_blocks)

    small_w = _small_rows(attn_norm_g, b_forget, q_norm_a, k_norm_a, sink_logits, q_norm_b, k_norm_b, mlp_norm_g)
    small_m = _small_rows(m_attn_norm_g, m_b_forget, m_q_norm_a, m_k_norm_a, m_sink_logits, m_q_norm_b, m_k_norm_b, m_mlp_norm_g)
    small_v = _small_rows(v_attn_norm_g, v_b_forget, v_q_norm_a, v_k_norm_a, v_sink_logits, v_q_norm_b, v_k_norm_b, v_mlp_norm_g)
    o_in = [a[0:IN_SHARD].T for a in _sum_adamw(r_in, w_in_r, _in_rows(m_w_in), _in_rows(v_w_in), IN_PAD, "adamw_in")]
    o_out = _sum_adamw(r_out, w_out, m_w_out, v_w_out, 128, "adamw_out")
    o_up = _sum_adamw(r_up, w_up, m_w_up, v_w_up, 256, "adamw_up")
    o_down = _sum_adamw(r_down, w_down, m_w_down, v_w_down, 128, "adamw_down")
    o_small = _sum_adamw(r_small, small_w, small_m, small_v, 8, "adamw_small")

    def leaves(i):
        row2 = o_small[i][2]
        return (o_small[i][0], o_in[i], row2[0:8], row2[8:72], row2[72:136], row2[136:144], row2[144:208], row2[208:272],
                o_out[i], o_small[i][1], o_up[i], o_down[i])

    return (o_small[0][3, 0], grad_x, *leaves(0), *leaves(1), *leaves(2), *leaves(3))
```

```python
import functools

import jax
import jax.numpy as jnp
from jax import lax
from jax.experimental import pallas as pl
from jax.experimental.pallas import tpu as pltpu

F32 = jnp.float32
BF16 = jnp.bfloat16

D_MODEL = 1024
HEAD_DIM = 64
N_DEV = 8
D_FF = 4096
MAIN_W = 2304
IN_W = 2312
IN_SHARD = 289
WINDOW = 128
EPS = 1e-6
SCALE = 0.125
LOG2E = 1.4426950408889634
LANES = 128
NEG_INF = float("-inf")

ADAM_LR = 0.001
ADAM_B1 = 0.9
ADAM_B2 = 0.999
ADAM_EPS = 1e-08
ADAM_WD = 0.01
ADAM_STEP = 10

VMEM_LIMIT = 56 * 1024 * 1024
VMEM_LIMIT_WIDE = 62 * 1024 * 1024


def _params(sem, vmem=VMEM_LIMIT):
    return pltpu.CompilerParams(dimension_semantics=sem, vmem_limit_bytes=vmem)


def _const_spec(shape):
    nd = len(shape)
    return pl.BlockSpec(shape, lambda *_: (0,) * nd, pipeline_mode=pl.Buffered(1))


def _lane(shape):
    return lax.broadcasted_iota(jnp.int32, shape, len(shape) - 1)


def _split_dot(v, mat):
    hi = v.astype(BF16)
    lo = (v - hi.astype(F32)).astype(BF16)
    return (jnp.dot(hi, mat, preferred_element_type=F32) + jnp.dot(lo, mat, preferred_element_type=F32))


def _head_ones(n):
    r = lax.shift_right_logical(lax.broadcasted_iota(jnp.int32, (n, n), 0), 6)
    c = lax.shift_right_logical(lax.broadcasted_iota(jnp.int32, (n, n), 1), 6)
    return (r == c).astype(BF16)


def _head_sum(v):
    w = v.shape[1]
    vb = v.astype(BF16)
    if w <= 256:
        return jnp.dot(vb, _head_ones(w), preferred_element_type=F32)
    ones = _head_ones(256)
    return jnp.concatenate([jnp.dot(vb[:, s:s + 256], ones, preferred_element_type=F32) for s in range(0, w, 256)], axis=1)


def _head_norm(seg, gain):
    rs = lax.rsqrt(_head_sum(seg * seg) * (1.0 / HEAD_DIM) + EPS)
    return seg * rs * gain


def _head_norm_bwd(seg, gain, d_out):
    rs = lax.rsqrt(_head_sum(seg * seg) * (1.0 / HEAD_DIM) + EPS)
    hat = seg * rs
    gd = d_out * gain
    d_seg = rs * (gd - hat * (_head_sum(gd * hat) * (1.0 / HEAD_DIM)))
    return d_seg, d_out * hat


def _expand_kv(v):
    r = pltpu.roll(v, 64, axis=1)
    lo = _lane(v.shape) < 64
    return jnp.concatenate([jnp.where(lo, v, r), jnp.where(lo, r, v)], axis=1)


def _fold_kv(e4):
    t0 = e4[:, 0:128] + e4[:, 128:256]
    t1 = e4[:, 256:384] + e4[:, 384:512]
    t0 = t0 + pltpu.roll(t0, 64, axis=1)
    t1 = t1 + pltpu.roll(t1, 64, axis=1)
    return jnp.where(_lane(t0.shape) < 64, t0, t1)


def _pick_lane(blk, idx):
    return jnp.sum(jnp.where(_lane(blk.shape) == idx, blk, 0.0), axis=1, keepdims=True)


def _nt(a, b):
    return lax.dot_general(a, b, (((1,), (1,)), ((), ())), preferred_element_type=F32)


def _tn(a, b):
    return lax.dot_general(a, b, (((0,), (0,)), ((), ())), preferred_element_type=F32)


def _norm_proj(x2, g1, w_main_t, w_f_t, gqa, gka, gqb, gkb, bf_row, s, tm):
    t = x2.shape[0]
    nt = s // tm

    def body(x_ref, g1_ref, wm_ref, wf_ref, gqa_ref, gka_ref, gqb_ref, gkb_ref, b_ref,
             xn_ref, raw_ref, fl_ref, qa_ref, kae_ref, vae_ref, qo_ref, ko_ref, vo_ref, carry, c_ref):
        @pl.when(lax.rem(pl.program_id(0), nt) == 0)
        def _():
            carry[...] = jnp.zeros_like(carry)

        x = x_ref[...]
        r = lax.rsqrt(jnp.mean(x * x, axis=-1, keepdims=True) + EPS)
        xn = (x * r * g1_ref[...]).astype(BF16)
        xn_ref[...] = xn
        proj = _nt(xn, wm_ref[...])
        raw_ref[...] = proj
        fl = _nt(xn, wf_ref[...])
        fl_ref[...] = fl
        qa_ref[...] = _head_norm(proj[:, 0:512], gqa_ref[...]).astype(BF16)
        kae_ref[...] = _expand_kv(_head_norm(proj[:, 512:640], gka_ref[...])).astype(BF16)
        vae_ref[...] = _expand_kv(proj[:, 640:768]).astype(BF16)

        z = fl + b_ref[...]
        e = jnp.exp(-jnp.abs(z))
        u = 1.0 + e
        log1p = jnp.where(u == 1.0, e, jnp.log(u) * (e / (u - 1.0)))
        lf = jnp.minimum(z, 0.0) - log1p
        for r0 in range(0, tm, 256):
            c_ref[r0:r0 + 256, :] = _tri_dot(256, False, lf[r0:r0 + 256]) + carry[...]
            carry[...] = c_ref[pl.ds(r0 + 255, 1), :]
        c2 = c_ref[...] * LOG2E
        qb = _head_norm(proj[:, 768:1280], gqb_ref[...]) * (SCALE * LOG2E)
        kb = _head_norm(proj[:, 1280:1792], gkb_ref[...])
        lane = _lane((tm, LANES))
        for h in range(8):
            j, half = h // 2, h % 2
            pair, blk = slice(LANES * j, LANES * (j + 1)), slice(LANES * h, LANES * (h + 1))
            feat = _spread3(c2[:, h:h + 1], (tm, LANES), (L_CK, L_CQ))
            q = _put_ones(_head_block(qb[:, pair], half), (L_CK, L_CK + 1, L_CK + 2))
            qo_ref[:, blk] = jnp.where((lane >= L_CQ) & (lane < L_CQ + 3), feat, q).astype(BF16)
            k = _put_ones(_head_block(kb[:, pair], half), tuple(range(L_CQ, L_CQ + 6)))
            ko_ref[:, blk] = jnp.where((lane >= L_CK) & (lane < L_CK + 3), -feat, k).astype(BF16)
            v = _head_block(proj[:, 1792 + LANES * j:1792 + LANES * (j + 1)], half)
            vo_ref[:, blk] = _put_ones(v, (L_ONE, L_DELTA, L_DELTA + 1, L_DELTA + 2)).astype(BF16)

    def tile(w):
        return pl.BlockSpec((tm, w), lambda i: (i, 0))

    aug = jax.ShapeDtypeStruct((t, 8 * LANES), BF16)
    return pl.pallas_call(
        body, name="norm_proj", grid=(t // tm,),
        in_specs=[tile(D_MODEL), _const_spec((1, D_MODEL)), _const_spec((MAIN_W, D_MODEL)), _const_spec((LANES, D_MODEL)),
                  _const_spec((1, 512)), _const_spec((1, 128)), _const_spec((1, 512)), _const_spec((1, 512)),
                  _const_spec((1, LANES))],
        out_specs=[tile(D_MODEL), tile(MAIN_W), tile(LANES), tile(512), tile(256), tile(256)] + [tile(8 * LANES)] * 3,
        out_shape=[jax.ShapeDtypeStruct((t, D_MODEL), BF16), jax.ShapeDtypeStruct((t, MAIN_W), F32),
                   jax.ShapeDtypeStruct((t, LANES), F32), jax.ShapeDtypeStruct((t, 512), BF16),
                   jax.ShapeDtypeStruct((t, 256), BF16), jax.ShapeDtypeStruct((t, 256), BF16), aug, aug, aug],
        scratch_shapes=[pltpu.VMEM((1, LANES), F32), pltpu.VMEM((tm, LANES), F32)],
        compiler_params=_params(("arbitrary",)),
    )(x2, g1, w_main_t, w_f_t, gqa, gka, gqb, gkb, bf_row)


def _tri_dot(n, upper, v):
    r = lax.broadcasted_iota(jnp.int32, (n, n), 0)
    c = lax.broadcasted_iota(jnp.int32, (n, n), 1)
    tri = ((c >= r) if upper else (c <= r)).astype(BF16)
    hi = v.astype(BF16)
    mid = (v - hi.astype(F32)).astype(BF16)
    lo = (v - hi.astype(F32) - mid.astype(F32)).astype(BF16)
    return (jnp.dot(tri, hi, preferred_element_type=F32) + jnp.dot(tri, mid, preferred_element_type=F32)
            + jnp.dot(tri, lo, preferred_element_type=F32))


def _slope(p, hh):
    out = jnp.float32(2.0 ** -(2 * 3 + hh + 1))
    for pp in (2, 1, 0):
        out = jnp.where(p == pp, jnp.float32(2.0 ** -(2 * pp + hh + 1)), out)
    return out


def _swa_windows(ref, i, tq):
    nsub = tq // WINDOW
    cur = ref[pl.ds(pl.multiple_of(i * tq, tq), tq), :].reshape(nsub, WINDOW, LANES)
    first = ref[pl.ds(pl.multiple_of(jnp.maximum(i * tq - WINDOW, 0), WINDOW), WINDOW), :].reshape(1, WINDOW, LANES)
    return jnp.concatenate([jnp.concatenate([first, cur[0:nsub - 1]], axis=0), cur], axis=1)


def _both_heads(x3, lo):
    zero = jnp.zeros_like(x3)
    return jnp.concatenate([jnp.where(lo, x3, zero), jnp.where(lo, zero, x3)], axis=0)


def _swa_head_consts(sink_ref, p, i, nsub):
    bidx = lax.broadcasted_iota(jnp.int32, (2 * nsub, 1, 1), 0)
    is_a = bidx < nsub
    slope = jnp.where(is_a, _slope(p, 0), _slope(p, 1))
    sinks = sink_ref[...]
    sink = jnp.where(is_a, _pick_lane(sinks, 2 * p).reshape(1, 1, 1), _pick_lane(sinks, 2 * p + 1).reshape(1, 1, 1))
    first = (i == 0) & ((bidx == 0) | (bidx == nsub))
    return slope, sink, first


def _swa_fwd(qa, kae, vae, sink_row, nb, s, tq):
    t = qa.shape[0]
    nq = s // tq
    nsub = tq // WINDOW

    def body(q_ref, k_ref, v_ref, sink_ref, o_ref, lse_ref):
        p, i = pl.program_id(1), pl.program_id(2)
        lo = _lane((1, 1, LANES)) < 64
        kk, vv = _swa_windows(k_ref, i, tq), _swa_windows(v_ref, i, tq)
        qs = (q_ref[...].astype(F32) * SCALE).astype(BF16).reshape(nsub, WINDOW, LANES)
        q8 = _both_heads(qs, lo)
        s8 = jnp.einsum("bqd,bkd->bqk", q8, jnp.concatenate([kk, kk], axis=0), preferred_element_type=F32)
        row = lax.broadcasted_iota(jnp.int32, (1, WINDOW, 2 * WINDOW), 1)
        col = lax.broadcasted_iota(jnp.int32, (1, WINDOW, 2 * WINDOW), 2)
        dist = row + WINDOW - col
        slope, sink, first = _swa_head_consts(sink_ref, p, i, nsub)
        valid = (dist >= 0) & (dist < WINDOW) & ((col >= WINDOW) | jnp.logical_not(first))
        s8 = jnp.where(valid, s8 - slope * dist.astype(F32), NEG_INF)
        m = jnp.maximum(jnp.max(s8, axis=2, keepdims=True), sink)
        e = jnp.exp(s8 - m)
        den = jnp.sum(e, axis=2, keepdims=True) + jnp.exp(sink - m)
        pr = (e * (1.0 / den)).astype(BF16)
        o8 = jnp.einsum("bqk,bkd->bqd", pr, jnp.concatenate([vv, vv], axis=0), preferred_element_type=F32)
        lse8 = m + jnp.log(den)
        o_ref[...] = jnp.where(lo, o8[0:nsub], o8[nsub:]).astype(BF16).reshape(tq, LANES)
        lse_ref[...] = jnp.where(lo, lse8[0:nsub], lse8[nsub:]).reshape(tq, LANES)

    return pl.pallas_call(
        body, name="swa_fwd", grid=(nb, 4, nq),
        in_specs=[pl.BlockSpec((tq, LANES), lambda b, p, i: (b * nq + i, p)),
                  pl.BlockSpec((s, LANES), lambda b, p, i: (b, lax.shift_right_logical(p, 1))),
                  pl.BlockSpec((s, LANES), lambda b, p, i: (b, lax.shift_right_logical(p, 1))),
                  pl.BlockSpec((1, LANES), lambda b, p, i: (0, 0))],
        out_specs=[pl.BlockSpec((tq, LANES), lambda b, p, i: (b * nq + i, p)),
                   pl.BlockSpec((None, tq, LANES), lambda b, p, i: (p, b * nq + i, 0))],
        out_shape=[jax.ShapeDtypeStruct((t, 512), BF16), jax.ShapeDtypeStruct((4, t, LANES), F32)],
        compiler_params=_params(("arbitrary", "arbitrary", "arbitrary")),
    )(qa, kae, vae, sink_row)


def _swa_bwd(qa, kae, vae, do_a, sink_row, lse, delta, nb, s, tq):
    t = qa.shape[0]
    nq = s // tq
    nsub = tq // WINDOW

    def body(q_ref, do_ref, k_ref, v_ref, sink_ref, lse_ref, dl_ref, dq_ref, dk_ref, dv_ref, ds_ref):
        p, i = pl.program_id(1), pl.program_id(2)

        @pl.when(i == 0)
        def _():
            ds_ref[...] = jnp.zeros_like(ds_ref)

        lo = _lane((1, 1, LANES)) < 64
        kk, vv = _swa_windows(k_ref, i, tq), _swa_windows(v_ref, i, tq)
        kks = (kk.astype(F32) * SCALE).astype(BF16)
        k8, v8 = jnp.concatenate([kks, kks], axis=0), jnp.concatenate([vv, vv], axis=0)
        q8 = _both_heads(q_ref[...].reshape(nsub, WINDOW, LANES), lo)
        do8 = _both_heads(do_ref[...].reshape(nsub, WINDOW, LANES), lo)
        cur = pl.multiple_of(i * tq, tq)
        sub = lax.broadcasted_iota(jnp.int32, (WINDOW, WINDOW), 0)
        lse_t = [lse_ref[u * WINDOW:(u + 1) * WINDOW, :].T for u in range(nsub)]
        dl_t = [dl_ref[u * WINDOW:(u + 1) * WINDOW, :].T for u in range(nsub)]
        lse8 = jnp.concatenate([t_[64 * hh:64 * hh + 1, :].reshape(1, 1, WINDOW) for hh in range(2) for t_ in lse_t], axis=0)
        dl8 = jnp.concatenate([jnp.sum(jnp.where(sub == 2 * p + hh, t_, 0.0), axis=0, keepdims=True).reshape(1, 1, WINDOW)
                               for hh in range(2) for t_ in dl_t], axis=0)
        row = lax.broadcasted_iota(jnp.int32, (1, 2 * WINDOW, WINDOW), 1)
        col = lax.broadcasted_iota(jnp.int32, (1, 2 * WINDOW, WINDOW), 2)
        dist = col + WINDOW - row
        slope, sink, first = _swa_head_consts(sink_ref, p, i, nsub)
        valid = (dist >= 0) & (dist < WINDOW) & ((row >= WINDOW) | jnp.logical_not(first))
        st = jnp.einsum("bkd,bqd->bkq", k8, q8, preferred_element_type=F32) - slope * dist.astype(F32) - lse8
        pt = jnp.where(valid, jnp.exp(jnp.where(valid, st, 0.0)), 0.0)
        dpt = jnp.einsum("bkd,bqd->bkq", v8, do8, preferred_element_type=F32)
        dst = pt * (dpt - dl8)
        ptb, dstb = pt.astype(BF16), dst.astype(BF16)
        dv8 = jnp.einsum("bkq,bqd->bkd", ptb, do8, preferred_element_type=F32)
        dk8 = jnp.einsum("bkq,bqd->bkd", dstb, q8, preferred_element_type=F32) * SCALE
        dq8 = jnp.einsum("bkq,bkd->bqd", dstb, k8, preferred_element_type=F32)
        dq_ref[...] = jnp.where(lo, dq8[0:nsub], dq8[nsub:]).reshape(tq, LANES)

        psd = jnp.exp(sink - lse8) * dl8
        row_h = lax.broadcasted_iota(jnp.int32, (8, LANES), 0)
        for hh in range(2):
            tot = jnp.sum(jnp.sum(psd[hh * nsub:(hh + 1) * nsub], axis=2, keepdims=True), axis=0, keepdims=True)
            ds_ref[...] += jnp.where(row_h == hh, -tot.reshape(1, 1), 0.0)

        prev = pl.multiple_of(jnp.maximum(i * tq - WINDOW, 0), WINDOW)
        for g8, g_ref in ((dk8, dk_ref), (dv8, dv_ref)):
            g4 = g8[0:nsub] + g8[nsub:]
            own, before = g4[:, WINDOW:, :], g4[:, 0:WINDOW, :]
            shifted = jnp.concatenate([before[1:nsub], jnp.zeros((1, WINDOW, LANES), F32)], axis=0)
            g_ref[pl.ds(cur, tq), :] = (own + shifted).reshape(tq, LANES)
            g_ref[pl.ds(prev, WINDOW), :] += before[0]

    return pl.pallas_call(
        body, name="swa_bwd", grid=(nb, 4, nq),
        in_specs=[pl.BlockSpec((tq, LANES), lambda b, p, i: (b * nq + i, p)),
                  pl.BlockSpec((tq, LANES), lambda b, p, i: (b * nq + i, p)),
                  pl.BlockSpec((s, LANES), lambda b, p, i: (b, lax.shift_right_logical(p, 1))),
                  pl.BlockSpec((s, LANES), lambda b, p, i: (b, lax.shift_right_logical(p, 1))),
                  pl.BlockSpec((1, LANES), lambda b, p, i: (0, 0)),
                  pl.BlockSpec((None, tq, LANES), lambda b, p, i: (p, b * nq + i, 0)),
                  pl.BlockSpec((tq, LANES), lambda b, p, i: (b * nq + i, 0))],
        out_specs=[pl.BlockSpec((tq, LANES), lambda b, p, i: (b * nq + i, p)),
                   pl.BlockSpec((s, LANES), lambda b, p, i: (b, p)),
                   pl.BlockSpec((s, LANES), lambda b, p, i: (b, p)),
                   pl.BlockSpec((None, None, 8, LANES), lambda b, p, i: (b, p, 0, 0))],
        out_shape=[jax.ShapeDtypeStruct((t, 512), F32), jax.ShapeDtypeStruct((t, 512), F32),
                   jax.ShapeDtypeStruct((t, 512), F32), jax.ShapeDtypeStruct((nb, 4, 8, LANES), F32)],
        compiler_params=_params(("arbitrary", "arbitrary", "arbitrary")),
    )(qa, do_a, kae, vae, sink_row, lse, delta)


MESH = pl.DeviceIdType.MESH
ANY = pl.BlockSpec(memory_space=pl.ANY)
N_SEM = 7


def _gather_steps(pairs, send_sems, recv_sems, local_sems):
    x, y, c = lax.axis_index("x"), lax.axis_index("y"), lax.axis_index("c")
    me, sibling = (x, y, c), (x, y, 1 - c)
    chips = [(1 - x, y), (x, 1 - y), (1 - x, 1 - y)]
    mine, first, passed, landed, last = [], [], [], [], []
    for a, (x_ref, out_ref) in enumerate(pairs):
        def slot(px, py, pc, out_ref=out_ref):
            return out_ref.at[4 * px + 2 * py + pc]

        def copy(k, block, to, src=None, a=a, slot=slot):
            return pltpu.make_async_remote_copy(
                src_ref=slot(*block) if src is None else src, dst_ref=slot(*block),
                send_sem=send_sems.at[N_SEM * a + k], recv_sem=recv_sems.at[N_SEM * a + k], device_id=to, device_id_type=MESH)

        mine.append(pltpu.make_async_copy(x_ref, slot(*me), local_sems.at[a]))
        first += [copy(0, me, sibling, src=x_ref)] + [copy(1 + j, me, (*chip, c), src=x_ref) for j, chip in enumerate(chips)]
        passed += [copy(4 + j, (*chip, c), sibling) for j, chip in enumerate(chips)]
        landed += [copy(1 + j, (*chip, c), me) for j, chip in enumerate(chips)]
        last += [copy(0, sibling, me)] + [copy(4 + j, (*chip, 1 - c), me) for j, chip in enumerate(chips)]

    def start():
        for cp in mine + first:
            cp.start()

    def forward():
        for arrived, onward in zip(landed, passed):
            arrived.wait_recv()
            onward.start()

    def finish():
        for cp in last:
            cp.wait_recv()
        for cp in first + passed:
            cp.wait_send()
        for cp in mine:
            cp.wait()

    return start, forward, finish


def _exchange_steps(pairs, send_sems, recv_sems, local_sems):
    x, y, c = lax.axis_index("x"), lax.axis_index("y"), lax.axis_index("c")
    my_id = 4 * x + 2 * y + c
    local, remote = [], []
    for a, (src, dst) in enumerate(pairs):
        local.append(pltpu.make_async_copy(src.at[my_id], dst.at[my_id], local_sems.at[a]))
        for k in range(1, N_DEV):
            px = 1 - x if k & 4 else x
            py = 1 - y if k & 2 else y
            pc = 1 - c if k & 1 else c
            remote.append(pltpu.make_async_remote_copy(
                src_ref=src.at[4 * px + 2 * py + pc], dst_ref=dst.at[my_id],
                send_sem=send_sems.at[N_SEM * a + k - 1], recv_sem=recv_sems.at[N_SEM * a + k - 1],
                device_id=(px, py, pc), device_id_type=MESH))

    def start():
        for cp in local + remote:
            cp.start()

    def finish():
        for cp in remote:
            cp.wait_recv()
        for cp in remote:
            cp.wait_send()
        for cp in local:
            cp.wait()

    return start, finish


L_ONE = 64
L_CK = 65
L_CQ = 68
L_LSE = 71
L_DELTA = 74


def _head_block(pair, half):
    y = pair if half == 0 else pltpu.roll(pair, 64, axis=1)
    return jnp.where(_lane(pair.shape) < 64, y, 0.0)


def _put3(blk, lane0, col):
    lane = _lane(blk.shape)
    hi = col.astype(BF16).astype(F32)
    mid = (col - hi).astype(BF16).astype(F32)
    lo = (col - hi - mid).astype(BF16).astype(F32)
    return jnp.where(lane == lane0, hi, jnp.where(lane == lane0 + 1, mid, jnp.where(lane == lane0 + 2, lo, blk)))


def _spread3(col, shape, lane0s):
    lane = _lane(shape)
    hi = col.astype(BF16).astype(F32)
    mid = (col - hi).astype(BF16).astype(F32)
    lo = (col - hi - mid).astype(BF16).astype(F32)

    def at(k):
        return functools.reduce(jnp.logical_or, [lane == ln + k for ln in lane0s])

    return jnp.where(at(0), hi, jnp.where(at(1), mid, jnp.where(at(2), lo, 0.0)))


def _put_ones(blk, lanes):
    lane = _lane(blk.shape)
    hit = functools.reduce(jnp.logical_or, [lane == ln for ln in lanes])
    return jnp.where(hit, 1.0, blk)


def _to_pairs(ref):
    out = []
    for j in range(4):
        a, b = ref[:, 2 * LANES * j:2 * LANES * j + LANES], ref[:, 2 * LANES * j + LANES:2 * LANES * (j + 1)]
        out.append(jnp.where(_lane(a.shape) < 64, a, pltpu.roll(b, 64, axis=1)))
    return jnp.concatenate(out, axis=1)


def _fox_fwd(q_aug, k_aug, v_aug, nb, s, bt, shards=()):
    t = q_aug.shape[0]
    nq = s // bt
    n_in, n_sh = 3, len(shards)

    def body(*refs):
        q_ref, k_ref, v_ref = refs[:n_in]
        o_ref, ql_ref = refs[n_in + n_sh:n_in + n_sh + 2]
        if shards:
            srcs, dsts = refs[n_in:n_in + n_sh], refs[n_in + n_sh + 2:n_in + 2 * n_sh + 2]
            start, forward, finish = _gather_steps(list(zip(srcs, dsts)), *refs[n_in + 2 * n_sh + 2:])
            step = (pl.program_id(0) * 4 + pl.program_id(1)) * nq + pl.program_id(2)
            pl.when(step == 0)(start)
            pl.when(step == nb * 3 * nq)(forward)
        i = pl.program_id(2)
        sls = [slice(LANES * hh, LANES * (hh + 1)) for hh in range(2)]
        qhs = [q_ref[:, sl] for sl in sls]

        def update(m, acc, qrows, start, size, sl, causal):
            sc = _nt(qrows, k_ref[pl.ds(start, size), sl])
            if causal:
                row = lax.broadcasted_iota(jnp.int32, sc.shape, 0)
                col = lax.broadcasted_iota(jnp.int32, sc.shape, 1)
                sc = jnp.where(row >= col, sc, NEG_INF)
            m_new = jnp.maximum(m, jnp.max(sc, axis=1, keepdims=True))
            pr = jnp.exp2(sc - m_new).astype(BF16)
            acc = jnp.exp2(m - m_new) * acc + jnp.dot(pr, v_ref[pl.ds(start, size), sl], preferred_element_type=F32)
            return m_new, acc

        def blk(kb_i, carry):
            start = pl.multiple_of(kb_i * bt, bt)
            return tuple(update(m, acc, qh, start, bt, sl, False) for (m, acc), qh, sl in zip(carry, qhs, sls))

        def diag_blk(carry):
            start = pl.multiple_of(i * bt, bt)
            return tuple(update(m, acc, qh, start, bt, sl, True) for (m, acc), qh, sl in zip(carry, qhs, sls))

        init = tuple((jnp.full((bt, 1), NEG_INF, F32), jnp.zeros((bt, LANES), F32)) for _ in range(2))
        carry = lax.fori_loop(0, i, blk, init)
        outs = []
        for (m, acc), qh, sl in zip(diag_blk(carry), qhs, sls):
            l = acc[:, L_ONE:L_ONE + 1]
            outs.append(acc * (1.0 / l))
            ql_ref[:, sl] = _put3(qh.astype(F32), L_LSE, -(m + jnp.log(l) * LOG2E)).astype(BF16)
        o_ref[...] = jnp.where(_lane((1, LANES)) < 64, outs[0], pltpu.roll(outs[1], 64, axis=1)).astype(BF16)
        if shards:
            pl.when(step == nb * 4 * nq - 1)(finish)

    in_specs = [pl.BlockSpec((bt, 2 * LANES), lambda b, j, i: (b * nq + i, j)),
                pl.BlockSpec((s, 2 * LANES), lambda b, j, i: (b, j)),
                pl.BlockSpec((s, 2 * LANES), lambda b, j, i: (b, j))]
    out_specs = [pl.BlockSpec((bt, LANES), lambda b, j, i: (b * nq + i, j)),
                 pl.BlockSpec((bt, 2 * LANES), lambda b, j, i: (b * nq + i, j))]
    out_shape = [jax.ShapeDtypeStruct((t, 512), BF16), jax.ShapeDtypeStruct((t, 8 * LANES), BF16)]
    args, scratch = [q_aug, k_aug, v_aug, *shards], []
    if shards:
        in_specs += [ANY] * n_sh
        out_specs += [ANY] * n_sh
        out_shape += [jax.ShapeDtypeStruct((N_DEV,) + sh.shape, sh.dtype) for sh in shards]
        scratch = [pltpu.SemaphoreType.DMA((N_SEM * n_sh,)), pltpu.SemaphoreType.DMA((N_SEM * n_sh,)),
                   pltpu.SemaphoreType.DMA((n_sh,))]
    return pl.pallas_call(
        body, name="fox_fwd", grid=(nb, 4, nq), in_specs=in_specs, out_specs=out_specs, out_shape=out_shape,
        scratch_shapes=scratch, compiler_params=_params(("arbitrary", "arbitrary", "arbitrary")),
    )(*args)


def _fox_bwd(ql_aug, k_aug, v_aug, do_aug, nb, s, bt, exch=()):
    t = ql_aug.shape[0]
    nk = s // bt
    n_in, n_out, n_ex = 4, 3, len(exch)

    def body(*refs):
        q_ref, do_ref, k_ref, v_ref = refs[:n_in]
        dq_ref, dk_ref, dv_ref = refs[n_in + n_ex:n_in + n_ex + n_out]
        if exch:
            srcs = refs[n_in:n_in + n_ex]
            dsts = refs[n_in + n_ex + n_out:n_in + 2 * n_ex + n_out]
            start, finish = _exchange_steps(list(zip(srcs, dsts)), *refs[n_in + 2 * n_ex + n_out:])
            step = (pl.program_id(0) * 4 + pl.program_id(1)) * nk + pl.program_id(2)
            pl.when(step == 0)(start)
        kb_i = pl.program_id(2)

        @pl.when(kb_i == 0)
        def _():
            dq_ref[...] = jnp.zeros_like(dq_ref)

        row = lax.broadcasted_iota(jnp.int32, (bt, bt), 0)
        col = lax.broadcasted_iota(jnp.int32, (bt, bt), 1)
        sls = [slice(LANES * hh, LANES * (hh + 1)) for hh in range(2)]
        khs, vhs = [k_ref[:, sl] for sl in sls], [v_ref[:, sl] for sl in sls]

        def blk(qi, carry, diag):
            start = pl.multiple_of(qi * bt, bt)
            new = []
            for (dk_a, dv_a), kh, vh, sl in zip(carry, khs, vhs, sls):
                qblk, doblk = q_ref[pl.ds(start, bt), sl], do_ref[pl.ds(start, bt), sl]
                st = _nt(kh, qblk)
                if diag:
                    pt = jnp.where(col >= row, jnp.exp2(jnp.where(col >= row, st, 0.0)), 0.0)
                else:
                    pt = jnp.exp2(st)
                dst = pt * _nt(vh, doblk)
                ptb, dstb = pt.astype(BF16), dst.astype(BF16)
                dv_a = dv_a + jnp.dot(ptb, doblk, preferred_element_type=F32)
                dk_a = dk_a + jnp.dot(dstb, qblk, preferred_element_type=F32)
                dq_ref[pl.ds(start, bt), sl] += _tn(dstb, kh)
                new.append((dk_a, dv_a))
            return tuple(new)

        zero = jnp.zeros((bt, LANES), F32)
        carry = blk(kb_i, ((zero, zero), (zero, zero)), True)
        carry = lax.fori_loop(kb_i + 1, nk, lambda qi, c: blk(qi, c, False), carry)
        for (dk_acc, dv_acc), sl in zip(carry, sls):
            dk_ref[:, sl] = dk_acc
            dv_ref[:, sl] = dv_acc
        if exch:
            pl.when(step == nb * 4 * nk - 1)(finish)

    scratch = []
    if exch:
        scratch = [pltpu.SemaphoreType.DMA((N_SEM * n_ex,)), pltpu.SemaphoreType.DMA((N_SEM * n_ex,)),
                   pltpu.SemaphoreType.DMA((n_ex,))]
    whole = pl.BlockSpec((s, 2 * LANES), lambda b, j, kb_i: (b, j))
    tile = pl.BlockSpec((bt, 2 * LANES), lambda b, j, kb_i: (b * nk + kb_i, j))
    shp = jax.ShapeDtypeStruct((t, 8 * LANES), F32)
    return pl.pallas_call(
        body, name="fox_bwd", grid=(nb, 4, nk),
        in_specs=[whole, whole, tile, tile] + [ANY] * n_ex,
        out_specs=[whole, tile, tile] + [ANY] * n_ex,
        out_shape=[shp, shp, shp] + [jax.ShapeDtypeStruct(e.shape, e.dtype) for e in exch],
        scratch_shapes=scratch, compiler_params=_params(("arbitrary", "arbitrary", "arbitrary")),
    )(ql_aug, do_aug, k_aug, v_aug, *exch)


FF_BLK = D_FF // N_DEV


def _mlp_fwd(x2, ma, mb, tgt, w_out, g2, w_up, w_down, tm):
    t = x2.shape[0]

    def body(x_ref, ma_ref, mb_ref, tg_ref, wo_ref, g2_ref, wu_ref, wd_ref,
             h_ref, hn_ref, hid_ref, dy_ref, dyb_ref, loss_ref):
        @pl.when(pl.program_id(0) == 0)
        def _():
            loss_ref[...] = jnp.zeros_like(loss_ref)

        h = (x_ref[...] + jnp.dot(ma_ref[...], wo_ref[0:512, :], preferred_element_type=F32)
             + jnp.dot(mb_ref[...], wo_ref[512:1024, :], preferred_element_type=F32))
        h_ref[...] = h
        r = lax.rsqrt(jnp.mean(h * h, axis=-1, keepdims=True) + EPS)
        hn = (h * r * g2_ref[...]).astype(BF16)
        hn_ref[...] = hn
        for d in range(N_DEV):
            u = jnp.maximum(jnp.dot(hn, wu_ref[d], preferred_element_type=F32), 0.0)
            hid_ref[:, FF_BLK * d:FF_BLK * (d + 1)] = (u * u).astype(BF16)
        y = h + jnp.dot(hid_ref[...], wd_ref[...], preferred_element_type=F32)
        err = y - tg_ref[...]
        dy = err * (1.0 / D_MODEL)
        dy_ref[...] = dy
        dyb_ref[...] = dy.astype(BF16)
        part =0.5 * jnp.sum(jnp.sum(err * err, axis=1, keepdims=True) * (1.0 / D_MODEL), axis=0, keepdims=True)
        loss_ref[...] += part

    def tile(w):
        return pl.BlockSpec((tm, w), lambda i: (i, 0))

    return pl.pallas_call(
        body, name="mlp_fwd", grid=(t // tm,),
        in_specs=[tile(D_MODEL), tile(512), tile(512), tile(D_MODEL), _const_spec((D_MODEL, D_MODEL)),
                  _const_spec((1, D_MODEL)), _const_spec((N_DEV, D_MODEL, FF_BLK)), _const_spec((D_FF, D_MODEL))],
        out_specs=[tile(D_MODEL), tile(D_MODEL), tile(D_FF), tile(D_MODEL), tile(D_MODEL),
                   pl.BlockSpec((8, LANES), lambda i: (0, 0))],
        out_shape=[jax.ShapeDtypeStruct((t, D_MODEL), F32), jax.ShapeDtypeStruct((t, D_MODEL), BF16),
                   jax.ShapeDtypeStruct((t, D_FF), BF16), jax.ShapeDtypeStruct((t, D_MODEL), F32),
                   jax.ShapeDtypeStruct((t, D_MODEL), BF16), jax.ShapeDtypeStruct((8, LANES), F32)],
        compiler_params=_params(("arbitrary",)),
    )(x2, ma, mb, tgt, w_out, g2, w_up, w_down)


def _mlp_bwd(dy, hid, h, ma, mb, w_down, w_up_t, w_out, g2, tm):
    t = dy.shape[0]

    def body(dy_ref, hid_ref, h_ref, ma_ref, mb_ref, wd_ref, wut_ref, wo_ref, g2_ref,
             du_ref, dh_ref, dhb_ref, dma_ref, dob_ref, dla_ref, gg_ref):
        @pl.when(pl.program_id(0) == 0)
        def _():
            gg_ref[...] = jnp.zeros_like(gg_ref)

        dy = dy_ref[...]
        d_hid = _nt(dy.astype(BF16), wd_ref[...])
        du = (d_hid * (2.0 * jnp.sqrt(hid_ref[...].astype(F32)))).astype(BF16)
        du_ref[...] = du
        d_hn = jnp.dot(du, wut_ref[...], preferred_element_type=F32)
        h = h_ref[...]
        r = lax.rsqrt(jnp.mean(h * h, axis=-1, keepdims=True) + EPS)
        hat = h * r
        gd = d_hn * g2_ref[...]
        dh = dy + r * (gd - hat * jnp.mean(gd * hat, axis=-1, keepdims=True))
        gg_ref[...] += jnp.sum(d_hn * hat, axis=0, keepdims=True)
        dh_ref[...] = dh
        dhb = dh.astype(BF16)
        dhb_ref[...] = dhb
        dm = _nt(dhb, wo_ref[...]).astype(BF16)
        dma, dmb = dm[:, 0:512], dm[:, 512:1024]
        dma_ref[...] = dma
        sel = (lax.shift_right_logical(lax.broadcasted_iota(jnp.int32, (512, LANES), 0), 6)
               == lax.broadcasted_iota(jnp.int32, (512, LANES), 1)).astype(BF16)
        dla_ref[...] = _split_dot(dma.astype(F32) * ma_ref[...].astype(F32), sel)
        dmb32 = dmb.astype(F32)
        dlb = _split_dot(dmb32 * mb_ref[...].astype(F32), sel)
        for hd in range(8):
            blk = _head_block(dmb32[:, LANES * (hd // 2):LANES * (hd // 2 + 1)], hd % 2)
            dob_ref[:, LANES * hd:LANES * (hd + 1)] = _put3(blk, L_DELTA, -dlb[:, hd:hd + 1]).astype(BF16)

    def tile(w):
        return pl.BlockSpec((tm, w), lambda i: (i, 0))

    return pl.pallas_call(
        body, name="mlp_bwd", grid=(t // tm,),
        in_specs=[tile(D_MODEL), tile(D_FF), tile(D_MODEL), tile(512), tile(512), _const_spec((D_FF, D_MODEL)),
                  _const_spec((D_FF, D_MODEL)), _const_spec((D_MODEL, D_MODEL)), _const_spec((1, D_MODEL))],
        out_specs=[tile(D_FF), tile(D_MODEL), tile(D_MODEL), tile(512), tile(8 * LANES), tile(LANES),
                   pl.BlockSpec((1, D_MODEL), lambda i: (0, 0))],
        out_shape=[jax.ShapeDtypeStruct((t, D_FF), BF16), jax.ShapeDtypeStruct((t, D_MODEL), F32),
                   jax.ShapeDtypeStruct((t, D_MODEL), BF16), jax.ShapeDtypeStruct((t, 512), BF16),
                   jax.ShapeDtypeStruct((t, 8 * LANES), BF16), jax.ShapeDtypeStruct((t, LANES), F32),
                   jax.ShapeDtypeStruct((1, D_MODEL), F32)],
        compiler_params=_params(("arbitrary",), VMEM_LIMIT_WIDE),
    )(dy, hid, h, ma, mb, w_down, w_up_t, w_out, g2)


def _wgrad(a, b, name, bm, bn, tk, out_dtype=F32, col_blocks=False, a2=None):
    t, m = a.shape
    n = b.shape[1]
    bm, bn = min(bm, m), min(bn, n)
    nk = t // tk

    def body(*refs):
        if a2 is None:
            a_ref, b_ref, o_ref, acc = refs
        else:
            a_ref, b_ref, a2_ref, o_ref, o2_ref, acc, acc2 = refs
        i, k = pl.program_id(0), pl.program_id(2)

        @pl.when(k == 0)
        def _():
            acc[...] = jnp.zeros_like(acc)

        acc[...] += _tn(a_ref[...], b_ref[...])

        @pl.when(k == nk - 1)
        def _():
            o_ref[...] = acc[...].astype(out_dtype)

        if a2 is not None:
            @pl.when((i == 0) & (k == 0))
            def _():
                acc2[...] = jnp.zeros_like(acc2)

            @pl.when(i == 0)
            def _():
                acc2[...] += _tn(a2_ref[...], b_ref[...])

            @pl.when((i == 0) & (k == nk - 1))
            def _():
                o2_ref[...] = acc2[...]

    if col_blocks:
        out_spec = pl.BlockSpec((None, bm, bn), lambda i, j, k: (j, i, 0))
        out_shape = jax.ShapeDtypeStruct((n // bn, m, bn), out_dtype)
    else:
        out_spec = pl.BlockSpec((bm, bn), lambda i, j, k: (i, j))
        out_shape = jax.ShapeDtypeStruct((m, n), out_dtype)
    in_specs = [pl.BlockSpec((tk, bm), lambda i, j, k: (k, i)), pl.BlockSpec((tk, bn), lambda i, j, k: (k, j))]
    out_specs, out_shapes, scratch, args = [out_spec], [out_shape], [pltpu.VMEM((bm, bn), F32)], [a, b]
    if a2 is not None:
        m2 = a2.shape[1]
        in_specs.append(pl.BlockSpec((tk, m2), lambda i, j, k: (k, 0)))
        out_specs.append(pl.BlockSpec((m2, n), lambda i, j, k: (0, 0)))
        out_shapes.append(jax.ShapeDtypeStruct((m2, n), F32))
        scratch.append(pltpu.VMEM((m2, n), F32))
        args.append(a2)
    out = pl.pallas_call(
        body, name=name, grid=(m // bm, n // bn, nk), in_specs=in_specs, out_specs=out_specs, out_shape=out_shapes,
        scratch_shapes=scratch, compiler_params=_params(("arbitrary", "arbitrary", "arbitrary")),
    )(*args)
    return out[0] if a2 is None else out


def _proj_bwd(raw, dqa, dkae, dvae, dqb, dkb, dvb, fl, bf_row, x2, dh, w_main_t, w_f_t, g1, gqa, gka, gqb, gkb, nb, s, tm):
    t = x2.shape[0]
    nt = s // tm

    def body(raw_ref, dqa_ref, dkae_ref, dvae_ref, dqb_ref, dkb_ref, dvb_ref, fl_ref, b_ref, x_ref, dh_ref,
             wmt_ref, wft_ref, g1_ref, gqa_ref, gka_ref, gqb_ref, gkb_ref,
             dx_ref, dp_ref, dfb_ref, ggqa_ref, ggka_ref, ggqb_ref, ggkb_ref, gg1_ref, gb_ref, carry, dlf_ref):
        @pl.when((pl.program_id(0) == 0) & (pl.program_id(1) == 0))
        def _():
            for r in (ggqa_ref, ggka_ref, ggqb_ref, ggkb_ref, gg1_ref, gb_ref):
                r[...] = jnp.zeros_like(r)

        @pl.when(pl.program_id(1) == 0)
        def _():
            carry[...] = jnp.zeros_like(carry)

        lane = _lane((tm, LANES))
        dc = jnp.zeros((tm, LANES), F32)
        for hd in range(8):
            col = (dqb_ref[:, LANES * hd + L_CQ:LANES * hd + L_CQ + 1] - dkb_ref[:, LANES * hd + L_CK:LANES * hd + L_CK + 1])
            dc = jnp.where(lane == hd, col, dc)
        dlf_ref[...] = _tri_dot(tm, True, dc) + carry[...]
        carry[...] = dlf_ref[pl.ds(0, 1), :]
        dfl = dlf_ref[...] * (1.0 / (1.0 + jnp.exp(fl_ref[...] + b_ref[...])))
        gb_ref[...] += jnp.sum(dfl, axis=0, keepdims=True)

        raw = raw_ref[...]
        d_qa, p_qa = _head_norm_bwd(raw[:, 0:512], gqa_ref[...], dqa_ref[...])
        d_ka, p_ka = _head_norm_bwd(raw[:, 512:640], gka_ref[...], _fold_kv(dkae_ref[...]))
        d_va = _fold_kv(dvae_ref[...])
        d_qb, p_qb = _head_norm_bwd(raw[:, 768:1280], gqb_ref[...], _to_pairs(dqb_ref) * SCALE)
        d_kb, p_kb = _head_norm_bwd(raw[:, 1280:1792], gkb_ref[...], _to_pairs(dkb_ref) * (1.0 / LOG2E))
        ggqa_ref[...] += jnp.sum(p_qa, axis=0, keepdims=True)
        ggka_ref[...] += jnp.sum(p_ka, axis=0, keepdims=True)
        ggqb_ref[...] += jnp.sum(p_qb, axis=0, keepdims=True)
        ggkb_ref[...] += jnp.sum(p_kb, axis=0, keepdims=True)
        dproj = jnp.concatenate([d_qa, d_ka, d_va, d_qb, d_kb, _to_pairs(dvb_ref)], axis=1).astype(BF16)
        dp_ref[...] = dproj
        dfb = dfl.astype(BF16)
        dfb_ref[...] = dfb
        d_xn = (jnp.dot(dproj, wmt_ref[...], preferred_element_type=F32)
                + jnp.dot(dfb, wft_ref[...], preferred_element_type=F32))
        x = x_ref[...]
        r = lax.rsqrt(jnp.mean(x * x, axis=-1, keepdims=True) + EPS)
        hat = x * r
        gd = d_xn * g1_ref[...]
        dx_ref[...] = dh_ref[...] + r * (gd - hat * jnp.mean(gd * hat, axis=-1, keepdims=True))
        gg1_ref[...] += jnp.sum(d_xn * hat, axis=0, keepdims=True)

    def tile(w):
        return pl.BlockSpec((tm, w), lambda b, i: (b * nt + (nt - 1 - i), 0))

    def acc(w):
        return pl.BlockSpec((1, w), lambda b, i: (0, 0))

    return pl.pallas_call(
        body, name="proj_bwd", grid=(nb, nt),
        in_specs=[tile(MAIN_W), tile(512), tile(512), tile(512), tile(8 * LANES), tile(8 * LANES), tile(8 * LANES), tile(LANES),
                  _const_spec((1, LANES)), tile(D_MODEL), tile(D_MODEL), _const_spec((MAIN_W, D_MODEL)),
                  _const_spec((LANES, D_MODEL)), _const_spec((1, D_MODEL)), _const_spec((1, 512)), _const_spec((1, 128)),
                  _const_spec((1, 512)), _const_spec((1, 512))],
        out_specs=[tile(D_MODEL), tile(MAIN_W), tile(LANES), acc(512), acc(128), acc(512), acc(512), acc(D_MODEL), acc(LANES)],
        out_shape=[jax.ShapeDtypeStruct((t, D_MODEL), F32), jax.ShapeDtypeStruct((t, MAIN_W), BF16),
                   jax.ShapeDtypeStruct((t, LANES), BF16), jax.ShapeDtypeStruct((1, 512), F32),
                   jax.ShapeDtypeStruct((1, 128), F32), jax.ShapeDtypeStruct((1, 512), F32),
                   jax.ShapeDtypeStruct((1, 512), F32), jax.ShapeDtypeStruct((1, D_MODEL), F32),
                   jax.ShapeDtypeStruct((1, LANES), F32)],
        scratch_shapes=[pltpu.VMEM((1, LANES), F32), pltpu.VMEM((tm, LANES), F32)],
        compiler_params=_params(("arbitrary", "arbitrary"), VMEM_LIMIT_WIDE),
    )(raw, dqa, dkae, dvae, dqb, dkb, dvb, fl, bf_row, x2, dh, w_main_t, w_f_t, g1, gqa, gka, gqb, gkb)


IN_PAD = 304


def _local_step(x, tgt, w_in_t, rest, g1, b_forget, qna, kna, sinks, qnb, knb, g2,
                tm=512, bt=1024, btf=1024, tq=4096, wk=4096, wkb=8192, distributed=False):
    nb, s, _ = x.shape
    t = nb * s
    x2, tgt2 = x.reshape(t, D_MODEL), tgt.reshape(t, D_MODEL)
    g1r, g2r = g1.reshape(1, D_MODEL), g2.reshape(1, D_MODEL)
    gqa, gka = jnp.tile(qna, 8).reshape(1, 512), jnp.tile(kna, 2).reshape(1, 128)
    gqb, gkb = jnp.tile(qnb, 8).reshape(1, 512), jnp.tile(knb, 8).reshape(1, 512)
    bf_row = jnp.pad(b_forget, (0, LANES - 8)).reshape(1, LANES)
    sink_row = jnp.pad(sinks, (0, LANES - 8)).reshape(1, LANES)
    w_main_t = w_in_t[0:MAIN_W]
    w_f_t = jnp.pad(w_in_t[MAIN_W:IN_W], ((0, LANES - 8), (0, 0)))

    xn, raw, fl, qa, kae, vae, q_aug, k_aug, v_aug = _norm_proj(x2, g1r, w_main_t, w_f_t, gqa, gka, gqb, gkb, bf_row, s, tm)
    ma, lse_a = _swa_fwd(qa, kae, vae, sink_row, nb, s, tq)
    if distributed:
        mb, ql_aug, w_out, w_up, w_down, w_up_t = _fox_fwd(q_aug, k_aug, v_aug, nb, s, btf, shards=rest)
    else:
        mb, ql_aug = _fox_fwd(q_aug, k_aug, v_aug, nb, s, btf)
        w_out, w_up, w_down, w_up_t = rest
    w_out, w_down = w_out.reshape(D_MODEL, D_MODEL), w_down.reshape(D_FF, D_MODEL)
    h, hn, hid, dy, dyb, loss_acc = _mlp_fwd(x2, ma, mb, tgt2, w_out, g2r, w_up, w_down, tm)

    du, dh, dhb, dma, do_aug, dla, gg2 = _mlp_bwd(dy, hid, h, ma, mb, w_down, w_up_t.reshape(D_FF, D_MODEL), w_out, g2r, tm)
    g_down = _wgrad(hid, dyb, "wgrad_down", 512, 1024, wkb, BF16).reshape(N_DEV, 512, D_MODEL)
    g_up = _wgrad(hn, du, "wgrad_up", 1024, 512, wkb, BF16, col_blocks=True)
    g_out = jnp.concatenate([_wgrad(ma, dhb, "wgrad_out_a", 512, 1024, wk, BF16),
                             _wgrad(mb, dhb, "wgrad_out_b", 512, 1024, wk, BF16)], axis=0).reshape(N_DEV, 128, D_MODEL)

    dqa, dkae, dvae, dsink = _swa_bwd(qa, kae, vae, dma, sink_row, lse_a, dla, nb, s, tq)
    fox = _fox_bwd(ql_aug, k_aug, v_aug, do_aug, nb, s, bt, exch=(g_out, g_up, g_down) if distributed else ())
    dqb, dkb, dvb = fox[:3]
    if distributed:
        g_out, g_up, g_down = fox[3:]
    grad_x, dproj, dfb, ggqa, ggka, ggqb, ggkb, gg1, gbf = _proj_bwd(
        raw, dqa, dkae, dvae, dqb, dkb, dvb, fl, bf_row, x2, dh, w_main_t, w_f_t, g1r, gqa, gka, gqb, gkb, nb, s, tm)
    g_main_t, g_gate_t = _wgrad(dproj, xn, "wgrad_in", 768, 1024, wk, a2=dfb)
    g_in_t = jnp.concatenate([g_main_t, g_gate_t[0:8]], axis=0)

    small = (gg1.reshape(D_MODEL), gbf[0, 0:8], ggqa.reshape(8, 64).sum(0), ggka.reshape(2, 64).sum(0),
             dsink.sum(0)[:, 0:2, 0].reshape(8), ggqb.reshape(8, 64).sum(0), ggkb.reshape(8, 64).sum(0),
             gg2.reshape(D_MODEL))
    return loss_acc[0, 0], grad_x.reshape(nb, s, D_MODEL), g_in_t, g_out, g_up, g_down, small


def _all_gather(shard):
    def body(x_ref, out_ref, send_sems, recv_sems, local_sem):
        start, forward, finish = _gather_steps([(x_ref, out_ref)], send_sems, recv_sems, local_sem)
        start()
        forward()
        finish()

    return pl.pallas_call(
        body, name="gather_w_in", out_shape=jax.ShapeDtypeStruct((N_DEV,) + shard.shape, shard.dtype),
        in_specs=[ANY], out_specs=ANY,
        scratch_shapes=[pltpu.SemaphoreType.DMA((N_SEM,)), pltpu.SemaphoreType.DMA((N_SEM,)), pltpu.SemaphoreType.DMA((1,))],
    )(shard)


def _exchange(*arrays):
    n_ex = len(arrays)

    def body(*refs):
        start, finish = _exchange_steps(list(zip(refs[:n_ex], refs[n_ex:2 * n_ex])), *refs[2 * n_ex:])
        start()
        finish()

    return pl.pallas_call(
        body, name="exchange_tail", out_shape=[jax.ShapeDtypeStruct(a.shape, a.dtype) for a in arrays],
        in_specs=[ANY] * n_ex, out_specs=[ANY] * n_ex,
        scratch_shapes=[pltpu.SemaphoreType.DMA((N_SEM * n_ex,)), pltpu.SemaphoreType.DMA((N_SEM * n_ex,)),
                        pltpu.SemaphoreType.DMA((n_ex,))],
    )(*arrays)


def _sum_adamw(recv, w, m, v, tr, name):
    _, r, n = recv.shape

    def body(r_ref, w_ref, m_ref, v_ref, g_ref, d_ref, nm_ref, nv_ref):
        g = r_ref[0].astype(F32)
        for s in range(1, N_DEV):
            g = g + r_ref[s].astype(F32)
        g_ref[...] = g
        nm = ADAM_B1 * m_ref[...] + (1.0 - ADAM_B1) * g
        nv = ADAM_B2 * v_ref[...] + (1.0 - ADAM_B2) * (g * g)
        m_hat = nm / (1.0 - ADAM_B1 ** ADAM_STEP)
        v_hat = nv / (1.0 - ADAM_B2 ** ADAM_STEP)
        d_ref[...] = -ADAM_LR * (m_hat / (jnp.sqrt(v_hat) + ADAM_EPS) + ADAM_WD * w_ref[...])
        nm_ref[...] = nm
        nv_ref[...] = nv

    tile = pl.BlockSpec((tr, n), lambda i: (i, 0))
    shp = jax.ShapeDtypeStruct((r, n), F32)
    return pl.pallas_call(
        body, name=name, grid=(r // tr,),
        in_specs=[pl.BlockSpec((N_DEV, tr, n), lambda i: (0, i, 0)), tile, tile, tile],
        out_specs=[tile, tile, tile, tile], out_shape=[shp, shp, shp, shp],
        compiler_params=_params(("arbitrary",)),
    )(recv, w, m, v)


def _small_rows(g1, bf, qna, kna, sk, qnb, knb, g2):
    row2 = jnp.concatenate([bf, qna, kna, sk, qnb, knb])
    return jnp.zeros((8, D_MODEL), F32).at[0].set(g1).at[1].set(g2).at[2, 0:row2.shape[0]].set(row2)


def _in_rows(w_in_s):
    return jnp.pad(w_in_s.T, ((0, IN_PAD - IN_SHARD), (0, 0)))


def kernel(x, attn_norm_g, w_in, b_forget, q_norm_a, k_norm_a, sink_logits, q_norm_b, k_norm_b, w_out, mlp_norm_g, w_up, w_down, loss_target, m_attn_norm_g, m_w_in, m_b_forget, m_q_norm_a, m_k_norm_a, m_sink_logits, m_q_norm_b, m_k_norm_b, m_w_out, m_mlp_norm_g, m_w_up, m_w_down, v_attn_norm_g, v_w_in, v_b_forget, v_q_norm_a, v_k_norm_a, v_sink_logits, v_q_norm_b, v_k_norm_b, v_w_out, v_mlp_norm_g, v_w_up, v_w_down):
    w_in_r = _in_rows(w_in)
    w_in_t = _all_gather(w_in_r.astype(BF16))[:, 0:IN_SHARD].reshape(IN_W, D_MODEL)
    w_up_b = w_up.astype(BF16)
    rest = (w_out.astype(BF16), w_up_b, w_down.astype(BF16), w_up_b.T)

    loss_part, grad_x, g_in_t, r_out, r_up, r_down, small = _local_step(
        x, loss_target, w_in_t, rest, attn_norm_g, b_forget, q_norm_a, k_norm_a, sink_logits, q_norm_b, k_norm_b, mlp_norm_g,
        distributed=True)

    g_in_blocks = jnp.pad(g_in_t.reshape(N_DEV, IN_SHARD, D_MODEL), ((0, 0), (0, IN_PAD - IN_SHARD), (0, 0))).astype(BF16)
    small_blocks = jnp.broadcast_to(_small_rows(*small).at[3, 0].set(loss_part), (N_DEV, 8, D_MODEL))
    r_in, r_small = _exchange(g_in_blocks, small_blocks)

    small_w = _small_rows(attn_norm_g, b_forget, q_norm_a, k_norm_a, sink_logits, q_norm_b, k_norm_b, mlp_norm_g)
    small_m = _small_rows(m_attn_norm_g, m_b_forget, m_q_norm_a, m_k_norm_a, m_sink_logits, m_q_norm_b, m_k_norm_b, m_mlp_norm_g)
    small_v = _small_rows(v_attn_norm_g, v_b_forget, v_q_norm_a, v_k_norm_a, v_sink_logits, v_q_norm_b, v_k_norm_b, v_mlp_norm_g)
    o_in = [a[0:IN_SHARD].T for a in _sum_adamw(r_in, w_in_r, _in_rows(m_w_in), _in_rows(v_w_in), IN_PAD, "adamw_in")]
    o_out = _sum_adamw(r_out, w_out, m_w_out, v_w_out, 128, "adamw_out")
    o_up = _sum_adamw(r_up, w_up, m_w_up, v_w_up, 256, "adamw_up")
    o_down = _sum_adamw(r_down, w_down, m_w_down, v_w_down, 128, "adamw_down")
    o_small = _sum_adamw(r_small, small_w, small_m, small_v, 8, "adamw_small")

    def leaves(i):
        row2 = o_small[i][2]
        return (o_small[i][0], o_in[i], row2[0:8], row2[8:72], row2[72:136], row2[136:144], row2[144:208], row2[208:272],
                o_out[i], o_small[i][1], o_up[i], o_down[i])

    return (o_small[0][3, 0], grad_x, *leaves(0), *leaves(1), *leaves(2), *leaves(3))
```

```python
import functools

import jax
import jax.numpy as jnp
from jax import lax
from jax.experimental import pallas as pl
from jax.experimental.pallas import tpu as pltpu

F32 = jnp.float32
BF16 = jnp.bfloat16

D_MODEL = 1024
HEAD_DIM = 64
N_DEV = 8
D_FF = 4096
MAIN_W = 2304
IN_W = 2312
IN_SHARD = 289
WINDOW = 128
EPS = 1e-6
SCALE = 0.125
LOG2E = 1.4426950408889634
LANES = 128
NEG_INF = float("-inf")

ADAM_LR = 0.001
ADAM_B1 = 0.9
ADAM_B2 = 0.999
ADAM_EPS = 1e-08
ADAM_WD = 0.01
ADAM_STEP = 10

VMEM_LIMIT = 56 * 1024 * 1024
VMEM_LIMIT_WIDE = 62 * 1024 * 1024


def _params(sem, vmem=VMEM_LIMIT):
    return pltpu.CompilerParams(dimension_semantics=sem, vmem_limit_bytes=vmem)


def _const_spec(shape):
    nd = len(shape)
    return pl.BlockSpec(shape, lambda *_: (0,) * nd, pipeline_mode=pl.Buffered(1))


def _lane(shape):
    return lax.broadcasted_iota(jnp.int32, shape, len(shape) - 1)


def _head_ones(n):
    r = lax.shift_right_logical(lax.broadcasted_iota(jnp.int32, (n, n), 0), 6)
    c = lax.shift_right_logical(lax.broadcasted_iota(jnp.int32, (n, n), 1), 6)
    return (r == c).astype(BF16)


def _head_sum(v):
    w = v.shape[1]
    vb = v.astype(BF16)
    if w <= 256:
        return jnp.dot(vb, _head_ones(w), preferred_element_type=F32)
    ones = _head_ones(256)
    return jnp.concatenate([jnp.dot(vb[:, s:s + 256], ones, preferred_element_type=F32) for s in range(0, w, 256)], axis=1)


def _head_norm(seg, gain):
    rs = lax.rsqrt(_head_sum(seg * seg) * (1.0 / HEAD_DIM) + EPS)
    return seg * rs * gain


def _head_norm_bwd(seg, gain, d_out):
    rs = lax.rsqrt(_head_sum(seg * seg) * (1.0 / HEAD_DIM) + EPS)
    hat = seg * rs
    gd = d_out * gain
    d_seg = rs * (gd - hat * (_head_sum(gd * hat) * (1.0 / HEAD_DIM)))
    return d_seg, d_out * hat


def _expand_kv(v):
    r = pltpu.roll(v, 64, axis=1)
    lo = _lane(v.shape) < 64
    return jnp.concatenate([jnp.where(lo, v, r), jnp.where(lo, r, v)], axis=1)


def _fold_kv(e4):
    t0 = e4[:, 0:128] + e4[:, 128:256]
    t1 = e4[:, 256:384] + e4[:, 384:512]
    t0 = t0 + pltpu.roll(t0, 64, axis=1)
    t1 = t1 + pltpu.roll(t1, 64, axis=1)
    return jnp.where(_lane(t0.shape) < 64, t0, t1)


def _pick_lane(blk, idx):
    return jnp.sum(jnp.where(_lane(blk.shape) == idx, blk, 0.0), axis=1, keepdims=True)


def _nt(a, b):
    return lax.dot_general(a, b, (((1,), (1,)), ((), ())), preferred_element_type=F32)


def _tn(a, b):
    return lax.dot_general(a, b, (((0,), (0,)), ((), ())), preferred_element_type=F32)


def _norm_proj(x2, g1, w_main_t, w_f_t, gqa, gka, gqb, gkb, bf_row, s, tm):
    t = x2.shape[0]
    nt = s // tm

    def body(x_ref, g1_ref, wm_ref, wf_ref, gqa_ref, gka_ref, gqb_ref, gkb_ref, b_ref,
             xn_ref, raw_ref, fl_ref, qa_ref, kae_ref, vae_ref, qo_ref, ko_ref, vo_ref, carry, c_ref):
        @pl.when(lax.rem(pl.program_id(0), nt) == 0)
        def _():
            carry[...] = jnp.zeros_like(carry)

        x = x_ref[...]
        r = lax.rsqrt(jnp.mean(x * x, axis=-1, keepdims=True) + EPS)
        xn = (x * r * g1_ref[...]).astype(BF16)
        xn_ref[...] = xn
        proj = _nt(xn, wm_ref[...])
        raw_ref[...] = proj
        fl = _nt(xn, wf_ref[...])
        fl_ref[...] = fl
        qa_ref[...] = _head_norm(proj[:, 0:512], gqa_ref[...]).astype(BF16)
        kae_ref[...] = _expand_kv(_head_norm(proj[:, 512:640], gka_ref[...])).astype(BF16)
        vae_ref[...] = _expand_kv(proj[:, 640:768]).astype(BF16)

        z = fl + b_ref[...]
        e = jnp.exp(-jnp.abs(z))
        u = 1.0 + e
        log1p = jnp.where(u == 1.0, e, jnp.log(u) * (e / (u - 1.0)))
        lf = jnp.minimum(z, 0.0) - log1p
        for r0 in range(0, tm, 256):
            c_ref[r0:r0 + 256, :] = _tri_dot(256, False, lf[r0:r0 + 256]) + carry[...]
            carry[...] = c_ref[pl.ds(r0 + 255, 1), :]
        c2 = c_ref[...] * LOG2E
        qb = _head_norm(proj[:, 768:1280], gqb_ref[...]) * (SCALE * LOG2E)
        kb = _head_norm(proj[:, 1280:1792], gkb_ref[...])
        lane = _lane((tm, LANES))
        for h in range(8):
            j, half = h // 2, h % 2
            pair, blk = slice(LANES * j, LANES * (j + 1)), slice(LANES * h, LANES * (h + 1))
            feat = _spread3(c2[:, h:h + 1], (tm, LANES), (L_CK, L_CQ))
            q = _put_ones(_head_block(qb[:, pair], half), (L_CK, L_CK + 1, L_CK + 2))
            qo_ref[:, blk] = jnp.where((lane >= L_CQ) & (lane < L_CQ + 3), feat, q).astype(BF16)
            k = _put_ones(_head_block(kb[:, pair], half), tuple(range(L_CQ, L_CQ + 6)))
            ko_ref[:, blk] = jnp.where((lane >= L_CK) & (lane < L_CK + 3), -feat, k).astype(BF16)
            v = _head_block(proj[:, 1792 + LANES * j:1792 + LANES * (j + 1)], half)
            vo_ref[:, blk] = _put_ones(v, (L_ONE, L_DELTA, L_DELTA + 1, L_DELTA + 2)).astype(BF16)

    def tile(w):
        return pl.BlockSpec((tm, w), lambda i: (i, 0))

    aug = jax.ShapeDtypeStruct((t, 8 * LANES), BF16)
    return pl.pallas_call(
        body, name="norm_proj", grid=(t // tm,),
        in_specs=[tile(D_MODEL), _const_spec((1, D_MODEL)), _const_spec((MAIN_W, D_MODEL)), _const_spec((LANES, D_MODEL)),
                  _const_spec((1, 512)), _const_spec((1, 128)), _const_spec((1, 512)), _const_spec((1, 512)),
                  _const_spec((1, LANES))],
        out_specs=[tile(D_MODEL), tile(MAIN_W), tile(LANES), tile(512), tile(256), tile(256)] + [tile(8 * LANES)] * 3,
        out_shape=[jax.ShapeDtypeStruct((t, D_MODEL), BF16), jax.ShapeDtypeStruct((t, MAIN_W), F32),
                   jax.ShapeDtypeStruct((t, LANES), F32), jax.ShapeDtypeStruct((t, 512), BF16),
                   jax.ShapeDtypeStruct((t, 256), BF16), jax.ShapeDtypeStruct((t, 256), BF16), aug, aug, aug],
        scratch_shapes=[pltpu.VMEM((1, LANES), F32), pltpu.VMEM((tm, LANES), F32)],
        compiler_params=_params(("arbitrary",)),
    )(x2, g1, w_main_t, w_f_t, gqa, gka, gqb, gkb, bf_row)


def _tri_dot(n, upper, v):
    r = lax.broadcasted_iota(jnp.int32, (n, n), 0)
    c = lax.broadcasted_iota(jnp.int32, (n, n), 1)
    tri = ((c >= r) if upper else (c <= r)).astype(BF16)
    hi = v.astype(BF16)
    mid = (v - hi.astype(F32)).astype(BF16)
    lo = (v - hi.astype(F32) - mid.astype(F32)).astype(BF16)
    return (jnp.dot(tri, hi, preferred_element_type=F32) + jnp.dot(tri, mid, preferred_element_type=F32)
            + jnp.dot(tri, lo, preferred_element_type=F32))


def _slope(p, hh):
    out = jnp.float32(2.0 ** -(2 * 3 + hh + 1))
    for pp in (2, 1, 0):
        out = jnp.where(p == pp, jnp.float32(2.0 ** -(2 * pp + hh + 1)), out)
    return out


def _swa_windows(ref, i, tq):
    nsub = tq // WINDOW
    cur = ref[pl.ds(pl.multiple_of(i * tq, tq), tq), :].reshape(nsub, WINDOW, LANES)
    first = ref[pl.ds(pl.multiple_of(jnp.maximum(i * tq - WINDOW, 0), WINDOW), WINDOW), :].reshape(1, WINDOW, LANES)
    return jnp.concatenate([jnp.concatenate([first, cur[0:nsub - 1]], axis=0), cur], axis=1)


def _both_heads(x3, lo):
    zero = jnp.zeros_like(x3)
    return jnp.concatenate([jnp.where(lo, x3, zero), jnp.where(lo, zero, x3)], axis=0)


def _swa_head_consts(sink_ref, p, i, nsub):
    bidx = lax.broadcasted_iota(jnp.int32, (2 * nsub, 1, 1), 0)
    is_a = bidx < nsub
    slope = jnp.where(is_a, _slope(p, 0), _slope(p, 1))
    sinks = sink_ref[...]
    sink = jnp.where(is_a, _pick_lane(sinks, 2 * p).reshape(1, 1, 1), _pick_lane(sinks, 2 * p + 1).reshape(1, 1, 1))
    first = (i == 0) & ((bidx == 0) | (bidx == nsub))
    return slope, sink, first


def _swa_fwd(qa, kae, vae, sink_row, nb, s, tq):
    t = qa.shape[0]
    nq = s // tq
    nsub = tq // WINDOW

    def body(q_ref, k_ref, v_ref, sink_ref, o_ref, lse_ref):
        p, i = pl.program_id(1), pl.program_id(2)
        lo = _lane((1, 1, LANES)) < 64
        kk, vv = _swa_windows(k_ref, i, tq), _swa_windows(v_ref, i, tq)
        qs = (q_ref[...].astype(F32) * SCALE).astype(BF16).reshape(nsub, WINDOW, LANES)
        q8 = _both_heads(qs, lo)
        s8 = jnp.einsum("bqd,bkd->bqk", q8, jnp.concatenate([kk, kk], axis=0), preferred_element_type=F32)
        row = lax.broadcasted_iota(jnp.int32, (1, WINDOW, 2 * WINDOW), 1)
        col = lax.broadcasted_iota(jnp.int32, (1, WINDOW, 2 * WINDOW), 2)
        dist = row + WINDOW - col
        slope, sink, first = _swa_head_consts(sink_ref, p, i, nsub)
        valid = (dist >= 0) & (dist < WINDOW) & ((col >= WINDOW) | jnp.logical_not(first))
        s8 = jnp.where(valid, s8 - slope * dist.astype(F32), NEG_INF)
        m = jnp.maximum(jnp.max(s8, axis=2, keepdims=True), sink)
        e = jnp.exp(s8 - m)
        den = jnp.sum(e, axis=2, keepdims=True) + jnp.exp(sink - m)
        pr = (e * (1.0 / den)).astype(BF16)
        o8 = jnp.einsum("bqk,bkd->bqd", pr, jnp.concatenate([vv, vv], axis=0), preferred_element_type=F32)
        lse8 = m + jnp.log(den)
        o_ref[...] = jnp.where(lo, o8[0:nsub], o8[nsub:]).astype(BF16).reshape(tq, LANES)
        lse_ref[...] = jnp.where(lo, lse8[0:nsub], lse8[nsub:]).reshape(tq, LANES)

    return pl.pallas_call(
        body, name="swa_fwd", grid=(nb, 4, nq),
        in_specs=[pl.BlockSpec((tq, LANES), lambda b, p, i: (b * nq + i, p)),
                  pl.BlockSpec((s, LANES), lambda b, p, i: (b, lax.shift_right_logical(p, 1))),
                  pl.BlockSpec((s, LANES), lambda b, p, i: (b, lax.shift_right_logical(p, 1))),
                  pl.BlockSpec((1, LANES), lambda b, p, i: (0, 0))],
        out_specs=[pl.BlockSpec((tq, LANES), lambda b, p, i: (b * nq + i, p)),
                   pl.BlockSpec((None, tq, LANES), lambda b, p, i: (p, b * nq + i, 0))],
        out_shape=[jax.ShapeDtypeStruct((t, 512), BF16), jax.ShapeDtypeStruct((4, t, LANES), F32)],
        compiler_params=_params(("arbitrary", "arbitrary", "arbitrary")),
    )(qa, kae, vae, sink_row)


def _swa_bwd(qa, kae, vae, do_a, sink_row, lse, delta, nb, s, tq):
    t = qa.shape[0]
    nq = s // tq
    nsub = tq // WINDOW

    def body(q_ref, do_ref, k_ref, v_ref, sink_ref, lse_ref, dl_ref, dq_ref, dk_ref, dv_ref, ds_ref):
        p, i = pl.program_id(1), pl.program_id(2)

        @pl.when(i == 0)
        def _():
            ds_ref[...] = jnp.zeros_like(ds_ref)

        lo = _lane((1, 1, LANES)) < 64
        kk, vv = _swa_windows(k_ref, i, tq), _swa_windows(v_ref, i, tq)
        kks = (kk.astype(F32) * SCALE).astype(BF16)
        k8, v8 = jnp.concatenate([kks, kks], axis=0), jnp.concatenate([vv, vv], axis=0)
        q8 = _both_heads(q_ref[...].reshape(nsub, WINDOW, LANES), lo)
        do8 = _both_heads(do_ref[...].reshape(nsub, WINDOW, LANES), lo)
        cur = pl.multiple_of(i * tq, tq)
        sub = lax.broadcasted_iota(jnp.int32, (WINDOW, WINDOW), 0)
        lse_t = [lse_ref[u * WINDOW:(u + 1) * WINDOW, :].T for u in range(nsub)]
        dl_t = [dl_ref[u * WINDOW:(u + 1) * WINDOW, :].T for u in range(nsub)]
        lse8 = jnp.concatenate([t_[64 * hh:64 * hh + 1, :].reshape(1, 1, WINDOW) for hh in range(2) for t_ in lse_t], axis=0)
        dl8 = jnp.concatenate([jnp.sum(jnp.where(sub == 2 * p + hh, t_, 0.0), axis=0, keepdims=True).reshape(1, 1, WINDOW)
                               for hh in range(2) for t_ in dl_t], axis=0)
        row = lax.broadcasted_iota(jnp.int32, (1, 2 * WINDOW, WINDOW), 1)
        col = lax.broadcasted_iota(jnp.int32, (1, 2 * WINDOW, WINDOW), 2)
        dist = col + WINDOW - row
        slope, sink, first = _swa_head_consts(sink_ref, p, i, nsub)
        valid = (dist >= 0) & (dist < WINDOW) & ((row >= WINDOW) | jnp.logical_not(first))
        st = jnp.einsum("bkd,bqd->bkq", k8, q8, preferred_element_type=F32) - slope * dist.astype(F32) - lse8
        pt = jnp.where(valid, jnp.exp(jnp.where(valid, st, 0.0)), 0.0)
        dpt = jnp.einsum("bkd,bqd->bkq", v8, do8, preferred_element_type=F32)
        dst = pt * (dpt - dl8)
        ptb, dstb = pt.astype(BF16), dst.astype(BF16)
        dv8 = jnp.einsum("bkq,bqd->bkd", ptb, do8, preferred_element_type=F32)
        dk8 = jnp.einsum("bkq,bqd->bkd", dstb, q8, preferred_element_type=F32) * SCALE
        dq8 = jnp.einsum("bkq,bkd->bqd", dstb, k8, preferred_element_type=F32)
        dq_ref[...] = jnp.where(lo, dq8[0:nsub], dq8[nsub:]).reshape(tq, LANES)

        psd = jnp.exp(sink - lse8) * dl8
        row_h = lax.broadcasted_iota(jnp.int32, (8, LANES), 0)
        for hh in range(2):
            tot = jnp.sum(jnp.sum(psd[hh * nsub:(hh + 1) * nsub], axis=2, keepdims=True), axis=0, keepdims=True)
            ds_ref[...] += jnp.where(row_h == hh, -tot.reshape(1, 1), 0.0)

        prev = pl.multiple_of(jnp.maximum(i * tq - WINDOW, 0), WINDOW)
        for g8, g_ref in ((dk8, dk_ref), (dv8, dv_ref)):
            g4 = g8[0:nsub] + g8[nsub:]
            own, before = g4[:, WINDOW:, :], g4[:, 0:WINDOW, :]
            shifted = jnp.concatenate([before[1:nsub], jnp.zeros((1, WINDOW, LANES), F32)], axis=0)
            g_ref[pl.ds(cur, tq), :] = (own + shifted).reshape(tq, LANES)
            g_ref[pl.ds(prev, WINDOW), :] += before[0]

    return pl.pallas_call(
        body, name="swa_bwd", grid=(nb, 4, nq),
        in_specs=[pl.BlockSpec((tq, LANES), lambda b, p, i: (b * nq + i, p)),
                  pl.BlockSpec((tq, LANES), lambda b, p, i: (b * nq + i, p)),
                  pl.BlockSpec((s, LANES), lambda b, p, i: (b, lax.shift_right_logical(p, 1))),
                  pl.BlockSpec((s, LANES), lambda b, p, i: (b, lax.shift_right_logical(p, 1))),
                  pl.BlockSpec((1, LANES), lambda b, p, i: (0, 0)),
                  pl.BlockSpec((None, tq, LANES), lambda b, p, i: (p, b * nq + i, 0)),
                  pl.BlockSpec((tq, LANES), lambda b, p, i: (b * nq + i, 0))],
        out_specs=[pl.BlockSpec((tq, LANES), lambda b, p, i: (b * nq + i, p)),
                   pl.BlockSpec((s, LANES), lambda b, p, i: (b, p)),
                   pl.BlockSpec((s, LANES), lambda b, p, i: (b, p)),
                   pl.BlockSpec((None, None, 8, LANES), lambda b, p, i: (b, p, 0, 0))],
        out_shape=[jax.ShapeDtypeStruct((t, 512), F32), jax.ShapeDtypeStruct((t, 512), F32),
                   jax.ShapeDtypeStruct((t, 512), F32), jax.ShapeDtypeStruct((nb, 4, 8, LANES), F32)],
        compiler_params=_params(("arbitrary", "arbitrary", "arbitrary")),
    )(qa, do_a, kae, vae, sink_row, lse, delta)


MESH = pl.DeviceIdType.MESH
ANY = pl.BlockSpec(memory_space=pl.ANY)
N_SEM = 7


def _gather_steps(pairs, send_sems, recv_sems, local_sems):
    x, y, c = lax.axis_index("x"), lax.axis_index("y"), lax.axis_index("c")
    me, sibling = (x, y, c), (x, y, 1 - c)
    chips = [(1 - x, y), (x, 1 - y), (1 - x, 1 - y)]
    mine, first, passed, landed, last = [], [], [], [], []
    for a, (x_ref, out_ref) in enumerate(pairs):
        def slot(px, py, pc, out_ref=out_ref):
            return out_ref.at[4 * px + 2 * py + pc]

        def copy(k, block, to, src=None, a=a, slot=slot):
            return pltpu.make_async_remote_copy(
                src_ref=slot(*block) if src is None else src, dst_ref=slot(*block),
                send_sem=send_sems.at[N_SEM * a + k], recv_sem=recv_sems.at[N_SEM * a + k], device_id=to, device_id_type=MESH)

        mine.append(pltpu.make_async_copy(x_ref, slot(*me), local_sems.at[a]))
        first += [copy(0, me, sibling, src=x_ref)] + [copy(1 + j, me, (*chip, c), src=x_ref) for j, chip in enumerate(chips)]
        passed += [copy(4 + j, (*chip, c), sibling) for j, chip in enumerate(chips)]
        landed += [copy(1 + j, (*chip, c), me) for j, chip in enumerate(chips)]
        last += [copy(0, sibling, me)] + [copy(4 + j, (*chip, 1 - c), me) for j, chip in enumerate(chips)]

    def start():
        for cp in mine + first:
            cp.start()

    def forward():
        for arrived, onward in zip(landed, passed):
            arrived.wait_recv()
            onward.start()

    def finish():
        for cp in last:
            cp.wait_recv()
        for cp in first + passed:
            cp.wait_send()
        for cp in mine:
            cp.wait()

    return start, forward, finish


def _exchange_steps(pairs, send_sems, recv_sems, local_sems):
    x, y, c = lax.axis_index("x"), lax.axis_index("y"), lax.axis_index("c")
    my_id = 4 * x + 2 * y + c
    local, remote = [], []
    for a, (src, dst) in enumerate(pairs):
        local.append(pltpu.make_async_copy(src.at[my_id], dst.at[my_id], local_sems.at[a]))
        for k in range(1, N_DEV):
            px = 1 - x if k & 4 else x
            py = 1 - y if k & 2 else y
            pc = 1 - c if k & 1 else c
            remote.append(pltpu.make_async_remote_copy(
                src_ref=src.at[4 * px + 2 * py + pc], dst_ref=dst.at[my_id],
                send_sem=send_sems.at[N_SEM * a + k - 1], recv_sem=recv_sems.at[N_SEM * a + k - 1],
                device_id=(px, py, pc), device_id_type=MESH))

    def start():
        for cp in local + remote:
            cp.start()

    def finish():
        for cp in remote:
            cp.wait_recv()
        for cp in remote:
            cp.wait_send()
        for cp in local:
            cp.wait()

    return start, finish


L_ONE = 64
L_CK = 65
L_CQ = 68
L_LSE = 71
L_DELTA = 74


def _head_block(pair, half):
    y = pair if half == 0 else pltpu.roll(pair, 64, axis=1)
    return jnp.where(_lane(pair.shape) < 64, y, 0.0)


def _put3(blk, lane0, col):
    lane = _lane(blk.shape)
    hi = col.astype(BF16).astype(F32)
    mid = (col - hi).astype(BF16).astype(F32)
    lo = (col - hi - mid).astype(BF16).astype(F32)
    return jnp.where(lane == lane0, hi, jnp.where(lane == lane0 + 1, mid, jnp.where(lane == lane0 + 2, lo, blk)))


def _spread3(col, shape, lane0s):
    lane = _lane(shape)
    hi = col.astype(BF16).astype(F32)
    mid = (col - hi).astype(BF16).astype(F32)
    lo = (col - hi - mid).astype(BF16).astype(F32)

    def at(k):
        return functools.reduce(jnp.logical_or, [lane == ln + k for ln in lane0s])

    return jnp.where(at(0), hi, jnp.where(at(1), mid, jnp.where(at(2), lo, 0.0)))


def _put_ones(blk, lanes):
    lane = _lane(blk.shape)
    hit = functools.reduce(jnp.logical_or, [lane == ln for ln in lanes])
    return jnp.where(hit, 1.0, blk)


def _to_pairs(ref):
    out = []
    for j in range(4):
        a, b = ref[:, 2 * LANES * j:2 * LANES * j + LANES], ref[:, 2 * LANES * j + LANES:2 * LANES * (j + 1)]
        out.append(jnp.where(_lane(a.shape) < 64, a, pltpu.roll(b, 64, axis=1)))
    return jnp.concatenate(out, axis=1)


def _fox_fwd(q_aug, k_aug, v_aug, nb, s, bt, shards=()):
    t = q_aug.shape[0]
    nq = s // bt
    n_in, n_sh = 3, len(shards)

    def body(*refs):
        q_ref, k_ref, v_ref = refs[:n_in]
        o_ref, ql_ref = refs[n_in + n_sh:n_in + n_sh + 2]
        if shards:
            srcs, dsts = refs[n_in:n_in + n_sh], refs[n_in + n_sh + 2:n_in + 2 * n_sh + 2]
            start, forward, finish = _gather_steps(list(zip(srcs, dsts)), *refs[n_in + 2 * n_sh + 2:])
            step = (pl.program_id(0) * 4 + pl.program_id(1)) * nq + pl.program_id(2)
            pl.when(step == 0)(start)
            pl.when(step == nb * 3 * nq)(forward)
        i = pl.program_id(2)
        sls = [slice(LANES * hh, LANES * (hh + 1)) for hh in range(2)]
        qhs = [q_ref[:, sl] for sl in sls]

        def update(m, acc, qrows, start, size, sl, causal):
            sc = _nt(qrows, k_ref[pl.ds(start, size), sl])
            if causal:
                row = lax.broadcasted_iota(jnp.int32, sc.shape, 0)
                col = lax.broadcasted_iota(jnp.int32, sc.shape, 1)
                sc = jnp.where(row >= col, sc, NEG_INF)
            m_new = jnp.maximum(m, jnp.max(sc, axis=1, keepdims=True))
            pr = jnp.exp2(sc - m_new).astype(BF16)
            acc = jnp.exp2(m - m_new) * acc + jnp.dot(pr, v_ref[pl.ds(start, size), sl], preferred_element_type=F32)
            return m_new, acc

        def blk(kb_i, carry):
            start = pl.multiple_of(kb_i * bt, bt)
            return tuple(update(m, acc, qh, start, bt, sl, False) for (m, acc), qh, sl in zip(carry, qhs, sls))

        def diag_blk(carry):
            start = pl.multiple_of(i * bt, bt)
            return tuple(update(m, acc, qh, start, bt, sl, True) for (m, acc), qh, sl in zip(carry, qhs, sls))

        init = tuple((jnp.full((bt, 1), NEG_INF, F32), jnp.zeros((bt, LANES), F32)) for _ in range(2))
        carry = lax.fori_loop(0, i, blk, init)
        outs = []
        for (m, acc), qh, sl in zip(diag_blk(carry), qhs, sls):
            l = acc[:, L_ONE:L_ONE + 1]
            outs.append(acc * (1.0 / l))
            ql_ref[:, sl] = _put3(qh.astype(F32), L_LSE, -(m + jnp.log(l) * LOG2E)).astype(BF16)
        o_ref[...] = jnp.where(_lane((1, LANES)) < 64, outs[0], pltpu.roll(outs[1], 64, axis=1)).astype(BF16)
        if shards:
            pl.when(step == nb * 4 * nq - 1)(finish)

    in_specs = [pl.BlockSpec((bt, 2 * LANES), lambda b, j, i: (b * nq + i, j)),
                pl.BlockSpec((s, 2 * LANES), lambda b, j, i: (b, j)),
                pl.BlockSpec((s, 2 * LANES), lambda b, j, i: (b, j))]
    out_specs = [pl.BlockSpec((bt, LANES), lambda b, j, i: (b * nq + i, j)),
                 pl.BlockSpec((bt, 2 * LANES), lambda b, j, i: (b * nq + i, j))]
    out_shape = [jax.ShapeDtypeStruct((t, 512), BF16), jax.ShapeDtypeStruct((t, 8 * LANES), BF16)]
    args, scratch = [q_aug, k_aug, v_aug, *shards], []
    if shards:
        in_specs += [ANY] * n_sh
        out_specs += [ANY] * n_sh
        out_shape += [jax.ShapeDtypeStruct((N_DEV,) + sh.shape, sh.dtype) for sh in shards]
        scratch = [pltpu.SemaphoreType.DMA((N_SEM * n_sh,)), pltpu.SemaphoreType.DMA((N_SEM * n_sh,)),
                   pltpu.SemaphoreType.DMA((n_sh,))]
    return pl.pallas_call(
        body, name="fox_fwd", grid=(nb, 4, nq), in_specs=in_specs, out_specs=out_specs, out_shape=out_shape,
        scratch_shapes=scratch, compiler_params=_params(("arbitrary", "arbitrary", "arbitrary")),
    )(*args)


def _fox_bwd(ql_aug, k_aug, v_aug, do_aug, nb, s, bt, exch=()):
    t = ql_aug.shape[0]
    nk = s // bt
    n_in, n_out, n_ex = 4, 3, len(exch)

    def body(*refs):
        q_ref, do_ref, k_ref, v_ref = refs[:n_in]
        dq_ref, dk_ref, dv_ref = refs[n_in + n_ex:n_in + n_ex + n_out]
        if exch:
            srcs = refs[n_in:n_in + n_ex]
            dsts = refs[n_in + n_ex + n_out:n_in + 2 * n_ex + n_out]
            start, finish = _exchange_steps(list(zip(srcs, dsts)), *refs[n_in + 2 * n_ex + n_out:])
            step = (pl.program_id(0) * 4 + pl.program_id(1)) * nk + pl.program_id(2)
            pl.when(step == 0)(start)
        kb_i = pl.program_id(2)

        @pl.when(kb_i == 0)
        def _():
            dq_ref[...] = jnp.zeros_like(dq_ref)

        row = lax.broadcasted_iota(jnp.int32, (bt, bt), 0)
        col = lax.broadcasted_iota(jnp.int32, (bt, bt), 1)
        sls = [slice(LANES * hh, LANES * (hh + 1)) for hh in range(2)]
        khs, vhs = [k_ref[:, sl] for sl in sls], [v_ref[:, sl] for sl in sls]

        def blk(qi, carry, diag):
            start = pl.multiple_of(qi * bt, bt)
            new = []
            for (dk_a, dv_a), kh, vh, sl in zip(carry, khs, vhs, sls):
                qblk, doblk = q_ref[pl.ds(start, bt), sl], do_ref[pl.ds(start, bt), sl]
                st = _nt(kh, qblk)
                if diag:
                    pt = jnp.where(col >= row, jnp.exp2(jnp.where(col >= row, st, 0.0)), 0.0)
                else:
                    pt = jnp.exp2(st)
                dst = pt * _nt(vh, doblk)
                ptb, dstb = pt.astype(BF16), dst.astype(BF16)
                dv_a = dv_a + jnp.dot(ptb, doblk, preferred_element_type=F32)
                dk_a = dk_a + jnp.dot(dstb, qblk, preferred_element_type=F32)
                dq_ref[pl.ds(start, bt), sl] += _tn(dstb, kh)
                new.append((dk_a, dv_a))
            return tuple(new)

        zero = jnp.zeros((bt, LANES), F32)
        carry = blk(kb_i, ((zero, zero), (zero, zero)), True)
        carry = lax.fori_loop(kb_i + 1, nk, lambda qi, c: blk(qi, c, False), carry)
        for (dk_acc, dv_acc), sl in zip(carry, sls):
            dk_ref[:, sl] = dk_acc
            dv_ref[:, sl] = dv_acc
        if exch:
            pl.when(step == nb * 4 * nk - 1)(finish)

    scratch = []
    if exch:
        scratch = [pltpu.SemaphoreType.DMA((N_SEM * n_ex,)), pltpu.SemaphoreType.DMA((N_SEM * n_ex,)),
                   pltpu.SemaphoreType.DMA((n_ex,))]
    whole = pl.BlockSpec((s, 2 * LANES), lambda b, j, kb_i: (b, j))
    tile = pl.BlockSpec((bt, 2 * LANES), lambda b, j, kb_i: (b * nk + kb_i, j))
    shp = jax.ShapeDtypeStruct((t, 8 * LANES), F32)
    return pl.pallas_call(
        body, name="fox_bwd", grid=(nb, 4, nk),
        in_specs=[whole, whole, tile, tile] + [ANY] * n_ex,
        out_specs=[whole, tile, tile] + [ANY] * n_ex,
        out_shape=[shp, shp, shp] + [jax.ShapeDtypeStruct(e.shape, e.dtype) for e in exch],
        scratch_shapes=scratch, compiler_params=_params(("arbitrary", "arbitrary", "arbitrary")),
    )(ql_aug, do_aug, k_aug, v_aug, *exch)


FF_BLK = D_FF // N_DEV


def _mlp_fwd(x2, ma, mb, tgt, w_out, g2, w_up, w_down, tm):
    t = x2.shape[0]

    def body(x_ref, ma_ref, mb_ref, tg_ref, wo_ref, g2_ref, wu_ref, wd_ref,
             h_ref, hn_ref, hid_ref, dy_ref, dyb_ref, loss_ref):
        @pl.when(pl.program_id(0) == 0)
        def _():
            loss_ref[...] = jnp.zeros_like(loss_ref)

        h = (x_ref[...] + jnp.dot(ma_ref[...], wo_ref[0:512, :], preferred_element_type=F32)
             + jnp.dot(mb_ref[...], wo_ref[512:1024, :], preferred_element_type=F32))
        h_ref[...] = h
        r = lax.rsqrt(jnp.mean(h * h, axis=-1, keepdims=True) + EPS)
        hn = (h * r * g2_ref[...]).astype(BF16)
        hn_ref[...] = hn
        for d in range(N_DEV):
            u = jnp.maximum(jnp.dot(hn, wu_ref[d], preferred_element_type=F32), 0.0)
            hid_ref[:, FF_BLK * d:FF_BLK * (d + 1)] = (u * u).astype(BF16)
        y = h + jnp.dot(hid_ref[...], wd_ref[...], preferred_element_type=F32)
        err = y - tg_ref[...]
        dy = err * (1.0 / D_MODEL)
        dy_ref[...] = dy
        dyb_ref[...] = dy.astype(BF16)
        part =0.5 * jnp.sum(jnp.sum(err * err, axis=1, keepdims=True) * (1.0 / D_MODEL), axis=0, keepdims=True)
        loss_ref[...] += part

    def tile(w):
        return pl.BlockSpec((tm, w), lambda i: (i, 0))

    return pl.pallas_call(
        body, name="mlp_fwd", grid=(t // tm,),
        in_specs=[tile(D_MODEL), tile(512), tile(512), tile(D_MODEL), _const_spec((D_MODEL, D_MODEL)),
                  _const_spec((1, D_MODEL)), _const_spec((N_DEV, D_MODEL, FF_BLK)), _const_spec((D_FF, D_MODEL))],
        out_specs=[tile(D_MODEL), tile(D_MODEL), tile(D_FF), tile(D_MODEL), tile(D_MODEL),
                   pl.BlockSpec((8, LANES), lambda i: (0, 0))],
        out_shape=[jax.ShapeDtypeStruct((t, D_MODEL), F32), jax.ShapeDtypeStruct((t, D_MODEL), BF16),
                   jax.ShapeDtypeStruct((t, D_FF), BF16), jax.ShapeDtypeStruct((t, D_MODEL), F32),
                   jax.ShapeDtypeStruct((t, D_MODEL), BF16), jax.ShapeDtypeStruct((8, LANES), F32)],
        compiler_params=_params(("arbitrary",)),
    )(x2, ma, mb, tgt, w_out, g2, w_up, w_down)


def _mlp_bwd(dy, hid, h, ma, mb, w_down, w_up_t, w_out, g2, tm):
    t = dy.shape[0]

    def body(dy_ref, hid_ref, h_ref, ma_ref, mb_ref, wd_ref, wut_ref, wo_ref, g2_ref,
             du_ref, dh_ref, dhb_ref, dma_ref, dob_ref, dla_ref, gg_ref):
        @pl.when(pl.program_id(0) == 0)
        def _():
            gg_ref[...] = jnp.zeros_like(gg_ref)

        dy = dy_ref[...]
        d_hid = _nt(dy.astype(BF16), wd_ref[...])
        du = (d_hid * (2.0 * jnp.sqrt(hid_ref[...].astype(F32)))).astype(BF16)
        du_ref[...] = du
        d_hn = jnp.dot(du, wut_ref[...], preferred_element_type=F32)
        h = h_ref[...]
        r = lax.rsqrt(jnp.mean(h * h, axis=-1, keepdims=True) + EPS)
        hat = h * r
        gd = d_hn * g2_ref[...]
        dh = dy + r * (gd - hat * jnp.mean(gd * hat, axis=-1, keepdims=True))
        gg_ref[...] += jnp.sum(d_hn * hat, axis=0, keepdims=True)
        dh_ref[...] = dh
        dhb = dh.astype(BF16)
        dhb_ref[...] = dhb
        dm = _nt(dhb, wo_ref[...]).astype(BF16)
        dma, dmb = dm[:, 0:512], dm[:, 512:1024]
        dma_ref[...] = dma
        sel = (lax.shift_right_logical(lax.broadcasted_iota(jnp.int32, (512, LANES), 0), 6)
               == lax.broadcasted_iota(jnp.int32, (512, LANES), 1)).astype(BF16)
        dla_ref[...] = jnp.dot((dma.astype(F32) * ma_ref[...].astype(F32)).astype(BF16), sel, preferred_element_type=F32)
        dmb32 = dmb.astype(F32)
        dlb = jnp.dot((dmb32 * mb_ref[...].astype(F32)).astype(BF16), sel, preferred_element_type=F32)
        for hd in range(8):
            blk = _head_block(dmb32[:, LANES * (hd // 2):LANES * (hd // 2 + 1)], hd % 2)
            dob_ref[:, LANES * hd:LANES * (hd + 1)] = _put3(blk, L_DELTA, -dlb[:, hd:hd + 1]).astype(BF16)

    def tile(w):
        return pl.BlockSpec((tm, w), lambda i: (i, 0))

    return pl.pallas_call(
        body, name="mlp_bwd", grid=(t // tm,),
        in_specs=[tile(D_MODEL), tile(D_FF), tile(D_MODEL), tile(512), tile(512), _const_spec((D_FF, D_MODEL)),
                  _const_spec((D_FF, D_MODEL)), _const_spec((D_MODEL, D_MODEL)), _const_spec((1, D_MODEL))],
        out_specs=[tile(D_FF), tile(D_MODEL), tile(D_MODEL), tile(512), tile(8 * LANES), tile(LANES),
                   pl.BlockSpec((1, D_MODEL), lambda i: (0, 0))],
        out_shape=[jax.ShapeDtypeStruct((t, D_FF), BF16), jax.ShapeDtypeStruct((t, D_MODEL), F32),
                   jax.ShapeDtypeStruct((t, D_MODEL), BF16), jax.ShapeDtypeStruct((t, 512), BF16),
                   jax.ShapeDtypeStruct((t, 8 * LANES), BF16), jax.ShapeDtypeStruct((t, LANES), F32),
                   jax.ShapeDtypeStruct((1, D_MODEL), F32)],
        compiler_params=_params(("arbitrary",), VMEM_LIMIT_WIDE),
    )(dy, hid, h, ma, mb, w_down, w_up_t, w_out, g2)


def _wgrad(a, b, name, bm, bn, tk, out_dtype=F32, col_blocks=False, a2=None):
    t, m = a.shape
    n = b.shape[1]
    bm, bn = min(bm, m), min(bn, n)
    nk = t // tk

    def body(*refs):
        if a2 is None:
            a_ref, b_ref, o_ref, acc = refs
        else:
            a_ref, b_ref, a2_ref, o_ref, o2_ref, acc, acc2 = refs
        i, k = pl.program_id(0), pl.program_id(2)

        @pl.when(k == 0)
        def _():
            acc[...] = jnp.zeros_like(acc)

        acc[...] += _tn(a_ref[...], b_ref[...])

        @pl.when(k == nk - 1)
        def _():
            o_ref[...] = acc[...].astype(out_dtype)

        if a2 is not None:
            @pl.when((i == 0) & (k == 0))
            def _():
                acc2[...] = jnp.zeros_like(acc2)

            @pl.when(i == 0)
            def _():
                acc2[...] += _tn(a2_ref[...], b_ref[...])

            @pl.when((i == 0) & (k == nk - 1))
            def _():
                o2_ref[...] = acc2[...]

    if col_blocks:
        out_spec = pl.BlockSpec((None, bm, bn), lambda i, j, k: (j, i, 0))
        out_shape = jax.ShapeDtypeStruct((n // bn, m, bn), out_dtype)
    else:
        out_spec = pl.BlockSpec((bm, bn), lambda i, j, k: (i, j))
        out_shape = jax.ShapeDtypeStruct((m, n), out_dtype)
    in_specs = [pl.BlockSpec((tk, bm), lambda i, j, k: (k, i)), pl.BlockSpec((tk, bn), lambda i, j, k: (k, j))]
    out_specs, out_shapes, scratch, args = [out_spec], [out_shape], [pltpu.VMEM((bm, bn), F32)], [a, b]
    if a2 is not None:
        m2 = a2.shape[1]
        in_specs.append(pl.BlockSpec((tk, m2), lambda i, j, k: (k, 0)))
        out_specs.append(pl.BlockSpec((m2, n), lambda i, j, k: (0, 0)))
        out_shapes.append(jax.ShapeDtypeStruct((m2, n), F32))
        scratch.append(pltpu.VMEM((m2, n), F32))
        args.append(a2)
    out = pl.pallas_call(
        body, name=name, grid=(m // bm, n // bn, nk), in_specs=in_specs, out_specs=out_specs, out_shape=out_shapes,
        scratch_shapes=scratch, compiler_params=_params(("arbitrary", "arbitrary", "arbitrary")),
    )(*args)
    return out[0] if a2 is None else out


def _proj_bwd(raw, dqa, dkae, dvae, dqb, dkb, dvb, fl, bf_row, x2, dh, w_main_t, w_f_t, g1, gqa, gka, gqb, gkb, nb, s, tm):
    t = x2.shape[0]
    nt = s // tm

    def body(raw_ref, dqa_ref, dkae_ref, dvae_ref, dqb_ref, dkb_ref, dvb_ref, fl_ref, b_ref, x_ref, dh_ref,
             wmt_ref, wft_ref, g1_ref, gqa_ref, gka_ref, gqb_ref, gkb_ref,
             dx_ref, dp_ref, dfb_ref, ggqa_ref, ggka_ref, ggqb_ref, ggkb_ref, gg1_ref, gb_ref, carry, dlf_ref):
        @pl.when((pl.program_id(0) == 0) & (pl.program_id(1) == 0))
        def _():
            for r in (ggqa_ref, ggka_ref, ggqb_ref, ggkb_ref, gg1_ref, gb_ref):
                r[...] = jnp.zeros_like(r)

        @pl.when(pl.program_id(1) == 0)
        def _():
            carry[...] = jnp.zeros_like(carry)

        lane = _lane((tm, LANES))
        dc = jnp.zeros((tm, LANES), F32)
        for hd in range(8):
            col = (dqb_ref[:, LANES * hd + L_CQ:LANES * hd + L_CQ + 1] - dkb_ref[:, LANES * hd + L_CK:LANES * hd + L_CK + 1])
            dc = jnp.where(lane == hd, col, dc)
        dlf_ref[...] = _tri_dot(tm, True, dc) + carry[...]
        carry[...] = dlf_ref[pl.ds(0, 1), :]
        dfl = dlf_ref[...] * (1.0 / (1.0 + jnp.exp(fl_ref[...] + b_ref[...])))
        gb_ref[...] += jnp.sum(dfl, axis=0, keepdims=True)

        raw = raw_ref[...]
        d_qa, p_qa = _head_norm_bwd(raw[:, 0:512], gqa_ref[...], dqa_ref[...])
        d_ka, p_ka = _head_norm_bwd(raw[:, 512:640], gka_ref[...], _fold_kv(dkae_ref[...]))
        d_va = _fold_kv(dvae_ref[...])
        d_qb, p_qb = _head_norm_bwd(raw[:, 768:1280], gqb_ref[...], _to_pairs(dqb_ref) * SCALE)
        d_kb, p_kb = _head_norm_bwd(raw[:, 1280:1792], gkb_ref[...], _to_pairs(dkb_ref) * (1.0 / LOG2E))
        ggqa_ref[...] += jnp.sum(p_qa, axis=0, keepdims=True)
        ggka_ref[...] += jnp.sum(p_ka, axis=0, keepdims=True)
        ggqb_ref[...] += jnp.sum(p_qb, axis=0, keepdims=True)
        ggkb_ref[...] += jnp.sum(p_kb, axis=0, keepdims=True)
        dproj = jnp.concatenate([d_qa, d_ka, d_va, d_qb, d_kb, _to_pairs(dvb_ref)], axis=1).astype(BF16)
        dp_ref[...] = dproj
        dfb = dfl.astype(BF16)
        dfb_ref[...] = dfb
        d_xn = (jnp.dot(dproj, wmt_ref[...], preferred_element_type=F32)
                + jnp.dot(dfb, wft_ref[...], preferred_element_type=F32))
        x = x_ref[...]
        r = lax.rsqrt(jnp.mean(x * x, axis=-1, keepdims=True) + EPS)
        hat = x * r
        gd = d_xn * g1_ref[...]
        dx_ref[...] = dh_ref[...] + r * (gd - hat * jnp.mean(gd * hat, axis=-1, keepdims=True))
        gg1_ref[...] += jnp.sum(d_xn * hat, axis=0, keepdims=True)

    def tile(w):
        return pl.BlockSpec((tm, w), lambda b, i: (b * nt + (nt - 1 - i), 0))

    def acc(w):
        return pl.BlockSpec((1, w), lambda b, i: (0, 0))

    return pl.pallas_call(
        body, name="proj_bwd", grid=(nb, nt),
        in_specs=[tile(MAIN_W), tile(512), tile(512), tile(512), tile(8 * LANES), tile(8 * LANES), tile(8 * LANES), tile(LANES),
                  _const_spec((1, LANES)), tile(D_MODEL), tile(D_MODEL), _const_spec((MAIN_W, D_MODEL)),
                  _const_spec((LANES, D_MODEL)), _const_spec((1, D_MODEL)), _const_spec((1, 512)), _const_spec((1, 128)),
                  _const_spec((1, 512)), _const_spec((1, 512))],
        out_specs=[tile(D_MODEL), tile(MAIN_W), tile(LANES), acc(512), acc(128), acc(512), acc(512), acc(D_MODEL), acc(LANES)],
        out_shape=[jax.ShapeDtypeStruct((t, D_MODEL), F32), jax.ShapeDtypeStruct((t, MAIN_W), BF16),
                   jax.ShapeDtypeStruct((t, LANES), BF16), jax.ShapeDtypeStruct((1, 512), F32),
                   jax.ShapeDtypeStruct((1, 128), F32), jax.ShapeDtypeStruct((1, 512), F32),
                   jax.ShapeDtypeStruct((1, 512), F32), jax.ShapeDtypeStruct((1, D_MODEL), F32),
                   jax.ShapeDtypeStruct((1, LANES), F32)],
        scratch_shapes=[pltpu.VMEM((1, LANES), F32), pltpu.VMEM((tm, LANES), F32)],
        compiler_params=_params(("arbitrary", "arbitrary"), VMEM_LIMIT_WIDE),
    )(raw, dqa, dkae, dvae, dqb, dkb, dvb, fl, bf_row, x2, dh, w_main_t, w_f_t, g1, gqa, gka, gqb, gkb)


IN_PAD = 304


def _local_step(x, tgt, w_in_t, rest, g1, b_forget, qna, kna, sinks, qnb, knb, g2,
                tm=512, bt=1024, btf=1024, tq=4096, wk=4096, wkb=8192, distributed=False):
    nb, s, _ = x.shape
    t = nb * s
    x2, tgt2 = x.reshape(t, D_MODEL), tgt.reshape(t, D_MODEL)
    g1r, g2r = g1.reshape(1, D_MODEL), g2.reshape(1, D_MODEL)
    gqa, gka = jnp.tile(qna, 8).reshape(1, 512), jnp.tile(kna, 2).reshape(1, 128)
    gqb, gkb = jnp.tile(qnb, 8).reshape(1, 512), jnp.tile(knb, 8).reshape(1, 512)
    bf_row = jnp.pad(b_forget, (0, LANES - 8)).reshape(1, LANES)
    sink_row = jnp.pad(sinks, (0, LANES - 8)).reshape(1, LANES)
    w_main_t = w_in_t[0:MAIN_W]
    w_f_t = jnp.pad(w_in_t[MAIN_W:IN_W], ((0, LANES - 8), (0, 0)))

    xn, raw, fl, qa, kae, vae, q_aug, k_aug, v_aug = _norm_proj(x2, g1r, w_main_t, w_f_t, gqa, gka, gqb, gkb, bf_row, s, tm)
    ma, lse_a = _swa_fwd(qa, kae, vae, sink_row, nb, s, tq)
    if distributed:
        mb, ql_aug, w_out, w_up, w_down, w_up_t = _fox_fwd(q_aug, k_aug, v_aug, nb, s, btf, shards=rest)
    else:
        mb, ql_aug = _fox_fwd(q_aug, k_aug, v_aug, nb, s, btf)
        w_out, w_up, w_down, w_up_t = rest
    w_out, w_down = w_out.reshape(D_MODEL, D_MODEL), w_down.reshape(D_FF, D_MODEL)
    h, hn, hid, dy, dyb, loss_acc = _mlp_fwd(x2, ma, mb, tgt2, w_out, g2r, w_up, w_down, tm)

    du, dh, dhb, dma, do_aug, dla, gg2 = _mlp_bwd(dy, hid, h, ma, mb, w_down, w_up_t.reshape(D_FF, D_MODEL), w_out, g2r, tm)
    g_down = _wgrad(hid, dyb, "wgrad_down", 512, 1024, wkb, BF16).reshape(N_DEV, 512, D_MODEL)
    g_up = _wgrad(hn, du, "wgrad_up", 1024, 512, wkb, BF16, col_blocks=True)
    g_out = jnp.concatenate([_wgrad(ma, dhb, "wgrad_out_a", 512, 1024, wk, BF16),
                             _wgrad(mb, dhb, "wgrad_out_b", 512, 1024, wk, BF16)], axis=0).reshape(N_DEV, 128, D_MODEL)

    dqa, dkae, dvae, dsink = _swa_bwd(qa, kae, vae, dma, sink_row, lse_a, dla, nb, s, tq)
    fox = _fox_bwd(ql_aug, k_aug, v_aug, do_aug, nb, s, bt, exch=(g_out, g_up, g_down) if distributed else ())
    dqb, dkb, dvb = fox[:3]
    if distributed:
        g_out, g_up, g_down = fox[3:]
    grad_x, dproj, dfb, ggqa, ggka, ggqb, ggkb, gg1, gbf = _proj_bwd(
        raw, dqa, dkae, dvae, dqb, dkb, dvb, fl, bf_row, x2, dh, w_main_t, w_f_t, g1r, gqa, gka, gqb, gkb, nb, s, tm)
    g_main_t, g_gate_t = _wgrad(dproj, xn, "wgrad_in", 768, 1024, wk, a2=dfb)
    g_in_t = jnp.concatenate([g_main_t, g_gate_t[0:8]], axis=0)

    small = (gg1.reshape(D_MODEL), gbf[0, 0:8], ggqa.reshape(8, 64).sum(0), ggka.reshape(2, 64).sum(0),
             dsink.sum(0)[:, 0:2, 0].reshape(8), ggqb.reshape(8, 64).sum(0), ggkb.reshape(8, 64).sum(0),
             gg2.reshape(D_MODEL))
    return loss_acc[0, 0], grad_x.reshape(nb, s, D_MODEL), g_in_t, g_out, g_up, g_down, small


def _all_gather(shard):
    def body(x_ref, out_ref, send_sems, recv_sems, local_sem):
        start, forward, finish = _gather_steps([(x_ref, out_ref)], send_sems, recv_sems, local_sem)
        start()
        forward()
        finish()

    return pl.pallas_call(
        body, name="gather_w_in", out_shape=jax.ShapeDtypeStruct((N_DEV,) + shard.shape, shard.dtype),
        in_specs=[ANY], out_specs=ANY,
        scratch_shapes=[pltpu.SemaphoreType.DMA((N_SEM,)), pltpu.SemaphoreType.DMA((N_SEM,)), pltpu.SemaphoreType.DMA((1,))],
    )(shard)


def _exchange(*arrays):
    n_ex = len(arrays)

    def body(*refs):
        start, finish = _exchange_steps(list(zip(refs[:n_ex], refs[n_ex:2 * n_ex])), *refs[2 * n_ex:])
        start()
        finish()

    return pl.pallas_call(
        body, name="exchange_tail", out_shape=[jax.ShapeDtypeStruct(a.shape, a.dtype) for a in arrays],
        in_specs=[ANY] * n_ex, out_specs=[ANY] * n_ex,
        scratch_shapes=[pltpu.SemaphoreType.DMA((N_SEM * n_ex,)), pltpu.SemaphoreType.DMA((N_SEM * n_ex,)),
                        pltpu.SemaphoreType.DMA((n_ex,))],
    )(*arrays)


def _sum_adamw(recv, w, m, v, tr, name):
    _, r, n = recv.shape

    def body(r_ref, w_ref, m_ref, v_ref, g_ref, d_ref, nm_ref, nv_ref):
        g = r_ref[0].astype(F32)
        for s in range(1, N_DEV):
            g = g + r_ref[s].astype(F32)
        g_ref[...] = g
        nm = ADAM_B1 * m_ref[...] + (1.0 - ADAM_B1) * g
        nv = ADAM_B2 * v_ref[...] + (1.0 - ADAM_B2) * (g * g)
        m_hat = nm / (1.0 - ADAM_B1 ** ADAM_STEP)
        v_hat = nv / (1.0 - ADAM_B2 ** ADAM_STEP)
        d_ref[...] = -ADAM_LR * (m_hat / (jnp.sqrt(v_hat) + ADAM_EPS) + ADAM_WD * w_ref[...])
        nm_ref[...] = nm
        nv_ref[...] = nv

    tile = pl.BlockSpec((tr, n), lambda i: (i, 0))
    shp = jax.ShapeDtypeStruct((r, n), F32)
    return pl.pallas_call(
        body, name=name, grid=(r // tr,),
        in_specs=[pl.BlockSpec((N_DEV, tr, n), lambda i: (0, i, 0)), tile, tile, tile],
        out_specs=[tile, tile, tile, tile], out_shape=[shp, shp, shp, shp],
        compiler_params=_params(("arbitrary",)),
    )(recv, w, m, v)


def _small_rows(g1, bf, qna, kna, sk, qnb, knb, g2, extra=None):
    row2 = jnp.concatenate([bf, qna, kna, sk, qnb, knb])
    rows = [g1, g2, jnp.pad(row2, (0, D_MODEL - row2.shape[0]))]
    if extra is not None:
        rows.append(jnp.pad(extra.reshape(1), (0, D_MODEL - 1)))
    return jnp.pad(jnp.stack(rows), ((0, 8 - len(rows)), (0, 0)))


def _in_rows(w_in_s):
    return jnp.pad(w_in_s.T, ((0, IN_PAD - IN_SHARD), (0, 0)))


def kernel(x, attn_norm_g, w_in, b_forget, q_norm_a, k_norm_a, sink_logits, q_norm_b, k_norm_b, w_out, mlp_norm_g, w_up, w_down, loss_target, m_attn_norm_g, m_w_in, m_b_forget, m_q_norm_a, m_k_norm_a, m_sink_logits, m_q_norm_b, m_k_norm_b, m_w_out, m_mlp_norm_g, m_w_up, m_w_down, v_attn_norm_g, v_w_in, v_b_forget, v_q_norm_a, v_k_norm_a, v_sink_logits, v_q_norm_b, v_k_norm_b, v_w_out, v_mlp_norm_g, v_w_up, v_w_down):
    w_in_r = _in_rows(w_in)
    w_in_t = _all_gather(w_in_r.astype(BF16))[:, 0:IN_SHARD].reshape(IN_W, D_MODEL)
    w_up_b = w_up.astype(BF16)
    rest = (w_out.astype(BF16), w_up_b, w_down.astype(BF16), w_up_b.T)

    loss_part, grad_x, g_in_t, r_out, r_up, r_down, small = _local_step(
        x, loss_target, w_in_t, rest, attn_norm_g, b_forget, q_norm_a, k_norm_a, sink_logits, q_norm_b, k_norm_b, mlp_norm_g,
        distributed=True)

    g_in_blocks = jnp.pad(g_in_t.reshape(N_DEV, IN_SHARD, D_MODEL), ((0, 0), (0, IN_PAD - IN_SHARD), (0, 0))).astype(BF16)
    small_blocks = jnp.broadcast_to(_small_rows(*small, extra=loss_part), (N_DEV, 8, D_MODEL))
    r_in, r_small = _exchange(g_in_blocks, small_blocks)

    small_w = _small_rows(attn_norm_g, b_forget, q_norm_a, k_norm_a, sink_logits, q_norm_b, k_norm_b, mlp_norm_g)
    small_m = _small_rows(m_attn_norm_g, m_b_forget, m_q_norm_a, m_k_norm_a, m_sink_logits, m_q_norm_b, m_k_norm_b, m_mlp_norm_g)
    small_v = _small_rows(v_attn_norm_g, v_b_forget, v_q_norm_a, v_k_norm_a, v_sink_logits, v_q_norm_b, v_k_norm_b, v_mlp_norm_g)
    o_in = [a[0:IN_SHARD].T for a in _sum_adamw(r_in, w_in_r, _in_rows(m_w_in), _in_rows(v_w_in), IN_PAD, "adamw_in")]
    o_out = _sum_adamw(r_out, w_out, m_w_out, v_w_out, 128, "adamw_out")
    o_up = _sum_adamw(r_up, w_up, m_w_up, v_w_up, 256, "adamw_up")
    o_down = _sum_adamw(r_down, w_down, m_w_down, v_w_down, 128, "adamw_down")
    o_small = _sum_adamw(r_small, small_w, small_m, small_v, 8, "adamw_small")

    def leaves(i):
        row2 = o_small[i][2]
        return (o_small[i][0], o_in[i], row2[0:8], row2[8:72], row2[72:136], row2[136:144], row2[144:208], row2[208:272],
                o_out[i], o_small[i][1], o_up[i], o_down[i])

    return (o_small[0][3, 0], grad_x, *leaves(0), *leaves(1), *leaves(2), *leaves(3))
```

```python
import functools

import jax
import jax.numpy as jnp
from jax import lax
from jax.experimental import pallas as pl
from jax.experimental.pallas import tpu as pltpu

F32 = jnp.float32
BF16 = jnp.bfloat16

D_MODEL = 1024
HEAD_DIM = 64
N_DEV = 8
D_FF = 4096
MAIN_W = 2304
IN_W = 2312
IN_SHARD = 289
WINDOW = 128
EPS = 1e-6
SCALE = 0.125
LOG2E = 1.4426950408889634
LANES = 128
NEG_INF = float("-inf")

ADAM_LR = 0.001
ADAM_B1 = 0.9
ADAM_B2 = 0.999
ADAM_EPS = 1e-08
ADAM_WD = 0.01
ADAM_STEP = 10

VMEM_LIMIT = 56 * 1024 * 1024
VMEM_LIMIT_WIDE = 62 * 1024 * 1024


def _params(sem, vmem=VMEM_LIMIT):
    return pltpu.CompilerParams(dimension_semantics=sem, vmem_limit_bytes=vmem)


def _const_spec(shape):
    nd = len(shape)
    return pl.BlockSpec(shape, lambda *_: (0,) * nd, pipeline_mode=pl.Buffered(1))


def _lane(shape):
    return lax.broadcasted_iota(jnp.int32, shape, len(shape) - 1)


def _head_ones(n):
    r = lax.shift_right_logical(lax.broadcasted_iota(jnp.int32, (n, n), 0), 6)
    c = lax.shift_right_logical(lax.broadcasted_iota(jnp.int32, (n, n), 1), 6)
    return (r == c).astype(BF16)


def _head_sum(v):
    w = v.shape[1]
    vb = v.astype(BF16)
    if w <= 256:
        return jnp.dot(vb, _head_ones(w), preferred_element_type=F32)
    ones = _head_ones(256)
    return jnp.concatenate([jnp.dot(vb[:, s:s + 256], ones, preferred_element_type=F32) for s in range(0, w, 256)], axis=1)


def _head_norm(seg, gain):
    rs = lax.rsqrt(_head_sum(seg * seg) * (1.0 / HEAD_DIM) + EPS)
    return seg * rs * gain


def _head_norm_bwd(seg, gain, d_out):
    rs = lax.rsqrt(_head_sum(seg * seg) * (1.0 / HEAD_DIM) + EPS)
    hat = seg * rs
    gd = d_out * gain
    d_seg = rs * (gd - hat * (_head_sum(gd * hat) * (1.0 / HEAD_DIM)))
    return d_seg, d_out * hat


def _expand_kv(v):
    r = pltpu.roll(v, 64, axis=1)
    lo = _lane(v.shape) < 64
    return jnp.concatenate([jnp.where(lo, v, r), jnp.where(lo, r, v)], axis=1)


def _fold_kv(e4):
    t0 = e4[:, 0:128] + e4[:, 128:256]
    t1 = e4[:, 256:384] + e4[:, 384:512]
    t0 = t0 + pltpu.roll(t0, 64, axis=1)
    t1 = t1 + pltpu.roll(t1, 64, axis=1)
    return jnp.where(_lane(t0.shape) < 64, t0, t1)


def _pick_lane(blk, idx):
    return jnp.sum(jnp.where(_lane(blk.shape) == idx, blk, 0.0), axis=1, keepdims=True)


def _nt(a, b):
    return lax.dot_general(a, b, (((1,), (1,)), ((), ())), preferred_element_type=F32)


def _tn(a, b):
    return lax.dot_general(a, b, (((0,), (0,)), ((), ())), preferred_element_type=F32)


def _norm_proj(x2, g1, w_main_t, w_f_t, gqa, gka, gqb, gkb, bf_row, s, tm):
    t = x2.shape[0]
    nt = s // tm

    def body(x_ref, g1_ref, wm_ref, wf_ref, gqa_ref, gka_ref, gqb_ref, gkb_ref, b_ref,
             xn_ref, raw_ref, fl_ref, qa_ref, kae_ref, vae_ref, qo_ref, ko_ref, vo_ref, carry, c_ref):
        @pl.when(lax.rem(pl.program_id(0), nt) == 0)
        def _():
            carry[...] = jnp.zeros_like(carry)

        x = x_ref[...]
        r = lax.rsqrt(jnp.mean(x * x, axis=-1, keepdims=True) + EPS)
        xn = (x * r * g1_ref[...]).astype(BF16)
        xn_ref[...] = xn
        proj = _nt(xn, wm_ref[...])
        raw_ref[...] = proj
        fl = _nt(xn, wf_ref[...])
        fl_ref[...] = fl
        qa_ref[...] = _head_norm(proj[:, 0:512], gqa_ref[...]).astype(BF16)
        kae_ref[...] = _expand_kv(_head_norm(proj[:, 512:640], gka_ref[...])).astype(BF16)
        vae_ref[...] = _expand_kv(proj[:, 640:768]).astype(BF16)

        z = fl + b_ref[...]
        e = jnp.exp(-jnp.abs(z))
        u = 1.0 + e
        log1p = jnp.where(u == 1.0, e, jnp.log(u) * (e / (u - 1.0)))
        lf = jnp.minimum(z, 0.0) - log1p
        for r0 in range(0, tm, 256):
            c_ref[r0:r0 + 256, :] = _tri_dot(256, False, lf[r0:r0 + 256]) + carry[...]
            carry[...] = c_ref[pl.ds(r0 + 255, 1), :]
        c2 = c_ref[...] * LOG2E
        qb = _head_norm(proj[:, 768:1280], gqb_ref[...]) * (SCALE * LOG2E)
        kb = _head_norm(proj[:, 1280:1792], gkb_ref[...])
        lane = _lane((tm, LANES))
        for h in range(8):
            j, half = h // 2, h % 2
            pair, blk = slice(LANES * j, LANES * (j + 1)), slice(LANES * h, LANES * (h + 1))
            feat = _spread3(c2[:, h:h + 1], (tm, LANES), (L_CK, L_CQ))
            q = _put_ones(_head_block(qb[:, pair], half), (L_CK, L_CK + 1, L_CK + 2))
            qo_ref[:, blk] = jnp.where((lane >= L_CQ) & (lane < L_CQ + 3), feat, q).astype(BF16)
            k = _put_ones(_head_block(kb[:, pair], half), tuple(range(L_CQ, L_CQ + 6)))
            ko_ref[:, blk] = jnp.where((lane >= L_CK) & (lane < L_CK + 3), -feat, k).astype(BF16)
            v = _head_block(proj[:, 1792 + LANES * j:1792 + LANES * (j + 1)], half)
            vo_ref[:, blk] = _put_ones(v, (L_ONE, L_DELTA, L_DELTA + 1, L_DELTA + 2)).astype(BF16)

    def tile(w):
        return pl.BlockSpec((tm, w), lambda i: (i, 0))

    aug = jax.ShapeDtypeStruct((t, 8 * LANES), BF16)
    return pl.pallas_call(
        body, name="norm_proj", grid=(t // tm,),
        in_specs=[tile(D_MODEL), _const_spec((1, D_MODEL)), _const_spec((MAIN_W, D_MODEL)), _const_spec((LANES, D_MODEL)),
                  _const_spec((1, 512)), _const_spec((1, 128)), _const_spec((1, 512)), _const_spec((1, 512)),
                  _const_spec((1, LANES))],
        out_specs=[tile(D_MODEL), tile(MAIN_W), tile(LANES), tile(512), tile(256), tile(256)] + [tile(8 * LANES)] * 3,
        out_shape=[jax.ShapeDtypeStruct((t, D_MODEL), BF16), jax.ShapeDtypeStruct((t, MAIN_W), F32),
                   jax.ShapeDtypeStruct((t, LANES), F32), jax.ShapeDtypeStruct((t, 512), BF16),
                   jax.ShapeDtypeStruct((t, 256), BF16), jax.ShapeDtypeStruct((t, 256), BF16), aug, aug, aug],
        scratch_shapes=[pltpu.VMEM((1, LANES), F32), pltpu.VMEM((tm, LANES), F32)],
        compiler_params=_params(("arbitrary",)),
    )(x2, g1, w_main_t, w_f_t, gqa, gka, gqb, gkb, bf_row)


def _tri_dot(n, upper, v):
    r = lax.broadcasted_iota(jnp.int32, (n, n), 0)
    c = lax.broadcasted_iota(jnp.int32, (n, n), 1)
    tri = ((c >= r) if upper else (c <= r)).astype(BF16)
    hi = v.astype(BF16)
    mid = (v - hi.astype(F32)).astype(BF16)
    lo = (v - hi.astype(F32) - mid.astype(F32)).astype(BF16)
    return (jnp.dot(tri, hi, preferred_element_type=F32) + jnp.dot(tri, mid, preferred_element_type=F32)
            + jnp.dot(tri, lo, preferred_element_type=F32))


def _slope(p, hh):
    out = jnp.float32(2.0 ** -(2 * 3 + hh + 1))
    for pp in (2, 1, 0):
        out = jnp.where(p == pp, jnp.float32(2.0 ** -(2 * pp + hh + 1)), out)
    return out


def _swa_windows(ref, i, tq):
    nsub = tq // WINDOW
    cur = ref[pl.ds(pl.multiple_of(i * tq, tq), tq), :].reshape(nsub, WINDOW, LANES)
    first = ref[pl.ds(pl.multiple_of(jnp.maximum(i * tq - WINDOW, 0), WINDOW), WINDOW), :].reshape(1, WINDOW, LANES)
    return jnp.concatenate([jnp.concatenate([first, cur[0:nsub - 1]], axis=0), cur], axis=1)


def _both_heads(x3, lo):
    zero = jnp.zeros_like(x3)
    return jnp.concatenate([jnp.where(lo, x3, zero), jnp.where(lo, zero, x3)], axis=0)


def _swa_sinks(sink_ref, p, nsub):
    is_a = lax.broadcasted_iota(jnp.int32, (2 * nsub, 1, 1), 0) < nsub
    sinks = sink_ref[...]
    return jnp.where(is_a, _pick_lane(sinks, 2 * p).reshape(1, 1, 1), _pick_lane(sinks, 2 * p + 1).reshape(1, 1, 1))


def _swa_bias(p, i, nsub, keys_first):
    shape = (1, 2 * WINDOW, WINDOW) if keys_first else (1, WINDOW, 2 * WINDOW)
    qi = lax.broadcasted_iota(jnp.int32, shape, 2 if keys_first else 1)
    ki = lax.broadcasted_iota(jnp.int32, shape, 1 if keys_first else 2)
    dist = qi + WINDOW - ki
    band = (dist >= 0) & (dist < WINDOW)
    tiles = []
    for hh in range(2):
        bias = jnp.where(band, -_slope(p, hh) * dist.astype(F32), NEG_INF)
        tiles += [jnp.where((i == 0) & (ki < WINDOW), NEG_INF, bias)] + [bias] * (nsub - 1)
    return jnp.concatenate(tiles, axis=0)


def _swa_fwd(qa, kae, vae, sink_row, nb, s, tq):
    t = qa.shape[0]
    nq = s // tq
    nsub = tq // WINDOW

    def body(q_ref, k_ref, v_ref, sink_ref, o_ref, lse_ref):
        p, i = pl.program_id(1), pl.program_id(2)
        lo = _lane((1, 1, LANES)) < 64
        kk, vv = _swa_windows(k_ref, i, tq), _swa_windows(v_ref, i, tq)
        qs = (q_ref[...].astype(F32) * SCALE).astype(BF16).reshape(nsub, WINDOW, LANES)
        q8 = _both_heads(qs, lo)
        s8 = jnp.einsum("bqd,bkd->bqk", q8, jnp.concatenate([kk, kk], axis=0), preferred_element_type=F32)
        sink = _swa_sinks(sink_ref, p, nsub)
        s8 = s8 + _swa_bias(p, i, nsub, False)
        m = jnp.maximum(jnp.max(s8, axis=2, keepdims=True), sink)
        e = jnp.exp(s8 - m)
        den = jnp.sum(e, axis=2, keepdims=True) + jnp.exp(sink - m)
        pr = (e * (1.0 / den)).astype(BF16)
        o8 = jnp.einsum("bqk,bkd->bqd", pr, jnp.concatenate([vv, vv], axis=0), preferred_element_type=F32)
        lse8 = m + jnp.log(den)
        o_ref[...] = jnp.where(lo, o8[0:nsub], o8[nsub:]).astype(BF16).reshape(tq, LANES)
        lse_ref[...] = jnp.where(lo, lse8[0:nsub], lse8[nsub:]).reshape(tq, LANES)

    return pl.pallas_call(
        body, name="swa_fwd", grid=(nb, 4, nq),
        in_specs=[pl.BlockSpec((tq, LANES), lambda b, p, i: (b * nq + i, p)),
                  pl.BlockSpec((s, LANES), lambda b, p, i: (b, lax.shift_right_logical(p, 1))),
                  pl.BlockSpec((s, LANES), lambda b, p, i: (b, lax.shift_right_logical(p, 1))),
                  pl.BlockSpec((1, LANES), lambda b, p, i: (0, 0))],
        out_specs=[pl.BlockSpec((tq, LANES), lambda b, p, i: (b * nq + i, p)),
                   pl.BlockSpec((None, tq, LANES), lambda b, p, i: (p, b * nq + i, 0))],
        out_shape=[jax.ShapeDtypeStruct((t, 512), BF16), jax.ShapeDtypeStruct((4, t, LANES), F32)],
        compiler_params=_params(("arbitrary", "arbitrary", "arbitrary")),
    )(qa, kae, vae, sink_row)


def _swa_bwd(qa, kae, vae, do_a, sink_row, lse, delta, nb, s, tq):
    t = qa.shape[0]
    nq = s // tq
    nsub = tq // WINDOW

    def body(q_ref, do_ref, k_ref, v_ref, sink_ref, lse_ref, dl_ref, dq_ref, dk_ref, dv_ref, ds_ref):
        p, i = pl.program_id(1), pl.program_id(2)

        @pl.when(i == 0)
        def _():
            ds_ref[...] = jnp.zeros_like(ds_ref)

        lo = _lane((1, 1, LANES)) < 64
        kk, vv = _swa_windows(k_ref, i, tq), _swa_windows(v_ref, i, tq)
        kks = (kk.astype(F32) * SCALE).astype(BF16)
        k8, v8 = jnp.concatenate([kks, kks], axis=0), jnp.concatenate([vv, vv], axis=0)
        q8 = _both_heads(q_ref[...].reshape(nsub, WINDOW, LANES), lo)
        do8 = _both_heads(do_ref[...].reshape(nsub, WINDOW, LANES), lo)
        cur = pl.multiple_of(i * tq, tq)
        sub = lax.broadcasted_iota(jnp.int32, (WINDOW, WINDOW), 0)
        lse_t = [lse_ref[u * WINDOW:(u + 1) * WINDOW, :].T for u in range(nsub)]
        dl_t = [dl_ref[u * WINDOW:(u + 1) * WINDOW, :].T for u in range(nsub)]
        lse8 = jnp.concatenate([t_[64 * hh:64 * hh + 1, :].reshape(1, 1, WINDOW) for hh in range(2) for t_ in lse_t], axis=0)
        dl8 = jnp.concatenate([jnp.sum(jnp.where(sub == 2 * p + hh, t_, 0.0), axis=0, keepdims=True).reshape(1, 1, WINDOW)
                               for hh in range(2) for t_ in dl_t], axis=0)
        sink = _swa_sinks(sink_ref, p, nsub)
        st = jnp.einsum("bkd,bqd->bkq", k8, q8, preferred_element_type=F32) + _swa_bias(p, i, nsub, True) - lse8
        pt = jnp.exp(st)
        dpt = jnp.einsum("bkd,bqd->bkq", v8, do8, preferred_element_type=F32)
        dst = pt * (dpt - dl8)
        ptb, dstb = pt.astype(BF16), dst.astype(BF16)
        dv8 = jnp.einsum("bkq,bqd->bkd", ptb, do8, preferred_element_type=F32)
        dk8 = jnp.einsum("bkq,bqd->bkd", dstb, q8, preferred_element_type=F32) * SCALE
        dq8 = jnp.einsum("bkq,bkd->bqd", dstb, k8, preferred_element_type=F32)
        dq_ref[...] = jnp.where(lo, dq8[0:nsub], dq8[nsub:]).reshape(tq, LANES)

        psd = jnp.exp(sink - lse8) * dl8
        row_h = lax.broadcasted_iota(jnp.int32, (8, LANES), 0)
        for hh in range(2):
            tot = jnp.sum(jnp.sum(psd[hh * nsub:(hh + 1) * nsub], axis=2, keepdims=True), axis=0, keepdims=True)
            ds_ref[...] += jnp.where(row_h == hh, -tot.reshape(1, 1), 0.0)

        prev = pl.multiple_of(jnp.maximum(i * tq - WINDOW, 0), WINDOW)
        for g8, g_ref in ((dk8, dk_ref), (dv8, dv_ref)):
            g4 = g8[0:nsub] + g8[nsub:]
            own, before = g4[:, WINDOW:, :], g4[:, 0:WINDOW, :]
            shifted = jnp.concatenate([before[1:nsub], jnp.zeros((1, WINDOW, LANES), F32)], axis=0)
            g_ref[pl.ds(cur, tq), :] = (own + shifted).reshape(tq, LANES)
            g_ref[pl.ds(prev, WINDOW), :] += before[0]

    return pl.pallas_call(
        body, name="swa_bwd", grid=(nb, 4, nq),
        in_specs=[pl.BlockSpec((tq, LANES), lambda b, p, i: (b * nq + i, p)),
                  pl.BlockSpec((tq, LANES), lambda b, p, i: (b * nq + i, p)),
                  pl.BlockSpec((s, LANES), lambda b, p, i: (b, lax.shift_right_logical(p, 1))),
                  pl.BlockSpec((s, LANES), lambda b, p, i: (b, lax.shift_right_logical(p, 1))),
                  pl.BlockSpec((1, LANES), lambda b, p, i: (0, 0)),
                  pl.BlockSpec((None, tq, LANES), lambda b, p, i: (p, b * nq + i, 0)),
                  pl.BlockSpec((tq, LANES), lambda b, p, i: (b * nq + i, 0))],
        out_specs=[pl.BlockSpec((tq, LANES), lambda b, p, i: (b * nq + i, p)),
                   pl.BlockSpec((s, LANES), lambda b, p, i: (b, p)),
                   pl.BlockSpec((s, LANES), lambda b, p, i: (b, p)),
                   pl.BlockSpec((None, None, 8, LANES), lambda b, p, i: (b, p, 0, 0))],
        out_shape=[jax.ShapeDtypeStruct((t, 512), F32), jax.ShapeDtypeStruct((t, 512), F32),
                   jax.ShapeDtypeStruct((t, 512), F32), jax.ShapeDtypeStruct((nb, 4, 8, LANES), F32)],
        compiler_params=_params(("arbitrary", "arbitrary", "arbitrary")),
    )(qa, do_a, kae, vae, sink_row, lse, delta)


MESH = pl.DeviceIdType.MESH
ANY = pl.BlockSpec(memory_space=pl.ANY)
N_SEM = 7


def _gather_steps(pairs, send_sems, recv_sems, local_sems):
    x, y, c = lax.axis_index("x"), lax.axis_index("y"), lax.axis_index("c")
    me, sibling = (x, y, c), (x, y, 1 - c)
    chips = [(1 - x, y), (x, 1 - y), (1 - x, 1 - y)]
    mine, first, passed, landed, last = [], [], [], [], []
    for a, (x_ref, out_ref) in enumerate(pairs):
        def slot(px, py, pc, out_ref=out_ref):
            return out_ref.at[4 * px + 2 * py + pc]

        def copy(k, block, to, src=None, a=a, slot=slot):
            return pltpu.make_async_remote_copy(
                src_ref=slot(*block) if src is None else src, dst_ref=slot(*block),
                send_sem=send_sems.at[N_SEM * a + k], recv_sem=recv_sems.at[N_SEM * a + k], device_id=to, device_id_type=MESH)

        mine.append(pltpu.make_async_copy(x_ref, slot(*me), local_sems.at[a]))
        first += [copy(0, me, sibling, src=x_ref)] + [copy(1 + j, me, (*chip, c), src=x_ref) for j, chip in enumerate(chips)]
        passed += [copy(4 + j, (*chip, c), sibling) for j, chip in enumerate(chips)]
        landed += [copy(1 + j, (*chip, c), me) for j, chip in enumerate(chips)]
        last += [copy(0, sibling, me)] + [copy(4 + j, (*chip, 1 - c), me) for j, chip in enumerate(chips)]

    def start():
        for cp in mine + first:
            cp.start()

    def forward():
        for arrived, onward in zip(landed, passed):
            arrived.wait_recv()
            onward.start()

    def finish():
        for cp in last:
            cp.wait_recv()
        for cp in first + passed:
            cp.wait_send()
        for cp in mine:
            cp.wait()

    return start, forward, finish


def _exchange_steps(pairs, send_sems, recv_sems, local_sems):
    x, y, c = lax.axis_index("x"), lax.axis_index("y"), lax.axis_index("c")
    my_id = 4 * x + 2 * y + c
    local, remote = [], []
    for a, (src, dst) in enumerate(pairs):
        local.append(pltpu.make_async_copy(src.at[my_id], dst.at[my_id], local_sems.at[a]))
        for k in range(1, N_DEV):
            px = 1 - x if k & 4 else x
            py = 1 - y if k & 2 else y
            pc = 1 - c if k & 1 else c
            remote.append(pltpu.make_async_remote_copy(
                src_ref=src.at[4 * px + 2 * py + pc], dst_ref=dst.at[my_id],
                send_sem=send_sems.at[N_SEM * a + k - 1], recv_sem=recv_sems.at[N_SEM * a + k - 1],
                device_id=(px, py, pc), device_id_type=MESH))

    def start():
        for cp in local + remote:
            cp.start()

    def finish():
        for cp in remote:
            cp.wait_recv()
        for cp in remote:
            cp.wait_send()
        for cp in local:
            cp.wait()

    return start, finish


L_ONE = 64
L_CK = 65
L_CQ = 68
L_LSE = 71
L_DELTA = 74


def _head_block(pair, half):
    y = pair if half == 0 else pltpu.roll(pair, 64, axis=1)
    return jnp.where(_lane(pair.shape) < 64, y, 0.0)


def _put3(blk, lane0, col):
    lane = _lane(blk.shape)
    hi = col.astype(BF16).astype(F32)
    mid = (col - hi).astype(BF16).astype(F32)
    lo = (col - hi - mid).astype(BF16).astype(F32)
    return jnp.where(lane == lane0, hi, jnp.where(lane == lane0 + 1, mid, jnp.where(lane == lane0 + 2, lo, blk)))


def _spread3(col, shape, lane0s):
    lane = _lane(shape)
    hi = col.astype(BF16).astype(F32)
    mid = (col - hi).astype(BF16).astype(F32)
    lo = (col - hi - mid).astype(BF16).astype(F32)

    def at(k):
        return functools.reduce(jnp.logical_or, [lane == ln + k for ln in lane0s])

    return jnp.where(at(0), hi, jnp.where(at(1), mid, jnp.where(at(2), lo, 0.0)))


def _put_ones(blk, lanes):
    lane = _lane(blk.shape)
    hit = functools.reduce(jnp.logical_or, [lane == ln for ln in lanes])
    return jnp.where(hit, 1.0, blk)


def _to_pairs(ref):
    out = []
    for j in range(4):
        a, b = ref[:, 2 * LANES * j:2 * LANES * j + LANES], ref[:, 2 * LANES * j + LANES:2 * LANES * (j + 1)]
        out.append(jnp.where(_lane(a.shape) < 64, a, pltpu.roll(b, 64, axis=1)))
    return jnp.concatenate(out, axis=1)


def _fox_fwd(q_aug, k_aug, v_aug, nb, s, bt, shards=()):
    t = q_aug.shape[0]
    nq = s // bt
    n_in, n_sh = 3, len(shards)

    def body(*refs):
        q_ref, k_ref, v_ref = refs[:n_in]
        o_ref, ql_ref = refs[n_in + n_sh:n_in + n_sh + 2]
        if shards:
            srcs, dsts = refs[n_in:n_in + n_sh], refs[n_in + n_sh + 2:n_in + 2 * n_sh + 2]
            start, forward, finish = _gather_steps(list(zip(srcs, dsts)), *refs[n_in + 2 * n_sh + 2:])
            step = (pl.program_id(0) * 4 + pl.program_id(1)) * nq + pl.program_id(2)
            pl.when(step == 0)(start)
            pl.when(step == nb * 3 * nq)(forward)
        i = pl.program_id(2)
        sls = [slice(LANES * hh, LANES * (hh + 1)) for hh in range(2)]
        qhs = [q_ref[:, sl] for sl in sls]

        def update(m, acc, qrows, start, size, sl, causal):
            sc = _nt(qrows, k_ref[pl.ds(start, size), sl])
            if causal:
                row = lax.broadcasted_iota(jnp.int32, sc.shape, 0)
                col = lax.broadcasted_iota(jnp.int32, sc.shape, 1)
                sc = jnp.where(row >= col, sc, NEG_INF)
            m_new = jnp.maximum(m, jnp.max(sc, axis=1, keepdims=True))
            pr = jnp.exp2(sc - m_new).astype(BF16)
            acc = jnp.exp2(m - m_new) * acc + jnp.dot(pr, v_ref[pl.ds(start, size), sl], preferred_element_type=F32)
            return m_new, acc

        def blk(kb_i, carry):
            start = pl.multiple_of(kb_i * bt, bt)
            return tuple(update(m, acc, qh, start, bt, sl, False) for (m, acc), qh, sl in zip(carry, qhs, sls))

        def diag_blk(carry):
            start = pl.multiple_of(i * bt, bt)
            return tuple(update(m, acc, qh, start, bt, sl, True) for (m, acc), qh, sl in zip(carry, qhs, sls))

        init = tuple((jnp.full((bt, 1), NEG_INF, F32), jnp.zeros((bt, LANES), F32)) for _ in range(2))
        carry = lax.fori_loop(0, i, blk, init)
        outs = []
        for (m, acc), qh, sl in zip(diag_blk(carry), qhs, sls):
            l = acc[:, L_ONE:L_ONE + 1]
            outs.append(acc * (1.0 / l))
            ql_ref[:, sl] = _put3(qh.astype(F32), L_LSE, -(m + jnp.log(l) * LOG2E)).astype(BF16)
        o_ref[...] = jnp.where(_lane((1, LANES)) < 64, outs[0], pltpu.roll(outs[1], 64, axis=1)).astype(BF16)
        if shards:
            pl.when(step == nb * 4 * nq - 1)(finish)

    in_specs = [pl.BlockSpec((bt, 2 * LANES), lambda b, j, i: (b * nq + i, j)),
                pl.BlockSpec((s, 2 * LANES), lambda b, j, i: (b, j)),
                pl.BlockSpec((s, 2 * LANES), lambda b, j, i: (b, j))]
    out_specs = [pl.BlockSpec((bt, LANES), lambda b, j, i: (b * nq + i, j)),
                 pl.BlockSpec((bt, 2 * LANES), lambda b, j, i: (b * nq + i, j))]
    out_shape = [jax.ShapeDtypeStruct((t, 512), BF16), jax.ShapeDtypeStruct((t, 8 * LANES), BF16)]
    args, scratch = [q_aug, k_aug, v_aug, *shards], []
    if shards:
        in_specs += [ANY] * n_sh
        out_specs += [ANY] * n_sh
        out_shape += [jax.ShapeDtypeStruct((N_DEV,) + sh.shape, sh.dtype) for sh in shards]
        scratch = [pltpu.SemaphoreType.DMA((N_SEM * n_sh,)), pltpu.SemaphoreType.DMA((N_SEM * n_sh,)),
                   pltpu.SemaphoreType.DMA((n_sh,))]
    return pl.pallas_call(
        body, name="fox_fwd", grid=(nb, 4, nq), in_specs=in_specs, out_specs=out_specs, out_shape=out_shape,
        scratch_shapes=scratch, compiler_params=_params(("arbitrary", "arbitrary", "arbitrary")),
    )(*args)


def _fox_bwd(ql_aug, k_aug, v_aug, do_aug, nb, s, bt, exch=()):
    t = ql_aug.shape[0]
    nk = s // bt
    n_in, n_out, n_ex = 4, 3, len(exch)

    def body(*refs):
        q_ref, do_ref, k_ref, v_ref = refs[:n_in]
        dq_ref, dk_ref, dv_ref = refs[n_in + n_ex:n_in + n_ex + n_out]
        if exch:
            srcs = refs[n_in:n_in + n_ex]
            dsts = refs[n_in + n_ex + n_out:n_in + 2 * n_ex + n_out]
            start, finish = _exchange_steps(list(zip(srcs, dsts)), *refs[n_in + 2 * n_ex + n_out:])
            step = (pl.program_id(0) * 4 + pl.program_id(1)) * nk + pl.program_id(2)
            pl.when(step == 0)(start)
        kb_i = pl.program_id(2)

        @pl.when(kb_i == 0)
        def _():
            dq_ref[...] = jnp.zeros_like(dq_ref)

        row = lax.broadcasted_iota(jnp.int32, (bt, bt), 0)
        col = lax.broadcasted_iota(jnp.int32, (bt, bt), 1)
        sls = [slice(LANES * hh, LANES * (hh + 1)) for hh in range(2)]
        khs, vhs = [k_ref[:, sl] for sl in sls], [v_ref[:, sl] for sl in sls]

        def blk(qi, carry, diag):
            start = pl.multiple_of(qi * bt, bt)
            new = []
            for (dk_a, dv_a), kh, vh, sl in zip(carry, khs, vhs, sls):
                qblk, doblk = q_ref[pl.ds(start, bt), sl], do_ref[pl.ds(start, bt), sl]
                st = _nt(kh, qblk)
                if diag:
                    pt = jnp.where(col >= row, jnp.exp2(jnp.where(col >= row, st, 0.0)), 0.0)
                else:
                    pt = jnp.exp2(st)
                dst = pt * _nt(vh, doblk)
                ptb, dstb = pt.astype(BF16), dst.astype(BF16)
                dv_a = dv_a + jnp.dot(ptb, doblk, preferred_element_type=F32)
                dk_a = dk_a + jnp.dot(dstb, qblk, preferred_element_type=F32)
                dq_ref[pl.ds(start, bt), sl] += _tn(dstb, kh)
                new.append((dk_a, dv_a))
            return tuple(new)

        zero = jnp.zeros((bt, LANES), F32)
        carry = blk(kb_i, ((zero, zero), (zero, zero)), True)
        carry = lax.fori_loop(kb_i + 1, nk, lambda qi, c: blk(qi, c, False), carry)
        for (dk_acc, dv_acc), sl in zip(carry, sls):
            dk_ref[:, sl] = dk_acc
            dv_ref[:, sl] = dv_acc
        if exch:
            pl.when(step == nb * 4 * nk - 1)(finish)

    scratch = []
    if exch:
        scratch = [pltpu.SemaphoreType.DMA((N_SEM * n_ex,)), pltpu.SemaphoreType.DMA((N_SEM * n_ex,)),
                   pltpu.SemaphoreType.DMA((n_ex,))]
    whole = pl.BlockSpec((s, 2 * LANES), lambda b, j, kb_i: (b, j))
    tile = pl.BlockSpec((bt, 2 * LANES), lambda b, j, kb_i: (b * nk + kb_i, j))
    shp = jax.ShapeDtypeStruct((t, 8 * LANES), F32)
    return pl.pallas_call(
        body, name="fox_bwd", grid=(nb, 4, nk),
        in_specs=[whole, whole, tile, tile] + [ANY] * n_ex,
        out_specs=[whole, tile, tile] + [ANY] * n_ex,
        out_shape=[shp, shp, shp] + [jax.ShapeDtypeStruct(e.shape, e.dtype) for e in exch],
        scratch_shapes=scratch, compiler_params=_params(("arbitrary", "arbitrary", "arbitrary")),
    )(ql_aug, do_aug, k_aug, v_aug, *exch)


FF_BLK = D_FF // N_DEV


def _mlp_fwd(x2, ma, mb, tgt, w_out, g2, w_up, w_down, tm):
    t = x2.shape[0]

    def body(x_ref, ma_ref, mb_ref, tg_ref, wo_ref, g2_ref, wu_ref, wd_ref,
             h_ref, hn_ref, hid_ref, dy_ref, dyb_ref, loss_ref):
        @pl.when(pl.program_id(0) == 0)
        def _():
            loss_ref[...] = jnp.zeros_like(loss_ref)

        h = (x_ref[...] + jnp.dot(ma_ref[...], wo_ref[0:512, :], preferred_element_type=F32)
             + jnp.dot(mb_ref[...], wo_ref[512:1024, :], preferred_element_type=F32))
        h_ref[...] = h
        r = lax.rsqrt(jnp.mean(h * h, axis=-1, keepdims=True) + EPS)
        hn = (h * r * g2_ref[...]).astype(BF16)
        hn_ref[...] = hn
        for d in range(N_DEV):
            u = jnp.maximum(jnp.dot(hn, wu_ref[d], preferred_element_type=F32), 0.0)
            hid_ref[:, FF_BLK * d:FF_BLK * (d + 1)] = (u * u).astype(BF16)
        y = h + jnp.dot(hid_ref[...], wd_ref[...], preferred_element_type=F32)
        err = y - tg_ref[...]
        dy = err * (1.0 / D_MODEL)
        dy_ref[...] = dy
        dyb_ref[...] = dy.astype(BF16)
        part =0.5 * jnp.sum(jnp.sum(err * err, axis=1, keepdims=True) * (1.0 / D_MODEL), axis=0, keepdims=True)
        loss_ref[...] += part

    def tile(w):
        return pl.BlockSpec((tm, w), lambda i: (i, 0))

    return pl.pallas_call(
        body, name="mlp_fwd", grid=(t // tm,),
        in_specs=[tile(D_MODEL), tile(512), tile(512), tile(D_MODEL), _const_spec((D_MODEL, D_MODEL)),
                  _const_spec((1, D_MODEL)), _const_spec((N_DEV, D_MODEL, FF_BLK)), _const_spec((D_FF, D_MODEL))],
        out_specs=[tile(D_MODEL), tile(D_MODEL), tile(D_FF), tile(D_MODEL), tile(D_MODEL),
                   pl.BlockSpec((8, LANES), lambda i: (0, 0))],
        out_shape=[jax.ShapeDtypeStruct((t, D_MODEL), F32), jax.ShapeDtypeStruct((t, D_MODEL), BF16),
                   jax.ShapeDtypeStruct((t, D_FF), BF16), jax.ShapeDtypeStruct((t, D_MODEL), F32),
                   jax.ShapeDtypeStruct((t, D_MODEL), BF16), jax.ShapeDtypeStruct((8, LANES), F32)],
        compiler_params=_params(("arbitrary",)),
    )(x2, ma, mb, tgt, w_out, g2, w_up, w_down)


def _mlp_bwd(dy, hid, h, ma, mb, w_down, w_up_t, w_out, g2, tm):
    t = dy.shape[0]

    def body(dy_ref, hid_ref, h_ref, ma_ref, mb_ref, wd_ref, wut_ref, wo_ref, g2_ref,
             du_ref, dh_ref, dhb_ref, dma_ref, dob_ref, dla_ref, gg_ref):
        @pl.when(pl.program_id(0) == 0)
        def _():
            gg_ref[...] = jnp.zeros_like(gg_ref)

        dy = dy_ref[...]
        d_hid = _nt(dy.astype(BF16), wd_ref[...])
        du = (d_hid * (2.0 * jnp.sqrt(hid_ref[...].astype(F32)))).astype(BF16)
        du_ref[...] = du
        d_hn = jnp.dot(du, wut_ref[...], preferred_element_type=F32)
        h = h_ref[...]
        r = lax.rsqrt(jnp.mean(h * h, axis=-1, keepdims=True) + EPS)
        hat = h * r
        gd = d_hn * g2_ref[...]
        dh = dy + r * (gd - hat * jnp.mean(gd * hat, axis=-1, keepdims=True))
        gg_ref[...] += jnp.sum(d_hn * hat, axis=0, keepdims=True)
        dh_ref[...] = dh
        dhb = dh.astype(BF16)
        dhb_ref[...] = dhb
        dm = _nt(dhb, wo_ref[...]).astype(BF16)
        dma, dmb = dm[:, 0:512], dm[:, 512:1024]
        dma_ref[...] = dma
        sel = (lax.shift_right_logical(lax.broadcasted_iota(jnp.int32, (512, LANES), 0), 6)
               == lax.broadcasted_iota(jnp.int32, (512, LANES), 1)).astype(BF16)
        dla_ref[...] = jnp.dot((dma.astype(F32) * ma_ref[...].astype(F32)).astype(BF16), sel, preferred_element_type=F32)
        dmb32 = dmb.astype(F32)
        dlb = jnp.dot((dmb32 * mb_ref[...].astype(F32)).astype(BF16), sel, preferred_element_type=F32)
        for hd in range(8):
            blk = _head_block(dmb32[:, LANES * (hd // 2):LANES * (hd // 2 + 1)], hd % 2)
            dob_ref[:, LANES * hd:LANES * (hd + 1)] = _put3(blk, L_DELTA, -dlb[:, hd:hd + 1]).astype(BF16)

    def tile(w):
        return pl.BlockSpec((tm, w), lambda i: (i, 0))

    return pl.pallas_call(
        body, name="mlp_bwd", grid=(t // tm,),
        in_specs=[tile(D_MODEL), tile(D_FF), tile(D_MODEL), tile(512), tile(512), _const_spec((D_FF, D_MODEL)),
                  _const_spec((D_FF, D_MODEL)), _const_spec((D_MODEL, D_MODEL)), _const_spec((1, D_MODEL))],
        out_specs=[tile(D_FF), tile(D_MODEL), tile(D_MODEL), tile(512), tile(8 * LANES), tile(LANES),
                   pl.BlockSpec((1, D_MODEL), lambda i: (0, 0))],
        out_shape=[jax.ShapeDtypeStruct((t, D_FF), BF16), jax.ShapeDtypeStruct((t, D_MODEL), F32),
                   jax.ShapeDtypeStruct((t, D_MODEL), BF16), jax.ShapeDtypeStruct((t, 512), BF16),
                   jax.ShapeDtypeStruct((t, 8 * LANES), BF16), jax.ShapeDtypeStruct((t, LANES), F32),
                   jax.ShapeDtypeStruct((1, D_MODEL), F32)],
        compiler_params=_params(("arbitrary",), VMEM_LIMIT_WIDE),
    )(dy, hid, h, ma, mb, w_down, w_up_t, w_out, g2)


def _wgrad(a, b, name, bm, bn, tk, out_dtype=F32, col_blocks=False, a2=None):
    t, m = a.shape
    n = b.shape[1]
    bm, bn = min(bm, m), min(bn, n)
    nk = t // tk

    def body(*refs):
        if a2 is None:
            a_ref, b_ref, o_ref, acc = refs
        else:
            a_ref, b_ref, a2_ref, o_ref, o2_ref, acc, acc2 = refs
        i, k = pl.program_id(0), pl.program_id(2)

        @pl.when(k == 0)
        def _():
            acc[...] = jnp.zeros_like(acc)

        acc[...] += _tn(a_ref[...], b_ref[...])

        @pl.when(k == nk - 1)
        def _():
            o_ref[...] = acc[...].astype(out_dtype)

        if a2 is not None:
            @pl.when((i == 0) & (k == 0))
            def _():
                acc2[...] = jnp.zeros_like(acc2)

            @pl.when(i == 0)
            def _():
                acc2[...] += _tn(a2_ref[...], b_ref[...])

            @pl.when((i == 0) & (k == nk - 1))
            def _():
                o2_ref[...] = acc2[...]

    if col_blocks:
        out_spec = pl.BlockSpec((None, bm, bn), lambda i, j, k: (j, i, 0))
        out_shape = jax.ShapeDtypeStruct((n // bn, m, bn), out_dtype)
    else:
        out_spec = pl.BlockSpec((bm, bn), lambda i, j, k: (i, j))
        out_shape = jax.ShapeDtypeStruct((m, n), out_dtype)
    in_specs = [pl.BlockSpec((tk, bm), lambda i, j, k: (k, i)), pl.BlockSpec((tk, bn), lambda i, j, k: (k, j))]
    out_specs, out_shapes, scratch, args = [out_spec], [out_shape], [pltpu.VMEM((bm, bn), F32)], [a, b]
    if a2 is not None:
        m2 = a2.shape[1]
        in_specs.append(pl.BlockSpec((tk, m2), lambda i, j, k: (k, 0)))
        out_specs.append(pl.BlockSpec((m2, n), lambda i, j, k: (0, 0)))
        out_shapes.append(jax.ShapeDtypeStruct((m2, n), F32))
        scratch.append(pltpu.VMEM((m2, n), F32))
        args.append(a2)
    out = pl.pallas_call(
        body, name=name, grid=(m // bm, n // bn, nk), in_specs=in_specs, out_specs=out_specs, out_shape=out_shapes,
        scratch_shapes=scratch, compiler_params=_params(("arbitrary", "arbitrary", "arbitrary")),
    )(*args)
    return out[0] if a2 is None else out


def _proj_bwd(raw, dqa, dkae, dvae, dqb, dkb, dvb, fl, bf_row, x2, dh, w_main_t, w_f_t, g1, gqa, gka, gqb, gkb, nb, s, tm):
    t = x2.shape[0]
    nt = s // tm

    def body(raw_ref, dqa_ref, dkae_ref, dvae_ref, dqb_ref, dkb_ref, dvb_ref, fl_ref, b_ref, x_ref, dh_ref,
             wmt_ref, wft_ref, g1_ref, gqa_ref, gka_ref, gqb_ref, gkb_ref,
             dx_ref, dp_ref, dfb_ref, ggqa_ref, ggka_ref, ggqb_ref, ggkb_ref, gg1_ref, gb_ref, carry, dlf_ref):
        @pl.when((pl.program_id(0) == 0) & (pl.program_id(1) == 0))
        def _():
            for r in (ggqa_ref, ggka_ref, ggqb_ref, ggkb_ref, gg1_ref, gb_ref):
                r[...] = jnp.zeros_like(r)

        @pl.when(pl.program_id(1) == 0)
        def _():
            carry[...] = jnp.zeros_like(carry)

        lane = _lane((tm, LANES))
        dc = jnp.zeros((tm, LANES), F32)
        for hd in range(8):
            col = (dqb_ref[:, LANES * hd + L_CQ:LANES * hd + L_CQ + 1] - dkb_ref[:, LANES * hd + L_CK:LANES * hd + L_CK + 1])
            dc = jnp.where(lane == hd, col, dc)
        dlf_ref[...] = _tri_dot(tm, True, dc) + carry[...]
        carry[...] = dlf_ref[pl.ds(0, 1), :]
        dfl = dlf_ref[...] * (1.0 / (1.0 + jnp.exp(fl_ref[...] + b_ref[...])))
        gb_ref[...] += jnp.sum(dfl, axis=0, keepdims=True)

        raw = raw_ref[...]
        d_qa, p_qa = _head_norm_bwd(raw[:, 0:512], gqa_ref[...], dqa_ref[...])
        d_ka, p_ka = _head_norm_bwd(raw[:, 512:640], gka_ref[...], _fold_kv(dkae_ref[...]))
        d_va = _fold_kv(dvae_ref[...])
        d_qb, p_qb = _head_norm_bwd(raw[:, 768:1280], gqb_ref[...], _to_pairs(dqb_ref) * SCALE)
        d_kb, p_kb = _head_norm_bwd(raw[:, 1280:1792], gkb_ref[...], _to_pairs(dkb_ref) * (1.0 / LOG2E))
        ggqa_ref[...] += jnp.sum(p_qa, axis=0, keepdims=True)
        ggka_ref[...] += jnp.sum(p_ka, axis=0, keepdims=True)
        ggqb_ref[...] += jnp.sum(p_qb, axis=0, keepdims=True)
        ggkb_ref[...] += jnp.sum(p_kb, axis=0, keepdims=True)
        dproj = jnp.concatenate([d_qa, d_ka, d_va, d_qb, d_kb, _to_pairs(dvb_ref)], axis=1).astype(BF16)
        dp_ref[...] = dproj
        dfb = dfl.astype(BF16)
        dfb_ref[...] = dfb
        d_xn = (jnp.dot(dproj, wmt_ref[...], preferred_element_type=F32)
                + jnp.dot(dfb, wft_ref[...], preferred_element_type=F32))
        x = x_ref[...]
        r = lax.rsqrt(jnp.mean(x * x, axis=-1, keepdims=True) + EPS)
        hat = x * r
        gd = d_xn * g1_ref[...]
        dx_ref[...] = dh_ref[...] + r * (gd - hat * jnp.mean(gd * hat, axis=-1, keepdims=True))
        gg1_ref[...] += jnp.sum(d_xn * hat, axis=0, keepdims=True)

    def tile(w):
        return pl.BlockSpec((tm, w), lambda b, i: (b * nt + (nt - 1 - i), 0))

    def acc(w):
        return pl.BlockSpec((1, w), lambda b, i: (0, 0))

    return pl.pallas_call(
        body, name="proj_bwd", grid=(nb, nt),
        in_specs=[tile(MAIN_W), tile(512), tile(512), tile(512), tile(8 * LANES), tile(8 * LANES), tile(8 * LANES), tile(LANES),
                  _const_spec((1, LANES)), tile(D_MODEL), tile(D_MODEL), _const_spec((MAIN_W, D_MODEL)),
                  _const_spec((LANES, D_MODEL)), _const_spec((1, D_MODEL)), _const_spec((1, 512)), _const_spec((1, 128)),
                  _const_spec((1, 512)), _const_spec((1, 512))],
        out_specs=[tile(D_MODEL), tile(MAIN_W), tile(LANES), acc(512), acc(128), acc(512), acc(512), acc(D_MODEL), acc(LANES)],
        out_shape=[jax.ShapeDtypeStruct((t, D_MODEL), F32), jax.ShapeDtypeStruct((t, MAIN_W), BF16),
                   jax.ShapeDtypeStruct((t, LANES), BF16), jax.ShapeDtypeStruct((1, 512), F32),
                   jax.ShapeDtypeStruct((1, 128), F32), jax.ShapeDtypeStruct((1, 512), F32),
                   jax.ShapeDtypeStruct((1, 512), F32), jax.ShapeDtypeStruct((1, D_MODEL), F32),
                   jax.ShapeDtypeStruct((1, LANES), F32)],
        scratch_shapes=[pltpu.VMEM((1, LANES), F32), pltpu.VMEM((tm, LANES), F32)],
        compiler_params=_params(("arbitrary", "arbitrary"), VMEM_LIMIT_WIDE),
    )(raw, dqa, dkae, dvae, dqb, dkb, dvb, fl, bf_row, x2, dh, w_main_t, w_f_t, g1, gqa, gka, gqb, gkb)


IN_PAD = 304


def _local_step(x, tgt, w_in_t, rest, g1, b_forget, qna, kna, sinks, qnb, knb, g2,
                tm=512, bt=1024, btf=1024, tq=4096, wk=4096, wkb=8192, distributed=False):
    nb, s, _ = x.shape
    t = nb * s
    x2, tgt2 = x.reshape(t, D_MODEL), tgt.reshape(t, D_MODEL)
    g1r, g2r = g1.reshape(1, D_MODEL), g2.reshape(1, D_MODEL)
    gqa, gka = jnp.tile(qna, 8).reshape(1, 512), jnp.tile(kna, 2).reshape(1, 128)
    gqb, gkb = jnp.tile(qnb, 8).reshape(1, 512), jnp.tile(knb, 8).reshape(1, 512)
    bf_row = jnp.pad(b_forget, (0, LANES - 8)).reshape(1, LANES)
    sink_row = jnp.pad(sinks, (0, LANES - 8)).reshape(1, LANES)
    w_main_t = w_in_t[0:MAIN_W]
    w_f_t = jnp.pad(w_in_t[MAIN_W:IN_W], ((0, LANES - 8), (0, 0)))

    xn, raw, fl, qa, kae, vae, q_aug, k_aug, v_aug = _norm_proj(x2, g1r, w_main_t, w_f_t, gqa, gka, gqb, gkb, bf_row, s, tm)
    ma, lse_a = _swa_fwd(qa, kae, vae, sink_row, nb, s, tq)
    if distributed:
        mb, ql_aug, w_out, w_up, w_down, w_up_t = _fox_fwd(q_aug, k_aug, v_aug, nb, s, btf, shards=rest)
    else:
        mb, ql_aug = _fox_fwd(q_aug, k_aug, v_aug, nb, s, btf)
        w_out, w_up, w_down, w_up_t = rest
    w_out, w_down = w_out.reshape(D_MODEL, D_MODEL), w_down.reshape(D_FF, D_MODEL)
    h, hn, hid, dy, dyb, loss_acc = _mlp_fwd(x2, ma, mb, tgt2, w_out, g2r, w_up, w_down, tm)

    du, dh, dhb, dma, do_aug, dla, gg2 = _mlp_bwd(dy, hid, h, ma, mb, w_down, w_up_t.reshape(D_FF, D_MODEL), w_out, g2r, tm)
    g_down = _wgrad(hid, dyb, "wgrad_down", 512, 1024, wkb, BF16).reshape(N_DEV, 512, D_MODEL)
    g_up = _wgrad(hn, du, "wgrad_up", 1024, 512, wkb, BF16, col_blocks=True)
    g_out = jnp.concatenate([_wgrad(ma, dhb, "wgrad_out_a", 512, 1024, wk, BF16),
                             _wgrad(mb, dhb, "wgrad_out_b", 512, 1024, wk, BF16)], axis=0).reshape(N_DEV, 128, D_MODEL)

    dqa, dkae, dvae, dsink = _swa_bwd(qa, kae, vae, dma, sink_row, lse_a, dla, nb, s, tq)
    fox = _fox_bwd(ql_aug, k_aug, v_aug, do_aug, nb, s, bt, exch=(g_out, g_up, g_down) if distributed else ())
    dqb, dkb, dvb = fox[:3]
    if distributed:
        g_out, g_up, g_down = fox[3:]
    grad_x, dproj, dfb, ggqa, ggka, ggqb, ggkb, gg1, gbf = _proj_bwd(
        raw, dqa, dkae, dvae, dqb, dkb, dvb, fl, bf_row, x2, dh, w_main_t, w_f_t, g1r, gqa, gka, gqb, gkb, nb, s, tm)
    g_main_t, g_gate_t = _wgrad(dproj, xn, "wgrad_in", 768, 1024, wk, a2=dfb)
    g_in_t = jnp.concatenate([g_main_t, g_gate_t[0:8]], axis=0)

    small = (gg1.reshape(D_MODEL), gbf[0, 0:8], ggqa.reshape(8, 64).sum(0), ggka.reshape(2, 64).sum(0),
             dsink.sum(0)[:, 0:2, 0].reshape(8), ggqb.reshape(8, 64).sum(0), ggkb.reshape(8, 64).sum(0),
             gg2.reshape(D_MODEL))
    return loss_acc[0, 0], grad_x.reshape(nb, s, D_MODEL), g_in_t, g_out, g_up, g_down, small


def _all_gather(shard):
    def body(x_ref, out_ref, send_sems, recv_sems, local_sem):
        start, forward, finish = _gather_steps([(x_ref, out_ref)], send_sems, recv_sems, local_sem)
        start()
        forward()
        finish()

    return pl.pallas_call(
        body, name="gather_w_in", out_shape=jax.ShapeDtypeStruct((N_DEV,) + shard.shape, shard.dtype),
        in_specs=[ANY], out_specs=ANY,
        scratch_shapes=[pltpu.SemaphoreType.DMA((N_SEM,)), pltpu.SemaphoreType.DMA((N_SEM,)), pltpu.SemaphoreType.DMA((1,))],
    )(shard)


def _exchange(*arrays):
    n_ex = len(arrays)

    def body(*refs):
        start, finish = _exchange_steps(list(zip(refs[:n_ex], refs[n_ex:2 * n_ex])), *refs[2 * n_ex:])
        start()
        finish()

    return pl.pallas_call(
        body, name="exchange_tail", out_shape=[jax.ShapeDtypeStruct(a.shape, a.dtype) for a in arrays],
        in_specs=[ANY] * n_ex, out_specs=[ANY] * n_ex,
        scratch_shapes=[pltpu.SemaphoreType.DMA((N_SEM * n_ex,)), pltpu.SemaphoreType.DMA((N_SEM * n_ex,)),
                        pltpu.SemaphoreType.DMA((n_ex,))],
    )(*arrays)


def _sum_adamw(recv, w, m, v, tr, name):
    _, r, n = recv.shape

    def body(r_ref, w_ref, m_ref, v_ref, g_ref, d_ref, nm_ref, nv_ref):
        g = r_ref[0].astype(F32)
        for s in range(1, N_DEV):
            g = g + r_ref[s].astype(F32)
        g_ref[...] = g
        nm = ADAM_B1 * m_ref[...] + (1.0 - ADAM_B1) * g
        nv = ADAM_B2 * v_ref[...] + (1.0 - ADAM_B2) * (g * g)
        m_hat = nm / (1.0 - ADAM_B1 ** ADAM_STEP)
        v_hat = nv / (1.0 - ADAM_B2 ** ADAM_STEP)
        d_ref[...] = -ADAM_LR * (m_hat / (jnp.sqrt(v_hat) + ADAM_EPS) + ADAM_WD * w_ref[...])
        nm_ref[...] = nm
        nv_ref[...] = nv

    tile = pl.BlockSpec((tr, n), lambda i: (i, 0))
    shp = jax.ShapeDtypeStruct((r, n), F32)
    return pl.pallas_call(
        body, name=name, grid=(r // tr,),
        in_specs=[pl.BlockSpec((N_DEV, tr, n), lambda i: (0, i, 0)), tile, tile, tile],
        out_specs=[tile, tile, tile, tile], out_shape=[shp, shp, shp, shp],
        compiler_params=_params(("arbitrary",)),
    )(recv, w, m, v)


def _small_rows(g1, bf, qna, kna, sk, qnb, knb, g2, extra=None):
    row2 = jnp.concatenate([bf, qna, kna, sk, qnb, knb])
    rows = [g1, g2, jnp.pad(row2, (0, D_MODEL - row2.shape[0]))]
    if extra is not None:
        rows.append(jnp.pad(extra.reshape(1), (0, D_MODEL - 1)))
    return jnp.pad(jnp.stack(rows), ((0, 8 - len(rows)), (0, 0)))


def _in_rows(w_in_s):
    return jnp.pad(w_in_s.T, ((0, IN_PAD - IN_SHARD), (0, 0)))


def kernel(x, attn_norm_g, w_in, b_forget, q_norm_a, k_norm_a, sink_logits, q_norm_b, k_norm_b, w_out, mlp_norm_g, w_up, w_down, loss_target, m_attn_norm_g, m_w_in, m_b_forget, m_q_norm_a, m_k_norm_a, m_sink_logits, m_q_norm_b, m_k_norm_b, m_w_out, m_mlp_norm_g, m_w_up, m_w_down, v_attn_norm_g, v_w_in, v_b_forget, v_q_norm_a, v_k_norm_a, v_sink_logits, v_q_norm_b, v_k_norm_b, v_w_out, v_mlp_norm_g, v_w_up, v_w_down):
    w_in_r = _in_rows(w_in)
    w_in_t = _all_gather(w_in_r.astype(BF16))[:, 0:IN_SHARD].reshape(IN_W, D_MODEL)
    w_up_b = w_up.astype(BF16)
    rest = (w_out.astype(BF16), w_up_b, w_down.astype(BF16), w_up_b.T)

    loss_part, grad_x, g_in_t, r_out, r_up, r_down, small = _local_step(
        x, loss_target, w_in_t, rest, attn_norm_g, b_forget, q_norm_a, k_norm_a, sink_logits, q_norm_b, k_norm_b, mlp_norm_g,
        distributed=True)

    g_in_blocks = jnp.pad(g_in_t.reshape(N_DEV, IN_SHARD, D_MODEL), ((0, 0), (0, IN_PAD - IN_SHARD), (0, 0))).astype(BF16)
    small_blocks = jnp.broadcast_to(_small_rows(*small, extra=loss_part), (N_DEV, 8, D_MODEL))
    r_in, r_small = _exchange(g_in_blocks, small_blocks)

    small_w = _small_rows(attn_norm_g, b_forget, q_norm_a, k_norm_a, sink_logits, q_norm_b, k_norm_b, mlp_norm_g)
    small_m = _small_rows(m_attn_norm_g, m_b_forget, m_q_norm_a, m_k_norm_a, m_sink_logits, m_q_norm_b, m_k_norm_b, m_mlp_norm_g)
    small_v = _small_rows(v_attn_norm_g, v_b_forget, v_q_norm_a, v_k_norm_a, v_sink_logits, v_q_norm_b, v_k_norm_b, v_mlp_norm_g)
    o_in = [a[0:IN_SHARD].T for a in _sum_adamw(r_in, w_in_r, _in_rows(m_w_in), _in_rows(v_w_in), IN_PAD, "adamw_in")]
    o_out = _sum_adamw(r_out, w_out, m_w_out, v_w_out, 128, "adamw_out")
    o_up = _sum_adamw(r_up, w_up, m_w_up, v_w_up, 256, "adamw_up")
    o_down = _sum_adamw(r_down, w_down, m_w_down, v_w_down, 128, "adamw_down")
    o_small = _sum_adamw(r_small, small_w, small_m, small_v, 8, "adamw_small")

    def leaves(i):
        row2 = o_small[i][2]
        return (o_small[i][0], o_in[i], row2[0:8], row2[8:72], row2[72:136], row2[136:144], row2[144:208], row2[208:272],
                o_out[i], o_small[i][1], o_up[i], o_down[i])

    return (o_small[0][3, 0], grad_x, *leaves(0), *leaves(1), *leaves(2), *leaves(3))
```

```python
import functools

import jax
import jax.numpy as jnp
from jax import lax
from jax.experimental import pallas as pl
from jax.experimental.pallas import tpu as pltpu

F32 = jnp.float32
BF16 = jnp.bfloat16

D_MODEL = 1024
HEAD_DIM = 64
N_DEV = 8
D_FF = 4096
MAIN_W = 2304
IN_W = 2312
IN_SHARD = 289
WINDOW = 128
EPS = 1e-6
SCALE = 0.125
LOG2E = 1.4426950408889634
LANES = 128
NEG_INF = float("-inf")

ADAM_LR = 0.001
ADAM_B1 = 0.9
ADAM_B2 = 0.999
ADAM_EPS = 1e-08
ADAM_WD = 0.01
ADAM_STEP = 10

VMEM_LIMIT = 56 * 1024 * 1024
VMEM_LIMIT_WIDE = 62 * 1024 * 1024


def _params(sem, vmem=VMEM_LIMIT):
    return pltpu.CompilerParams(dimension_semantics=sem, vmem_limit_bytes=vmem)


def _const_spec(shape):
    nd = len(shape)
    return pl.BlockSpec(shape, lambda *_: (0,) * nd, pipeline_mode=pl.Buffered(1))


def _lane(shape):
    return lax.broadcasted_iota(jnp.int32, shape, len(shape) - 1)


def _head_ones(n):
    r = lax.shift_right_logical(lax.broadcasted_iota(jnp.int32, (n, n), 0), 6)
    c = lax.shift_right_logical(lax.broadcasted_iota(jnp.int32, (n, n), 1), 6)
    return (r == c).astype(BF16)


def _head_sum(v):
    w = v.shape[1]
    vb = v.astype(BF16)
    if w <= 256:
        return jnp.dot(vb, _head_ones(w), preferred_element_type=F32)
    ones = _head_ones(256)
    return jnp.concatenate([jnp.dot(vb[:, s:s + 256], ones, preferred_element_type=F32) for s in range(0, w, 256)], axis=1)


def _head_norm(seg, gain):
    rs = lax.rsqrt(_head_sum(seg * seg) * (1.0 / HEAD_DIM) + EPS)
    return seg * rs * gain


def _head_norm_bwd(seg, gain, d_out):
    rs = lax.rsqrt(_head_sum(seg * seg) * (1.0 / HEAD_DIM) + EPS)
    hat = seg * rs
    gd = d_out * gain
    d_seg = rs * (gd - hat * (_head_sum(gd * hat) * (1.0 / HEAD_DIM)))
    return d_seg, d_out * hat


def _expand_kv(v):
    r = pltpu.roll(v, 64, axis=1)
    lo = _lane(v.shape) < 64
    return jnp.concatenate([jnp.where(lo, v, r), jnp.where(lo, r, v)], axis=1)


def _fold_kv(e4):
    t0 = e4[:, 0:128] + e4[:, 128:256]
    t1 = e4[:, 256:384] + e4[:, 384:512]
    t0 = t0 + pltpu.roll(t0, 64, axis=1)
    t1 = t1 + pltpu.roll(t1, 64, axis=1)
    return jnp.where(_lane(t0.shape) < 64, t0, t1)


def _pick_lane(blk, idx):
    return jnp.sum(jnp.where(_lane(blk.shape) == idx, blk, 0.0), axis=1, keepdims=True)


def _nt(a, b):
    return lax.dot_general(a, b, (((1,), (1,)), ((), ())), preferred_element_type=F32)


def _tn(a, b):
    return lax.dot_general(a, b, (((0,), (0,)), ((), ())), preferred_element_type=F32)


def _norm_proj(x2, g1, w_main_t, w_f_t, gqa, gka, gqb, gkb, bf_row, s, tm):
    t = x2.shape[0]
    nt = s // tm

    def body(x_ref, g1_ref, wm_ref, wf_ref, gqa_ref, gka_ref, gqb_ref, gkb_ref, b_ref,
             xn_ref, raw_ref, fl_ref, qa_ref, kae_ref, vae_ref, qo_ref, ko_ref, vo_ref, carry, c_ref):
        @pl.when(lax.rem(pl.program_id(0), nt) == 0)
        def _():
            carry[...] = jnp.zeros_like(carry)

        x = x_ref[...]
        r = lax.rsqrt(jnp.mean(x * x, axis=-1, keepdims=True) + EPS)
        xn = (x * r * g1_ref[...]).astype(BF16)
        xn_ref[...] = xn
        proj = _nt(xn, wm_ref[...])
        raw_ref[...] = proj
        fl = _nt(xn, wf_ref[...])
        fl_ref[...] = fl
        qa_ref[...] = _head_norm(proj[:, 0:512], gqa_ref[...]).astype(BF16)
        kae_ref[...] = _expand_kv(_head_norm(proj[:, 512:640], gka_ref[...])).astype(BF16)
        vae_ref[...] = _expand_kv(proj[:, 640:768]).astype(BF16)

        z = fl + b_ref[...]
        e = jnp.exp(-jnp.abs(z))
        u = 1.0 + e
        log1p = jnp.where(u == 1.0, e, jnp.log(u) * (e / (u - 1.0)))
        lf = jnp.minimum(z, 0.0) - log1p
        for r0 in range(0, tm, 256):
            c_ref[r0:r0 + 256, :] = _tri_dot(256, False, lf[r0:r0 + 256]) + carry[...]
            carry[...] = c_ref[pl.ds(r0 + 255, 1), :]
        c2 = c_ref[...] * LOG2E
        qb = _head_norm(proj[:, 768:1280], gqb_ref[...]) * (SCALE * LOG2E)
        kb = _head_norm(proj[:, 1280:1792], gkb_ref[...])
        lane = _lane((tm, LANES))
        for h in range(8):
            j, half = h // 2, h % 2
            pair, blk = slice(LANES * j, LANES * (j + 1)), slice(LANES * h, LANES * (h + 1))
            feat = _spread3(c2[:, h:h + 1], (tm, LANES), (L_CK, L_CQ))
            q = _put_ones(_head_block(qb[:, pair], half), (L_CK, L_CK + 1, L_CK + 2))
            qo_ref[:, blk] = jnp.where((lane >= L_CQ) & (lane < L_CQ + 3), feat, q).astype(BF16)
            k = _put_ones(_head_block(kb[:, pair], half), tuple(range(L_CQ, L_CQ + 6)))
            ko_ref[:, blk] = jnp.where((lane >= L_CK) & (lane < L_CK + 3), -feat, k).astype(BF16)
            v = _head_block(proj[:, 1792 + LANES * j:1792 + LANES * (j + 1)], half)
            vo_ref[:, blk] = _put_ones(v, (L_ONE, L_DELTA, L_DELTA + 1, L_DELTA + 2)).astype(BF16)

    def tile(w):
        return pl.BlockSpec((tm, w), lambda i: (i, 0))

    aug = jax.ShapeDtypeStruct((t, 8 * LANES), BF16)
    return pl.pallas_call(
        body, name="norm_proj", grid=(t // tm,),
        in_specs=[tile(D_MODEL), _const_spec((1, D_MODEL)), _const_spec((MAIN_W, D_MODEL)), _const_spec((LANES, D_MODEL)),
                  _const_spec((1, 512)), _const_spec((1, 128)), _const_spec((1, 512)), _const_spec((1, 512)),
                  _const_spec((1, LANES))],
        out_specs=[tile(D_MODEL), tile(MAIN_W), tile(LANES), tile(512), tile(256), tile(256)] + [tile(8 * LANES)] * 3,
        out_shape=[jax.ShapeDtypeStruct((t, D_MODEL), BF16), jax.ShapeDtypeStruct((t, MAIN_W), F32),
                   jax.ShapeDtypeStruct((t, LANES), F32), jax.ShapeDtypeStruct((t, 512), BF16),
                   jax.ShapeDtypeStruct((t, 256), BF16), jax.ShapeDtypeStruct((t, 256), BF16), aug, aug, aug],
        scratch_shapes=[pltpu.VMEM((1, LANES), F32), pltpu.VMEM((tm, LANES), F32)],
        compiler_params=_params(("arbitrary",)),
    )(x2, g1, w_main_t, w_f_t, gqa, gka, gqb, gkb, bf_row)


def _tri_dot(n, upper, v):
    r = lax.broadcasted_iota(jnp.int32, (n, n), 0)
    c = lax.broadcasted_iota(jnp.int32, (n, n), 1)
    tri = ((c >= r) if upper else (c <= r)).astype(BF16)
    hi = v.astype(BF16)
    mid = (v - hi.astype(F32)).astype(BF16)
    lo = (v - hi.astype(F32) - mid.astype(F32)).astype(BF16)
    return (jnp.dot(tri, hi, preferred_element_type=F32) + jnp.dot(tri, mid, preferred_element_type=F32)
            + jnp.dot(tri, lo, preferred_element_type=F32))


def _slope(p, hh):
    out = jnp.float32(2.0 ** -(2 * 3 + hh + 1))
    for pp in (2, 1, 0):
        out = jnp.where(p == pp, jnp.float32(2.0 ** -(2 * pp + hh + 1)), out)
    return out


def _swa_windows(ref, i, tq):
    nsub = tq // WINDOW
    cur = ref[pl.ds(pl.multiple_of(i * tq, tq), tq), :].reshape(nsub, WINDOW, LANES)
    first = ref[pl.ds(pl.multiple_of(jnp.maximum(i * tq - WINDOW, 0), WINDOW), WINDOW), :].reshape(1, WINDOW, LANES)
    return jnp.concatenate([jnp.concatenate([first, cur[0:nsub - 1]], axis=0), cur], axis=1)


def _both_heads(x3, lo):
    zero = jnp.zeros_like(x3)
    return jnp.concatenate([jnp.where(lo, x3, zero), jnp.where(lo, zero, x3)], axis=0)


def _swa_sinks(sink_ref, p, nsub):
    is_a = lax.broadcasted_iota(jnp.int32, (2 * nsub, 1, 1), 0) < nsub
    sinks = sink_ref[...]
    return jnp.where(is_a, _pick_lane(sinks, 2 * p).reshape(1, 1, 1), _pick_lane(sinks, 2 * p + 1).reshape(1, 1, 1))


def _swa_bias(p, i, nsub, keys_first):
    shape = (1, 2 * WINDOW, WINDOW) if keys_first else (1, WINDOW, 2 * WINDOW)
    qi = lax.broadcasted_iota(jnp.int32, shape, 2 if keys_first else 1)
    ki = lax.broadcasted_iota(jnp.int32, shape, 1 if keys_first else 2)
    dist = qi + WINDOW - ki
    band = (dist >= 0) & (dist < WINDOW)
    tiles = []
    for hh in range(2):
        bias = jnp.where(band, -_slope(p, hh) * dist.astype(F32), NEG_INF)
        tiles += [jnp.where((i == 0) & (ki < WINDOW), NEG_INF, bias)] + [bias] * (nsub - 1)
    return jnp.concatenate(tiles, axis=0)


def _swa_fwd(qa, kae, vae, sink_row, nb, s, tq):
    t = qa.shape[0]
    nq = s // tq
    nsub = tq // WINDOW

    def body(q_ref, k_ref, v_ref, sink_ref, o_ref, lse_ref):
        p, i = pl.program_id(1), pl.program_id(2)
        lo = _lane((1, 1, LANES)) < 64
        kk, vv = _swa_windows(k_ref, i, tq), _swa_windows(v_ref, i, tq)
        qs = (q_ref[...].astype(F32) * SCALE).astype(BF16).reshape(nsub, WINDOW, LANES)
        q8 = _both_heads(qs, lo)
        s8 = jnp.einsum("bqd,bkd->bqk", q8, jnp.concatenate([kk, kk], axis=0), preferred_element_type=F32)
        sink = _swa_sinks(sink_ref, p, nsub)
        s8 = s8 + _swa_bias(p, i, nsub, False)
        m = jnp.maximum(jnp.max(s8, axis=2, keepdims=True), sink)
        e = jnp.exp(s8 - m)
        den = jnp.sum(e, axis=2, keepdims=True) + jnp.exp(sink - m)
        pr = (e * (1.0 / den)).astype(BF16)
        o8 = jnp.einsum("bqk,bkd->bqd", pr, jnp.concatenate([vv, vv], axis=0), preferred_element_type=F32)
        lse8 = m + jnp.log(den)
        o_ref[...] = jnp.where(lo, o8[0:nsub], o8[nsub:]).astype(BF16).reshape(tq, LANES)
        lse_ref[...] = jnp.where(lo, lse8[0:nsub], lse8[nsub:]).reshape(tq, LANES)

    return pl.pallas_call(
        body, name="swa_fwd", grid=(nb, 4, nq),
        in_specs=[pl.BlockSpec((tq, LANES), lambda b, p, i: (b * nq + i, p)),
                  pl.BlockSpec((s, LANES), lambda b, p, i: (b, lax.shift_right_logical(p, 1))),
                  pl.BlockSpec((s, LANES), lambda b, p, i: (b, lax.shift_right_logical(p, 1))),
                  pl.BlockSpec((1, LANES), lambda b, p, i: (0, 0))],
        out_specs=[pl.BlockSpec((tq, LANES), lambda b, p, i: (b * nq + i, p)),
                   pl.BlockSpec((None, tq, LANES), lambda b, p, i: (p, b * nq + i, 0))],
        out_shape=[jax.ShapeDtypeStruct((t, 512), BF16), jax.ShapeDtypeStruct((4, t, LANES), F32)],
        compiler_params=_params(("arbitrary", "arbitrary", "arbitrary")),
    )(qa, kae, vae, sink_row)


def _swa_bwd(qa, kae, vae, do_a, sink_row, lse, delta, nb, s, tq):
    t = qa.shape[0]
    nq = s // tq
    nsub = tq // WINDOW

    def body(q_ref, do_ref, k_ref, v_ref, sink_ref, lse_ref, dl_ref, dq_ref, dk_ref, dv_ref, ds_ref):
        p, i = pl.program_id(1), pl.program_id(2)

        @pl.when(i == 0)
        def _():
            ds_ref[...] = jnp.zeros_like(ds_ref)

        lo = _lane((1, 1, LANES)) < 64
        kk, vv = _swa_windows(k_ref, i, tq), _swa_windows(v_ref, i, tq)
        kks = (kk.astype(F32) * SCALE).astype(BF16)
        k8, v8 = jnp.concatenate([kks, kks], axis=0), jnp.concatenate([vv, vv], axis=0)
        q8 = _both_heads(q_ref[...].reshape(nsub, WINDOW, LANES), lo)
        do8 = _both_heads(do_ref[...].reshape(nsub, WINDOW, LANES), lo)
        cur = pl.multiple_of(i * tq, tq)
        sub = lax.broadcasted_iota(jnp.int32, (WINDOW, WINDOW), 0)
        lse_t = [lse_ref[u * WINDOW:(u + 1) * WINDOW, :].T for u in range(nsub)]
        dl_t = [dl_ref[u * WINDOW:(u + 1) * WINDOW, :].T for u in range(nsub)]
        lse8 = jnp.concatenate([t_[64 * hh:64 * hh + 1, :].reshape(1, 1, WINDOW) for hh in range(2) for t_ in lse_t], axis=0)
        dl8 = jnp.concatenate([jnp.sum(jnp.where(sub == 2 * p + hh, t_, 0.0), axis=0, keepdims=True).reshape(1, 1, WINDOW)
                               for hh in range(2) for t_ in dl_t], axis=0)
        sink = _swa_sinks(sink_ref, p, nsub)
        st = jnp.einsum("bkd,bqd->bkq", k8, q8, preferred_element_type=F32) + _swa_bias(p, i, nsub, True) - lse8
        pt = jnp.exp(st)
        dpt = jnp.einsum("bkd,bqd->bkq", v8, do8, preferred_element_type=F32)
        dst = pt * (dpt - dl8)
        ptb, dstb = pt.astype(BF16), dst.astype(BF16)
        dv8 = jnp.einsum("bkq,bqd->bkd", ptb, do8, preferred_element_type=F32)
        dk8 = jnp.einsum("bkq,bqd->bkd", dstb, q8, preferred_element_type=F32) * SCALE
        dq8 = jnp.einsum("bkq,bkd->bqd", dstb, k8, preferred_element_type=F32)
        dq_ref[...] = jnp.where(lo, dq8[0:nsub], dq8[nsub:]).reshape(tq, LANES)

        psd = jnp.exp(sink - lse8) * dl8
        row_h = lax.broadcasted_iota(jnp.int32, (8, LANES), 0)
        for hh in range(2):
            tot = jnp.sum(jnp.sum(psd[hh * nsub:(hh + 1) * nsub], axis=2, keepdims=True), axis=0, keepdims=True)
            ds_ref[...] += jnp.where(row_h == hh, -tot.reshape(1, 1), 0.0)

        prev = pl.multiple_of(jnp.maximum(i * tq - WINDOW, 0), WINDOW)
        for g8, g_ref in ((dk8, dk_ref), (dv8, dv_ref)):
            g4 = g8[0:nsub] + g8[nsub:]
            own, before = g4[:, WINDOW:, :], g4[:, 0:WINDOW, :]
            shifted = jnp.concatenate([before[1:nsub], jnp.zeros((1, WINDOW, LANES), F32)], axis=0)
            g_ref[pl.ds(cur, tq), :] = (own + shifted).reshape(tq, LANES)
            g_ref[pl.ds(prev, WINDOW), :] += before[0]

    return pl.pallas_call(
        body, name="swa_bwd", grid=(nb, 4, nq),
        in_specs=[pl.BlockSpec((tq, LANES), lambda b, p, i: (b * nq + i, p)),
                  pl.BlockSpec((tq, LANES), lambda b, p, i: (b * nq + i, p)),
                  pl.BlockSpec((s, LANES), lambda b, p, i: (b, lax.shift_right_logical(p, 1))),
                  pl.BlockSpec((s, LANES), lambda b, p, i: (b, lax.shift_right_logical(p, 1))),
                  pl.BlockSpec((1, LANES), lambda b, p, i: (0, 0)),
                  pl.BlockSpec((None, tq, LANES), lambda b, p, i: (p, b * nq + i, 0)),
                  pl.BlockSpec((tq, LANES), lambda b, p, i: (b * nq + i, 0))],
        out_specs=[pl.BlockSpec((tq, LANES), lambda b, p, i: (b * nq + i, p)),
                   pl.BlockSpec((s, LANES), lambda b, p, i: (b, p)),
                   pl.BlockSpec((s, LANES), lambda b, p, i: (b, p)),
                   pl.BlockSpec((None, None, 8, LANES), lambda b, p, i: (b, p, 0, 0))],
        out_shape=[jax.ShapeDtypeStruct((t, 512), F32), jax.ShapeDtypeStruct((t, 512), F32),
                   jax.ShapeDtypeStruct((t, 512), F32), jax.ShapeDtypeStruct((nb, 4, 8, LANES), F32)],
        compiler_params=_params(("arbitrary", "arbitrary", "arbitrary")),
    )(qa, do_a, kae, vae, sink_row, lse, delta)


MESH = pl.DeviceIdType.MESH
ANY = pl.BlockSpec(memory_space=pl.ANY)
N_SEM = 7


def _gather_steps(pairs, send_sems, recv_sems, local_sems):
    x, y, c = lax.axis_index("x"), lax.axis_index("y"), lax.axis_index("c")
    me, sibling = (x, y, c), (x, y, 1 - c)
    chips = [(1 - x, y), (x, 1 - y), (1 - x, 1 - y)]
    mine, first, passed, landed, last = [], [], [], [], []
    for a, (x_ref, out_ref) in enumerate(pairs):
        def slot(px, py, pc, out_ref=out_ref):
            return out_ref.at[4 * px + 2 * py + pc]

        def copy(k, block, to, src=None, a=a, slot=slot):
            return pltpu.make_async_remote_copy(
                src_ref=slot(*block) if src is None else src, dst_ref=slot(*block),
                send_sem=send_sems.at[N_SEM * a + k], recv_sem=recv_sems.at[N_SEM * a + k], device_id=to, device_id_type=MESH)

        mine.append(pltpu.make_async_copy(x_ref, slot(*me), local_sems.at[a]))
        first += [copy(0, me, sibling, src=x_ref)] + [copy(1 + j, me, (*chip, c), src=x_ref) for j, chip in enumerate(chips)]
        passed += [copy(4 + j, (*chip, c), sibling) for j, chip in enumerate(chips)]
        landed += [copy(1 + j, (*chip, c), me) for j, chip in enumerate(chips)]
        last += [copy(0, sibling, me)] + [copy(4 + j, (*chip, 1 - c), me) for j, chip in enumerate(chips)]

    def start():
        for cp in mine + first:
            cp.start()

    def forward():
        for arrived, onward in zip(landed, passed):
            arrived.wait_recv()
            onward.start()

    def finish():
        for cp in last:
            cp.wait_recv()
        for cp in first + passed:
            cp.wait_send()
        for cp in mine:
            cp.wait()

    return start, forward, finish


def _exchange_steps(pairs, send_sems, recv_sems, local_sems):
    x, y, c = lax.axis_index("x"), lax.axis_index("y"), lax.axis_index("c")
    my_id = 4 * x + 2 * y + c
    local, remote = [], []
    for a, (src, dst) in enumerate(pairs):
        local.append(pltpu.make_async_copy(src.at[my_id], dst.at[my_id], local_sems.at[a]))
        for k in range(1, N_DEV):
            px = 1 - x if k & 4 else x
            py = 1 - y if k & 2 else y
            pc = 1 - c if k & 1 else c
            remote.append(pltpu.make_async_remote_copy(
                src_ref=src.at[4 * px + 2 * py + pc], dst_ref=dst.at[my_id],
                send_sem=send_sems.at[N_SEM * a + k - 1], recv_sem=recv_sems.at[N_SEM * a + k - 1],
                device_id=(px, py, pc), device_id_type=MESH))

    def start():
        for cp in local + remote:
            cp.start()

    def finish():
        for cp in remote:
            cp.wait_recv()
        for cp in remote:
            cp.wait_send()
        for cp in local:
            cp.wait()

    return start, finish


L_ONE = 64
L_CK = 65
L_CQ = 68
L_LSE = 71
L_DELTA = 74


def _head_block(pair, half):
    y = pair if half == 0 else pltpu.roll(pair, 64, axis=1)
    return jnp.where(_lane(pair.shape) < 64, y, 0.0)


def _put3(blk, lane0, col):
    lane = _lane(blk.shape)
    hi = col.astype(BF16).astype(F32)
    mid = (col - hi).astype(BF16).astype(F32)
    lo = (col - hi - mid).astype(BF16).astype(F32)
    return jnp.where(lane == lane0, hi, jnp.where(lane == lane0 + 1, mid, jnp.where(lane == lane0 + 2, lo, blk)))


def _spread3(col, shape, lane0s):
    lane = _lane(shape)
    hi = col.astype(BF16).astype(F32)
    mid = (col - hi).astype(BF16).astype(F32)
    lo = (col - hi - mid).astype(BF16).astype(F32)

    def at(k):
        return functools.reduce(jnp.logical_or, [lane == ln + k for ln in lane0s])

    return jnp.where(at(0), hi, jnp.where(at(1), mid, jnp.where(at(2), lo, 0.0)))


def _put_ones(blk, lanes):
    lane = _lane(blk.shape)
    hit = functools.reduce(jnp.logical_or, [lane == ln for ln in lanes])
    return jnp.where(hit, 1.0, blk)


def _to_pairs(ref):
    out = []
    for j in range(4):
        a, b = ref[:, 2 * LANES * j:2 * LANES * j + LANES], ref[:, 2 * LANES * j + LANES:2 * LANES * (j + 1)]
        out.append(jnp.where(_lane(a.shape) < 64, a, pltpu.roll(b, 64, axis=1)))
    return jnp.concatenate(out, axis=1)


def _fox_fwd(q_aug, k_aug, v_aug, nb, s, bt, shards=()):
    t = q_aug.shape[0]
    nq = s // bt
    n_in, n_sh = 3, len(shards)

    def body(*refs):
        q_ref, k_ref, v_ref = refs[:n_in]
        o_ref, ql_ref = refs[n_in + n_sh:n_in + n_sh + 2]
        if shards:
            srcs, dsts = refs[n_in:n_in + n_sh], refs[n_in + n_sh + 2:n_in + 2 * n_sh + 2]
            start, forward, finish = _gather_steps(list(zip(srcs, dsts)), *refs[n_in + 2 * n_sh + 2:])
            step = (pl.program_id(0) * 4 + pl.program_id(1)) * nq + pl.program_id(2)
            pl.when(step == 0)(start)
            pl.when(step == nb * 3 * nq)(forward)
        i = pl.program_id(2)
        sls = [slice(LANES * hh, LANES * (hh + 1)) for hh in range(2)]
        qhs = [q_ref[:, sl] for sl in sls]

        def update(m, acc, qrows, start, size, sl, causal):
            sc = _nt(qrows, k_ref[pl.ds(start, size), sl])
            if causal:
                row = lax.broadcasted_iota(jnp.int32, sc.shape, 0)
                col = lax.broadcasted_iota(jnp.int32, sc.shape, 1)
                sc = jnp.where(row >= col, sc, NEG_INF)
            m_new = jnp.maximum(m, jnp.max(sc, axis=1, keepdims=True))
            pr = jnp.exp2(sc - m_new).astype(BF16)
            acc = jnp.exp2(m - m_new) * acc + jnp.dot(pr, v_ref[pl.ds(start, size), sl], preferred_element_type=F32)
            return m_new, acc

        def blk(kb_i, carry):
            start = pl.multiple_of(kb_i * bt, bt)
            return tuple(update(m, acc, qh, start, bt, sl, False) for (m, acc), qh, sl in zip(carry, qhs, sls))

        def diag_blk(carry):
            start = pl.multiple_of(i * bt, bt)
            return tuple(update(m, acc, qh, start, bt, sl, True) for (m, acc), qh, sl in zip(carry, qhs, sls))

        init = tuple((jnp.full((bt, 1), NEG_INF, F32), jnp.zeros((bt, LANES), F32)) for _ in range(2))
        carry = lax.fori_loop(0, i, blk, init)
        outs = []
        for (m, acc), qh, sl in zip(diag_blk(carry), qhs, sls):
            l = acc[:, L_ONE:L_ONE + 1]
            outs.append(acc * (1.0 / l))
            ql_ref[:, sl] = _put3(qh.astype(F32), L_LSE, -(m + jnp.log(l) * LOG2E)).astype(BF16)
        o_ref[...] = jnp.where(_lane((1, LANES)) < 64, outs[0], pltpu.roll(outs[1], 64, axis=1)).astype(BF16)
        if shards:
            pl.when(step == nb * 4 * nq - 1)(finish)

    in_specs = [pl.BlockSpec((bt, 2 * LANES), lambda b, j, i: (b * nq + i, j)),
                pl.BlockSpec((s, 2 * LANES), lambda b, j, i: (b, j)),
                pl.BlockSpec((s, 2 * LANES), lambda b, j, i: (b, j))]
    out_specs = [pl.BlockSpec((bt, LANES), lambda b, j, i: (b * nq + i, j)),
                 pl.BlockSpec((bt, 2 * LANES), lambda b, j, i: (b * nq + i, j))]
    out_shape = [jax.ShapeDtypeStruct((t, 512), BF16), jax.ShapeDtypeStruct((t, 8 * LANES), BF16)]
    args, scratch = [q_aug, k_aug, v_aug, *shards], []
    if shards:
        in_specs += [ANY] * n_sh
        out_specs += [ANY] * n_sh
        out_shape += [jax.ShapeDtypeStruct((N_DEV,) + sh.shape, sh.dtype) for sh in shards]
        scratch = [pltpu.SemaphoreType.DMA((N_SEM * n_sh,)), pltpu.SemaphoreType.DMA((N_SEM * n_sh,)),
                   pltpu.SemaphoreType.DMA((n_sh,))]
    return pl.pallas_call(
        body, name="fox_fwd", grid=(nb, 4, nq), in_specs=in_specs, out_specs=out_specs, out_shape=out_shape,
        scratch_shapes=scratch, compiler_params=_params(("arbitrary", "arbitrary", "arbitrary")),
    )(*args)


def _fox_bwd(ql_aug, k_aug, v_aug, do_aug, nb, s, bt, exch=()):
    t = ql_aug.shape[0]
    nk = s // bt
    n_in, n_out, n_ex = 4, 3, len(exch)

    def body(*refs):
        q_ref, do_ref, k_ref, v_ref = refs[:n_in]
        dq_ref, dk_ref, dv_ref = refs[n_in + n_ex:n_in + n_ex + n_out]
        if exch:
            srcs = refs[n_in:n_in + n_ex]
            dsts = refs[n_in + n_ex + n_out:n_in + 2 * n_ex + n_out]
            start, finish = _exchange_steps(list(zip(srcs, dsts)), *refs[n_in + 2 * n_ex + n_out:])
            step = (pl.program_id(0) * 4 + pl.program_id(1)) * nk + pl.program_id(2)
            pl.when(step == 0)(start)
        kb_i = pl.program_id(2)

        @pl.when(kb_i == 0)
        def _():
            dq_ref[...] = jnp.zeros_like(dq_ref)

        row = lax.broadcasted_iota(jnp.int32, (bt, bt), 0)
        col = lax.broadcasted_iota(jnp.int32, (bt, bt), 1)
        sls = [slice(LANES * hh, LANES * (hh + 1)) for hh in range(2)]
        khs, vhs = [k_ref[:, sl] for sl in sls], [v_ref[:, sl] for sl in sls]

        def blk(qi, carry, diag):
            start = pl.multiple_of(qi * bt, bt)
            new = []
            for (dk_a, dv_a), kh, vh, sl in zip(carry, khs, vhs, sls):
                qblk, doblk = q_ref[pl.ds(start, bt), sl], do_ref[pl.ds(start, bt), sl]
                st = _nt(kh, qblk)
                if diag:
                    pt = jnp.where(col >= row, jnp.exp2(jnp.where(col >= row, st, 0.0)), 0.0)
                else:
                    pt = jnp.exp2(st)
                dst = pt * _nt(vh, doblk)
                ptb, dstb = pt.astype(BF16), dst.astype(BF16)
                dv_a = dv_a + jnp.dot(ptb, doblk, preferred_element_type=F32)
                dk_a = dk_a + jnp.dot(dstb, qblk, preferred_element_type=F32)
                dq_ref[pl.ds(start, bt), sl] += _tn(dstb, kh)
                new.append((dk_a, dv_a))
            return tuple(new)

        zero = jnp.zeros((bt, LANES), F32)
        carry = blk(kb_i, ((zero, zero), (zero, zero)), True)
        carry = lax.fori_loop(kb_i + 1, nk, lambda qi, c: blk(qi, c, False), carry)
        for (dk_acc, dv_acc), sl in zip(carry, sls):
            dk_ref[:, sl] = dk_acc
            dv_ref[:, sl] = dv_acc
        if exch:
            pl.when(step == nb * 4 * nk - 1)(finish)

    scratch = []
    if exch:
        scratch = [pltpu.SemaphoreType.DMA((N_SEM * n_ex,)), pltpu.SemaphoreType.DMA((N_SEM * n_ex,)),
                   pltpu.SemaphoreType.DMA((n_ex,))]
    whole = pl.BlockSpec((s, 2 * LANES), lambda b, j, kb_i: (b, j))
    tile = pl.BlockSpec((bt, 2 * LANES), lambda b, j, kb_i: (b * nk + kb_i, j))
    shp = jax.ShapeDtypeStruct((t, 8 * LANES), F32)
    return pl.pallas_call(
        body, name="fox_bwd", grid=(nb, 4, nk),
        in_specs=[whole, whole, tile, tile] + [ANY] * n_ex,
        out_specs=[whole, tile, tile] + [ANY] * n_ex,
        out_shape=[shp, shp, shp] + [jax.ShapeDtypeStruct(e.shape, e.dtype) for e in exch],
        scratch_shapes=scratch, compiler_params=_params(("arbitrary", "arbitrary", "arbitrary")),
    )(ql_aug, do_aug, k_aug, v_aug, *exch)


FF_BLK = D_FF // N_DEV


def _mlp_fwd(x2, ma, mb, tgt, w_out, g2, w_up, w_down, tm):
    t = x2.shape[0]

    def body(x_ref, ma_ref, mb_ref, tg_ref, wo_ref, g2_ref, wu_ref, wd_ref,
             h_ref, hn_ref, hid_ref, dy_ref, dyb_ref, loss_ref):
        @pl.when(pl.program_id(0) == 0)
        def _():
            loss_ref[...] = jnp.zeros_like(loss_ref)

        h = (x_ref[...] + jnp.dot(ma_ref[...], wo_ref[0:512, :], preferred_element_type=F32)
             + jnp.dot(mb_ref[...], wo_ref[512:1024, :], preferred_element_type=F32))
        h_ref[...] = h
        r = lax.rsqrt(jnp.mean(h * h, axis=-1, keepdims=True) + EPS)
        hn = (h * r * g2_ref[...]).astype(BF16)
        hn_ref[...] = hn
        for d in range(N_DEV):
            u = jnp.maximum(jnp.dot(hn, wu_ref[d], preferred_element_type=F32), 0.0)
            hid_ref[:, FF_BLK * d:FF_BLK * (d + 1)] = (u * u).astype(BF16)
        y = h + jnp.dot(hid_ref[...], wd_ref[...], preferred_element_type=F32)
        err = y - tg_ref[...]
        dy = err * (1.0 / D_MODEL)
        dy_ref[...] = dy
        dyb_ref[...] = dy.astype(BF16)
        part =0.5 * jnp.sum(jnp.sum(err * err, axis=1, keepdims=True) * (1.0 / D_MODEL), axis=0, keepdims=True)
        loss_ref[...] += part

    def tile(w):
        return pl.BlockSpec((tm, w), lambda i: (i, 0))

    return pl.pallas_call(
        body, name="mlp_fwd", grid=(t // tm,),
        in_specs=[tile(D_MODEL), tile(512), tile(512), tile(D_MODEL), _const_spec((D_MODEL, D_MODEL)),
                  _const_spec((1, D_MODEL)), _const_spec((N_DEV, D_MODEL, FF_BLK)), _const_spec((D_FF, D_MODEL))],
        out_specs=[tile(D_MODEL), tile(D_MODEL), tile(D_FF), tile(D_MODEL), tile(D_MODEL),
                   pl.BlockSpec((8, LANES), lambda i: (0, 0))],
        out_shape=[jax.ShapeDtypeStruct((t, D_MODEL), F32), jax.ShapeDtypeStruct((t, D_MODEL), BF16),
                   jax.ShapeDtypeStruct((t, D_FF), BF16), jax.ShapeDtypeStruct((t, D_MODEL), F32),
                   jax.ShapeDtypeStruct((t, D_MODEL), BF16), jax.ShapeDtypeStruct((8, LANES), F32)],
        compiler_params=_params(("arbitrary",)),
    )(x2, ma, mb, tgt, w_out, g2, w_up, w_down)


def _mlp_bwd(dy, hid, h, ma, mb, w_down, w_up_t, w_out, g2, tm):
    t = dy.shape[0]

    def body(dy_ref, hid_ref, h_ref, ma_ref, mb_ref, wd_ref, wut_ref, wo_ref, g2_ref,
             du_ref, dh_ref, dhb_ref, dma_ref, dob_ref, dla_ref, gg_ref):
        @pl.when(pl.program_id(0) == 0)
        def _():
            gg_ref[...] = jnp.zeros_like(gg_ref)

        dy = dy_ref[...]
        d_hid = _nt(dy.astype(BF16), wd_ref[...])
        du = (d_hid * (2.0 * jnp.sqrt(hid_ref[...].astype(F32)))).astype(BF16)
        du_ref[...] = du
        d_hn = jnp.dot(du, wut_ref[...], preferred_element_type=F32)
        h = h_ref[...]
        r = lax.rsqrt(jnp.mean(h * h, axis=-1, keepdims=True) + EPS)
        hat = h * r
        gd = d_hn * g2_ref[...]
        dh = dy + r * (gd - hat * jnp.mean(gd * hat, axis=-1, keepdims=True))
        gg_ref[...] += jnp.sum(d_hn * hat, axis=0, keepdims=True)
        dh_ref[...] = dh
        dhb = dh.astype(BF16)
        dhb_ref[...] = dhb
        dm = _nt(dhb, wo_ref[...]).astype(BF16)
        dma, dmb = dm[:, 0:512], dm[:, 512:1024]
        dma_ref[...] = dma
        sel = (lax.shift_right_logical(lax.broadcasted_iota(jnp.int32, (512, LANES), 0), 6)
               == lax.broadcasted_iota(jnp.int32, (512, LANES), 1)).astype(BF16)
        dla_ref[...] = jnp.dot((dma.astype(F32) * ma_ref[...].astype(F32)).astype(BF16), sel, preferred_element_type=F32)
        dmb32 = dmb.astype(F32)
        dlb = jnp.dot((dmb32 * mb_ref[...].astype(F32)).astype(BF16), sel, preferred_element_type=F32)
        for hd in range(8):
            blk = _head_block(dmb32[:, LANES * (hd // 2):LANES * (hd // 2 + 1)], hd % 2)
            dob_ref[:, LANES * hd:LANES * (hd + 1)] = _put3(blk, L_DELTA, -dlb[:, hd:hd + 1]).astype(BF16)

    def tile(w):
        return pl.BlockSpec((tm, w), lambda i: (i, 0))

    return pl.pallas_call(
        body, name="mlp_bwd", grid=(t // tm,),
        in_specs=[tile(D_MODEL), tile(D_FF), tile(D_MODEL), tile(512), tile(512), _const_spec((D_FF, D_MODEL)),
                  _const_spec((D_FF, D_MODEL)), _const_spec((D_MODEL, D_MODEL)), _const_spec((1, D_MODEL))],
        out_specs=[tile(D_FF), tile(D_MODEL), tile(D_MODEL), tile(512), tile(8 * LANES), tile(LANES),
                   pl.BlockSpec((1, D_MODEL), lambda i: (0, 0))],
        out_shape=[jax.ShapeDtypeStruct((t, D_FF), BF16), jax.ShapeDtypeStruct((t, D_MODEL), F32),
                   jax.ShapeDtypeStruct((t, D_MODEL), BF16), jax.ShapeDtypeStruct((t, 512), BF16),
                   jax.ShapeDtypeStruct((t, 8 * LANES), BF16), jax.ShapeDtypeStruct((t, LANES), F32),
                   jax.ShapeDtypeStruct((1, D_MODEL), F32)],
        compiler_params=_params(("arbitrary",), VMEM_LIMIT_WIDE),
    )(dy, hid, h, ma, mb, w_down, w_up_t, w_out, g2)


def _wgrad(a, b, name, bm, bn, tk, out_dtype=F32, col_blocks=False, a2=None):
    t, m = a.shape
    n = b.shape[1]
    bm, bn = min(bm, m), min(bn, n)
    nk = t // tk

    def body(*refs):
        if a2 is None:
            a_ref, b_ref, o_ref, acc = refs
        else:
            a_ref, b_ref, a2_ref, o_ref, o2_ref, acc, acc2 = refs
        i, k = pl.program_id(0), pl.program_id(2)

        @pl.when(k == 0)
        def _():
            acc[...] = jnp.zeros_like(acc)

        acc[...] += _tn(a_ref[...], b_ref[...])

        @pl.when(k == nk - 1)
        def _():
            o_ref[...] = acc[...].astype(out_dtype)

        if a2 is not None:
            @pl.when((i == 0) & (k == 0))
            def _():
                acc2[...] = jnp.zeros_like(acc2)

            @pl.when(i == 0)
            def _():
                acc2[...] += _tn(a2_ref[...], b_ref[...])

            @pl.when((i == 0) & (k == nk - 1))
            def _():
                o2_ref[...] = acc2[...]

    if col_blocks:
        out_spec = pl.BlockSpec((None, bm, bn), lambda i, j, k: (j, i, 0))
        out_shape = jax.ShapeDtypeStruct((n // bn, m, bn), out_dtype)
    else:
        out_spec = pl.BlockSpec((bm, bn), lambda i, j, k: (i, j))
        out_shape = jax.ShapeDtypeStruct((m, n), out_dtype)
    in_specs = [pl.BlockSpec((tk, bm), lambda i, j, k: (k, i)), pl.BlockSpec((tk, bn), lambda i, j, k: (k, j))]
    out_specs, out_shapes, scratch, args = [out_spec], [out_shape], [pltpu.VMEM((bm, bn), F32)], [a, b]
    if a2 is not None:
        m2 = a2.shape[1]
        in_specs.append(pl.BlockSpec((tk, m2), lambda i, j, k: (k, 0)))
        out_specs.append(pl.BlockSpec((m2, n), lambda i, j, k: (0, 0)))
        out_shapes.append(jax.ShapeDtypeStruct((m2, n), F32))
        scratch.append(pltpu.VMEM((m2, n), F32))
        args.append(a2)
    out = pl.pallas_call(
        body, name=name, grid=(m // bm, n // bn, nk), in_specs=in_specs, out_specs=out_specs, out_shape=out_shapes,
        scratch_shapes=scratch, compiler_params=_params(("arbitrary", "arbitrary", "arbitrary")),
    )(*args)
    return out[0] if a2 is None else out


def _proj_bwd(raw, dqa, dkae, dvae, dqb, dkb, dvb, fl, bf_row, x2, dh, w_main_t, w_f_t, g1, gqa, gka, gqb, gkb, nb, s, tm):
    t = x2.shape[0]
    nt = s // tm

    def body(raw_ref, dqa_ref, dkae_ref, dvae_ref, dqb_ref, dkb_ref, dvb_ref, fl_ref, b_ref, x_ref, dh_ref,
             wmt_ref, wft_ref, g1_ref, gqa_ref, gka_ref, gqb_ref, gkb_ref,
             dx_ref, dp_ref, dfb_ref, ggqa_ref, ggka_ref, ggqb_ref, ggkb_ref, gg1_ref, gb_ref, carry, dlf_ref):
        @pl.when((pl.program_id(0) == 0) & (pl.program_id(1) == 0))
        def _():
            for r in (ggqa_ref, ggka_ref, ggqb_ref, ggkb_ref, gg1_ref, gb_ref):
                r[...] = jnp.zeros_like(r)

        @pl.when(pl.program_id(1) == 0)
        def _():
            carry[...] = jnp.zeros_like(carry)

        lane = _lane((tm, LANES))
        dc = jnp.zeros((tm, LANES), F32)
        for hd in range(8):
            col = (dqb_ref[:, LANES * hd + L_CQ:LANES * hd + L_CQ + 1] - dkb_ref[:, LANES * hd + L_CK:LANES * hd + L_CK + 1])
            dc = jnp.where(lane == hd, col, dc)
        dlf_ref[...] = _tri_dot(tm, True, dc) + carry[...]
        carry[...] = dlf_ref[pl.ds(0, 1), :]
        dfl = dlf_ref[...] * (1.0 / (1.0 + jnp.exp(fl_ref[...] + b_ref[...])))
        gb_ref[...] += jnp.sum(dfl, axis=0, keepdims=True)

        raw = raw_ref[...]
        d_qa, p_qa = _head_norm_bwd(raw[:, 0:512], gqa_ref[...], dqa_ref[...])
        d_ka, p_ka = _head_norm_bwd(raw[:, 512:640], gka_ref[...], _fold_kv(dkae_ref[...]))
        d_va = _fold_kv(dvae_ref[...])
        d_qb, p_qb = _head_norm_bwd(raw[:, 768:1280], gqb_ref[...], _to_pairs(dqb_ref) * SCALE)
        d_kb, p_kb = _head_norm_bwd(raw[:, 1280:1792], gkb_ref[...], _to_pairs(dkb_ref) * (1.0 / LOG2E))
        ggqa_ref[...] += jnp.sum(p_qa, axis=0, keepdims=True)
        ggka_ref[...] += jnp.sum(p_ka, axis=0, keepdims=True)
        ggqb_ref[...] += jnp.sum(p_qb, axis=0, keepdims=True)
        ggkb_ref[...] += jnp.sum(p_kb, axis=0, keepdims=True)
        dproj = jnp.concatenate([d_qa, d_ka, d_va, d_qb, d_kb, _to_pairs(dvb_ref)], axis=1).astype(BF16)
        dp_ref[...] = dproj
        dfb = dfl.astype(BF16)
        dfb_ref[...] = dfb
        d_xn = (jnp.dot(dproj, wmt_ref[...], preferred_element_type=F32)
                + jnp.dot(dfb, wft_ref[...], preferred_element_type=F32))
        x = x_ref[...]
        r = lax.rsqrt(jnp.mean(x * x, axis=-1, keepdims=True) + EPS)
        hat = x * r
        gd = d_xn * g1_ref[...]
        dx_ref[...] = dh_ref[...] + r * (gd - hat * jnp.mean(gd * hat, axis=-1, keepdims=True))
        gg1_ref[...] += jnp.sum(d_xn * hat, axis=0, keepdims=True)

    def tile(w):
        return pl.BlockSpec((tm, w), lambda b, i: (b * nt + (nt - 1 - i), 0))

    def acc(w):
        return pl.BlockSpec((1, w), lambda b, i: (0, 0))

    return pl.pallas_call(
        body, name="proj_bwd", grid=(nb, nt),
        in_specs=[tile(MAIN_W), tile(512), tile(512), tile(512), tile(8 * LANES), tile(8 * LANES), tile(8 * LANES), tile(LANES),
                  _const_spec((1, LANES)), tile(D_MODEL), tile(D_MODEL), _const_spec((MAIN_W, D_MODEL)),
                  _const_spec((LANES, D_MODEL)), _const_spec((1, D_MODEL)), _const_spec((1, 512)), _const_spec((1, 128)),
                  _const_spec((1, 512)), _const_spec((1, 512))],
        out_specs=[tile(D_MODEL), tile(MAIN_W), tile(LANES), acc(512), acc(128), acc(512), acc(512), acc(D_MODEL), acc(LANES)],
        out_shape=[jax.ShapeDtypeStruct((t, D_MODEL), F32), jax.ShapeDtypeStruct((t, MAIN_W), BF16),
                   jax.ShapeDtypeStruct((t, LANES), BF16), jax.ShapeDtypeStruct((1, 512), F32),
                   jax.ShapeDtypeStruct((1, 128), F32), jax.ShapeDtypeStruct((1, 512), F32),
                   jax.ShapeDtypeStruct((1, 512), F32), jax.ShapeDtypeStruct((1, D_MODEL), F32),
                   jax.ShapeDtypeStruct((1, LANES), F32)],
        scratch_shapes=[pltpu.VMEM((1, LANES), F32), pltpu.VMEM((tm, LANES), F32)],
        compiler_params=_params(("arbitrary", "arbitrary"), VMEM_LIMIT_WIDE),
    )(raw, dqa, dkae, dvae, dqb, dkb, dvb, fl, bf_row, x2, dh, w_main_t, w_f_t, g1, gqa, gka, gqb, gkb)


IN_PAD = 304


def _local_step(x, tgt, w_in_t, rest, g1, b_forget, qna, kna, sinks, qnb, knb, g2,
                tm=512, bt=1024, btf=1024, tq=4096, wk=4096, wkb=8192, distributed=False):
    nb, s, _ = x.shape
    t = nb * s
    x2, tgt2 = x.reshape(t, D_MODEL), tgt.reshape(t, D_MODEL)
    g1r, g2r = g1.reshape(1, D_MODEL), g2.reshape(1, D_MODEL)
    gqa, gka = jnp.tile(qna, 8).reshape(1, 512), jnp.tile(kna, 2).reshape(1, 128)
    gqb, gkb = jnp.tile(qnb, 8).reshape(1, 512), jnp.tile(knb, 8).reshape(1, 512)
    bf_row = jnp.pad(b_forget, (0, LANES - 8)).reshape(1, LANES)
    sink_row = jnp.pad(sinks, (0, LANES - 8)).reshape(1, LANES)
    w_main_t = w_in_t[0:MAIN_W]
    w_f_t = jnp.pad(w_in_t[MAIN_W:IN_W], ((0, LANES - 8), (0, 0)))

    xn, raw, fl, qa, kae, vae, q_aug, k_aug, v_aug = _norm_proj(x2, g1r, w_main_t, w_f_t, gqa, gka, gqb, gkb, bf_row, s, tm)
    ma, lse_a = _swa_fwd(qa, kae, vae, sink_row, nb, s, tq)
    if distributed:
        mb, ql_aug, w_out, w_up, w_down, w_up_t = _fox_fwd(q_aug, k_aug, v_aug, nb, s, btf, shards=rest)
    else:
        mb, ql_aug = _fox_fwd(q_aug, k_aug, v_aug, nb, s, btf)
        w_out, w_up, w_down, w_up_t = rest
    w_out, w_down = w_out.reshape(D_MODEL, D_MODEL), w_down.reshape(D_FF, D_MODEL)
    h, hn, hid, dy, dyb, loss_acc = _mlp_fwd(x2, ma, mb, tgt2, w_out, g2r, w_up, w_down, tm)

    du, dh, dhb, dma, do_aug, dla, gg2 = _mlp_bwd(dy, hid, h, ma, mb, w_down, w_up_t.reshape(D_FF, D_MODEL), w_out, g2r, tm)
    g_down = _wgrad(hid, dyb, "wgrad_down", 512, 1024, wkb, BF16).reshape(N_DEV, 512, D_MODEL)
    g_up = _wgrad(hn, du, "wgrad_up", 1024, 512, wkb, BF16, col_blocks=True)
    g_out = jnp.concatenate([_wgrad(ma, dhb, "wgrad_out_a", 512, 1024, wk, BF16),
                             _wgrad(mb, dhb, "wgrad_out_b", 512, 1024, wk, BF16)], axis=0).reshape(N_DEV, 128, D_MODEL)

    dqa, dkae, dvae, dsink = _swa_bwd(qa, kae, vae, dma, sink_row, lse_a, dla, nb, s, tq)
    fox = _fox_bwd(ql_aug, k_aug, v_aug, do_aug, nb, s, bt, exch=(g_out, g_up, g_down) if distributed else ())
    dqb, dkb, dvb = fox[:3]
    if distributed:
        g_out, g_up, g_down = fox[3:]
    grad_x, dproj, dfb, ggqa, ggka, ggqb, ggkb, gg1, gbf = _proj_bwd(
        raw, dqa, dkae, dvae, dqb, dkb, dvb, fl, bf_row, x2, dh, w_main_t, w_f_t, g1r, gqa, gka, gqb, gkb, nb, s, tm)
    g_main_t, g_gate_t = _wgrad(dproj, xn, "wgrad_in", 768, 1024, wk, a2=dfb)
    g_in_t = jnp.concatenate([g_main_t, g_gate_t[0:8]], axis=0)

    small = (gg1.reshape(D_MODEL), gbf[0, 0:8], ggqa.reshape(8, 64).sum(0), ggka.reshape(2, 64).sum(0),
             dsink.sum(0)[:, 0:2, 0].reshape(8), ggqb.reshape(8, 64).sum(0), ggkb.reshape(8, 64).sum(0),
             gg2.reshape(D_MODEL))
    return loss_acc[0, 0], grad_x.reshape(nb, s, D_MODEL), g_in_t, g_out, g_up, g_down, small


def _all_gather(shard):
    def body(x_ref, out_ref, send_sems, recv_sems, local_sem):
        start, forward, finish = _gather_steps([(x_ref, out_ref)], send_sems, recv_sems, local_sem)
        start()
        forward()
        finish()

    return pl.pallas_call(
        body, name="gather_w_in", out_shape=jax.ShapeDtypeStruct((N_DEV,) + shard.shape, shard.dtype),
        in_specs=[ANY], out_specs=ANY,
        scratch_shapes=[pltpu.SemaphoreType.DMA((N_SEM,)), pltpu.SemaphoreType.DMA((N_SEM,)), pltpu.SemaphoreType.DMA((1,))],
    )(shard)


def _sum_adamw(recv, w, m, v, tr, name, exch=()):
    _, r, n = recv.shape
    n_ex, n_steps = len(exch), r // tr

    def body(*refs):
        r_ref, w_ref, m_ref, v_ref = refs[:4]
        g_ref, d_ref, nm_ref, nv_ref = refs[4 + n_ex:8 + n_ex]
        if exch:
            srcs, dsts = refs[4:4 + n_ex], refs[8 + n_ex:8 + 2 * n_ex]
            start, finish = _exchange_steps(list(zip(srcs, dsts)), *refs[8 + 2 * n_ex:])
            pl.when(pl.program_id(0) == 0)(start)
        g = r_ref[0].astype(F32)
        for s in range(1, N_DEV):
            g = g + r_ref[s].astype(F32)
        g_ref[...] = g
        nm = ADAM_B1 * m_ref[...] + (1.0 - ADAM_B1) * g
        nv = ADAM_B2 * v_ref[...] + (1.0 - ADAM_B2) * (g * g)
        m_hat = nm / (1.0 - ADAM_B1 ** ADAM_STEP)
        v_hat = nv / (1.0 - ADAM_B2 ** ADAM_STEP)
        d_ref[...] = -ADAM_LR * (m_hat / (jnp.sqrt(v_hat) + ADAM_EPS) + ADAM_WD * w_ref[...])
        nm_ref[...] = nm
        nv_ref[...] = nv
        if exch:
            pl.when(pl.program_id(0) == n_steps - 1)(finish)

    tile = pl.BlockSpec((tr, n), lambda i: (i, 0))
    shp = jax.ShapeDtypeStruct((r, n), F32)
    scratch = []
    if exch:
        scratch = [pltpu.SemaphoreType.DMA((N_SEM * n_ex,)), pltpu.SemaphoreType.DMA((N_SEM * n_ex,)),
                   pltpu.SemaphoreType.DMA((n_ex,))]
    return pl.pallas_call(
        body, name=name, grid=(n_steps,),
        in_specs=[pl.BlockSpec((N_DEV, tr, n), lambda i: (0, i, 0)), tile, tile, tile] + [ANY] * n_ex,
        out_specs=[tile, tile, tile, tile] + [ANY] * n_ex,
        out_shape=[shp, shp, shp, shp] + [jax.ShapeDtypeStruct(e.shape, e.dtype) for e in exch],
        scratch_shapes=scratch, compiler_params=_params(("arbitrary",)),
    )(recv, w, m, v, *exch)


def _small_rows(g1, bf, qna, kna, sk, qnb, knb, g2, extra=None):
    row2 = jnp.concatenate([bf, qna, kna, sk, qnb, knb])
    rows = [g1, g2, jnp.pad(row2, (0, D_MODEL - row2.shape[0]))]
    if extra is not None:
        rows.append(jnp.pad(extra.reshape(1), (0, D_MODEL - 1)))
    return jnp.pad(jnp.stack(rows), ((0, 8 - len(rows)), (0, 0)))


def _in_rows(w_in_s):
    return jnp.pad(w_in_s.T, ((0, IN_PAD - IN_SHARD), (0, 0)))


def kernel(x, attn_norm_g, w_in, b_forget, q_norm_a, k_norm_a, sink_logits, q_norm_b, k_norm_b, w_out, mlp_norm_g, w_up, w_down, loss_target, m_attn_norm_g, m_w_in, m_b_forget, m_q_norm_a, m_k_norm_a, m_sink_logits, m_q_norm_b, m_k_norm_b, m_w_out, m_mlp_norm_g, m_w_up, m_w_down, v_attn_norm_g, v_w_in, v_b_forget, v_q_norm_a, v_k_norm_a, v_sink_logits, v_q_norm_b, v_k_norm_b, v_w_out, v_mlp_norm_g, v_w_up, v_w_down):
    w_in_r = _in_rows(w_in)
    w_in_t = _all_gather(w_in_r.astype(BF16))[:, 0:IN_SHARD].reshape(IN_W, D_MODEL)
    w_up_b = w_up.astype(BF16)
    rest = (w_out.astype(BF16), w_up_b, w_down.astype(BF16), w_up_b.T)

    loss_part, grad_x, g_in_t, r_out, r_up, r_down, small = _local_step(
        x, loss_target, w_in_t, rest, attn_norm_g, b_forget, q_norm_a, k_norm_a, sink_logits, q_norm_b, k_norm_b, mlp_norm_g,
        distributed=True)

    g_in_blocks = jnp.pad(g_in_t.reshape(N_DEV, IN_SHARD, D_MODEL), ((0, 0), (0, IN_PAD - IN_SHARD), (0, 0))).astype(BF16)
    small_blocks = jnp.broadcast_to(_small_rows(*small, extra=loss_part), (N_DEV, 8, D_MODEL))

    small_w = _small_rows(attn_norm_g, b_forget, q_norm_a, k_norm_a, sink_logits, q_norm_b, k_norm_b, mlp_norm_g)
    small_m = _small_rows(m_attn_norm_g, m_b_forget, m_q_norm_a, m_k_norm_a, m_sink_logits, m_q_norm_b, m_k_norm_b, m_mlp_norm_g)
    small_v = _small_rows(v_attn_norm_g, v_b_forget, v_q_norm_a, v_k_norm_a, v_sink_logits, v_q_norm_b, v_k_norm_b, v_mlp_norm_g)
    *o_down, r_in, r_small = _sum_adamw(r_down, w_down, m_w_down, v_w_down, 128, "adamw_down", exch=(g_in_blocks, small_blocks))
    o_out = _sum_adamw(r_out, w_out, m_w_out, v_w_out, 128, "adamw_out")
    o_up = _sum_adamw(r_up, w_up, m_w_up, v_w_up, 256, "adamw_up")
    o_in = [a[0:IN_SHARD].T for a in _sum_adamw(r_in, w_in_r, _in_rows(m_w_in), _in_rows(v_w_in), IN_PAD, "adamw_in")]
    o_small = _sum_adamw(r_small, small_w, small_m, small_v, 8, "adamw_small")

    def leaves(i):
        row2 = o_small[i][2]
        return (o_small[i][0], o_in[i], row2[0:8], row2[8:72], row2[72:136], row2[136:144], row2[144:208], row2[208:272],
                o_out[i], o_small[i][1], o_up[i], o_down[i])

    return (o_small[0][3, 0], grad_x, *leaves(0), *leaves(1), *leaves(2), *leaves(3))
```

```python
import functools

import jax
import jax.numpy as jnp
from jax import lax
from jax.experimental import pallas as pl
from jax.experimental.pallas import tpu as pltpu
from jax.experimental.pallas import tpu_sc as plsc

F32 = jnp.float32
BF16 = jnp.bfloat16

D_MODEL = 1024
HEAD_DIM = 64
N_DEV = 8
D_FF = 4096
MAIN_W = 2304
IN_W = 2312
IN_SHARD = 289
WINDOW = 128
EPS = 1e-6
SCALE = 0.125
LOG2E = 1.4426950408889634
LANES = 128
NEG_INF = float("-inf")

ADAM_LR = 0.001
ADAM_B1 = 0.9
ADAM_B2 = 0.999
ADAM_EPS = 1e-08
ADAM_WD = 0.01
ADAM_STEP = 10

VMEM_LIMIT = 56 * 1024 * 1024
VMEM_LIMIT_WIDE = 62 * 1024 * 1024


def _params(sem, vmem=VMEM_LIMIT):
    return pltpu.CompilerParams(dimension_semantics=sem, vmem_limit_bytes=vmem)


def _const_spec(shape):
    nd = len(shape)
    return pl.BlockSpec(shape, lambda *_: (0,) * nd, pipeline_mode=pl.Buffered(1))


def _lane(shape):
    return lax.broadcasted_iota(jnp.int32, shape, len(shape) - 1)


def _head_ones(n):
    r = lax.shift_right_logical(lax.broadcasted_iota(jnp.int32, (n, n), 0), 6)
    c = lax.shift_right_logical(lax.broadcasted_iota(jnp.int32, (n, n), 1), 6)
    return (r == c).astype(BF16)


def _head_sum(v):
    w = v.shape[1]
    vb = v.astype(BF16)
    if w <= 256:
        return jnp.dot(vb, _head_ones(w), preferred_element_type=F32)
    ones = _head_ones(256)
    return jnp.concatenate([jnp.dot(vb[:, s:s + 256], ones, preferred_element_type=F32) for s in range(0, w, 256)], axis=1)


def _head_norm(seg, gain):
    rs = lax.rsqrt(_head_sum(seg * seg) * (1.0 / HEAD_DIM) + EPS)
    return seg * rs * gain


def _head_norm_bwd(seg, gain, d_out):
    rs = lax.rsqrt(_head_sum(seg * seg) * (1.0 / HEAD_DIM) + EPS)
    hat = seg * rs
    gd = d_out * gain
    d_seg = rs * (gd - hat * (_head_sum(gd * hat) * (1.0 / HEAD_DIM)))
    return d_seg, d_out * hat


def _expand_kv(v):
    r = pltpu.roll(v, 64, axis=1)
    lo = _lane(v.shape) < 64
    return jnp.concatenate([jnp.where(lo, v, r), jnp.where(lo, r, v)], axis=1)


def _fold_kv(e4):
    t0 = e4[:, 0:128] + e4[:, 128:256]
    t1 = e4[:, 256:384] + e4[:, 384:512]
    t0 = t0 + pltpu.roll(t0, 64, axis=1)
    t1 = t1 + pltpu.roll(t1, 64, axis=1)
    return jnp.where(_lane(t0.shape) < 64, t0, t1)


def _pick_lane(blk, idx):
    return jnp.sum(jnp.where(_lane(blk.shape) == idx, blk, 0.0), axis=1, keepdims=True)


def _nt(a, b):
    return lax.dot_general(a, b, (((1,), (1,)), ((), ())), preferred_element_type=F32)


def _tn(a, b):
    return lax.dot_general(a, b, (((0,), (0,)), ((), ())), preferred_element_type=F32)


def _norm_proj(x2, g1, w_main_t, w_f_t, gqa, gka, gqb, gkb, bf_row, s, tm):
    t = x2.shape[0]
    nt = s // tm

    def body(x_ref, g1_ref, wm_ref, wf_ref, gqa_ref, gka_ref, gqb_ref, gkb_ref, b_ref,
             xn_ref, raw_ref, fl_ref, qa_ref, kae_ref, vae_ref, qo_ref, ko_ref, vo_ref, carry, c_ref):
        @pl.when(lax.rem(pl.program_id(0), nt) == 0)
        def _():
            carry[...] = jnp.zeros_like(carry)

        x = x_ref[...]
        r = lax.rsqrt(jnp.mean(x * x, axis=-1, keepdims=True) + EPS)
        xn = (x * r * g1_ref[...]).astype(BF16)
        xn_ref[...] = xn
        proj = _nt(xn, wm_ref[...])
        raw_ref[...] = proj
        fl = _nt(xn, wf_ref[...])
        fl_ref[...] = fl
        qa_ref[...] = _head_norm(proj[:, 0:512], gqa_ref[...]).astype(BF16)
        kae_ref[...] = _expand_kv(_head_norm(proj[:, 512:640], gka_ref[...])).astype(BF16)
        vae_ref[...] = _expand_kv(proj[:, 640:768]).astype(BF16)

        z = fl + b_ref[...]
        e = jnp.exp(-jnp.abs(z))
        u = 1.0 + e
        log1p = jnp.where(u == 1.0, e, jnp.log(u) * (e / (u - 1.0)))
        lf = jnp.minimum(z, 0.0) - log1p
        for r0 in range(0, tm, 256):
            c_ref[r0:r0 + 256, :] = _tri_dot(256, False, lf[r0:r0 + 256]) + carry[...]
            carry[...] = c_ref[pl.ds(r0 + 255, 1), :]
        c2 = c_ref[...] * LOG2E
        qb = _head_norm(proj[:, 768:1280], gqb_ref[...]) * (SCALE * LOG2E)
        kb = _head_norm(proj[:, 1280:1792], gkb_ref[...])
        lane = _lane((tm, LANES))
        for h in range(8):
            j, half = h // 2, h % 2
            pair, blk = slice(LANES * j, LANES * (j + 1)), slice(LANES * h, LANES * (h + 1))
            feat = _spread3(c2[:, h:h + 1], (tm, LANES), (L_CK, L_CQ))
            q = _put_ones(_head_block(qb[:, pair], half), (L_CK, L_CK + 1, L_CK + 2))
            qo_ref[:, blk] = jnp.where((lane >= L_CQ) & (lane < L_CQ + 3), feat, q).astype(BF16)
            k = _put_ones(_head_block(kb[:, pair], half), tuple(range(L_CQ, L_CQ + 6)))
            ko_ref[:, blk] = jnp.where((lane >= L_CK) & (lane < L_CK + 3), -feat, k).astype(BF16)
            v = _head_block(proj[:, 1792 + LANES * j:1792 + LANES * (j + 1)], half)
            vo_ref[:, blk] = _put_ones(v, (L_ONE, L_DELTA, L_DELTA + 1, L_DELTA + 2)).astype(BF16)

    def tile(w):
        return pl.BlockSpec((tm, w), lambda i: (i, 0))

    aug = jax.ShapeDtypeStruct((t, 8 * LANES), BF16)
    return pl.pallas_call(
        body, name="norm_proj", grid=(t // tm,),
        in_specs=[tile(D_MODEL), _const_spec((1, D_MODEL)), _const_spec((MAIN_W, D_MODEL)), _const_spec((LANES, D_MODEL)),
                  _const_spec((1, 512)), _const_spec((1, 128)), _const_spec((1, 512)), _const_spec((1, 512)),
                  _const_spec((1, LANES))],
        out_specs=[tile(D_MODEL), tile(MAIN_W), tile(LANES), tile(512), tile(256), tile(256)] + [tile(8 * LANES)] * 3,
        out_shape=[jax.ShapeDtypeStruct((t, D_MODEL), BF16), jax.ShapeDtypeStruct((t, MAIN_W), F32),
                   jax.ShapeDtypeStruct((t, LANES), F32), jax.ShapeDtypeStruct((t, 512), BF16),
                   jax.ShapeDtypeStruct((t, 256), BF16), jax.ShapeDtypeStruct((t, 256), BF16), aug, aug, aug],
        scratch_shapes=[pltpu.VMEM((1, LANES), F32), pltpu.VMEM((tm, LANES), F32)],
        compiler_params=_params(("arbitrary",)),
    )(x2, g1, w_main_t, w_f_t, gqa, gka, gqb, gkb, bf_row)


def _tri_dot(n, upper, v):
    r = lax.broadcasted_iota(jnp.int32, (n, n), 0)
    c = lax.broadcasted_iota(jnp.int32, (n, n), 1)
    tri = ((c >= r) if upper else (c <= r)).astype(BF16)
    hi = v.astype(BF16)
    mid = (v - hi.astype(F32)).astype(BF16)
    lo = (v - hi.astype(F32) - mid.astype(F32)).astype(BF16)
    return (jnp.dot(tri, hi, preferred_element_type=F32) + jnp.dot(tri, mid, preferred_element_type=F32)
            + jnp.dot(tri, lo, preferred_element_type=F32))


def _slope(p, hh):
    out = jnp.float32(2.0 ** -(2 * 3 + hh + 1))
    for pp in (2, 1, 0):
        out = jnp.where(p == pp, jnp.float32(2.0 ** -(2 * pp + hh + 1)), out)
    return out


def _swa_windows(ref, i, tq):
    nsub = tq // WINDOW
    cur = ref[pl.ds(pl.multiple_of(i * tq, tq), tq), :].reshape(nsub, WINDOW, LANES)
    first = ref[pl.ds(pl.multiple_of(jnp.maximum(i * tq - WINDOW, 0), WINDOW), WINDOW), :].reshape(1, WINDOW, LANES)
    return jnp.concatenate([jnp.concatenate([first, cur[0:nsub - 1]], axis=0), cur], axis=1)


def _both_heads(x3, lo):
    zero = jnp.zeros_like(x3)
    return jnp.concatenate([jnp.where(lo, x3, zero), jnp.where(lo, zero, x3)], axis=0)


def _swa_sinks(sink_ref, p, nsub):
    is_a = lax.broadcasted_iota(jnp.int32, (2 * nsub, 1, 1), 0) < nsub
    sinks = sink_ref[...]
    return jnp.where(is_a, _pick_lane(sinks, 2 * p).reshape(1, 1, 1), _pick_lane(sinks, 2 * p + 1).reshape(1, 1, 1))


def _swa_bias(p, i, nsub, keys_first):
    shape = (1, 2 * WINDOW, WINDOW) if keys_first else (1, WINDOW, 2 * WINDOW)
    qi = lax.broadcasted_iota(jnp.int32, shape, 2 if keys_first else 1)
    ki = lax.broadcasted_iota(jnp.int32, shape, 1 if keys_first else 2)
    dist = qi + WINDOW - ki
    band = (dist >= 0) & (dist < WINDOW)
    tiles = []
    for hh in range(2):
        bias = jnp.where(band, -_slope(p, hh) * dist.astype(F32), NEG_INF)
        tiles += [jnp.where((i == 0) & (ki < WINDOW), NEG_INF, bias)] + [bias] * (nsub - 1)
    return jnp.concatenate(tiles, axis=0)


def _swa_fwd(qa, kae, vae, sink_row, nb, s, tq):
    t = qa.shape[0]
    nq = s // tq
    nsub = tq // WINDOW

    def body(q_ref, k_ref, v_ref, sink_ref, o_ref, lse_ref):
        p, i = pl.program_id(1), pl.program_id(2)
        lo = _lane((1, 1, LANES)) < 64
        kk, vv = _swa_windows(k_ref, i, tq), _swa_windows(v_ref, i, tq)
        qs = (q_ref[...].astype(F32) * SCALE).astype(BF16).reshape(nsub, WINDOW, LANES)
        q8 = _both_heads(qs, lo)
        s8 = jnp.einsum("bqd,bkd->bqk", q8, jnp.concatenate([kk, kk], axis=0), preferred_element_type=F32)
        sink = _swa_sinks(sink_ref, p, nsub)
        s8 = s8 + _swa_bias(p, i, nsub, False)
        m = jnp.maximum(jnp.max(s8, axis=2, keepdims=True), sink)
        e = jnp.exp(s8 - m)
        den = jnp.sum(e, axis=2, keepdims=True) + jnp.exp(sink - m)
        pr = (e * (1.0 / den)).astype(BF16)
        o8 = jnp.einsum("bqk,bkd->bqd", pr, jnp.concatenate([vv, vv], axis=0), preferred_element_type=F32)
        lse8 = m + jnp.log(den)
        o_ref[...] = jnp.where(lo, o8[0:nsub], o8[nsub:]).astype(BF16).reshape(tq, LANES)
        lse_ref[...] = jnp.where(lo, lse8[0:nsub], lse8[nsub:]).reshape(tq, LANES)

    return pl.pallas_call(
        body, name="swa_fwd", grid=(nb, 4, nq),
        in_specs=[pl.BlockSpec((tq, LANES), lambda b, p, i: (b * nq + i, p)),
                  pl.BlockSpec((s, LANES), lambda b, p, i: (b, lax.shift_right_logical(p, 1))),
                  pl.BlockSpec((s, LANES), lambda b, p, i: (b, lax.shift_right_logical(p, 1))),
                  pl.BlockSpec((1, LANES), lambda b, p, i: (0, 0))],
        out_specs=[pl.BlockSpec((tq, LANES), lambda b, p, i: (b * nq + i, p)),
                   pl.BlockSpec((None, tq, LANES), lambda b, p, i: (p, b * nq + i, 0))],
        out_shape=[jax.ShapeDtypeStruct((t, 512), BF16), jax.ShapeDtypeStruct((4, t, LANES), F32)],
        compiler_params=_params(("arbitrary", "arbitrary", "arbitrary")),
    )(qa, kae, vae, sink_row)


def _swa_bwd(qa, kae, vae, do_a, sink_row, lse, delta, nb, s, tq):
    t = qa.shape[0]
    nq = s // tq
    nsub = tq // WINDOW

    def body(q_ref, do_ref, k_ref, v_ref, sink_ref, lse_ref, dl_ref, dq_ref, dk_ref, dv_ref, ds_ref):
        p, i = pl.program_id(1), pl.program_id(2)

        @pl.when(i == 0)
        def _():
            ds_ref[...] = jnp.zeros_like(ds_ref)

        lo = _lane((1, 1, LANES)) < 64
        kk, vv = _swa_windows(k_ref, i, tq), _swa_windows(v_ref, i, tq)
        kks = (kk.astype(F32) * SCALE).astype(BF16)
        k8, v8 = jnp.concatenate([kks, kks], axis=0), jnp.concatenate([vv, vv], axis=0)
        q8 = _both_heads(q_ref[...].reshape(nsub, WINDOW, LANES), lo)
        do8 = _both_heads(do_ref[...].reshape(nsub, WINDOW, LANES), lo)
        cur = pl.multiple_of(i * tq, tq)
        sub = lax.broadcasted_iota(jnp.int32, (WINDOW, WINDOW), 0)
        lse_t = [lse_ref[u * WINDOW:(u + 1) * WINDOW, :].T for u in range(nsub)]
        dl_t = [dl_ref[u * WINDOW:(u + 1) * WINDOW, :].T for u in range(nsub)]
        lse8 = jnp.concatenate([t_[64 * hh:64 * hh + 1, :].reshape(1, 1, WINDOW) for hh in range(2) for t_ in lse_t], axis=0)
        dl8 = jnp.concatenate([jnp.sum(jnp.where(sub == 2 * p + hh, t_, 0.0), axis=0, keepdims=True).reshape(1, 1, WINDOW)
                               for hh in range(2) for t_ in dl_t], axis=0)
        sink = _swa_sinks(sink_ref, p, nsub)
        st = jnp.einsum("bkd,bqd->bkq", k8, q8, preferred_element_type=F32) + _swa_bias(p, i, nsub, True) - lse8
        pt = jnp.exp(st)
        dpt = jnp.einsum("bkd,bqd->bkq", v8, do8, preferred_element_type=F32)
        dst = pt * (dpt - dl8)
        ptb, dstb = pt.astype(BF16), dst.astype(BF16)
        dv8 = jnp.einsum("bkq,bqd->bkd", ptb, do8, preferred_element_type=F32)
        dk8 = jnp.einsum("bkq,bqd->bkd", dstb, q8, preferred_element_type=F32) * SCALE
        dq8 = jnp.einsum("bkq,bkd->bqd", dstb, k8, preferred_element_type=F32)
        dq_ref[...] = jnp.where(lo, dq8[0:nsub], dq8[nsub:]).reshape(tq, LANES)

        psd = jnp.exp(sink - lse8) * dl8
        row_h = lax.broadcasted_iota(jnp.int32, (8, LANES), 0)
        for hh in range(2):
            tot = jnp.sum(jnp.sum(psd[hh * nsub:(hh + 1) * nsub], axis=2, keepdims=True), axis=0, keepdims=True)
            ds_ref[...] += jnp.where(row_h == hh, -tot.reshape(1, 1), 0.0)

        prev = pl.multiple_of(jnp.maximum(i * tq - WINDOW, 0), WINDOW)
        for g8, g_ref in ((dk8, dk_ref), (dv8, dv_ref)):
            g4 = g8[0:nsub] + g8[nsub:]
            own, before = g4[:, WINDOW:, :], g4[:, 0:WINDOW, :]
            shifted = jnp.concatenate([before[1:nsub], jnp.zeros((1, WINDOW, LANES), F32)], axis=0)
            g_ref[pl.ds(cur, tq), :] = (own + shifted).reshape(tq, LANES)
            g_ref[pl.ds(prev, WINDOW), :] += before[0]

    return pl.pallas_call(
        body, name="swa_bwd", grid=(nb, 4, nq),
        in_specs=[pl.BlockSpec((tq, LANES), lambda b, p, i: (b * nq + i, p)),
                  pl.BlockSpec((tq, LANES), lambda b, p, i: (b * nq + i, p)),
                  pl.BlockSpec((s, LANES), lambda b, p, i: (b, lax.shift_right_logical(p, 1))),
                  pl.BlockSpec((s, LANES), lambda b, p, i: (b, lax.shift_right_logical(p, 1))),
                  pl.BlockSpec((1, LANES), lambda b, p, i: (0, 0)),
                  pl.BlockSpec((None, tq, LANES), lambda b, p, i: (p, b * nq + i, 0)),
                  pl.BlockSpec((tq, LANES), lambda b, p, i: (b * nq + i, 0))],
        out_specs=[pl.BlockSpec((tq, LANES), lambda b, p, i: (b * nq + i, p)),
                   pl.BlockSpec((s, LANES), lambda b, p, i: (b, p)),
                   pl.BlockSpec((s, LANES), lambda b, p, i: (b, p)),
                   pl.BlockSpec((None, None, 8, LANES), lambda b, p, i: (b, p, 0, 0))],
        out_shape=[jax.ShapeDtypeStruct((t, 512), F32), jax.ShapeDtypeStruct((t, 512), F32),
                   jax.ShapeDtypeStruct((t, 512), F32), jax.ShapeDtypeStruct((nb, 4, 8, LANES), F32)],
        compiler_params=_params(("arbitrary", "arbitrary", "arbitrary")),
    )(qa, do_a, kae, vae, sink_row, lse, delta)


MESH = pl.DeviceIdType.MESH
ANY = pl.BlockSpec(memory_space=pl.ANY)
N_SEM = 7


def _gather_steps(pairs, send_sems, recv_sems, local_sems):
    x, y, c = lax.axis_index("x"), lax.axis_index("y"), lax.axis_index("c")
    me, sibling = (x, y, c), (x, y, 1 - c)
    chips = [(1 - x, y), (x, 1 - y), (1 - x, 1 - y)]
    mine, first, passed, landed, last = [], [], [], [], []
    for a, (x_ref, out_ref) in enumerate(pairs):
        def slot(px, py, pc, out_ref=out_ref):
            return out_ref.at[4 * px + 2 * py + pc]

        def copy(k, block, to, src=None, a=a, slot=slot):
            return pltpu.make_async_remote_copy(
                src_ref=slot(*block) if src is None else src, dst_ref=slot(*block),
                send_sem=send_sems.at[N_SEM * a + k], recv_sem=recv_sems.at[N_SEM * a + k], device_id=to, device_id_type=MESH)

        mine.append(pltpu.make_async_copy(x_ref, slot(*me), local_sems.at[a]))
        first += [copy(0, me, sibling, src=x_ref)] + [copy(1 + j, me, (*chip, c), src=x_ref) for j, chip in enumerate(chips)]
        passed += [copy(4 + j, (*chip, c), sibling) for j, chip in enumerate(chips)]
        landed += [copy(1 + j, (*chip, c), me) for j, chip in enumerate(chips)]
        last += [copy(0, sibling, me)] + [copy(4 + j, (*chip, 1 - c), me) for j, chip in enumerate(chips)]

    def start():
        for cp in mine + first:
            cp.start()

    def forward():
        for arrived, onward in zip(landed, passed):
            arrived.wait_recv()
            onward.start()

    def finish():
        for cp in last:
            cp.wait_recv()
        for cp in first + passed:
            cp.wait_send()
        for cp in mine:
            cp.wait()

    return start, forward, finish


def _exchange_steps(pairs, send_sems, recv_sems, local_sems):
    x, y, c = lax.axis_index("x"), lax.axis_index("y"), lax.axis_index("c")
    my_id = 4 * x + 2 * y + c
    local, remote = [], []
    for a, (src, dst) in enumerate(pairs):
        local.append(pltpu.make_async_copy(src.at[my_id], dst.at[my_id], local_sems.at[a]))
        for k in range(1, N_DEV):
            px = 1 - x if k & 4 else x
            py = 1 - y if k & 2 else y
            pc = 1 - c if k & 1 else c
            remote.append(pltpu.make_async_remote_copy(
                src_ref=src.at[4 * px + 2 * py + pc], dst_ref=dst.at[my_id],
                send_sem=send_sems.at[N_SEM * a + k - 1], recv_sem=recv_sems.at[N_SEM * a + k - 1],
                device_id=(px, py, pc), device_id_type=MESH))

    def start():
        for cp in local + remote:
            cp.start()

    def finish():
        for cp in remote:
            cp.wait_recv()
        for cp in remote:
            cp.wait_send()
        for cp in local:
            cp.wait()

    return start, finish


L_ONE = 64
L_CK = 65
L_CQ = 68
L_LSE = 71
L_DELTA = 74


def _head_block(pair, half):
    y = pair if half == 0 else pltpu.roll(pair, 64, axis=1)
    return jnp.where(_lane(pair.shape) < 64, y, 0.0)


def _put3(blk, lane0, col):
    lane = _lane(blk.shape)
    hi = col.astype(BF16).astype(F32)
    mid = (col - hi).astype(BF16).astype(F32)
    lo = (col - hi - mid).astype(BF16).astype(F32)
    return jnp.where(lane == lane0, hi, jnp.where(lane == lane0 + 1, mid, jnp.where(lane == lane0 + 2, lo, blk)))


def _spread3(col, shape, lane0s):
    lane = _lane(shape)
    hi = col.astype(BF16).astype(F32)
    mid = (col - hi).astype(BF16).astype(F32)
    lo = (col - hi - mid).astype(BF16).astype(F32)

    def at(k):
        return functools.reduce(jnp.logical_or, [lane == ln + k for ln in lane0s])

    return jnp.where(at(0), hi, jnp.where(at(1), mid, jnp.where(at(2), lo, 0.0)))


def _put_ones(blk, lanes):
    lane = _lane(blk.shape)
    hit = functools.reduce(jnp.logical_or, [lane == ln for ln in lanes])
    return jnp.where(hit, 1.0, blk)


def _to_pairs(ref):
    out = []
    for j in range(4):
        a, b = ref[:, 2 * LANES * j:2 * LANES * j + LANES], ref[:, 2 * LANES * j + LANES:2 * LANES * (j + 1)]
        out.append(jnp.where(_lane(a.shape) < 64, a, pltpu.roll(b, 64, axis=1)))
    return jnp.concatenate(out, axis=1)


def _fox_fwd(q_aug, k_aug, v_aug, nb, s, bt, shards=()):
    t = q_aug.shape[0]
    nq = s // bt
    n_in, n_sh = 3, len(shards)

    def body(*refs):
        q_ref, k_ref, v_ref = refs[:n_in]
        o_ref, ql_ref = refs[n_in + n_sh:n_in + n_sh + 2]
        if shards:
            srcs, dsts = refs[n_in:n_in + n_sh], refs[n_in + n_sh + 2:n_in + 2 * n_sh + 2]
            start, forward, finish = _gather_steps(list(zip(srcs, dsts)), *refs[n_in + 2 * n_sh + 2:])
            step = (pl.program_id(0) * 4 + pl.program_id(1)) * nq + pl.program_id(2)
            pl.when(step == 0)(start)
            pl.when(step == nb * 3 * nq)(forward)
        i = pl.program_id(2)
        sls = [slice(LANES * hh, LANES * (hh + 1)) for hh in range(2)]
        qhs = [q_ref[:, sl] for sl in sls]

        def update(m, acc, qrows, start, size, sl, causal):
            sc = _nt(qrows, k_ref[pl.ds(start, size), sl])
            if causal:
                row = lax.broadcasted_iota(jnp.int32, sc.shape, 0)
                col = lax.broadcasted_iota(jnp.int32, sc.shape, 1)
                sc = jnp.where(row >= col, sc, NEG_INF)
            m_new = jnp.maximum(m, jnp.max(sc, axis=1, keepdims=True))
            pr = jnp.exp2(sc - m_new).astype(BF16)
            acc = jnp.exp2(m - m_new) * acc + jnp.dot(pr, v_ref[pl.ds(start, size), sl], preferred_element_type=F32)
            return m_new, acc

        def blk(kb_i, carry):
            start = pl.multiple_of(kb_i * bt, bt)
            return tuple(update(m, acc, qh, start, bt, sl, False) for (m, acc), qh, sl in zip(carry, qhs, sls))

        def diag_blk(carry):
            start = pl.multiple_of(i * bt, bt)
            return tuple(update(m, acc, qh, start, bt, sl, True) for (m, acc), qh, sl in zip(carry, qhs, sls))

        init = tuple((jnp.full((bt, 1), NEG_INF, F32), jnp.zeros((bt, LANES), F32)) for _ in range(2))
        carry = lax.fori_loop(0, i, blk, init)
        outs = []
        for (m, acc), qh, sl in zip(diag_blk(carry), qhs, sls):
            l = acc[:, L_ONE:L_ONE + 1]
            outs.append(acc * (1.0 / l))
            ql_ref[:, sl] = _put3(qh.astype(F32), L_LSE, -(m + jnp.log(l) * LOG2E)).astype(BF16)
        o_ref[...] = jnp.where(_lane((1, LANES)) < 64, outs[0], pltpu.roll(outs[1], 64, axis=1)).astype(BF16)
        if shards:
            pl.when(step == nb * 4 * nq - 1)(finish)

    in_specs = [pl.BlockSpec((bt, 2 * LANES), lambda b, j, i: (b * nq + i, j)),
                pl.BlockSpec((s, 2 * LANES), lambda b, j, i: (b, j)),
                pl.BlockSpec((s, 2 * LANES), lambda b, j, i: (b, j))]
    out_specs = [pl.BlockSpec((bt, LANES), lambda b, j, i: (b * nq + i, j)),
                 pl.BlockSpec((bt, 2 * LANES), lambda b, j, i: (b * nq + i, j))]
    out_shape = [jax.ShapeDtypeStruct((t, 512), BF16), jax.ShapeDtypeStruct((t, 8 * LANES), BF16)]
    args, scratch = [q_aug, k_aug, v_aug, *shards], []
    if shards:
        in_specs += [ANY] * n_sh
        out_specs += [ANY] * n_sh
        out_shape += [jax.ShapeDtypeStruct((N_DEV,) + sh.shape, sh.dtype) for sh in shards]
        scratch = [pltpu.SemaphoreType.DMA((N_SEM * n_sh,)), pltpu.SemaphoreType.DMA((N_SEM * n_sh,)),
                   pltpu.SemaphoreType.DMA((n_sh,))]
    return pl.pallas_call(
        body, name="fox_fwd", grid=(nb, 4, nq), in_specs=in_specs, out_specs=out_specs, out_shape=out_shape,
        scratch_shapes=scratch, compiler_params=_params(("arbitrary", "arbitrary", "arbitrary")),
    )(*args)


def _fox_bwd(ql_aug, k_aug, v_aug, do_aug, nb, s, bt, exch=()):
    t = ql_aug.shape[0]
    nk = s // bt
    n_in, n_out, n_ex = 4, 3, len(exch)

    def body(*refs):
        q_ref, do_ref, k_ref, v_ref = refs[:n_in]
        dq_ref, dk_ref, dv_ref = refs[n_in + n_ex:n_in + n_ex + n_out]
        if exch:
            srcs = refs[n_in:n_in + n_ex]
            dsts = refs[n_in + n_ex + n_out:n_in + 2 * n_ex + n_out]
            start, finish = _exchange_steps(list(zip(srcs, dsts)), *refs[n_in + 2 * n_ex + n_out:])
            step = (pl.program_id(0) * 4 + pl.program_id(1)) * nk + pl.program_id(2)
            pl.when(step == 0)(start)
        kb_i = pl.program_id(2)

        @pl.when(kb_i == 0)
        def _():
            dq_ref[...] = jnp.zeros_like(dq_ref)

        row = lax.broadcasted_iota(jnp.int32, (bt, bt), 0)
        col = lax.broadcasted_iota(jnp.int32, (bt, bt), 1)
        sls = [slice(LANES * hh, LANES * (hh + 1)) for hh in range(2)]
        khs, vhs = [k_ref[:, sl] for sl in sls], [v_ref[:, sl] for sl in sls]

        def blk(qi, carry, diag):
            start = pl.multiple_of(qi * bt, bt)
            new = []
            for (dk_a, dv_a), kh, vh, sl in zip(carry, khs, vhs, sls):
                qblk, doblk = q_ref[pl.ds(start, bt), sl], do_ref[pl.ds(start, bt), sl]
                st = _nt(kh, qblk)
                if diag:
                    pt = jnp.where(col >= row, jnp.exp2(jnp.where(col >= row, st, 0.0)), 0.0)
                else:
                    pt = jnp.exp2(st)
                dst = pt * _nt(vh, doblk)
                ptb, dstb = pt.astype(BF16), dst.astype(BF16)
                dv_a = dv_a + jnp.dot(ptb, doblk, preferred_element_type=F32)
                dk_a = dk_a + jnp.dot(dstb, qblk, preferred_element_type=F32)
                dq_ref[pl.ds(start, bt), sl] += _tn(dstb, kh)
                new.append((dk_a, dv_a))
            return tuple(new)

        zero = jnp.zeros((bt, LANES), F32)
        carry = blk(kb_i, ((zero, zero), (zero, zero)), True)
        carry = lax.fori_loop(kb_i + 1, nk, lambda qi, c: blk(qi, c, False), carry)
        for (dk_acc, dv_acc), sl in zip(carry, sls):
            dk_ref[:, sl] = dk_acc
            dv_ref[:, sl] = dv_acc
        if exch:
            pl.when(step == nb * 4 * nk - 1)(finish)

    scratch = []
    if exch:
        scratch = [pltpu.SemaphoreType.DMA((N_SEM * n_ex,)), pltpu.SemaphoreType.DMA((N_SEM * n_ex,)),
                   pltpu.SemaphoreType.DMA((n_ex,))]
    whole = pl.BlockSpec((s, 2 * LANES), lambda b, j, kb_i: (b, j))
    tile = pl.BlockSpec((bt, 2 * LANES), lambda b, j, kb_i: (b * nk + kb_i, j))
    shp = jax.ShapeDtypeStruct((t, 8 * LANES), F32)
    return pl.pallas_call(
        body, name="fox_bwd", grid=(nb, 4, nk),
        in_specs=[whole, whole, tile, tile] + [ANY] * n_ex,
        out_specs=[whole, tile, tile] + [ANY] * n_ex,
        out_shape=[shp, shp, shp] + [jax.ShapeDtypeStruct(e.shape, e.dtype) for e in exch],
        scratch_shapes=scratch, compiler_params=_params(("arbitrary", "arbitrary", "arbitrary")),
    )(ql_aug, do_aug, k_aug, v_aug, *exch)


FF_BLK = D_FF // N_DEV


def _mlp_fwd(x2, ma, mb, tgt, w_out, g2, w_up, w_down, tm):
    t = x2.shape[0]

    def body(x_ref, ma_ref, mb_ref, tg_ref, wo_ref, g2_ref, wu_ref, wd_ref,
             h_ref, hn_ref, hid_ref, dy_ref, dyb_ref, loss_ref):
        @pl.when(pl.program_id(0) == 0)
        def _():
            loss_ref[...] = jnp.zeros_like(loss_ref)

        h = (x_ref[...] + jnp.dot(ma_ref[...], wo_ref[0:512, :], preferred_element_type=F32)
             + jnp.dot(mb_ref[...], wo_ref[512:1024, :], preferred_element_type=F32))
        h_ref[...] = h
        r = lax.rsqrt(jnp.mean(h * h, axis=-1, keepdims=True) + EPS)
        hn = (h * r * g2_ref[...]).astype(BF16)
        hn_ref[...] = hn
        for d in range(N_DEV):
            u = jnp.maximum(jnp.dot(hn, wu_ref[d], preferred_element_type=F32), 0.0)
            hid_ref[:, FF_BLK * d:FF_BLK * (d + 1)] = (u * u).astype(BF16)
        y = h + jnp.dot(hid_ref[...], wd_ref[...], preferred_element_type=F32)
        err = y - tg_ref[...]
        dy = err * (1.0 / D_MODEL)
        dy_ref[...] = dy
        dyb_ref[...] = dy.astype(BF16)
        part =0.5 * jnp.sum(jnp.sum(err * err, axis=1, keepdims=True) * (1.0 / D_MODEL), axis=0, keepdims=True)
        loss_ref[...] += part

    def tile(w):
        return pl.BlockSpec((tm, w), lambda i: (i, 0))

    return pl.pallas_call(
        body, name="mlp_fwd", grid=(t // tm,),
        in_specs=[tile(D_MODEL), tile(512), tile(512), tile(D_MODEL), _const_spec((D_MODEL, D_MODEL)),
                  _const_spec((1, D_MODEL)), _const_spec((N_DEV, D_MODEL, FF_BLK)), _const_spec((D_FF, D_MODEL))],
        out_specs=[tile(D_MODEL), tile(D_MODEL), tile(D_FF), tile(D_MODEL), tile(D_MODEL),
                   pl.BlockSpec((8, LANES), lambda i: (0, 0))],
        out_shape=[jax.ShapeDtypeStruct((t, D_MODEL), F32), jax.ShapeDtypeStruct((t, D_MODEL), BF16),
                   jax.ShapeDtypeStruct((t, D_FF), BF16), jax.ShapeDtypeStruct((t, D_MODEL), F32),
                   jax.ShapeDtypeStruct((t, D_MODEL), BF16), jax.ShapeDtypeStruct((8, LANES), F32)],
        compiler_params=_params(("arbitrary",)),
    )(x2, ma, mb, tgt, w_out, g2, w_up, w_down)


def _mlp_bwd(dy, hid, h, ma, mb, w_down, w_up_t, w_out, g2, tm):
    t = dy.shape[0]

    def body(dy_ref, hid_ref, h_ref, ma_ref, mb_ref, wd_ref, wut_ref, wo_ref, g2_ref,
             du_ref, dh_ref, dhb_ref, dma_ref, dob_ref, dla_ref, gg_ref):
        @pl.when(pl.program_id(0) == 0)
        def _():
            gg_ref[...] = jnp.zeros_like(gg_ref)

        dy = dy_ref[...]
        d_hid = _nt(dy.astype(BF16), wd_ref[...])
        du = (d_hid * (2.0 * jnp.sqrt(hid_ref[...].astype(F32)))).astype(BF16)
        du_ref[...] = du
        d_hn = jnp.dot(du, wut_ref[...], preferred_element_type=F32)
        h = h_ref[...]
        r = lax.rsqrt(jnp.mean(h * h, axis=-1, keepdims=True) + EPS)
        hat = h * r
        gd = d_hn * g2_ref[...]
        dh = dy + r * (gd - hat * jnp.mean(gd * hat, axis=-1, keepdims=True))
        gg_ref[...] += jnp.sum(d_hn * hat, axis=0, keepdims=True)
        dh_ref[...] = dh
        dhb = dh.astype(BF16)
        dhb_ref[...] = dhb
        dm = _nt(dhb, wo_ref[...]).astype(BF16)
        dma, dmb = dm[:, 0:512], dm[:, 512:1024]
        dma_ref[...] = dma
        sel = (lax.shift_right_logical(lax.broadcasted_iota(jnp.int32, (512, LANES), 0), 6)
               == lax.broadcasted_iota(jnp.int32, (512, LANES), 1)).astype(BF16)
        dla_ref[...] = jnp.dot((dma.astype(F32) * ma_ref[...].astype(F32)).astype(BF16), sel, preferred_element_type=F32)
        dmb32 = dmb.astype(F32)
        dlb = jnp.dot((dmb32 * mb_ref[...].astype(F32)).astype(BF16), sel, preferred_element_type=F32)
        for hd in range(8):
            blk = _head_block(dmb32[:, LANES * (hd // 2):LANES * (hd // 2 + 1)], hd % 2)
            dob_ref[:, LANES * hd:LANES * (hd + 1)] = _put3(blk, L_DELTA, -dlb[:, hd:hd + 1]).astype(BF16)

    def tile(w):
        return pl.BlockSpec((tm, w), lambda i: (i, 0))

    return pl.pallas_call(
        body, name="mlp_bwd", grid=(t // tm,),
        in_specs=[tile(D_MODEL), tile(D_FF), tile(D_MODEL), tile(512), tile(512), _const_spec((D_FF, D_MODEL)),
                  _const_spec((D_FF, D_MODEL)), _const_spec((D_MODEL, D_MODEL)), _const_spec((1, D_MODEL))],
        out_specs=[tile(D_FF), tile(D_MODEL), tile(D_MODEL), tile(512), tile(8 * LANES), tile(LANES),
                   pl.BlockSpec((1, D_MODEL), lambda i: (0, 0))],
        out_shape=[jax.ShapeDtypeStruct((t, D_FF), BF16), jax.ShapeDtypeStruct((t, D_MODEL), F32),
                   jax.ShapeDtypeStruct((t, D_MODEL), BF16), jax.ShapeDtypeStruct((t, 512), BF16),
                   jax.ShapeDtypeStruct((t, 8 * LANES), BF16), jax.ShapeDtypeStruct((t, LANES), F32),
                   jax.ShapeDtypeStruct((1, D_MODEL), F32)],
        compiler_params=_params(("arbitrary",), VMEM_LIMIT_WIDE),
    )(dy, hid, h, ma, mb, w_down, w_up_t, w_out, g2)


def _wgrad(a, b, name, bm, bn, tk, out_dtype=F32, col_blocks=False, a2=None):
    t, m = a.shape
    n = b.shape[1]
    bm, bn = min(bm, m), min(bn, n)
    nk = t // tk

    def body(*refs):
        if a2 is None:
            a_ref, b_ref, o_ref, acc = refs
        else:
            a_ref, b_ref, a2_ref, o_ref, o2_ref, acc, acc2 = refs
        i, k = pl.program_id(0), pl.program_id(2)

        @pl.when(k == 0)
        def _():
            acc[...] = jnp.zeros_like(acc)

        acc[...] += _tn(a_ref[...], b_ref[...])

        @pl.when(k == nk - 1)
        def _():
            o_ref[...] = acc[...].astype(out_dtype)

        if a2 is not None:
            @pl.when((i == 0) & (k == 0))
            def _():
                acc2[...] = jnp.zeros_like(acc2)

            @pl.when(i == 0)
            def _():
                acc2[...] += _tn(a2_ref[...], b_ref[...])

            @pl.when((i == 0) & (k == nk - 1))
            def _():
                o2_ref[...] = acc2[...]

    if col_blocks:
        out_spec = pl.BlockSpec((None, bm, bn), lambda i, j, k: (j, i, 0))
        out_shape = jax.ShapeDtypeStruct((n // bn, m, bn), out_dtype)
    else:
        out_spec = pl.BlockSpec((bm, bn), lambda i, j, k: (i, j))
        out_shape = jax.ShapeDtypeStruct((m, n), out_dtype)
    in_specs = [pl.BlockSpec((tk, bm), lambda i, j, k: (k, i)), pl.BlockSpec((tk, bn), lambda i, j, k: (k, j))]
    out_specs, out_shapes, scratch, args = [out_spec], [out_shape], [pltpu.VMEM((bm, bn), F32)], [a, b]
    if a2 is not None:
        m2 = a2.shape[1]
        in_specs.append(pl.BlockSpec((tk, m2), lambda i, j, k: (k, 0)))
        out_specs.append(pl.BlockSpec((m2, n), lambda i, j, k: (0, 0)))
        out_shapes.append(jax.ShapeDtypeStruct((m2, n), F32))
        scratch.append(pltpu.VMEM((m2, n), F32))
        args.append(a2)
    out = pl.pallas_call(
        body, name=name, grid=(m // bm, n // bn, nk), in_specs=in_specs, out_specs=out_specs, out_shape=out_shapes,
        scratch_shapes=scratch, compiler_params=_params(("arbitrary", "arbitrary", "arbitrary")),
    )(*args)
    return out[0] if a2 is None else out


def _proj_bwd(raw, dqa, dkae, dvae, dqb, dkb, dvb, fl, bf_row, x2, dh, w_main_t, w_f_t, g1, gqa, gka, gqb, gkb, nb, s, tm):
    t = x2.shape[0]
    nt = s // tm

    def body(raw_ref, dqa_ref, dkae_ref, dvae_ref, dqb_ref, dkb_ref, dvb_ref, fl_ref, b_ref, x_ref, dh_ref,
             wmt_ref, wft_ref, g1_ref, gqa_ref, gka_ref, gqb_ref, gkb_ref,
             dx_ref, dp_ref, dfb_ref, ggqa_ref, ggka_ref, ggqb_ref, ggkb_ref, gg1_ref, gb_ref, carry, dlf_ref):
        @pl.when((pl.program_id(0) == 0) & (pl.program_id(1) == 0))
        def _():
            for r in (ggqa_ref, ggka_ref, ggqb_ref, ggkb_ref, gg1_ref, gb_ref):
                r[...] = jnp.zeros_like(r)

        @pl.when(pl.program_id(1) == 0)
        def _():
            carry[...] = jnp.zeros_like(carry)

        lane = _lane((tm, LANES))
        dc = jnp.zeros((tm, LANES), F32)
        for hd in range(8):
            col = (dqb_ref[:, LANES * hd + L_CQ:LANES * hd + L_CQ + 1] - dkb_ref[:, LANES * hd + L_CK:LANES * hd + L_CK + 1])
            dc = jnp.where(lane == hd, col, dc)
        dlf_ref[...] = _tri_dot(tm, True, dc) + carry[...]
        carry[...] = dlf_ref[pl.ds(0, 1), :]
        dfl = dlf_ref[...] * (1.0 / (1.0 + jnp.exp(fl_ref[...] + b_ref[...])))
        gb_ref[...] += jnp.sum(dfl, axis=0, keepdims=True)

        raw = raw_ref[...]
        d_qa, p_qa = _head_norm_bwd(raw[:, 0:512], gqa_ref[...], dqa_ref[...])
        d_ka, p_ka = _head_norm_bwd(raw[:, 512:640], gka_ref[...], _fold_kv(dkae_ref[...]))
        d_va = _fold_kv(dvae_ref[...])
        d_qb, p_qb = _head_norm_bwd(raw[:, 768:1280], gqb_ref[...], _to_pairs(dqb_ref) * SCALE)
        d_kb, p_kb = _head_norm_bwd(raw[:, 1280:1792], gkb_ref[...], _to_pairs(dkb_ref) * (1.0 / LOG2E))
        ggqa_ref[...] += jnp.sum(p_qa, axis=0, keepdims=True)
        ggka_ref[...] += jnp.sum(p_ka, axis=0, keepdims=True)
        ggqb_ref[...] += jnp.sum(p_qb, axis=0, keepdims=True)
        ggkb_ref[...] += jnp.sum(p_kb, axis=0, keepdims=True)
        dproj = jnp.concatenate([d_qa, d_ka, d_va, d_qb, d_kb, _to_pairs(dvb_ref)], axis=1).astype(BF16)
        dp_ref[...] = dproj
        dfb = dfl.astype(BF16)
        dfb_ref[...] = dfb
        d_xn = (jnp.dot(dproj, wmt_ref[...], preferred_element_type=F32)
                + jnp.dot(dfb, wft_ref[...], preferred_element_type=F32))
        x = x_ref[...]
        r = lax.rsqrt(jnp.mean(x * x, axis=-1, keepdims=True) + EPS)
        hat = x * r
        gd = d_xn * g1_ref[...]
        dx_ref[...] = dh_ref[...] + r * (gd - hat * jnp.mean(gd * hat, axis=-1, keepdims=True))
        gg1_ref[...] += jnp.sum(d_xn * hat, axis=0, keepdims=True)

    def tile(w):
        return pl.BlockSpec((tm, w), lambda b, i: (b * nt + (nt - 1 - i), 0))

    def acc(w):
        return pl.BlockSpec((1, w), lambda b, i: (0, 0))

    return pl.pallas_call(
        body, name="proj_bwd", grid=(nb, nt),
        in_specs=[tile(MAIN_W), tile(512), tile(512), tile(512), tile(8 * LANES), tile(8 * LANES), tile(8 * LANES), tile(LANES),
                  _const_spec((1, LANES)), tile(D_MODEL), tile(D_MODEL), _const_spec((MAIN_W, D_MODEL)),
                  _const_spec((LANES, D_MODEL)), _const_spec((1, D_MODEL)), _const_spec((1, 512)), _const_spec((1, 128)),
                  _const_spec((1, 512)), _const_spec((1, 512))],
        out_specs=[tile(D_MODEL), tile(MAIN_W), tile(LANES), acc(512), acc(128), acc(512), acc(512), acc(D_MODEL), acc(LANES)],
        out_shape=[jax.ShapeDtypeStruct((t, D_MODEL), F32), jax.ShapeDtypeStruct((t, MAIN_W), BF16),
                   jax.ShapeDtypeStruct((t, LANES), BF16), jax.ShapeDtypeStruct((1, 512), F32),
                   jax.ShapeDtypeStruct((1, 128), F32), jax.ShapeDtypeStruct((1, 512), F32),
                   jax.ShapeDtypeStruct((1, 512), F32), jax.ShapeDtypeStruct((1, D_MODEL), F32),
                   jax.ShapeDtypeStruct((1, LANES), F32)],
        scratch_shapes=[pltpu.VMEM((1, LANES), F32), pltpu.VMEM((tm, LANES), F32)],
        compiler_params=_params(("arbitrary", "arbitrary"), VMEM_LIMIT_WIDE),
    )(raw, dqa, dkae, dvae, dqb, dkb, dvb, fl, bf_row, x2, dh, w_main_t, w_f_t, g1, gqa, gka, gqb, gkb)


IN_PAD = 304


def _local_step(x, tgt, w_in_t, rest, g1, b_forget, qna, kna, sinks, qnb, knb, g2,
                tm=512, bt=1024, btf=1024, tq=4096, wk=4096, wkb=8192, distributed=False):
    nb, s, _ = x.shape
    t = nb * s
    x2, tgt2 = x.reshape(t, D_MODEL), tgt.reshape(t, D_MODEL)
    g1r, g2r = g1.reshape(1, D_MODEL), g2.reshape(1, D_MODEL)
    gqa, gka = jnp.tile(qna, 8).reshape(1, 512), jnp.tile(kna, 2).reshape(1, 128)
    gqb, gkb = jnp.tile(qnb, 8).reshape(1, 512), jnp.tile(knb, 8).reshape(1, 512)
    bf_row = jnp.pad(b_forget, (0, LANES - 8)).reshape(1, LANES)
    sink_row = jnp.pad(sinks, (0, LANES - 8)).reshape(1, LANES)
    w_main_t = w_in_t[0:MAIN_W]
    w_f_t = jnp.pad(w_in_t[MAIN_W:IN_W], ((0, LANES - 8), (0, 0)))

    xn, raw, fl, qa, kae, vae, q_aug, k_aug, v_aug = _norm_proj(x2, g1r, w_main_t, w_f_t, gqa, gka, gqb, gkb, bf_row, s, tm)
    ma, lse_a = _swa_fwd(qa, kae, vae, sink_row, nb, s, tq)
    if distributed:
        mb, ql_aug, w_out, w_up, w_down, w_up_t = _fox_fwd(q_aug, k_aug, v_aug, nb, s, btf, shards=rest)
    else:
        mb, ql_aug = _fox_fwd(q_aug, k_aug, v_aug, nb, s, btf)
        w_out, w_up, w_down, w_up_t = rest
    w_out, w_down = w_out.reshape(D_MODEL, D_MODEL), w_down.reshape(D_FF, D_MODEL)
    h, hn, hid, dy, dyb, loss_acc = _mlp_fwd(x2, ma, mb, tgt2, w_out, g2r, w_up, w_down, tm)

    du, dh, dhb, dma, do_aug, dla, gg2 = _mlp_bwd(dy, hid, h, ma, mb, w_down, w_up_t.reshape(D_FF, D_MODEL), w_out, g2r, tm)
    g_down = _wgrad(hid, dyb, "wgrad_down", 512, 1024, wkb, BF16).reshape(N_DEV, 512, D_MODEL)
    g_up = _wgrad(hn, du, "wgrad_up", 1024, 512, wkb, BF16, col_blocks=True)
    g_out = jnp.concatenate([_wgrad(ma, dhb, "wgrad_out_a", 512, 1024, wk, BF16),
                             _wgrad(mb, dhb, "wgrad_out_b", 512, 1024, wk, BF16)], axis=0).reshape(N_DEV, 128, D_MODEL)

    dqa, dkae, dvae, dsink = _swa_bwd(qa, kae, vae, dma, sink_row, lse_a, dla, nb, s, tq)
    fox = _fox_bwd(ql_aug, k_aug, v_aug, do_aug, nb, s, bt, exch=(g_out, g_up, g_down) if distributed else ())
    dqb, dkb, dvb = fox[:3]
    if distributed:
        g_out, g_up, g_down = fox[3:]
    grad_x, dproj, dfb, ggqa, ggka, ggqb, ggkb, gg1, gbf = _proj_bwd(
        raw, dqa, dkae, dvae, dqb, dkb, dvb, fl, bf_row, x2, dh, w_main_t, w_f_t, g1r, gqa, gka, gqb, gkb, nb, s, tm)
    g_main_t, g_gate_t = _wgrad(dproj, xn, "wgrad_in", 768, 1024, wk, a2=dfb)
    g_in_t = jnp.concatenate([g_main_t, g_gate_t[0:8]], axis=0)

    small = (gg1.reshape(D_MODEL), gbf[0, 0:8], ggqa.reshape(8, 64).sum(0), ggka.reshape(2, 64).sum(0),
             dsink.sum(0)[:, 0:2, 0].reshape(8), ggqb.reshape(8, 64).sum(0), ggkb.reshape(8, 64).sum(0),
             gg2.reshape(D_MODEL))
    return loss_acc[0, 0], grad_x.reshape(nb, s, D_MODEL), g_in_t, g_out, g_up, g_down, small


def _all_gather(shard):
    def body(x_ref, out_ref, send_sems, recv_sems, local_sem):
        start, forward, finish = _gather_steps([(x_ref, out_ref)], send_sems, recv_sems, local_sem)
        start()
        forward()
        finish()

    return pl.pallas_call(
        body, name="gather_w_in", out_shape=jax.ShapeDtypeStruct((N_DEV,) + shard.shape, shard.dtype),
        in_specs=[ANY], out_specs=ANY,
        scratch_shapes=[pltpu.SemaphoreType.DMA((N_SEM,)), pltpu.SemaphoreType.DMA((N_SEM,)), pltpu.SemaphoreType.DMA((1,))],
    )(shard)


def _exchange(*arrays):
    n_ex = len(arrays)
    srcs = [jax.new_ref(a, memory_space=pltpu.MemorySpace.HBM) for a in arrays]
    dsts = [jax.empty_ref(jax.ShapeDtypeStruct(a.shape, a.dtype), memory_space=pltpu.MemorySpace.HBM) for a in arrays]

    @pl.kernel(mesh=plsc.ScalarSubcoreMesh(axis_name="sequencer", num_cores=1), name="exchange_tail",
               scratch_types=(pltpu.SemaphoreType.DMA((N_SEM * n_ex,)), pltpu.SemaphoreType.DMA((N_SEM * n_ex,)),
                              pltpu.SemaphoreType.DMA((n_ex,))),
               compiler_params=pltpu.CompilerParams(collective_id=0))
    def launch(send_sems, recv_sems, local_sems):
        x, y, c = lax.axis_index("x"), lax.axis_index("y"), lax.axis_index("c")
        barrier = pltpu.get_barrier_semaphore()
        for k in range(1, N_DEV):
            peer = (1 - x if k & 4 else x, 1 - y if k & 2 else y, 1 - c if k & 1 else c)
            pl.semaphore_signal(barrier, inc=1, device_id=peer, device_id_type=MESH)
        pl.semaphore_wait(barrier, N_DEV - 1)
        start, finish = _exchange_steps(list(zip(srcs, dsts)), send_sems, recv_sems, local_sems)
        start()
        finish()

    launch()
    return [d[...] for d in dsts]


def _sum_adamw(recv, w, m, v, tr, name):
    _, r, n = recv.shape

    def body(r_ref, w_ref, m_ref, v_ref, g_ref, d_ref, nm_ref, nv_ref):
        g = r_ref[0].astype(F32)
        for s in range(1, N_DEV):
            g = g + r_ref[s].astype(F32)
        g_ref[...] = g
        nm = ADAM_B1 * m_ref[...] + (1.0 - ADAM_B1) * g
        nv = ADAM_B2 * v_ref[...] + (1.0 - ADAM_B2) * (g * g)
        m_hat = nm / (1.0 - ADAM_B1 ** ADAM_STEP)
        v_hat = nv / (1.0 - ADAM_B2 ** ADAM_STEP)
        d_ref[...] = -ADAM_LR * (m_hat / (jnp.sqrt(v_hat) + ADAM_EPS) + ADAM_WD * w_ref[...])
        nm_ref[...] = nm
        nv_ref[...] = nv

    tile = pl.BlockSpec((tr, n), lambda i: (i, 0))
    shp = jax.ShapeDtypeStruct((r, n), F32)
    return pl.pallas_call(
        body, name=name, grid=(r // tr,),
        in_specs=[pl.BlockSpec((N_DEV, tr, n), lambda i: (0, i, 0)), tile, tile, tile],
        out_specs=[tile, tile, tile, tile], out_shape=[shp, shp, shp, shp],
        compiler_params=_params(("arbitrary",)),
    )(recv, w, m, v)


def _small_rows(g1, bf, qna, kna, sk, qnb, knb, g2, extra=None):
    row2 = jnp.concatenate([bf, qna, kna, sk, qnb, knb])
    rows = [g1, g2, jnp.pad(row2, (0, D_MODEL - row2.shape[0]))]
    if extra is not None:
        rows.append(jnp.pad(extra.reshape(1), (0, D_MODEL - 1)))
    return jnp.pad(jnp.stack(rows), ((0, 8 - len(rows)), (0, 0)))


def _in_rows(w_in_s):
    return jnp.pad(w_in_s.T, ((0, IN_PAD - IN_SHARD), (0, 0)))


def kernel(x, attn_norm_g, w_in, b_forget, q_norm_a, k_norm_a, sink_logits, q_norm_b, k_norm_b, w_out, mlp_norm_g, w_up, w_down, loss_target, m_attn_norm_g, m_w_in, m_b_forget, m_q_norm_a, m_k_norm_a, m_sink_logits, m_q_norm_b, m_k_norm_b, m_w_out, m_mlp_norm_g, m_w_up, m_w_down, v_attn_norm_g, v_w_in, v_b_forget, v_q_norm_a, v_k_norm_a, v_sink_logits, v_q_norm_b, v_k_norm_b, v_w_out, v_mlp_norm_g, v_w_up, v_w_down):
    w_in_r = _in_rows(w_in)
    w_in_t = _all_gather(w_in_r.astype(BF16))[:, 0:IN_SHARD].reshape(IN_W, D_MODEL)
    w_up_b = w_up.astype(BF16)
    rest = (w_out.astype(BF16), w_up_b, w_down.astype(BF16), w_up_b.T)

    loss_part, grad_x, g_in_t, r_out, r_up, r_down, small = _local_step(
        x, loss_target, w_in_t, rest, attn_norm_g, b_forget, q_norm_a, k_norm_a, sink_logits, q_norm_b, k_norm_b, mlp_norm_g,
        distributed=True)

    g_in_blocks = jnp.pad(g_in_t.reshape(N_DEV, IN_SHARD, D_MODEL), ((0, 0), (0, IN_PAD - IN_SHARD), (0, 0))).astype(BF16)
    small_blocks = jnp.broadcast_to(_small_rows(*small, extra=loss_part), (N_DEV, 8, D_MODEL))
    r_in, r_small = _exchange(g_in_blocks, small_blocks)

    small_w = _small_rows(attn_norm_g, b_forget, q_norm_a, k_norm_a, sink_logits, q_norm_b, k_norm_b, mlp_norm_g)
    small_m = _small_rows(m_attn_norm_g, m_b_forget, m_q_norm_a, m_k_norm_a, m_sink_logits, m_q_norm_b, m_k_norm_b, m_mlp_norm_g)
    small_v = _small_rows(v_attn_norm_g, v_b_forget, v_q_norm_a, v_k_norm_a, v_sink_logits, v_q_norm_b, v_k_norm_b, v_mlp_norm_g)
    o_in = [a[0:IN_SHARD].T for a in _sum_adamw(r_in, w_in_r, _in_rows(m_w_in), _in_rows(v_w_in), IN_PAD, "adamw_in")]
    o_out = _sum_adamw(r_out, w_out, m_w_out, v_w_out, 128, "adamw_out")
    o_up = _sum_adamw(r_up, w_up, m_w_up, v_w_up, 256, "adamw_up")
    o_down = _sum_adamw(r_down, w_down, m_w_down, v_w_down, 128, "adamw_down")
    o_small = _sum_adamw(r_small, small_w, small_m, small_v, 8, "adamw_small")

    def leaves(i):
        row2 = o_small[i][2]
        return (o_small[i][0], o_in[i], row2[0:8], row2[8:72], row2[72:136], row2[136:144], row2[144:208], row2[208:272],
                o_out[i], o_small[i][1], o_up[i], o_down[i])

    return (o_small[0][3, 0], grad_x, *leaves(0), *leaves(1), *leaves(2), *leaves(3))
```

```python
import functools

import jax
import jax.numpy as jnp
from jax import lax
from jax.experimental import pallas as pl
from jax.experimental.pallas import tpu as pltpu
from jax.experimental.pallas import tpu_sc as plsc

F32 = jnp.float32
BF16 = jnp.bfloat16

D_MODEL = 1024
HEAD_DIM = 64
N_DEV = 8
D_FF = 4096
MAIN_W = 2304
IN_W = 2312
IN_SHARD = 289
WINDOW = 128
EPS = 1e-6
SCALE = 0.125
LOG2E = 1.4426950408889634
LANES = 128
NEG_INF = float("-inf")

ADAM_LR = 0.001
ADAM_B1 = 0.9
ADAM_B2 = 0.999
ADAM_EPS = 1e-08
ADAM_WD = 0.01
ADAM_STEP = 10

VMEM_LIMIT = 56 * 1024 * 1024
VMEM_LIMIT_WIDE = 62 * 1024 * 1024


def _params(sem, vmem=VMEM_LIMIT):
    return pltpu.CompilerParams(dimension_semantics=sem, vmem_limit_bytes=vmem)


def _const_spec(shape):
    nd = len(shape)
    return pl.BlockSpec(shape, lambda *_: (0,) * nd, pipeline_mode=pl.Buffered(1))


def _lane(shape):
    return lax.broadcasted_iota(jnp.int32, shape, len(shape) - 1)


def _head_ones(n):
    r = lax.shift_right_logical(lax.broadcasted_iota(jnp.int32, (n, n), 0), 6)
    c = lax.shift_right_logical(lax.broadcasted_iota(jnp.int32, (n, n), 1), 6)
    return (r == c).astype(BF16)


def _head_sum(v):
    w = v.shape[1]
    vb = v.astype(BF16)
    if w <= 256:
        return jnp.dot(vb, _head_ones(w), preferred_element_type=F32)
    ones = _head_ones(256)
    return jnp.concatenate([jnp.dot(vb[:, s:s + 256], ones, preferred_element_type=F32) for s in range(0, w, 256)], axis=1)


def _head_norm(seg, gain):
    rs = lax.rsqrt(_head_sum(seg * seg) * (1.0 / HEAD_DIM) + EPS)
    return seg * rs * gain


def _head_norm_bwd(seg, gain, d_out):
    rs = lax.rsqrt(_head_sum(seg * seg) * (1.0 / HEAD_DIM) + EPS)
    hat = seg * rs
    gd = d_out * gain
    d_seg = rs * (gd - hat * (_head_sum(gd * hat) * (1.0 / HEAD_DIM)))
    return d_seg, d_out * hat


def _expand_kv(v):
    r = pltpu.roll(v, 64, axis=1)
    lo = _lane(v.shape) < 64
    return jnp.concatenate([jnp.where(lo, v, r), jnp.where(lo, r, v)], axis=1)


def _fold_kv(e4):
    t0 = e4[:, 0:128] + e4[:, 128:256]
    t1 = e4[:, 256:384] + e4[:, 384:512]
    t0 = t0 + pltpu.roll(t0, 64, axis=1)
    t1 = t1 + pltpu.roll(t1, 64, axis=1)
    return jnp.where(_lane(t0.shape) < 64, t0, t1)


def _pick_lane(blk, idx):
    return jnp.sum(jnp.where(_lane(blk.shape) == idx, blk, 0.0), axis=1, keepdims=True)


def _nt(a, b):
    return lax.dot_general(a, b, (((1,), (1,)), ((), ())), preferred_element_type=F32)


def _tn(a, b):
    return lax.dot_general(a, b, (((0,), (0,)), ((), ())), preferred_element_type=F32)


def _xnorm(x2, g1, tm):
    t = x2.shape[0]

    def body(x_ref, g1_ref, xn_ref):
        x = x_ref[...]
        r = lax.rsqrt(jnp.mean(x * x, axis=-1, keepdims=True) + EPS)
        xn_ref[...] = (x * r * g1_ref[...]).astype(BF16)

    tile = pl.BlockSpec((tm, D_MODEL), lambda i: (i, 0))
    return pl.pallas_call(
        body, name="xnorm", grid=(t // tm,), in_specs=[tile, _const_spec((1, D_MODEL))], out_specs=tile,
        out_shape=jax.ShapeDtypeStruct((t, D_MODEL), BF16), compiler_params=_params(("arbitrary",)),
    )(x2, g1)


def _norm_proj(xn, w_main_t, w_f_t, gqa, gka, gqb, gkb, bf_row, s, tm):
    t = xn.shape[0]
    nt = s // tm

    def body(xn_ref, wm_ref, wf_ref, gqa_ref, gka_ref, gqb_ref, gkb_ref, b_ref,
             raw_ref, fl_ref, qa_ref, kae_ref, vae_ref, qo_ref, ko_ref, vo_ref, carry, c_ref):
        @pl.when(lax.rem(pl.program_id(0), nt) == 0)
        def _():
            carry[...] = jnp.zeros_like(carry)

        xn = xn_ref[...]
        proj = _nt(xn, wm_ref[...])
        raw_ref[...] = proj
        fl = _nt(xn, wf_ref[...])
        fl_ref[...] = fl
        qa_ref[...] = _head_norm(proj[:, 0:512], gqa_ref[...]).astype(BF16)
        kae_ref[...] = _expand_kv(_head_norm(proj[:, 512:640], gka_ref[...])).astype(BF16)
        vae_ref[...] = _expand_kv(proj[:, 640:768]).astype(BF16)

        z = fl + b_ref[...]
        e = jnp.exp(-jnp.abs(z))
        u = 1.0 + e
        log1p = jnp.where(u == 1.0, e, jnp.log(u) * (e / (u - 1.0)))
        lf = jnp.minimum(z, 0.0) - log1p
        for r0 in range(0, tm, 256):
            c_ref[r0:r0 + 256, :] = _tri_dot(256, False, lf[r0:r0 + 256]) + carry[...]
            carry[...] = c_ref[pl.ds(r0 + 255, 1), :]
        c2 = c_ref[...] * LOG2E
        qb = _head_norm(proj[:, 768:1280], gqb_ref[...]) * (SCALE * LOG2E)
        kb = _head_norm(proj[:, 1280:1792], gkb_ref[...])
        lane = _lane((tm, LANES))
        for h in range(8):
            j, half = h // 2, h % 2
            pair, blk = slice(LANES * j, LANES * (j + 1)), slice(LANES * h, LANES * (h + 1))
            feat = _spread3(c2[:, h:h + 1], (tm, LANES), (L_CK, L_CQ))
            q = _put_ones(_head_block(qb[:, pair], half), (L_CK, L_CK + 1, L_CK + 2))
            qo_ref[:, blk] = jnp.where((lane >= L_CQ) & (lane < L_CQ + 3), feat, q).astype(BF16)
            k = _put_ones(_head_block(kb[:, pair], half), tuple(range(L_CQ, L_CQ + 6)))
            ko_ref[:, blk] = jnp.where((lane >= L_CK) & (lane < L_CK + 3), -feat, k).astype(BF16)
            v = _head_block(proj[:, 1792 + LANES * j:1792 + LANES * (j + 1)], half)
            vo_ref[:, blk] = _put_ones(v, (L_ONE, L_DELTA, L_DELTA + 1, L_DELTA + 2)).astype(BF16)

    def tile(w):
        return pl.BlockSpec((tm, w), lambda i: (i, 0))

    aug = jax.ShapeDtypeStruct((t, 8 * LANES), BF16)
    return pl.pallas_call(
        body, name="norm_proj", grid=(t // tm,),
        in_specs=[tile(D_MODEL), _const_spec((MAIN_W, D_MODEL)), _const_spec((LANES, D_MODEL)),
                  _const_spec((1, 512)), _const_spec((1, 128)), _const_spec((1, 512)), _const_spec((1, 512)),
                  _const_spec((1, LANES))],
        out_specs=[tile(MAIN_W), tile(LANES), tile(512), tile(256), tile(256)] + [tile(8 * LANES)] * 3,
        out_shape=[jax.ShapeDtypeStruct((t, MAIN_W), F32),
                   jax.ShapeDtypeStruct((t, LANES), F32), jax.ShapeDtypeStruct((t, 512), BF16),
                   jax.ShapeDtypeStruct((t, 256), BF16), jax.ShapeDtypeStruct((t, 256), BF16), aug, aug, aug],
        scratch_shapes=[pltpu.VMEM((1, LANES), F32), pltpu.VMEM((tm, LANES), F32)],
        compiler_params=_params(("arbitrary",)),
    )(xn, w_main_t, w_f_t, gqa, gka, gqb, gkb, bf_row)


def _tri_dot(n, upper, v):
    r = lax.broadcasted_iota(jnp.int32, (n, n), 0)
    c = lax.broadcasted_iota(jnp.int32, (n, n), 1)
    tri = ((c >= r) if upper else (c <= r)).astype(BF16)
    hi = v.astype(BF16)
    mid = (v - hi.astype(F32)).astype(BF16)
    lo = (v - hi.astype(F32) - mid.astype(F32)).astype(BF16)
    return (jnp.dot(tri, hi, preferred_element_type=F32) + jnp.dot(tri, mid, preferred_element_type=F32)
            + jnp.dot(tri, lo, preferred_element_type=F32))


def _slope(p, hh):
    out = jnp.float32(2.0 ** -(2 * 3 + hh + 1))
    for pp in (2, 1, 0):
        out = jnp.where(p == pp, jnp.float32(2.0 ** -(2 * pp + hh + 1)), out)
    return out


def _swa_windows(ref, i, tq):
    nsub = tq // WINDOW
    cur = ref[pl.ds(pl.multiple_of(i * tq, tq), tq), :].reshape(nsub, WINDOW, LANES)
    first = ref[pl.ds(pl.multiple_of(jnp.maximum(i * tq - WINDOW, 0), WINDOW), WINDOW), :].reshape(1, WINDOW, LANES)
    return jnp.concatenate([jnp.concatenate([first, cur[0:nsub - 1]], axis=0), cur], axis=1)


def _both_heads(x3, lo):
    zero = jnp.zeros_like(x3)
    return jnp.concatenate([jnp.where(lo, x3, zero), jnp.where(lo, zero, x3)], axis=0)


def _swa_sinks(sink_ref, p, nsub):
    is_a = lax.broadcasted_iota(jnp.int32, (2 * nsub, 1, 1), 0) < nsub
    sinks = sink_ref[...]
    return jnp.where(is_a, _pick_lane(sinks, 2 * p).reshape(1, 1, 1), _pick_lane(sinks, 2 * p + 1).reshape(1, 1, 1))


def _swa_bias(p, i, nsub, keys_first):
    shape = (1, 2 * WINDOW, WINDOW) if keys_first else (1, WINDOW, 2 * WINDOW)
    qi = lax.broadcasted_iota(jnp.int32, shape, 2 if keys_first else 1)
    ki = lax.broadcasted_iota(jnp.int32, shape, 1 if keys_first else 2)
    dist = qi + WINDOW - ki
    band = (dist >= 0) & (dist < WINDOW)
    tiles = []
    for hh in range(2):
        bias = jnp.where(band, -_slope(p, hh) * dist.astype(F32), NEG_INF)
        tiles += [jnp.where((i == 0) & (ki < WINDOW), NEG_INF, bias)] + [bias] * (nsub - 1)
    return jnp.concatenate(tiles, axis=0)


def _swa_fwd(qa, kae, vae, sink_row, nb, s, tq):
    t = qa.shape[0]
    nq = s // tq
    nsub = tq // WINDOW

    def body(q_ref, k_ref, v_ref, sink_ref, o_ref, lse_ref):
        p, i = pl.program_id(1), pl.program_id(2)
        lo = _lane((1, 1, LANES)) < 64
        kk, vv = _swa_windows(k_ref, i, tq), _swa_windows(v_ref, i, tq)
        qs = (q_ref[...].astype(F32) * SCALE).astype(BF16).reshape(nsub, WINDOW, LANES)
        q8 = _both_heads(qs, lo)
        s8 = jnp.einsum("bqd,bkd->bqk", q8, jnp.concatenate([kk, kk], axis=0), preferred_element_type=F32)
        sink = _swa_sinks(sink_ref, p, nsub)
        s8 = s8 + _swa_bias(p, i, nsub, False)
        m = jnp.maximum(jnp.max(s8, axis=2, keepdims=True), sink)
        e = jnp.exp(s8 - m)
        den = jnp.sum(e, axis=2, keepdims=True) + jnp.exp(sink - m)
        pr = (e * (1.0 / den)).astype(BF16)
        o8 = jnp.einsum("bqk,bkd->bqd", pr, jnp.concatenate([vv, vv], axis=0), preferred_element_type=F32)
        lse8 = m + jnp.log(den)
        o_ref[...] = jnp.where(lo, o8[0:nsub], o8[nsub:]).astype(BF16).reshape(tq, LANES)
        lse_ref[...] = jnp.where(lo, lse8[0:nsub], lse8[nsub:]).reshape(tq, LANES)

    return pl.pallas_call(
        body, name="swa_fwd", grid=(nb, 4, nq),
        in_specs=[pl.BlockSpec((tq, LANES), lambda b, p, i: (b * nq + i, p)),
                  pl.BlockSpec((s, LANES), lambda b, p, i: (b, lax.shift_right_logical(p, 1))),
                  pl.BlockSpec((s, LANES), lambda b, p, i: (b, lax.shift_right_logical(p, 1))),
                  pl.BlockSpec((1, LANES), lambda b, p, i: (0, 0))],
        out_specs=[pl.BlockSpec((tq, LANES), lambda b, p, i: (b * nq + i, p)),
                   pl.BlockSpec((None, tq, LANES), lambda b, p, i: (p, b * nq + i, 0))],
        out_shape=[jax.ShapeDtypeStruct((t, 512), BF16), jax.ShapeDtypeStruct((4, t, LANES), F32)],
        compiler_params=_params(("arbitrary", "arbitrary", "arbitrary")),
    )(qa, kae, vae, sink_row)


def _swa_bwd(qa, kae, vae, do_a, sink_row, lse, delta, nb, s, tq):
    t = qa.shape[0]
    nq = s // tq
    nsub = tq // WINDOW

    def body(q_ref, do_ref, k_ref, v_ref, sink_ref, lse_ref, dl_ref, dq_ref, dk_ref, dv_ref, ds_ref):
        p, i = pl.program_id(1), pl.program_id(2)

        @pl.when(i == 0)
        def _():
            ds_ref[...] = jnp.zeros_like(ds_ref)

        lo = _lane((1, 1, LANES)) < 64
        kk, vv = _swa_windows(k_ref, i, tq), _swa_windows(v_ref, i, tq)
        kks = (kk.astype(F32) * SCALE).astype(BF16)
        k8, v8 = jnp.concatenate([kks, kks], axis=0), jnp.concatenate([vv, vv], axis=0)
        q8 = _both_heads(q_ref[...].reshape(nsub, WINDOW, LANES), lo)
        do8 = _both_heads(do_ref[...].reshape(nsub, WINDOW, LANES), lo)
        cur = pl.multiple_of(i * tq, tq)
        sub = lax.broadcasted_iota(jnp.int32, (WINDOW, WINDOW), 0)
        lse_t = [lse_ref[u * WINDOW:(u + 1) * WINDOW, :].T for u in range(nsub)]
        dl_t = [dl_ref[u * WINDOW:(u + 1) * WINDOW, :].T for u in range(nsub)]
        lse8 = jnp.concatenate([t_[64 * hh:64 * hh + 1, :].reshape(1, 1, WINDOW) for hh in range(2) for t_ in lse_t], axis=0)
        dl8 = jnp.concatenate([jnp.sum(jnp.where(sub == 2 * p + hh, t_, 0.0), axis=0, keepdims=True).reshape(1, 1, WINDOW)
                               for hh in range(2) for t_ in dl_t], axis=0)
        sink = _swa_sinks(sink_ref, p, nsub)
        st = jnp.einsum("bkd,bqd->bkq", k8, q8, preferred_element_type=F32) + _swa_bias(p, i, nsub, True) - lse8
        pt = jnp.exp(st)
        dpt = jnp.einsum("bkd,bqd->bkq", v8, do8, preferred_element_type=F32)
        dst = pt * (dpt - dl8)
        ptb, dstb = pt.astype(BF16), dst.astype(BF16)
        dv8 = jnp.einsum("bkq,bqd->bkd", ptb, do8, preferred_element_type=F32)
        dk8 = jnp.einsum("bkq,bqd->bkd", dstb, q8, preferred_element_type=F32) * SCALE
        dq8 = jnp.einsum("bkq,bkd->bqd", dstb, k8, preferred_element_type=F32)
        dq_ref[...] = jnp.where(lo, dq8[0:nsub], dq8[nsub:]).reshape(tq, LANES)

        psd = jnp.exp(sink - lse8) * dl8
        row_h = lax.broadcasted_iota(jnp.int32, (8, LANES), 0)
        for hh in range(2):
            tot = jnp.sum(jnp.sum(psd[hh * nsub:(hh + 1) * nsub], axis=2, keepdims=True), axis=0, keepdims=True)
            ds_ref[...] += jnp.where(row_h == hh, -tot.reshape(1, 1), 0.0)

        prev = pl.multiple_of(jnp.maximum(i * tq - WINDOW, 0), WINDOW)
        for g8, g_ref in ((dk8, dk_ref), (dv8, dv_ref)):
            g4 = g8[0:nsub] + g8[nsub:]
            own, before = g4[:, WINDOW:, :], g4[:, 0:WINDOW, :]
            shifted = jnp.concatenate([before[1:nsub], jnp.zeros((1, WINDOW, LANES), F32)], axis=0)
            g_ref[pl.ds(cur, tq), :] = (own + shifted).reshape(tq, LANES)
            g_ref[pl.ds(prev, WINDOW), :] += before[0]

    return pl.pallas_call(
        body, name="swa_bwd", grid=(nb, 4, nq),
        in_specs=[pl.BlockSpec((tq, LANES), lambda b, p, i: (b * nq + i, p)),
                  pl.BlockSpec((tq, LANES), lambda b, p, i: (b * nq + i, p)),
                  pl.BlockSpec((s, LANES), lambda b, p, i: (b, lax.shift_right_logical(p, 1))),
                  pl.BlockSpec((s, LANES), lambda b, p, i: (b, lax.shift_right_logical(p, 1))),
                  pl.BlockSpec((1, LANES), lambda b, p, i: (0, 0)),
                  pl.BlockSpec((None, tq, LANES), lambda b, p, i: (p, b * nq + i, 0)),
                  pl.BlockSpec((tq, LANES), lambda b, p, i: (b * nq + i, 0))],
        out_specs=[pl.BlockSpec((tq, LANES), lambda b, p, i: (b * nq + i, p)),
                   pl.BlockSpec((s, LANES), lambda b, p, i: (b, p)),
                   pl.BlockSpec((s, LANES), lambda b, p, i: (b, p)),
                   pl.BlockSpec((None, None, 8, LANES), lambda b, p, i: (b, p, 0, 0))],
        out_shape=[jax.ShapeDtypeStruct((t, 512), F32), jax.ShapeDtypeStruct((t, 512), F32),
                   jax.ShapeDtypeStruct((t, 512), F32), jax.ShapeDtypeStruct((nb, 4, 8, LANES), F32)],
        compiler_params=_params(("arbitrary", "arbitrary", "arbitrary")),
    )(qa, do_a, kae, vae, sink_row, lse, delta)


MESH = pl.DeviceIdType.MESH
ANY = pl.BlockSpec(memory_space=pl.ANY)
N_SEM = 7


def _gather_steps(pairs, send_sems, recv_sems, local_sems):
    x, y, c = lax.axis_index("x"), lax.axis_index("y"), lax.axis_index("c")
    me, sibling = (x, y, c), (x, y, 1 - c)
    chips = [(1 - x, y), (x, 1 - y), (1 - x, 1 - y)]
    mine, first, passed, landed, last = [], [], [], [], []
    for a, (x_ref, out_ref) in enumerate(pairs):
        def slot(px, py, pc, out_ref=out_ref):
            return out_ref.at[4 * px + 2 * py + pc]

        def copy(k, block, to, src=None, a=a, slot=slot):
            return pltpu.make_async_remote_copy(
                src_ref=slot(*block) if src is None else src, dst_ref=slot(*block),
                send_sem=send_sems.at[N_SEM * a + k], recv_sem=recv_sems.at[N_SEM * a + k], device_id=to, device_id_type=MESH)

        mine.append(pltpu.make_async_copy(x_ref, slot(*me), local_sems.at[a]))
        first += [copy(0, me, sibling, src=x_ref)] + [copy(1 + j, me, (*chip, c), src=x_ref) for j, chip in enumerate(chips)]
        passed += [copy(4 + j, (*chip, c), sibling) for j, chip in enumerate(chips)]
        landed += [copy(1 + j, (*chip, c), me) for j, chip in enumerate(chips)]
        last += [copy(0, sibling, me)] + [copy(4 + j, (*chip, 1 - c), me) for j, chip in enumerate(chips)]

    def start():
        for cp in mine + first:
            cp.start()

    def forward():
        for arrived, onward in zip(landed, passed):
            arrived.wait_recv()
            onward.start()

    def finish():
        for cp in last:
            cp.wait_recv()
        for cp in first + passed:
            cp.wait_send()
        for cp in mine:
            cp.wait()

    return start, forward, finish


def _exchange_steps(pairs, send_sems, recv_sems, local_sems):
    x, y, c = lax.axis_index("x"), lax.axis_index("y"), lax.axis_index("c")
    my_id = 4 * x + 2 * y + c
    local, remote = [], []
    for a, (src, dst) in enumerate(pairs):
        local.append(pltpu.make_async_copy(src.at[my_id], dst.at[my_id], local_sems.at[a]))
        for k in range(1, N_DEV):
            px = 1 - x if k & 4 else x
            py = 1 - y if k & 2 else y
            pc = 1 - c if k & 1 else c
            remote.append(pltpu.make_async_remote_copy(
                src_ref=src.at[4 * px + 2 * py + pc], dst_ref=dst.at[my_id],
                send_sem=send_sems.at[N_SEM * a + k - 1], recv_sem=recv_sems.at[N_SEM * a + k - 1],
                device_id=(px, py, pc), device_id_type=MESH))

    def start():
        for cp in local + remote:
            cp.start()

    def finish():
        for cp in remote:
            cp.wait_recv()
        for cp in remote:
            cp.wait_send()
        for cp in local:
            cp.wait()

    return start, finish


L_ONE = 64
L_CK = 65
L_CQ = 68
L_LSE = 71
L_DELTA = 74


def _head_block(pair, half):
    y = pair if half == 0 else pltpu.roll(pair, 64, axis=1)
    return jnp.where(_lane(pair.shape) < 64, y, 0.0)


def _put3(blk, lane0, col):
    lane = _lane(blk.shape)
    hi = col.astype(BF16).astype(F32)
    mid = (col - hi).astype(BF16).astype(F32)
    lo = (col - hi - mid).astype(BF16).astype(F32)
    return jnp.where(lane == lane0, hi, jnp.where(lane == lane0 + 1, mid, jnp.where(lane == lane0 + 2, lo, blk)))


def _spread3(col, shape, lane0s):
    lane = _lane(shape)
    hi = col.astype(BF16).astype(F32)
    mid = (col - hi).astype(BF16).astype(F32)
    lo = (col - hi - mid).astype(BF16).astype(F32)

    def at(k):
        return functools.reduce(jnp.logical_or, [lane == ln + k for ln in lane0s])

    return jnp.where(at(0), hi, jnp.where(at(1), mid, jnp.where(at(2), lo, 0.0)))


def _put_ones(blk, lanes):
    lane = _lane(blk.shape)
    hit = functools.reduce(jnp.logical_or, [lane == ln for ln in lanes])
    return jnp.where(hit, 1.0, blk)


def _to_pairs(ref):
    out = []
    for j in range(4):
        a, b = ref[:, 2 * LANES * j:2 * LANES * j + LANES], ref[:, 2 * LANES * j + LANES:2 * LANES * (j + 1)]
        out.append(jnp.where(_lane(a.shape) < 64, a, pltpu.roll(b, 64, axis=1)))
    return jnp.concatenate(out, axis=1)


def _fox_fwd(q_aug, k_aug, v_aug, nb, s, bt, shards=()):
    t = q_aug.shape[0]
    nq = s // bt
    n_in, n_sh = 3, len(shards)

    def body(*refs):
        q_ref, k_ref, v_ref = refs[:n_in]
        o_ref, ql_ref = refs[n_in + n_sh:n_in + n_sh + 2]
        if shards:
            srcs, dsts = refs[n_in:n_in + n_sh], refs[n_in + n_sh + 2:n_in + 2 * n_sh + 2]
            start, forward, finish = _gather_steps(list(zip(srcs, dsts)), *refs[n_in + 2 * n_sh + 2:])
            step = (pl.program_id(0) * 4 + pl.program_id(1)) * nq + pl.program_id(2)
            pl.when(step == 0)(start)
            pl.when(step == nb * 3 * nq)(forward)
        i = pl.program_id(2)
        sls = [slice(LANES * hh, LANES * (hh + 1)) for hh in range(2)]
        qhs = [q_ref[:, sl] for sl in sls]

        def update(m, acc, qrows, start, size, sl, causal):
            sc = _nt(qrows, k_ref[pl.ds(start, size), sl])
            if causal:
                row = lax.broadcasted_iota(jnp.int32, sc.shape, 0)
                col = lax.broadcasted_iota(jnp.int32, sc.shape, 1)
                sc = jnp.where(row >= col, sc, NEG_INF)
            m_new = jnp.maximum(m, jnp.max(sc, axis=1, keepdims=True))
            pr = jnp.exp2(sc - m_new).astype(BF16)
            acc = jnp.exp2(m - m_new) * acc + jnp.dot(pr, v_ref[pl.ds(start, size), sl], preferred_element_type=F32)
            return m_new, acc

        def blk(kb_i, carry):
            start = pl.multiple_of(kb_i * bt, bt)
            return tuple(update(m, acc, qh, start, bt, sl, False) for (m, acc), qh, sl in zip(carry, qhs, sls))

        def diag_blk(carry):
            start = pl.multiple_of(i * bt, bt)
            return tuple(update(m, acc, qh, start, bt, sl, True) for (m, acc), qh, sl in zip(carry, qhs, sls))

        init = tuple((jnp.full((bt, 1), NEG_INF, F32), jnp.zeros((bt, LANES), F32)) for _ in range(2))
        carry = lax.fori_loop(0, i, blk, init)
        outs = []
        for (m, acc), qh, sl in zip(diag_blk(carry), qhs, sls):
            l = acc[:, L_ONE:L_ONE + 1]
            outs.append(acc * (1.0 / l))
            ql_ref[:, sl] = _put3(qh.astype(F32), L_LSE, -(m + jnp.log(l) * LOG2E)).astype(BF16)
        o_ref[...] = jnp.where(_lane((1, LANES)) < 64, outs[0], pltpu.roll(outs[1], 64, axis=1)).astype(BF16)
        if shards:
            pl.when(step == nb * 4 * nq - 1)(finish)

    in_specs = [pl.BlockSpec((bt, 2 * LANES), lambda b, j, i: (b * nq + i, j)),
                pl.BlockSpec((s, 2 * LANES), lambda b, j, i: (b, j)),
                pl.BlockSpec((s, 2 * LANES), lambda b, j, i: (b, j))]
    out_specs = [pl.BlockSpec((bt, LANES), lambda b, j, i: (b * nq + i, j)),
                 pl.BlockSpec((bt, 2 * LANES), lambda b, j, i: (b * nq + i, j))]
    out_shape = [jax.ShapeDtypeStruct((t, 512), BF16), jax.ShapeDtypeStruct((t, 8 * LANES), BF16)]
    args, scratch = [q_aug, k_aug, v_aug, *shards], []
    if shards:
        in_specs += [ANY] * n_sh
        out_specs += [ANY] * n_sh
        out_shape += [jax.ShapeDtypeStruct((N_DEV,) + sh.shape, sh.dtype) for sh in shards]
        scratch = [pltpu.SemaphoreType.DMA((N_SEM * n_sh,)), pltpu.SemaphoreType.DMA((N_SEM * n_sh,)),
                   pltpu.SemaphoreType.DMA((n_sh,))]
    return pl.pallas_call(
        body, name="fox_fwd", grid=(nb, 4, nq), in_specs=in_specs, out_specs=out_specs, out_shape=out_shape,
        scratch_shapes=scratch, compiler_params=_params(("arbitrary", "arbitrary", "arbitrary")),
    )(*args)


def _fox_bwd(ql_aug, k_aug, v_aug, do_aug, nb, s, bt, exch=()):
    t = ql_aug.shape[0]
    nk = s // bt
    n_in, n_out, n_ex = 4, 3, len(exch)

    def body(*refs):
        q_ref, do_ref, k_ref, v_ref = refs[:n_in]
        dq_ref, dk_ref, dv_ref = refs[n_in + n_ex:n_in + n_ex + n_out]
        if exch:
            srcs = refs[n_in:n_in + n_ex]
            dsts = refs[n_in + n_ex + n_out:n_in + 2 * n_ex + n_out]
            start, finish = _exchange_steps(list(zip(srcs, dsts)), *refs[n_in + 2 * n_ex + n_out:])
            step = (pl.program_id(0) * 4 + pl.program_id(1)) * nk + pl.program_id(2)
            pl.when(step == 0)(start)
        kb_i = pl.program_id(2)

        @pl.when(kb_i == 0)
        def _():
            dq_ref[...] = jnp.zeros_like(dq_ref)

        row = lax.broadcasted_iota(jnp.int32, (bt, bt), 0)
        col = lax.broadcasted_iota(jnp.int32, (bt, bt), 1)
        sls = [slice(LANES * hh, LANES * (hh + 1)) for hh in range(2)]
        khs, vhs = [k_ref[:, sl] for sl in sls], [v_ref[:, sl] for sl in sls]

        def blk(qi, carry, diag):
            start = pl.multiple_of(qi * bt, bt)
            new = []
            for (dk_a, dv_a), kh, vh, sl in zip(carry, khs, vhs, sls):
                qblk, doblk = q_ref[pl.ds(start, bt), sl], do_ref[pl.ds(start, bt), sl]
                st = _nt(kh, qblk)
                if diag:
                    pt = jnp.where(col >= row, jnp.exp2(jnp.where(col >= row, st, 0.0)), 0.0)
                else:
                    pt = jnp.exp2(st)
                dst = pt * _nt(vh, doblk)
                ptb, dstb = pt.astype(BF16), dst.astype(BF16)
                dv_a = dv_a + jnp.dot(ptb, doblk, preferred_element_type=F32)
                dk_a = dk_a + jnp.dot(dstb, qblk, preferred_element_type=F32)
                dq_ref[pl.ds(start, bt), sl] += _tn(dstb, kh)
                new.append((dk_a, dv_a))
            return tuple(new)

        zero = jnp.zeros((bt, LANES), F32)
        carry = blk(kb_i, ((zero, zero), (zero, zero)), True)
        carry = lax.fori_loop(kb_i + 1, nk, lambda qi, c: blk(qi, c, False), carry)
        for (dk_acc, dv_acc), sl in zip(carry, sls):
            dk_ref[:, sl] = dk_acc
            dv_ref[:, sl] = dv_acc
        if exch:
            pl.when(step == nb * 4 * nk - 1)(finish)

    scratch = []
    if exch:
        scratch = [pltpu.SemaphoreType.DMA((N_SEM * n_ex,)), pltpu.SemaphoreType.DMA((N_SEM * n_ex,)),
                   pltpu.SemaphoreType.DMA((n_ex,))]
    whole = pl.BlockSpec((s, 2 * LANES), lambda b, j, kb_i: (b, j))
    tile = pl.BlockSpec((bt, 2 * LANES), lambda b, j, kb_i: (b * nk + kb_i, j))
    shp = jax.ShapeDtypeStruct((t, 8 * LANES), F32)
    return pl.pallas_call(
        body, name="fox_bwd", grid=(nb, 4, nk),
        in_specs=[whole, whole, tile, tile] + [ANY] * n_ex,
        out_specs=[whole, tile, tile] + [ANY] * n_ex,
        out_shape=[shp, shp, shp] + [jax.ShapeDtypeStruct(e.shape, e.dtype) for e in exch],
        scratch_shapes=scratch, compiler_params=_params(("arbitrary", "arbitrary", "arbitrary")),
    )(ql_aug, do_aug, k_aug, v_aug, *exch)


FF_BLK = D_FF // N_DEV


def _mlp_fwd(x2, ma, mb, tgt, w_out, g2, w_up, w_down, tm):
    t = x2.shape[0]

    def body(x_ref, ma_ref, mb_ref, tg_ref, wo_ref, g2_ref, wu_ref, wd_ref,
             h_ref, hn_ref, hid_ref, dy_ref, dyb_ref, loss_ref):
        @pl.when(pl.program_id(0) == 0)
        def _():
            loss_ref[...] = jnp.zeros_like(loss_ref)

        h = (x_ref[...] + jnp.dot(ma_ref[...], wo_ref[0:512, :], preferred_element_type=F32)
             + jnp.dot(mb_ref[...], wo_ref[512:1024, :], preferred_element_type=F32))
        h_ref[...] = h
        r = lax.rsqrt(jnp.mean(h * h, axis=-1, keepdims=True) + EPS)
        hn = (h * r * g2_ref[...]).astype(BF16)
        hn_ref[...] = hn
        for d in range(N_DEV):
            u = jnp.maximum(jnp.dot(hn, wu_ref[d], preferred_element_type=F32), 0.0)
            hid_ref[:, FF_BLK * d:FF_BLK * (d + 1)] = (u * u).astype(BF16)
        y = h + jnp.dot(hid_ref[...], wd_ref[...], preferred_element_type=F32)
        err = y - tg_ref[...]
        dy = err * (1.0 / D_MODEL)
        dy_ref[...] = dy
        dyb_ref[...] = dy.astype(BF16)
        part =0.5 * jnp.sum(jnp.sum(err * err, axis=1, keepdims=True) * (1.0 / D_MODEL), axis=0, keepdims=True)
        loss_ref[...] += part

    def tile(w):
        return pl.BlockSpec((tm, w), lambda i: (i, 0))

    return pl.pallas_call(
        body, name="mlp_fwd", grid=(t // tm,),
        in_specs=[tile(D_MODEL), tile(512), tile(512), tile(D_MODEL), _const_spec((D_MODEL, D_MODEL)),
                  _const_spec((1, D_MODEL)), _const_spec((N_DEV, D_MODEL, FF_BLK)), _const_spec((D_FF, D_MODEL))],
        out_specs=[tile(D_MODEL), tile(D_MODEL), tile(D_FF), tile(D_MODEL), tile(D_MODEL),
                   pl.BlockSpec((8, LANES), lambda i: (0, 0))],
        out_shape=[jax.ShapeDtypeStruct((t, D_MODEL), F32), jax.ShapeDtypeStruct((t, D_MODEL), BF16),
                   jax.ShapeDtypeStruct((t, D_FF), BF16), jax.ShapeDtypeStruct((t, D_MODEL), F32),
                   jax.ShapeDtypeStruct((t, D_MODEL), BF16), jax.ShapeDtypeStruct((8, LANES), F32)],
        compiler_params=_params(("arbitrary",)),
    )(x2, ma, mb, tgt, w_out, g2, w_up, w_down)


def _mlp_bwd(dy, hid, h, ma, mb, w_down, w_up_t, w_out, g2, tm):
    t = dy.shape[0]

    def body(dy_ref, hid_ref, h_ref, ma_ref, mb_ref, wd_ref, wut_ref, wo_ref, g2_ref,
             du_ref, dh_ref, dhb_ref, dma_ref, dob_ref, dla_ref, gg_ref):
        @pl.when(pl.program_id(0) == 0)
        def _():
            gg_ref[...] = jnp.zeros_like(gg_ref)

        dy = dy_ref[...]
        d_hid = _nt(dy.astype(BF16), wd_ref[...])
        du = (d_hid * (2.0 * jnp.sqrt(hid_ref[...].astype(F32)))).astype(BF16)
        du_ref[...] = du
        d_hn = jnp.dot(du, wut_ref[...], preferred_element_type=F32)
        h = h_ref[...]
        r = lax.rsqrt(jnp.mean(h * h, axis=-1, keepdims=True) + EPS)
        hat = h * r
        gd = d_hn * g2_ref[...]
        dh = dy + r * (gd - hat * jnp.mean(gd * hat, axis=-1, keepdims=True))
        gg_ref[...] += jnp.sum(d_hn * hat, axis=0, keepdims=True)
        dh_ref[...] = dh
        dhb = dh.astype(BF16)
        dhb_ref[...] = dhb
        dm = _nt(dhb, wo_ref[...]).astype(BF16)
        dma, dmb = dm[:, 0:512], dm[:, 512:1024]
        dma_ref[...] = dma
        sel = (lax.shift_right_logical(lax.broadcasted_iota(jnp.int32, (512, LANES), 0), 6)
               == lax.broadcasted_iota(jnp.int32, (512, LANES), 1)).astype(BF16)
        dla_ref[...] = jnp.dot((dma.astype(F32) * ma_ref[...].astype(F32)).astype(BF16), sel, preferred_element_type=F32)
        dmb32 = dmb.astype(F32)
        dlb = jnp.dot((dmb32 * mb_ref[...].astype(F32)).astype(BF16), sel, preferred_element_type=F32)
        for hd in range(8):
            blk = _head_block(dmb32[:, LANES * (hd // 2):LANES * (hd // 2 + 1)], hd % 2)
            dob_ref[:, LANES * hd:LANES * (hd + 1)] = _put3(blk, L_DELTA, -dlb[:, hd:hd + 1]).astype(BF16)

    def tile(w):
        return pl.BlockSpec((tm, w), lambda i: (i, 0))

    return pl.pallas_call(
        body, name="mlp_bwd", grid=(t // tm,),
        in_specs=[tile(D_MODEL), tile(D_FF), tile(D_MODEL), tile(512), tile(512), _const_spec((D_FF, D_MODEL)),
                  _const_spec((D_FF, D_MODEL)), _const_spec((D_MODEL, D_MODEL)), _const_spec((1, D_MODEL))],
        out_specs=[tile(D_FF), tile(D_MODEL), tile(D_MODEL), tile(512), tile(8 * LANES), tile(LANES),
                   pl.BlockSpec((1, D_MODEL), lambda i: (0, 0))],
        out_shape=[jax.ShapeDtypeStruct((t, D_FF), BF16), jax.ShapeDtypeStruct((t, D_MODEL), F32),
                   jax.ShapeDtypeStruct((t, D_MODEL), BF16), jax.ShapeDtypeStruct((t, 512), BF16),
                   jax.ShapeDtypeStruct((t, 8 * LANES), BF16), jax.ShapeDtypeStruct((t, LANES), F32),
                   jax.ShapeDtypeStruct((1, D_MODEL), F32)],
        compiler_params=_params(("arbitrary",), VMEM_LIMIT_WIDE),
    )(dy, hid, h, ma, mb, w_down, w_up_t, w_out, g2)


def _wgrad(a, b, name, bm, bn, tk, out_dtype=F32, col_blocks=False, a2=None):
    t, m = a.shape
    n = b.shape[1]
    bm, bn = min(bm, m), min(bn, n)
    nk = t // tk

    def body(*refs):
        if a2 is None:
            a_ref, b_ref, o_ref, acc = refs
        else:
            a_ref, b_ref, a2_ref, o_ref, o2_ref, acc, acc2 = refs
        i, k = pl.program_id(0), pl.program_id(2)

        @pl.when(k == 0)
        def _():
            acc[...] = jnp.zeros_like(acc)

        acc[...] += _tn(a_ref[...], b_ref[...])

        @pl.when(k == nk - 1)
        def _():
            o_ref[...] = acc[...].astype(out_dtype)

        if a2 is not None:
            @pl.when((i == 0) & (k == 0))
            def _():
                acc2[...] = jnp.zeros_like(acc2)

            @pl.when(i == 0)
            def _():
                acc2[...] += _tn(a2_ref[...], b_ref[...])

            @pl.when((i == 0) & (k == nk - 1))
            def _():
                o2_ref[...] = acc2[...]

    if col_blocks:
        out_spec = pl.BlockSpec((None, bm, bn), lambda i, j, k: (j, i, 0))
        out_shape = jax.ShapeDtypeStruct((n // bn, m, bn), out_dtype)
    else:
        out_spec = pl.BlockSpec((bm, bn), lambda i, j, k: (i, j))
        out_shape = jax.ShapeDtypeStruct((m, n), out_dtype)
    in_specs = [pl.BlockSpec((tk, bm), lambda i, j, k: (k, i)), pl.BlockSpec((tk, bn), lambda i, j, k: (k, j))]
    out_specs, out_shapes, scratch, args = [out_spec], [out_shape], [pltpu.VMEM((bm, bn), F32)], [a, b]
    if a2 is not None:
        m2 = a2.shape[1]
        in_specs.append(pl.BlockSpec((tk, m2), lambda i, j, k: (k, 0)))
        out_specs.append(pl.BlockSpec((m2, n), lambda i, j, k: (0, 0)))
        out_shapes.append(jax.ShapeDtypeStruct((m2, n), F32))
        scratch.append(pltpu.VMEM((m2, n), F32))
        args.append(a2)
    out = pl.pallas_call(
        body, name=name, grid=(m // bm, n // bn, nk), in_specs=in_specs, out_specs=out_specs, out_shape=out_shapes,
        scratch_shapes=scratch, compiler_params=_params(("arbitrary", "arbitrary", "arbitrary")),
    )(*args)
    return out[0] if a2 is None else out


def _proj_bwd(raw, dqa, dkae, dvae, dqb, dkb, dvb, fl, bf_row, x2, dh, w_main_t, w_f_t, g1, gqa, gka, gqb, gkb, nb, s, tm):
    t = x2.shape[0]
    nt = s // tm

    def body(raw_ref, dqa_ref, dkae_ref, dvae_ref, dqb_ref, dkb_ref, dvb_ref, fl_ref, b_ref, x_ref, dh_ref,
             wmt_ref, wft_ref, g1_ref, gqa_ref, gka_ref, gqb_ref, gkb_ref,
             dx_ref, dp_ref, dfb_ref, ggqa_ref, ggka_ref, ggqb_ref, ggkb_ref, gg1_ref, gb_ref, carry, dlf_ref):
        @pl.when((pl.program_id(0) == 0) & (pl.program_id(1) == 0))
        def _():
            for r in (ggqa_ref, ggka_ref, ggqb_ref, ggkb_ref, gg1_ref, gb_ref):
                r[...] = jnp.zeros_like(r)

        @pl.when(pl.program_id(1) == 0)
        def _():
            carry[...] = jnp.zeros_like(carry)

        lane = _lane((tm, LANES))
        dc = jnp.zeros((tm, LANES), F32)
        for hd in range(8):
            col = (dqb_ref[:, LANES * hd + L_CQ:LANES * hd + L_CQ + 1] - dkb_ref[:, LANES * hd + L_CK:LANES * hd + L_CK + 1])
            dc = jnp.where(lane == hd, col, dc)
        dlf_ref[...] = _tri_dot(tm, True, dc) + carry[...]
        carry[...] = dlf_ref[pl.ds(0, 1), :]
        dfl = dlf_ref[...] * (1.0 / (1.0 + jnp.exp(fl_ref[...] + b_ref[...])))
        gb_ref[...] += jnp.sum(dfl, axis=0, keepdims=True)

        raw = raw_ref[...]
        d_qa, p_qa = _head_norm_bwd(raw[:, 0:512], gqa_ref[...], dqa_ref[...])
        d_ka, p_ka = _head_norm_bwd(raw[:, 512:640], gka_ref[...], _fold_kv(dkae_ref[...]))
        d_va = _fold_kv(dvae_ref[...])
        d_qb, p_qb = _head_norm_bwd(raw[:, 768:1280], gqb_ref[...], _to_pairs(dqb_ref) * SCALE)
        d_kb, p_kb = _head_norm_bwd(raw[:, 1280:1792], gkb_ref[...], _to_pairs(dkb_ref) * (1.0 / LOG2E))
        ggqa_ref[...] += jnp.sum(p_qa, axis=0, keepdims=True)
        ggka_ref[...] += jnp.sum(p_ka, axis=0, keepdims=True)
        ggqb_ref[...] += jnp.sum(p_qb, axis=0, keepdims=True)
        ggkb_ref[...] += jnp.sum(p_kb, axis=0, keepdims=True)
        dproj = jnp.concatenate([d_qa, d_ka, d_va, d_qb, d_kb, _to_pairs(dvb_ref)], axis=1).astype(BF16)
        dp_ref[...] = dproj
        dfb = dfl.astype(BF16)
        dfb_ref[...] = dfb
        d_xn = (jnp.dot(dproj, wmt_ref[...], preferred_element_type=F32)
                + jnp.dot(dfb, wft_ref[...], preferred_element_type=F32))
        x = x_ref[...]
        r = lax.rsqrt(jnp.mean(x * x, axis=-1, keepdims=True) + EPS)
        hat = x * r
        gd = d_xn * g1_ref[...]
        dx_ref[...] = dh_ref[...] + r * (gd - hat * jnp.mean(gd * hat, axis=-1, keepdims=True))
        gg1_ref[...] += jnp.sum(d_xn * hat, axis=0, keepdims=True)

    def tile(w):
        return pl.BlockSpec((tm, w), lambda b, i: (b * nt + (nt - 1 - i), 0))

    def acc(w):
        return pl.BlockSpec((1, w), lambda b, i: (0, 0))

    return pl.pallas_call(
        body, name="proj_bwd", grid=(nb, nt),
        in_specs=[tile(MAIN_W), tile(512), tile(512), tile(512), tile(8 * LANES), tile(8 * LANES), tile(8 * LANES), tile(LANES),
                  _const_spec((1, LANES)), tile(D_MODEL), tile(D_MODEL), _const_spec((MAIN_W, D_MODEL)),
                  _const_spec((LANES, D_MODEL)), _const_spec((1, D_MODEL)), _const_spec((1, 512)), _const_spec((1, 128)),
                  _const_spec((1, 512)), _const_spec((1, 512))],
        out_specs=[tile(D_MODEL), tile(MAIN_W), tile(LANES), acc(512), acc(128), acc(512), acc(512), acc(D_MODEL), acc(LANES)],
        out_shape=[jax.ShapeDtypeStruct((t, D_MODEL), F32), jax.ShapeDtypeStruct((t, MAIN_W), BF16),
                   jax.ShapeDtypeStruct((t, LANES), BF16), jax.ShapeDtypeStruct((1, 512), F32),
                   jax.ShapeDtypeStruct((1, 128), F32), jax.ShapeDtypeStruct((1, 512), F32),
                   jax.ShapeDtypeStruct((1, 512), F32), jax.ShapeDtypeStruct((1, D_MODEL), F32),
                   jax.ShapeDtypeStruct((1, LANES), F32)],
        scratch_shapes=[pltpu.VMEM((1, LANES), F32), pltpu.VMEM((tm, LANES), F32)],
        compiler_params=_params(("arbitrary", "arbitrary"), VMEM_LIMIT_WIDE),
    )(raw, dqa, dkae, dvae, dqb, dkb, dvb, fl, bf_row, x2, dh, w_main_t, w_f_t, g1, gqa, gka, gqb, gkb)


IN_PAD = 304


def _local_step(x, tgt, w_in_t, rest, g1, b_forget, qna, kna, sinks, qnb, knb, g2,
                tm=512, bt=1024, btf=1024, tq=4096, wk=4096, wkb=8192, distributed=False):
    nb, s, _ = x.shape
    t = nb * s
    x2, tgt2 = x.reshape(t, D_MODEL), tgt.reshape(t, D_MODEL)
    g1r, g2r = g1.reshape(1, D_MODEL), g2.reshape(1, D_MODEL)
    gqa, gka = jnp.tile(qna, 8).reshape(1, 512), jnp.tile(kna, 2).reshape(1, 128)
    gqb, gkb = jnp.tile(qnb, 8).reshape(1, 512), jnp.tile(knb, 8).reshape(1, 512)
    bf_row = jnp.pad(b_forget, (0, LANES - 8)).reshape(1, LANES)
    sink_row = jnp.pad(sinks, (0, LANES - 8)).reshape(1, LANES)
    w_main_t = w_in_t[0:MAIN_W]
    w_f_t = jnp.pad(w_in_t[MAIN_W:IN_W], ((0, LANES - 8), (0, 0)))

    xn = _xnorm(x2, g1r, 2 * tm)
    raw, fl, qa, kae, vae, q_aug, k_aug, v_aug = _norm_proj(xn, w_main_t, w_f_t, gqa, gka, gqb, gkb, bf_row, s, tm)
    ma, lse_a = _swa_fwd(qa, kae, vae, sink_row, nb, s, tq)
    if distributed:
        mb, ql_aug, w_out, w_up, w_down, w_up_t = _fox_fwd(q_aug, k_aug, v_aug, nb, s, btf, shards=rest)
    else:
        mb, ql_aug = _fox_fwd(q_aug, k_aug, v_aug, nb, s, btf)
        w_out, w_up, w_down, w_up_t = rest
    w_out, w_down = w_out.reshape(D_MODEL, D_MODEL), w_down.reshape(D_FF, D_MODEL)
    h, hn, hid, dy, dyb, loss_acc = _mlp_fwd(x2, ma, mb, tgt2, w_out, g2r, w_up, w_down, tm)

    du, dh, dhb, dma, do_aug, dla, gg2 = _mlp_bwd(dy, hid, h, ma, mb, w_down, w_up_t.reshape(D_FF, D_MODEL), w_out, g2r, tm)
    g_down = _wgrad(hid, dyb, "wgrad_down", 512, 1024, wkb, BF16).reshape(N_DEV, 512, D_MODEL)
    g_up = _wgrad(hn, du, "wgrad_up", 1024, 512, wkb, BF16, col_blocks=True)
    g_out = jnp.concatenate([_wgrad(ma, dhb, "wgrad_out_a", 512, 1024, wk, BF16),
                             _wgrad(mb, dhb, "wgrad_out_b", 512, 1024, wk, BF16)], axis=0).reshape(N_DEV, 128, D_MODEL)

    dqa, dkae, dvae, dsink = _swa_bwd(qa, kae, vae, dma, sink_row, lse_a, dla, nb, s, tq)
    fox = _fox_bwd(ql_aug, k_aug, v_aug, do_aug, nb, s, bt, exch=(g_out, g_up, g_down) if distributed else ())
    dqb, dkb, dvb = fox[:3]
    if distributed:
        g_out, g_up, g_down = fox[3:]
    grad_x, dproj, dfb, ggqa, ggka, ggqb, ggkb, gg1, gbf = _proj_bwd(
        raw, dqa, dkae, dvae, dqb, dkb, dvb, fl, bf_row, x2, dh, w_main_t, w_f_t, g1r, gqa, gka, gqb, gkb, nb, s, tm)
    g_main_t, g_gate_t = _wgrad(dproj, xn, "wgrad_in", 768, 1024, wk, a2=dfb)
    g_in_t = jnp.concatenate([g_main_t, g_gate_t[0:8]], axis=0)

    small = (gg1.reshape(D_MODEL), gbf[0, 0:8], ggqa.reshape(8, 64).sum(0), ggka.reshape(2, 64).sum(0),
             dsink.sum(0)[:, 0:2, 0].reshape(8), ggqb.reshape(8, 64).sum(0), ggkb.reshape(8, 64).sum(0),
             gg2.reshape(D_MODEL))
    return loss_acc[0, 0], grad_x.reshape(nb, s, D_MODEL), g_in_t, g_out, g_up, g_down, small


def _all_gather(shard):
    x_ref = jax.new_ref(shard, memory_space=pltpu.MemorySpace.HBM)
    out_ref = jax.empty_ref(jax.ShapeDtypeStruct((N_DEV,) + shard.shape, shard.dtype), memory_space=pltpu.MemorySpace.HBM)

    @pl.kernel(mesh=plsc.ScalarSubcoreMesh(axis_name="sequencer", num_cores=1), name="gather_w_in",
               scratch_types=(pltpu.SemaphoreType.DMA((N_SEM,)), pltpu.SemaphoreType.DMA((N_SEM,)), pltpu.SemaphoreType.DMA((1,))),
               compiler_params=pltpu.CompilerParams(collective_id=1))
    def launch(send_sems, recv_sems, local_sems):
        x, y, c = lax.axis_index("x"), lax.axis_index("y"), lax.axis_index("c")
        barrier = pltpu.get_barrier_semaphore()
        peers = [(x, y, 1 - c), (1 - x, y, c), (x, 1 - y, c), (1 - x, 1 - y, c)]
        for peer in peers:
            pl.semaphore_signal(barrier, inc=1, device_id=peer, device_id_type=MESH)
        pl.semaphore_wait(barrier, len(peers))
        start, forward, finish = _gather_steps([(x_ref, out_ref)], send_sems, recv_sems, local_sems)
        start()
        forward()
        finish()

    launch()
    return out_ref[...]


def _exchange(*arrays):
    n_ex = len(arrays)
    srcs = [jax.new_ref(a, memory_space=pltpu.MemorySpace.HBM) for a in arrays]
    dsts = [jax.empty_ref(jax.ShapeDtypeStruct(a.shape, a.dtype), memory_space=pltpu.MemorySpace.HBM) for a in arrays]

    @pl.kernel(mesh=plsc.ScalarSubcoreMesh(axis_name="sequencer", num_cores=1), name="exchange_tail",
               scratch_types=(pltpu.SemaphoreType.DMA((N_SEM * n_ex,)), pltpu.SemaphoreType.DMA((N_SEM * n_ex,)),
                              pltpu.SemaphoreType.DMA((n_ex,))),
               compiler_params=pltpu.CompilerParams(collective_id=0))
    def launch(send_sems, recv_sems, local_sems):
        x, y, c = lax.axis_index("x"), lax.axis_index("y"), lax.axis_index("c")
        barrier = pltpu.get_barrier_semaphore()
        for k in range(1, N_DEV):
            peer = (1 - x if k & 4 else x, 1 - y if k & 2 else y, 1 - c if k & 1 else c)
            pl.semaphore_signal(barrier, inc=1, device_id=peer, device_id_type=MESH)
        pl.semaphore_wait(barrier, N_DEV - 1)
        start, finish = _exchange_steps(list(zip(srcs, dsts)), send_sems, recv_sems, local_sems)
        start()
        finish()

    launch()
    return [d[...] for d in dsts]


def _sum_adamw(recv, w, m, v, tr, name):
    _, r, n = recv.shape

    def body(r_ref, w_ref, m_ref, v_ref, g_ref, d_ref, nm_ref, nv_ref):
        g = r_ref[0].astype(F32)
        for s in range(1, N_DEV):
            g = g + r_ref[s].astype(F32)
        g_ref[...] = g
        nm = ADAM_B1 * m_ref[...] + (1.0 - ADAM_B1) * g
        nv = ADAM_B2 * v_ref[...] + (1.0 - ADAM_B2) * (g * g)
        m_hat = nm / (1.0 - ADAM_B1 ** ADAM_STEP)
        v_hat = nv / (1.0 - ADAM_B2 ** ADAM_STEP)
        d_ref[...] = -ADAM_LR * (m_hat / (jnp.sqrt(v_hat) + ADAM_EPS) + ADAM_WD * w_ref[...])
        nm_ref[...] = nm
        nv_ref[...] = nv

    tile = pl.BlockSpec((tr, n), lambda i: (i, 0))
    shp = jax.ShapeDtypeStruct((r, n), F32)
    return pl.pallas_call(
        body, name=name, grid=(r // tr,),
        in_specs=[pl.BlockSpec((N_DEV, tr, n), lambda i: (0, i, 0)), tile, tile, tile],
        out_specs=[tile, tile, tile, tile], out_shape=[shp, shp, shp, shp],
        compiler_params=_params(("arbitrary",)),
    )(recv, w, m, v)


def _small_rows(g1, bf, qna, kna, sk, qnb, knb, g2, extra=None):
    row2 = jnp.concatenate([bf, qna, kna, sk, qnb, knb])
    rows = [g1, g2, jnp.pad(row2, (0, D_MODEL - row2.shape[0]))]
    if extra is not None:
        rows.append(jnp.pad(extra.reshape(1), (0, D_MODEL - 1)))
    return jnp.pad(jnp.stack(rows), ((0, 8 - len(rows)), (0, 0)))


def _in_rows(w_in_s):
    return jnp.pad(w_in_s.T, ((0, IN_PAD - IN_SHARD), (0, 0)))


def kernel(x, attn_norm_g, w_in, b_forget, q_norm_a, k_norm_a, sink_logits, q_norm_b, k_norm_b, w_out, mlp_norm_g, w_up, w_down, loss_target, m_attn_norm_g, m_w_in, m_b_forget, m_q_norm_a, m_k_norm_a, m_sink_logits, m_q_norm_b, m_k_norm_b, m_w_out, m_mlp_norm_g, m_w_up, m_w_down, v_attn_norm_g, v_w_in, v_b_forget, v_q_norm_a, v_k_norm_a, v_sink_logits, v_q_norm_b, v_k_norm_b, v_w_out, v_mlp_norm_g, v_w_up, v_w_down):
    w_in_r = _in_rows(w_in)
    w_in_t = _all_gather(w_in_r.astype(BF16))[:, 0:IN_SHARD].reshape(IN_W, D_MODEL)
    w_up_b = w_up.astype(BF16)
    rest = (w_out.astype(BF16), w_up_b, w_down.astype(BF16), w_up_b.T)

    loss_part, grad_x, g_in_t, r_out, r_up, r_down, small = _local_step(
        x, loss_target, w_in_t, rest, attn_norm_g, b_forget, q_norm_a, k_norm_a, sink_logits, q_norm_b, k_norm_b, mlp_norm_g,
        distributed=True)

    g_in_blocks = jnp.pad(g_in_t.reshape(N_DEV, IN_SHARD, D_MODEL), ((0, 0), (0, IN_PAD - IN_SHARD), (0, 0))).astype(BF16)
    small_blocks = jnp.broadcast_to(_small_rows(*small, extra=loss_part), (N_DEV, 8, D_MODEL))
    r_in, r_small = _exchange(g_in_blocks, small_blocks)

    small_w = _small_rows(attn_norm_g, b_forget, q_norm_a, k_norm_a, sink_logits, q_norm_b, k_norm_b, mlp_norm_g)
    small_m = _small_rows(m_attn_norm_g, m_b_forget, m_q_norm_a, m_k_norm_a, m_sink_logits, m_q_norm_b, m_k_norm_b, m_mlp_norm_g)
    small_v = _small_rows(v_attn_norm_g, v_b_forget, v_q_norm_a, v_k_norm_a, v_sink_logits, v_q_norm_b, v_k_norm_b, v_mlp_norm_g)
    o_in = [a[0:IN_SHARD].T for a in _sum_adamw(r_in, w_in_r, _in_rows(m_w_in), _in_rows(v_w_in), IN_PAD, "adamw_in")]
    o_out = _sum_adamw(r_out, w_out, m_w_out, v_w_out, 128, "adamw_out")
    o_up = _sum_adamw(r_up, w_up, m_w_up, v_w_up, 256, "adamw_up")
    o_down = _sum_adamw(r_down, w_down, m_w_down, v_w_down, 128, "adamw_down")
    o_small = _sum_adamw(r_small, small_w, small_m, small_v, 8, "adamw_small")

    def leaves(i):
        row2 = o_small[i][2]
        return (o_small[i][0], o_in[i], row2[0:8], row2[8:72], row2[72:136], row2[136:144], row2[144:208], row2[208:272],
                o_out[i], o_small[i][1], o_up[i], o_down[i])

    return (o_small[0][3, 0], grad_x, *leaves(0), *leaves(1), *leaves(2), *leaves(3))
```

```python
import functools

import jax
import jax.numpy as jnp
from jax import lax
from jax.experimental import pallas as pl
from jax.experimental.pallas import tpu as pltpu
from jax.experimental.pallas import tpu_sc as plsc

F32 = jnp.float32
BF16 = jnp.bfloat16

D_MODEL = 1024
HEAD_DIM = 64
N_DEV = 8
D_FF = 4096
MAIN_W = 2304
IN_W = 2312
IN_SHARD = 289
WINDOW = 128
EPS = 1e-6
SCALE = 0.125
LOG2E = 1.4426950408889634
LANES = 128
NEG_INF = float("-inf")

ADAM_LR = 0.001
ADAM_B1 = 0.9
ADAM_B2 = 0.999
ADAM_EPS = 1e-08
ADAM_WD = 0.01
ADAM_STEP = 10

VMEM_LIMIT = 56 * 1024 * 1024
VMEM_LIMIT_WIDE = 62 * 1024 * 1024


def _params(sem, vmem=VMEM_LIMIT):
    return pltpu.CompilerParams(dimension_semantics=sem, vmem_limit_bytes=vmem)


def _const_spec(shape):
    nd = len(shape)
    return pl.BlockSpec(shape, lambda *_: (0,) * nd, pipeline_mode=pl.Buffered(1))


def _lane(shape):
    return lax.broadcasted_iota(jnp.int32, shape, len(shape) - 1)


def _head_ones(n):
    r = lax.shift_right_logical(lax.broadcasted_iota(jnp.int32, (n, n), 0), 6)
    c = lax.shift_right_logical(lax.broadcasted_iota(jnp.int32, (n, n), 1), 6)
    return (r == c).astype(BF16)


def _head_sum(v):
    w = v.shape[1]
    vb = v.astype(BF16)
    if w <= 256:
        return jnp.dot(vb, _head_ones(w), preferred_element_type=F32)
    ones = _head_ones(256)
    return jnp.concatenate([jnp.dot(vb[:, s:s + 256], ones, preferred_element_type=F32) for s in range(0, w, 256)], axis=1)


def _head_norm(seg, gain):
    rs = lax.rsqrt(_head_sum(seg * seg) * (1.0 / HEAD_DIM) + EPS)
    return seg * rs * gain


def _head_norm_bwd(seg, gain, d_out):
    rs = lax.rsqrt(_head_sum(seg * seg) * (1.0 / HEAD_DIM) + EPS)
    hat = seg * rs
    gd = d_out * gain
    d_seg = rs * (gd - hat * (_head_sum(gd * hat) * (1.0 / HEAD_DIM)))
    return d_seg, d_out * hat


def _expand_kv(v):
    r = pltpu.roll(v, 64, axis=1)
    lo = _lane(v.shape) < 64
    return jnp.concatenate([jnp.where(lo, v, r), jnp.where(lo, r, v)], axis=1)


def _fold_kv(e4):
    t0 = e4[:, 0:128] + e4[:, 128:256]
    t1 = e4[:, 256:384] + e4[:, 384:512]
    t0 = t0 + pltpu.roll(t0, 64, axis=1)
    t1 = t1 + pltpu.roll(t1, 64, axis=1)
    return jnp.where(_lane(t0.shape) < 64, t0, t1)


def _pick_lane(blk, idx):
    return jnp.sum(jnp.where(_lane(blk.shape) == idx, blk, 0.0), axis=1, keepdims=True)


def _nt(a, b):
    return lax.dot_general(a, b, (((1,), (1,)), ((), ())), preferred_element_type=F32)


def _tn(a, b):
    return lax.dot_general(a, b, (((0,), (0,)), ((), ())), preferred_element_type=F32)


def _xnorm(x2, g1, tm):
    t = x2.shape[0]

    def body(x_ref, g1_ref, xn_ref):
        x = x_ref[...]
        r = lax.rsqrt(jnp.mean(x * x, axis=-1, keepdims=True) + EPS)
        xn_ref[...] = (x * r * g1_ref[...]).astype(BF16)

    tile = pl.BlockSpec((tm, D_MODEL), lambda i: (i, 0))
    return pl.pallas_call(
        body, name="xnorm", grid=(t // tm,), in_specs=[tile, _const_spec((1, D_MODEL))], out_specs=tile,
        out_shape=jax.ShapeDtypeStruct((t, D_MODEL), BF16), compiler_params=_params(("arbitrary",)),
    )(x2, g1)


def _norm_proj(xn, w_main_t, w_f_t, gqa, gka, gqb, gkb, bf_row, s, tm):
    t = xn.shape[0]
    nt = s // tm

    def body(xn_ref, wm_ref, wf_ref, gqa_ref, gka_ref, gqb_ref, gkb_ref, b_ref,
             raw_ref, fl_ref, qa_ref, kae_ref, vae_ref, qo_ref, ko_ref, vo_ref, carry, c_ref):
        @pl.when(lax.rem(pl.program_id(0), nt) == 0)
        def _():
            carry[...] = jnp.zeros_like(carry)

        xn = xn_ref[...]
        proj = _nt(xn, wm_ref[...])
        raw_ref[...] = proj
        fl = _nt(xn, wf_ref[...])
        fl_ref[...] = fl
        qa_ref[...] = _head_norm(proj[:, 0:512], gqa_ref[...]).astype(BF16)
        kae_ref[...] = _expand_kv(_head_norm(proj[:, 512:640], gka_ref[...])).astype(BF16)
        vae_ref[...] = _expand_kv(proj[:, 640:768]).astype(BF16)

        z = fl + b_ref[...]
        e = jnp.exp(-jnp.abs(z))
        u = 1.0 + e
        log1p = jnp.where(u == 1.0, e, jnp.log(u) * (e / (u - 1.0)))
        lf = jnp.minimum(z, 0.0) - log1p
        for r0 in range(0, tm, 256):
            c_ref[r0:r0 + 256, :] = _tri_dot(256, False, lf[r0:r0 + 256]) + carry[...]
            carry[...] = c_ref[pl.ds(r0 + 255, 1), :]
        c2 = c_ref[...] * LOG2E
        qb = _head_norm(proj[:, 768:1280], gqb_ref[...]) * (SCALE * LOG2E)
        kb = _head_norm(proj[:, 1280:1792], gkb_ref[...])
        lane = _lane((tm, LANES))
        for h in range(8):
            j, half = h // 2, h % 2
            pair, blk = slice(LANES * j, LANES * (j + 1)), slice(LANES * h, LANES * (h + 1))
            feat = _spread3(c2[:, h:h + 1], (tm, LANES), (L_CK, L_CQ))
            q = _put_ones(_head_block(qb[:, pair], half), (L_CK, L_CK + 1, L_CK + 2))
            qo_ref[:, blk] = jnp.where((lane >= L_CQ) & (lane < L_CQ + 3), feat, q).astype(BF16)
            k = _put_ones(_head_block(kb[:, pair], half), tuple(range(L_CQ, L_CQ + 6)))
            ko_ref[:, blk] = jnp.where((lane >= L_CK) & (lane < L_CK + 3), -feat, k).astype(BF16)
            v = _head_block(proj[:, 1792 + LANES * j:1792 + LANES * (j + 1)], half)
            vo_ref[:, blk] = _put_ones(v, (L_ONE, L_DELTA, L_DELTA + 1, L_DELTA + 2)).astype(BF16)

    def tile(w):
        return pl.BlockSpec((tm, w), lambda i: (i, 0))

    aug = jax.ShapeDtypeStruct((t, 8 * LANES), BF16)
    return pl.pallas_call(
        body, name="norm_proj", grid=(t // tm,),
        in_specs=[tile(D_MODEL), _const_spec((MAIN_W, D_MODEL)), _const_spec((LANES, D_MODEL)),
                  _const_spec((1, 512)), _const_spec((1, 128)), _const_spec((1, 512)), _const_spec((1, 512)),
                  _const_spec((1, LANES))],
        out_specs=[tile(MAIN_W), tile(LANES), tile(512), tile(256), tile(256)] + [tile(8 * LANES)] * 3,
        out_shape=[jax.ShapeDtypeStruct((t, MAIN_W), F32),
                   jax.ShapeDtypeStruct((t, LANES), F32), jax.ShapeDtypeStruct((t, 512), BF16),
                   jax.ShapeDtypeStruct((t, 256), BF16), jax.ShapeDtypeStruct((t, 256), BF16), aug, aug, aug],
        scratch_shapes=[pltpu.VMEM((1, LANES), F32), pltpu.VMEM((tm, LANES), F32)],
        compiler_params=_params(("arbitrary",)),
    )(xn, w_main_t, w_f_t, gqa, gka, gqb, gkb, bf_row)


def _tri_dot(n, upper, v):
    r = lax.broadcasted_iota(jnp.int32, (n, n), 0)
    c = lax.broadcasted_iota(jnp.int32, (n, n), 1)
    tri = ((c >= r) if upper else (c <= r)).astype(BF16)
    hi = v.astype(BF16)
    mid = (v - hi.astype(F32)).astype(BF16)
    lo = (v - hi.astype(F32) - mid.astype(F32)).astype(BF16)
    return (jnp.dot(tri, hi, preferred_element_type=F32) + jnp.dot(tri, mid, preferred_element_type=F32)
            + jnp.dot(tri, lo, preferred_element_type=F32))


def _slope(p, hh):
    out = jnp.float32(2.0 ** -(2 * 3 + hh + 1))
    for pp in (2, 1, 0):
        out = jnp.where(p == pp, jnp.float32(2.0 ** -(2 * pp + hh + 1)), out)
    return out


def _swa_windows(ref, i, tq):
    nsub = tq // WINDOW
    cur = ref[pl.ds(pl.multiple_of(i * tq, tq), tq), :].reshape(nsub, WINDOW, LANES)
    first = ref[pl.ds(pl.multiple_of(jnp.maximum(i * tq - WINDOW, 0), WINDOW), WINDOW), :].reshape(1, WINDOW, LANES)
    return jnp.concatenate([jnp.concatenate([first, cur[0:nsub - 1]], axis=0), cur], axis=1)


def _both_heads(x3, lo):
    zero = jnp.zeros_like(x3)
    return jnp.concatenate([jnp.where(lo, x3, zero), jnp.where(lo, zero, x3)], axis=0)


def _swa_sinks(sink_ref, p, nsub):
    is_a = lax.broadcasted_iota(jnp.int32, (2 * nsub, 1, 1), 0) < nsub
    sinks = sink_ref[...]
    return jnp.where(is_a, _pick_lane(sinks, 2 * p).reshape(1, 1, 1), _pick_lane(sinks, 2 * p + 1).reshape(1, 1, 1))


def _swa_bias(p, i, nsub, keys_first):
    shape = (1, 2 * WINDOW, WINDOW) if keys_first else (1, WINDOW, 2 * WINDOW)
    qi = lax.broadcasted_iota(jnp.int32, shape, 2 if keys_first else 1)
    ki = lax.broadcasted_iota(jnp.int32, shape, 1 if keys_first else 2)
    dist = qi + WINDOW - ki
    band = (dist >= 0) & (dist < WINDOW)
    tiles = []
    for hh in range(2):
        bias = jnp.where(band, -_slope(p, hh) * dist.astype(F32), NEG_INF)
        tiles += [jnp.where((i == 0) & (ki < WINDOW), NEG_INF, bias)] + [bias] * (nsub - 1)
    return jnp.concatenate(tiles, axis=0)


def _swa_fwd(qa, kae, vae, sink_row, nb, s, tq):
    t = qa.shape[0]
    nq = s // tq
    nsub = tq // WINDOW

    def body(q_ref, k_ref, v_ref, sink_ref, o_ref, lse_ref):
        p, i = pl.program_id(1), pl.program_id(2)
        lo = _lane((1, 1, LANES)) < 64
        kk, vv = _swa_windows(k_ref, i, tq), _swa_windows(v_ref, i, tq)
        qs = (q_ref[...].astype(F32) * SCALE).astype(BF16).reshape(nsub, WINDOW, LANES)
        q8 = _both_heads(qs, lo)
        s8 = jnp.einsum("bqd,bkd->bqk", q8, jnp.concatenate([kk, kk], axis=0), preferred_element_type=F32)
        sink = _swa_sinks(sink_ref, p, nsub)
        s8 = s8 + _swa_bias(p, i, nsub, False)
        m = jnp.maximum(jnp.max(s8, axis=2, keepdims=True), sink)
        e = jnp.exp(s8 - m)
        den = jnp.sum(e, axis=2, keepdims=True) + jnp.exp(sink - m)
        pr = (e * (1.0 / den)).astype(BF16)
        o8 = jnp.einsum("bqk,bkd->bqd", pr, jnp.concatenate([vv, vv], axis=0), preferred_element_type=F32)
        lse8 = m + jnp.log(den)
        o_ref[...] = jnp.where(lo, o8[0:nsub], o8[nsub:]).astype(BF16).reshape(tq, LANES)
        lse_ref[...] = jnp.where(lo, lse8[0:nsub], lse8[nsub:]).reshape(tq, LANES)

    return pl.pallas_call(
        body, name="swa_fwd", grid=(nb, 4, nq),
        in_specs=[pl.BlockSpec((tq, LANES), lambda b, p, i: (b * nq + i, p)),
                  pl.BlockSpec((s, LANES), lambda b, p, i: (b, lax.shift_right_logical(p, 1))),
                  pl.BlockSpec((s, LANES), lambda b, p, i: (b, lax.shift_right_logical(p, 1))),
                  pl.BlockSpec((1, LANES), lambda b, p, i: (0, 0))],
        out_specs=[pl.BlockSpec((tq, LANES), lambda b, p, i: (b * nq + i, p)),
                   pl.BlockSpec((None, tq, LANES), lambda b, p, i: (p, b * nq + i, 0))],
        out_shape=[jax.ShapeDtypeStruct((t, 512), BF16), jax.ShapeDtypeStruct((4, t, LANES), F32)],
        compiler_params=_params(("arbitrary", "arbitrary", "arbitrary")),
    )(qa, kae, vae, sink_row)


def _swa_bwd(qa, kae, vae, do_a, sink_row, lse, delta, nb, s, tq):
    t = qa.shape[0]
    nq = s // tq
    nsub = tq // WINDOW

    def body(q_ref, do_ref, k_ref, v_ref, sink_ref, lse_ref, dl_ref, dq_ref, dk_ref, dv_ref, ds_ref):
        p, i = pl.program_id(1), pl.program_id(2)

        @pl.when(i == 0)
        def _():
            ds_ref[...] = jnp.zeros_like(ds_ref)

        lo = _lane((1, 1, LANES)) < 64
        kk, vv = _swa_windows(k_ref, i, tq), _swa_windows(v_ref, i, tq)
        kks = (kk.astype(F32) * SCALE).astype(BF16)
        k8, v8 = jnp.concatenate([kks, kks], axis=0), jnp.concatenate([vv, vv], axis=0)
        q8 = _both_heads(q_ref[...].reshape(nsub, WINDOW, LANES), lo)
        do8 = _both_heads(do_ref[...].reshape(nsub, WINDOW, LANES), lo)
        cur = pl.multiple_of(i * tq, tq)
        sub = lax.broadcasted_iota(jnp.int32, (WINDOW, WINDOW), 0)
        lse_t = [lse_ref[u * WINDOW:(u + 1) * WINDOW, :].T for u in range(nsub)]
        dl_t = [dl_ref[u * WINDOW:(u + 1) * WINDOW, :].T for u in range(nsub)]
        lse8 = jnp.concatenate([t_[64 * hh:64 * hh + 1, :].reshape(1, 1, WINDOW) for hh in range(2) for t_ in lse_t], axis=0)
        dl8 = jnp.concatenate([jnp.sum(jnp.where(sub == 2 * p + hh, t_, 0.0), axis=0, keepdims=True).reshape(1, 1, WINDOW)
                               for hh in range(2) for t_ in dl_t], axis=0)
        sink = _swa_sinks(sink_ref, p, nsub)
        st = jnp.einsum("bkd,bqd->bkq", k8, q8, preferred_element_type=F32) + _swa_bias(p, i, nsub, True) - lse8
        pt = jnp.exp(st)
        dpt = jnp.einsum("bkd,bqd->bkq", v8, do8, preferred_element_type=F32)
        dst = pt * (dpt - dl8)
        ptb, dstb = pt.astype(BF16), dst.astype(BF16)
        dv8 = jnp.einsum("bkq,bqd->bkd", ptb, do8, preferred_element_type=F32)
        dk8 = jnp.einsum("bkq,bqd->bkd", dstb, q8, preferred_element_type=F32) * SCALE
        dq8 = jnp.einsum("bkq,bkd->bqd", dstb, k8, preferred_element_type=F32)
        dq_ref[...] = jnp.where(lo, dq8[0:nsub], dq8[nsub:]).reshape(tq, LANES)

        psd = jnp.exp(sink - lse8) * dl8
        row_h = lax.broadcasted_iota(jnp.int32, (8, LANES), 0)
        for hh in range(2):
            tot = jnp.sum(jnp.sum(psd[hh * nsub:(hh + 1) * nsub], axis=2, keepdims=True), axis=0, keepdims=True)
            ds_ref[...] += jnp.where(row_h == hh, -tot.reshape(1, 1), 0.0)

        prev = pl.multiple_of(jnp.maximum(i * tq - WINDOW, 0), WINDOW)
        for g8, g_ref in ((dk8, dk_ref), (dv8, dv_ref)):
            g4 = g8[0:nsub] + g8[nsub:]
            own, before = g4[:, WINDOW:, :], g4[:, 0:WINDOW, :]
            shifted = jnp.concatenate([before[1:nsub], jnp.zeros((1, WINDOW, LANES), F32)], axis=0)
            g_ref[pl.ds(cur, tq), :] = (own + shifted).reshape(tq, LANES)
            g_ref[pl.ds(prev, WINDOW), :] += before[0]

    return pl.pallas_call(
        body, name="swa_bwd", grid=(nb, 4, nq),
        in_specs=[pl.BlockSpec((tq, LANES), lambda b, p, i: (b * nq + i, p)),
                  pl.BlockSpec((tq, LANES), lambda b, p, i: (b * nq + i, p)),
                  pl.BlockSpec((s, LANES), lambda b, p, i: (b, lax.shift_right_logical(p, 1))),
                  pl.BlockSpec((s, LANES), lambda b, p, i: (b, lax.shift_right_logical(p, 1))),
                  pl.BlockSpec((1, LANES), lambda b, p, i: (0, 0)),
                  pl.BlockSpec((None, tq, LANES), lambda b, p, i: (p, b * nq + i, 0)),
                  pl.BlockSpec((tq, LANES), lambda b, p, i: (b * nq + i, 0))],
        out_specs=[pl.BlockSpec((tq, LANES), lambda b, p, i: (b * nq + i, p)),
                   pl.BlockSpec((s, LANES), lambda b, p, i: (b, p)),
                   pl.BlockSpec((s, LANES), lambda b, p, i: (b, p)),
                   pl.BlockSpec((None, None, 8, LANES), lambda b, p, i: (b, p, 0, 0))],
        out_shape=[jax.ShapeDtypeStruct((t, 512), F32), jax.ShapeDtypeStruct((t, 512), F32),
                   jax.ShapeDtypeStruct((t, 512), F32), jax.ShapeDtypeStruct((nb, 4, 8, LANES), F32)],
        compiler_params=_params(("arbitrary", "arbitrary", "arbitrary")),
    )(qa, do_a, kae, vae, sink_row, lse, delta)


MESH = pl.DeviceIdType.MESH
ANY = pl.BlockSpec(memory_space=pl.ANY)
N_SEM = 7


def _gather_steps(pairs, send_sems, recv_sems, local_sems):
    x, y, c = lax.axis_index("x"), lax.axis_index("y"), lax.axis_index("c")
    me, sibling = (x, y, c), (x, y, 1 - c)
    chips = [(1 - x, y), (x, 1 - y), (1 - x, 1 - y)]
    mine, first, passed, landed, last = [], [], [], [], []
    for a, (x_ref, out_ref) in enumerate(pairs):
        def slot(px, py, pc, out_ref=out_ref):
            return out_ref.at[4 * px + 2 * py + pc]

        def copy(k, block, to, src=None, a=a, slot=slot):
            return pltpu.make_async_remote_copy(
                src_ref=slot(*block) if src is None else src, dst_ref=slot(*block),
                send_sem=send_sems.at[N_SEM * a + k], recv_sem=recv_sems.at[N_SEM * a + k], device_id=to, device_id_type=MESH)

        mine.append(pltpu.make_async_copy(x_ref, slot(*me), local_sems.at[a]))
        first += [copy(0, me, sibling, src=x_ref)] + [copy(1 + j, me, (*chip, c), src=x_ref) for j, chip in enumerate(chips)]
        passed += [copy(4 + j, (*chip, c), sibling) for j, chip in enumerate(chips)]
        landed += [copy(1 + j, (*chip, c), me) for j, chip in enumerate(chips)]
        last += [copy(0, sibling, me)] + [copy(4 + j, (*chip, 1 - c), me) for j, chip in enumerate(chips)]

    def start():
        for cp in mine + first:
            cp.start()

    def forward():
        for arrived, onward in zip(landed, passed):
            arrived.wait_recv()
            onward.start()

    def finish():
        for cp in last:
            cp.wait_recv()
        for cp in first + passed:
            cp.wait_send()
        for cp in mine:
            cp.wait()

    return start, forward, finish


def _exchange_steps(pairs, send_sems, recv_sems, local_sems):
    x, y, c = lax.axis_index("x"), lax.axis_index("y"), lax.axis_index("c")
    my_id = 4 * x + 2 * y + c
    local, remote = [], []
    for a, (src, dst) in enumerate(pairs):
        local.append(pltpu.make_async_copy(src.at[my_id], dst.at[my_id], local_sems.at[a]))
        for k in range(1, N_DEV):
            px = 1 - x if k & 4 else x
            py = 1 - y if k & 2 else y
            pc = 1 - c if k & 1 else c
            remote.append(pltpu.make_async_remote_copy(
                src_ref=src.at[4 * px + 2 * py + pc], dst_ref=dst.at[my_id],
                send_sem=send_sems.at[N_SEM * a + k - 1], recv_sem=recv_sems.at[N_SEM * a + k - 1],
                device_id=(px, py, pc), device_id_type=MESH))

    def start():
        for cp in local + remote:
            cp.start()

    def finish():
        for cp in remote:
            cp.wait_recv()
        for cp in remote:
            cp.wait_send()
        for cp in local:
            cp.wait()

    return start, finish


L_ONE = 64
L_CK = 65
L_CQ = 68
L_LSE = 71
L_DELTA = 74


def _head_block(pair, half):
    y = pair if half == 0 else pltpu.roll(pair, 64, axis=1)
    return jnp.where(_lane(pair.shape) < 64, y, 0.0)


def _put3(blk, lane0, col):
    lane = _lane(blk.shape)
    hi = col.astype(BF16).astype(F32)
    mid = (col - hi).astype(BF16).astype(F32)
    lo = (col - hi - mid).astype(BF16).astype(F32)
    return jnp.where(lane == lane0, hi, jnp.where(lane == lane0 + 1, mid, jnp.where(lane == lane0 + 2, lo, blk)))


def _spread3(col, shape, lane0s):
    lane = _lane(shape)
    hi = col.astype(BF16).astype(F32)
    mid = (col - hi).astype(BF16).astype(F32)
    lo = (col - hi - mid).astype(BF16).astype(F32)

    def at(k):
        return functools.reduce(jnp.logical_or, [lane == ln + k for ln in lane0s])

    return jnp.where(at(0), hi, jnp.where(at(1), mid, jnp.where(at(2), lo, 0.0)))


def _put_ones(blk, lanes):
    lane = _lane(blk.shape)
    hit = functools.reduce(jnp.logical_or, [lane == ln for ln in lanes])
    return jnp.where(hit, 1.0, blk)


def _to_pairs(ref):
    out = []
    for j in range(4):
        a, b = ref[:, 2 * LANES * j:2 * LANES * j + LANES], ref[:, 2 * LANES * j + LANES:2 * LANES * (j + 1)]
        out.append(jnp.where(_lane(a.shape) < 64, a, pltpu.roll(b, 64, axis=1)))
    return jnp.concatenate(out, axis=1)


def _fox_fwd(q_aug, k_aug, v_aug, nb, s, bt, shards=()):
    t = q_aug.shape[0]
    nq = s // bt
    n_in, n_sh = 3, len(shards)

    def body(*refs):
        q_ref, k_ref, v_ref = refs[:n_in]
        o_ref, ql_ref = refs[n_in + n_sh:n_in + n_sh + 2]
        if shards:
            srcs, dsts = refs[n_in:n_in + n_sh], refs[n_in + n_sh + 2:n_in + 2 * n_sh + 2]
            start, forward, finish = _gather_steps(list(zip(srcs, dsts)), *refs[n_in + 2 * n_sh + 2:])
            step = (pl.program_id(0) * 4 + pl.program_id(1)) * nq + pl.program_id(2)
            pl.when(step == 0)(start)
            pl.when(step == nb * 3 * nq)(forward)
        i = pl.program_id(2)
        sls = [slice(LANES * hh, LANES * (hh + 1)) for hh in range(2)]
        qhs = [q_ref[:, sl] for sl in sls]

        def update(m, acc, qrows, start, size, sl, causal):
            sc = _nt(qrows, k_ref[pl.ds(start, size), sl])
            if causal:
                row = lax.broadcasted_iota(jnp.int32, sc.shape, 0)
                col = lax.broadcasted_iota(jnp.int32, sc.shape, 1)
                sc = jnp.where(row >= col, sc, NEG_INF)
            m_new = jnp.maximum(m, jnp.max(sc, axis=1, keepdims=True))
            pr = jnp.exp2(sc - m_new).astype(BF16)
            acc = jnp.exp2(m - m_new) * acc + jnp.dot(pr, v_ref[pl.ds(start, size), sl], preferred_element_type=F32)
            return m_new, acc

        def blk(kb_i, carry):
            start = pl.multiple_of(kb_i * bt, bt)
            return tuple(update(m, acc, qh, start, bt, sl, False) for (m, acc), qh, sl in zip(carry, qhs, sls))

        def diag_blk(carry):
            start = pl.multiple_of(i * bt, bt)
            return tuple(update(m, acc, qh, start, bt, sl, True) for (m, acc), qh, sl in zip(carry, qhs, sls))

        init = tuple((jnp.full((bt, 1), NEG_INF, F32), jnp.zeros((bt, LANES), F32)) for _ in range(2))
        carry = lax.fori_loop(0, i, blk, init)
        outs = []
        for (m, acc), qh, sl in zip(diag_blk(carry), qhs, sls):
            l = acc[:, L_ONE:L_ONE + 1]
            outs.append(acc * (1.0 / l))
            ql_ref[:, sl] = _put3(qh.astype(F32), L_LSE, -(m + jnp.log(l) * LOG2E)).astype(BF16)
        o_ref[...] = jnp.where(_lane((1, LANES)) < 64, outs[0], pltpu.roll(outs[1], 64, axis=1)).astype(BF16)
        if shards:
            pl.when(step == nb * 4 * nq - 1)(finish)

    in_specs = [pl.BlockSpec((bt, 2 * LANES), lambda b, j, i: (b * nq + i, j)),
                pl.BlockSpec((s, 2 * LANES), lambda b, j, i: (b, j)),
                pl.BlockSpec((s, 2 * LANES), lambda b, j, i: (b, j))]
    out_specs = [pl.BlockSpec((bt, LANES), lambda b, j, i: (b * nq + i, j)),
                 pl.BlockSpec((bt, 2 * LANES), lambda b, j, i: (b * nq + i, j))]
    out_shape = [jax.ShapeDtypeStruct((t, 512), BF16), jax.ShapeDtypeStruct((t, 8 * LANES), BF16)]
    args, scratch = [q_aug, k_aug, v_aug, *shards], []
    if shards:
        in_specs += [ANY] * n_sh
        out_specs += [ANY] * n_sh
        out_shape += [jax.ShapeDtypeStruct((N_DEV,) + sh.shape, sh.dtype) for sh in shards]
        scratch = [pltpu.SemaphoreType.DMA((N_SEM * n_sh,)), pltpu.SemaphoreType.DMA((N_SEM * n_sh,)),
                   pltpu.SemaphoreType.DMA((n_sh,))]
    return pl.pallas_call(
        body, name="fox_fwd", grid=(nb, 4, nq), in_specs=in_specs, out_specs=out_specs, out_shape=out_shape,
        scratch_shapes=scratch, compiler_params=_params(("arbitrary", "arbitrary", "arbitrary")),
    )(*args)


def _fox_bwd(ql_aug, k_aug, v_aug, do_aug, nb, s, bt, exch=()):
    t = ql_aug.shape[0]
    nk = s // bt
    n_in, n_out, n_ex = 4, 3, len(exch)

    def body(*refs):
        q_ref, do_ref, k_ref, v_ref = refs[:n_in]
        dq_ref, dk_ref, dv_ref = refs[n_in + n_ex:n_in + n_ex + n_out]
        if exch:
            srcs = refs[n_in:n_in + n_ex]
            dsts = refs[n_in + n_ex + n_out:n_in + 2 * n_ex + n_out]
            start, finish = _exchange_steps(list(zip(srcs, dsts)), *refs[n_in + 2 * n_ex + n_out:])
            step = (pl.program_id(0) * 4 + pl.program_id(1)) * nk + pl.program_id(2)
            pl.when(step == 0)(start)
        kb_i = pl.program_id(2)

        @pl.when(kb_i == 0)
        def _():
            dq_ref[...] = jnp.zeros_like(dq_ref)

        row = lax.broadcasted_iota(jnp.int32, (bt, bt), 0)
        col = lax.broadcasted_iota(jnp.int32, (bt, bt), 1)
        sls = [slice(LANES * hh, LANES * (hh + 1)) for hh in range(2)]
        khs, vhs = [k_ref[:, sl] for sl in sls], [v_ref[:, sl] for sl in sls]

        def blk(qi, carry, diag):
            start = pl.multiple_of(qi * bt, bt)
            new = []
            for (dk_a, dv_a), kh, vh, sl in zip(carry, khs, vhs, sls):
                qblk, doblk = q_ref[pl.ds(start, bt), sl], do_ref[pl.ds(start, bt), sl]
                st = _nt(kh, qblk)
                if diag:
                    pt = jnp.where(col >= row, jnp.exp2(jnp.where(col >= row, st, 0.0)), 0.0)
                else:
                    pt = jnp.exp2(st)
                dst = pt * _nt(vh, doblk)
                ptb, dstb = pt.astype(BF16), dst.astype(BF16)
                dv_a = dv_a + jnp.dot(ptb, doblk, preferred_element_type=F32)
                dk_a = dk_a + jnp.dot(dstb, qblk, preferred_element_type=F32)
                dq_ref[pl.ds(start, bt), sl] += _tn(dstb, kh)
                new.append((dk_a, dv_a))
            return tuple(new)

        zero = jnp.zeros((bt, LANES), F32)
        carry = blk(kb_i, ((zero, zero), (zero, zero)), True)
        carry = lax.fori_loop(kb_i + 1, nk, lambda qi, c: blk(qi, c, False), carry)
        for (dk_acc, dv_acc), sl in zip(carry, sls):
            dk_ref[:, sl] = dk_acc
            dv_ref[:, sl] = dv_acc
        if exch:
            pl.when(step == nb * 4 * nk - 1)(finish)

    scratch = []
    if exch:
        scratch = [pltpu.SemaphoreType.DMA((N_SEM * n_ex,)), pltpu.SemaphoreType.DMA((N_SEM * n_ex,)),
                   pltpu.SemaphoreType.DMA((n_ex,))]
    whole = pl.BlockSpec((s, 2 * LANES), lambda b, j, kb_i: (b, j))
    tile = pl.BlockSpec((bt, 2 * LANES), lambda b, j, kb_i: (b * nk + kb_i, j))
    shp = jax.ShapeDtypeStruct((t, 8 * LANES), F32)
    return pl.pallas_call(
        body, name="fox_bwd", grid=(nb, 4, nk),
        in_specs=[whole, whole, tile, tile] + [ANY] * n_ex,
        out_specs=[whole, tile, tile] + [ANY] * n_ex,
        out_shape=[shp, shp, shp] + [jax.ShapeDtypeStruct(e.shape, e.dtype) for e in exch],
        scratch_shapes=scratch, compiler_params=_params(("arbitrary", "arbitrary", "arbitrary")),
    )(ql_aug, do_aug, k_aug, v_aug, *exch)


FF_BLK = D_FF // N_DEV


def _mlp_fwd(x2, ma, mb, tgt, w_out, g2, w_up, w_down, tm):
    t = x2.shape[0]

    def body(x_ref, ma_ref, mb_ref, tg_ref, wo_ref, g2_ref, wu_ref, wd_ref,
             h_ref, hn_ref, hid_ref, dy_ref, dyb_ref, loss_ref):
        @pl.when(pl.program_id(0) == 0)
        def _():
            loss_ref[...] = jnp.zeros_like(loss_ref)

        h = (x_ref[...] + jnp.dot(ma_ref[...], wo_ref[0:512, :], preferred_element_type=F32)
             + jnp.dot(mb_ref[...], wo_ref[512:1024, :], preferred_element_type=F32))
        h_ref[...] = h
        r = lax.rsqrt(jnp.mean(h * h, axis=-1, keepdims=True) + EPS)
        hn = (h * r * g2_ref[...]).astype(BF16)
        hn_ref[...] = hn
        for d in range(N_DEV):
            u = jnp.maximum(jnp.dot(hn, wu_ref[d], preferred_element_type=F32), 0.0)
            hid_ref[:, FF_BLK * d:FF_BLK * (d + 1)] = (u * u).astype(BF16)
        y = h + jnp.dot(hid_ref[...], wd_ref[...], preferred_element_type=F32)
        err = y - tg_ref[...]
        dy = err * (1.0 / D_MODEL)
        dy_ref[...] = dy
        dyb_ref[...] = dy.astype(BF16)
        part =0.5 * jnp.sum(jnp.sum(err * err, axis=1, keepdims=True) * (1.0 / D_MODEL), axis=0, keepdims=True)
        loss_ref[...] += part

    def tile(w):
        return pl.BlockSpec((tm, w), lambda i: (i, 0))

    return pl.pallas_call(
        body, name="mlp_fwd", grid=(t // tm,),
        in_specs=[tile(D_MODEL), tile(512), tile(512), tile(D_MODEL), _const_spec((D_MODEL, D_MODEL)),
                  _const_spec((1, D_MODEL)), _const_spec((N_DEV, D_MODEL, FF_BLK)), _const_spec((D_FF, D_MODEL))],
        out_specs=[tile(D_MODEL), tile(D_MODEL), tile(D_FF), tile(D_MODEL), tile(D_MODEL),
                   pl.BlockSpec((8, LANES), lambda i: (0, 0))],
        out_shape=[jax.ShapeDtypeStruct((t, D_MODEL), F32), jax.ShapeDtypeStruct((t, D_MODEL), BF16),
                   jax.ShapeDtypeStruct((t, D_FF), BF16), jax.ShapeDtypeStruct((t, D_MODEL), F32),
                   jax.ShapeDtypeStruct((t, D_MODEL), BF16), jax.ShapeDtypeStruct((8, LANES), F32)],
        compiler_params=_params(("arbitrary",)),
    )(x2, ma, mb, tgt, w_out, g2, w_up, w_down)


def _mlp_bwd(dy, hid, h, ma, mb, w_down, w_up_t, w_out, g2, tm):
    t = dy.shape[0]

    def body(dy_ref, hid_ref, h_ref, ma_ref, mb_ref, wd_ref, wut_ref, wo_ref, g2_ref,
             du_ref, dh_ref, dhb_ref, dma_ref, dob_ref, dla_ref, gg_ref):
        @pl.when(pl.program_id(0) == 0)
        def _():
            gg_ref[...] = jnp.zeros_like(gg_ref)

        dy = dy_ref[...]
        d_hid = _nt(dy.astype(BF16), wd_ref[...])
        du = (d_hid * (2.0 * jnp.sqrt(hid_ref[...].astype(F32)))).astype(BF16)
        du_ref[...] = du
        d_hn = jnp.dot(du, wut_ref[...], preferred_element_type=F32)
        h = h_ref[...]
        r = lax.rsqrt(jnp.mean(h * h, axis=-1, keepdims=True) + EPS)
        hat = h * r
        gd = d_hn * g2_ref[...]
        dh = dy + r * (gd - hat * jnp.mean(gd * hat, axis=-1, keepdims=True))
        gg_ref[...] += jnp.sum(d_hn * hat, axis=0, keepdims=True)
        dh_ref[...] = dh
        dhb = dh.astype(BF16)
        dhb_ref[...] = dhb
        dm = _nt(dhb, wo_ref[...]).astype(BF16)
        dma, dmb = dm[:, 0:512], dm[:, 512:1024]
        dma_ref[...] = dma
        sel = (lax.shift_right_logical(lax.broadcasted_iota(jnp.int32, (512, LANES), 0), 6)
               == lax.broadcasted_iota(jnp.int32, (512, LANES), 1)).astype(BF16)
        dla_ref[...] = jnp.dot((dma.astype(F32) * ma_ref[...].astype(F32)).astype(BF16), sel, preferred_element_type=F32)
        dmb32 = dmb.astype(F32)
        dlb = jnp.dot((dmb32 * mb_ref[...].astype(F32)).astype(BF16), sel, preferred_element_type=F32)
        for hd in range(8):
            blk = _head_block(dmb32[:, LANES * (hd // 2):LANES * (hd // 2 + 1)], hd % 2)
            dob_ref[:, LANES * hd:LANES * (hd + 1)] = _put3(blk, L_DELTA, -dlb[:, hd:hd + 1]).astype(BF16)

    def tile(w):
        return pl.BlockSpec((tm, w), lambda i: (i, 0))

    return pl.pallas_call(
        body, name="mlp_bwd", grid=(t // tm,),
        in_specs=[tile(D_MODEL), tile(D_FF), tile(D_MODEL), tile(512), tile(512), _const_spec((D_FF, D_MODEL)),
                  _const_spec((D_FF, D_MODEL)), _const_spec((D_MODEL, D_MODEL)), _const_spec((1, D_MODEL))],
        out_specs=[tile(D_FF), tile(D_MODEL), tile(D_MODEL), tile(512), tile(8 * LANES), tile(LANES),
                   pl.BlockSpec((1, D_MODEL), lambda i: (0, 0))],
        out_shape=[jax.ShapeDtypeStruct((t, D_FF), BF16), jax.ShapeDtypeStruct((t, D_MODEL), F32),
                   jax.ShapeDtypeStruct((t, D_MODEL), BF16), jax.ShapeDtypeStruct((t, 512), BF16),
                   jax.ShapeDtypeStruct((t, 8 * LANES), BF16), jax.ShapeDtypeStruct((t, LANES), F32),
                   jax.ShapeDtypeStruct((1, D_MODEL), F32)],
        compiler_params=_params(("arbitrary",), VMEM_LIMIT_WIDE),
    )(dy, hid, h, ma, mb, w_down, w_up_t, w_out, g2)


def _wgrad(a, b, name, bm, bn, tk, out_dtype=F32, col_blocks=False, a2=None):
    t, m = a.shape
    n = b.shape[1]
    bm, bn = min(bm, m), min(bn, n)
    nk = t // tk

    def body(*refs):
        if a2 is None:
            a_ref, b_ref, o_ref, acc = refs
        else:
            a_ref, b_ref, a2_ref, o_ref, o2_ref, acc, acc2 = refs
        i, k = pl.program_id(0), pl.program_id(2)

        @pl.when(k == 0)
        def _():
            acc[...] = jnp.zeros_like(acc)

        acc[...] += _tn(a_ref[...], b_ref[...])

        @pl.when(k == nk - 1)
        def _():
            o_ref[...] = acc[...].astype(out_dtype)

        if a2 is not None:
            @pl.when((i == 0) & (k == 0))
            def _():
                acc2[...] = jnp.zeros_like(acc2)

            @pl.when(i == 0)
            def _():
                acc2[...] += _tn(a2_ref[...], b_ref[...])

            @pl.when((i == 0) & (k == nk - 1))
            def _():
                o2_ref[...] = acc2[...]

    if col_blocks:
        out_spec = pl.BlockSpec((None, bm, bn), lambda i, j, k: (j, i, 0))
        out_shape = jax.ShapeDtypeStruct((n // bn, m, bn), out_dtype)
    else:
        out_spec = pl.BlockSpec((bm, bn), lambda i, j, k: (i, j))
        out_shape = jax.ShapeDtypeStruct((m, n), out_dtype)
    in_specs = [pl.BlockSpec((tk, bm), lambda i, j, k: (k, i)), pl.BlockSpec((tk, bn), lambda i, j, k: (k, j))]
    out_specs, out_shapes, scratch, args = [out_spec], [out_shape], [pltpu.VMEM((bm, bn), F32)], [a, b]
    if a2 is not None:
        m2 = a2.shape[1]
        in_specs.append(pl.BlockSpec((tk, m2), lambda i, j, k: (k, 0)))
        out_specs.append(pl.BlockSpec((m2, n), lambda i, j, k: (0, 0)))
        out_shapes.append(jax.ShapeDtypeStruct((m2, n), F32))
        scratch.append(pltpu.VMEM((m2, n), F32))
        args.append(a2)
    out = pl.pallas_call(
        body, name=name, grid=(m // bm, n // bn, nk), in_specs=in_specs, out_specs=out_specs, out_shape=out_shapes,
        scratch_shapes=scratch, compiler_params=_params(("arbitrary", "arbitrary", "arbitrary")),
    )(*args)
    return out[0] if a2 is None else out


def _xgrad(dproj, dfb, x2, dh, w_main_t, w_f_t, g1, tm):
    t = x2.shape[0]

    def body(dp_ref, dfb_ref, x_ref, dh_ref, wmt_ref, wft_ref, g1_ref, dx_ref, gg1_ref):
        @pl.when(pl.program_id(0) == 0)
        def _():
            gg1_ref[...] = jnp.zeros_like(gg1_ref)

        d_xn = (jnp.dot(dp_ref[...], wmt_ref[...], preferred_element_type=F32)
                + jnp.dot(dfb_ref[...], wft_ref[...], preferred_element_type=F32))
        x = x_ref[...]
        r = lax.rsqrt(jnp.mean(x * x, axis=-1, keepdims=True) + EPS)
        hat = x * r
        gd = d_xn * g1_ref[...]
        dx_ref[...] = dh_ref[...] + r * (gd - hat * jnp.mean(gd * hat, axis=-1, keepdims=True))
        gg1_ref[...] += jnp.sum(d_xn * hat, axis=0, keepdims=True)

    def tile(w):
        return pl.BlockSpec((tm, w), lambda i: (i, 0))

    return pl.pallas_call(
        body, name="xgrad", grid=(t // tm,),
        in_specs=[tile(MAIN_W), tile(LANES), tile(D_MODEL), tile(D_MODEL), _const_spec((MAIN_W, D_MODEL)),
                  _const_spec((LANES, D_MODEL)), _const_spec((1, D_MODEL))],
        out_specs=[tile(D_MODEL), pl.BlockSpec((1, D_MODEL), lambda i: (0, 0))],
        out_shape=[jax.ShapeDtypeStruct((t, D_MODEL), F32), jax.ShapeDtypeStruct((1, D_MODEL), F32)],
        compiler_params=_params(("arbitrary",)),
    )(dproj, dfb, x2, dh, w_main_t, w_f_t, g1)


def _proj_bwd(raw, dqa, dkae, dvae, dqb, dkb, dvb, fl, bf_row, gqa, gka, gqb, gkb, nb, s, tm):
    t = raw.shape[0]
    nt = s // tm

    def body(raw_ref, dqa_ref, dkae_ref, dvae_ref, dqb_ref, dkb_ref, dvb_ref, fl_ref, b_ref,
             gqa_ref, gka_ref, gqb_ref, gkb_ref,
             dp_ref, dfb_ref, ggqa_ref, ggka_ref, ggqb_ref, ggkb_ref, gb_ref, carry, dlf_ref):
        @pl.when((pl.program_id(0) == 0) & (pl.program_id(1) == 0))
        def _():
            for r in (ggqa_ref, ggka_ref, ggqb_ref, ggkb_ref, gb_ref):
                r[...] = jnp.zeros_like(r)

        @pl.when(pl.program_id(1) == 0)
        def _():
            carry[...] = jnp.zeros_like(carry)

        lane = _lane((tm, LANES))
        dc = jnp.zeros((tm, LANES), F32)
        for hd in range(8):
            col = (dqb_ref[:, LANES * hd + L_CQ:LANES * hd + L_CQ + 1] - dkb_ref[:, LANES * hd + L_CK:LANES * hd + L_CK + 1])
            dc = jnp.where(lane == hd, col, dc)
        dlf_ref[...] = _tri_dot(tm, True, dc) + carry[...]
        carry[...] = dlf_ref[pl.ds(0, 1), :]
        dfl = dlf_ref[...] * (1.0 / (1.0 + jnp.exp(fl_ref[...] + b_ref[...])))
        gb_ref[...] += jnp.sum(dfl, axis=0, keepdims=True)

        raw = raw_ref[...]
        d_qa, p_qa = _head_norm_bwd(raw[:, 0:512], gqa_ref[...], dqa_ref[...])
        d_ka, p_ka = _head_norm_bwd(raw[:, 512:640], gka_ref[...], _fold_kv(dkae_ref[...]))
        d_va = _fold_kv(dvae_ref[...])
        d_qb, p_qb = _head_norm_bwd(raw[:, 768:1280], gqb_ref[...], _to_pairs(dqb_ref) * SCALE)
        d_kb, p_kb = _head_norm_bwd(raw[:, 1280:1792], gkb_ref[...], _to_pairs(dkb_ref) * (1.0 / LOG2E))
        ggqa_ref[...] += jnp.sum(p_qa, axis=0, keepdims=True)
        ggka_ref[...] += jnp.sum(p_ka, axis=0, keepdims=True)
        ggqb_ref[...] += jnp.sum(p_qb, axis=0, keepdims=True)
        ggkb_ref[...] += jnp.sum(p_kb, axis=0, keepdims=True)
        dproj = jnp.concatenate([d_qa, d_ka, d_va, d_qb, d_kb, _to_pairs(dvb_ref)], axis=1).astype(BF16)
        dp_ref[...] = dproj
        dfb_ref[...] = dfl.astype(BF16)

    def tile(w):
        return pl.BlockSpec((tm, w), lambda b, i: (b * nt + (nt - 1 - i), 0))

    def acc(w):
        return pl.BlockSpec((1, w), lambda b, i: (0, 0))

    return pl.pallas_call(
        body, name="proj_bwd", grid=(nb, nt),
        in_specs=[tile(MAIN_W), tile(512), tile(512), tile(512), tile(8 * LANES), tile(8 * LANES), tile(8 * LANES), tile(LANES),
                  _const_spec((1, LANES)), _const_spec((1, 512)), _const_spec((1, 128)), _const_spec((1, 512)),
                  _const_spec((1, 512))],
        out_specs=[tile(MAIN_W), tile(LANES), acc(512), acc(128), acc(512), acc(512), acc(LANES)],
        out_shape=[jax.ShapeDtypeStruct((t, MAIN_W), BF16), jax.ShapeDtypeStruct((t, LANES), BF16),
                   jax.ShapeDtypeStruct((1, 512), F32), jax.ShapeDtypeStruct((1, 128), F32),
                   jax.ShapeDtypeStruct((1, 512), F32), jax.ShapeDtypeStruct((1, 512), F32),
                   jax.ShapeDtypeStruct((1, LANES), F32)],
        scratch_shapes=[pltpu.VMEM((1, LANES), F32), pltpu.VMEM((tm, LANES), F32)],
        compiler_params=_params(("arbitrary", "arbitrary"), VMEM_LIMIT_WIDE),
    )(raw, dqa, dkae, dvae, dqb, dkb, dvb, fl, bf_row, gqa, gka, gqb, gkb)


IN_PAD = 304


def _local_step(x, tgt, w_in_t, rest, g1, b_forget, qna, kna, sinks, qnb, knb, g2,
                tm=512, bt=1024, btf=1024, tq=4096, wk=4096, wkb=8192, distributed=False):
    nb, s, _ = x.shape
    t = nb * s
    x2, tgt2 = x.reshape(t, D_MODEL), tgt.reshape(t, D_MODEL)
    g1r, g2r = g1.reshape(1, D_MODEL), g2.reshape(1, D_MODEL)
    gqa, gka = jnp.tile(qna, 8).reshape(1, 512), jnp.tile(kna, 2).reshape(1, 128)
    gqb, gkb = jnp.tile(qnb, 8).reshape(1, 512), jnp.tile(knb, 8).reshape(1, 512)
    bf_row = jnp.pad(b_forget, (0, LANES - 8)).reshape(1, LANES)
    sink_row = jnp.pad(sinks, (0, LANES - 8)).reshape(1, LANES)
    w_main_t = w_in_t[0:MAIN_W]
    w_f_t = jnp.pad(w_in_t[MAIN_W:IN_W], ((0, LANES - 8), (0, 0)))

    xn = _xnorm(x2, g1r, 2 * tm)
    raw, fl, qa, kae, vae, q_aug, k_aug, v_aug = _norm_proj(xn, w_main_t, w_f_t, gqa, gka, gqb, gkb, bf_row, s, tm)
    ma, lse_a = _swa_fwd(qa, kae, vae, sink_row, nb, s, tq)
    if distributed:
        mb, ql_aug, w_out, w_up, w_down, w_up_t = _fox_fwd(q_aug, k_aug, v_aug, nb, s, btf, shards=rest)
    else:
        mb, ql_aug = _fox_fwd(q_aug, k_aug, v_aug, nb, s, btf)
        w_out, w_up, w_down, w_up_t = rest
    w_out, w_down = w_out.reshape(D_MODEL, D_MODEL), w_down.reshape(D_FF, D_MODEL)
    h, hn, hid, dy, dyb, loss_acc = _mlp_fwd(x2, ma, mb, tgt2, w_out, g2r, w_up, w_down, tm)

    du, dh, dhb, dma, do_aug, dla, gg2 = _mlp_bwd(dy, hid, h, ma, mb, w_down, w_up_t.reshape(D_FF, D_MODEL), w_out, g2r, tm)
    g_down = _wgrad(hid, dyb, "wgrad_down", 512, 1024, wkb, BF16).reshape(N_DEV, 512, D_MODEL)
    g_up = _wgrad(hn, du, "wgrad_up", 1024, 512, wkb, BF16, col_blocks=True)
    g_out = jnp.concatenate([_wgrad(ma, dhb, "wgrad_out_a", 512, 1024, wk, BF16),
                             _wgrad(mb, dhb, "wgrad_out_b", 512, 1024, wk, BF16)], axis=0).reshape(N_DEV, 128, D_MODEL)

    dqa, dkae, dvae, dsink = _swa_bwd(qa, kae, vae, dma, sink_row, lse_a, dla, nb, s, tq)
    fox = _fox_bwd(ql_aug, k_aug, v_aug, do_aug, nb, s, bt, exch=(g_out, g_up, g_down) if distributed else ())
    dqb, dkb, dvb = fox[:3]
    if distributed:
        g_out, g_up, g_down = fox[3:]
    dproj, dfb, ggqa, ggka, ggqb, ggkb, gbf = _proj_bwd(raw, dqa, dkae, dvae, dqb, dkb, dvb, fl, bf_row, gqa, gka, gqb, gkb, nb, s, tm)
    g_main_t, g_gate_t = _wgrad(dproj, xn, "wgrad_in", 768, 1024, wk, a2=dfb)
    g_in_t = jnp.concatenate([g_main_t, g_gate_t[0:8]], axis=0)
    grad_x, gg1 = _xgrad(dproj, dfb, x2, dh, w_main_t, w_f_t, g1r, tm)

    small = (gg1.reshape(D_MODEL), gbf[0, 0:8], ggqa.reshape(8, 64).sum(0), ggka.reshape(2, 64).sum(0),
             dsink.sum(0)[:, 0:2, 0].reshape(8), ggqb.reshape(8, 64).sum(0), ggkb.reshape(8, 64).sum(0),
             gg2.reshape(D_MODEL))
    return loss_acc[0, 0], grad_x.reshape(nb, s, D_MODEL), g_in_t, g_out, g_up, g_down, small


def _all_gather(shard):
    x_ref = jax.new_ref(shard, memory_space=pltpu.MemorySpace.HBM)
    out_ref = jax.empty_ref(jax.ShapeDtypeStruct((N_DEV,) + shard.shape, shard.dtype), memory_space=pltpu.MemorySpace.HBM)

    @pl.kernel(mesh=plsc.ScalarSubcoreMesh(axis_name="sequencer", num_cores=1), name="gather_w_in",
               scratch_types=(pltpu.SemaphoreType.DMA((N_SEM,)), pltpu.SemaphoreType.DMA((N_SEM,)), pltpu.SemaphoreType.DMA((1,))),
               compiler_params=pltpu.CompilerParams(collective_id=1))
    def launch(send_sems, recv_sems, local_sems):
        x, y, c = lax.axis_index("x"), lax.axis_index("y"), lax.axis_index("c")
        barrier = pltpu.get_barrier_semaphore()
        peers = [(x, y, 1 - c), (1 - x, y, c), (x, 1 - y, c), (1 - x, 1 - y, c)]
        for peer in peers:
            pl.semaphore_signal(barrier, inc=1, device_id=peer, device_id_type=MESH)
        pl.semaphore_wait(barrier, len(peers))
        start, forward, finish = _gather_steps([(x_ref, out_ref)], send_sems, recv_sems, local_sems)
        start()
        forward()
        finish()

    launch()
    return out_ref[...]


def _exchange(name, collective_id, *arrays):
    n_ex = len(arrays)
    srcs = [jax.new_ref(a, memory_space=pltpu.MemorySpace.HBM) for a in arrays]
    dsts = [jax.empty_ref(jax.ShapeDtypeStruct(a.shape, a.dtype), memory_space=pltpu.MemorySpace.HBM) for a in arrays]

    @pl.kernel(mesh=plsc.ScalarSubcoreMesh(axis_name="sequencer", num_cores=1), name=name,
               scratch_types=(pltpu.SemaphoreType.DMA((N_SEM * n_ex,)), pltpu.SemaphoreType.DMA((N_SEM * n_ex,)),
                              pltpu.SemaphoreType.DMA((n_ex,))),
               compiler_params=pltpu.CompilerParams(collective_id=collective_id))
    def launch(send_sems, recv_sems, local_sems):
        x, y, c = lax.axis_index("x"), lax.axis_index("y"), lax.axis_index("c")
        barrier = pltpu.get_barrier_semaphore()
        for k in range(1, N_DEV):
            peer = (1 - x if k & 4 else x, 1 - y if k & 2 else y, 1 - c if k & 1 else c)
            pl.semaphore_signal(barrier, inc=1, device_id=peer, device_id_type=MESH)
        pl.semaphore_wait(barrier, N_DEV - 1)
        start, finish = _exchange_steps(list(zip(srcs, dsts)), send_sems, recv_sems, local_sems)
        start()
        finish()

    launch()
    return [d[...] for d in dsts]


def _sum_adamw(recv, w, m, v, tr, name):
    _, r, n = recv.shape

    def body(r_ref, w_ref, m_ref, v_ref, g_ref, d_ref, nm_ref, nv_ref):
        g = r_ref[0].astype(F32)
        for s in range(1, N_DEV):
            g = g + r_ref[s].astype(F32)
        g_ref[...] = g
        nm = ADAM_B1 * m_ref[...] + (1.0 - ADAM_B1) * g
        nv = ADAM_B2 * v_ref[...] + (1.0 - ADAM_B2) * (g * g)
        m_hat = nm / (1.0 - ADAM_B1 ** ADAM_STEP)
        v_hat = nv / (1.0 - ADAM_B2 ** ADAM_STEP)
        d_ref[...] = -ADAM_LR * (m_hat / (jnp.sqrt(v_hat) + ADAM_EPS) + ADAM_WD * w_ref[...])
        nm_ref[...] = nm
        nv_ref[...] = nv

    tile = pl.BlockSpec((tr, n), lambda i: (i, 0))
    shp = jax.ShapeDtypeStruct((r, n), F32)
    return pl.pallas_call(
        body, name=name, grid=(r // tr,),
        in_specs=[pl.BlockSpec((N_DEV, tr, n), lambda i: (0, i, 0)), tile, tile, tile],
        out_specs=[tile, tile, tile, tile], out_shape=[shp, shp, shp, shp],
        compiler_params=_params(("arbitrary",)),
    )(recv, w, m, v)


def _small_rows(g1, bf, qna, kna, sk, qnb, knb, g2, extra=None):
    row2 = jnp.concatenate([bf, qna, kna, sk, qnb, knb])
    rows = [g1, g2, jnp.pad(row2, (0, D_MODEL - row2.shape[0]))]
    if extra is not None:
        rows.append(jnp.pad(extra.reshape(1), (0, D_MODEL - 1)))
    return jnp.pad(jnp.stack(rows), ((0, 8 - len(rows)), (0, 0)))


def _in_rows(w_in_s):
    return jnp.pad(w_in_s.T, ((0, IN_PAD - IN_SHARD), (0, 0)))


def kernel(x, attn_norm_g, w_in, b_forget, q_norm_a, k_norm_a, sink_logits, q_norm_b, k_norm_b, w_out, mlp_norm_g, w_up, w_down, loss_target, m_attn_norm_g, m_w_in, m_b_forget, m_q_norm_a, m_k_norm_a, m_sink_logits, m_q_norm_b, m_k_norm_b, m_w_out, m_mlp_norm_g, m_w_up, m_w_down, v_attn_norm_g, v_w_in, v_b_forget, v_q_norm_a, v_k_norm_a, v_sink_logits, v_q_norm_b, v_k_norm_b, v_w_out, v_mlp_norm_g, v_w_up, v_w_down):
    w_in_r = _in_rows(w_in)
    w_in_t = _all_gather(w_in_r.astype(BF16))[:, 0:IN_SHARD].reshape(IN_W, D_MODEL)
    w_up_b = w_up.astype(BF16)
    rest = (w_out.astype(BF16), w_up_b, w_down.astype(BF16), w_up_b.T)

    loss_part, grad_x, g_in_t, r_out, r_up, r_down, small = _local_step(
        x, loss_target, w_in_t, rest, attn_norm_g, b_forget, q_norm_a, k_norm_a, sink_logits, q_norm_b, k_norm_b, mlp_norm_g,
        distributed=True)

    g_in_blocks = jnp.pad(g_in_t.reshape(N_DEV, IN_SHARD, D_MODEL), ((0, 0), (0, IN_PAD - IN_SHARD), (0, 0))).astype(BF16)
    small_blocks = jnp.broadcast_to(_small_rows(*small, extra=loss_part), (N_DEV, 8, D_MODEL))
    r_in, = _exchange("exchange_w_in", 0, g_in_blocks)
    r_small, = _exchange("exchange_small", 2, small_blocks)

    small_w = _small_rows(attn_norm_g, b_forget, q_norm_a, k_norm_a, sink_logits, q_norm_b, k_norm_b, mlp_norm_g)
    small_m = _small_rows(m_attn_norm_g, m_b_forget, m_q_norm_a, m_k_norm_a, m_sink_logits, m_q_norm_b, m_k_norm_b, m_mlp_norm_g)
    small_v = _small_rows(v_attn_norm_g, v_b_forget, v_q_norm_a, v_k_norm_a, v_sink_logits, v_q_norm_b, v_k_norm_b, v_mlp_norm_g)
    o_in = [a[0:IN_SHARD].T for a in _sum_adamw(r_in, w_in_r, _in_rows(m_w_in), _in_rows(v_w_in), IN_PAD, "adamw_in")]
    o_out = _sum_adamw(r_out, w_out, m_w_out, v_w_out, 128, "adamw_out")
    o_up = _sum_adamw(r_up, w_up, m_w_up, v_w_up, 256, "adamw_up")
    o_down = _sum_adamw(r_down, w_down, m_w_down, v_w_down, 128, "adamw_down")
    o_small = _sum_adamw(r_small, small_w, small_m, small_v, 8, "adamw_small")

    def leaves(i):
        row2 = o_small[i][2]
        return (o_small[i][0], o_in[i], row2[0:8], row2[8:72], row2[72:136], row2[136:144], row2[144:208], row2[208:272],
                o_out[i], o_small[i][1], o_up[i], o_down[i])

    return (o_small[0][3, 0], grad_x, *leaves(0), *leaves(1), *leaves(2), *leaves(3))
```

```python
import functools

import jax
import jax.numpy as jnp
from jax import lax
from jax.experimental import pallas as pl
from jax.experimental.pallas import tpu as pltpu
from jax.experimental.pallas import tpu_sc as plsc

F32 = jnp.float32
BF16 = jnp.bfloat16

D_MODEL = 1024
HEAD_DIM = 64
N_DEV = 8
D_FF = 4096
MAIN_W = 2304
IN_W = 2312
IN_SHARD = 289
WINDOW = 128
EPS = 1e-6
SCALE = 0.125
LOG2E = 1.4426950408889634
LANES = 128
NEG_INF = float("-inf")

ADAM_LR = 0.001
ADAM_B1 = 0.9
ADAM_B2 = 0.999
ADAM_EPS = 1e-08
ADAM_WD = 0.01
ADAM_STEP = 10

VMEM_LIMIT = 56 * 1024 * 1024
VMEM_LIMIT_WIDE = 62 * 1024 * 1024


def _params(sem, vmem=VMEM_LIMIT):
    return pltpu.CompilerParams(dimension_semantics=sem, vmem_limit_bytes=vmem)


def _const_spec(shape):
    nd = len(shape)
    return pl.BlockSpec(shape, lambda *_: (0,) * nd, pipeline_mode=pl.Buffered(1))


def _lane(shape):
    return lax.broadcasted_iota(jnp.int32, shape, len(shape) - 1)


def _head_ones(n):
    r = lax.shift_right_logical(lax.broadcasted_iota(jnp.int32, (n, n), 0), 6)
    c = lax.shift_right_logical(lax.broadcasted_iota(jnp.int32, (n, n), 1), 6)
    return (r == c).astype(BF16)


def _head_sum(v):
    w = v.shape[1]
    vb = v.astype(BF16)
    if w <= 256:
        return jnp.dot(vb, _head_ones(w), preferred_element_type=F32)
    ones = _head_ones(256)
    return jnp.concatenate([jnp.dot(vb[:, s:s + 256], ones, preferred_element_type=F32) for s in range(0, w, 256)], axis=1)


def _head_norm(seg, gain):
    rs = lax.rsqrt(_head_sum(seg * seg) * (1.0 / HEAD_DIM) + EPS)
    return seg * rs * gain


def _head_norm_bwd(seg, gain, d_out):
    rs = lax.rsqrt(_head_sum(seg * seg) * (1.0 / HEAD_DIM) + EPS)
    hat = seg * rs
    gd = d_out * gain
    d_seg = rs * (gd - hat * (_head_sum(gd * hat) * (1.0 / HEAD_DIM)))
    return d_seg, d_out * hat


def _expand_kv(v):
    r = pltpu.roll(v, 64, axis=1)
    lo = _lane(v.shape) < 64
    return jnp.concatenate([jnp.where(lo, v, r), jnp.where(lo, r, v)], axis=1)


def _fold_kv(e4):
    t0 = e4[:, 0:128] + e4[:, 128:256]
    t1 = e4[:, 256:384] + e4[:, 384:512]
    t0 = t0 + pltpu.roll(t0, 64, axis=1)
    t1 = t1 + pltpu.roll(t1, 64, axis=1)
    return jnp.where(_lane(t0.shape) < 64, t0, t1)


def _pick_lane(blk, idx):
    return jnp.sum(jnp.where(_lane(blk.shape) == idx, blk, 0.0), axis=1, keepdims=True)


def _nt(a, b):
    return lax.dot_general(a, b, (((1,), (1,)), ((), ())), preferred_element_type=F32)


def _tn(a, b):
    return lax.dot_general(a, b, (((0,), (0,)), ((), ())), preferred_element_type=F32)


def _xnorm(x2, g1, tm):
    t = x2.shape[0]

    def body(x_ref, g1_ref, xn_ref):
        x = x_ref[...]
        r = lax.rsqrt(jnp.mean(x * x, axis=-1, keepdims=True) + EPS)
        xn_ref[...] = (x * r * g1_ref[...]).astype(BF16)

    tile = pl.BlockSpec((tm, D_MODEL), lambda i: (i, 0))
    return pl.pallas_call(
        body, name="xnorm", grid=(t // tm,), in_specs=[tile, _const_spec((1, D_MODEL))], out_specs=tile,
        out_shape=jax.ShapeDtypeStruct((t, D_MODEL), BF16), compiler_params=_params(("arbitrary",)),
    )(x2, g1)


def _norm_proj(xn, w_main_t, w_f_t, gqa, gka, gqb, gkb, bf_row, s, tm):
    t = xn.shape[0]
    nt = s // tm

    def body(xn_ref, wm_ref, wf_ref, gqa_ref, gka_ref, gqb_ref, gkb_ref, b_ref,
             raw_ref, fl_ref, qa_ref, kae_ref, vae_ref, qo_ref, ko_ref, vo_ref, carry, c_ref):
        @pl.when(lax.rem(pl.program_id(0), nt) == 0)
        def _():
            carry[...] = jnp.zeros_like(carry)

        xn = xn_ref[...]
        proj = _nt(xn, wm_ref[...])
        raw_ref[...] = proj
        fl = _nt(xn, wf_ref[...])
        fl_ref[...] = fl
        qa_ref[...] = _head_norm(proj[:, 0:512], gqa_ref[...]).astype(BF16)
        kae_ref[...] = _expand_kv(_head_norm(proj[:, 512:640], gka_ref[...])).astype(BF16)
        vae_ref[...] = _expand_kv(proj[:, 640:768]).astype(BF16)

        z = fl + b_ref[...]
        e = jnp.exp(-jnp.abs(z))
        u = 1.0 + e
        log1p = jnp.where(u == 1.0, e, jnp.log(u) * (e / (u - 1.0)))
        lf = jnp.minimum(z, 0.0) - log1p
        for r0 in range(0, tm, 256):
            c_ref[r0:r0 + 256, :] = _tri_dot(256, False, lf[r0:r0 + 256]) + carry[...]
            carry[...] = c_ref[pl.ds(r0 + 255, 1), :]
        c2 = c_ref[...] * LOG2E
        qb = _head_norm(proj[:, 768:1280], gqb_ref[...]) * (SCALE * LOG2E)
        kb = _head_norm(proj[:, 1280:1792], gkb_ref[...])
        lane = _lane((tm, LANES))
        for h in range(8):
            j, half = h // 2, h % 2
            pair, blk = slice(LANES * j, LANES * (j + 1)), slice(LANES * h, LANES * (h + 1))
            feat = _spread3(c2[:, h:h + 1], (tm, LANES), (L_CK, L_CQ))
            q = _put_ones(_head_block(qb[:, pair], half), (L_CK, L_CK + 1, L_CK + 2))
            qo_ref[:, blk] = jnp.where((lane >= L_CQ) & (lane < L_CQ + 3), feat, q).astype(BF16)
            k = _put_ones(_head_block(kb[:, pair], half), tuple(range(L_CQ, L_CQ + 6)))
            ko_ref[:, blk] = jnp.where((lane >= L_CK) & (lane < L_CK + 3), -feat, k).astype(BF16)
            v = _head_block(proj[:, 1792 + LANES * j:1792 + LANES * (j + 1)], half)
            vo_ref[:, blk] = _put_ones(v, (L_ONE, L_DELTA, L_DELTA + 1, L_DELTA + 2)).astype(BF16)

    def tile(w):
        return pl.BlockSpec((tm, w), lambda i: (i, 0))

    aug = jax.ShapeDtypeStruct((t, 8 * LANES), BF16)
    return pl.pallas_call(
        body, name="norm_proj", grid=(t // tm,),
        in_specs=[tile(D_MODEL), _const_spec((MAIN_W, D_MODEL)), _const_spec((LANES, D_MODEL)),
                  _const_spec((1, 512)), _const_spec((1, 128)), _const_spec((1, 512)), _const_spec((1, 512)),
                  _const_spec((1, LANES))],
        out_specs=[tile(MAIN_W), tile(LANES), tile(512), tile(256), tile(256)] + [tile(8 * LANES)] * 3,
        out_shape=[jax.ShapeDtypeStruct((t, MAIN_W), F32),
                   jax.ShapeDtypeStruct((t, LANES), F32), jax.ShapeDtypeStruct((t, 512), BF16),
                   jax.ShapeDtypeStruct((t, 256), BF16), jax.ShapeDtypeStruct((t, 256), BF16), aug, aug, aug],
        scratch_shapes=[pltpu.VMEM((1, LANES), F32), pltpu.VMEM((tm, LANES), F32)],
        compiler_params=_params(("arbitrary",)),
    )(xn, w_main_t, w_f_t, gqa, gka, gqb, gkb, bf_row)


def _tri_dot(n, upper, v):
    r = lax.broadcasted_iota(jnp.int32, (n, n), 0)
    c = lax.broadcasted_iota(jnp.int32, (n, n), 1)
    tri = ((c >= r) if upper else (c <= r)).astype(BF16)
    hi = v.astype(BF16)
    mid = (v - hi.astype(F32)).astype(BF16)
    lo = (v - hi.astype(F32) - mid.astype(F32)).astype(BF16)
    return (jnp.dot(tri, hi, preferred_element_type=F32) + jnp.dot(tri, mid, preferred_element_type=F32)
            + jnp.dot(tri, lo, preferred_element_type=F32))


def _slope(p, hh):
    out = jnp.float32(2.0 ** -(2 * 3 + hh + 1))
    for pp in (2, 1, 0):
        out = jnp.where(p == pp, jnp.float32(2.0 ** -(2 * pp + hh + 1)), out)
    return out


def _swa_windows(ref, i, tq):
    nsub = tq // WINDOW
    cur = ref[pl.ds(pl.multiple_of(i * tq, tq), tq), :].reshape(nsub, WINDOW, LANES)
    first = ref[pl.ds(pl.multiple_of(jnp.maximum(i * tq - WINDOW, 0), WINDOW), WINDOW), :].reshape(1, WINDOW, LANES)
    return jnp.concatenate([jnp.concatenate([first, cur[0:nsub - 1]], axis=0), cur], axis=1)


def _both_heads(x3, lo):
    zero = jnp.zeros_like(x3)
    return jnp.concatenate([jnp.where(lo, x3, zero), jnp.where(lo, zero, x3)], axis=0)


def _swa_sinks(sink_ref, p, nsub):
    is_a = lax.broadcasted_iota(jnp.int32, (2 * nsub, 1, 1), 0) < nsub
    sinks = sink_ref[...]
    return jnp.where(is_a, _pick_lane(sinks, 2 * p).reshape(1, 1, 1), _pick_lane(sinks, 2 * p + 1).reshape(1, 1, 1))


def _swa_bias(p, i, nsub, keys_first):
    shape = (1, 2 * WINDOW, WINDOW) if keys_first else (1, WINDOW, 2 * WINDOW)
    qi = lax.broadcasted_iota(jnp.int32, shape, 2 if keys_first else 1)
    ki = lax.broadcasted_iota(jnp.int32, shape, 1 if keys_first else 2)
    dist = qi + WINDOW - ki
    band = (dist >= 0) & (dist < WINDOW)
    tiles = []
    for hh in range(2):
        bias = jnp.where(band, -_slope(p, hh) * dist.astype(F32), NEG_INF)
        tiles += [jnp.where((i == 0) & (ki < WINDOW), NEG_INF, bias)] + [bias] * (nsub - 1)
    return jnp.concatenate(tiles, axis=0)


def _swa_fwd(qa, kae, vae, sink_row, nb, s, tq):
    t = qa.shape[0]
    nq = s // tq
    nsub = tq // WINDOW

    def body(q_ref, k_ref, v_ref, sink_ref, o_ref, lse_ref):
        p, i = pl.program_id(1), pl.program_id(2)
        lo = _lane((1, 1, LANES)) < 64
        kk, vv = _swa_windows(k_ref, i, tq), _swa_windows(v_ref, i, tq)
        qs = (q_ref[...].astype(F32) * SCALE).astype(BF16).reshape(nsub, WINDOW, LANES)
        q8 = _both_heads(qs, lo)
        s8 = jnp.einsum("bqd,bkd->bqk", q8, jnp.concatenate([kk, kk], axis=0), preferred_element_type=F32)
        sink = _swa_sinks(sink_ref, p, nsub)
        s8 = s8 + _swa_bias(p, i, nsub, False)
        m = jnp.maximum(jnp.max(s8, axis=2, keepdims=True), sink)
        e = jnp.exp(s8 - m)
        den = jnp.sum(e, axis=2, keepdims=True) + jnp.exp(sink - m)
        pr = (e * (1.0 / den)).astype(BF16)
        o8 = jnp.einsum("bqk,bkd->bqd", pr, jnp.concatenate([vv, vv], axis=0), preferred_element_type=F32)
        lse8 = m + jnp.log(den)
        o_ref[...] = jnp.where(lo, o8[0:nsub], o8[nsub:]).astype(BF16).reshape(tq, LANES)
        lse_ref[...] = jnp.where(lo, lse8[0:nsub], lse8[nsub:]).reshape(tq, LANES)

    return pl.pallas_call(
        body, name="swa_fwd", grid=(nb, 4, nq),
        in_specs=[pl.BlockSpec((tq, LANES), lambda b, p, i: (b * nq + i, p)),
                  pl.BlockSpec((s, LANES), lambda b, p, i: (b, lax.shift_right_logical(p, 1))),
                  pl.BlockSpec((s, LANES), lambda b, p, i: (b, lax.shift_right_logical(p, 1))),
                  pl.BlockSpec((1, LANES), lambda b, p, i: (0, 0))],
        out_specs=[pl.BlockSpec((tq, LANES), lambda b, p, i: (b * nq + i, p)),
                   pl.BlockSpec((None, tq, LANES), lambda b, p, i: (p, b * nq + i, 0))],
        out_shape=[jax.ShapeDtypeStruct((t, 512), BF16), jax.ShapeDtypeStruct((4, t, LANES), F32)],
        compiler_params=_params(("arbitrary", "arbitrary", "arbitrary")),
    )(qa, kae, vae, sink_row)


def _swa_bwd(qa, kae, vae, do_a, sink_row, lse, delta, nb, s, tq):
    t = qa.shape[0]
    nq = s // tq
    nsub = tq // WINDOW

    def body(q_ref, do_ref, k_ref, v_ref, sink_ref, lse_ref, dl_ref, dq_ref, dk_ref, dv_ref, ds_ref):
        p, i = pl.program_id(1), pl.program_id(2)

        @pl.when(i == 0)
        def _():
            ds_ref[...] = jnp.zeros_like(ds_ref)

        lo = _lane((1, 1, LANES)) < 64
        kk, vv = _swa_windows(k_ref, i, tq), _swa_windows(v_ref, i, tq)
        kks = (kk.astype(F32) * SCALE).astype(BF16)
        k8, v8 = jnp.concatenate([kks, kks], axis=0), jnp.concatenate([vv, vv], axis=0)
        q8 = _both_heads(q_ref[...].reshape(nsub, WINDOW, LANES), lo)
        do8 = _both_heads(do_ref[...].reshape(nsub, WINDOW, LANES), lo)
        cur = pl.multiple_of(i * tq, tq)
        sub = lax.broadcasted_iota(jnp.int32, (WINDOW, WINDOW), 0)
        lse_t = [lse_ref[u * WINDOW:(u + 1) * WINDOW, :].T for u in range(nsub)]
        dl_t = [dl_ref[u * WINDOW:(u + 1) * WINDOW, :].T for u in range(nsub)]
        lse8 = jnp.concatenate([t_[64 * hh:64 * hh + 1, :].reshape(1, 1, WINDOW) for hh in range(2) for t_ in lse_t], axis=0)
        dl8 = jnp.concatenate([jnp.sum(jnp.where(sub == 2 * p + hh, t_, 0.0), axis=0, keepdims=True).reshape(1, 1, WINDOW)
                               for hh in range(2) for t_ in dl_t], axis=0)
        sink = _swa_sinks(sink_ref, p, nsub)
        st = jnp.einsum("bkd,bqd->bkq", k8, q8, preferred_element_type=F32) + _swa_bias(p, i, nsub, True) - lse8
        pt = jnp.exp(st)
        dpt = jnp.einsum("bkd,bqd->bkq", v8, do8, preferred_element_type=F32)
        dst = pt * (dpt - dl8)
        ptb, dstb = pt.astype(BF16), dst.astype(BF16)
        dv8 = jnp.einsum("bkq,bqd->bkd", ptb, do8, preferred_element_type=F32)
        dk8 = jnp.einsum("bkq,bqd->bkd", dstb, q8, preferred_element_type=F32) * SCALE
        dq8 = jnp.einsum("bkq,bkd->bqd", dstb, k8, preferred_element_type=F32)
        dq_ref[...] = jnp.where(lo, dq8[0:nsub], dq8[nsub:]).reshape(tq, LANES)

        psd = jnp.exp(sink - lse8) * dl8
        row_h = lax.broadcasted_iota(jnp.int32, (8, LANES), 0)
        for hh in range(2):
            tot = jnp.sum(jnp.sum(psd[hh * nsub:(hh + 1) * nsub], axis=2, keepdims=True), axis=0, keepdims=True)
            ds_ref[...] += jnp.where(row_h == hh, -tot.reshape(1, 1), 0.0)

        prev = pl.multiple_of(jnp.maximum(i * tq - WINDOW, 0), WINDOW)
        for g8, g_ref in ((dk8, dk_ref), (dv8, dv_ref)):
            g4 = g8[0:nsub] + g8[nsub:]
            own, before = g4[:, WINDOW:, :], g4[:, 0:WINDOW, :]
            shifted = jnp.concatenate([before[1:nsub], jnp.zeros((1, WINDOW, LANES), F32)], axis=0)
            g_ref[pl.ds(cur, tq), :] = (own + shifted).reshape(tq, LANES)
            g_ref[pl.ds(prev, WINDOW), :] += before[0]

    return pl.pallas_call(
        body, name="swa_bwd", grid=(nb, 4, nq),
        in_specs=[pl.BlockSpec((tq, LANES), lambda b, p, i: (b * nq + i, p)),
                  pl.BlockSpec((tq, LANES), lambda b, p, i: (b * nq + i, p)),
                  pl.BlockSpec((s, LANES), lambda b, p, i: (b, lax.shift_right_logical(p, 1))),
                  pl.BlockSpec((s, LANES), lambda b, p, i: (b, lax.shift_right_logical(p, 1))),
                  pl.BlockSpec((1, LANES), lambda b, p, i: (0, 0)),
                  pl.BlockSpec((None, tq, LANES), lambda b, p, i: (p, b * nq + i, 0)),
                  pl.BlockSpec((tq, LANES), lambda b, p, i: (b * nq + i, 0))],
        out_specs=[pl.BlockSpec((tq, LANES), lambda b, p, i: (b * nq + i, p)),
                   pl.BlockSpec((s, LANES), lambda b, p, i: (b, p)),
                   pl.BlockSpec((s, LANES), lambda b, p, i: (b, p)),
                   pl.BlockSpec((None, None, 8, LANES), lambda b, p, i: (b, p, 0, 0))],
        out_shape=[jax.ShapeDtypeStruct((t, 512), F32), jax.ShapeDtypeStruct((t, 512), F32),
                   jax.ShapeDtypeStruct((t, 512), F32), jax.ShapeDtypeStruct((nb, 4, 8, LANES), F32)],
        compiler_params=_params(("arbitrary", "arbitrary", "arbitrary")),
    )(qa, do_a, kae, vae, sink_row, lse, delta)


MESH = pl.DeviceIdType.MESH
ANY = pl.BlockSpec(memory_space=pl.ANY)
N_SEM = 7


def _gather_steps(pairs, send_sems, recv_sems, local_sems):
    x, y, c = lax.axis_index("x"), lax.axis_index("y"), lax.axis_index("c")
    me, sibling = (x, y, c), (x, y, 1 - c)
    chips = [(1 - x, y), (x, 1 - y), (1 - x, 1 - y)]
    mine, first, passed, landed, last = [], [], [], [], []
    for a, (x_ref, out_ref) in enumerate(pairs):
        def slot(px, py, pc, out_ref=out_ref):
            return out_ref.at[4 * px + 2 * py + pc]

        def copy(k, block, to, src=None, a=a, slot=slot):
            return pltpu.make_async_remote_copy(
                src_ref=slot(*block) if src is None else src, dst_ref=slot(*block),
                send_sem=send_sems.at[N_SEM * a + k], recv_sem=recv_sems.at[N_SEM * a + k], device_id=to, device_id_type=MESH)

        mine.append(pltpu.make_async_copy(x_ref, slot(*me), local_sems.at[a]))
        first += [copy(0, me, sibling, src=x_ref)] + [copy(1 + j, me, (*chip, c), src=x_ref) for j, chip in enumerate(chips)]
        passed += [copy(4 + j, (*chip, c), sibling) for j, chip in enumerate(chips)]
        landed += [copy(1 + j, (*chip, c), me) for j, chip in enumerate(chips)]
        last += [copy(0, sibling, me)] + [copy(4 + j, (*chip, 1 - c), me) for j, chip in enumerate(chips)]

    def start():
        for cp in mine + first:
            cp.start()

    def forward():
        for arrived, onward in zip(landed, passed):
            arrived.wait_recv()
            onward.start()

    def finish():
        for cp in last:
            cp.wait_recv()
        for cp in first + passed:
            cp.wait_send()
        for cp in mine:
            cp.wait()

    return start, forward, finish


def _exchange_steps(pairs, send_sems, recv_sems, local_sems):
    x, y, c = lax.axis_index("x"), lax.axis_index("y"), lax.axis_index("c")
    my_id = 4 * x + 2 * y + c
    local, remote = [], []
    for a, (src, dst) in enumerate(pairs):
        local.append(pltpu.make_async_copy(src.at[my_id], dst.at[my_id], local_sems.at[a]))
        for k in range(1, N_DEV):
            px = 1 - x if k & 4 else x
            py = 1 - y if k & 2 else y
            pc = 1 - c if k & 1 else c
            remote.append(pltpu.make_async_remote_copy(
                src_ref=src.at[4 * px + 2 * py + pc], dst_ref=dst.at[my_id],
                send_sem=send_sems.at[N_SEM * a + k - 1], recv_sem=recv_sems.at[N_SEM * a + k - 1],
                device_id=(px, py, pc), device_id_type=MESH))

    def start():
        for cp in local + remote:
            cp.start()

    def finish():
        for cp in remote:
            cp.wait_recv()
        for cp in remote:
            cp.wait_send()
        for cp in local:
            cp.wait()

    return start, finish


L_ONE = 64
L_CK = 65
L_CQ = 68
L_LSE = 71
L_DELTA = 74


def _head_block(pair, half):
    y = pair if half == 0 else pltpu.roll(pair, 64, axis=1)
    return jnp.where(_lane(pair.shape) < 64, y, 0.0)


def _put3(blk, lane0, col):
    lane = _lane(blk.shape)
    hi = col.astype(BF16).astype(F32)
    mid = (col - hi).astype(BF16).astype(F32)
    lo = (col - hi - mid).astype(BF16).astype(F32)
    return jnp.where(lane == lane0, hi, jnp.where(lane == lane0 + 1, mid, jnp.where(lane == lane0 + 2, lo, blk)))


def _spread3(col, shape, lane0s):
    lane = _lane(shape)
    hi = col.astype(BF16).astype(F32)
    mid = (col - hi).astype(BF16).astype(F32)
    lo = (col - hi - mid).astype(BF16).astype(F32)

    def at(k):
        return functools.reduce(jnp.logical_or, [lane == ln + k for ln in lane0s])

    return jnp.where(at(0), hi, jnp.where(at(1), mid, jnp.where(at(2), lo, 0.0)))


def _put_ones(blk, lanes):
    lane = _lane(blk.shape)
    hit = functools.reduce(jnp.logical_or, [lane == ln for ln in lanes])
    return jnp.where(hit, 1.0, blk)


def _to_pairs(ref):
    out = []
    for j in range(4):
        a, b = ref[:, 2 * LANES * j:2 * LANES * j + LANES], ref[:, 2 * LANES * j + LANES:2 * LANES * (j + 1)]
        out.append(jnp.where(_lane(a.shape) < 64, a, pltpu.roll(b, 64, axis=1)))
    return jnp.concatenate(out, axis=1)


def _fox_fwd(q_aug, k_aug, v_aug, nb, s, bt, shards=()):
    t = q_aug.shape[0]
    nq = s // bt
    n_in, n_sh = 3, len(shards)

    def body(*refs):
        q_ref, k_ref, v_ref = refs[:n_in]
        o_ref, ql_ref = refs[n_in + n_sh:n_in + n_sh + 2]
        if shards:
            srcs, dsts = refs[n_in:n_in + n_sh], refs[n_in + n_sh + 2:n_in + 2 * n_sh + 2]
            start, forward, finish = _gather_steps(list(zip(srcs, dsts)), *refs[n_in + 2 * n_sh + 2:])
            step = (pl.program_id(0) * 4 + pl.program_id(1)) * nq + pl.program_id(2)
            pl.when(step == 0)(start)
            pl.when(step == nb * 3 * nq)(forward)
        i = pl.program_id(2)
        sls = [slice(LANES * hh, LANES * (hh + 1)) for hh in range(2)]
        qhs = [q_ref[:, sl] for sl in sls]

        def update(m, acc, qrows, start, size, sl, causal):
            sc = _nt(qrows, k_ref[pl.ds(start, size), sl])
            if causal:
                row = lax.broadcasted_iota(jnp.int32, sc.shape, 0)
                col = lax.broadcasted_iota(jnp.int32, sc.shape, 1)
                sc = jnp.where(row >= col, sc, NEG_INF)
            m_new = jnp.maximum(m, jnp.max(sc, axis=1, keepdims=True))
            pr = jnp.exp2(sc - m_new).astype(BF16)
            acc = jnp.exp2(m - m_new) * acc + jnp.dot(pr, v_ref[pl.ds(start, size), sl], preferred_element_type=F32)
            return m_new, acc

        def blk(kb_i, carry):
            start = pl.multiple_of(kb_i * bt, bt)
            return tuple(update(m, acc, qh, start, bt, sl, False) for (m, acc), qh, sl in zip(carry, qhs, sls))

        def diag_blk(carry):
            start = pl.multiple_of(i * bt, bt)
            return tuple(update(m, acc, qh, start, bt, sl, True) for (m, acc), qh, sl in zip(carry, qhs, sls))

        init = tuple((jnp.full((bt, 1), NEG_INF, F32), jnp.zeros((bt, LANES), F32)) for _ in range(2))
        carry = lax.fori_loop(0, i, blk, init)
        outs = []
        for (m, acc), qh, sl in zip(diag_blk(carry), qhs, sls):
            l = acc[:, L_ONE:L_ONE + 1]
            outs.append(acc * (1.0 / l))
            ql_ref[:, sl] = _put3(qh.astype(F32), L_LSE, -(m + jnp.log(l) * LOG2E)).astype(BF16)
        o_ref[...] = jnp.where(_lane((1, LANES)) < 64, outs[0], pltpu.roll(outs[1], 64, axis=1)).astype(BF16)
        if shards:
            pl.when(step == nb * 4 * nq - 1)(finish)

    in_specs = [pl.BlockSpec((bt, 2 * LANES), lambda b, j, i: (b * nq + i, j)),
                pl.BlockSpec((s, 2 * LANES), lambda b, j, i: (b, j)),
                pl.BlockSpec((s, 2 * LANES), lambda b, j, i: (b, j))]
    out_specs = [pl.BlockSpec((bt, LANES), lambda b, j, i: (b * nq + i, j)),
                 pl.BlockSpec((bt, 2 * LANES), lambda b, j, i: (b * nq + i, j))]
    out_shape = [jax.ShapeDtypeStruct((t, 512), BF16), jax.ShapeDtypeStruct((t, 8 * LANES), BF16)]
    args, scratch = [q_aug, k_aug, v_aug, *shards], []
    if shards:
        in_specs += [ANY] * n_sh
        out_specs += [ANY] * n_sh
        out_shape += [jax.ShapeDtypeStruct((N_DEV,) + sh.shape, sh.dtype) for sh in shards]
        scratch = [pltpu.SemaphoreType.DMA((N_SEM * n_sh,)), pltpu.SemaphoreType.DMA((N_SEM * n_sh,)),
                   pltpu.SemaphoreType.DMA((n_sh,))]
    return pl.pallas_call(
        body, name="fox_fwd", grid=(nb, 4, nq), in_specs=in_specs, out_specs=out_specs, out_shape=out_shape,
        scratch_shapes=scratch, compiler_params=_params(("arbitrary", "arbitrary", "arbitrary")),
    )(*args)


def _fox_bwd(ql_aug, k_aug, v_aug, do_aug, nb, s, bt, exch=()):
    t = ql_aug.shape[0]
    nk = s // bt
    n_in, n_out, n_ex = 4, 3, len(exch)

    def body(*refs):
        q_ref, do_ref, k_ref, v_ref = refs[:n_in]
        dq_ref, dk_ref, dv_ref = refs[n_in + n_ex:n_in + n_ex + n_out]
        if exch:
            srcs = refs[n_in:n_in + n_ex]
            dsts = refs[n_in + n_ex + n_out:n_in + 2 * n_ex + n_out]
            start, finish = _exchange_steps(list(zip(srcs, dsts)), *refs[n_in + 2 * n_ex + n_out:])
            step = (pl.program_id(0) * 4 + pl.program_id(1)) * nk + pl.program_id(2)
            pl.when(step == 0)(start)
        kb_i = pl.program_id(2)

        @pl.when(kb_i == 0)
        def _():
            dq_ref[...] = jnp.zeros_like(dq_ref)

        row = lax.broadcasted_iota(jnp.int32, (bt, bt), 0)
        col = lax.broadcasted_iota(jnp.int32, (bt, bt), 1)
        sls = [slice(LANES * hh, LANES * (hh + 1)) for hh in range(2)]
        khs, vhs = [k_ref[:, sl] for sl in sls], [v_ref[:, sl] for sl in sls]

        def blk(qi, carry, diag):
            start = pl.multiple_of(qi * bt, bt)
            new = []
            for (dk_a, dv_a), kh, vh, sl in zip(carry, khs, vhs, sls):
                qblk, doblk = q_ref[pl.ds(start, bt), sl], do_ref[pl.ds(start, bt), sl]
                st = _nt(kh, qblk)
                if diag:
                    pt = jnp.where(col >= row, jnp.exp2(jnp.where(col >= row, st, 0.0)), 0.0)
                else:
                    pt = jnp.exp2(st)
                dst = pt * _nt(vh, doblk)
                ptb, dstb = pt.astype(BF16), dst.astype(BF16)
                dv_a = dv_a + jnp.dot(ptb, doblk, preferred_element_type=F32)
                dk_a = dk_a + jnp.dot(dstb, qblk, preferred_element_type=F32)
                dq_ref[pl.ds(start, bt), sl] += _tn(dstb, kh)
                new.append((dk_a, dv_a))
            return tuple(new)

        zero = jnp.zeros((bt, LANES), F32)
        carry = blk(kb_i, ((zero, zero), (zero, zero)), True)
        carry = lax.fori_loop(kb_i + 1, nk, lambda qi, c: blk(qi, c, False), carry)
        for (dk_acc, dv_acc), sl in zip(carry, sls):
            dk_ref[:, sl] = dk_acc
            dv_ref[:, sl] = dv_acc
        if exch:
            pl.when(step == nb * 4 * nk - 1)(finish)

    scratch = []
    if exch:
        scratch = [pltpu.SemaphoreType.DMA((N_SEM * n_ex,)), pltpu.SemaphoreType.DMA((N_SEM * n_ex,)),
                   pltpu.SemaphoreType.DMA((n_ex,))]
    whole = pl.BlockSpec((s, 2 * LANES), lambda b, j, kb_i: (b, j))
    tile = pl.BlockSpec((bt, 2 * LANES), lambda b, j, kb_i: (b * nk + kb_i, j))
    shp = jax.ShapeDtypeStruct((t, 8 * LANES), F32)
    return pl.pallas_call(
        body, name="fox_bwd", grid=(nb, 4, nk),
        in_specs=[whole, whole, tile, tile] + [ANY] * n_ex,
        out_specs=[whole, tile, tile] + [ANY] * n_ex,
        out_shape=[shp, shp, shp] + [jax.ShapeDtypeStruct(e.shape, e.dtype) for e in exch],
        scratch_shapes=scratch, compiler_params=_params(("arbitrary", "arbitrary", "arbitrary")),
    )(ql_aug, do_aug, k_aug, v_aug, *exch)


FF_BLK = D_FF // N_DEV


def _mlp_fwd(x2, ma, mb, tgt, w_out, g2, w_up, w_down, tm):
    t = x2.shape[0]

    def body(x_ref, ma_ref, mb_ref, tg_ref, wo_ref, g2_ref, wu_ref, wd_ref,
             h_ref, hn_ref, hid_ref, dy_ref, dyb_ref, loss_ref):
        @pl.when(pl.program_id(0) == 0)
        def _():
            loss_ref[...] = jnp.zeros_like(loss_ref)

        h = (x_ref[...] + jnp.dot(ma_ref[...], wo_ref[0:512, :], preferred_element_type=F32)
             + jnp.dot(mb_ref[...], wo_ref[512:1024, :], preferred_element_type=F32))
        h_ref[...] = h
        r = lax.rsqrt(jnp.mean(h * h, axis=-1, keepdims=True) + EPS)
        hn = (h * r * g2_ref[...]).astype(BF16)
        hn_ref[...] = hn
        for d in range(N_DEV):
            u = jnp.maximum(jnp.dot(hn, wu_ref[d], preferred_element_type=F32), 0.0)
            hid_ref[:, FF_BLK * d:FF_BLK * (d + 1)] = (u * u).astype(BF16)
        y = h + jnp.dot(hid_ref[...], wd_ref[...], preferred_element_type=F32)
        err = y - tg_ref[...]
        dy = err * (1.0 / D_MODEL)
        dy_ref[...] = dy
        dyb_ref[...] = dy.astype(BF16)
        part =0.5 * jnp.sum(jnp.sum(err * err, axis=1, keepdims=True) * (1.0 / D_MODEL), axis=0, keepdims=True)
        loss_ref[...] += part

    def tile(w):
        return pl.BlockSpec((tm, w), lambda i: (i, 0))

    return pl.pallas_call(
        body, name="mlp_fwd", grid=(t // tm,),
        in_specs=[tile(D_MODEL), tile(512), tile(512), tile(D_MODEL), _const_spec((D_MODEL, D_MODEL)),
                  _const_spec((1, D_MODEL)), _const_spec((N_DEV, D_MODEL, FF_BLK)), _const_spec((D_FF, D_MODEL))],
        out_specs=[tile(D_MODEL), tile(D_MODEL), tile(D_FF), tile(D_MODEL), tile(D_MODEL),
                   pl.BlockSpec((8, LANES), lambda i: (0, 0))],
        out_shape=[jax.ShapeDtypeStruct((t, D_MODEL), F32), jax.ShapeDtypeStruct((t, D_MODEL), BF16),
                   jax.ShapeDtypeStruct((t, D_FF), BF16), jax.ShapeDtypeStruct((t, D_MODEL), F32),
                   jax.ShapeDtypeStruct((t, D_MODEL), BF16), jax.ShapeDtypeStruct((8, LANES), F32)],
        compiler_params=_params(("arbitrary",)),
    )(x2, ma, mb, tgt, w_out, g2, w_up, w_down)


def _mlp_bwd(dy, hid, h, ma, mb, w_down, w_up_t, w_out, g2, tm):
    t = dy.shape[0]

    def body(dy_ref, hid_ref, h_ref, ma_ref, mb_ref, wd_ref, wut_ref, wo_ref, g2_ref,
             du_ref, dh_ref, dhb_ref, dma_ref, dob_ref, dla_ref, gg_ref):
        @pl.when(pl.program_id(0) == 0)
        def _():
            gg_ref[...] = jnp.zeros_like(gg_ref)

        dy = dy_ref[...]
        d_hid = _nt(dy.astype(BF16), wd_ref[...])
        du = (d_hid * (2.0 * jnp.sqrt(hid_ref[...].astype(F32)))).astype(BF16)
        du_ref[...] = du
        d_hn = jnp.dot(du, wut_ref[...], preferred_element_type=F32)
        h = h_ref[...]
        r = lax.rsqrt(jnp.mean(h * h, axis=-1, keepdims=True) + EPS)
        hat = h * r
        gd = d_hn * g2_ref[...]
        dh = dy + r * (gd - hat * jnp.mean(gd * hat, axis=-1, keepdims=True))
        gg_ref[...] += jnp.sum(d_hn * hat, axis=0, keepdims=True)
        dh_ref[...] = dh
        dhb = dh.astype(BF16)
        dhb_ref[...] = dhb
        dm = _nt(dhb, wo_ref[...]).astype(BF16)
        dma, dmb = dm[:, 0:512], dm[:, 512:1024]
        dma_ref[...] = dma
        sel = (lax.shift_right_logical(lax.broadcasted_iota(jnp.int32, (512, LANES), 0), 6)
               == lax.broadcasted_iota(jnp.int32, (512, LANES), 1)).astype(BF16)
        dla_ref[...] = jnp.dot((dma.astype(F32) * ma_ref[...].astype(F32)).astype(BF16), sel, preferred_element_type=F32)
        dmb32 = dmb.astype(F32)
        dlb = jnp.dot((dmb32 * mb_ref[...].astype(F32)).astype(BF16), sel, preferred_element_type=F32)
        for hd in range(8):
            blk = _head_block(dmb32[:, LANES * (hd // 2):LANES * (hd // 2 + 1)], hd % 2)
            dob_ref[:, LANES * hd:LANES * (hd + 1)] = _put3(blk, L_DELTA, -dlb[:, hd:hd + 1]).astype(BF16)

    def tile(w):
        return pl.BlockSpec((tm, w), lambda i: (i, 0))

    return pl.pallas_call(
        body, name="mlp_bwd", grid=(t // tm,),
        in_specs=[tile(D_MODEL), tile(D_FF), tile(D_MODEL), tile(512), tile(512), _const_spec((D_FF, D_MODEL)),
                  _const_spec((D_FF, D_MODEL)), _const_spec((D_MODEL, D_MODEL)), _const_spec((1, D_MODEL))],
        out_specs=[tile(D_FF), tile(D_MODEL), tile(D_MODEL), tile(512), tile(8 * LANES), tile(LANES),
                   pl.BlockSpec((1, D_MODEL), lambda i: (0, 0))],
        out_shape=[jax.ShapeDtypeStruct((t, D_FF), BF16), jax.ShapeDtypeStruct((t, D_MODEL), F32),
                   jax.ShapeDtypeStruct((t, D_MODEL), BF16), jax.ShapeDtypeStruct((t, 512), BF16),
                   jax.ShapeDtypeStruct((t, 8 * LANES), BF16), jax.ShapeDtypeStruct((t, LANES), F32),
                   jax.ShapeDtypeStruct((1, D_MODEL), F32)],
        compiler_params=_params(("arbitrary",), VMEM_LIMIT_WIDE),
    )(dy, hid, h, ma, mb, w_down, w_up_t, w_out, g2)


def _wgrad(a, b, name, bm, bn, tk, out_dtype=F32, col_blocks=False, a2=None):
    t, m = a.shape
    n = b.shape[1]
    bm, bn = min(bm, m), min(bn, n)
    nk = t // tk

    def body(*refs):
        if a2 is None:
            a_ref, b_ref, o_ref, acc = refs
        else:
            a_ref, b_ref, a2_ref, o_ref, o2_ref, acc, acc2 = refs
        i, k = pl.program_id(0), pl.program_id(2)

        @pl.when(k == 0)
        def _():
            acc[...] = jnp.zeros_like(acc)

        acc[...] += _tn(a_ref[...], b_ref[...])

        @pl.when(k == nk - 1)
        def _():
            o_ref[...] = acc[...].astype(out_dtype)

        if a2 is not None:
            @pl.when((i == 0) & (k == 0))
            def _():
                acc2[...] = jnp.zeros_like(acc2)

            @pl.when(i == 0)
            def _():
                acc2[...] += _tn(a2_ref[...], b_ref[...])

            @pl.when((i == 0) & (k == nk - 1))
            def _():
                o2_ref[...] = acc2[...]

    if col_blocks:
        out_spec = pl.BlockSpec((None, bm, bn), lambda i, j, k: (j, i, 0))
        out_shape = jax.ShapeDtypeStruct((n // bn, m, bn), out_dtype)
    else:
        out_spec = pl.BlockSpec((bm, bn), lambda i, j, k: (i, j))
        out_shape = jax.ShapeDtypeStruct((m, n), out_dtype)
    in_specs = [pl.BlockSpec((tk, bm), lambda i, j, k: (k, i)), pl.BlockSpec((tk, bn), lambda i, j, k: (k, j))]
    out_specs, out_shapes, scratch, args = [out_spec], [out_shape], [pltpu.VMEM((bm, bn), F32)], [a, b]
    if a2 is not None:
        m2 = a2.shape[1]
        in_specs.append(pl.BlockSpec((tk, m2), lambda i, j, k: (k, 0)))
        out_specs.append(pl.BlockSpec((m2, n), lambda i, j, k: (0, 0)))
        out_shapes.append(jax.ShapeDtypeStruct((m2, n), F32))
        scratch.append(pltpu.VMEM((m2, n), F32))
        args.append(a2)
    out = pl.pallas_call(
        body, name=name, grid=(m // bm, n // bn, nk), in_specs=in_specs, out_specs=out_specs, out_shape=out_shapes,
        scratch_shapes=scratch, compiler_params=_params(("arbitrary", "arbitrary", "arbitrary")),
    )(*args)
    return out[0] if a2 is None else out


def _proj_bwd(raw, dqa, dkae, dvae, dqb, dkb, dvb, fl, bf_row, x2, dh, w_main_t, w_f_t, g1, gqa, gka, gqb, gkb, nb, s, tm):
    t = x2.shape[0]
    nt = s // tm

    def body(raw_ref, dqa_ref, dkae_ref, dvae_ref, dqb_ref, dkb_ref, dvb_ref, fl_ref, b_ref, x_ref, dh_ref,
             wmt_ref, wft_ref, g1_ref, gqa_ref, gka_ref, gqb_ref, gkb_ref,
             dx_ref, dp_ref, dfb_ref, ggqa_ref, ggka_ref, ggqb_ref, ggkb_ref, gg1_ref, gb_ref, carry, dlf_ref):
        @pl.when((pl.program_id(0) == 0) & (pl.program_id(1) == 0))
        def _():
            for r in (ggqa_ref, ggka_ref, ggqb_ref, ggkb_ref, gg1_ref, gb_ref):
                r[...] = jnp.zeros_like(r)

        @pl.when(pl.program_id(1) == 0)
        def _():
            carry[...] = jnp.zeros_like(carry)

        lane = _lane((tm, LANES))
        dc = jnp.zeros((tm, LANES), F32)
        for hd in range(8):
            col = (dqb_ref[:, LANES * hd + L_CQ:LANES * hd + L_CQ + 1] - dkb_ref[:, LANES * hd + L_CK:LANES * hd + L_CK + 1])
            dc = jnp.where(lane == hd, col, dc)
        dlf_ref[...] = _tri_dot(tm, True, dc) + carry[...]
        carry[...] = dlf_ref[pl.ds(0, 1), :]
        dfl = dlf_ref[...] * (1.0 / (1.0 + jnp.exp(fl_ref[...] + b_ref[...])))
        gb_ref[...] += jnp.sum(dfl, axis=0, keepdims=True)

        raw = raw_ref[...]
        d_qa, p_qa = _head_norm_bwd(raw[:, 0:512], gqa_ref[...], dqa_ref[...])
        d_ka, p_ka = _head_norm_bwd(raw[:, 512:640], gka_ref[...], _fold_kv(dkae_ref[...]))
        d_va = _fold_kv(dvae_ref[...])
        d_qb, p_qb = _head_norm_bwd(raw[:, 768:1280], gqb_ref[...], _to_pairs(dqb_ref) * SCALE)
        d_kb, p_kb = _head_norm_bwd(raw[:, 1280:1792], gkb_ref[...], _to_pairs(dkb_ref) * (1.0 / LOG2E))
        ggqa_ref[...] += jnp.sum(p_qa, axis=0, keepdims=True)
        ggka_ref[...] += jnp.sum(p_ka, axis=0, keepdims=True)
        ggqb_ref[...] += jnp.sum(p_qb, axis=0, keepdims=True)
        ggkb_ref[...] += jnp.sum(p_kb, axis=0, keepdims=True)
        dproj = jnp.concatenate([d_qa, d_ka, d_va, d_qb, d_kb, _to_pairs(dvb_ref)], axis=1).astype(BF16)
        dp_ref[...] = dproj
        dfb = dfl.astype(BF16)
        dfb_ref[...] = dfb
        d_xn = (jnp.dot(dproj, wmt_ref[...], preferred_element_type=F32)
                + jnp.dot(dfb, wft_ref[...], preferred_element_type=F32))
        x = x_ref[...]
        r = lax.rsqrt(jnp.mean(x * x, axis=-1, keepdims=True) + EPS)
        hat = x * r
        gd = d_xn * g1_ref[...]
        dx_ref[...] = dh_ref[...] + r * (gd - hat * jnp.mean(gd * hat, axis=-1, keepdims=True))
        gg1_ref[...] += jnp.sum(d_xn * hat, axis=0, keepdims=True)

    def tile(w):
        return pl.BlockSpec((tm, w), lambda b, i: (b * nt + (nt - 1 - i), 0))

    def acc(w):
        return pl.BlockSpec((1, w), lambda b, i: (0, 0))

    return pl.pallas_call(
        body, name="proj_bwd", grid=(nb, nt),
        in_specs=[tile(MAIN_W), tile(512), tile(512), tile(512), tile(8 * LANES), tile(8 * LANES), tile(8 * LANES), tile(LANES),
                  _const_spec((1, LANES)), tile(D_MODEL), tile(D_MODEL), _const_spec((MAIN_W, D_MODEL)),
                  _const_spec((LANES, D_MODEL)), _const_spec((1, D_MODEL)), _const_spec((1, 512)), _const_spec((1, 128)),
                  _const_spec((1, 512)), _const_spec((1, 512))],
        out_specs=[tile(D_MODEL), tile(MAIN_W), tile(LANES), acc(512), acc(128), acc(512), acc(512), acc(D_MODEL), acc(LANES)],
        out_shape=[jax.ShapeDtypeStruct((t, D_MODEL), F32), jax.ShapeDtypeStruct((t, MAIN_W), BF16),
                   jax.ShapeDtypeStruct((t, LANES), BF16), jax.ShapeDtypeStruct((1, 512), F32),
                   jax.ShapeDtypeStruct((1, 128), F32), jax.ShapeDtypeStruct((1, 512), F32),
                   jax.ShapeDtypeStruct((1, 512), F32), jax.ShapeDtypeStruct((1, D_MODEL), F32),
                   jax.ShapeDtypeStruct((1, LANES), F32)],
        scratch_shapes=[pltpu.VMEM((1, LANES), F32), pltpu.VMEM((tm, LANES), F32)],
        compiler_params=_params(("arbitrary", "arbitrary"), VMEM_LIMIT_WIDE),
    )(raw, dqa, dkae, dvae, dqb, dkb, dvb, fl, bf_row, x2, dh, w_main_t, w_f_t, g1, gqa, gka, gqb, gkb)


IN_PAD = 304


def _local_step(x, tgt, w_in_t, rest, g1, b_forget, qna, kna, sinks, qnb, knb, g2,
                tm=512, bt=1024, btf=1024, tq=4096, wk=4096, wkb=8192, distributed=False):
    nb, s, _ = x.shape
    t = nb * s
    x2, tgt2 = x.reshape(t, D_MODEL), tgt.reshape(t, D_MODEL)
    g1r, g2r = g1.reshape(1, D_MODEL), g2.reshape(1, D_MODEL)
    gqa, gka = jnp.tile(qna, 8).reshape(1, 512), jnp.tile(kna, 2).reshape(1, 128)
    gqb, gkb = jnp.tile(qnb, 8).reshape(1, 512), jnp.tile(knb, 8).reshape(1, 512)
    bf_row = jnp.pad(b_forget, (0, LANES - 8)).reshape(1, LANES)
    sink_row = jnp.pad(sinks, (0, LANES - 8)).reshape(1, LANES)
    w_main_t = w_in_t[0:MAIN_W]
    w_f_t = jnp.pad(w_in_t[MAIN_W:IN_W], ((0, LANES - 8), (0, 0)))

    xn = _xnorm(x2, g1r, 2 * tm)
    raw, fl, qa, kae, vae, q_aug, k_aug, v_aug = _norm_proj(xn, w_main_t, w_f_t, gqa, gka, gqb, gkb, bf_row, s, tm)
    ma, lse_a = _swa_fwd(qa, kae, vae, sink_row, nb, s, tq)
    if distributed:
        mb, ql_aug, w_out, w_up, w_down, w_up_t = _fox_fwd(q_aug, k_aug, v_aug, nb, s, btf, shards=rest)
    else:
        mb, ql_aug = _fox_fwd(q_aug, k_aug, v_aug, nb, s, btf)
        w_out, w_up, w_down, w_up_t = rest
    w_out, w_down = w_out.reshape(D_MODEL, D_MODEL), w_down.reshape(D_FF, D_MODEL)
    h, hn, hid, dy, dyb, loss_acc = _mlp_fwd(x2, ma, mb, tgt2, w_out, g2r, w_up, w_down, tm)

    du, dh, dhb, dma, do_aug, dla, gg2 = _mlp_bwd(dy, hid, h, ma, mb, w_down, w_up_t.reshape(D_FF, D_MODEL), w_out, g2r, tm)
    g_down = _wgrad(hid, dyb, "wgrad_down", 512, 1024, wkb, BF16).reshape(N_DEV, 512, D_MODEL)
    g_up = _wgrad(hn, du, "wgrad_up", 1024, 512, wkb, BF16, col_blocks=True)

    def out_grad():
        return jnp.concatenate([_wgrad(ma, dhb, "wgrad_out_a", 512, 1024, wk, BF16),
                                _wgrad(mb, dhb, "wgrad_out_b", 512, 1024, wk, BF16)], axis=0).reshape(N_DEV, 128, D_MODEL)

    dqa, dkae, dvae, dsink = _swa_bwd(qa, kae, vae, dma, sink_row, lse_a, dla, nb, s, tq)
    fox = _fox_bwd(ql_aug, k_aug, v_aug, do_aug, nb, s, bt, exch=(g_up, g_down) if distributed else ())
    dqb, dkb, dvb = fox[:3]
    if distributed:
        g_up, g_down = fox[3:]
        g_out = out_grad
    else:
        g_out = out_grad()
    grad_x, dproj, dfb, ggqa, ggka, ggqb, ggkb, gg1, gbf = _proj_bwd(
        raw, dqa, dkae, dvae, dqb, dkb, dvb, fl, bf_row, x2, dh, w_main_t, w_f_t, g1r, gqa, gka, gqb, gkb, nb, s, tm)
    g_main_t, g_gate_t = _wgrad(dproj, xn, "wgrad_in", 768, 1024, wk, a2=dfb)
    g_in_t = jnp.concatenate([g_main_t, g_gate_t[0:8]], axis=0)

    small = (gg1.reshape(D_MODEL), gbf[0, 0:8], ggqa.reshape(8, 64).sum(0), ggka.reshape(2, 64).sum(0),
             dsink.sum(0)[:, 0:2, 0].reshape(8), ggqb.reshape(8, 64).sum(0), ggkb.reshape(8, 64).sum(0),
             gg2.reshape(D_MODEL))
    return loss_acc[0, 0], grad_x.reshape(nb, s, D_MODEL), g_in_t, g_out, g_up, g_down, small


def _all_gather(shard):
    x_ref = jax.new_ref(shard, memory_space=pltpu.MemorySpace.HBM)
    out_ref = jax.empty_ref(jax.ShapeDtypeStruct((N_DEV,) + shard.shape, shard.dtype), memory_space=pltpu.MemorySpace.HBM)

    @pl.kernel(mesh=plsc.ScalarSubcoreMesh(axis_name="sequencer", num_cores=1), name="gather_w_in",
               scratch_types=(pltpu.SemaphoreType.DMA((N_SEM,)), pltpu.SemaphoreType.DMA((N_SEM,)), pltpu.SemaphoreType.DMA((1,))),
               compiler_params=pltpu.CompilerParams(collective_id=1))
    def launch(send_sems, recv_sems, local_sems):
        x, y, c = lax.axis_index("x"), lax.axis_index("y"), lax.axis_index("c")
        barrier = pltpu.get_barrier_semaphore()
        peers = [(x, y, 1 - c), (1 - x, y, c), (x, 1 - y, c), (1 - x, 1 - y, c)]
        for peer in peers:
            pl.semaphore_signal(barrier, inc=1, device_id=peer, device_id_type=MESH)
        pl.semaphore_wait(barrier, len(peers))
        start, forward, finish = _gather_steps([(x_ref, out_ref)], send_sems, recv_sems, local_sems)
        start()
        forward()
        finish()

    launch()
    return out_ref[...]


def _exchange(*arrays, name="exchange_tail", collective_id=0):
    n_ex = len(arrays)
    srcs = [jax.new_ref(a, memory_space=pltpu.MemorySpace.HBM) for a in arrays]
    dsts = [jax.empty_ref(jax.ShapeDtypeStruct(a.shape, a.dtype), memory_space=pltpu.MemorySpace.HBM) for a in arrays]

    @pl.kernel(mesh=plsc.ScalarSubcoreMesh(axis_name="sequencer", num_cores=1), name=name,
               scratch_types=(pltpu.SemaphoreType.DMA((N_SEM * n_ex,)), pltpu.SemaphoreType.DMA((N_SEM * n_ex,)),
                              pltpu.SemaphoreType.DMA((n_ex,))),
               compiler_params=pltpu.CompilerParams(collective_id=collective_id))
    def launch(send_sems, recv_sems, local_sems):
        x, y, c = lax.axis_index("x"), lax.axis_index("y"), lax.axis_index("c")
        barrier = pltpu.get_barrier_semaphore()
        for k in range(1, N_DEV):
            peer = (1 - x if k & 4 else x, 1 - y if k & 2 else y, 1 - c if k & 1 else c)
            pl.semaphore_signal(barrier, inc=1, device_id=peer, device_id_type=MESH)
        pl.semaphore_wait(barrier, N_DEV - 1)
        start, finish = _exchange_steps(list(zip(srcs, dsts)), send_sems, recv_sems, local_sems)
        start()
        finish()

    launch()
    return [d[...] for d in dsts]


def _sum_adamw(recv, w, m, v, tr, name):
    _, r, n = recv.shape

    def body(r_ref, w_ref, m_ref, v_ref, g_ref, d_ref, nm_ref, nv_ref):
        g = r_ref[0].astype(F32)
        for s in range(1, N_DEV):
            g = g + r_ref[s].astype(F32)
        g_ref[...] = g
        nm = ADAM_B1 * m_ref[...] + (1.0 - ADAM_B1) * g
        nv = ADAM_B2 * v_ref[...] + (1.0 - ADAM_B2) * (g * g)
        m_hat = nm / (1.0 - ADAM_B1 ** ADAM_STEP)
        v_hat = nv / (1.0 - ADAM_B2 ** ADAM_STEP)
        d_ref[...] = -ADAM_LR * (m_hat / (jnp.sqrt(v_hat) + ADAM_EPS) + ADAM_WD * w_ref[...])
        nm_ref[...] = nm
        nv_ref[...] = nv

    tile = pl.BlockSpec((tr, n), lambda i: (i, 0))
    shp = jax.ShapeDtypeStruct((r, n), F32)
    return pl.pallas_call(
        body, name=name, grid=(r // tr,),
        in_specs=[pl.BlockSpec((N_DEV, tr, n), lambda i: (0, i, 0)), tile, tile, tile],
        out_specs=[tile, tile, tile, tile], out_shape=[shp, shp, shp, shp],
        compiler_params=_params(("arbitrary",)),
    )(recv, w, m, v)


def _small_rows(g1, bf, qna, kna, sk, qnb, knb, g2, extra=None):
    row2 = jnp.concatenate([bf, qna, kna, sk, qnb, knb])
    rows = [g1, g2, jnp.pad(row2, (0, D_MODEL - row2.shape[0]))]
    if extra is not None:
        rows.append(jnp.pad(extra.reshape(1), (0, D_MODEL - 1)))
    return jnp.pad(jnp.stack(rows), ((0, 8 - len(rows)), (0, 0)))


def _in_rows(w_in_s):
    return jnp.pad(w_in_s.T, ((0, IN_PAD - IN_SHARD), (0, 0)))


def kernel(x, attn_norm_g, w_in, b_forget, q_norm_a, k_norm_a, sink_logits, q_norm_b, k_norm_b, w_out, mlp_norm_g, w_up, w_down, loss_target, m_attn_norm_g, m_w_in, m_b_forget, m_q_norm_a, m_k_norm_a, m_sink_logits, m_q_norm_b, m_k_norm_b, m_w_out, m_mlp_norm_g, m_w_up, m_w_down, v_attn_norm_g, v_w_in, v_b_forget, v_q_norm_a, v_k_norm_a, v_sink_logits, v_q_norm_b, v_k_norm_b, v_w_out, v_mlp_norm_g, v_w_up, v_w_down):
    w_in_r = _in_rows(w_in)
    w_in_t = _all_gather(w_in_r.astype(BF16))[:, 0:IN_SHARD].reshape(IN_W, D_MODEL)
    w_up_b = w_up.astype(BF16)
    rest = (w_out.astype(BF16), w_up_b, w_down.astype(BF16), w_up_b.T)

    loss_part, grad_x, g_in_t, out_grad, r_up, r_down, small = _local_step(
        x, loss_target, w_in_t, rest, attn_norm_g, b_forget, q_norm_a, k_norm_a, sink_logits, q_norm_b, k_norm_b, mlp_norm_g,
        distributed=True)

    g_in_blocks = jnp.pad(g_in_t.reshape(N_DEV, IN_SHARD, D_MODEL), ((0, 0), (0, IN_PAD - IN_SHARD), (0, 0))).astype(BF16)
    small_blocks = jnp.broadcast_to(_small_rows(*small, extra=loss_part), (N_DEV, 8, D_MODEL))
    r_in, r_small = _exchange(g_in_blocks, small_blocks)
    r_out, = _exchange(out_grad(), name="exchange_out", collective_id=2)

    small_w = _small_rows(attn_norm_g, b_forget, q_norm_a, k_norm_a, sink_logits, q_norm_b, k_norm_b, mlp_norm_g)
    small_m = _small_rows(m_attn_norm_g, m_b_forget, m_q_norm_a, m_k_norm_a, m_sink_logits, m_q_norm_b, m_k_norm_b, m_mlp_norm_g)
    small_v = _small_rows(v_attn_norm_g, v_b_forget, v_q_norm_a, v_k_norm_a, v_sink_logits, v_q_norm_b, v_k_norm_b, v_mlp_norm_g)
    o_up = _sum_adamw(r_up, w_up, m_w_up, v_w_up, 256, "adamw_up")
    o_down = _sum_adamw(r_down, w_down, m_w_down, v_w_down, 128, "adamw_down")
    o_in = [a[0:IN_SHARD].T for a in _sum_adamw(r_in, w_in_r, _in_rows(m_w_in), _in_rows(v_w_in), IN_PAD, "adamw_in")]
    o_out = _sum_adamw(r_out, w_out, m_w_out, v_w_out, 128, "adamw_out")
    o_small = _sum_adamw(r_small, small_w, small_m, small_v, 8, "adamw_small")

    def leaves(i):
        row2 = o_small[i][2]
        return (o_small[i][0], o_in[i], row2[0:8], row2[8:72], row2[72:136], row2[136:144], row2[144:208], row2[208:272],
                o_out[i], o_small[i][1], o_up[i], o_down[i])

    return (o_small[0][3, 0], grad_x, *leaves(0), *leaves(1), *leaves(2), *leaves(3))
```

```python
import functools

import jax
import jax.numpy as jnp
from jax import lax
from jax.experimental import pallas as pl
from jax.experimental.pallas import tpu as pltpu
from jax.experimental.pallas import tpu_sc as plsc

F32 = jnp.float32
BF16 = jnp.bfloat16

D_MODEL = 1024
HEAD_DIM = 64
N_DEV = 8
D_FF = 4096
MAIN_W = 2304
IN_W = 2312
IN_SHARD = 289
WINDOW = 128
EPS = 1e-6
SCALE = 0.125
LOG2E = 1.4426950408889634
LANES = 128
NEG_INF = float("-inf")

ADAM_LR = 0.001
ADAM_B1 = 0.9
ADAM_B2 = 0.999
ADAM_EPS = 1e-08
ADAM_WD = 0.01
ADAM_STEP = 10

VMEM_LIMIT = 56 * 1024 * 1024
VMEM_LIMIT_WIDE = 62 * 1024 * 1024


def _params(sem, vmem=VMEM_LIMIT):
    return pltpu.CompilerParams(dimension_semantics=sem, vmem_limit_bytes=vmem)


def _const_spec(shape):
    nd = len(shape)
    return pl.BlockSpec(shape, lambda *_: (0,) * nd, pipeline_mode=pl.Buffered(1))


def _lane(shape):
    return lax.broadcasted_iota(jnp.int32, shape, len(shape) - 1)


def _head_ones(n):
    r = lax.shift_right_logical(lax.broadcasted_iota(jnp.int32, (n, n), 0), 6)
    c = lax.shift_right_logical(lax.broadcasted_iota(jnp.int32, (n, n), 1), 6)
    return (r == c).astype(BF16)


def _head_sum(v):
    w = v.shape[1]
    vb = v.astype(BF16)
    if w <= 256:
        return jnp.dot(vb, _head_ones(w), preferred_element_type=F32)
    ones = _head_ones(256)
    return jnp.concatenate([jnp.dot(vb[:, s:s + 256], ones, preferred_element_type=F32) for s in range(0, w, 256)], axis=1)


def _head_norm(seg, gain):
    rs = lax.rsqrt(_head_sum(seg * seg) * (1.0 / HEAD_DIM) + EPS)
    return seg * rs * gain


def _head_norm_bwd(seg, gain, d_out):
    rs = lax.rsqrt(_head_sum(seg * seg) * (1.0 / HEAD_DIM) + EPS)
    hat = seg * rs
    gd = d_out * gain
    d_seg = rs * (gd - hat * (_head_sum(gd * hat) * (1.0 / HEAD_DIM)))
    return d_seg, d_out * hat


def _expand_kv(v):
    r = pltpu.roll(v, 64, axis=1)
    lo = _lane(v.shape) < 64
    return jnp.concatenate([jnp.where(lo, v, r), jnp.where(lo, r, v)], axis=1)


def _fold_kv(e4):
    t0 = e4[:, 0:128] + e4[:, 128:256]
    t1 = e4[:, 256:384] + e4[:, 384:512]
    t0 = t0 + pltpu.roll(t0, 64, axis=1)
    t1 = t1 + pltpu.roll(t1, 64, axis=1)
    return jnp.where(_lane(t0.shape) < 64, t0, t1)


def _pick_lane(blk, idx):
    return jnp.sum(jnp.where(_lane(blk.shape) == idx, blk, 0.0), axis=1, keepdims=True)


def _nt(a, b):
    return lax.dot_general(a, b, (((1,), (1,)), ((), ())), preferred_element_type=F32)


def _tn(a, b):
    return lax.dot_general(a, b, (((0,), (0,)), ((), ())), preferred_element_type=F32)


def _xnorm(x2, g1, tm):
    t = x2.shape[0]

    def body(x_ref, g1_ref, xn_ref):
        x = x_ref[...]
        r = lax.rsqrt(jnp.mean(x * x, axis=-1, keepdims=True) + EPS)
        xn_ref[...] = (x * r * g1_ref[...]).astype(BF16)

    tile = pl.BlockSpec((tm, D_MODEL), lambda i: (i, 0))
    return pl.pallas_call(
        body, name="xnorm", grid=(t // tm,), in_specs=[tile, _const_spec((1, D_MODEL))], out_specs=tile,
        out_shape=jax.ShapeDtypeStruct((t, D_MODEL), BF16), compiler_params=_params(("arbitrary",)),
    )(x2, g1)


def _norm_proj(xn, w_main_t, w_f_t, gqa, gka, gqb, gkb, bf_row, s, tm):
    t = xn.shape[0]
    nt = s // tm

    def body(xn_ref, wm_ref, wf_ref, gqa_ref, gka_ref, gqb_ref, gkb_ref, b_ref,
             raw_ref, fl_ref, qa_ref, kae_ref, vae_ref, qo_ref, ko_ref, vo_ref, carry, c_ref):
        @pl.when(lax.rem(pl.program_id(0), nt) == 0)
        def _():
            carry[...] = jnp.zeros_like(carry)

        xn = xn_ref[...]
        proj = _nt(xn, wm_ref[...])
        raw_ref[...] = proj
        fl = _nt(xn, wf_ref[...])
        fl_ref[...] = fl
        qa_ref[...] = _head_norm(proj[:, 0:512], gqa_ref[...]).astype(BF16)
        kae_ref[...] = _expand_kv(_head_norm(proj[:, 512:640], gka_ref[...])).astype(BF16)
        vae_ref[...] = _expand_kv(proj[:, 640:768]).astype(BF16)

        z = fl + b_ref[...]
        e = jnp.exp(-jnp.abs(z))
        u = 1.0 + e
        log1p = jnp.where(u == 1.0, e, jnp.log(u) * (e / (u - 1.0)))
        lf = jnp.minimum(z, 0.0) - log1p
        for r0 in range(0, tm, 256):
            c_ref[r0:r0 + 256, :] = _tri_dot(256, False, lf[r0:r0 + 256]) + carry[...]
            carry[...] = c_ref[pl.ds(r0 + 255, 1), :]
        c2 = c_ref[...] * LOG2E
        qb = _head_norm(proj[:, 768:1280], gqb_ref[...]) * (SCALE * LOG2E)
        kb = _head_norm(proj[:, 1280:1792], gkb_ref[...])
        lane = _lane((tm, LANES))
        for h in range(8):
            j, half = h // 2, h % 2
            pair, blk = slice(LANES * j, LANES * (j + 1)), slice(LANES * h, LANES * (h + 1))
            feat = _spread3(c2[:, h:h + 1], (tm, LANES), (L_CK, L_CQ))
            q = _put_ones(_head_block(qb[:, pair], half), (L_CK, L_CK + 1, L_CK + 2))
            qo_ref[:, blk] = jnp.where((lane >= L_CQ) & (lane < L_CQ + 3), feat, q).astype(BF16)
            k = _put_ones(_head_block(kb[:, pair], half), tuple(range(L_CQ, L_CQ + 6)))
            ko_ref[:, blk] = jnp.where((lane >= L_CK) & (lane < L_CK + 3), -feat, k).astype(BF16)
            v = _head_block(proj[:, 1792 + LANES * j:1792 + LANES * (j + 1)], half)
            vo_ref[:, blk] = _put_ones(v, (L_ONE, L_DELTA, L_DELTA + 1, L_DELTA + 2)).astype(BF16)

    def tile(w):
        return pl.BlockSpec((tm, w), lambda i: (i, 0))

    aug = jax.ShapeDtypeStruct((t, 8 * LANES), BF16)
    return pl.pallas_call(
        body, name="norm_proj", grid=(t // tm,),
        in_specs=[tile(D_MODEL), _const_spec((MAIN_W, D_MODEL)), _const_spec((LANES, D_MODEL)),
                  _const_spec((1, 512)), _const_spec((1, 128)), _const_spec((1, 512)), _const_spec((1, 512)),
                  _const_spec((1, LANES))],
        out_specs=[tile(MAIN_W), tile(LANES), tile(512), tile(256), tile(256)] + [tile(8 * LANES)] * 3,
        out_shape=[jax.ShapeDtypeStruct((t, MAIN_W), F32),
                   jax.ShapeDtypeStruct((t, LANES), F32), jax.ShapeDtypeStruct((t, 512), BF16),
                   jax.ShapeDtypeStruct((t, 256), BF16), jax.ShapeDtypeStruct((t, 256), BF16), aug, aug, aug],
        scratch_shapes=[pltpu.VMEM((1, LANES), F32), pltpu.VMEM((tm, LANES), F32)],
        compiler_params=_params(("arbitrary",)),
    )(xn, w_main_t, w_f_t, gqa, gka, gqb, gkb, bf_row)


def _tri_dot(n, upper, v):
    r = lax.broadcasted_iota(jnp.int32, (n, n), 0)
    c = lax.broadcasted_iota(jnp.int32, (n, n), 1)
    tri = ((c >= r) if upper else (c <= r)).astype(BF16)
    hi = v.astype(BF16)
    mid = (v - hi.astype(F32)).astype(BF16)
    lo = (v - hi.astype(F32) - mid.astype(F32)).astype(BF16)
    return (jnp.dot(tri, hi, preferred_element_type=F32) + jnp.dot(tri, mid, preferred_element_type=F32)
            + jnp.dot(tri, lo, preferred_element_type=F32))


def _slope(p, hh):
    out = jnp.float32(2.0 ** -(2 * 3 + hh + 1))
    for pp in (2, 1, 0):
        out = jnp.where(p == pp, jnp.float32(2.0 ** -(2 * pp + hh + 1)), out)
    return out


def _swa_windows(ref, i, tq):
    nsub = tq // WINDOW
    cur = ref[pl.ds(pl.multiple_of(i * tq, tq), tq), :].reshape(nsub, WINDOW, LANES)
    first = ref[pl.ds(pl.multiple_of(jnp.maximum(i * tq - WINDOW, 0), WINDOW), WINDOW), :].reshape(1, WINDOW, LANES)
    return jnp.concatenate([jnp.concatenate([first, cur[0:nsub - 1]], axis=0), cur], axis=1)


def _both_heads(x3, lo):
    zero = jnp.zeros_like(x3)
    return jnp.concatenate([jnp.where(lo, x3, zero), jnp.where(lo, zero, x3)], axis=0)


def _swa_sinks(sink_ref, p, nsub):
    is_a = lax.broadcasted_iota(jnp.int32, (2 * nsub, 1, 1), 0) < nsub
    sinks = sink_ref[...]
    return jnp.where(is_a, _pick_lane(sinks, 2 * p).reshape(1, 1, 1), _pick_lane(sinks, 2 * p + 1).reshape(1, 1, 1))


def _swa_bias(p, i, nsub, keys_first):
    shape = (1, 2 * WINDOW, WINDOW) if keys_first else (1, WINDOW, 2 * WINDOW)
    qi = lax.broadcasted_iota(jnp.int32, shape, 2 if keys_first else 1)
    ki = lax.broadcasted_iota(jnp.int32, shape, 1 if keys_first else 2)
    dist = qi + WINDOW - ki
    band = (dist >= 0) & (dist < WINDOW)
    tiles = []
    for hh in range(2):
        bias = jnp.where(band, -_slope(p, hh) * dist.astype(F32), NEG_INF)
        tiles += [jnp.where((i == 0) & (ki < WINDOW), NEG_INF, bias)] + [bias] * (nsub - 1)
    return jnp.concatenate(tiles, axis=0)


def _swa_fwd(qa, kae, vae, sink_row, nb, s, tq):
    t = qa.shape[0]
    nq = s // tq
    nsub = tq // WINDOW

    def body(q_ref, k_ref, v_ref, sink_ref, o_ref, lse_ref):
        p, i = pl.program_id(1), pl.program_id(2)
        lo = _lane((1, 1, LANES)) < 64
        kk, vv = _swa_windows(k_ref, i, tq), _swa_windows(v_ref, i, tq)
        qs = (q_ref[...].astype(F32) * SCALE).astype(BF16).reshape(nsub, WINDOW, LANES)
        q8 = _both_heads(qs, lo)
        s8 = jnp.einsum("bqd,bkd->bqk", q8, jnp.concatenate([kk, kk], axis=0), preferred_element_type=F32)
        sink = _swa_sinks(sink_ref, p, nsub)
        s8 = s8 + _swa_bias(p, i, nsub, False)
        m = jnp.maximum(jnp.max(s8, axis=2, keepdims=True), sink)
        e = jnp.exp(s8 - m)
        den = jnp.sum(e, axis=2, keepdims=True) + jnp.exp(sink - m)
        pr = (e * (1.0 / den)).astype(BF16)
        o8 = jnp.einsum("bqk,bkd->bqd", pr, jnp.concatenate([vv, vv], axis=0), preferred_element_type=F32)
        lse8 = m + jnp.log(den)
        o_ref[...] = jnp.where(lo, o8[0:nsub], o8[nsub:]).astype(BF16).reshape(tq, LANES)
        lse_ref[...] = jnp.where(lo, lse8[0:nsub], lse8[nsub:]).reshape(tq, LANES)

    return pl.pallas_call(
        body, name="swa_fwd", grid=(nb, 4, nq),
        in_specs=[pl.BlockSpec((tq, LANES), lambda b, p, i: (b * nq + i, p)),
                  pl.BlockSpec((s, LANES), lambda b, p, i: (b, lax.shift_right_logical(p, 1))),
                  pl.BlockSpec((s, LANES), lambda b, p, i: (b, lax.shift_right_logical(p, 1))),
                  pl.BlockSpec((1, LANES), lambda b, p, i: (0, 0))],
        out_specs=[pl.BlockSpec((tq, LANES), lambda b, p, i: (b * nq + i, p)),
                   pl.BlockSpec((None, tq, LANES), lambda b, p, i: (p, b * nq + i, 0))],
        out_shape=[jax.ShapeDtypeStruct((t, 512), BF16), jax.ShapeDtypeStruct((4, t, LANES), F32)],
        compiler_params=_params(("arbitrary", "arbitrary", "arbitrary")),
    )(qa, kae, vae, sink_row)


def _swa_bwd(qa, kae, vae, do_a, sink_row, lse, delta, nb, s, tq):
    t = qa.shape[0]
    nq = s // tq
    nsub = tq // WINDOW

    def body(q_ref, do_ref, k_ref, v_ref, sink_ref, lse_ref, dl_ref, dq_ref, dk_ref, dv_ref, ds_ref):
        p, i = pl.program_id(1), pl.program_id(2)

        @pl.when(i == 0)
        def _():
            ds_ref[...] = jnp.zeros_like(ds_ref)

        lo = _lane((1, 1, LANES)) < 64
        kk, vv = _swa_windows(k_ref, i, tq), _swa_windows(v_ref, i, tq)
        kks = (kk.astype(F32) * SCALE).astype(BF16)
        k8, v8 = jnp.concatenate([kks, kks], axis=0), jnp.concatenate([vv, vv], axis=0)
        q8 = _both_heads(q_ref[...].reshape(nsub, WINDOW, LANES), lo)
        do8 = _both_heads(do_ref[...].reshape(nsub, WINDOW, LANES), lo)
        cur = pl.multiple_of(i * tq, tq)
        sub = lax.broadcasted_iota(jnp.int32, (WINDOW, WINDOW), 0)
        lse_t = [lse_ref[u * WINDOW:(u + 1) * WINDOW, :].T for u in range(nsub)]
        dl_t = [dl_ref[u * WINDOW:(u + 1) * WINDOW, :].T for u in range(nsub)]
        lse8 = jnp.concatenate([t_[64 * hh:64 * hh + 1, :].reshape(1, 1, WINDOW) for hh in range(2) for t_ in lse_t], axis=0)
        dl8 = jnp.concatenate([jnp.sum(jnp.where(sub == 2 * p + hh, t_, 0.0), axis=0, keepdims=True).reshape(1, 1, WINDOW)
                               for hh in range(2) for t_ in dl_t], axis=0)
        sink = _swa_sinks(sink_ref, p, nsub)
        st = jnp.einsum("bkd,bqd->bkq", k8, q8, preferred_element_type=F32) + _swa_bias(p, i, nsub, True) - lse8
        pt = jnp.exp(st)
        dpt = jnp.einsum("bkd,bqd->bkq", v8, do8, preferred_element_type=F32)
        dst = pt * (dpt - dl8)
        ptb, dstb = pt.astype(BF16), dst.astype(BF16)
        dv8 = jnp.einsum("bkq,bqd->bkd", ptb, do8, preferred_element_type=F32)
        dk8 = jnp.einsum("bkq,bqd->bkd", dstb, q8, preferred_element_type=F32) * SCALE
        dq8 = jnp.einsum("bkq,bkd->bqd", dstb, k8, preferred_element_type=F32)
        dq_ref[...] = jnp.where(lo, dq8[0:nsub], dq8[nsub:]).reshape(tq, LANES)

        psd = jnp.exp(sink - lse8) * dl8
        row_h = lax.broadcasted_iota(jnp.int32, (8, LANES), 0)
        for hh in range(2):
            tot = jnp.sum(jnp.sum(psd[hh * nsub:(hh + 1) * nsub], axis=2, keepdims=True), axis=0, keepdims=True)
            ds_ref[...] += jnp.where(row_h == hh, -tot.reshape(1, 1), 0.0)

        prev = pl.multiple_of(jnp.maximum(i * tq - WINDOW, 0), WINDOW)
        for g8, g_ref in ((dk8, dk_ref), (dv8, dv_ref)):
            g4 = g8[0:nsub] + g8[nsub:]
            own, before = g4[:, WINDOW:, :], g4[:, 0:WINDOW, :]
            shifted = jnp.concatenate([before[1:nsub], jnp.zeros((1, WINDOW, LANES), F32)], axis=0)
            g_ref[pl.ds(cur, tq), :] = (own + shifted).reshape(tq, LANES)
            g_ref[pl.ds(prev, WINDOW), :] += before[0]

    return pl.pallas_call(
        body, name="swa_bwd", grid=(nb, 4, nq),
        in_specs=[pl.BlockSpec((tq, LANES), lambda b, p, i: (b * nq + i, p)),
                  pl.BlockSpec((tq, LANES), lambda b, p, i: (b * nq + i, p)),
                  pl.BlockSpec((s, LANES), lambda b, p, i: (b, lax.shift_right_logical(p, 1))),
                  pl.BlockSpec((s, LANES), lambda b, p, i: (b, lax.shift_right_logical(p, 1))),
                  pl.BlockSpec((1, LANES), lambda b, p, i: (0, 0)),
                  pl.BlockSpec((None, tq, LANES), lambda b, p, i: (p, b * nq + i, 0)),
                  pl.BlockSpec((tq, LANES), lambda b, p, i: (b * nq + i, 0))],
        out_specs=[pl.BlockSpec((tq, LANES), lambda b, p, i: (b * nq + i, p)),
                   pl.BlockSpec((s, LANES), lambda b, p, i: (b, p)),
                   pl.BlockSpec((s, LANES), lambda b, p, i: (b, p)),
                   pl.BlockSpec((None, None, 8, LANES), lambda b, p, i: (b, p, 0, 0))],
        out_shape=[jax.ShapeDtypeStruct((t, 512), F32), jax.ShapeDtypeStruct((t, 512), F32),
                   jax.ShapeDtypeStruct((t, 512), F32), jax.ShapeDtypeStruct((nb, 4, 8, LANES), F32)],
        compiler_params=_params(("arbitrary", "arbitrary", "arbitrary")),
    )(qa, do_a, kae, vae, sink_row, lse, delta)


MESH = pl.DeviceIdType.MESH
ANY = pl.BlockSpec(memory_space=pl.ANY)
N_SEM = 7


def _gather_steps(pairs, send_sems, recv_sems, local_sems):
    x, y, c = lax.axis_index("x"), lax.axis_index("y"), lax.axis_index("c")
    me, sibling = (x, y, c), (x, y, 1 - c)
    chips = [(1 - x, y), (x, 1 - y), (1 - x, 1 - y)]
    mine, first, passed, landed, last = [], [], [], [], []
    for a, (x_ref, out_ref) in enumerate(pairs):
        def slot(px, py, pc, out_ref=out_ref):
            return out_ref.at[4 * px + 2 * py + pc]

        def copy(k, block, to, src=None, a=a, slot=slot):
            return pltpu.make_async_remote_copy(
                src_ref=slot(*block) if src is None else src, dst_ref=slot(*block),
                send_sem=send_sems.at[N_SEM * a + k], recv_sem=recv_sems.at[N_SEM * a + k], device_id=to, device_id_type=MESH)

        mine.append(pltpu.make_async_copy(x_ref, slot(*me), local_sems.at[a]))
        first += [copy(0, me, sibling, src=x_ref)] + [copy(1 + j, me, (*chip, c), src=x_ref) for j, chip in enumerate(chips)]
        passed += [copy(4 + j, (*chip, c), sibling) for j, chip in enumerate(chips)]
        landed += [copy(1 + j, (*chip, c), me) for j, chip in enumerate(chips)]
        last += [copy(0, sibling, me)] + [copy(4 + j, (*chip, 1 - c), me) for j, chip in enumerate(chips)]

    def start():
        for cp in mine + first:
            cp.start()

    def forward():
        for arrived, onward in zip(landed, passed):
            arrived.wait_recv()
            onward.start()

    def finish():
        for cp in last:
            cp.wait_recv()
        for cp in first + passed:
            cp.wait_send()
        for cp in mine:
            cp.wait()

    return start, forward, finish


def _exchange_steps(pairs, send_sems, recv_sems, local_sems):
    x, y, c = lax.axis_index("x"), lax.axis_index("y"), lax.axis_index("c")
    my_id = 4 * x + 2 * y + c
    local, remote = [], []
    for a, (src, dst) in enumerate(pairs):
        local.append(pltpu.make_async_copy(src.at[my_id], dst.at[my_id], local_sems.at[a]))
        for k in range(1, N_DEV):
            px = 1 - x if k & 4 else x
            py = 1 - y if k & 2 else y
            pc = 1 - c if k & 1 else c
            remote.append(pltpu.make_async_remote_copy(
                src_ref=src.at[4 * px + 2 * py + pc], dst_ref=dst.at[my_id],
                send_sem=send_sems.at[N_SEM * a + k - 1], recv_sem=recv_sems.at[N_SEM * a + k - 1],
                device_id=(px, py, pc), device_id_type=MESH))

    def start():
        for cp in local + remote:
            cp.start()

    def finish():
        for cp in remote:
            cp.wait_recv()
        for cp in remote:
            cp.wait_send()
        for cp in local:
            cp.wait()

    return start, finish


L_ONE = 64
L_CK = 65
L_CQ = 68
L_LSE = 71
L_DELTA = 74


def _head_block(pair, half):
    y = pair if half == 0 else pltpu.roll(pair, 64, axis=1)
    return jnp.where(_lane(pair.shape) < 64, y, 0.0)


def _put3(blk, lane0, col):
    lane = _lane(blk.shape)
    hi = col.astype(BF16).astype(F32)
    mid = (col - hi).astype(BF16).astype(F32)
    lo = (col - hi - mid).astype(BF16).astype(F32)
    return jnp.where(lane == lane0, hi, jnp.where(lane == lane0 + 1, mid, jnp.where(lane == lane0 + 2, lo, blk)))


def _spread3(col, shape, lane0s):
    lane = _lane(shape)
    hi = col.astype(BF16).astype(F32)
    mid = (col - hi).astype(BF16).astype(F32)
    lo = (col - hi - mid).astype(BF16).astype(F32)

    def at(k):
        return functools.reduce(jnp.logical_or, [lane == ln + k for ln in lane0s])

    return jnp.where(at(0), hi, jnp.where(at(1), mid, jnp.where(at(2), lo, 0.0)))


def _put_ones(blk, lanes):
    lane = _lane(blk.shape)
    hit = functools.reduce(jnp.logical_or, [lane == ln for ln in lanes])
    return jnp.where(hit, 1.0, blk)


def _to_pairs(ref):
    out = []
    for j in range(4):
        a, b = ref[:, 2 * LANES * j:2 * LANES * j + LANES], ref[:, 2 * LANES * j + LANES:2 * LANES * (j + 1)]
        out.append(jnp.where(_lane(a.shape) < 64, a, pltpu.roll(b, 64, axis=1)))
    return jnp.concatenate(out, axis=1)


def _fox_fwd(q_aug, k_aug, v_aug, nb, s, bt, shards=()):
    t = q_aug.shape[0]
    nq = s // bt
    n_in, n_sh = 3, len(shards)

    def body(*refs):
        q_ref, k_ref, v_ref = refs[:n_in]
        o_ref, ql_ref = refs[n_in + n_sh:n_in + n_sh + 2]
        if shards:
            srcs, dsts = refs[n_in:n_in + n_sh], refs[n_in + n_sh + 2:n_in + 2 * n_sh + 2]
            start, forward, finish = _gather_steps(list(zip(srcs, dsts)), *refs[n_in + 2 * n_sh + 2:])
            step = (pl.program_id(0) * 4 + pl.program_id(1)) * nq + pl.program_id(2)
            pl.when(step == 0)(start)
            pl.when(step == nb * 3 * nq)(forward)
        i = pl.program_id(2)
        sls = [slice(LANES * hh, LANES * (hh + 1)) for hh in range(2)]
        qhs = [q_ref[:, sl] for sl in sls]

        def update(m, acc, qrows, start, size, sl, causal):
            sc = _nt(qrows, k_ref[pl.ds(start, size), sl])
            if causal:
                row = lax.broadcasted_iota(jnp.int32, sc.shape, 0)
                col = lax.broadcasted_iota(jnp.int32, sc.shape, 1)
                sc = jnp.where(row >= col, sc, NEG_INF)
            m_new = jnp.maximum(m, jnp.max(sc, axis=1, keepdims=True))
            pr = jnp.exp2(sc - m_new).astype(BF16)
            acc = jnp.exp2(m - m_new) * acc + jnp.dot(pr, v_ref[pl.ds(start, size), sl], preferred_element_type=F32)
            return m_new, acc

        def blk(kb_i, carry):
            start = pl.multiple_of(kb_i * bt, bt)
            return tuple(update(m, acc, qh, start, bt, sl, False) for (m, acc), qh, sl in zip(carry, qhs, sls))

        def diag_blk(carry):
            start = pl.multiple_of(i * bt, bt)
            return tuple(update(m, acc, qh, start, bt, sl, True) for (m, acc), qh, sl in zip(carry, qhs, sls))

        init = tuple((jnp.full((bt, 1), NEG_INF, F32), jnp.zeros((bt, LANES), F32)) for _ in range(2))
        carry = lax.fori_loop(0, i, blk, init)
        outs = []
        for (m, acc), qh, sl in zip(diag_blk(carry), qhs, sls):
            l = acc[:, L_ONE:L_ONE + 1]
            outs.append(acc * (1.0 / l))
            ql_ref[:, sl] = _put3(qh.astype(F32), L_LSE, -(m + jnp.log(l) * LOG2E)).astype(BF16)
        o_ref[...] = jnp.where(_lane((1, LANES)) < 64, outs[0], pltpu.roll(outs[1], 64, axis=1)).astype(BF16)
        if shards:
            pl.when(step == nb * 4 * nq - 1)(finish)

    in_specs = [pl.BlockSpec((bt, 2 * LANES), lambda b, j, i: (b * nq + i, j)),
                pl.BlockSpec((s, 2 * LANES), lambda b, j, i: (b, j)),
                pl.BlockSpec((s, 2 * LANES), lambda b, j, i: (b, j))]
    out_specs = [pl.BlockSpec((bt, LANES), lambda b, j, i: (b * nq + i, j)),
                 pl.BlockSpec((bt, 2 * LANES), lambda b, j, i: (b * nq + i, j))]
    out_shape = [jax.ShapeDtypeStruct((t, 512), BF16), jax.ShapeDtypeStruct((t, 8 * LANES), BF16)]
    args, scratch = [q_aug, k_aug, v_aug, *shards], []
    if shards:
        in_specs += [ANY] * n_sh
        out_specs += [ANY] * n_sh
        out_shape += [jax.ShapeDtypeStruct((N_DEV,) + sh.shape, sh.dtype) for sh in shards]
        scratch = [pltpu.SemaphoreType.DMA((N_SEM * n_sh,)), pltpu.SemaphoreType.DMA((N_SEM * n_sh,)),
                   pltpu.SemaphoreType.DMA((n_sh,))]
    return pl.pallas_call(
        body, name="fox_fwd", grid=(nb, 4, nq), in_specs=in_specs, out_specs=out_specs, out_shape=out_shape,
        scratch_shapes=scratch, compiler_params=_params(("arbitrary", "arbitrary", "arbitrary")),
    )(*args)


def _fox_bwd(ql_aug, k_aug, v_aug, do_aug, nb, s, bt, exch=()):
    t = ql_aug.shape[0]
    nk = s // bt
    n_in, n_out, n_ex = 4, 3, len(exch)

    def body(*refs):
        q_ref, do_ref, k_ref, v_ref = refs[:n_in]
        dq_ref, dk_ref, dv_ref = refs[n_in + n_ex:n_in + n_ex + n_out]
        if exch:
            srcs = refs[n_in:n_in + n_ex]
            dsts = refs[n_in + n_ex + n_out:n_in + 2 * n_ex + n_out]
            start, finish = _exchange_steps(list(zip(srcs, dsts)), *refs[n_in + 2 * n_ex + n_out:])
            step = (pl.program_id(0) * 4 + pl.program_id(1)) * nk + pl.program_id(2)
            pl.when(step == 0)(start)
        kb_i = pl.program_id(2)

        @pl.when(kb_i == 0)
        def _():
            dq_ref[...] = jnp.zeros_like(dq_ref)

        row = lax.broadcasted_iota(jnp.int32, (bt, bt), 0)
        col = lax.broadcasted_iota(jnp.int32, (bt, bt), 1)
        sls = [slice(LANES * hh, LANES * (hh + 1)) for hh in range(2)]
        khs, vhs = [k_ref[:, sl] for sl in sls], [v_ref[:, sl] for sl in sls]

        def blk(qi, carry, diag):
            start = pl.multiple_of(qi * bt, bt)
            new = []
            for (dk_a, dv_a), kh, vh, sl in zip(carry, khs, vhs, sls):
                qblk, doblk = q_ref[pl.ds(start, bt), sl], do_ref[pl.ds(start, bt), sl]
                st = _nt(kh, qblk)
                if diag:
                    pt = jnp.where(col >= row, jnp.exp2(jnp.where(col >= row, st, 0.0)), 0.0)
                else:
                    pt = jnp.exp2(st)
                dst = pt * _nt(vh, doblk)
                ptb, dstb = pt.astype(BF16), dst.astype(BF16)
                dv_a = dv_a + jnp.dot(ptb, doblk, preferred_element_type=F32)
                dk_a = dk_a + jnp.dot(dstb, qblk, preferred_element_type=F32)
                dq_ref[pl.ds(start, bt), sl] += _tn(dstb, kh)
                new.append((dk_a, dv_a))
            return tuple(new)

        zero = jnp.zeros((bt, LANES), F32)
        carry = blk(kb_i, ((zero, zero), (zero, zero)), True)
        carry = lax.fori_loop(kb_i + 1, nk, lambda qi, c: blk(qi, c, False), carry)
        for (dk_acc, dv_acc), sl in zip(carry, sls):
            dk_ref[:, sl] = dk_acc
            dv_ref[:, sl] = dv_acc
        if exch:
            pl.when(step == nb * 4 * nk - 1)(finish)

    scratch = []
    if exch:
        scratch = [pltpu.SemaphoreType.DMA((N_SEM * n_ex,)), pltpu.SemaphoreType.DMA((N_SEM * n_ex,)),
                   pltpu.SemaphoreType.DMA((n_ex,))]
    whole = pl.BlockSpec((s, 2 * LANES), lambda b, j, kb_i: (b, j))
    tile = pl.BlockSpec((bt, 2 * LANES), lambda b, j, kb_i: (b * nk + kb_i, j))
    shp = jax.ShapeDtypeStruct((t, 8 * LANES), F32)
    return pl.pallas_call(
        body, name="fox_bwd", grid=(nb, 4, nk),
        in_specs=[whole, whole, tile, tile] + [ANY] * n_ex,
        out_specs=[whole, tile, tile] + [ANY] * n_ex,
        out_shape=[shp, shp, shp] + [jax.ShapeDtypeStruct(e.shape, e.dtype) for e in exch],
        scratch_shapes=scratch, compiler_params=_params(("arbitrary", "arbitrary", "arbitrary")),
    )(ql_aug, do_aug, k_aug, v_aug, *exch)


FF_BLK = D_FF // N_DEV


def _mlp_fwd(x2, ma, mb, tgt, w_out, g2, w_up, w_down, tm):
    t = x2.shape[0]

    def body(x_ref, ma_ref, mb_ref, tg_ref, wo_ref, g2_ref, wu_ref, wd_ref,
             h_ref, hn_ref, hid_ref, dy_ref, dyb_ref, loss_ref):
        @pl.when(pl.program_id(0) == 0)
        def _():
            loss_ref[...] = jnp.zeros_like(loss_ref)

        h = (x_ref[...] + jnp.dot(ma_ref[...], wo_ref[0:512, :], preferred_element_type=F32)
             + jnp.dot(mb_ref[...], wo_ref[512:1024, :], preferred_element_type=F32))
        h_ref[...] = h
        r = lax.rsqrt(jnp.mean(h * h, axis=-1, keepdims=True) + EPS)
        hn = (h * r * g2_ref[...]).astype(BF16)
        hn_ref[...] = hn
        for d in range(N_DEV):
            u = jnp.maximum(jnp.dot(hn, wu_ref[d], preferred_element_type=F32), 0.0)
            hid_ref[:, FF_BLK * d:FF_BLK * (d + 1)] = (u * u).astype(BF16)
        y = h + jnp.dot(hid_ref[...], wd_ref[...], preferred_element_type=F32)
        err = y - tg_ref[...]
        dy = err * (1.0 / D_MODEL)
        dy_ref[...] = dy
        dyb_ref[...] = dy.astype(BF16)
        part =0.5 * jnp.sum(jnp.sum(err * err, axis=1, keepdims=True) * (1.0 / D_MODEL), axis=0, keepdims=True)
        loss_ref[...] += part

    def tile(w):
        return pl.BlockSpec((tm, w), lambda i: (i, 0))

    return pl.pallas_call(
        body, name="mlp_fwd", grid=(t // tm,),
        in_specs=[tile(D_MODEL), tile(512), tile(512), tile(D_MODEL), _const_spec((D_MODEL, D_MODEL)),
                  _const_spec((1, D_MODEL)), _const_spec((N_DEV, D_MODEL, FF_BLK)), _const_spec((D_FF, D_MODEL))],
        out_specs=[tile(D_MODEL), tile(D_MODEL), tile(D_FF), tile(D_MODEL), tile(D_MODEL),
                   pl.BlockSpec((8, LANES), lambda i: (0, 0))],
        out_shape=[jax.ShapeDtypeStruct((t, D_MODEL), F32), jax.ShapeDtypeStruct((t, D_MODEL), BF16),
                   jax.ShapeDtypeStruct((t, D_FF), BF16), jax.ShapeDtypeStruct((t, D_MODEL), F32),
                   jax.ShapeDtypeStruct((t, D_MODEL), BF16), jax.ShapeDtypeStruct((8, LANES), F32)],
        compiler_params=_params(("arbitrary",)),
    )(x2, ma, mb, tgt, w_out, g2, w_up, w_down)


def _mlp_bwd(dy, hid, h, ma, mb, w_down, w_up_t, w_out, g2, tm):
    t = dy.shape[0]

    def body(dy_ref, hid_ref, h_ref, ma_ref, mb_ref, wd_ref, wut_ref, wo_ref, g2_ref,
             du_ref, dh_ref, dhb_ref, dma_ref, dob_ref, dla_ref, gg_ref):
        @pl.when(pl.program_id(0) == 0)
        def _():
            gg_ref[...] = jnp.zeros_like(gg_ref)

        dy = dy_ref[...]
        d_hid = _nt(dy.astype(BF16), wd_ref[...])
        du = (d_hid * (2.0 * jnp.sqrt(hid_ref[...].astype(F32)))).astype(BF16)
        du_ref[...] = du
        d_hn = jnp.dot(du, wut_ref[...], preferred_element_type=F32)
        h = h_ref[...]
        r = lax.rsqrt(jnp.mean(h * h, axis=-1, keepdims=True) + EPS)
        hat = h * r
        gd = d_hn * g2_ref[...]
        dh = dy + r * (gd - hat * jnp.mean(gd * hat, axis=-1, keepdims=True))
        gg_ref[...] += jnp.sum(d_hn * hat, axis=0, keepdims=True)
        dh_ref[...] = dh
        dhb = dh.astype(BF16)
        dhb_ref[...] = dhb
        dm = _nt(dhb, wo_ref[...]).astype(BF16)
        dma, dmb = dm[:, 0:512], dm[:, 512:1024]
        dma_ref[...] = dma
        sel = (lax.shift_right_logical(lax.broadcasted_iota(jnp.int32, (512, LANES), 0), 6)
               == lax.broadcasted_iota(jnp.int32, (512, LANES), 1)).astype(BF16)
        dla_ref[...] = jnp.dot((dma.astype(F32) * ma_ref[...].astype(F32)).astype(BF16), sel, preferred_element_type=F32)
        dmb32 = dmb.astype(F32)
        dlb = jnp.dot((dmb32 * mb_ref[...].astype(F32)).astype(BF16), sel, preferred_element_type=F32)
        for hd in range(8):
            blk = _head_block(dmb32[:, LANES * (hd // 2):LANES * (hd // 2 + 1)], hd % 2)
            dob_ref[:, LANES * hd:LANES * (hd + 1)] = _put3(blk, L_DELTA, -dlb[:, hd:hd + 1]).astype(BF16)

    def tile(w):
        return pl.BlockSpec((tm, w), lambda i: (i, 0))

    return pl.pallas_call(
        body, name="mlp_bwd", grid=(t // tm,),
        in_specs=[tile(D_MODEL), tile(D_FF), tile(D_MODEL), tile(512), tile(512), _const_spec((D_FF, D_MODEL)),
                  _const_spec((D_FF, D_MODEL)), _const_spec((D_MODEL, D_MODEL)), _const_spec((1, D_MODEL))],
        out_specs=[tile(D_FF), tile(D_MODEL), tile(D_MODEL), tile(512), tile(8 * LANES), tile(LANES),
                   pl.BlockSpec((1, D_MODEL), lambda i: (0, 0))],
        out_shape=[jax.ShapeDtypeStruct((t, D_FF), BF16), jax.ShapeDtypeStruct((t, D_MODEL), F32),
                   jax.ShapeDtypeStruct((t, D_MODEL), BF16), jax.ShapeDtypeStruct((t, 512), BF16),
                   jax.ShapeDtypeStruct((t, 8 * LANES), BF16), jax.ShapeDtypeStruct((t, LANES), F32),
                   jax.ShapeDtypeStruct((1, D_MODEL), F32)],
        compiler_params=_params(("arbitrary",), VMEM_LIMIT_WIDE),
    )(dy, hid, h, ma, mb, w_down, w_up_t, w_out, g2)


def _wgrad(a, b, name, bm, bn, tk, out_dtype=F32, col_blocks=False, a2=None):
    t, m = a.shape
    n = b.shape[1]
    bm, bn = min(bm, m), min(bn, n)
    nk = t // tk

    def body(*refs):
        if a2 is None:
            a_ref, b_ref, o_ref, acc = refs
        else:
            a_ref, b_ref, a2_ref, o_ref, o2_ref, acc, acc2 = refs
        i, k = pl.program_id(0), pl.program_id(2)

        @pl.when(k == 0)
        def _():
            acc[...] = jnp.zeros_like(acc)

        acc[...] += _tn(a_ref[...], b_ref[...])

        @pl.when(k == nk - 1)
        def _():
            o_ref[...] = acc[...].astype(out_dtype)

        if a2 is not None:
            @pl.when((i == 0) & (k == 0))
            def _():
                acc2[...] = jnp.zeros_like(acc2)

            @pl.when(i == 0)
            def _():
                acc2[...] += _tn(a2_ref[...], b_ref[...])

            @pl.when((i == 0) & (k == nk - 1))
            def _():
                o2_ref[...] = acc2[...]

    if col_blocks:
        out_spec = pl.BlockSpec((None, bm, bn), lambda i, j, k: (j, i, 0))
        out_shape = jax.ShapeDtypeStruct((n // bn, m, bn), out_dtype)
    else:
        out_spec = pl.BlockSpec((bm, bn), lambda i, j, k: (i, j))
        out_shape = jax.ShapeDtypeStruct((m, n), out_dtype)
    in_specs = [pl.BlockSpec((tk, bm), lambda i, j, k: (k, i)), pl.BlockSpec((tk, bn), lambda i, j, k: (k, j))]
    out_specs, out_shapes, scratch, args = [out_spec], [out_shape], [pltpu.VMEM((bm, bn), F32)], [a, b]
    if a2 is not None:
        m2 = a2.shape[1]
        in_specs.append(pl.BlockSpec((tk, m2), lambda i, j, k: (k, 0)))
        out_specs.append(pl.BlockSpec((m2, n), lambda i, j, k: (0, 0)))
        out_shapes.append(jax.ShapeDtypeStruct((m2, n), F32))
        scratch.append(pltpu.VMEM((m2, n), F32))
        args.append(a2)
    out = pl.pallas_call(
        body, name=name, grid=(m // bm, n // bn, nk), in_specs=in_specs, out_specs=out_specs, out_shape=out_shapes,
        scratch_shapes=scratch, compiler_params=_params(("arbitrary", "arbitrary", "arbitrary")),
    )(*args)
    return out[0] if a2 is None else out


def _proj_bwd(raw, dqa, dkae, dvae, dqb, dkb, dvb, fl, bf_row, x2, dh, w_main_t, w_f_t, g1, gqa, gka, gqb, gkb, nb, s, tm):
    t = x2.shape[0]
    nt = s // tm

    def body(raw_ref, dqa_ref, dkae_ref, dvae_ref, dqb_ref, dkb_ref, dvb_ref, fl_ref, b_ref, x_ref, dh_ref,
             wmt_ref, wft_ref, g1_ref, gqa_ref, gka_ref, gqb_ref, gkb_ref,
             dx_ref, dp_ref, dfb_ref, ggqa_ref, ggka_ref, ggqb_ref, ggkb_ref, gg1_ref, gb_ref, carry, dlf_ref):
        @pl.when((pl.program_id(0) == 0) & (pl.program_id(1) == 0))
        def _():
            for r in (ggqa_ref, ggka_ref, ggqb_ref, ggkb_ref, gg1_ref, gb_ref):
                r[...] = jnp.zeros_like(r)

        @pl.when(pl.program_id(1) == 0)
        def _():
            carry[...] = jnp.zeros_like(carry)

        lane = _lane((tm, LANES))
        dc = jnp.zeros((tm, LANES), F32)
        for hd in range(8):
            col = (dqb_ref[:, LANES * hd + L_CQ:LANES * hd + L_CQ + 1] - dkb_ref[:, LANES * hd + L_CK:LANES * hd + L_CK + 1])
            dc = jnp.where(lane == hd, col, dc)
        dlf_ref[...] = _tri_dot(tm, True, dc) + carry[...]
        carry[...] = dlf_ref[pl.ds(0, 1), :]
        dfl = dlf_ref[...] * (1.0 / (1.0 + jnp.exp(fl_ref[...] + b_ref[...])))
        gb_ref[...] += jnp.sum(dfl, axis=0, keepdims=True)

        raw = raw_ref[...]
        d_qa, p_qa = _head_norm_bwd(raw[:, 0:512], gqa_ref[...], dqa_ref[...])
        d_ka, p_ka = _head_norm_bwd(raw[:, 512:640], gka_ref[...], _fold_kv(dkae_ref[...]))
        d_va = _fold_kv(dvae_ref[...])
        d_qb, p_qb = _head_norm_bwd(raw[:, 768:1280], gqb_ref[...], _to_pairs(dqb_ref) * SCALE)
        d_kb, p_kb = _head_norm_bwd(raw[:, 1280:1792], gkb_ref[...], _to_pairs(dkb_ref) * (1.0 / LOG2E))
        ggqa_ref[...] += jnp.sum(p_qa, axis=0, keepdims=True)
        ggka_ref[...] += jnp.sum(p_ka, axis=0, keepdims=True)
        ggqb_ref[...] += jnp.sum(p_qb, axis=0, keepdims=True)
        ggkb_ref[...] += jnp.sum(p_kb, axis=0, keepdims=True)
        dproj = jnp.concatenate([d_qa, d_ka, d_va, d_qb, d_kb, _to_pairs(dvb_ref)], axis=1).astype(BF16)
        dp_ref[...] = dproj
        dfb = dfl.astype(BF16)
        dfb_ref[...] = dfb
        d_xn = (jnp.dot(dproj, wmt_ref[...], preferred_element_type=F32)
                + jnp.dot(dfb, wft_ref[...], preferred_element_type=F32))
        x = x_ref[...]
        r = lax.rsqrt(jnp.mean(x * x, axis=-1, keepdims=True) + EPS)
        hat = x * r
        gd = d_xn * g1_ref[...]
        dx_ref[...] = dh_ref[...] + r * (gd - hat * jnp.mean(gd * hat, axis=-1, keepdims=True))
        gg1_ref[...] += jnp.sum(d_xn * hat, axis=0, keepdims=True)

    def tile(w):
        return pl.BlockSpec((tm, w), lambda b, i: (b * nt + (nt - 1 - i), 0))

    def acc(w):
        return pl.BlockSpec((1, w), lambda b, i: (0, 0))

    return pl.pallas_call(
        body, name="proj_bwd", grid=(nb, nt),
        in_specs=[tile(MAIN_W), tile(512), tile(512), tile(512), tile(8 * LANES), tile(8 * LANES), tile(8 * LANES), tile(LANES),
                  _const_spec((1, LANES)), tile(D_MODEL), tile(D_MODEL), _const_spec((MAIN_W, D_MODEL)),
                  _const_spec((LANES, D_MODEL)), _const_spec((1, D_MODEL)), _const_spec((1, 512)), _const_spec((1, 128)),
                  _const_spec((1, 512)), _const_spec((1, 512))],
        out_specs=[tile(D_MODEL), tile(MAIN_W), tile(LANES), acc(512), acc(128), acc(512), acc(512), acc(D_MODEL), acc(LANES)],
        out_shape=[jax.ShapeDtypeStruct((t, D_MODEL), F32), jax.ShapeDtypeStruct((t, MAIN_W), BF16),
                   jax.ShapeDtypeStruct((t, LANES), BF16), jax.ShapeDtypeStruct((1, 512), F32),
                   jax.ShapeDtypeStruct((1, 128), F32), jax.ShapeDtypeStruct((1, 512), F32),
                   jax.ShapeDtypeStruct((1, 512), F32), jax.ShapeDtypeStruct((1, D_MODEL), F32),
                   jax.ShapeDtypeStruct((1, LANES), F32)],
        scratch_shapes=[pltpu.VMEM((1, LANES), F32), pltpu.VMEM((tm, LANES), F32)],
        compiler_params=_params(("arbitrary", "arbitrary"), VMEM_LIMIT_WIDE),
    )(raw, dqa, dkae, dvae, dqb, dkb, dvb, fl, bf_row, x2, dh, w_main_t, w_f_t, g1, gqa, gka, gqb, gkb)


IN_PAD = 304


def _local_step(x, tgt, w_in_t, rest, g1, b_forget, qna, kna, sinks, qnb, knb, g2,
                tm=512, bt=1024, btf=1024, tq=4096, wk=4096, wkb=8192, distributed=False):
    nb, s, _ = x.shape
    t = nb * s
    x2, tgt2 = x.reshape(t, D_MODEL), tgt.reshape(t, D_MODEL)
    g1r, g2r = g1.reshape(1, D_MODEL), g2.reshape(1, D_MODEL)
    gqa, gka = jnp.tile(qna, 8).reshape(1, 512), jnp.tile(kna, 2).reshape(1, 128)
    gqb, gkb = jnp.tile(qnb, 8).reshape(1, 512), jnp.tile(knb, 8).reshape(1, 512)
    bf_row = jnp.pad(b_forget, (0, LANES - 8)).reshape(1, LANES)
    sink_row = jnp.pad(sinks, (0, LANES - 8)).reshape(1, LANES)
    w_main_t = w_in_t
    w_f_t = jnp.pad(w_in_t[MAIN_W:IN_W], ((0, LANES - 8), (0, 0)))

    xn = _xnorm(x2, g1r, 2 * tm)
    raw, fl, qa, kae, vae, q_aug, k_aug, v_aug = _norm_proj(xn, w_main_t, w_f_t, gqa, gka, gqb, gkb, bf_row, s, tm)
    ma, lse_a = _swa_fwd(qa, kae, vae, sink_row, nb, s, tq)
    if distributed:
        mb, ql_aug, w_out, w_up, w_down, w_up_t = _fox_fwd(q_aug, k_aug, v_aug, nb, s, btf, shards=rest)
    else:
        mb, ql_aug = _fox_fwd(q_aug, k_aug, v_aug, nb, s, btf)
        w_out, w_up, w_down, w_up_t = rest
    w_out, w_down = w_out.reshape(D_MODEL, D_MODEL), w_down.reshape(D_FF, D_MODEL)
    h, hn, hid, dy, dyb, loss_acc = _mlp_fwd(x2, ma, mb, tgt2, w_out, g2r, w_up, w_down, tm)

    du, dh, dhb, dma, do_aug, dla, gg2 = _mlp_bwd(dy, hid, h, ma, mb, w_down, w_up_t.reshape(D_FF, D_MODEL), w_out, g2r, tm)
    g_down = _wgrad(hid, dyb, "wgrad_down", 512, 1024, wkb, BF16).reshape(N_DEV, 512, D_MODEL)
    g_up = _wgrad(hn, du, "wgrad_up", 1024, 512, wkb, BF16, col_blocks=True)

    def out_grad():
        return jnp.concatenate([_wgrad(ma, dhb, "wgrad_out_a", 512, 1024, wk, BF16),
                                _wgrad(mb, dhb, "wgrad_out_b", 512, 1024, wk, BF16)], axis=0).reshape(N_DEV, 128, D_MODEL)

    dqa, dkae, dvae, dsink = _swa_bwd(qa, kae, vae, dma, sink_row, lse_a, dla, nb, s, tq)
    fox = _fox_bwd(ql_aug, k_aug, v_aug, do_aug, nb, s, bt, exch=(g_up, g_down) if distributed else ())
    dqb, dkb, dvb = fox[:3]
    if distributed:
        g_up, g_down = fox[3:]
        g_out = out_grad
    else:
        g_out = out_grad()
    grad_x, dproj, dfb, ggqa, ggka, ggqb, ggkb, gg1, gbf = _proj_bwd(
        raw, dqa, dkae, dvae, dqb, dkb, dvb, fl, bf_row, x2, dh, w_main_t, w_f_t, g1r, gqa, gka, gqb, gkb, nb, s, tm)
    g_main_t, g_gate_t = _wgrad(dproj, xn, "wgrad_in", 768, 1024, wk, a2=dfb)
    g_in_t = jnp.concatenate([g_main_t, g_gate_t[0:8]], axis=0)

    small = (gg1.reshape(D_MODEL), gbf[0, 0:8], ggqa.reshape(8, 64).sum(0), ggka.reshape(2, 64).sum(0),
             dsink.sum(0)[:, 0:2, 0].reshape(8), ggqb.reshape(8, 64).sum(0), ggkb.reshape(8, 64).sum(0),
             gg2.reshape(D_MODEL))
    return loss_acc[0, 0], grad_x.reshape(nb, s, D_MODEL), g_in_t, g_out, g_up, g_down, small


def _all_gather(shard):
    x_ref = jax.new_ref(shard, memory_space=pltpu.MemorySpace.HBM)
    out_ref = jax.empty_ref(jax.ShapeDtypeStruct((N_DEV,) + shard.shape, shard.dtype), memory_space=pltpu.MemorySpace.HBM)

    @pl.kernel(mesh=plsc.ScalarSubcoreMesh(axis_name="sequencer", num_cores=1), name="gather_w_in",
               scratch_types=(pltpu.SemaphoreType.DMA((N_SEM,)), pltpu.SemaphoreType.DMA((N_SEM,)), pltpu.SemaphoreType.DMA((1,))),
               compiler_params=pltpu.CompilerParams(collective_id=1))
    def launch(send_sems, recv_sems, local_sems):
        x, y, c = lax.axis_index("x"), lax.axis_index("y"), lax.axis_index("c")
        barrier = pltpu.get_barrier_semaphore()
        peers = [(x, y, 1 - c), (1 - x, y, c), (x, 1 - y, c), (1 - x, 1 - y, c)]
        for peer in peers:
            pl.semaphore_signal(barrier, inc=1, device_id=peer, device_id_type=MESH)
        pl.semaphore_wait(barrier, len(peers))
        start, forward, finish = _gather_steps([(x_ref, out_ref)], send_sems, recv_sems, local_sems)
        start()
        forward()
        finish()

    launch()
    return out_ref[...]


def _exchange(*arrays, name="exchange_tail", collective_id=0):
    n_ex = len(arrays)
    srcs = [jax.new_ref(a, memory_space=pltpu.MemorySpace.HBM) for a in arrays]
    dsts = [jax.empty_ref(jax.ShapeDtypeStruct(a.shape, a.dtype), memory_space=pltpu.MemorySpace.HBM) for a in arrays]

    @pl.kernel(mesh=plsc.ScalarSubcoreMesh(axis_name="sequencer", num_cores=1), name=name,
               scratch_types=(pltpu.SemaphoreType.DMA((N_SEM * n_ex,)), pltpu.SemaphoreType.DMA((N_SEM * n_ex,)),
                              pltpu.SemaphoreType.DMA((n_ex,))),
               compiler_params=pltpu.CompilerParams(collective_id=collective_id))
    def launch(send_sems, recv_sems, local_sems):
        x, y, c = lax.axis_index("x"), lax.axis_index("y"), lax.axis_index("c")
        barrier = pltpu.get_barrier_semaphore()
        for k in range(1, N_DEV):
            peer = (1 - x if k & 4 else x, 1 - y if k & 2 else y, 1 - c if k & 1 else c)
            pl.semaphore_signal(barrier, inc=1, device_id=peer, device_id_type=MESH)
        pl.semaphore_wait(barrier, N_DEV - 1)
        start, finish = _exchange_steps(list(zip(srcs, dsts)), send_sems, recv_sems, local_sems)
        start()
        finish()

    launch()
    return [d[...] for d in dsts]


def _sum_adamw(recv, w, m, v, tr, name):
    _, r, n = recv.shape

    def body(r_ref, w_ref, m_ref, v_ref, g_ref, d_ref, nm_ref, nv_ref):
        g = r_ref[0].astype(F32)
        for s in range(1, N_DEV):
            g = g + r_ref[s].astype(F32)
        g_ref[...] = g
        nm = ADAM_B1 * m_ref[...] + (1.0 - ADAM_B1) * g
        nv = ADAM_B2 * v_ref[...] + (1.0 - ADAM_B2) * (g * g)
        m_hat = nm / (1.0 - ADAM_B1 ** ADAM_STEP)
        v_hat = nv / (1.0 - ADAM_B2 ** ADAM_STEP)
        d_ref[...] = -ADAM_LR * (m_hat / (jnp.sqrt(v_hat) + ADAM_EPS) + ADAM_WD * w_ref[...])
        nm_ref[...] = nm
        nv_ref[...] = nv

    tile = pl.BlockSpec((tr, n), lambda i: (i, 0))
    shp = jax.ShapeDtypeStruct((r, n), F32)
    return pl.pallas_call(
        body, name=name, grid=(r // tr,),
        in_specs=[pl.BlockSpec((N_DEV, tr, n), lambda i: (0, i, 0)), tile, tile, tile],
        out_specs=[tile, tile, tile, tile], out_shape=[shp, shp, shp, shp],
        compiler_params=_params(("arbitrary",)),
    )(recv, w, m, v)


def _small_rows(g1, bf, qna, kna, sk, qnb, knb, g2, extra=None):
    row2 = jnp.concatenate([bf, qna, kna, sk, qnb, knb])
    rows = [g1, g2, jnp.pad(row2, (0, D_MODEL - row2.shape[0]))]
    if extra is not None:
        rows.append(jnp.pad(extra.reshape(1), (0, D_MODEL - 1)))
    return jnp.pad(jnp.stack(rows), ((0, 8 - len(rows)), (0, 0)))


def _in_rows(w_in_s):
    return jnp.pad(w_in_s.T, ((0, IN_PAD - IN_SHARD), (0, 0)))


def kernel(x, attn_norm_g, w_in, b_forget, q_norm_a, k_norm_a, sink_logits, q_norm_b, k_norm_b, w_out, mlp_norm_g, w_up, w_down, loss_target, m_attn_norm_g, m_w_in, m_b_forget, m_q_norm_a, m_k_norm_a, m_sink_logits, m_q_norm_b, m_k_norm_b, m_w_out, m_mlp_norm_g, m_w_up, m_w_down, v_attn_norm_g, v_w_in, v_b_forget, v_q_norm_a, v_k_norm_a, v_sink_logits, v_q_norm_b, v_k_norm_b, v_w_out, v_mlp_norm_g, v_w_up, v_w_down):
    w_in_r = _in_rows(w_in)
    w_in_t = _all_gather(w_in_r.astype(BF16))[:, 0:IN_SHARD].reshape(IN_W, D_MODEL)
    w_up_b = w_up.astype(BF16)
    rest = (w_out.astype(BF16), w_up_b, w_down.astype(BF16), w_up_b.T)

    loss_part, grad_x, g_in_t, out_grad, r_up, r_down, small = _local_step(
        x, loss_target, w_in_t, rest, attn_norm_g, b_forget, q_norm_a, k_norm_a, sink_logits, q_norm_b, k_norm_b, mlp_norm_g,
        distributed=True)

    g_in_blocks = jnp.pad(g_in_t.reshape(N_DEV, IN_SHARD, D_MODEL), ((0, 0), (0, IN_PAD - IN_SHARD), (0, 0))).astype(BF16)
    small_blocks = jnp.broadcast_to(_small_rows(*small, extra=loss_part), (N_DEV, 8, D_MODEL))
    r_in, r_small = _exchange(g_in_blocks, small_blocks)
    r_out, = _exchange(out_grad(), name="exchange_out", collective_id=2)

    small_w = _small_rows(attn_norm_g, b_forget, q_norm_a, k_norm_a, sink_logits, q_norm_b, k_norm_b, mlp_norm_g)
    small_m = _small_rows(m_attn_norm_g, m_b_forget, m_q_norm_a, m_k_norm_a, m_sink_logits, m_q_norm_b, m_k_norm_b, m_mlp_norm_g)
    small_v = _small_rows(v_attn_norm_g, v_b_forget, v_q_norm_a, v_k_norm_a, v_sink_logits, v_q_norm_b, v_k_norm_b, v_mlp_norm_g)
    o_up = _sum_adamw(r_up, w_up, m_w_up, v_w_up, 256, "adamw_up")
    o_down = _sum_adamw(r_down, w_down, m_w_down, v_w_down, 128, "adamw_down")
    o_in = [a[0:IN_SHARD].T for a in _sum_adamw(r_in, w_in_r, _in_rows(m_w_in), _in_rows(v_w_in), IN_PAD, "adamw_in")]
    o_out = _sum_adamw(r_out, w_out, m_w_out, v_w_out, 128, "adamw_out")
    o_small = _sum_adamw(r_small, small_w, small_m, small_v, 8, "adamw_small")

    def leaves(i):
        row2 = o_small[i][2]
        return (o_small[i][0], o_in[i], row2[0:8], row2[8:72], row2[72:136], row2[136:144], row2[144:208], row2[208:272],
                o_out[i], o_small[i][1], o_up[i], o_down[i])

    return (o_small[0][3, 0], grad_x, *leaves(0), *leaves(1), *leaves(2), *leaves(3))
```

```python
import functools

import jax
import jax.numpy as jnp
from jax import lax
from jax.experimental import pallas as pl
from jax.experimental.pallas import tpu as pltpu
from jax.experimental.pallas import tpu_sc as plsc

F32 = jnp.float32
BF16 = jnp.bfloat16

D_MODEL = 1024
HEAD_DIM = 64
N_DEV = 8
D_FF = 4096
MAIN_W = 2304
IN_W = 2312
IN_SHARD = 289
WINDOW = 128
EPS = 1e-6
SCALE = 0.125
LOG2E = 1.4426950408889634
LANES = 128
NEG_INF = float("-inf")

ADAM_LR = 0.001
ADAM_B1 = 0.9
ADAM_B2 = 0.999
ADAM_EPS = 1e-08
ADAM_WD = 0.01
ADAM_STEP = 10

VMEM_LIMIT = 56 * 1024 * 1024
VMEM_LIMIT_WIDE = 62 * 1024 * 1024


def _params(sem, vmem=VMEM_LIMIT):
    return pltpu.CompilerParams(dimension_semantics=sem, vmem_limit_bytes=vmem)


def _const_spec(shape):
    nd = len(shape)
    return pl.BlockSpec(shape, lambda *_: (0,) * nd, pipeline_mode=pl.Buffered(1))


def _lane(shape):
    return lax.broadcasted_iota(jnp.int32, shape, len(shape) - 1)


def _head_ones(n):
    r = lax.shift_right_logical(lax.broadcasted_iota(jnp.int32, (n, n), 0), 6)
    c = lax.shift_right_logical(lax.broadcasted_iota(jnp.int32, (n, n), 1), 6)
    return (r == c).astype(BF16)


def _head_sum(v):
    w = v.shape[1]
    vb = v.astype(BF16)
    if w <= 256:
        return jnp.dot(vb, _head_ones(w), preferred_element_type=F32)
    ones = _head_ones(256)
    return jnp.concatenate([jnp.dot(vb[:, s:s + 256], ones, preferred_element_type=F32) for s in range(0, w, 256)], axis=1)


def _head_norm(seg, gain):
    rs = lax.rsqrt(_head_sum(seg * seg) * (1.0 / HEAD_DIM) + EPS)
    return seg * rs * gain


def _head_norm_bwd(seg, gain, d_out):
    rs = lax.rsqrt(_head_sum(seg * seg) * (1.0 / HEAD_DIM) + EPS)
    hat = seg * rs
    gd = d_out * gain
    d_seg = rs * (gd - hat * (_head_sum(gd * hat) * (1.0 / HEAD_DIM)))
    return d_seg, d_out * hat


def _expand_kv(v):
    r = pltpu.roll(v, 64, axis=1)
    lo = _lane(v.shape) < 64
    return jnp.concatenate([jnp.where(lo, v, r), jnp.where(lo, r, v)], axis=1)


def _fold_kv(e4):
    t0 = e4[:, 0:128] + e4[:, 128:256]
    t1 = e4[:, 256:384] + e4[:, 384:512]
    t0 = t0 + pltpu.roll(t0, 64, axis=1)
    t1 = t1 + pltpu.roll(t1, 64, axis=1)
    return jnp.where(_lane(t0.shape) < 64, t0, t1)


def _pick_lane(blk, idx):
    return jnp.sum(jnp.where(_lane(blk.shape) == idx, blk, 0.0), axis=1, keepdims=True)


def _nt(a, b):
    return lax.dot_general(a, b, (((1,), (1,)), ((), ())), preferred_element_type=F32)


def _tn(a, b):
    return lax.dot_general(a, b, (((0,), (0,)), ((), ())), preferred_element_type=F32)


def _xnorm(x2, g1, tm):
    t = x2.shape[0]

    def body(x_ref, g1_ref, xn_ref):
        x = x_ref[...]
        r = lax.rsqrt(jnp.mean(x * x, axis=-1, keepdims=True) + EPS)
        xn_ref[...] = (x * r * g1_ref[...]).astype(BF16)

    tile = pl.BlockSpec((tm, D_MODEL), lambda i: (i, 0))
    return pl.pallas_call(
        body, name="xnorm", grid=(t // tm,), in_specs=[tile, _const_spec((1, D_MODEL))], out_specs=tile,
        out_shape=jax.ShapeDtypeStruct((t, D_MODEL), BF16), compiler_params=_params(("arbitrary",)),
    )(x2, g1)


def _norm_proj(xn, w_main_t, w_f_t, gqa, gka, gqb, gkb, bf_row, s, tm):
    t = xn.shape[0]
    nt = s // tm

    def body(xn_ref, wm_ref, wf_ref, gqa_ref, gka_ref, gqb_ref, gkb_ref, b_ref,
             raw_ref, fl_ref, qa_ref, kae_ref, vae_ref, qo_ref, ko_ref, vo_ref, carry, c_ref):
        @pl.when(lax.rem(pl.program_id(0), nt) == 0)
        def _():
            carry[...] = jnp.zeros_like(carry)

        xn = xn_ref[...]
        proj = _nt(xn, wm_ref[...])
        raw_ref[...] = proj
        fl = _nt(xn, wf_ref[...])
        fl_ref[...] = fl
        qa_ref[...] = _head_norm(proj[:, 0:512], gqa_ref[...]).astype(BF16)
        kae_ref[...] = _expand_kv(_head_norm(proj[:, 512:640], gka_ref[...])).astype(BF16)
        vae_ref[...] = _expand_kv(proj[:, 640:768]).astype(BF16)

        z = fl + b_ref[...]
        e = jnp.exp(-jnp.abs(z))
        u = 1.0 + e
        log1p = jnp.where(u == 1.0, e, jnp.log(u) * (e / (u - 1.0)))
        lf = jnp.minimum(z, 0.0) - log1p
        for r0 in range(0, tm, 256):
            c_ref[r0:r0 + 256, :] = _tri_dot(256, False, lf[r0:r0 + 256]) + carry[...]
            carry[...] = c_ref[pl.ds(r0 + 255, 1), :]
        c2 = c_ref[...] * LOG2E
        qb = _head_norm(proj[:, 768:1280], gqb_ref[...]) * (SCALE * LOG2E)
        kb = _head_norm(proj[:, 1280:1792], gkb_ref[...])
        lane = _lane((tm, LANES))
        for h in range(8):
            j, half = h // 2, h % 2
            pair, blk = slice(LANES * j, LANES * (j + 1)), slice(LANES * h, LANES * (h + 1))
            feat = _spread3(c2[:, h:h + 1], (tm, LANES), (L_CK, L_CQ))
            q = _put_ones(_head_block(qb[:, pair], half), (L_CK, L_CK + 1, L_CK + 2))
            qo_ref[:, blk] = jnp.where((lane >= L_CQ) & (lane < L_CQ + 3), feat, q).astype(BF16)
            k = _put_ones(_head_block(kb[:, pair], half), tuple(range(L_CQ, L_CQ + 6)))
            ko_ref[:, blk] = jnp.where((lane >= L_CK) & (lane < L_CK + 3), -feat, k).astype(BF16)
            v = _head_block(proj[:, 1792 + LANES * j:1792 + LANES * (j + 1)], half)
            vo_ref[:, blk] = _put_ones(v, (L_ONE, L_DELTA, L_DELTA + 1, L_DELTA + 2)).astype(BF16)

    def tile(w):
        return pl.BlockSpec((tm, w), lambda i: (i, 0))

    aug = jax.ShapeDtypeStruct((t, 8 * LANES), BF16)
    return pl.pallas_call(
        body, name="norm_proj", grid=(t // tm,),
        in_specs=[tile(D_MODEL), _const_spec((MAIN_W, D_MODEL)), _const_spec((LANES, D_MODEL)),
                  _const_spec((1, 512)), _const_spec((1, 128)), _const_spec((1, 512)), _const_spec((1, 512)),
                  _const_spec((1, LANES))],
        out_specs=[tile(MAIN_W), tile(LANES), tile(512), tile(256), tile(256)] + [tile(8 * LANES)] * 3,
        out_shape=[jax.ShapeDtypeStruct((t, MAIN_W), F32),
                   jax.ShapeDtypeStruct((t, LANES), F32), jax.ShapeDtypeStruct((t, 512), BF16),
                   jax.ShapeDtypeStruct((t, 256), BF16), jax.ShapeDtypeStruct((t, 256), BF16), aug, aug, aug],
        scratch_shapes=[pltpu.VMEM((1, LANES), F32), pltpu.VMEM((tm, LANES), F32)],
        compiler_params=_params(("arbitrary",)),
    )(xn, w_main_t, w_f_t, gqa, gka, gqb, gkb, bf_row)


def _tri_dot(n, upper, v):
    r = lax.broadcasted_iota(jnp.int32, (n, n), 0)
    c = lax.broadcasted_iota(jnp.int32, (n, n), 1)
    tri = ((c >= r) if upper else (c <= r)).astype(BF16)
    hi = v.astype(BF16)
    mid = (v - hi.astype(F32)).astype(BF16)
    lo = (v - hi.astype(F32) - mid.astype(F32)).astype(BF16)
    return (jnp.dot(tri, hi, preferred_element_type=F32) + jnp.dot(tri, mid, preferred_element_type=F32)
            + jnp.dot(tri, lo, preferred_element_type=F32))


def _slope(p, hh):
    out = jnp.float32(2.0 ** -(2 * 3 + hh + 1))
    for pp in (2, 1, 0):
        out = jnp.where(p == pp, jnp.float32(2.0 ** -(2 * pp + hh + 1)), out)
    return out


def _swa_windows(ref, i, tq):
    nsub = tq // WINDOW
    cur = ref[pl.ds(pl.multiple_of(i * tq, tq), tq), :].reshape(nsub, WINDOW, LANES)
    first = ref[pl.ds(pl.multiple_of(jnp.maximum(i * tq - WINDOW, 0), WINDOW), WINDOW), :].reshape(1, WINDOW, LANES)
    return jnp.concatenate([jnp.concatenate([first, cur[0:nsub - 1]], axis=0), cur], axis=1)


def _both_heads(x3, lo):
    zero = jnp.zeros_like(x3)
    return jnp.concatenate([jnp.where(lo, x3, zero), jnp.where(lo, zero, x3)], axis=0)


def _swa_sinks(sink_ref, p, nsub):
    is_a = lax.broadcasted_iota(jnp.int32, (2 * nsub, 1, 1), 0) < nsub
    sinks = sink_ref[...]
    return jnp.where(is_a, _pick_lane(sinks, 2 * p).reshape(1, 1, 1), _pick_lane(sinks, 2 * p + 1).reshape(1, 1, 1))


def _swa_bias(p, i, nsub, keys_first):
    shape = (1, 2 * WINDOW, WINDOW) if keys_first else (1, WINDOW, 2 * WINDOW)
    qi = lax.broadcasted_iota(jnp.int32, shape, 2 if keys_first else 1)
    ki = lax.broadcasted_iota(jnp.int32, shape, 1 if keys_first else 2)
    dist = qi + WINDOW - ki
    band = (dist >= 0) & (dist < WINDOW)
    tiles = []
    for hh in range(2):
        bias = jnp.where(band, -_slope(p, hh) * dist.astype(F32), NEG_INF)
        tiles += [jnp.where((i == 0) & (ki < WINDOW), NEG_INF, bias)] + [bias] * (nsub - 1)
    return jnp.concatenate(tiles, axis=0)


def _swa_fwd(qa, kae, vae, sink_row, nb, s, tq):
    t = qa.shape[0]
    nq = s // tq
    nsub = tq // WINDOW

    def body(q_ref, k_ref, v_ref, sink_ref, o_ref, lse_ref):
        p, i = pl.program_id(1), pl.program_id(2)
        lo = _lane((1, 1, LANES)) < 64
        kk, vv = _swa_windows(k_ref, i, tq), _swa_windows(v_ref, i, tq)
        qs = (q_ref[...].astype(F32) * SCALE).astype(BF16).reshape(nsub, WINDOW, LANES)
        q8 = _both_heads(qs, lo)
        s8 = jnp.einsum("bqd,bkd->bqk", q8, jnp.concatenate([kk, kk], axis=0), preferred_element_type=F32)
        sink = _swa_sinks(sink_ref, p, nsub)
        s8 = s8 + _swa_bias(p, i, nsub, False)
        m = jnp.maximum(jnp.max(s8, axis=2, keepdims=True), sink)
        e = jnp.exp(s8 - m)
        den = jnp.sum(e, axis=2, keepdims=True) + jnp.exp(sink - m)
        pr = (e * (1.0 / den)).astype(BF16)
        o8 = jnp.einsum("bqk,bkd->bqd", pr, jnp.concatenate([vv, vv], axis=0), preferred_element_type=F32)
        lse8 = m + jnp.log(den)
        o_ref[...] = jnp.where(lo, o8[0:nsub], o8[nsub:]).astype(BF16).reshape(tq, LANES)
        lse_ref[...] = jnp.where(lo, lse8[0:nsub], lse8[nsub:]).reshape(tq, LANES)

    return pl.pallas_call(
        body, name="swa_fwd", grid=(nb, 4, nq),
        in_specs=[pl.BlockSpec((tq, LANES), lambda b, p, i: (b * nq + i, p)),
                  pl.BlockSpec((s, LANES), lambda b, p, i: (b, lax.shift_right_logical(p, 1))),
                  pl.BlockSpec((s, LANES), lambda b, p, i: (b, lax.shift_right_logical(p, 1))),
                  pl.BlockSpec((1, LANES), lambda b, p, i: (0, 0))],
        out_specs=[pl.BlockSpec((tq, LANES), lambda b, p, i: (b * nq + i, p)),
                   pl.BlockSpec((None, tq, LANES), lambda b, p, i: (p, b * nq + i, 0))],
        out_shape=[jax.ShapeDtypeStruct((t, 512), BF16), jax.ShapeDtypeStruct((4, t, LANES), F32)],
        compiler_params=_params(("arbitrary", "arbitrary", "arbitrary")),
    )(qa, kae, vae, sink_row)


def _swa_bwd(qa, kae, vae, do_a, sink_row, lse, delta, nb, s, tq):
    t = qa.shape[0]
    nq = s // tq
    nsub = tq // WINDOW

    def body(q_ref, do_ref, k_ref, v_ref, sink_ref, lse_ref, dl_ref, dq_ref, dk_ref, dv_ref, ds_ref):
        p, i = pl.program_id(1), pl.program_id(2)

        @pl.when(i == 0)
        def _():
            ds_ref[...] = jnp.zeros_like(ds_ref)

        lo = _lane((1, 1, LANES)) < 64
        kk, vv = _swa_windows(k_ref, i, tq), _swa_windows(v_ref, i, tq)
        kks = (kk.astype(F32) * SCALE).astype(BF16)
        k8, v8 = jnp.concatenate([kks, kks], axis=0), jnp.concatenate([vv, vv], axis=0)
        q8 = _both_heads(q_ref[...].reshape(nsub, WINDOW, LANES), lo)
        do8 = _both_heads(do_ref[...].reshape(nsub, WINDOW, LANES), lo)
        cur = pl.multiple_of(i * tq, tq)
        sub = lax.broadcasted_iota(jnp.int32, (WINDOW, WINDOW), 0)
        lse_t = [lse_ref[u * WINDOW:(u + 1) * WINDOW, :].T for u in range(nsub)]
        dl_t = [dl_ref[u * WINDOW:(u + 1) * WINDOW, :].T for u in range(nsub)]
        lse8 = jnp.concatenate([t_[64 * hh:64 * hh + 1, :].reshape(1, 1, WINDOW) for hh in range(2) for t_ in lse_t], axis=0)
        dl8 = jnp.concatenate([jnp.sum(jnp.where(sub == 2 * p + hh, t_, 0.0), axis=0, keepdims=True).reshape(1, 1, WINDOW)
                               for hh in range(2) for t_ in dl_t], axis=0)
        sink = _swa_sinks(sink_ref, p, nsub)
        st = jnp.einsum("bkd,bqd->bkq", k8, q8, preferred_element_type=F32) + _swa_bias(p, i, nsub, True) - lse8
        pt = jnp.exp(st)
        dpt = jnp.einsum("bkd,bqd->bkq", v8, do8, preferred_element_type=F32)
        dst = pt * (dpt - dl8)
        ptb, dstb = pt.astype(BF16), dst.astype(BF16)
        dv8 = jnp.einsum("bkq,bqd->bkd", ptb, do8, preferred_element_type=F32)
        dk8 = jnp.einsum("bkq,bqd->bkd", dstb, q8, preferred_element_type=F32) * SCALE
        dq8 = jnp.einsum("bkq,bkd->bqd", dstb, k8, preferred_element_type=F32)
        dq_ref[...] = jnp.where(lo, dq8[0:nsub], dq8[nsub:]).reshape(tq, LANES)

        psd = jnp.exp(sink - lse8) * dl8
        row_h = lax.broadcasted_iota(jnp.int32, (8, LANES), 0)
        for hh in range(2):
            tot = jnp.sum(jnp.sum(psd[hh * nsub:(hh + 1) * nsub], axis=2, keepdims=True), axis=0, keepdims=True)
            ds_ref[...] += jnp.where(row_h == hh, -tot.reshape(1, 1), 0.0)

        prev = pl.multiple_of(jnp.maximum(i * tq - WINDOW, 0), WINDOW)
        for g8, g_ref in ((dk8, dk_ref), (dv8, dv_ref)):
            g4 = g8[0:nsub] + g8[nsub:]
            own, before = g4[:, WINDOW:, :], g4[:, 0:WINDOW, :]
            shifted = jnp.concatenate([before[1:nsub], jnp.zeros((1, WINDOW, LANES), F32)], axis=0)
            g_ref[pl.ds(cur, tq), :] = (own + shifted).reshape(tq, LANES)
            g_ref[pl.ds(prev, WINDOW), :] += before[0]

    return pl.pallas_call(
        body, name="swa_bwd", grid=(nb, 4, nq),
        in_specs=[pl.BlockSpec((tq, LANES), lambda b, p, i: (b * nq + i, p)),
                  pl.BlockSpec((tq, LANES), lambda b, p, i: (b * nq + i, p)),
                  pl.BlockSpec((s, LANES), lambda b, p, i: (b, lax.shift_right_logical(p, 1))),
                  pl.BlockSpec((s, LANES), lambda b, p, i: (b, lax.shift_right_logical(p, 1))),
                  pl.BlockSpec((1, LANES), lambda b, p, i: (0, 0)),
                  pl.BlockSpec((None, tq, LANES), lambda b, p, i: (p, b * nq + i, 0)),
                  pl.BlockSpec((tq, LANES), lambda b, p, i: (b * nq + i, 0))],
        out_specs=[pl.BlockSpec((tq, LANES), lambda b, p, i: (b * nq + i, p)),
                   pl.BlockSpec((s, LANES), lambda b, p, i: (b, p)),
                   pl.BlockSpec((s, LANES), lambda b, p, i: (b, p)),
                   pl.BlockSpec((None, None, 8, LANES), lambda b, p, i: (b, p, 0, 0))],
        out_shape=[jax.ShapeDtypeStruct((t, 512), F32), jax.ShapeDtypeStruct((t, 512), F32),
                   jax.ShapeDtypeStruct((t, 512), F32), jax.ShapeDtypeStruct((nb, 4, 8, LANES), F32)],
        compiler_params=_params(("arbitrary", "arbitrary", "arbitrary")),
    )(qa, do_a, kae, vae, sink_row, lse, delta)


MESH = pl.DeviceIdType.MESH
ANY = pl.BlockSpec(memory_space=pl.ANY)
N_SEM = 7


def _gather_steps(pairs, send_sems, recv_sems, local_sems):
    x, y, c = lax.axis_index("x"), lax.axis_index("y"), lax.axis_index("c")
    me, sibling = (x, y, c), (x, y, 1 - c)
    chips = [(1 - x, y), (x, 1 - y), (1 - x, 1 - y)]
    mine, first, passed, landed, last = [], [], [], [], []
    for a, (x_ref, out_ref) in enumerate(pairs):
        def slot(px, py, pc, out_ref=out_ref):
            return out_ref.at[4 * px + 2 * py + pc]

        def copy(k, block, to, src=None, a=a, slot=slot):
            return pltpu.make_async_remote_copy(
                src_ref=slot(*block) if src is None else src, dst_ref=slot(*block),
                send_sem=send_sems.at[N_SEM * a + k], recv_sem=recv_sems.at[N_SEM * a + k], device_id=to, device_id_type=MESH)

        mine.append(pltpu.make_async_copy(x_ref, slot(*me), local_sems.at[a]))
        first += [copy(0, me, sibling, src=x_ref)] + [copy(1 + j, me, (*chip, c), src=x_ref) for j, chip in enumerate(chips)]
        passed += [copy(4 + j, (*chip, c), sibling) for j, chip in enumerate(chips)]
        landed += [copy(1 + j, (*chip, c), me) for j, chip in enumerate(chips)]
        last += [copy(0, sibling, me)] + [copy(4 + j, (*chip, 1 - c), me) for j, chip in enumerate(chips)]

    def start():
        for cp in mine + first:
            cp.start()

    def forward():
        for arrived, onward in zip(landed, passed):
            arrived.wait_recv()
            onward.start()

    def finish():
        for cp in last:
            cp.wait_recv()
        for cp in first + passed:
            cp.wait_send()
        for cp in mine:
            cp.wait()

    return start, forward, finish


def _exchange_steps(pairs, send_sems, recv_sems, local_sems):
    x, y, c = lax.axis_index("x"), lax.axis_index("y"), lax.axis_index("c")
    my_id = 4 * x + 2 * y + c
    local, remote = [], []
    for a, (src, dst) in enumerate(pairs):
        local.append(pltpu.make_async_copy(src.at[my_id], dst.at[my_id], local_sems.at[a]))
        for k in range(1, N_DEV):
            px = 1 - x if k & 4 else x
            py = 1 - y if k & 2 else y
            pc = 1 - c if k & 1 else c
            remote.append(pltpu.make_async_remote_copy(
                src_ref=src.at[4 * px + 2 * py + pc], dst_ref=dst.at[my_id],
                send_sem=send_sems.at[N_SEM * a + k - 1], recv_sem=recv_sems.at[N_SEM * a + k - 1],
                device_id=(px, py, pc), device_id_type=MESH))

    def start():
        for cp in local + remote:
            cp.start()

    def finish():
        for cp in remote:
            cp.wait_recv()
        for cp in remote:
            cp.wait_send()
        for cp in local:
            cp.wait()

    return start, finish


L_ONE = 64
L_CK = 65
L_CQ = 68
L_LSE = 71
L_DELTA = 74


def _head_block(pair, half):
    y = pair if half == 0 else pltpu.roll(pair, 64, axis=1)
    return jnp.where(_lane(pair.shape) < 64, y, 0.0)


def _put3(blk, lane0, col):
    lane = _lane(blk.shape)
    hi = col.astype(BF16).astype(F32)
    mid = (col - hi).astype(BF16).astype(F32)
    lo = (col - hi - mid).astype(BF16).astype(F32)
    return jnp.where(lane == lane0, hi, jnp.where(lane == lane0 + 1, mid, jnp.where(lane == lane0 + 2, lo, blk)))


def _spread3(col, shape, lane0s):
    lane = _lane(shape)
    hi = col.astype(BF16).astype(F32)
    mid = (col - hi).astype(BF16).astype(F32)
    lo = (col - hi - mid).astype(BF16).astype(F32)

    def at(k):
        return functools.reduce(jnp.logical_or, [lane == ln + k for ln in lane0s])

    return jnp.where(at(0), hi, jnp.where(at(1), mid, jnp.where(at(2), lo, 0.0)))


def _put_ones(blk, lanes):
    lane = _lane(blk.shape)
    hit = functools.reduce(jnp.logical_or, [lane == ln for ln in lanes])
    return jnp.where(hit, 1.0, blk)


def _to_pairs(ref):
    out = []
    for j in range(4):
        a, b = ref[:, 2 * LANES * j:2 * LANES * j + LANES], ref[:, 2 * LANES * j + LANES:2 * LANES * (j + 1)]
        out.append(jnp.where(_lane(a.shape) < 64, a, pltpu.roll(b, 64, axis=1)))
    return jnp.concatenate(out, axis=1)


def _fox_fwd(q_aug, k_aug, v_aug, nb, s, bt, shards=()):
    t = q_aug.shape[0]
    nq = s // bt
    n_in, n_sh = 3, len(shards)

    def body(*refs):
        q_ref, k_ref, v_ref = refs[:n_in]
        o_ref, ql_ref = refs[n_in + n_sh:n_in + n_sh + 2]
        if shards:
            srcs, dsts = refs[n_in:n_in + n_sh], refs[n_in + n_sh + 2:n_in + 2 * n_sh + 2]
            start, forward, finish = _gather_steps(list(zip(srcs, dsts)), *refs[n_in + 2 * n_sh + 2:])
            step = (pl.program_id(0) * 4 + pl.program_id(1)) * nq + pl.program_id(2)
            pl.when(step == 0)(start)
            pl.when(step == nb * 3 * nq)(forward)
        i = pl.program_id(2)
        sls = [slice(LANES * hh, LANES * (hh + 1)) for hh in range(2)]
        qhs = [q_ref[:, sl] for sl in sls]

        def update(m, acc, qrows, start, size, sl, causal):
            sc = _nt(qrows, k_ref[pl.ds(start, size), sl])
            if causal:
                row = lax.broadcasted_iota(jnp.int32, sc.shape, 0)
                col = lax.broadcasted_iota(jnp.int32, sc.shape, 1)
                sc = jnp.where(row >= col, sc, NEG_INF)
            m_new = jnp.maximum(m, jnp.max(sc, axis=1, keepdims=True))
            pr = jnp.exp2(sc - m_new).astype(BF16)
            acc = jnp.exp2(m - m_new) * acc + jnp.dot(pr, v_ref[pl.ds(start, size), sl], preferred_element_type=F32)
            return m_new, acc

        def blk(kb_i, carry):
            start = pl.multiple_of(kb_i * bt, bt)
            return tuple(update(m, acc, qh, start, bt, sl, False) for (m, acc), qh, sl in zip(carry, qhs, sls))

        def diag_blk(carry):
            start = pl.multiple_of(i * bt, bt)
            return tuple(update(m, acc, qh, start, bt, sl, True) for (m, acc), qh, sl in zip(carry, qhs, sls))

        init = tuple((jnp.full((bt, 1), NEG_INF, F32), jnp.zeros((bt, LANES), F32)) for _ in range(2))
        carry = lax.fori_loop(0, i, blk, init)
        outs = []
        for (m, acc), qh, sl in zip(diag_blk(carry), qhs, sls):
            l = acc[:, L_ONE:L_ONE + 1]
            outs.append(acc * (1.0 / l))
            ql_ref[:, sl] = _put3(qh.astype(F32), L_LSE, -(m + jnp.log(l) * LOG2E)).astype(BF16)
        o_ref[...] = jnp.where(_lane((1, LANES)) < 64, outs[0], pltpu.roll(outs[1], 64, axis=1)).astype(BF16)
        if shards:
            pl.when(step == nb * 4 * nq - 1)(finish)

    in_specs = [pl.BlockSpec((bt, 2 * LANES), lambda b, j, i: (b * nq + i, j)),
                pl.BlockSpec((s, 2 * LANES), lambda b, j, i: (b, j)),
                pl.BlockSpec((s, 2 * LANES), lambda b, j, i: (b, j))]
    out_specs = [pl.BlockSpec((bt, LANES), lambda b, j, i: (b * nq + i, j)),
                 pl.BlockSpec((bt, 2 * LANES), lambda b, j, i: (b * nq + i, j))]
    out_shape = [jax.ShapeDtypeStruct((t, 512), BF16), jax.ShapeDtypeStruct((t, 8 * LANES), BF16)]
    args, scratch = [q_aug, k_aug, v_aug, *shards], []
    if shards:
        in_specs += [ANY] * n_sh
        out_specs += [ANY] * n_sh
        out_shape += [jax.ShapeDtypeStruct((N_DEV,) + sh.shape, sh.dtype) for sh in shards]
        scratch = [pltpu.SemaphoreType.DMA((N_SEM * n_sh,)), pltpu.SemaphoreType.DMA((N_SEM * n_sh,)),
                   pltpu.SemaphoreType.DMA((n_sh,))]
    return pl.pallas_call(
        body, name="fox_fwd", grid=(nb, 4, nq), in_specs=in_specs, out_specs=out_specs, out_shape=out_shape,
        scratch_shapes=scratch, compiler_params=_params(("arbitrary", "arbitrary", "arbitrary")),
    )(*args)


def _fox_bwd(ql_aug, k_aug, v_aug, do_aug, nb, s, bt, exch=()):
    t = ql_aug.shape[0]
    nk = s // bt
    n_in, n_out, n_ex = 4, 3, len(exch)

    def body(*refs):
        q_ref, do_ref, k_ref, v_ref = refs[:n_in]
        dq_ref, dk_ref, dv_ref = refs[n_in + n_ex:n_in + n_ex + n_out]
        if exch:
            srcs = refs[n_in:n_in + n_ex]
            dsts = refs[n_in + n_ex + n_out:n_in + 2 * n_ex + n_out]
            start, finish = _exchange_steps(list(zip(srcs, dsts)), *refs[n_in + 2 * n_ex + n_out:])
            step = (pl.program_id(0) * 4 + pl.program_id(1)) * nk + pl.program_id(2)
            pl.when(step == 0)(start)
        kb_i = pl.program_id(2)

        @pl.when(kb_i == 0)
        def _():
            dq_ref[...] = jnp.zeros_like(dq_ref)

        row = lax.broadcasted_iota(jnp.int32, (bt, bt), 0)
        col = lax.broadcasted_iota(jnp.int32, (bt, bt), 1)
        sls = [slice(LANES * hh, LANES * (hh + 1)) for hh in range(2)]
        khs, vhs = [k_ref[:, sl] for sl in sls], [v_ref[:, sl] for sl in sls]

        def blk(qi, carry, diag):
            start = pl.multiple_of(qi * bt, bt)
            new = []
            for (dk_a, dv_a), kh, vh, sl in zip(carry, khs, vhs, sls):
                qblk, doblk = q_ref[pl.ds(start, bt), sl], do_ref[pl.ds(start, bt), sl]
                st = _nt(kh, qblk)
                if diag:
                    pt = jnp.where(col >= row, jnp.exp2(jnp.where(col >= row, st, 0.0)), 0.0)
                else:
                    pt = jnp.exp2(st)
                dst = pt * _nt(vh, doblk)
                ptb, dstb = pt.astype(BF16), dst.astype(BF16)
                dv_a = dv_a + jnp.dot(ptb, doblk, preferred_element_type=F32)
                dk_a = dk_a + jnp.dot(dstb, qblk, preferred_element_type=F32)
                dq_ref[pl.ds(start, bt), sl] += _tn(dstb, kh)
                new.append((dk_a, dv_a))
            return tuple(new)

        zero = jnp.zeros((bt, LANES), F32)
        carry = blk(kb_i, ((zero, zero), (zero, zero)), True)
        carry = lax.fori_loop(kb_i + 1, nk, lambda qi, c: blk(qi, c, False), carry)
        for (dk_acc, dv_acc), sl in zip(carry, sls):
            dk_ref[:, sl] = dk_acc
            dv_ref[:, sl] = dv_acc
        if exch:
            pl.when(step == nb * 4 * nk - 1)(finish)

    scratch = []
    if exch:
        scratch = [pltpu.SemaphoreType.DMA((N_SEM * n_ex,)), pltpu.SemaphoreType.DMA((N_SEM * n_ex,)),
                   pltpu.SemaphoreType.DMA((n_ex,))]
    whole = pl.BlockSpec((s, 2 * LANES), lambda b, j, kb_i: (b, j))
    tile = pl.BlockSpec((bt, 2 * LANES), lambda b, j, kb_i: (b * nk + kb_i, j))
    shp = jax.ShapeDtypeStruct((t, 8 * LANES), F32)
    return pl.pallas_call(
        body, name="fox_bwd", grid=(nb, 4, nk),
        in_specs=[whole, whole, tile, tile] + [ANY] * n_ex,
        out_specs=[whole, tile, tile] + [ANY] * n_ex,
        out_shape=[shp, shp, shp] + [jax.ShapeDtypeStruct(e.shape, e.dtype) for e in exch],
        scratch_shapes=scratch, compiler_params=_params(("arbitrary", "arbitrary", "arbitrary")),
    )(ql_aug, do_aug, k_aug, v_aug, *exch)


FF_BLK = D_FF // N_DEV


def _mlp_fwd(x2, ma, mb, tgt, w_out, g2, w_up, w_down, tm):
    t = x2.shape[0]

    def body(x_ref, ma_ref, mb_ref, tg_ref, wo_ref, g2_ref, wu_ref, wd_ref,
             h_ref, hn_ref, hid_ref, dy_ref, dyb_ref, loss_ref):
        @pl.when(pl.program_id(0) == 0)
        def _():
            loss_ref[...] = jnp.zeros_like(loss_ref)

        h = (x_ref[...] + jnp.dot(ma_ref[...], wo_ref[0:512, :], preferred_element_type=F32)
             + jnp.dot(mb_ref[...], wo_ref[512:1024, :], preferred_element_type=F32))
        h_ref[...] = h
        r = lax.rsqrt(jnp.mean(h * h, axis=-1, keepdims=True) + EPS)
        hn = (h * r * g2_ref[...]).astype(BF16)
        hn_ref[...] = hn
        for d in range(N_DEV):
            u = jnp.maximum(jnp.dot(hn, wu_ref[d], preferred_element_type=F32), 0.0)
            hid_ref[:, FF_BLK * d:FF_BLK * (d + 1)] = (u * u).astype(BF16)
        y = h + jnp.dot(hid_ref[...], wd_ref[...], preferred_element_type=F32)
        err = y - tg_ref[...]
        dy = err * (1.0 / D_MODEL)
        dy_ref[...] = dy
        dyb_ref[...] = dy.astype(BF16)
        part =0.5 * jnp.sum(jnp.sum(err * err, axis=1, keepdims=True) * (1.0 / D_MODEL), axis=0, keepdims=True)
        loss_ref[...] += part

    def tile(w):
        return pl.BlockSpec((tm, w), lambda i: (i, 0))

    return pl.pallas_call(
        body, name="mlp_fwd", grid=(t // tm,),
        in_specs=[tile(D_MODEL), tile(512), tile(512), tile(D_MODEL), _const_spec((D_MODEL, D_MODEL)),
                  _const_spec((1, D_MODEL)), _const_spec((N_DEV, D_MODEL, FF_BLK)), _const_spec((D_FF, D_MODEL))],
        out_specs=[tile(D_MODEL), tile(D_MODEL), tile(D_FF), tile(D_MODEL), tile(D_MODEL),
                   pl.BlockSpec((8, LANES), lambda i: (0, 0))],
        out_shape=[jax.ShapeDtypeStruct((t, D_MODEL), F32), jax.ShapeDtypeStruct((t, D_MODEL), BF16),
                   jax.ShapeDtypeStruct((t, D_FF), BF16), jax.ShapeDtypeStruct((t, D_MODEL), F32),
                   jax.ShapeDtypeStruct((t, D_MODEL), BF16), jax.ShapeDtypeStruct((8, LANES), F32)],
        compiler_params=_params(("arbitrary",)),
    )(x2, ma, mb, tgt, w_out, g2, w_up, w_down)


def _mlp_bwd(dy, hid, h, ma, mb, w_down, w_up_t, w_out, g2, tm):
    t = dy.shape[0]

    def body(dy_ref, hid_ref, h_ref, ma_ref, mb_ref, wd_ref, wut_ref, wo_ref, g2_ref,
             du_ref, dh_ref, dhb_ref, dma_ref, dob_ref, dla_ref, gg_ref):
        @pl.when(pl.program_id(0) == 0)
        def _():
            gg_ref[...] = jnp.zeros_like(gg_ref)

        dy = dy_ref[...]
        d_hid = _nt(dy.astype(BF16), wd_ref[...])
        du = (d_hid * (2.0 * jnp.sqrt(hid_ref[...].astype(F32)))).astype(BF16)
        du_ref[...] = du
        d_hn = jnp.dot(du, wut_ref[...], preferred_element_type=F32)
        h = h_ref[...]
        r = lax.rsqrt(jnp.mean(h * h, axis=-1, keepdims=True) + EPS)
        hat = h * r
        gd = d_hn * g2_ref[...]
        dh = dy + r * (gd - hat * jnp.mean(gd * hat, axis=-1, keepdims=True))
        gg_ref[...] += jnp.sum(d_hn * hat, axis=0, keepdims=True)
        dh_ref[...] = dh
        dhb = dh.astype(BF16)
        dhb_ref[...] = dhb
        dm = _nt(dhb, wo_ref[...]).astype(BF16)
        dma, dmb = dm[:, 0:512], dm[:, 512:1024]
        dma_ref[...] = dma
        sel = (lax.shift_right_logical(lax.broadcasted_iota(jnp.int32, (512, LANES), 0), 6)
               == lax.broadcasted_iota(jnp.int32, (512, LANES), 1)).astype(BF16)
        dla_ref[...] = jnp.dot((dma.astype(F32) * ma_ref[...].astype(F32)).astype(BF16), sel, preferred_element_type=F32)
        dmb32 = dmb.astype(F32)
        dlb = jnp.dot((dmb32 * mb_ref[...].astype(F32)).astype(BF16), sel, preferred_element_type=F32)
        for hd in range(8):
            blk = _head_block(dmb32[:, LANES * (hd // 2):LANES * (hd // 2 + 1)], hd % 2)
            dob_ref[:, LANES * hd:LANES * (hd + 1)] = _put3(blk, L_DELTA, -dlb[:, hd:hd + 1]).astype(BF16)

    def tile(w):
        return pl.BlockSpec((tm, w), lambda i: (i, 0))

    return pl.pallas_call(
        body, name="mlp_bwd", grid=(t // tm,),
        in_specs=[tile(D_MODEL), tile(D_FF), tile(D_MODEL), tile(512), tile(512), _const_spec((D_FF, D_MODEL)),
                  _const_spec((D_FF, D_MODEL)), _const_spec((D_MODEL, D_MODEL)), _const_spec((1, D_MODEL))],
        out_specs=[tile(D_FF), tile(D_MODEL), tile(D_MODEL), tile(512), tile(8 * LANES), tile(LANES),
                   pl.BlockSpec((1, D_MODEL), lambda i: (0, 0))],
        out_shape=[jax.ShapeDtypeStruct((t, D_FF), BF16), jax.ShapeDtypeStruct((t, D_MODEL), F32),
                   jax.ShapeDtypeStruct((t, D_MODEL), BF16), jax.ShapeDtypeStruct((t, 512), BF16),
                   jax.ShapeDtypeStruct((t, 8 * LANES), BF16), jax.ShapeDtypeStruct((t, LANES), F32),
                   jax.ShapeDtypeStruct((1, D_MODEL), F32)],
        compiler_params=_params(("arbitrary",), VMEM_LIMIT_WIDE),
    )(dy, hid, h, ma, mb, w_down, w_up_t, w_out, g2)


def _wgrad(a, b, name, bm, bn, tk, out_dtype=F32, col_blocks=False, a2=None):
    t, m = a.shape
    n = b.shape[1]
    bm, bn = min(bm, m), min(bn, n)
    nk = t // tk

    def body(*refs):
        if a2 is None:
            a_ref, b_ref, o_ref, acc = refs
        else:
            a_ref, b_ref, a2_ref, o_ref, o2_ref, acc, acc2 = refs
        i, k = pl.program_id(0), pl.program_id(2)

        @pl.when(k == 0)
        def _():
            acc[...] = jnp.zeros_like(acc)

        acc[...] += _tn(a_ref[...], b_ref[...])

        @pl.when(k == nk - 1)
        def _():
            o_ref[...] = acc[...].astype(out_dtype)

        if a2 is not None:
            @pl.when((i == 0) & (k == 0))
            def _():
                acc2[...] = jnp.zeros_like(acc2)

            @pl.when(i == 0)
            def _():
                acc2[...] += _tn(a2_ref[...], b_ref[...])

            @pl.when((i == 0) & (k == nk - 1))
            def _():
                o2_ref[...] = acc2[...]

    if col_blocks:
        out_spec = pl.BlockSpec((None, bm, bn), lambda i, j, k: (j, i, 0))
        out_shape = jax.ShapeDtypeStruct((n // bn, m, bn), out_dtype)
    else:
        out_spec = pl.BlockSpec((bm, bn), lambda i, j, k: (i, j))
        out_shape = jax.ShapeDtypeStruct((m, n), out_dtype)
    in_specs = [pl.BlockSpec((tk, bm), lambda i, j, k: (k, i)), pl.BlockSpec((tk, bn), lambda i, j, k: (k, j))]
    out_specs, out_shapes, scratch, args = [out_spec], [out_shape], [pltpu.VMEM((bm, bn), F32)], [a, b]
    if a2 is not None:
        m2 = a2.shape[1]
        in_specs.append(pl.BlockSpec((tk, m2), lambda i, j, k: (k, 0)))
        out_specs.append(pl.BlockSpec((m2, n), lambda i, j, k: (0, 0)))
        out_shapes.append(jax.ShapeDtypeStruct((m2, n), F32))
        scratch.append(pltpu.VMEM((m2, n), F32))
        args.append(a2)
    out = pl.pallas_call(
        body, name=name, grid=(m // bm, n // bn, nk), in_specs=in_specs, out_specs=out_specs, out_shape=out_shapes,
        scratch_shapes=scratch, compiler_params=_params(("arbitrary", "arbitrary", "arbitrary")),
    )(*args)
    return out[0] if a2 is None else out


def _proj_bwd(raw, dqa, dkae, dvae, dqb, dkb, dvb, fl, bf_row, x2, dh, w_main_t, w_f_t, g1, gqa, gka, gqb, gkb, nb, s, tm):
    t = x2.shape[0]
    nt = s // tm

    def body(raw_ref, dqa_ref, dkae_ref, dvae_ref, dqb_ref, dkb_ref, dvb_ref, fl_ref, b_ref, x_ref, dh_ref,
             wmt_ref, wft_ref, g1_ref, gqa_ref, gka_ref, gqb_ref, gkb_ref,
             dx_ref, dp_ref, dfb_ref, ggqa_ref, ggka_ref, ggqb_ref, ggkb_ref, gg1_ref, gb_ref, carry, dlf_ref):
        @pl.when((pl.program_id(0) == 0) & (pl.program_id(1) == 0))
        def _():
            for r in (ggqa_ref, ggka_ref, ggqb_ref, ggkb_ref, gg1_ref, gb_ref):
                r[...] = jnp.zeros_like(r)

        @pl.when(pl.program_id(1) == 0)
        def _():
            carry[...] = jnp.zeros_like(carry)

        lane = _lane((tm, LANES))
        dc = jnp.zeros((tm, LANES), F32)
        for hd in range(8):
            col = (dqb_ref[:, LANES * hd + L_CQ:LANES * hd + L_CQ + 1] - dkb_ref[:, LANES * hd + L_CK:LANES * hd + L_CK + 1])
            dc = jnp.where(lane == hd, col, dc)
        dlf_ref[...] = _tri_dot(tm, True, dc) + carry[...]
        carry[...] = dlf_ref[pl.ds(0, 1), :]
        dfl = dlf_ref[...] * (1.0 / (1.0 + jnp.exp(fl_ref[...] + b_ref[...])))
        gb_ref[...] += jnp.sum(dfl, axis=0, keepdims=True)

        raw = raw_ref[...]
        d_qa, p_qa = _head_norm_bwd(raw[:, 0:512], gqa_ref[...], dqa_ref[...])
        d_ka, p_ka = _head_norm_bwd(raw[:, 512:640], gka_ref[...], _fold_kv(dkae_ref[...]))
        d_va = _fold_kv(dvae_ref[...])
        d_qb, p_qb = _head_norm_bwd(raw[:, 768:1280], gqb_ref[...], _to_pairs(dqb_ref) * SCALE)
        d_kb, p_kb = _head_norm_bwd(raw[:, 1280:1792], gkb_ref[...], _to_pairs(dkb_ref) * (1.0 / LOG2E))
        ggqa_ref[...] += jnp.sum(p_qa, axis=0, keepdims=True)
        ggka_ref[...] += jnp.sum(p_ka, axis=0, keepdims=True)
        ggqb_ref[...] += jnp.sum(p_qb, axis=0, keepdims=True)
        ggkb_ref[...] += jnp.sum(p_kb, axis=0, keepdims=True)
        dproj = jnp.concatenate([d_qa, d_ka, d_va, d_qb, d_kb, _to_pairs(dvb_ref)], axis=1).astype(BF16)
        dp_ref[...] = dproj
        dfb = dfl.astype(BF16)
        dfb_ref[...] = dfb
        d_xn = (jnp.dot(dproj, wmt_ref[...], preferred_element_type=F32)
                + jnp.dot(dfb, wft_ref[...], preferred_element_type=F32))
        x = x_ref[...]
        r = lax.rsqrt(jnp.mean(x * x, axis=-1, keepdims=True) + EPS)
        hat = x * r
        gd = d_xn * g1_ref[...]
        dx_ref[...] = dh_ref[...] + r * (gd - hat * jnp.mean(gd * hat, axis=-1, keepdims=True))
        gg1_ref[...] += jnp.sum(d_xn * hat, axis=0, keepdims=True)

    def tile(w):
        return pl.BlockSpec((tm, w), lambda b, i: (b * nt + (nt - 1 - i), 0))

    def acc(w):
        return pl.BlockSpec((1, w), lambda b, i: (0, 0))

    return pl.pallas_call(
        body, name="proj_bwd", grid=(nb, nt),
        in_specs=[tile(MAIN_W), tile(512), tile(512), tile(512), tile(8 * LANES), tile(8 * LANES), tile(8 * LANES), tile(LANES),
                  _const_spec((1, LANES)), tile(D_MODEL), tile(D_MODEL), _const_spec((MAIN_W, D_MODEL)),
                  _const_spec((LANES, D_MODEL)), _const_spec((1, D_MODEL)), _const_spec((1, 512)), _const_spec((1, 128)),
                  _const_spec((1, 512)), _const_spec((1, 512))],
        out_specs=[tile(D_MODEL), tile(MAIN_W), tile(LANES), acc(512), acc(128), acc(512), acc(512), acc(D_MODEL), acc(LANES)],
        out_shape=[jax.ShapeDtypeStruct((t, D_MODEL), F32), jax.ShapeDtypeStruct((t, MAIN_W), BF16),
                   jax.ShapeDtypeStruct((t, LANES), BF16), jax.ShapeDtypeStruct((1, 512), F32),
                   jax.ShapeDtypeStruct((1, 128), F32), jax.ShapeDtypeStruct((1, 512), F32),
                   jax.ShapeDtypeStruct((1, 512), F32), jax.ShapeDtypeStruct((1, D_MODEL), F32),
                   jax.ShapeDtypeStruct((1, LANES), F32)],
        scratch_shapes=[pltpu.VMEM((1, LANES), F32), pltpu.VMEM((tm, LANES), F32)],
        compiler_params=_params(("arbitrary", "arbitrary"), VMEM_LIMIT_WIDE),
    )(raw, dqa, dkae, dvae, dqb, dkb, dvb, fl, bf_row, x2, dh, w_main_t, w_f_t, g1, gqa, gka, gqb, gkb)


IN_PAD = 304


def _local_step(x, tgt, w_in_t, rest, g1, b_forget, qna, kna, sinks, qnb, knb, g2,
                tm=512, bt=1024, btf=1024, tq=4096, wk=4096, wkb=8192, distributed=False):
    nb, s, _ = x.shape
    t = nb * s
    x2, tgt2 = x.reshape(t, D_MODEL), tgt.reshape(t, D_MODEL)
    g1r, g2r = g1.reshape(1, D_MODEL), g2.reshape(1, D_MODEL)
    gqa, gka = jnp.tile(qna, 8).reshape(1, 512), jnp.tile(kna, 2).reshape(1, 128)
    gqb, gkb = jnp.tile(qnb, 8).reshape(1, 512), jnp.tile(knb, 8).reshape(1, 512)
    bf_row = jnp.pad(b_forget, (0, LANES - 8)).reshape(1, LANES)
    sink_row = jnp.pad(sinks, (0, LANES - 8)).reshape(1, LANES)
    w_main_t = w_in_t
    w_f_t = jnp.pad(w_in_t[MAIN_W:IN_W], ((0, LANES - 8), (0, 0)))

    xn = _xnorm(x2, g1r, 2 * tm)
    raw, fl, qa, kae, vae, q_aug, k_aug, v_aug = _norm_proj(xn, w_main_t, w_f_t, gqa, gka, gqb, gkb, bf_row, s, tm)
    ma, lse_a = _swa_fwd(qa, kae, vae, sink_row, nb, s, tq)
    if distributed:
        mb, ql_aug, w_out, w_up, w_down, w_up_t = _fox_fwd(q_aug, k_aug, v_aug, nb, s, btf, shards=rest)
    else:
        mb, ql_aug = _fox_fwd(q_aug, k_aug, v_aug, nb, s, btf)
        w_out, w_up, w_down, w_up_t = rest
    w_out, w_down = w_out.reshape(D_MODEL, D_MODEL), w_down.reshape(D_FF, D_MODEL)
    h, hn, hid, dy, dyb, loss_acc = _mlp_fwd(x2, ma, mb, tgt2, w_out, g2r, w_up, w_down, tm)

    du, dh, dhb, dma, do_aug, dla, gg2 = _mlp_bwd(dy, hid, h, ma, mb, w_down, w_up_t.reshape(D_FF, D_MODEL), w_out, g2r, tm)
    g_down = _wgrad(hid, dyb, "wgrad_down", 512, 1024, wkb, BF16).reshape(N_DEV, 512, D_MODEL)
    g_up = _wgrad(hn, du, "wgrad_up", 1024, 512, wkb, BF16, col_blocks=True)

    def out_grad():
        return jnp.concatenate([_wgrad(ma, dhb, "wgrad_out_a", 512, 1024, wk, BF16),
                                _wgrad(mb, dhb, "wgrad_out_b", 512, 1024, wk, BF16)], axis=0).reshape(N_DEV, 128, D_MODEL)

    dqa, dkae, dvae, dsink = _swa_bwd(qa, kae, vae, dma, sink_row, lse_a, dla, nb, s, tq)
    fox = _fox_bwd(ql_aug, k_aug, v_aug, do_aug, nb, s, bt, exch=(g_up, g_down) if distributed else ())
    dqb, dkb, dvb = fox[:3]
    if distributed:
        g_up, g_down = fox[3:]
        g_out = out_grad
    else:
        g_out = out_grad()
    grad_x, dproj, dfb, ggqa, ggka, ggqb, ggkb, gg1, gbf = _proj_bwd(
        raw, dqa, dkae, dvae, dqb, dkb, dvb, fl, bf_row, x2, dh, w_main_t, w_f_t, g1r, gqa, gka, gqb, gkb, nb, s, tm)
    g_main_t, g_gate_t = _wgrad(dproj, xn, "wgrad_in", 768, 1024, wk, BF16, a2=dfb)
    g_in_t = jnp.concatenate([g_main_t, g_gate_t[0:8].astype(BF16)], axis=0)

    small = (gg1.reshape(D_MODEL), gbf[0, 0:8], ggqa.reshape(8, 64).sum(0), ggka.reshape(2, 64).sum(0),
             dsink.sum(0)[:, 0:2, 0].reshape(8), ggqb.reshape(8, 64).sum(0), ggkb.reshape(8, 64).sum(0),
             gg2.reshape(D_MODEL))
    return loss_acc[0, 0], grad_x.reshape(nb, s, D_MODEL), g_in_t, g_out, g_up, g_down, small


def _all_gather(shard):
    x_ref = jax.new_ref(shard, memory_space=pltpu.MemorySpace.HBM)
    out_ref = jax.empty_ref(jax.ShapeDtypeStruct((N_DEV,) + shard.shape, shard.dtype), memory_space=pltpu.MemorySpace.HBM)

    @pl.kernel(mesh=plsc.ScalarSubcoreMesh(axis_name="sequencer", num_cores=1), name="gather_w_in",
               scratch_types=(pltpu.SemaphoreType.DMA((N_SEM,)), pltpu.SemaphoreType.DMA((N_SEM,)), pltpu.SemaphoreType.DMA((1,))),
               compiler_params=pltpu.CompilerParams(collective_id=1))
    def launch(send_sems, recv_sems, local_sems):
        x, y, c = lax.axis_index("x"), lax.axis_index("y"), lax.axis_index("c")
        barrier = pltpu.get_barrier_semaphore()
        peers = [(x, y, 1 - c), (1 - x, y, c), (x, 1 - y, c), (1 - x, 1 - y, c)]
        for peer in peers:
            pl.semaphore_signal(barrier, inc=1, device_id=peer, device_id_type=MESH)
        pl.semaphore_wait(barrier, len(peers))
        start, forward, finish = _gather_steps([(x_ref, out_ref)], send_sems, recv_sems, local_sems)
        start()
        forward()
        finish()

    launch()
    return out_ref[...]


def _exchange(*arrays, name="exchange_tail", collective_id=0):
    n_ex = len(arrays)
    srcs = [jax.new_ref(a, memory_space=pltpu.MemorySpace.HBM) for a in arrays]
    dsts = [jax.empty_ref(jax.ShapeDtypeStruct(a.shape, a.dtype), memory_space=pltpu.MemorySpace.HBM) for a in arrays]

    @pl.kernel(mesh=plsc.ScalarSubcoreMesh(axis_name="sequencer", num_cores=1), name=name,
               scratch_types=(pltpu.SemaphoreType.DMA((N_SEM * n_ex,)), pltpu.SemaphoreType.DMA((N_SEM * n_ex,)),
                              pltpu.SemaphoreType.DMA((n_ex,))),
               compiler_params=pltpu.CompilerParams(collective_id=collective_id))
    def launch(send_sems, recv_sems, local_sems):
        x, y, c = lax.axis_index("x"), lax.axis_index("y"), lax.axis_index("c")
        barrier = pltpu.get_barrier_semaphore()
        for k in range(1, N_DEV):
            peer = (1 - x if k & 4 else x, 1 - y if k & 2 else y, 1 - c if k & 1 else c)
            pl.semaphore_signal(barrier, inc=1, device_id=peer, device_id_type=MESH)
        pl.semaphore_wait(barrier, N_DEV - 1)
        start, finish = _exchange_steps(list(zip(srcs, dsts)), send_sems, recv_sems, local_sems)
        start()
        finish()

    launch()
    return [d[...] for d in dsts]


def _sum_adamw(recv, w, m, v, tr, name):
    _, r, n = recv.shape

    def body(r_ref, w_ref, m_ref, v_ref, g_ref, d_ref, nm_ref, nv_ref):
        g = r_ref[0].astype(F32)
        for s in range(1, N_DEV):
            g = g + r_ref[s].astype(F32)
        g_ref[...] = g
        nm = ADAM_B1 * m_ref[...] + (1.0 - ADAM_B1) * g
        nv = ADAM_B2 * v_ref[...] + (1.0 - ADAM_B2) * (g * g)
        m_hat = nm / (1.0 - ADAM_B1 ** ADAM_STEP)
        v_hat = nv / (1.0 - ADAM_B2 ** ADAM_STEP)
        d_ref[...] = -ADAM_LR * (m_hat / (jnp.sqrt(v_hat) + ADAM_EPS) + ADAM_WD * w_ref[...])
        nm_ref[...] = nm
        nv_ref[...] = nv

    tile = pl.BlockSpec((tr, n), lambda i: (i, 0))
    shp = jax.ShapeDtypeStruct((r, n), F32)
    return pl.pallas_call(
        body, name=name, grid=(r // tr,),
        in_specs=[pl.BlockSpec((N_DEV, tr, n), lambda i: (0, i, 0)), tile, tile, tile],
        out_specs=[tile, tile, tile, tile], out_shape=[shp, shp, shp, shp],
        compiler_params=_params(("arbitrary",)),
    )(recv, w, m, v)


def _small_rows(g1, bf, qna, kna, sk, qnb, knb, g2, extra=None):
    row2 = jnp.concatenate([bf, qna, kna, sk, qnb, knb])
    rows = [g1, g2, jnp.pad(row2, (0, D_MODEL - row2.shape[0]))]
    if extra is not None:
        rows.append(jnp.pad(extra.reshape(1), (0, D_MODEL - 1)))
    return jnp.pad(jnp.stack(rows), ((0, 8 - len(rows)), (0, 0)))


def _in_rows(w_in_s):
    return jnp.pad(w_in_s.T, ((0, IN_PAD - IN_SHARD), (0, 0)))


def kernel(x, attn_norm_g, w_in, b_forget, q_norm_a, k_norm_a, sink_logits, q_norm_b, k_norm_b, w_out, mlp_norm_g, w_up, w_down, loss_target, m_attn_norm_g, m_w_in, m_b_forget, m_q_norm_a, m_k_norm_a, m_sink_logits, m_q_norm_b, m_k_norm_b, m_w_out, m_mlp_norm_g, m_w_up, m_w_down, v_attn_norm_g, v_w_in, v_b_forget, v_q_norm_a, v_k_norm_a, v_sink_logits, v_q_norm_b, v_k_norm_b, v_w_out, v_mlp_norm_g, v_w_up, v_w_down):
    w_in_r = _in_rows(w_in)
    w_in_t = _all_gather(w_in_r.astype(BF16))[:, 0:IN_SHARD].reshape(IN_W, D_MODEL)
    w_up_b = w_up.astype(BF16)
    rest = (w_out.astype(BF16), w_up_b, w_down.astype(BF16), w_up_b.T)

    loss_part, grad_x, g_in_t, out_grad, r_up, r_down, small = _local_step(
        x, loss_target, w_in_t, rest, attn_norm_g, b_forget, q_norm_a, k_norm_a, sink_logits, q_norm_b, k_norm_b, mlp_norm_g,
        distributed=True)

    g_in_blocks = jnp.pad(g_in_t.reshape(N_DEV, IN_SHARD, D_MODEL), ((0, 0), (0, IN_PAD - IN_SHARD), (0, 0))).astype(BF16)
    small_blocks = jnp.broadcast_to(_small_rows(*small, extra=loss_part), (N_DEV, 8, D_MODEL))
    r_in, r_small = _exchange(g_in_blocks, small_blocks)
    r_out, = _exchange(out_grad(), name="exchange_out", collective_id=2)

    small_w = _small_rows(attn_norm_g, b_forget, q_norm_a, k_norm_a, sink_logits, q_norm_b, k_norm_b, mlp_norm_g)
    small_m = _small_rows(m_attn_norm_g, m_b_forget, m_q_norm_a, m_k_norm_a, m_sink_logits, m_q_norm_b, m_k_norm_b, m_mlp_norm_g)
    small_v = _small_rows(v_attn_norm_g, v_b_forget, v_q_norm_a, v_k_norm_a, v_sink_logits, v_q_norm_b, v_k_norm_b, v_mlp_norm_g)
    o_up = _sum_adamw(r_up, w_up, m_w_up, v_w_up, 256, "adamw_up")
    o_down = _sum_adamw(r_down, w_down, m_w_down, v_w_down, 128, "adamw_down")
    o_in = [a[0:IN_SHARD].T for a in _sum_adamw(r_in, w_in_r, _in_rows(m_w_in), _in_rows(v_w_in), IN_PAD, "adamw_in")]
    o_out = _sum_adamw(r_out, w_out, m_w_out, v_w_out, 128, "adamw_out")
    o_small = _sum_adamw(r_small, small_w, small_m, small_v, 8, "adamw_small")

    def leaves(i):
        row2 = o_small[i][2]
        return (o_small[i][0], o_in[i], row2[0:8], row2[8:72], row2[72:136], row2[136:144], row2[144:208], row2[208:272],
                o_out[i], o_small[i][1], o_up[i], o_down[i])

    return (o_small[0][3, 0], grad_x, *leaves(0), *leaves(1), *leaves(2), *leaves(3))
```
